```python
import math
import jax
import jax.numpy as jnp
from jax import lax
import numpy as np

D_MODEL = 1024
BATCH = 8
SEQ = 8192
DEPTH = 2

CTX_LEN = 256
GRID_W = 64
EPS = 1e-6

DN_HEADS = 4
DN_HEAD_DIM = 128
DN_WIDTH = DN_HEADS * DN_HEAD_DIM
DN_CHUNK = 64
POOL_WINDOWS = (2, 4, 8, 16)
POOL_GROUPS = len(POOL_WINDOWS)
POOL_WIDTH = D_MODEL // 4
POOL_GROUP_DIM = POOL_WIDTH // POOL_GROUPS
SC_WIDTH = D_MODEL // 4
N_BRANCH = 3
D_FF = ((8 * D_MODEL + 3 * 256 - 1) // (3 * 256)) * 256

OFF_Z = 3 * DN_WIDTH
OFF_A = OFF_Z + DN_WIDTH
OFF_BETA = OFF_A + 2 * DN_HEADS
OFF_POOL = OFF_BETA + 2 * DN_HEADS
OFF_SC = OFF_POOL + POOL_WIDTH
OFF_GATE = OFF_SC + 3 * SC_WIDTH
N_IN = OFF_GATE + N_BRANCH * D_MODEL
IN_SPLITS = (OFF_Z, OFF_A, OFF_BETA, OFF_POOL, OFF_SC, OFF_GATE)

kernel_name = "hybrid_parallel_deltanet_pool_shortconv_dit"


def rmsnorm(x, g):
    xf = x.astype(jnp.float32)
    y = xf * lax.rsqrt(jnp.mean(xf * xf, axis=-1, keepdims=True) + EPS)
    return (y * g.astype(jnp.float32)).astype(x.dtype)


def l2norm(x):
    return x * lax.rsqrt(jnp.sum(x * x, axis=-1, keepdims=True) + EPS)


def conv3(x, w):
    xp = jnp.pad(x, ((0, 0), (1, 1), (0, 0)))
    return xp[:, :-2] * w[0] + xp[:, 1:-1] * w[1] + xp[:, 2:] * w[2]


def decay_gate(p_a, p_b, a_log, dt_bias):
    bn, l, _ = p_a.shape
    a = p_a.astype(jnp.float32).reshape(bn, l, 2, DN_HEADS)
    g = -jnp.exp(a_log.astype(jnp.float32)) * jax.nn.softplus(a + dt_bias.astype(jnp.float32))
    beta = jax.nn.sigmoid(p_b.astype(jnp.float32).reshape(bn, l, 2, DN_HEADS))
    return g, beta


def gated_delta_chunked(q, k, v, g, beta, s0):
    bn, l, h, _ = k.shape
    dv = v.shape[-1]
    n = l // DN_CHUNK

    def chunks(t):
        t = t.reshape(bn, n, DN_CHUNK, h, *t.shape[3:])
        return jnp.moveaxis(t, (1, 3), (0, 2))

    kc, vc, bc = chunks(k), chunks(v), chunks(beta)
    gc = jnp.cumsum(chunks(g), axis=-1)
    idx = jnp.arange(DN_CHUNK)
    incl = idx[:, None] >= idx[None, :]
    strict = idx[:, None] > idx[None, :]
    decay = jnp.exp(jnp.where(incl, gc[..., :, None] - gc[..., None, :], -jnp.inf))
    kb = kc * bc[..., None]
    a = jnp.einsum('nbhid,nbhjd->nbhij', kb, kc) * jnp.where(strict, decay, 0.0)
    eye = jnp.eye(DN_CHUNK, dtype=jnp.float32)
    rhs = jnp.concatenate([vc * bc[..., None], kb * jnp.exp(gc)[..., None]], axis=-1)
    sol = lax.linalg.triangular_solve(eye + a, rhs, left_side=True, lower=True, unit_diagonal=True)
    u, w = sol[..., :dv], sol[..., dv:]
    g_last = gc[..., -1]
    k_state = kc * jnp.exp(g_last[..., None] - gc)[..., None]
    with_output = q is not None
    xs = (u, w, k_state, g_last)
    if with_output:
        qc = chunks(q)
        q_dec = qc * jnp.exp(gc)[..., None]
        a_qk = jnp.einsum('nbhid,nbhjd->nbhij', qc, kc) * decay
        xs = xs + (q_dec, a_qk)

    def step(s, xs_i):
        u_i, w_i, k_i, gl_i = xs_i[:4]
        v_new = u_i - jnp.einsum('bhck,bhkv->bhcv', w_i, s)
        s_next = s * jnp.exp(gl_i)[..., None, None] + jnp.einsum('bhck,bhcv->bhkv', k_i, v_new)
        if with_output:
            qd_i, aqk_i = xs_i[4:]
            o_i = jnp.einsum('bhck,bhkv->bhcv', qd_i, s) + jnp.einsum('bhij,bhjv->bhiv', aqk_i, v_new)
            return s_next, o_i
        return s_next, None

    s_fin, o = lax.scan(step, s0, xs)
    if with_output:
        o = jnp.moveaxis(o, (0, 2), (1, 3)).reshape(bn, l, h, dv)
    return o, s_fin


def dn_bidir(q, k, v, g, beta, s0_f, s0_b):
    rev = lambda t: jnp.flip(t, axis=1)
    o_f, s_f = gated_delta_chunked(q, k, v, g[:, :, 0], beta[:, :, 0], s0_f)
    o_b, s_b = gated_delta_chunked(None if q is None else rev(q), rev(k), rev(v),
                                   rev(g[:, :, 1]), rev(beta[:, :, 1]), s0_b)
    o = None if q is None else o_f + rev(o_b)
    return o, s_f, s_b


def box_mean(x, w, axis):
    l = x.shape[axis]
    lo = w // 2
    hi = w - 1 - lo
    cs = jnp.cumsum(x, axis=axis)
    cs = jnp.concatenate([jnp.zeros_like(lax.slice_in_dim(cs, 0, 1, axis=axis)), cs], axis=axis)
    pos = jnp.arange(l)
    start = jnp.clip(pos - lo, 0, l)
    end = jnp.clip(pos + hi + 1, 0, l)
    total = jnp.take(cs, end, axis=axis) - jnp.take(cs, start, axis=axis)
    shape = [1] * x.ndim
    shape[axis] = l
    return total / (end - start).astype(x.dtype).reshape(shape)


def pool_mixer(u, pool_w, pool_scale, rows):
    bn, l, _ = u.shape
    uf = u.astype(jnp.float32)
    outs = []
    for gi, w in enumerate(POOL_WINDOWS):
        ug = uf[..., gi * POOL_GROUP_DIM:(gi + 1) * POOL_GROUP_DIM]
        if rows is None:
            m = box_mean(ug, w, 1)
        else:
            ug2 = ug.reshape(bn, rows, GRID_W, POOL_GROUP_DIM)
            m = box_mean(box_mean(ug2, w, 1), w, 2).reshape(bn, l, POOL_GROUP_DIM)
        outs.append(m - ug)
    d = jnp.stack(outs, axis=2).astype(u.dtype)
    y = jnp.einsum('blgc,gcd->blgd', d, pool_w).reshape(bn, l, POOL_WIDTH)
    return y * pool_scale


def shortconv_mixer(p, conv_w):
    xin, gate_b, gate_c = jnp.split(p, 3, axis=-1)
    return gate_b * conv3(gate_c * xin, conv_w)


def mixer(h, lp, s0_f, s0_b, rows):
    bn, l, _ = h.shape
    p = h @ lp['w_in']
    p_qkv, p_z, p_a, p_b, p_pool, p_sc, p_gate = jnp.split(p, IN_SPLITS, axis=-1)
    qkv = jax.nn.silu(conv3(p_qkv, lp['dn_conv_w'])).astype(jnp.float32)
    qkv = qkv.reshape(bn, l, 3, DN_HEADS, DN_HEAD_DIM)
    q = l2norm(qkv[:, :, 0]) * (DN_HEAD_DIM ** -0.5)
    k = l2norm(qkv[:, :, 1])
    v = qkv[:, :, 2]
    g, beta = decay_gate(p_a, p_b, lp['dn_a_log'], lp['dn_dt_bias'])
    o, s_f, s_b = dn_bidir(q, k, v, g, beta, s0_f, s0_b)
    z = p_z.astype(jnp.float32).reshape(bn, l, DN_HEADS, DN_HEAD_DIM)
    o = rmsnorm(o, lp['dn_norm_g']) * jax.nn.silu(z)
    y_a = o.reshape(bn, l, DN_WIDTH).astype(h.dtype) @ lp['w_br_a']
    y_b = pool_mixer(p_pool, lp['pool_w'], lp['pool_scale'], rows) @ lp['w_br_b']
    y_c = shortconv_mixer(p_sc, lp['sc_conv_w']) @ lp['w_br_c']
    gates = jax.nn.sigmoid(p_gate.reshape(bn, l, N_BRANCH, D_MODEL))
    y = gates[:, :, 0] * y_a + gates[:, :, 1] * y_b + gates[:, :, 2] * y_c
    return y @ lp['w_o'], s_f, s_b


def context_states(h, lp, s0):
    bn, l, _ = h.shape
    w = lp['w_in']
    kv = jax.nn.silu(conv3(h @ w[:, DN_WIDTH:OFF_Z], lp['dn_conv_w'][:, DN_WIDTH:])).astype(jnp.float32)
    kv = kv.reshape(bn, l, 2, DN_HEADS, DN_HEAD_DIM)
    k = l2norm(kv[:, :, 0])
    v = kv[:, :, 1]
    p_a, p_b = jnp.split(h @ w[:, OFF_A:OFF_POOL], 2, axis=-1)
    g, beta = decay_gate(p_a, p_b, lp['dn_a_log'], lp['dn_dt_bias'])
    _, s_f, s_b = dn_bidir(None, k, v, g, beta, s0, s0)
    return s_f, s_b


def swiglu(h, w_gu, w_down):
    gate, up = jnp.split(h @ w_gu, 2, axis=-1)
    return (jax.nn.silu(gate) * up) @ w_down


def modulate(x, norm_g, shift, scale):
    return rmsnorm(x, norm_g) * (1 + scale) + shift


def _fwd_setup_inputs(seed: int = 0) -> dict:
    key = jax.random.key(seed)
    ks = jax.random.split(key, 24)
    f32 = jnp.float32
    nrm = lambda k, shape, s: jax.random.normal(k, shape, f32) * s
    L = DEPTH
    dt = jnp.exp(jax.random.uniform(ks[9], (L, 2, DN_HEADS), f32, math.log(1e-3), math.log(1e-1)))
    return {
        'x': nrm(ks[0], (BATCH, SEQ, D_MODEL), 1.0),
        'c': nrm(ks[1], (BATCH, D_MODEL), 1.0),
        'ctx': nrm(ks[2], (BATCH, CTX_LEN, D_MODEL), 1.0),
        'c_ctx': nrm(ks[3], (D_MODEL,), 1.0),
        'w_ada': nrm(ks[4], (L, D_MODEL, 6 * D_MODEL), 0.5 * D_MODEL ** -0.5),
        'b_ada': nrm(ks[5], (L, 6 * D_MODEL), 0.01),
        'norm1_g': 1.0 + nrm(ks[6], (L, D_MODEL), 0.1),
        'norm2_g': 1.0 + nrm(ks[7], (L, D_MODEL), 0.1),
        'w_in': nrm(ks[8], (L, D_MODEL, N_IN), D_MODEL ** -0.5),
        'dn_conv_w': nrm(ks[10], (L, 3, 3 * DN_WIDTH), 3 ** -0.5),
        'dn_a_log': jnp.log(jax.random.uniform(ks[11], (L, 2, DN_HEADS), f32, 1.0, 16.0)),
        'dn_dt_bias': dt + jnp.log(-jnp.expm1(-dt)),
        'dn_norm_g': 1.0 + nrm(ks[12], (L, DN_HEAD_DIM), 0.1),
        'pool_w': nrm(ks[13], (L, POOL_GROUPS, POOL_GROUP_DIM, POOL_GROUP_DIM), POOL_GROUP_DIM ** -0.5),
        'pool_scale': 1.0 + nrm(ks[14], (L, POOL_WIDTH), 0.1),
        'sc_conv_w': nrm(ks[15], (L, 3, SC_WIDTH), 3 ** -0.5),
        'w_br_a': nrm(ks[16], (L, DN_WIDTH, D_MODEL), DN_WIDTH ** -0.5),
        'w_br_b': nrm(ks[17], (L, POOL_WIDTH, D_MODEL), POOL_WIDTH ** -0.5),
        'w_br_c': nrm(ks[18], (L, SC_WIDTH, D_MODEL), SC_WIDTH ** -0.5),
        'w_o': nrm(ks[19], (L, D_MODEL, D_MODEL), D_MODEL ** -0.5),
        'w_gu': nrm(ks[20], (L, D_MODEL, 2 * D_FF), D_MODEL ** -0.5),
        'w_down': nrm(ks[21], (L, D_FF, D_MODEL), D_FF ** -0.5),
        'final_norm_g': 1.0 + nrm(ks[22], (D_MODEL,), 0.1),
    }


def _fwd_reference(x, c, ctx, c_ctx, w_ada, b_ada, norm1_g, norm2_g, w_in, dn_conv_w, dn_a_log,
              dn_dt_bias, dn_norm_g, pool_w, pool_scale, sc_conv_w, w_br_a, w_br_b, w_br_c,
              w_o, w_gu, w_down, final_norm_g):
    bn = x.shape[0]
    rows = x.shape[1] // GRID_W
    s0 = jnp.zeros((bn, DN_HEADS, DN_HEAD_DIM, DN_HEAD_DIM), jnp.float32)
    for l in range(DEPTH):
        lp = {'w_in': w_in[l], 'dn_conv_w': dn_conv_w[l], 'dn_a_log': dn_a_log[l],
              'dn_dt_bias': dn_dt_bias[l], 'dn_norm_g': dn_norm_g[l], 'pool_w': pool_w[l],
              'pool_scale': pool_scale[l], 'sc_conv_w': sc_conv_w[l], 'w_br_a': w_br_a[l],
              'w_br_b': w_br_b[l], 'w_br_c': w_br_c[l], 'w_o': w_o[l]}
        mod = jax.nn.silu(c) @ w_ada[l] + b_ada[l]
        sh1, sc1, g1, sh2, sc2, g2 = jnp.split(mod[:, None, :], 6, axis=-1)
        mod_c = jax.nn.silu(c_ctx) @ w_ada[l] + b_ada[l]
        sh1c, sc1c, g1c, sh2c, sc2c, g2c = jnp.split(mod_c, 6)
        hc = modulate(ctx, norm1_g[l], sh1c, sc1c)
        if l == DEPTH - 1:
            s_f, s_b = context_states(hc, lp, s0)
        else:
            mix_c, s_f, s_b = mixer(hc, lp, s0, s0, None)
            ctx = ctx + g1c * mix_c
            ctx = ctx + g2c * swiglu(modulate(ctx, norm2_g[l], sh2c, sc2c), w_gu[l], w_down[l])
        h = modulate(x, norm1_g[l], sh1, sc1)
        mix, _, _ = mixer(h, lp, s_f, s_b, rows)
        x = x + g1 * mix
        x = x + g2 * swiglu(modulate(x, norm2_g[l], sh2, sc2), w_gu[l], w_down[l])
    return rmsnorm(x, final_norm_g)


import jax as _jax
import jax.numpy as _jnp

TWIN_FORMAT = 'train_step'
FWD_PARAMS = ['x', 'c', 'ctx', 'c_ctx', 'w_ada', 'b_ada', 'norm1_g', 'norm2_g', 'w_in', 'dn_conv_w', 'dn_a_log', 'dn_dt_bias', 'dn_norm_g', 'pool_w', 'pool_scale', 'sc_conv_w', 'w_br_a', 'w_br_b', 'w_br_c', 'w_o', 'w_gu', 'w_down', 'final_norm_g']
TWIN_WEIGHTS = ['c_ctx', 'w_ada', 'b_ada', 'norm1_g', 'norm2_g', 'w_in', 'dn_conv_w', 'dn_a_log', 'dn_dt_bias', 'dn_norm_g', 'pool_w', 'pool_scale', 'sc_conv_w', 'w_br_a', 'w_br_b', 'w_br_c', 'w_o', 'w_gu', 'w_down', 'final_norm_g']
TWIN_DIFF_INPUT = 'x'
TWIN_INPUTS = ['x', 'c', 'ctx', 'c_ctx', 'w_ada', 'b_ada', 'norm1_g', 'norm2_g', 'w_in', 'dn_conv_w', 'dn_a_log', 'dn_dt_bias', 'dn_norm_g', 'pool_w', 'pool_scale', 'sc_conv_w', 'w_br_a', 'w_br_b', 'w_br_c', 'w_o', 'w_gu', 'w_down', 'final_norm_g', 'loss_target', 'm_c_ctx', 'm_w_ada', 'm_b_ada', 'm_norm1_g', 'm_norm2_g', 'm_w_in', 'm_dn_conv_w', 'm_dn_a_log', 'm_dn_dt_bias', 'm_dn_norm_g', 'm_pool_w', 'm_pool_scale', 'm_sc_conv_w', 'm_w_br_a', 'm_w_br_b', 'm_w_br_c', 'm_w_o', 'm_w_gu', 'm_w_down', 'm_final_norm_g', 'v_c_ctx', 'v_w_ada', 'v_b_ada', 'v_norm1_g', 'v_norm2_g', 'v_w_in', 'v_dn_conv_w', 'v_dn_a_log', 'v_dn_dt_bias', 'v_dn_norm_g', 'v_pool_w', 'v_pool_scale', 'v_sc_conv_w', 'v_w_br_a', 'v_w_br_b', 'v_w_br_c', 'v_w_o', 'v_w_gu', 'v_w_down', 'v_final_norm_g']
TWIN_OUTPUTS = ['loss', 'grad_x', 'grad_c_ctx', 'grad_w_ada', 'grad_b_ada', 'grad_norm1_g', 'grad_norm2_g', 'grad_w_in', 'grad_dn_conv_w', 'grad_dn_a_log', 'grad_dn_dt_bias', 'grad_dn_norm_g', 'grad_pool_w', 'grad_pool_scale', 'grad_sc_conv_w', 'grad_w_br_a', 'grad_w_br_b', 'grad_w_br_c', 'grad_w_o', 'grad_w_gu', 'grad_w_down', 'grad_final_norm_g', 'delta_c_ctx', 'delta_w_ada', 'delta_b_ada', 'delta_norm1_g', 'delta_norm2_g', 'delta_w_in', 'delta_dn_conv_w', 'delta_dn_a_log', 'delta_dn_dt_bias', 'delta_dn_norm_g', 'delta_pool_w', 'delta_pool_scale', 'delta_sc_conv_w', 'delta_w_br_a', 'delta_w_br_b', 'delta_w_br_c', 'delta_w_o', 'delta_w_gu', 'delta_w_down', 'delta_final_norm_g', 'new_m_c_ctx', 'new_m_w_ada', 'new_m_b_ada', 'new_m_norm1_g', 'new_m_norm2_g', 'new_m_w_in', 'new_m_dn_conv_w', 'new_m_dn_a_log', 'new_m_dn_dt_bias', 'new_m_dn_norm_g', 'new_m_pool_w', 'new_m_pool_scale', 'new_m_sc_conv_w', 'new_m_w_br_a', 'new_m_w_br_b', 'new_m_w_br_c', 'new_m_w_o', 'new_m_w_gu', 'new_m_w_down', 'new_m_final_norm_g', 'new_v_c_ctx', 'new_v_w_ada', 'new_v_b_ada', 'new_v_norm1_g', 'new_v_norm2_g', 'new_v_w_in', 'new_v_dn_conv_w', 'new_v_dn_a_log', 'new_v_dn_dt_bias', 'new_v_dn_norm_g', 'new_v_pool_w', 'new_v_pool_scale', 'new_v_sc_conv_w', 'new_v_w_br_a', 'new_v_w_br_b', 'new_v_w_br_c', 'new_v_w_o', 'new_v_w_gu', 'new_v_w_down', 'new_v_final_norm_g']
TWIN_LEAF_KINDS = {'loss': 'loss', 'grad_x': 'grad_x', 'grad_c_ctx': 'grad_w', 'grad_w_ada': 'grad_w', 'grad_b_ada': 'grad_w', 'grad_norm1_g': 'grad_w', 'grad_norm2_g': 'grad_w', 'grad_w_in': 'grad_w', 'grad_dn_conv_w': 'grad_w', 'grad_dn_a_log': 'grad_w', 'grad_dn_dt_bias': 'grad_w', 'grad_dn_norm_g': 'grad_w', 'grad_pool_w': 'grad_w', 'grad_pool_scale': 'grad_w', 'grad_sc_conv_w': 'grad_w', 'grad_w_br_a': 'grad_w', 'grad_w_br_b': 'grad_w', 'grad_w_br_c': 'grad_w', 'grad_w_o': 'grad_w', 'grad_w_gu': 'grad_w', 'grad_w_down': 'grad_w', 'grad_final_norm_g': 'grad_w', 'delta_c_ctx': 'delta_w', 'delta_w_ada': 'delta_w', 'delta_b_ada': 'delta_w', 'delta_norm1_g': 'delta_w', 'delta_norm2_g': 'delta_w', 'delta_w_in': 'delta_w', 'delta_dn_conv_w': 'delta_w', 'delta_dn_a_log': 'delta_w', 'delta_dn_dt_bias': 'delta_w', 'delta_dn_norm_g': 'delta_w', 'delta_pool_w': 'delta_w', 'delta_pool_scale': 'delta_w', 'delta_sc_conv_w': 'delta_w', 'delta_w_br_a': 'delta_w', 'delta_w_br_b': 'delta_w', 'delta_w_br_c': 'delta_w', 'delta_w_o': 'delta_w', 'delta_w_gu': 'delta_w', 'delta_w_down': 'delta_w', 'delta_final_norm_g': 'delta_w', 'new_m_c_ctx': 'new_m', 'new_m_w_ada': 'new_m', 'new_m_b_ada': 'new_m', 'new_m_norm1_g': 'new_m', 'new_m_norm2_g': 'new_m', 'new_m_w_in': 'new_m', 'new_m_dn_conv_w': 'new_m', 'new_m_dn_a_log': 'new_m', 'new_m_dn_dt_bias': 'new_m', 'new_m_dn_norm_g': 'new_m', 'new_m_pool_w': 'new_m', 'new_m_pool_scale': 'new_m', 'new_m_sc_conv_w': 'new_m', 'new_m_w_br_a': 'new_m', 'new_m_w_br_b': 'new_m', 'new_m_w_br_c': 'new_m', 'new_m_w_o': 'new_m', 'new_m_w_gu': 'new_m', 'new_m_w_down': 'new_m', 'new_m_final_norm_g': 'new_m', 'new_v_c_ctx': 'new_v', 'new_v_w_ada': 'new_v', 'new_v_b_ada': 'new_v', 'new_v_norm1_g': 'new_v', 'new_v_norm2_g': 'new_v', 'new_v_w_in': 'new_v', 'new_v_dn_conv_w': 'new_v', 'new_v_dn_a_log': 'new_v', 'new_v_dn_dt_bias': 'new_v', 'new_v_dn_norm_g': 'new_v', 'new_v_pool_w': 'new_v', 'new_v_pool_scale': 'new_v', 'new_v_sc_conv_w': 'new_v', 'new_v_w_br_a': 'new_v', 'new_v_w_br_b': 'new_v', 'new_v_w_br_c': 'new_v', 'new_v_w_o': 'new_v', 'new_v_w_gu': 'new_v', 'new_v_w_down': 'new_v', 'new_v_final_norm_g': 'new_v'}


def _forward(args):
    return _fwd_reference(*[args[k] for k in FWD_PARAMS])


def _output_shape():
    def fwd():
        inp = _fwd_setup_inputs(0)
        return _fwd_reference(*[inp[k] for k in FWD_PARAMS])
    out = _jax.eval_shape(fwd)
    return out.shape, out.dtype

N_MICROBATCH = 1
ADAM_LR = 0.001
ADAM_B1 = 0.9
ADAM_B2 = 0.999
ADAM_EPS = 1e-08
ADAM_WD = 0.01
ADAM_STEP = 10
PER_EXAMPLE_BATCH_AXIS = {'x': 0, 'c': 0, 'ctx': 0, 'loss_target': 0}
SHARED_INPUTS = []
_WEIGHT_DTYPES = {'c_ctx': _jnp.float32, 'w_ada': _jnp.float32, 'b_ada': _jnp.float32, 'norm1_g': _jnp.float32, 'norm2_g': _jnp.float32, 'w_in': _jnp.float32, 'dn_conv_w': _jnp.float32, 'dn_a_log': _jnp.float32, 'dn_dt_bias': _jnp.float32, 'dn_norm_g': _jnp.float32, 'pool_w': _jnp.float32, 'pool_scale': _jnp.float32, 'sc_conv_w': _jnp.float32, 'w_br_a': _jnp.float32, 'w_br_b': _jnp.float32, 'w_br_c': _jnp.float32, 'w_o': _jnp.float32, 'w_gu': _jnp.float32, 'w_down': _jnp.float32, 'final_norm_g': _jnp.float32}
MOMENT_SCALE = {'c_ctx': 3.235931e-03, 'w_ada': 2.236240e-01, 'b_ada': 4.996951e-01, 'norm1_g': 1.167797e-01, 'norm2_g': 7.442907e-02, 'w_in': 4.877687e-02, 'dn_conv_w': 2.889279e-02, 'dn_a_log': 7.503221e-02, 'dn_dt_bias': 7.264301e-02, 'dn_norm_g': 8.425006e-02, 'pool_w': 9.038501e-02, 'pool_scale': 9.550178e-02, 'sc_conv_w': 1.120951e-01, 'w_br_a': 2.897825e-02, 'w_br_b': 4.753504e-02, 'w_br_c': 5.880721e-02, 'w_o': 8.236849e-02, 'w_gu': 3.303961e-02, 'w_down': 5.422686e-02, 'final_norm_g': 6.445240e+01}


def _to_microbatches(a, axis):
    t = _jnp.moveaxis(a, axis, 0)
    t = t.reshape((N_MICROBATCH, t.shape[0] // N_MICROBATCH) + t.shape[1:])
    return _jnp.moveaxis(t, 1, axis + 1)


def setup_inputs(seed: int = 0) -> dict:
    inp = _fwd_setup_inputs(seed)
    key = _jax.random.fold_in(_jax.random.key(seed), 7919)
    shape, _ = _output_shape()
    out = dict(inp)
    out["loss_target"] = _jax.random.normal(_jax.random.fold_in(key, 0), shape, _jnp.float32)
    for i, name in enumerate(TWIN_WEIGHTS):
        w = inp[name].astype(_jnp.float32)
        if MOMENT_SCALE is None:
            s = _jnp.sqrt(_jnp.mean(_jnp.square(w)) + 1e-30)
        else:
            s = MOMENT_SCALE[name]
        km, kv = _jax.random.split(_jax.random.fold_in(key, i + 1))
        out[name] = w
        out["m_" + name] = s * _jax.random.normal(km, w.shape, _jnp.float32)
        out["v_" + name] = (s * s) * _jax.random.uniform(kv, w.shape, _jnp.float32, 0.5, 1.5)
    if N_MICROBATCH > 1:
        for name, axis in PER_EXAMPLE_BATCH_AXIS.items():
            out[name] = _to_microbatches(out[name], axis)
    return {'x': out['x'], 'c': out['c'], 'ctx': out['ctx'], 'c_ctx': out['c_ctx'], 'w_ada': out['w_ada'], 'b_ada': out['b_ada'], 'norm1_g': out['norm1_g'], 'norm2_g': out['norm2_g'], 'w_in': out['w_in'], 'dn_conv_w': out['dn_conv_w'], 'dn_a_log': out['dn_a_log'], 'dn_dt_bias': out['dn_dt_bias'], 'dn_norm_g': out['dn_norm_g'], 'pool_w': out['pool_w'], 'pool_scale': out['pool_scale'], 'sc_conv_w': out['sc_conv_w'], 'w_br_a': out['w_br_a'], 'w_br_b': out['w_br_b'], 'w_br_c': out['w_br_c'], 'w_o': out['w_o'], 'w_gu': out['w_gu'], 'w_down': out['w_down'], 'final_norm_g': out['final_norm_g'], 'loss_target': out['loss_target'], 'm_c_ctx': out['m_c_ctx'], 'm_w_ada': out['m_w_ada'], 'm_b_ada': out['m_b_ada'], 'm_norm1_g': out['m_norm1_g'], 'm_norm2_g': out['m_norm2_g'], 'm_w_in': out['m_w_in'], 'm_dn_conv_w': out['m_dn_conv_w'], 'm_dn_a_log': out['m_dn_a_log'], 'm_dn_dt_bias': out['m_dn_dt_bias'], 'm_dn_norm_g': out['m_dn_norm_g'], 'm_pool_w': out['m_pool_w'], 'm_pool_scale': out['m_pool_scale'], 'm_sc_conv_w': out['m_sc_conv_w'], 'm_w_br_a': out['m_w_br_a'], 'm_w_br_b': out['m_w_br_b'], 'm_w_br_c': out['m_w_br_c'], 'm_w_o': out['m_w_o'], 'm_w_gu': out['m_w_gu'], 'm_w_down': out['m_w_down'], 'm_final_norm_g': out['m_final_norm_g'], 'v_c_ctx': out['v_c_ctx'], 'v_w_ada': out['v_w_ada'], 'v_b_ada': out['v_b_ada'], 'v_norm1_g': out['v_norm1_g'], 'v_norm2_g': out['v_norm2_g'], 'v_w_in': out['v_w_in'], 'v_dn_conv_w': out['v_dn_conv_w'], 'v_dn_a_log': out['v_dn_a_log'], 'v_dn_dt_bias': out['v_dn_dt_bias'], 'v_dn_norm_g': out['v_dn_norm_g'], 'v_pool_w': out['v_pool_w'], 'v_pool_scale': out['v_pool_scale'], 'v_sc_conv_w': out['v_sc_conv_w'], 'v_w_br_a': out['v_w_br_a'], 'v_w_br_b': out['v_w_br_b'], 'v_w_br_c': out['v_w_br_c'], 'v_w_o': out['v_w_o'], 'v_w_gu': out['v_w_gu'], 'v_w_down': out['v_w_down'], 'v_final_norm_g': out['v_final_norm_g']}


def _loss(weights, diff, rest, loss_target):
    with _jax.named_scope("forward"):
        args = {**rest, TWIN_DIFF_INPUT: diff, **{k: w.astype(_WEIGHT_DTYPES[k]) for k, w in weights.items()}}
        y = _forward(args)
    with _jax.named_scope("loss_head"):
        err = _jnp.square(y.astype(_jnp.float32) - loss_target)
        return 0.5 * _jnp.sum(_jnp.mean(err, axis=-1)) if err.ndim else 0.5 * err


def _adamw(w, g, m, v):
    m = ADAM_B1 * m + (1.0 - ADAM_B1) * g
    v = ADAM_B2 * v + (1.0 - ADAM_B2) * _jnp.square(g)
    m_hat = m / (1.0 - ADAM_B1 ** ADAM_STEP)
    v_hat = v / (1.0 - ADAM_B2 ** ADAM_STEP)
    delta = -ADAM_LR * (m_hat / (_jnp.sqrt(v_hat) + ADAM_EPS) + ADAM_WD * w)
    return delta, m, v


def reference(x, c, ctx, c_ctx, w_ada, b_ada, norm1_g, norm2_g, w_in, dn_conv_w, dn_a_log, dn_dt_bias, dn_norm_g, pool_w, pool_scale, sc_conv_w, w_br_a, w_br_b, w_br_c, w_o, w_gu, w_down, final_norm_g, loss_target, m_c_ctx, m_w_ada, m_b_ada, m_norm1_g, m_norm2_g, m_w_in, m_dn_conv_w, m_dn_a_log, m_dn_dt_bias, m_dn_norm_g, m_pool_w, m_pool_scale, m_sc_conv_w, m_w_br_a, m_w_br_b, m_w_br_c, m_w_o, m_w_gu, m_w_down, m_final_norm_g, v_c_ctx, v_w_ada, v_b_ada, v_norm1_g, v_norm2_g, v_w_in, v_dn_conv_w, v_dn_a_log, v_dn_dt_bias, v_dn_norm_g, v_pool_w, v_pool_scale, v_sc_conv_w, v_w_br_a, v_w_br_b, v_w_br_c, v_w_o, v_w_gu, v_w_down, v_final_norm_g):
    given = dict(x=x, c=c, ctx=ctx, c_ctx=c_ctx, w_ada=w_ada, b_ada=b_ada, norm1_g=norm1_g, norm2_g=norm2_g, w_in=w_in, dn_conv_w=dn_conv_w, dn_a_log=dn_a_log, dn_dt_bias=dn_dt_bias, dn_norm_g=dn_norm_g, pool_w=pool_w, pool_scale=pool_scale, sc_conv_w=sc_conv_w, w_br_a=w_br_a, w_br_b=w_br_b, w_br_c=w_br_c, w_o=w_o, w_gu=w_gu, w_down=w_down, final_norm_g=final_norm_g, loss_target=loss_target, m_c_ctx=m_c_ctx, m_w_ada=m_w_ada, m_b_ada=m_b_ada, m_norm1_g=m_norm1_g, m_norm2_g=m_norm2_g, m_w_in=m_w_in, m_dn_conv_w=m_dn_conv_w, m_dn_a_log=m_dn_a_log, m_dn_dt_bias=m_dn_dt_bias, m_dn_norm_g=m_dn_norm_g, m_pool_w=m_pool_w, m_pool_scale=m_pool_scale, m_sc_conv_w=m_sc_conv_w, m_w_br_a=m_w_br_a, m_w_br_b=m_w_br_b, m_w_br_c=m_w_br_c, m_w_o=m_w_o, m_w_gu=m_w_gu, m_w_down=m_w_down, m_final_norm_g=m_final_norm_g, v_c_ctx=v_c_ctx, v_w_ada=v_w_ada, v_b_ada=v_b_ada, v_norm1_g=v_norm1_g, v_norm2_g=v_norm2_g, v_w_in=v_w_in, v_dn_conv_w=v_dn_conv_w, v_dn_a_log=v_dn_a_log, v_dn_dt_bias=v_dn_dt_bias, v_dn_norm_g=v_dn_norm_g, v_pool_w=v_pool_w, v_pool_scale=v_pool_scale, v_sc_conv_w=v_sc_conv_w, v_w_br_a=v_w_br_a, v_w_br_b=v_w_br_b, v_w_br_c=v_w_br_c, v_w_o=v_w_o, v_w_gu=v_w_gu, v_w_down=v_w_down, v_final_norm_g=v_final_norm_g)
    weights = {n: given[n] for n in TWIN_WEIGHTS}
    shared = {n: given[n] for n in SHARED_INPUTS}
    per_example = {n: given[n] for n in ['x', 'c', 'ctx']}
    grad_fn = _jax.value_and_grad(_loss, argnums=(0, 1))

    def one_microbatch(ex, loss_target):
        ex = dict(ex)
        diff = ex.pop(TWIN_DIFF_INPUT)
        return grad_fn(weights, diff, {**shared, **ex}, loss_target)

    if N_MICROBATCH == 1:
        loss, (grad_w, grad_x) = one_microbatch(per_example, given["loss_target"])
    else:
        def body(carry, xs):
            loss_sum, grad_sum = carry
            l_k, (gw_k, gx_k) = one_microbatch(xs[0], xs[1])
            with _jax.named_scope("update"):
                return (loss_sum + l_k, _jax.tree.map(_jnp.add, grad_sum, gw_k)), gx_k

        init = (_jnp.zeros((), _jnp.float32), _jax.tree.map(_jnp.zeros_like, weights))
        (loss, grad_w), grad_x = _jax.lax.scan(body, init, (per_example, given["loss_target"]))
    with _jax.named_scope("update"):
        delta_w, new_m, new_v = {}, {}, {}
        for n in TWIN_WEIGHTS:
            delta_w[n], new_m[n], new_v[n] = _adamw(weights[n], grad_w[n], given["m_" + n], given["v_" + n])
    return (loss, grad_x, *[grad_w[n] for n in TWIN_WEIGHTS], *[delta_w[n] for n in TWIN_WEIGHTS],
            *[new_m[n] for n in TWIN_WEIGHTS], *[new_v[n] for n in TWIN_WEIGHTS])
```

```python
import functools

import numpy as np
import jax
import jax.numpy as jnp
from jax import lax
from jax.experimental import pallas as pl
from jax.experimental.pallas import tpu as pltpu

F32 = jnp.float32
BF16 = jnp.bfloat16
HI = lax.Precision.HIGHEST

D = 1024
NL = 2
NH = 4
DH = 128
DN = NH * DH
CH = 64
GW = 64
PW = 256
DFF = 2816
EPS = 1e-6
N_DEV = 8
LANES = 128
BF16_ROWS = 16
VMEM_MB = 56

ADAM_LR, ADAM_B1, ADAM_B2, ADAM_EPS, ADAM_WD, ADAM_STEP = 0.001, 0.9, 0.999, 1e-08, 0.01, 10

IN_BOUNDS = (0, 1536, 2048, 2064, 2320, 2576, 2832, 3088, 6160)
IN_WIDTHS = (1536, 512, 128, 256, 256, 256, 256, 3072)
POOL_WIN = ((1, 0), (2, 1), (4, 3), (8, 7))

NN = ((1,), (0,))
NT = ((1,), (1,))
TN = ((0,), (0,))


def _dot(a, b, dims, hi=False):
    if hi:
        return lax.dot_general(a, b, (dims, ((), ())), precision=HI, preferred_element_type=F32)
    return lax.dot_general(a.astype(BF16), b.astype(BF16), (dims, ((), ())), preferred_element_type=F32)


def _S(shape, dtype=F32):
    return jax.ShapeDtypeStruct(tuple(shape), dtype)


def _full(shape):
    nd = len(shape)
    return pl.BlockSpec(tuple(shape), lambda *_: (0,) * nd)


def _rows(tt, w):
    return pl.BlockSpec((tt, w), lambda i: (i, 0))


def _call(body, name, grid, in_specs, out_specs, out_shape, scratch=()):
    return pl.pallas_call(
        body, name=name, grid=grid, in_specs=in_specs, out_specs=out_specs, out_shape=out_shape,
        scratch_shapes=list(scratch),
        compiler_params=pltpu.CompilerParams(
            dimension_semantics=("arbitrary",) * len(grid), vmem_limit_bytes=VMEM_MB << 20),
    )


def _iota(shape, axis):
    return lax.broadcasted_iota(jnp.int32, shape, axis)


def _colsum(a):
    return jnp.sum(a, axis=0, keepdims=True)


def _silu(x):
    return x * jax.nn.sigmoid(x)


def _modulate(x, g, sh, sc):
    xn = x * lax.rsqrt(jnp.mean(x * x, axis=-1, keepdims=True) + EPS)
    return (xn * g) * (1.0 + sc) + sh


def _stream_rows(mv_ref, i, tt, tc, k):
    isc = (i * tt + _iota((tt, 1), 0)) < tc
    return isc, jnp.where(isc, mv_ref[k:k + 1, :], mv_ref[3 + k:4 + k, :])


def _acc_stream(ref, k, isc, val):
    ref[k:k + 1, :] += _colsum(jnp.where(isc, val, 0.0))
    ref[3 + k:4 + k, :] += _colsum(jnp.where(isc, 0.0, val))


MOD_CT = 1536


def _mod_fwd(cc8, w_ada, b_ada3):
    def body(cc_ref, w_ref, b_ref, o_ref):
        o_ref[0] = _dot(_silu(cc_ref[...]), w_ref[0], NN) + b_ref[0]

    return _call(
        body, "mod_fwd", (NL, 6 * D // MOD_CT),
        [pl.BlockSpec((8, D), lambda l, j: (0, 0)), pl.BlockSpec((1, D, MOD_CT), lambda l, j: (l, 0, j)),
         pl.BlockSpec((1, 1, MOD_CT), lambda l, j: (l, 0, j))],
        pl.BlockSpec((1, 8, MOD_CT), lambda l, j: (l, 0, j)), _S((NL, 8, 6 * D)))(cc8, w_ada, b_ada3)


def _mod_bwd(cc8, w_ada, dmods):
    def body(cc_ref, w_ref, dm_ref, dw_ref, db_ref, dcc_ref):
        first = (pl.program_id(0) == 0) & (pl.program_id(1) == 0)
        cc = cc_ref[...]
        sg = jax.nn.sigmoid(cc)
        dm = dm_ref[0]
        dw_ref[0] = _dot(cc * sg, dm, TN)
        db_ref[0] = dm[0:1, :] + dm[1:2, :]

        @pl.when(first)
        def _():
            dcc_ref[...] = jnp.zeros_like(dcc_ref)

        dcc_ref[...] += _dot(dm, w_ref[0], NT) * (sg * (1.0 + cc * (1.0 - sg)))

    return _call(
        body, "mod_bwd", (NL, 6 * D // MOD_CT),
        [pl.BlockSpec((8, D), lambda l, j: (0, 0)), pl.BlockSpec((1, D, MOD_CT), lambda l, j: (l, 0, j)),
         pl.BlockSpec((1, 8, MOD_CT), lambda l, j: (l, 0, j))],
        [pl.BlockSpec((1, D, MOD_CT), lambda l, j: (l, 0, j)), pl.BlockSpec((1, 1, MOD_CT), lambda l, j: (l, 0, j)),
         pl.BlockSpec((8, D), lambda l, j: (0, 0))],
        [_S((NL, D, 6 * D)), _S((NL, 1, 6 * D)), _S((8, D))])(cc8, w_ada, dmods)


def _inproj_fwd(X, mv, g, ws, tc, tt):
    T = X.shape[0]
    nw = len(ws)

    def body(x_ref, mv_ref, g_ref, *refs):
        w_refs, h_ref, p_refs = refs[:nw], refs[nw], refs[nw + 1:]
        i = pl.program_id(0)
        _, sh = _stream_rows(mv_ref, i, tt, tc, 0)
        _, sc = _stream_rows(mv_ref, i, tt, tc, 1)
        hb = _modulate(x_ref[...], g_ref[...], sh, sc).astype(BF16)
        h_ref[...] = hb
        for w_ref, p_ref in zip(w_refs, p_refs):
            p_ref[...] = jnp.dot(hb, w_ref[...], preferred_element_type=F32)

    return _call(
        body, "inproj_fwd", (T // tt,),
        [_rows(tt, D), _full((8, D)), _full((1, D))] + [_full(w.shape) for w in ws],
        [_rows(tt, D)] + [_rows(tt, w.shape[1]) for w in ws],
        [_S((T, D), BF16)] + [_S((T, w.shape[1])) for w in ws])(X, mv, g, *ws)


def _inproj_bwd(X, mv, g, ws, dps, dp_w, dres, tc, tt):
    T = X.shape[0]
    nw, nd = len(ws), len(dps)

    def body(x_ref, mv_ref, g_ref, dres_ref, *refs):
        w_refs, dp_refs = refs[:nw], refs[nw:nw + nd]
        dx_ref, dg_ref, dm_ref = refs[nw + nd:]
        i = pl.program_id(0)
        isc, sh = _stream_rows(mv_ref, i, tt, tc, 0)
        _, sc = _stream_rows(mv_ref, i, tt, tc, 1)
        dh = None
        for dp_ref, k in zip(dp_refs, dp_w):
            t = _dot(dp_ref[...], w_refs[k][...], NT)
            dh = t if dh is None else dh + t
        _, vjp = jax.vjp(_modulate, x_ref[...], g_ref[...], sh, sc)
        dx, dg, dsh, dsc = vjp(dh)
        dx_ref[...] = dres_ref[...] + dx

        @pl.when(i == 0)
        def _():
            dg_ref[...] = jnp.zeros_like(dg_ref)
            dm_ref[...] = jnp.zeros_like(dm_ref)

        dg_ref[...] += dg
        _acc_stream(dm_ref, 0, isc, dsh)
        _acc_stream(dm_ref, 1, isc, dsc)

    return _call(
        body, "inproj_bwd", (T // tt,),
        [_rows(tt, D), _full((8, D)), _full((1, D)), _rows(tt, D)] + [_full(w.shape) for w in ws]
        + [_rows(tt, dp.shape[1]) for dp in dps],
        [_rows(tt, D), _full((1, D)), _full((8, D))],
        [_S((T, D)), _S((1, D)), _S((8, D))])(X, mv, g, dres, *ws, *dps)


def _dw(A, B, tt):
    T, K = A.shape
    N = B.shape[1]
    tn = 512 if N % 512 == 0 else (256 if N % 256 == 0 else LANES)

    def body(a_ref, b_ref, o_ref):
        @pl.when(pl.program_id(1) == 0)
        def _():
            o_ref[...] = jnp.zeros_like(o_ref)

        o_ref[...] += _dot(a_ref[...], b_ref[...], TN)

    return _call(
        body, "dw", (N // tn, T // tt),
        [pl.BlockSpec((tt, K), lambda j, i: (i, 0)), pl.BlockSpec((tt, tn), lambda j, i: (i, j))],
        pl.BlockSpec((K, tn), lambda j, i: (0, j)), _S((K, N)))(A, B)


def _halo_specs(T, tt, cw, col):
    r8, nb8 = tt // 8, T // 8
    return [pl.BlockSpec((tt, cw), lambda j, i: (i, col(j))),
            pl.BlockSpec((8, cw), lambda j, i: (jnp.maximum(i * r8 - 1, 0), col(j))),
            pl.BlockSpec((8, cw), lambda j, i: (jnp.minimum((i + 1) * r8, nb8 - 1), col(j)))]


def _shifts(a, prev8, next8, i, tt, tc, T):
    r = _iota((tt, 1), 0)
    t = i * tt + r
    dn = jnp.where(r == 0, prev8[7:8, :], pltpu.roll(a, 1, 0))
    dn = jnp.where((t == 0) | (t == tc), 0.0, dn)
    up = jnp.where(r == tt - 1, next8[0:1, :], pltpu.roll(a, tt - 1, 0))
    up = jnp.where((t == T - 1) | (t == tc - 1), 0.0, up)
    return dn, up


def _dn_post(y, j):
    a = _silu(y)
    nrm = lax.rsqrt(jnp.sum(a * a, axis=-1, keepdims=True) + EPS)
    f = jnp.where(j < NH, nrm * (DH ** -0.5), jnp.where(j < 2 * NH, nrm, 1.0))
    return a * f


def _conv3(w_ref, dn, mid, up):
    return w_ref[0:1, :] * dn + w_ref[1:2, :] * mid + w_ref[2:3, :] * up


def _dnprep_fwd(pq, cw, tc, tt):
    T = pq.shape[0]

    def body(p_ref, pp_ref, pn_ref, w_ref, a_ref):
        j, i = pl.program_id(0), pl.program_id(1)
        p = p_ref[...]
        dn, up = _shifts(p, pp_ref[...], pn_ref[...], i, tt, tc, T)
        a_ref[...] = _dn_post(_conv3(w_ref, dn, p, up), j)

    return _call(
        body, "dnprep_fwd", (3 * NH, T // tt),
        _halo_specs(T, tt, DH, lambda j: j) + [pl.BlockSpec((3, DH), lambda j, i: (0, j))],
        pl.BlockSpec((tt, DH), lambda j, i: (i, j)), _S((T, 3 * DN)))(pq, pq, pq, cw)


def _dnprep_bwd_act(pq, cw, da_f, da_b, tc, tt):
    T = pq.shape[0]

    def body(p_ref, pp_ref, pn_ref, w_ref, df_ref, db_ref, dy_ref):
        j, i = pl.program_id(0), pl.program_id(1)
        p = p_ref[...]
        dn, up = _shifts(p, pp_ref[...], pn_ref[...], i, tt, tc, T)
        _, vjp = jax.vjp(lambda y: _dn_post(y, j), _conv3(w_ref, dn, p, up))
        dy_ref[...] = vjp(df_ref[...] + db_ref[...])[0]

    blk = pl.BlockSpec((tt, DH), lambda j, i: (i, j))
    return _call(
        body, "dnprep_bwd_act", (3 * NH, T // tt),
        _halo_specs(T, tt, DH, lambda j: j) + [pl.BlockSpec((3, DH), lambda j, i: (0, j)), blk, blk],
        blk, _S((T, 3 * DN)))(pq, pq, pq, cw, da_f, da_b)


def _conv_bwd(dy, p, cw, tc, tt):
    T, W = p.shape

    def body(dy_ref, dyp_ref, dyn_ref, p_ref, pp_ref, pn_ref, w_ref, dp_ref, dw_ref):
        i = pl.program_id(1)
        dy, p_ = dy_ref[...], p_ref[...]
        ddn, dup = _shifts(dy, dyp_ref[...], dyn_ref[...], i, tt, tc, T)
        dp_ref[...] = _conv3(w_ref, dup, dy, ddn)
        pdn, pup = _shifts(p_, pp_ref[...], pn_ref[...], i, tt, tc, T)

        @pl.when(i == 0)
        def _():
            dw_ref[...] = jnp.zeros_like(dw_ref)

        dw_ref[0:1, :] += _colsum(dy * pdn)
        dw_ref[1:2, :] += _colsum(dy * p_)
        dw_ref[2:3, :] += _colsum(dy * pup)

    wspec = pl.BlockSpec((3, LANES), lambda j, i: (0, j))
    return _call(
        body, "conv_bwd", (W // LANES, T // tt),
        _halo_specs(T, tt, LANES, lambda j: j) * 2 + [wspec],
        [pl.BlockSpec((tt, LANES), lambda j, i: (i, j)), wspec], [_S((T, W)), _S((3, W))])(dy, dy, dy, p, p, p, cw)


def _sc_fwd(sx, sb, sc_, cw, tc, tt):
    T = sx.shape[0]

    def body(x_ref, xp_ref, xn_ref, c_ref, cp_ref, cn_ref, b_ref, w_ref, y_ref):
        i = pl.program_id(1)
        u = c_ref[...] * x_ref[...]
        dn, up = _shifts(u, cp_ref[...] * xp_ref[...], cn_ref[...] * xn_ref[...], i, tt, tc, T)
        y_ref[...] = b_ref[...] * _conv3(w_ref, dn, u, up)

    blk = pl.BlockSpec((tt, LANES), lambda j, i: (i, j))
    return _call(
        body, "sc_fwd", (PW // LANES, T // tt),
        _halo_specs(T, tt, LANES, lambda j: j) * 2 + [blk, pl.BlockSpec((3, LANES), lambda j, i: (0, j))],
        blk, _S((T, PW)))(sx, sx, sx, sc_, sc_, sc_, sb, cw)


def _sc_bwd(sx, sb, sc_, cw, dy, tc, tt):
    T = sx.shape[0]

    def body(x_ref, xp_ref, xn_ref, c_ref, cp_ref, cn_ref, b_ref, bp_ref, bn_ref, dy_ref, dyp_ref, dyn_ref, w_ref,
             dx_ref, db_ref, dc_ref, dw_ref):
        i = pl.program_id(1)
        x, c, dy_ = x_ref[...], c_ref[...], dy_ref[...]
        u = c * x
        udn, uup = _shifts(u, cp_ref[...] * xp_ref[...], cn_ref[...] * xn_ref[...], i, tt, tc, T)
        db_ref[...] = dy_ * _conv3(w_ref, udn, u, uup)
        e = dy_ * b_ref[...]
        edn, eup = _shifts(e, dyp_ref[...] * bp_ref[...], dyn_ref[...] * bn_ref[...], i, tt, tc, T)
        du = _conv3(w_ref, eup, e, edn)
        dx_ref[...] = du * c
        dc_ref[...] = du * x

        @pl.when(i == 0)
        def _():
            dw_ref[...] = jnp.zeros_like(dw_ref)

        dw_ref[0:1, :] += _colsum(e * udn)
        dw_ref[1:2, :] += _colsum(e * u)
        dw_ref[2:3, :] += _colsum(e * uup)

    blk = pl.BlockSpec((tt, LANES), lambda j, i: (i, j))
    wspec = pl.BlockSpec((3, LANES), lambda j, i: (0, j))
    return _call(
        body, "sc_bwd", (PW // LANES, T // tt),
        _halo_specs(T, tt, LANES, lambda j: j) * 4 + [wspec],
        [blk, blk, blk, wspec], [_S((T, PW))] * 3 + [_S((3, PW))])(
            sx, sx, sx, sc_, sc_, sc_, sb, sb, sb, dy, dy, dy, cw)


def _group_select(vals):
    g = _iota((1, PW), 1) // (PW // len(POOL_WIN))
    return jnp.where(g == 0, vals[0], jnp.where(g == 1, vals[1], jnp.where(g == 2, vals[2], vals[3])))


def _nested_box(get, mirror):
    acc, outs, pl_, ph_ = get(0), [], 0, 0
    for lo, hi in POOL_WIN:
        if mirror:
            lo, hi = hi, lo
        for k in range(pl_ + 1, lo + 1):
            acc = acc + get(-k)
        for k in range(ph_ + 1, hi + 1):
            acc = acc + get(k)
        pl_, ph_ = lo, hi
        outs.append(acc)
    return _group_select(outs)


def _box_tokens(a, n, mirror):
    idx = _iota((n, 1), 0)

    def get(k):
        if k == 0:
            return a
        return jnp.where((idx + k >= 0) & (idx + k < n), pltpu.roll(a, (-k) % n, 0), 0.0)

    return _nested_box(get, mirror)


def _inv_count(pos, n):
    return _group_select([1.0 / (jnp.minimum(pos + hi, n - 1) - jnp.maximum(pos - lo, 0) + 1).astype(F32)
                          for lo, hi in POOL_WIN])


def _pool_rows(ref, r, R, tc, mirror):
    def get(k):
        rr = r + k
        rc = jnp.clip(rr, 0, R - 1)
        v = ref[pl.ds(pl.multiple_of(tc + rc * GW, GW), GW), :]
        if mirror:
            v = v * _inv_count(jnp.full((1, PW), rc, jnp.int32), R)
        return jnp.where((rr >= 0) & (rr < R), v, 0.0)

    return _nested_box(get, mirror)


def _pool_fwd(u, pwbd, ps, tc):
    T = u.shape[0]
    R = (T - tc) // GW

    def body(u_ref, pw_ref, ps_ref, y_ref):
        pw, scale = pw_ref[...], ps_ref[...]
        uc = u_ref[0:tc, :]
        mc = _box_tokens(uc, tc, False) * _inv_count(_iota((tc, 1), 0), tc)
        y_ref[0:tc, :] = _dot(mc - uc, pw, NN) * scale
        inv_c = _inv_count(_iota((GW, 1), 0), GW)

        def row(r, carry):
            rs = _pool_rows(u_ref, r, R, tc, False) * _inv_count(jnp.full((1, PW), r, jnp.int32), R)
            m = _box_tokens(rs, GW, False) * inv_c
            sl = pl.ds(pl.multiple_of(tc + r * GW, GW), GW)
            y_ref[sl, :] = _dot(m - u_ref[sl, :], pw, NN) * scale
            return carry

        lax.fori_loop(0, R, row, 0)

    return pl.pallas_call(
        body, name="pool_fwd", out_shape=_S((T, PW)),
        compiler_params=pltpu.CompilerParams(vmem_limit_bytes=VMEM_MB << 20))(u, pwbd, ps)


def _pool_bwd(u, pwbd, ps, dy, tc):
    T = u.shape[0]
    R = (T - tc) // GW

    def body(u_ref, pw_ref, ps_ref, dy_ref, du_ref, dpw_ref, dps_ref, dd_ref):
        pw, scale = pw_ref[...], ps_ref[...]
        dpw_ref[...] = jnp.zeros_like(dpw_ref)
        dps_ref[...] = jnp.zeros_like(dps_ref)

        def back(d, dy_):
            dz = dy_ * scale
            dpw_ref[...] += _dot(d, dz, TN)
            dps_ref[...] += _colsum(dy_ * _dot(d, pw, NN))
            return _dot(dz, pw, NT)

        uc = u_ref[0:tc, :]
        inv_cc = _inv_count(_iota((tc, 1), 0), tc)
        ddc = back(_box_tokens(uc, tc, False) * inv_cc - uc, dy_ref[0:tc, :])
        du_ref[0:tc, :] = _box_tokens(ddc * inv_cc, tc, True) - ddc
        inv_c = _inv_count(_iota((GW, 1), 0), GW)

        def row1(r, carry):
            rs = _pool_rows(u_ref, r, R, tc, False) * _inv_count(jnp.full((1, PW), r, jnp.int32), R)
            m = _box_tokens(rs, GW, False) * inv_c
            sl = pl.ds(pl.multiple_of(tc + r * GW, GW), GW)
            dd_ref[sl, :] = back(m - u_ref[sl, :], dy_ref[sl, :])
            return carry

        lax.fori_loop(0, R, row1, 0)

        def row2(r, carry):
            t1 = _pool_rows(dd_ref, r, R, tc, True)
            sl = pl.ds(pl.multiple_of(tc + r * GW, GW), GW)
            du_ref[sl, :] = _box_tokens(t1 * inv_c, GW, True) - dd_ref[sl, :]
            return carry

        lax.fori_loop(0, R, row2, 0)

    return pl.pallas_call(
        body, name="pool_bwd", out_shape=[_S((T, PW)), _S((PW, PW)), _S((1, PW))],
        scratch_shapes=[pltpu.VMEM((T, PW), F32)],
        compiler_params=pltpu.CompilerParams(vmem_limit_bytes=VMEM_MB << 20))(u, pwbd, ps, dy)


def _scan_consts():
    i = np.arange(CH)
    lower = (i[:, None] >= i[None, :]).astype(np.float32)
    return jnp.asarray(np.stack([lower, lower.T])), jnp.asarray(np.stack([lower.T, lower]))


def _gates(pab, al, dtb, csum):
    sp_in = pab + dtb
    sp = jnp.maximum(sp_in, 0.0) + jnp.log(1.0 + jnp.exp(-jnp.abs(sp_in)))
    nexp = -jnp.exp(al)
    gm = nexp * sp
    return gm, jax.nn.sigmoid(pab), _dot(csum, gm, NN, hi=True), sp_in, nexp


def _lane_col(m, j):
    return jnp.sum(jnp.where(_iota(m.shape, 1) == j, m, 0.0), axis=1, keepdims=True)


def _lane_rows(m):
    eye = (_iota((16, LANES), 0) == _iota((16, LANES), 1)).astype(F32)
    return _dot(eye, m, NT, hi=True)


def _dn_prep(q, k, v, gc, gr, beta, d):
    ii, jj = _iota((CH, CH), 0), _iota((CH, CH), 1)
    incl = (ii >= jj) if d == 0 else (ii <= jj)
    strict = (ii > jj) if d == 0 else (ii < jj)
    last = CH - 1 if d == 0 else 0
    Di = jnp.where(incl, jnp.exp(jnp.where(incl, gc - gr, 0.0)), 0.0)
    Ds = jnp.where(strict, Di, 0.0)
    kb = k * beta
    kk = _dot(kb, k, NT)
    nm = -(kk * Ds)
    tm = (ii == jj).astype(F32) + nm
    mp = nm
    for _ in range(5):
        mp = _dot(mp, mp, NN, hi=True)
        tm = tm + _dot(tm, mp, NN, hi=True)
    E = jnp.exp(gc)
    gl = gc[last:last + 1, :]
    xs = jnp.exp(gl - gc)
    qk = _dot(q, k, NT)
    return dict(Di=Di, Ds=Ds, strict=strict, last=last, kb=kb, kk=kk, tm=tm, E=E, gl=gl, xs=xs, qk=qk,
                u=_dot(tm, v * beta, NN, hi=True), w=_dot(tm, kb * E, NN, hi=True), ks=k * xs, qd=q * E, aqk=qk * Di)


def _dn_step_fwd(q, k, v, gc, gr, beta, S, d):
    c = _dn_prep(q, k, v, gc, gr, beta, d)
    vn = c["u"] - _dot(c["w"], S, NN)
    o = _dot(c["qd"], S, NN) + _dot(c["aqk"], vn, NN)
    return o, S * jnp.exp(c["gl"]) + _dot(c["ks"], vn, TN)


def _dn_step_bwd(q, k, v, gc, gr, beta, S, do, dS2, d):
    c = _dn_prep(q, k, v, gc, gr, beta, d)
    tm, E, xs, kb, u, w = c["tm"], c["E"], c["xs"], c["kb"], c["u"], c["w"]
    egl = jnp.exp(c["gl"])
    vn = u - _dot(w, S, NN)
    dvn = _dot(c["aqk"], do, TN) + _dot(c["ks"], dS2, NN)
    dS = _dot(c["qd"], do, TN) + egl * dS2 - _dot(w, dvn, TN)
    dqd = _dot(do, S, NT)
    daqk = _dot(do, vn, NT)
    dks = _dot(vn, dS2, NT)
    dgl = jnp.sum(jnp.sum(S * dS2, axis=1, keepdims=True), axis=0, keepdims=True) * egl
    dw = -_dot(dvn, S, NT)
    drb = _dot(tm, dvn, TN, hi=True)
    drw = _dot(tm, dw, TN, hi=True)
    dA = jnp.where(c["strict"], -(_dot(drb, u, NT, hi=True) + _dot(drw, w, NT, hi=True)), 0.0)
    dM1 = dA * c["Ds"]
    dM2 = daqk * c["Di"]
    dkb = _dot(dM1, k, NN) + drw * E
    dk = _dot(dM1, kb, TN) + _dot(dM2, q, TN) + dks * xs
    dq = _dot(dM2, k, NN) + dqd * E
    G = dM1 * c["kk"] + dM2 * c["qk"]
    col = _dot(G, jnp.ones((CH, LANES), F32), TN, hi=True)[:, 0:1]
    dE = jnp.sum(dqd * q, axis=1, keepdims=True) + jnp.sum(drw * kb, axis=1, keepdims=True)
    dx = jnp.sum(dks * k, axis=1, keepdims=True)
    dgl = dgl + jnp.sum(dx * xs, axis=0, keepdims=True)
    dgc = jnp.sum(G, axis=1, keepdims=True) - col + dE * E - dx * xs
    dgc = dgc + jnp.where(_iota((CH, 1), 0) == c["last"], dgl, 0.0)
    dbeta = jnp.sum(drb * v, axis=1, keepdims=True) + jnp.sum(dkb * k, axis=1, keepdims=True)
    return dq, dk + dkb * beta, drb * beta, dgc, dbeta, dS


def _hs(h):
    return slice(h * DH, (h + 1) * DH)


def _scan_fwd(qkv, pab, alr, dtr, tc):
    T = qkv.shape[0]
    n, ncx = T // CH, tc // CH
    csum, _ = _scan_consts()

    def cb(i):
        return jnp.where(i < ncx, ncx - 1 - i, n - 1 - (i - ncx))

    def body(qf_ref, pf_ref, qb_ref, pb_ref, cs_ref, al_ref, dt_ref, of_ref, ob_ref, sf_ref, sb_ref, S_f, S_b):
        @pl.when(pl.program_id(0) == 0)
        def _():
            S_f[...] = jnp.zeros_like(S_f)
            S_b[...] = jnp.zeros_like(S_b)

        for d, (q_ref, p_ref, o_ref, ss_ref, S) in enumerate(
                ((qf_ref, pf_ref, of_ref, sf_ref, S_f), (qb_ref, pb_ref, ob_ref, sb_ref, S_b))):
            ss_ref[0] = S[...]
            _, bm, gcm, _, _ = _gates(p_ref[...], al_ref[...], dt_ref[...], cs_ref[d])
            gct = _lane_rows(gcm)
            for h in range(NH):
                j = d * NH + h
                o, s2 = _dn_step_fwd(q_ref[:, _hs(h)], q_ref[:, _hs(NH + h)], q_ref[:, _hs(2 * NH + h)],
                                     _lane_col(gcm, j), gct[j:j + 1, :], _lane_col(bm, 8 + j), S[_hs(h), :], d)
                o_ref[:, _hs(h)] = o
                S[_hs(h), :] = s2

    qs = lambda f: pl.BlockSpec((CH, 3 * DN), lambda i: (f(i), 0))
    ps = lambda f: pl.BlockSpec((CH, LANES), lambda i: (f(i), 0))
    os_ = lambda f: pl.BlockSpec((CH, DN), lambda i: (f(i), 0))
    ss = pl.BlockSpec((1, DN, DH), lambda i: (i, 0, 0))
    ident = lambda i: i
    return _call(
        body, "scan_fwd", (n,),
        [qs(ident), ps(ident), qs(cb), ps(cb), _full((2, CH, CH)), _full((1, LANES)), _full((1, LANES))],
        [os_(ident), os_(cb), ss, ss],
        [_S((T, DN)), _S((T, DN)), _S((n, DN, DH)), _S((n, DN, DH))],
        scratch=[pltpu.VMEM((DN, DH), F32), pltpu.VMEM((DN, DH), F32)])(qkv, pab, qkv, pab, csum, alr, dtr)


def _scan_bwd(qkv, pab, alr, dtr, do, ssf, ssb, tc):
    T = qkv.shape[0]
    n, ncx = T // CH, tc // CH
    csum, csum_t = _scan_consts()

    def cf(s):
        return n - 1 - s

    def cb(s):
        i = n - 1 - s
        return jnp.where(i < ncx, ncx - 1 - i, n - 1 - (i - ncx))

    def body(qf_ref, pf_ref, dof_ref, sf_ref, qb_ref, pb_ref, dob_ref, sb_ref, cs_ref, cst_ref, al_ref, dt_ref,
             dqf_ref, dqb_ref, dpf_ref, dpb_ref, acc_ref, dS_f, dS_b):
        @pl.when(pl.program_id(0) == 0)
        def _():
            dS_f[...] = jnp.zeros_like(dS_f)
            dS_b[...] = jnp.zeros_like(dS_b)
            acc_ref[...] = jnp.zeros_like(acc_ref)

        lane = _iota((CH, LANES), 1)
        for d, (q_ref, p_ref, do_ref, ss_ref, dq_ref, dp_ref, dS) in enumerate(
                ((qf_ref, pf_ref, dof_ref, sf_ref, dqf_ref, dpf_ref, dS_f),
                 (qb_ref, pb_ref, dob_ref, sb_ref, dqb_ref, dpb_ref, dS_b))):
            gm, bm, gcm, sp_in, nexp = _gates(p_ref[...], al_ref[...], dt_ref[...], cs_ref[d])
            gct = _lane_rows(gcm)
            dgcm = jnp.zeros((CH, LANES), F32)
            dbm = jnp.zeros((CH, LANES), F32)
            for h in range(NH):
                j = d * NH + h
                dq, dk, dv, dgc, dbeta, ds = _dn_step_bwd(
                    q_ref[:, _hs(h)], q_ref[:, _hs(NH + h)], q_ref[:, _hs(2 * NH + h)],
                    _lane_col(gcm, j), gct[j:j + 1, :], _lane_col(bm, 8 + j), ss_ref[0, _hs(h), :],
                    do_ref[:, _hs(h)], dS[_hs(h), :], d)
                dq_ref[:, _hs(h)] = dq
                dq_ref[:, _hs(NH + h)] = dk
                dq_ref[:, _hs(2 * NH + h)] = dv
                dS[_hs(h), :] = ds
                dgcm = jnp.where(lane == j, dgc, dgcm)
                dbm = jnp.where(lane == 8 + j, dbeta, dbm)
            dgm = _dot(cst_ref[d], dgcm, NN, hi=True)
            dsp = dgm * nexp * jax.nn.sigmoid(sp_in)
            dp_ref[...] = dsp + dbm * bm * (1.0 - bm)
            acc_ref[0:1, :] += _colsum(dgm * gm)
            acc_ref[1:2, :] += _colsum(dsp)

    qs = lambda f: pl.BlockSpec((CH, 3 * DN), lambda s: (f(s), 0))
    ps = lambda f: pl.BlockSpec((CH, LANES), lambda s: (f(s), 0))
    os_ = lambda f: pl.BlockSpec((CH, DN), lambda s: (f(s), 0))
    ss = pl.BlockSpec((1, DN, DH), lambda s: (n - 1 - s, 0, 0))
    return _call(
        body, "scan_bwd", (n,),
        [qs(cf), ps(cf), os_(cf), ss, qs(cb), ps(cb), os_(cb), ss,
         _full((2, CH, CH)), _full((2, CH, CH)), _full((1, LANES)), _full((1, LANES))],
        [qs(cf), qs(cb), ps(cf), ps(cb), _full((8, LANES))],
        [_S((T, 3 * DN)), _S((T, 3 * DN)), _S((T, LANES)), _S((T, LANES)), _S((8, LANES))],
        scratch=[pltpu.VMEM((DN, DH), F32), pltpu.VMEM((DN, DH), F32)])(
            qkv, pab, do, ssf, qkv, pab, do, ssb, csum, csum_t, alr, dtr)


def _head_out(o, z, g):
    on = o * lax.rsqrt(jnp.mean(o * o, axis=-1, keepdims=True) + EPS) * g
    return on * _silu(z)


def _mix_branches(of_ref, ob_ref, z_ref, yp_ref, ys_ref, pg_ref, gdn_ref, wa_ref, wb_ref, wc_ref):
    ons, ya = [], None
    for h in range(NH):
        on = _head_out(of_ref[:, _hs(h)] + ob_ref[:, _hs(h)], z_ref[:, _hs(h)], gdn_ref[...])
        t = _dot(on, wa_ref[_hs(h), :], NN)
        ya = t if ya is None else ya + t
        ons.append(on)
    ys = [ya, _dot(yp_ref[...], wb_ref[...], NN), _dot(ys_ref[...], wc_ref[...], NN)]
    sg = [jax.nn.sigmoid(pg_ref[:, k * D:(k + 1) * D]) for k in range(3)]
    return ons, ys, sg


def _mix_fwd(X, of, ob, z, yp, ys, pg, mv, gdn, wa, wb, wc, wo, tc, tt):
    T = X.shape[0]

    def body(x_ref, of_ref, ob_ref, z_ref, yp_ref, ys_ref, pg_ref, mv_ref, gdn_ref, wa_ref, wb_ref, wc_ref, wo_ref,
             x1_ref):
        _, yb, sg = _mix_branches(of_ref, ob_ref, z_ref, yp_ref, ys_ref, pg_ref, gdn_ref, wa_ref, wb_ref, wc_ref)
        mix = _dot(sg[0] * yb[0] + sg[1] * yb[1] + sg[2] * yb[2], wo_ref[...], NN)
        _, gate = _stream_rows(mv_ref, pl.program_id(0), tt, tc, 2)
        x1_ref[...] = x_ref[...] + gate * mix

    return _call(
        body, "mix_fwd", (T // tt,),
        [_rows(tt, D), _rows(tt, DN), _rows(tt, DN), _rows(tt, DN), _rows(tt, PW), _rows(tt, PW), _rows(tt, 3 * D),
         _full((8, D)), _full((1, DH)), _full(wa.shape), _full(wb.shape), _full(wc.shape), _full(wo.shape)],
        _rows(tt, D), _S((T, D)))(X, of, ob, z, yp, ys, pg, mv, gdn, wa, wb, wc, wo)


def _mix_bwd(dx1, of, ob, z, yp, ys, pg, mv, gdn, wa, wb, wc, wo, tc, tt):
    T = dx1.shape[0]

    def body(dx_ref, of_ref, ob_ref, z_ref, yp_ref, ys_ref, pg_ref, mv_ref, gdn_ref, wa_ref, wb_ref, wc_ref, wo_ref,
             do_ref, dz_ref, dyp_ref, dys_ref, dpg_ref, dwa_ref, dwb_ref, dwc_ref, dwo_ref, dgdn_ref, dm_ref):
        i = pl.program_id(0)

        @pl.when(i == 0)
        def _():
            for r in (dwa_ref, dwb_ref, dwc_ref, dwo_ref, dgdn_ref, dm_ref):
                r[...] = jnp.zeros_like(r)

        ons, yb, sg = _mix_branches(of_ref, ob_ref, z_ref, yp_ref, ys_ref, pg_ref, gdn_ref, wa_ref, wb_ref, wc_ref)
        ymix = sg[0] * yb[0] + sg[1] * yb[1] + sg[2] * yb[2]
        isc, gate = _stream_rows(mv_ref, i, tt, tc, 2)
        dx = dx_ref[...]
        dmix = dx * gate
        _acc_stream(dm_ref, 2, isc, dx * _dot(ymix, wo_ref[...], NN))
        dwo_ref[...] += _dot(ymix, dmix, TN)
        dymix = _dot(dmix, wo_ref[...], NT)
        dyb = []
        for k in range(3):
            dyb.append(dymix * sg[k])
            dpg_ref[:, k * D:(k + 1) * D] = dymix * yb[k] * sg[k] * (1.0 - sg[k])
        dwb_ref[...] += _dot(yp_ref[...], dyb[1], TN)
        dwc_ref[...] += _dot(ys_ref[...], dyb[2], TN)
        dyp_ref[...] = _dot(dyb[1], wb_ref[...], NT)
        dys_ref[...] = _dot(dyb[2], wc_ref[...], NT)
        dg = jnp.zeros((1, DH), F32)
        for h in range(NH):
            dwa_ref[_hs(h), :] += _dot(ons[h], dyb[0], TN)
            don = _dot(dyb[0], wa_ref[_hs(h), :], NT)
            _, vjp = jax.vjp(_head_out, of_ref[:, _hs(h)] + ob_ref[:, _hs(h)], z_ref[:, _hs(h)], gdn_ref[...])
            do_h, dz_h, dg_h = vjp(don)
            do_ref[:, _hs(h)] = do_h
            dz_ref[:, _hs(h)] = dz_h
            dg = dg + dg_h
        dgdn_ref[...] += dg

    return _call(
        body, "mix_bwd", (T // tt,),
        [_rows(tt, D), _rows(tt, DN), _rows(tt, DN), _rows(tt, DN), _rows(tt, PW), _rows(tt, PW), _rows(tt, 3 * D),
         _full((8, D)), _full((1, DH)), _full(wa.shape), _full(wb.shape), _full(wc.shape), _full(wo.shape)],
        [_rows(tt, DN), _rows(tt, DN), _rows(tt, PW), _rows(tt, PW), _rows(tt, 3 * D),
         _full(wa.shape), _full(wb.shape), _full(wc.shape), _full(wo.shape), _full((1, DH)), _full((8, D))],
        [_S((T, DN)), _S((T, DN)), _S((T, PW)), _S((T, PW)), _S((T, 3 * D)),
         _S(wa.shape), _S(wb.shape), _S(wc.shape), _S(wo.shape), _S((1, DH)), _S((8, D))])(
            dx1, of, ob, z, yp, ys, pg, mv, gdn, wa, wb, wc, wo)


def _ffn_fwd(X1, mv, g, wgu, wd, tc, tt):
    T = X1.shape[0]

    def body(x_ref, mv_ref, g_ref, wgu_ref, wd_ref, x2_ref):
        i = pl.program_id(0)
        _, sh = _stream_rows(mv_ref, i, tt, tc, 0)
        _, sc = _stream_rows(mv_ref, i, tt, tc, 1)
        _, gate = _stream_rows(mv_ref, i, tt, tc, 2)
        x = x_ref[...]
        gu = _dot(_modulate(x, g_ref[...], sh, sc), wgu_ref[...], NN)
        x2_ref[...] = x + gate * _dot(_silu(gu[:, :DFF]) * gu[:, DFF:], wd_ref[...], NN)

    return _call(
        body, "ffn_fwd", (T // tt,),
        [_rows(tt, D), _full((8, D)), _full((1, D)), _full(wgu.shape), _full(wd.shape)],
        _rows(tt, D), _S((T, D)))(X1, mv, g, wgu, wd)


def _ffn_bwd(X1, dx2, mv, g, wgu, wd, tc, tt):
    T = X1.shape[0]

    def body(x_ref, dx2_ref, mv_ref, g_ref, wgu_ref, wd_ref, dx1_ref, h_ref, dgu_ref, act_ref, dff_ref, dg_ref, dm_ref):
        i = pl.program_id(0)
        isc, sh = _stream_rows(mv_ref, i, tt, tc, 0)
        _, sc = _stream_rows(mv_ref, i, tt, tc, 1)
        _, gate = _stream_rows(mv_ref, i, tt, tc, 2)
        x, dx2_ = x_ref[...], dx2_ref[...]
        h, vjp = jax.vjp(_modulate, x, g_ref[...], sh, sc)
        hb = h.astype(BF16)
        h_ref[...] = hb
        gu = jnp.dot(hb, wgu_ref[...], preferred_element_type=F32)
        ga, up = gu[:, :DFF], gu[:, DFF:]
        sg = jax.nn.sigmoid(ga)
        act = (ga * sg * up).astype(BF16)
        act_ref[...] = act
        dff = dx2_ * gate
        dff_ref[...] = dff.astype(BF16)
        dact = _dot(dff, wd_ref[...], NT)
        dga = (dact * up * (sg * (1.0 + ga * (1.0 - sg)))).astype(BF16)
        dup = (dact * ga * sg).astype(BF16)
        dgu_ref[:, :DFF] = dga
        dgu_ref[:, DFF:] = dup
        dh = _dot(dga, wgu_ref[:, :DFF], NT) + _dot(dup, wgu_ref[:, DFF:], NT)
        dx, dg, dsh, dsc = vjp(dh)
        dx1_ref[...] = dx2_ + dx

        @pl.when(i == 0)
        def _():
            dg_ref[...] = jnp.zeros_like(dg_ref)
            dm_ref[...] = jnp.zeros_like(dm_ref)

        dg_ref[...] += dg
        _acc_stream(dm_ref, 0, isc, dsh)
        _acc_stream(dm_ref, 1, isc, dsc)
        _acc_stream(dm_ref, 2, isc, dx2_ * jnp.dot(act, wd_ref[...], preferred_element_type=F32))

    return _call(
        body, "ffn_bwd", (T // tt,),
        [_rows(tt, D), _rows(tt, D), _full((8, D)), _full((1, D)), _full(wgu.shape), _full(wd.shape)],
        [_rows(tt, D), _rows(tt, D), _rows(tt, 2 * DFF), _rows(tt, DFF), _rows(tt, D), _full((1, D)), _full((8, D))],
        [_S((T, D)), _S((T, D), BF16), _S((T, 2 * DFF), BF16), _S((T, DFF), BF16), _S((T, D), BF16),
         _S((1, D)), _S((8, D))])(X1, dx2, mv, g, wgu, wd)


def _rms(x, g):
    return x * lax.rsqrt(jnp.mean(x * x, axis=-1, keepdims=True) + EPS) * g


def _loss_head(X2, tgt, gf, tc):
    T = X2.shape[0]

    def body(x_ref, t_ref, g_ref, dx_ref, loss_ref, dg_ref):
        i = pl.program_id(0)

        @pl.when(i == 0)
        def _():
            dx_ref[...] = jnp.zeros_like(dx_ref)
            loss_ref[...] = jnp.zeros_like(loss_ref)
            dg_ref[...] = jnp.zeros_like(dg_ref)

        @pl.when(i > 0)
        def _():
            y, vjp = jax.vjp(_rms, x_ref[...], g_ref[...])
            err = y - t_ref[...]
            dx, dg = vjp(err * (1.0 / D))
            dx_ref[...] = dx
            dg_ref[...] += dg
            loss_ref[...] += (0.5 / D) * jnp.sum(jnp.sum(err * err, axis=1, keepdims=True), axis=0, keepdims=True)

    return _call(
        body, "loss_head", (T // tc,),
        [_rows(tc, D), pl.BlockSpec((tc, D), lambda i: (jnp.maximum(i - 1, 0), 0)), _full((1, D))],
        [_rows(tc, D), _full((8, LANES)), _full((1, D))],
        [_S((T, D)), _S((8, LANES)), _S((1, D))])(X2, tgt, gf)


def _block_diag(pw):
    g, n = pw.shape[0], pw.shape[1]
    out = jnp.zeros((g * n, g * n), pw.dtype)
    for k in range(g):
        out = lax.dynamic_update_slice(out, pw[k], (k * n, k * n))
    return out


def _split_w_in(w):
    parts = [w[:, IN_BOUNDS[k]:IN_BOUNDS[k + 1]] for k in range(8)]
    parts[2] = jnp.pad(parts[2], ((0, 0), (0, LANES - 16)))
    return parts


def _mod_rows(mods_l, k0):
    rows = [mods_l[s, (k0 + k) * D:(k0 + k + 1) * D] for s in (0, 1) for k in range(3)]
    return jnp.stack(rows + [jnp.zeros((D,), F32)] * 2)


def _lane_row(v8):
    return jnp.pad(v8.reshape(1, 8), ((0, 0), (0, LANES - 8)))


def _device_step(x, c, ctx, tgt, wts, tt):
    tc = ctx.shape[0]
    X = jnp.concatenate([ctx, x], axis=0)
    cc8 = jnp.concatenate([wts["c_ctx"][None, :], c, jnp.zeros((6, D), F32)], axis=0)
    w_ada = wts["w_ada"].astype(BF16)
    mods = _mod_fwd(cc8, w_ada, wts["b_ada"].reshape(NL, 1, 6 * D))

    saved = []
    for l in range(NL):
        ws = [w.astype(BF16) for w in _split_w_in(wts["w_in"][l])]
        wbr = [wts[k][l].astype(BF16) for k in ("w_br_a", "w_br_b", "w_br_c", "w_o", "w_gu", "w_down")]
        mv1, mv2 = _mod_rows(mods[l], 0), _mod_rows(mods[l], 3)
        g1, g2 = wts["norm1_g"][l][None, :], wts["norm2_g"][l][None, :]
        cw, scw = wts["dn_conv_w"][l], wts["sc_conv_w"][l]
        alr, dtr = _lane_row(wts["dn_a_log"][l]), _lane_row(wts["dn_dt_bias"][l])
        gdn = wts["dn_norm_g"][l][None, :]
        pwbd, ps = _block_diag(wts["pool_w"][l]), wts["pool_scale"][l][None, :]
        hb, pq, pz, pab, pp, sx, sb, sc_, pg = _inproj_fwd(X, mv1, g1, ws, tc, tt)
        qkv = _dnprep_fwd(pq, cw, tc, tt)
        of, ob, ssf, ssb = _scan_fwd(qkv, pab, alr, dtr, tc)
        yp = _pool_fwd(pp, pwbd, ps, tc)
        ys = _sc_fwd(sx, sb, sc_, scw, tc, tt)
        X1 = _mix_fwd(X, of, ob, pz, yp, ys, pg, mv1, gdn, *wbr[:4], tc, tt)
        X2 = _ffn_fwd(X1, mv2, g2, wbr[4], wbr[5], tc, tt)
        saved.append(dict(X=X, X1=X1, ws=ws, wbr=wbr, mv1=mv1, mv2=mv2, g1=g1, g2=g2, cw=cw, scw=scw, alr=alr, dtr=dtr,
                          gdn=gdn, pwbd=pwbd, ps=ps, hb=hb, pq=pq, pz=pz, pab=pab, pp=pp, sx=sx, sb=sb, sc=sc_, pg=pg,
                          qkv=qkv, of=of, ob=ob, ssf=ssf, ssb=ssb, yp=yp, ys=ys))
        X = X2

    dX, loss, dgf = _loss_head(X, tgt, wts["final_norm_g"][None, :], tc)

    gl = {k: [None] * NL for k in ("w_in", "norm1_g", "norm2_g", "dn_conv_w", "dn_a_log", "dn_dt_bias", "dn_norm_g",
                                   "pool_w", "pool_scale", "sc_conv_w", "w_br_a", "w_br_b", "w_br_c", "w_o", "w_gu",
                                   "w_down")}
    dmods = [None] * NL
    for l in reversed(range(NL)):
        s = saved[l]
        dx1, h2, dgu, act, dff, dg2, dm2 = _ffn_bwd(s["X1"], dX, s["mv2"], s["g2"], s["wbr"][4], s["wbr"][5], tc, tt)
        gl["w_gu"][l] = _dw(h2, dgu, tt)
        gl["w_down"][l] = _dw(act, dff, tt)
        do, dz, dyp, dys, dpg, dwa, dwb, dwc, dwo, dgdn, dmg = _mix_bwd(
            dx1, s["of"], s["ob"], s["pz"], s["yp"], s["ys"], s["pg"], s["mv1"], s["gdn"], *s["wbr"][:4], tc, tt)
        dpp, dpw, dps = _pool_bwd(s["pp"], s["pwbd"], s["ps"], dyp, tc)
        dsx, dsb, dsc, dscw = _sc_bwd(s["sx"], s["sb"], s["sc"], s["scw"], dys, tc, tt)
        dqf, dqb, dpf, dpb, gacc = _scan_bwd(s["qkv"], s["pab"], s["alr"], s["dtr"], do, s["ssf"], s["ssb"], tc)
        dy = _dnprep_bwd_act(s["pq"], s["cw"], dqf, dqb, tc, tt)
        dpq, dcw = _conv_bwd(dy, s["pq"], s["cw"], tc, tt)
        dps_ = [dpq, dz, dpf, dpb, dpp, dsx, dsb, dsc, dpg]
        dp_w = [0, 1, 2, 2, 3, 4, 5, 6, 7]
        dX, dg1, dm1 = _inproj_bwd(s["X"], s["mv1"], s["g1"], s["ws"], dps_, dp_w, dx1, tc, tt)
        dws = [_dw(s["hb"], dpq, tt), _dw(s["hb"], dz, tt), _dw(s["hb"], dpf + dpb, tt)[:, :16], _dw(s["hb"], dpp, tt),
               _dw(s["hb"], dsx, tt), _dw(s["hb"], dsb, tt), _dw(s["hb"], dsc, tt), _dw(s["hb"], dpg, tt)]
        gl["w_in"][l] = jnp.concatenate(dws, axis=1)
        gl["norm1_g"][l], gl["norm2_g"][l] = dg1[0], dg2[0]
        gl["dn_conv_w"][l], gl["sc_conv_w"][l] = dcw, dscw
        gl["dn_a_log"][l], gl["dn_dt_bias"][l] = gacc[0, :8].reshape(2, NH), gacc[1, :8].reshape(2, NH)
        gl["dn_norm_g"][l] = dgdn[0]
        gl["pool_w"][l] = jnp.stack([dpw[k * GW:(k + 1) * GW, k * GW:(k + 1) * GW] for k in range(4)])
        gl["pool_scale"][l] = dps[0]
        gl["w_br_a"][l], gl["w_br_b"][l], gl["w_br_c"][l], gl["w_o"][l] = dwa, dwb, dwc, dwo
        dm = dm1 + dmg
        row = lambda r: jnp.concatenate([dm[r], dm[r + 1], dm[r + 2], dm2[r], dm2[r + 1], dm2[r + 2]])
        dmods[l] = jnp.stack([row(0), row(3)] + [jnp.zeros((6 * D,), F32)] * 6)

    dwada, dbada, dcc = _mod_bwd(cc8, w_ada, jnp.stack(dmods))
    grads = {k: jnp.stack(v) for k, v in gl.items()}
    grads.update(w_ada=dwada, b_ada=dbada.reshape(NL, 6 * D), c_ctx=dcc[0], final_norm_g=dgf[0])
    return loss, dX[tc:], grads


MESH_ID = pl.DeviceIdType.MESH
HBM_SPEC = pl.BlockSpec(memory_space=pltpu.HBM)


def _me():
    return lax.axis_index("x"), lax.axis_index("y"), lax.axis_index("c")


def _dev_index(p):
    return 4 * p[0] + 2 * p[1] + p[2]


def _allgather(parts):
    n = len(parts)

    def body(*refs):
        ins, outs = refs[:n], refs[n:2 * n]
        send_sems, recv_sems, local_sems = refs[2 * n:]
        x, y, c = _me()
        me, sibling = (x, y, c), (x, y, 1 - c)
        chips = [(1 - x, y), (x, 1 - y), (1 - x, 1 - y)]

        def copy(a, k, block, to, src=None):
            dst = outs[a].at[_dev_index(block)]
            return pltpu.make_async_remote_copy(
                src_ref=dst if src is None else src, dst_ref=dst, send_sem=send_sems.at[a, k], recv_sem=recv_sems.at[a, k],
                device_id=to, device_id_type=MESH_ID)

        mine, first, passed = [], [], []
        for a in range(n):
            mine.append(pltpu.make_async_copy(ins[a], outs[a].at[_dev_index(me)], local_sems.at[a]))
            mine[-1].start()
            first.append(copy(a, 0, me, sibling, src=ins[a]))
            first += [copy(a, 1 + j, me, (*chip, c), src=ins[a]) for j, chip in enumerate(chips)]
        for cp in first:
            cp.start()
        for a in range(n):
            for j, chip in enumerate(chips):
                copy(a, 1 + j, (*chip, c), me).wait_recv()
                passed.append(copy(a, 4 + j, (*chip, c), sibling))
                passed[-1].start()
        for a in range(n):
            copy(a, 0, sibling, me).wait_recv()
            for j, chip in enumerate(chips):
                copy(a, 4 + j, (*chip, 1 - c), me).wait_recv()
        for cp in first + passed:
            cp.wait_send()
        for cp in mine:
            cp.wait()

    return pl.pallas_call(
        body, name="allgather", in_specs=[HBM_SPEC] * n, out_specs=[HBM_SPEC] * n,
        out_shape=[_S((N_DEV,) + p.shape, p.dtype) for p in parts],
        scratch_shapes=[pltpu.SemaphoreType.DMA((n, 7)), pltpu.SemaphoreType.DMA((n, 7)), pltpu.SemaphoreType.DMA((n,))],
    )(*parts)


def _exchange(blocked, small):
    arrs = list(blocked) + [small]
    n = len(arrs)

    def body(*refs):
        ins, outs = refs[:n], refs[n:2 * n]
        send_sems, recv_sems, local_sems = refs[2 * n:]
        x, y, c = _me()
        my = _dev_index((x, y, c))

        def src(a, idx):
            return ins[a] if a == n - 1 else ins[a].at[idx]

        copies = []
        for a in range(n):
            mine = pltpu.make_async_copy(src(a, my), outs[a].at[my], local_sems.at[a])
            mine.start()
            copies.append(mine)
        remote = []
        for k in range(1, N_DEV):
            peer = (x ^ (k >> 2), y ^ ((k >> 1) & 1), c ^ (k & 1))
            for a in range(n):
                cp = pltpu.make_async_remote_copy(
                    src_ref=src(a, _dev_index(peer)), dst_ref=outs[a].at[my], send_sem=send_sems.at[a, k - 1],
                    recv_sem=recv_sems.at[a, k - 1], device_id=peer, device_id_type=MESH_ID)
                cp.start()
                remote.append(cp)
        for cp in remote:
            cp.wait_recv()
        for cp in remote:
            cp.wait_send()
        for cp in copies:
            cp.wait()

    outs = pl.pallas_call(
        body, name="grad_exchange", in_specs=[HBM_SPEC] * n, out_specs=[HBM_SPEC] * n,
        out_shape=[_S(a.shape, a.dtype) for a in blocked] + [_S((N_DEV,) + small.shape, small.dtype)],
        scratch_shapes=[pltpu.SemaphoreType.DMA((n, 7)), pltpu.SemaphoreType.DMA((n, 7)), pltpu.SemaphoreType.DMA((n,))],
    )(*arrs)
    return outs[:-1], outs[-1]


def _adam(w, g, m, v):
    m2 = ADAM_B1 * m + (1.0 - ADAM_B1) * g
    v2 = ADAM_B2 * v + (1.0 - ADAM_B2) * (g * g)
    m_hat = m2 / (1.0 - ADAM_B1 ** ADAM_STEP)
    v_hat = v2 / (1.0 - ADAM_B2 ** ADAM_STEP)
    return -ADAM_LR * (m_hat / (jnp.sqrt(v_hat) + ADAM_EPS) + ADAM_WD * w), m2, v2


def _sum_adam(recv, w, m, v):
    L, R, C = w.shape
    tr = 256 if R % 256 == 0 else R

    def body(r_ref, w_ref, m_ref, v_ref, g_ref, d_ref, m2_ref, v2_ref):
        g = r_ref[0, 0]
        for k in range(1, N_DEV):
            g = g + r_ref[k, 0]
        g_ref[0] = g
        d_ref[0], m2_ref[0], v2_ref[0] = _adam(w_ref[0], g, m_ref[0], v_ref[0])

    blk = pl.BlockSpec((1, tr, C), lambda l, i: (l, i, 0))
    return _call(
        body, "sum_adam", (L, R // tr),
        [pl.BlockSpec((N_DEV, 1, tr, C), lambda l, i: (0, l, i, 0)), blk, blk, blk],
        [blk] * 4, [_S(w.shape)] * 4)(recv, w, m, v)


def _sum_small(recv):
    def body(r_ref, o_ref):
        g = r_ref[0]
        for k in range(1, N_DEV):
            g = g + r_ref[k]
        o_ref[...] = g

    return pl.pallas_call(body, name="sum_small", out_shape=_S(recv.shape[1:]))(recv)


def _adam_small(w, g, m, v):
    def body(w_ref, g_ref, m_ref, v_ref, d_ref, m2_ref, v2_ref):
        d_ref[...], m2_ref[...], v2_ref[...] = _adam(w_ref[...], g_ref[...], m_ref[...], v_ref[...])

    return pl.pallas_call(body, name="adam_small", out_shape=[_S(w.shape)] * 3)(w, g, m, v)


def _pack(arrs, dtype, row_mult):
    parts, offs, r = [], [], 0
    for a in arrs:
        nr = -(-a.size // LANES)
        parts.append(jnp.pad(a.reshape(-1).astype(dtype), (0, nr * LANES - a.size)))
        offs.append(r)
        r += nr
    pad = (-r) % row_mult
    if pad:
        parts.append(jnp.zeros((pad * LANES,), dtype))
    return jnp.concatenate(parts).reshape(r + pad, LANES), offs


def _unpack(packed, offs, shapes, lead=()):
    out = []
    for off, shp in zip(offs, shapes):
        size = int(np.prod(shp))
        nr = -(-size // LANES)
        flat = packed[..., off:off + nr, :].reshape(lead + (nr * LANES,))
        out.append(flat[..., :size].reshape(lead + tuple(shp)))
    return out


BIG = (("w_ada", 2), ("w_in", 2), ("w_br_a", 2), ("w_br_b", 2), ("w_br_c", 2), ("w_o", 1), ("w_gu", 2), ("w_down", 1))
CONV = ("dn_conv_w", "sc_conv_w")
REPL = ("c_ctx", "b_ada", "norm1_g", "norm2_g", "dn_a_log", "dn_dt_bias", "dn_norm_g", "pool_w", "pool_scale",
        "final_norm_g")
WEIGHTS = ("c_ctx", "w_ada", "b_ada", "norm1_g", "norm2_g", "w_in", "dn_conv_w", "dn_a_log", "dn_dt_bias", "dn_norm_g",
           "pool_w", "pool_scale", "sc_conv_w", "w_br_a", "w_br_b", "w_br_c", "w_o", "w_gu", "w_down", "final_norm_g")
TOKEN_TILE = 256


def _join(blocks, axis):
    nd, nl, r, c = blocks.shape
    if axis == 2:
        return blocks.transpose(1, 2, 0, 3).reshape(nl, r, nd * c)
    return blocks.transpose(1, 0, 2, 3).reshape(nl, nd * r, c)


def _split(full, axis):
    nl, r, c = full.shape
    if axis == 2:
        return full.reshape(nl, r, N_DEV, c // N_DEV).transpose(2, 0, 1, 3)
    return full.reshape(nl, N_DEV, r // N_DEV, c).transpose(1, 0, 2, 3)


def kernel(x, c, ctx, c_ctx, w_ada, b_ada, norm1_g, norm2_g, w_in, dn_conv_w, dn_a_log, dn_dt_bias, dn_norm_g, pool_w, pool_scale, sc_conv_w, w_br_a, w_br_b, w_br_c, w_o, w_gu, w_down, final_norm_g, loss_target, m_c_ctx, m_w_ada, m_b_ada, m_norm1_g, m_norm2_g, m_w_in, m_dn_conv_w, m_dn_a_log, m_dn_dt_bias, m_dn_norm_g, m_pool_w, m_pool_scale, m_sc_conv_w, m_w_br_a, m_w_br_b, m_w_br_c, m_w_o, m_w_gu, m_w_down, m_final_norm_g, v_c_ctx, v_w_ada, v_b_ada, v_norm1_g, v_norm2_g, v_w_in, v_dn_conv_w, v_dn_a_log, v_dn_dt_bias, v_dn_norm_g, v_pool_w, v_pool_scale, v_sc_conv_w, v_w_br_a, v_w_br_b, v_w_br_c, v_w_o, v_w_gu, v_w_down, v_final_norm_g):
    loc = dict(c_ctx=c_ctx, w_ada=w_ada, b_ada=b_ada, norm1_g=norm1_g, norm2_g=norm2_g, w_in=w_in, dn_conv_w=dn_conv_w,
               dn_a_log=dn_a_log, dn_dt_bias=dn_dt_bias, dn_norm_g=dn_norm_g, pool_w=pool_w, pool_scale=pool_scale,
               sc_conv_w=sc_conv_w, w_br_a=w_br_a, w_br_b=w_br_b, w_br_c=w_br_c, w_o=w_o, w_gu=w_gu, w_down=w_down,
               final_norm_g=final_norm_g)
    mom_m = dict(c_ctx=m_c_ctx, w_ada=m_w_ada, b_ada=m_b_ada, norm1_g=m_norm1_g, norm2_g=m_norm2_g, w_in=m_w_in,
                 dn_conv_w=m_dn_conv_w, dn_a_log=m_dn_a_log, dn_dt_bias=m_dn_dt_bias, dn_norm_g=m_dn_norm_g,
                 pool_w=m_pool_w, pool_scale=m_pool_scale, sc_conv_w=m_sc_conv_w, w_br_a=m_w_br_a, w_br_b=m_w_br_b,
                 w_br_c=m_w_br_c, w_o=m_w_o, w_gu=m_w_gu, w_down=m_w_down, final_norm_g=m_final_norm_g)
    mom_v = dict(c_ctx=v_c_ctx, w_ada=v_w_ada, b_ada=v_b_ada, norm1_g=v_norm1_g, norm2_g=v_norm2_g, w_in=v_w_in,
                 dn_conv_w=v_dn_conv_w, dn_a_log=v_dn_a_log, dn_dt_bias=v_dn_dt_bias, dn_norm_g=v_dn_norm_g,
                 pool_w=v_pool_w, pool_scale=v_pool_scale, sc_conv_w=v_sc_conv_w, w_br_a=v_w_br_a, w_br_b=v_w_br_b,
                 w_br_c=v_w_br_c, w_o=v_w_o, w_gu=v_w_gu, w_down=v_w_down, final_norm_g=v_final_norm_g)
    my = _dev_index(_me())

    big_pack, big_offs = _pack([loc[k] for k, _ in BIG], BF16, BF16_ROWS)
    conv_pack, conv_offs = _pack([loc[k] for k in CONV], F32, 8)
    big_all, conv_all = _allgather([big_pack, conv_pack])
    full = {k: loc[k] for k in REPL}
    for (k, axis), blocks in zip(BIG, _unpack(big_all, big_offs, [loc[k].shape for k, _ in BIG], (N_DEV,))):
        full[k] = _join(blocks, axis)
    for k, blocks in zip(CONV, _unpack(conv_all, conv_offs, [loc[k].shape for k in CONV], (N_DEV,))):
        full[k] = _join(blocks, 2)

    loss8, grad_x, g = _device_step(x[0], c, ctx[0], loss_target[0], full, TOKEN_TILE)

    small_names = REPL + CONV
    small_pack, small_offs = _pack([g[k] for k in small_names] + [loss8[0:1, 0:1]], F32, 8)
    recv_big, recv_small = _exchange([_split(g[k], axis) for k, axis in BIG], small_pack)
    small_sum = _sum_small(recv_small)
    sums = _unpack(small_sum, small_offs, [g[k].shape for k in small_names] + [(1, 1)])
    grads = dict(zip(small_names, sums[:-1]))
    loss = sums[-1][0, 0]
    for k in CONV:
        w = loc[k].shape[2]
        grads[k] = lax.dynamic_slice_in_dim(grads[k], my * w, w, axis=2)

    delta, new_m, new_v = {}, {}, {}
    for (k, _), recv in zip(BIG, recv_big):
        grads[k], delta[k], new_m[k], new_v[k] = _sum_adam(recv, loc[k], mom_m[k], mom_v[k])
    packs = [_pack([src[k] for k in small_names], F32, 8)[0] for src in (loc, grads, mom_m, mom_v)]
    _, offs = _pack([loc[k] for k in small_names], F32, 8)
    shapes = [loc[k].shape for k in small_names]
    for dst, packed in zip((delta, new_m, new_v), _adam_small(*packs)):
        dst.update(zip(small_names, _unpack(packed, offs, shapes)))

    return (loss, grad_x[None], *[grads[k] for k in WEIGHTS], *[delta[k] for k in WEIGHTS],
            *[new_m[k] for k in WEIGHTS], *[new_v[k] for k in WEIGHTS])
```

```python
import functools

import numpy as np
import jax
import jax.numpy as jnp
from jax import lax
from jax.experimental import pallas as pl
from jax.experimental.pallas import tpu as pltpu

F32 = jnp.float32
BF16 = jnp.bfloat16
HI = lax.Precision.HIGHEST

D = 1024
NL = 2
NH = 4
DH = 128
DN = NH * DH
CH = 64
GW = 64
PW = 256
DFF = 2816
EPS = 1e-6
N_DEV = 8
LANES = 128
BF16_ROWS = 16
VMEM_MB = 56

ADAM_LR, ADAM_B1, ADAM_B2, ADAM_EPS, ADAM_WD, ADAM_STEP = 0.001, 0.9, 0.999, 1e-08, 0.01, 10

IN_BOUNDS = (0, 1536, 2048, 2064, 2320, 2576, 2832, 3088, 6160)
IN_WIDTHS = (1536, 512, 128, 256, 256, 256, 256, 3072)
POOL_WIN = ((1, 0), (2, 1), (4, 3), (8, 7))

NN = ((1,), (0,))
NT = ((1,), (1,))
TN = ((0,), (0,))


def _dot(a, b, dims, hi=False):
    if hi:
        prec = lax.Precision.HIGH if hi == "x3" else HI
        return lax.dot_general(a, b, (dims, ((), ())), precision=prec, preferred_element_type=F32)
    return lax.dot_general(a.astype(BF16), b.astype(BF16), (dims, ((), ())), preferred_element_type=F32)


def _S(shape, dtype=F32):
    return jax.ShapeDtypeStruct(tuple(shape), dtype)


def _full(shape):
    nd = len(shape)
    return pl.BlockSpec(tuple(shape), lambda *_: (0,) * nd)


def _rows(tt, w):
    return pl.BlockSpec((tt, w), lambda i: (i, 0))


def _call(body, name, grid, in_specs, out_specs, out_shape, scratch=()):
    return pl.pallas_call(
        body, name=name, grid=grid, in_specs=in_specs, out_specs=out_specs, out_shape=out_shape,
        scratch_shapes=list(scratch),
        compiler_params=pltpu.CompilerParams(
            dimension_semantics=("arbitrary",) * len(grid), vmem_limit_bytes=VMEM_MB << 20),
    )


def _iota(shape, axis):
    return lax.broadcasted_iota(jnp.int32, shape, axis)


def _colsum(a):
    return jnp.sum(a, axis=0, keepdims=True)


def _silu(x):
    return x * jax.nn.sigmoid(x)


def _modulate(x, g, sh, sc):
    xn = x * lax.rsqrt(jnp.mean(x * x, axis=-1, keepdims=True) + EPS)
    return (xn * g) * (1.0 + sc) + sh


def _stream_rows(mv_ref, i, tt, tc, k):
    isc = (i * tt + _iota((tt, 1), 0)) < tc
    return isc, jnp.where(isc, mv_ref[k:k + 1, :], mv_ref[3 + k:4 + k, :])


def _acc_stream(ref, k, isc, val):
    ref[k:k + 1, :] += _colsum(jnp.where(isc, val, 0.0))
    ref[3 + k:4 + k, :] += _colsum(jnp.where(isc, 0.0, val))


MOD_CT = 1536


def _mod_fwd(cc8, w_ada, b_ada3):
    def body(cc_ref, w_ref, b_ref, o_ref):
        o_ref[0] = _dot(_silu(cc_ref[...]), w_ref[0], NN) + b_ref[0]

    return _call(
        body, "mod_fwd", (NL, 6 * D // MOD_CT),
        [pl.BlockSpec((8, D), lambda l, j: (0, 0)), pl.BlockSpec((1, D, MOD_CT), lambda l, j: (l, 0, j)),
         pl.BlockSpec((1, 1, MOD_CT), lambda l, j: (l, 0, j))],
        pl.BlockSpec((1, 8, MOD_CT), lambda l, j: (l, 0, j)), _S((NL, 8, 6 * D)))(cc8, w_ada, b_ada3)


def _mod_bwd(cc8, w_ada, dmods):
    def body(cc_ref, w_ref, dm_ref, dw_ref, db_ref, dcc_ref):
        first = (pl.program_id(0) == 0) & (pl.program_id(1) == 0)
        cc = cc_ref[...]
        sg = jax.nn.sigmoid(cc)
        dm = dm_ref[0]
        dw_ref[0] = _dot(cc * sg, dm, TN)
        db_ref[0] = dm[0:1, :] + dm[1:2, :]

        @pl.when(first)
        def _():
            dcc_ref[...] = jnp.zeros_like(dcc_ref)

        dcc_ref[...] += _dot(dm, w_ref[0], NT) * (sg * (1.0 + cc * (1.0 - sg)))

    return _call(
        body, "mod_bwd", (NL, 6 * D // MOD_CT),
        [pl.BlockSpec((8, D), lambda l, j: (0, 0)), pl.BlockSpec((1, D, MOD_CT), lambda l, j: (l, 0, j)),
         pl.BlockSpec((1, 8, MOD_CT), lambda l, j: (l, 0, j))],
        [pl.BlockSpec((1, D, MOD_CT), lambda l, j: (l, 0, j)), pl.BlockSpec((1, 1, MOD_CT), lambda l, j: (l, 0, j)),
         pl.BlockSpec((8, D), lambda l, j: (0, 0))],
        [_S((NL, D, 6 * D)), _S((NL, 1, 6 * D)), _S((8, D))])(cc8, w_ada, dmods)


def _inproj_fwd(X, mv, g, ws, tc, tt):
    T = X.shape[0]
    nw = len(ws)

    def body(x_ref, mv_ref, g_ref, *refs):
        w_refs, h_ref, p_refs = refs[:nw], refs[nw], refs[nw + 1:]
        i = pl.program_id(0)
        _, sh = _stream_rows(mv_ref, i, tt, tc, 0)
        _, sc = _stream_rows(mv_ref, i, tt, tc, 1)
        hb = _modulate(x_ref[...], g_ref[...], sh, sc).astype(BF16)
        h_ref[...] = hb
        for w_ref, p_ref in zip(w_refs, p_refs):
            p_ref[...] = jnp.dot(hb, w_ref[...], preferred_element_type=F32)

    return _call(
        body, "inproj_fwd", (T // tt,),
        [_rows(tt, D), _full((8, D)), _full((1, D))] + [_full(w.shape) for w in ws],
        [_rows(tt, D)] + [_rows(tt, w.shape[1]) for w in ws],
        [_S((T, D), BF16)] + [_S((T, w.shape[1])) for w in ws])(X, mv, g, *ws)


def _inproj_bwd(X, mv, g, ws, dps, dp_w, dres, tc, tt):
    T = X.shape[0]
    nw, nd = len(ws), len(dps)

    def body(x_ref, mv_ref, g_ref, dres_ref, *refs):
        w_refs, dp_refs = refs[:nw], refs[nw:nw + nd]
        dx_ref, dg_ref, dm_ref = refs[nw + nd:]
        i = pl.program_id(0)
        isc, sh = _stream_rows(mv_ref, i, tt, tc, 0)
        _, sc = _stream_rows(mv_ref, i, tt, tc, 1)
        dh = None
        for dp_ref, k in zip(dp_refs, dp_w):
            t = _dot(dp_ref[...], w_refs[k][...], NT)
            dh = t if dh is None else dh + t
        _, vjp = jax.vjp(_modulate, x_ref[...], g_ref[...], sh, sc)
        dx, dg, dsh, dsc = vjp(dh)
        dx_ref[...] = dres_ref[...] + dx

        @pl.when(i == 0)
        def _():
            dg_ref[...] = jnp.zeros_like(dg_ref)
            dm_ref[...] = jnp.zeros_like(dm_ref)

        dg_ref[...] += dg
        _acc_stream(dm_ref, 0, isc, dsh)
        _acc_stream(dm_ref, 1, isc, dsc)

    return _call(
        body, "inproj_bwd", (T // tt,),
        [_rows(tt, D), _full((8, D)), _full((1, D)), _rows(tt, D)] + [_full(w.shape) for w in ws]
        + [_rows(tt, dp.shape[1]) for dp in dps],
        [_rows(tt, D), _full((1, D)), _full((8, D))],
        [_S((T, D)), _S((1, D)), _S((8, D))])(X, mv, g, dres, *ws, *dps)


def _dw(A, B, tt):
    T, K = A.shape
    N = B.shape[1]
    tn = 512 if N % 512 == 0 else (256 if N % 256 == 0 else LANES)

    def body(a_ref, b_ref, o_ref):
        @pl.when(pl.program_id(1) == 0)
        def _():
            o_ref[...] = jnp.zeros_like(o_ref)

        o_ref[...] += _dot(a_ref[...], b_ref[...], TN)

    return _call(
        body, "dw", (N // tn, T // tt),
        [pl.BlockSpec((tt, K), lambda j, i: (i, 0)), pl.BlockSpec((tt, tn), lambda j, i: (i, j))],
        pl.BlockSpec((K, tn), lambda j, i: (0, j)), _S((K, N)))(A, B)


def _halo_specs(T, tt, cw, col):
    r8, nb8 = tt // 8, T // 8
    return [pl.BlockSpec((tt, cw), lambda j, i: (i, col(j))),
            pl.BlockSpec((8, cw), lambda j, i: (jnp.maximum(i * r8 - 1, 0), col(j))),
            pl.BlockSpec((8, cw), lambda j, i: (jnp.minimum((i + 1) * r8, nb8 - 1), col(j)))]


def _shifts(a, prev8, next8, i, tt, tc, T):
    r = _iota((tt, 1), 0)
    t = i * tt + r
    dn = jnp.where(r == 0, prev8[7:8, :], pltpu.roll(a, 1, 0))
    dn = jnp.where((t == 0) | (t == tc), 0.0, dn)
    up = jnp.where(r == tt - 1, next8[0:1, :], pltpu.roll(a, tt - 1, 0))
    up = jnp.where((t == T - 1) | (t == tc - 1), 0.0, up)
    return dn, up


def _dn_post(y, part):
    a = _silu(y)
    nrm = lax.rsqrt(jnp.sum(a * a, axis=-1, keepdims=True) + EPS)
    f = jnp.where(part == 0, nrm * (DH ** -0.5), jnp.where(part == 1, nrm, 1.0))
    return a * f


def _conv3(w_ref, dn, mid, up):
    return w_ref[0:1, :] * dn + w_ref[1:2, :] * mid + w_ref[2:3, :] * up


def _dnprep_fwd(pq, cw, tc, tt):
    T = pq.shape[0]

    def body(p_ref, pp_ref, pn_ref, w_ref, a_ref):
        part, i = pl.program_id(0), pl.program_id(1)
        p = p_ref[...]
        dn, up = _shifts(p, pp_ref[...], pn_ref[...], i, tt, tc, T)
        y = _conv3(w_ref, dn, p, up)
        for h in range(NH):
            a_ref[:, _hs(h)] = _dn_post(y[:, _hs(h)], part)

    return _call(
        body, "dnprep_fwd", (3, T // tt),
        _halo_specs(T, tt, DN, lambda j: j) + [pl.BlockSpec((3, DN), lambda j, i: (0, j))],
        pl.BlockSpec((tt, DN), lambda j, i: (i, j)), _S((T, 3 * DN)))(pq, pq, pq, cw)


def _dnprep_bwd_act(pq, cw, da_f, da_b, tc, tt):
    T = pq.shape[0]

    def body(p_ref, pp_ref, pn_ref, w_ref, df_ref, db_ref, dy_ref):
        part, i = pl.program_id(0), pl.program_id(1)
        p = p_ref[...]
        dn, up = _shifts(p, pp_ref[...], pn_ref[...], i, tt, tc, T)
        y = _conv3(w_ref, dn, p, up)
        for h in range(NH):
            _, vjp = jax.vjp(lambda yh: _dn_post(yh, part), y[:, _hs(h)])
            dy_ref[:, _hs(h)] = vjp(df_ref[:, _hs(h)] + db_ref[:, _hs(h)])[0]

    blk = pl.BlockSpec((tt, DN), lambda j, i: (i, j))
    return _call(
        body, "dnprep_bwd_act", (3, T // tt),
        _halo_specs(T, tt, DN, lambda j: j) + [pl.BlockSpec((3, DN), lambda j, i: (0, j)), blk, blk],
        blk, _S((T, 3 * DN)))(pq, pq, pq, cw, da_f, da_b)


def _conv_bwd(dy, p, cw, tc, tt):
    T, W = p.shape
    cb = DN

    def body(dy_ref, dyp_ref, dyn_ref, p_ref, pp_ref, pn_ref, w_ref, dp_ref, dw_ref):
        i = pl.program_id(1)
        dy, p_ = dy_ref[...], p_ref[...]
        ddn, dup = _shifts(dy, dyp_ref[...], dyn_ref[...], i, tt, tc, T)
        dp_ref[...] = _conv3(w_ref, dup, dy, ddn)
        pdn, pup = _shifts(p_, pp_ref[...], pn_ref[...], i, tt, tc, T)

        @pl.when(i == 0)
        def _():
            dw_ref[...] = jnp.zeros_like(dw_ref)

        dw_ref[0:1, :] += _colsum(dy * pdn)
        dw_ref[1:2, :] += _colsum(dy * p_)
        dw_ref[2:3, :] += _colsum(dy * pup)

    wspec = pl.BlockSpec((3, cb), lambda j, i: (0, j))
    return _call(
        body, "conv_bwd", (W // cb, T // tt),
        _halo_specs(T, tt, cb, lambda j: j) * 2 + [wspec],
        [pl.BlockSpec((tt, cb), lambda j, i: (i, j)), wspec], [_S((T, W)), _S((3, W))])(dy, dy, dy, p, p, p, cw)


def _sc_fwd(sx, sb, sc_, cw, tc, tt):
    T = sx.shape[0]

    def body(x_ref, xp_ref, xn_ref, c_ref, cp_ref, cn_ref, b_ref, w_ref, y_ref):
        i = pl.program_id(1)
        u = c_ref[...] * x_ref[...]
        dn, up = _shifts(u, cp_ref[...] * xp_ref[...], cn_ref[...] * xn_ref[...], i, tt, tc, T)
        y_ref[...] = b_ref[...] * _conv3(w_ref, dn, u, up)

    blk = pl.BlockSpec((tt, LANES), lambda j, i: (i, j))
    return _call(
        body, "sc_fwd", (PW // LANES, T // tt),
        _halo_specs(T, tt, LANES, lambda j: j) * 2 + [blk, pl.BlockSpec((3, LANES), lambda j, i: (0, j))],
        blk, _S((T, PW)))(sx, sx, sx, sc_, sc_, sc_, sb, cw)


def _sc_bwd(sx, sb, sc_, cw, dy, tc, tt):
    T = sx.shape[0]

    def body(x_ref, xp_ref, xn_ref, c_ref, cp_ref, cn_ref, b_ref, bp_ref, bn_ref, dy_ref, dyp_ref, dyn_ref, w_ref,
             dx_ref, db_ref, dc_ref, dw_ref):
        i = pl.program_id(1)
        x, c, dy_ = x_ref[...], c_ref[...], dy_ref[...]
        u = c * x
        udn, uup = _shifts(u, cp_ref[...] * xp_ref[...], cn_ref[...] * xn_ref[...], i, tt, tc, T)
        db_ref[...] = dy_ * _conv3(w_ref, udn, u, uup)
        e = dy_ * b_ref[...]
        edn, eup = _shifts(e, dyp_ref[...] * bp_ref[...], dyn_ref[...] * bn_ref[...], i, tt, tc, T)
        du = _conv3(w_ref, eup, e, edn)
        dx_ref[...] = du * c
        dc_ref[...] = du * x

        @pl.when(i == 0)
        def _():
            dw_ref[...] = jnp.zeros_like(dw_ref)

        dw_ref[0:1, :] += _colsum(e * udn)
        dw_ref[1:2, :] += _colsum(e * u)
        dw_ref[2:3, :] += _colsum(e * uup)

    blk = pl.BlockSpec((tt, LANES), lambda j, i: (i, j))
    wspec = pl.BlockSpec((3, LANES), lambda j, i: (0, j))
    return _call(
        body, "sc_bwd", (PW // LANES, T // tt),
        _halo_specs(T, tt, LANES, lambda j: j) * 4 + [wspec],
        [blk, blk, blk, wspec], [_S((T, PW))] * 3 + [_S((3, PW))])(
            sx, sx, sx, sc_, sc_, sc_, sb, sb, sb, dy, dy, dy, cw)


def _group_select(vals):
    g = _iota((1, PW), 1) // (PW // len(POOL_WIN))
    return jnp.where(g == 0, vals[0], jnp.where(g == 1, vals[1], jnp.where(g == 2, vals[2], vals[3])))


def _nested_box(get, mirror):
    acc, outs, pl_, ph_ = get(0), [], 0, 0
    for lo, hi in POOL_WIN:
        if mirror:
            lo, hi = hi, lo
        for k in range(pl_ + 1, lo + 1):
            acc = acc + get(-k)
        for k in range(ph_ + 1, hi + 1):
            acc = acc + get(k)
        pl_, ph_ = lo, hi
        outs.append(acc)
    return _group_select(outs)


def _box_tokens(a, n, mirror):
    idx = _iota((n, 1), 0)

    def get(k):
        if k == 0:
            return a
        return jnp.where((idx + k >= 0) & (idx + k < n), pltpu.roll(a, (-k) % n, 0), 0.0)

    return _nested_box(get, mirror)


def _inv_count(pos, n):
    return _group_select([1.0 / (jnp.minimum(pos + hi, n - 1) - jnp.maximum(pos - lo, 0) + 1).astype(F32)
                          for lo, hi in POOL_WIN])


def _pool_rows(ref, r, R, tc, mirror):
    def get(k):
        rr = r + k
        rc = jnp.clip(rr, 0, R - 1)
        v = ref[pl.ds(pl.multiple_of(tc + rc * GW, GW), GW), :]
        if mirror:
            v = v * _inv_count(jnp.full((1, PW), rc, jnp.int32), R)
        return jnp.where((rr >= 0) & (rr < R), v, 0.0)

    return _nested_box(get, mirror)


def _pool_fwd(u, pwbd, ps, tc):
    T = u.shape[0]
    R = (T - tc) // GW

    def body(u_ref, pw_ref, ps_ref, y_ref):
        pw, scale = pw_ref[...], ps_ref[...]
        uc = u_ref[0:tc, :]
        mc = _box_tokens(uc, tc, False) * _inv_count(_iota((tc, 1), 0), tc)
        y_ref[0:tc, :] = _dot(mc - uc, pw, NN) * scale
        inv_c = _inv_count(_iota((GW, 1), 0), GW)

        def row(r, carry):
            rs = _pool_rows(u_ref, r, R, tc, False) * _inv_count(jnp.full((1, PW), r, jnp.int32), R)
            m = _box_tokens(rs, GW, False) * inv_c
            sl = pl.ds(pl.multiple_of(tc + r * GW, GW), GW)
            y_ref[sl, :] = _dot(m - u_ref[sl, :], pw, NN) * scale
            return carry

        lax.fori_loop(0, R, row, 0)

    return pl.pallas_call(
        body, name="pool_fwd", out_shape=_S((T, PW)),
        compiler_params=pltpu.CompilerParams(vmem_limit_bytes=VMEM_MB << 20))(u, pwbd, ps)


def _pool_bwd(u, pwbd, ps, dy, tc):
    T = u.shape[0]
    R = (T - tc) // GW

    def body(u_ref, pw_ref, ps_ref, dy_ref, du_ref, dpw_ref, dps_ref, dd_ref):
        pw, scale = pw_ref[...], ps_ref[...]
        dpw_ref[...] = jnp.zeros_like(dpw_ref)
        dps_ref[...] = jnp.zeros_like(dps_ref)

        def back(d, dy_):
            dz = dy_ * scale
            dpw_ref[...] += _dot(d, dz, TN)
            dps_ref[...] += _colsum(dy_ * _dot(d, pw, NN))
            return _dot(dz, pw, NT)

        uc = u_ref[0:tc, :]
        inv_cc = _inv_count(_iota((tc, 1), 0), tc)
        ddc = back(_box_tokens(uc, tc, False) * inv_cc - uc, dy_ref[0:tc, :])
        du_ref[0:tc, :] = _box_tokens(ddc * inv_cc, tc, True) - ddc
        inv_c = _inv_count(_iota((GW, 1), 0), GW)

        def row1(r, carry):
            rs = _pool_rows(u_ref, r, R, tc, False) * _inv_count(jnp.full((1, PW), r, jnp.int32), R)
            m = _box_tokens(rs, GW, False) * inv_c
            sl = pl.ds(pl.multiple_of(tc + r * GW, GW), GW)
            dd_ref[sl, :] = back(m - u_ref[sl, :], dy_ref[sl, :])
            return carry

        lax.fori_loop(0, R, row1, 0)

        def row2(r, carry):
            t1 = _pool_rows(dd_ref, r, R, tc, True)
            sl = pl.ds(pl.multiple_of(tc + r * GW, GW), GW)
            du_ref[sl, :] = _box_tokens(t1 * inv_c, GW, True) - dd_ref[sl, :]
            return carry

        lax.fori_loop(0, R, row2, 0)

    return pl.pallas_call(
        body, name="pool_bwd", out_shape=[_S((T, PW)), _S((PW, PW)), _S((1, PW))],
        scratch_shapes=[pltpu.VMEM((T, PW), F32)],
        compiler_params=pltpu.CompilerParams(vmem_limit_bytes=VMEM_MB << 20))(u, pwbd, ps, dy)


def _scan_consts():
    i = np.arange(CH)
    lower = (i[:, None] >= i[None, :]).astype(np.float32)
    return jnp.asarray(np.stack([lower, lower.T])), jnp.asarray(np.stack([lower.T, lower]))


def _gates(pab, al, dtb, csum):
    sp_in = pab + dtb
    sp = jnp.maximum(sp_in, 0.0) + jnp.log(1.0 + jnp.exp(-jnp.abs(sp_in)))
    nexp = -jnp.exp(al)
    gm = nexp * sp
    return gm, jax.nn.sigmoid(pab), _dot(csum, gm, NN, hi=True), sp_in, nexp


def _lane_col(m, j):
    return jnp.sum(jnp.where(_iota(m.shape, 1) == j, m, 0.0), axis=1, keepdims=True)


def _hs(h):
    return slice(h * DH, (h + 1) * DH)


HS = NH * CH
X3 = "x3"


def _stack(x, base=0):
    return jnp.concatenate([x[:, base + h * DH:base + (h + 1) * DH] for h in range(NH)], axis=0)


def _heads(st):
    return [st[h * CH:(h + 1) * CH] for h in range(NH)]


def _rowsum(a):
    return jnp.sum(a, axis=1, keepdims=True)


def _row_of(col):
    e0 = (_iota((8, LANES), 1) == 0).astype(F32)
    return _dot(e0, jnp.broadcast_to(col, (HS, LANES)), NT, hi=True)[0:1, :]


def _dn_chunk(qkv, pab, al, dtb, csum_d, d):
    gm, bm, gcm, sp_in, nexp = _gates(pab, al, dtb, csum_d)
    gc = jnp.concatenate([_lane_col(gcm, d * NH + h) for h in range(NH)], axis=0)
    beta = jnp.concatenate([_lane_col(bm, 8 + d * NH + h) for h in range(NH)], axis=0)
    q, k, v = _stack(qkv, 0), _stack(qkv, DN), _stack(qkv, 2 * DN)
    ii, jj = _iota((HS, HS), 0), _iota((HS, HS), 1)
    sh = CH.bit_length() - 1
    same = (ii >> sh) == (jj >> sh)
    incl = same & ((ii >= jj) if d == 0 else (ii <= jj))
    strict = same & ((ii > jj) if d == 0 else (ii < jj))
    last = CH - 1 if d == 0 else 0
    Di = jnp.where(incl, jnp.exp(jnp.where(incl, gc - _row_of(gc), 0.0)), 0.0)
    Ds = jnp.where(strict, Di, 0.0)
    kb = k * beta
    kk = _dot(kb, k, NT)
    nm = -(kk * Ds)
    tm = (ii == jj).astype(F32) + nm
    mp = nm
    for _ in range(5):
        mp = _dot(mp, mp, NN, hi=X3)
        tm = tm + _dot(tm, mp, NN, hi=X3)
    E = jnp.exp(gc)
    gls = [gc[h * CH + last:h * CH + last + 1, :] for h in range(NH)]
    xs = jnp.exp(jnp.concatenate([jnp.broadcast_to(g, (CH, 1)) for g in gls], axis=0) - gc)
    qk = _dot(q, k, NT)
    return dict(q=q, k=k, v=v, beta=beta, gm=gm, bm=bm, sp_in=sp_in, nexp=nexp, Di=Di, Ds=Ds, strict=strict, last=last,
                kb=kb, kk=kk, tm=tm, E=E, gls=gls, xs=xs, qk=qk, u=_dot(tm, v * beta, NN, hi=X3),
                w=_dot(tm, kb * E, NN, hi=X3), ks=k * xs, qd=q * E, aqk=qk * Di)


def _dn_chunk_bwd(c, S, dS2, do, vn, dvn):
    q, k, v, beta, tm, E, xs, kb, u, w = (c[n] for n in ("q", "k", "v", "beta", "tm", "E", "xs", "kb", "u", "w"))
    doh, vnh, dvnh = _heads(do), _heads(vn), _heads(dvn)
    dqd = jnp.concatenate([_dot(doh[h], S[h], NT) for h in range(NH)], axis=0)
    dks = jnp.concatenate([_dot(vnh[h], dS2[h], NT) for h in range(NH)], axis=0)
    dw = -jnp.concatenate([_dot(dvnh[h], S[h], NT) for h in range(NH)], axis=0)
    daqk = _dot(do, vn, NT)
    drb = _dot(tm, dvn, TN, hi=X3)
    drw = _dot(tm, dw, TN, hi=X3)
    dA = jnp.where(c["strict"], -(_dot(drb, u, NT, hi=X3) + _dot(drw, w, NT, hi=X3)), 0.0)
    dM1 = dA * c["Ds"]
    dM2 = daqk * c["Di"]
    dkb = _dot(dM1, k, NN) + drw * E
    dk = _dot(dM1, kb, TN) + _dot(dM2, q, TN) + dks * xs
    dq = _dot(dM2, k, NN) + dqd * E
    G = dM1 * c["kk"] + dM2 * c["qk"]
    col = _dot(G, jnp.ones((HS, LANES), F32), TN, hi=True)[:, 0:1]
    dxx = _rowsum(dks * k) * xs
    dgc = _rowsum(G) - col + (_rowsum(dqd * q) + _rowsum(drw * kb)) * E - dxx
    at_last = _iota((CH, 1), 0) == c["last"]
    ends = []
    for h in range(NH):
        dgl = _colsum(_rowsum(S[h] * dS2[h])) * jnp.exp(c["gls"][h]) + _colsum(dxx[h * CH:(h + 1) * CH])
        ends.append(jnp.where(at_last, dgl, 0.0))
    dgc = dgc + jnp.concatenate(ends, axis=0)
    dbeta = _rowsum(drb * v) + _rowsum(dkb * k)
    return dq, dk + dkb * beta, drb * beta, dgc, dbeta


def _chunk_group(n):
    return 2 if n % 2 == 0 else 1


def _dn_chunks_fwd(qkv, pab, alr, dtr):
    T = qkv.shape[0]
    n = T // CH
    G = _chunk_group(n)
    csum, _ = _scan_consts()

    def body(q_ref, p_ref, cs_ref, al_ref, dt_ref, *outs):
        for g in range(G):
            tok, rows = slice(g * CH, (g + 1) * CH), slice(g * HS, (g + 1) * HS)
            for d in range(2):
                u_ref, w_ref, ks_ref, qd_ref, aqk_ref, eg_ref = outs[6 * d:6 * d + 6]
                c = _dn_chunk(q_ref[tok, :], p_ref[tok, :], al_ref[...], dt_ref[...], cs_ref[d], d)
                u_ref[rows, :] = c["u"]
                w_ref[rows, :] = c["w"].astype(BF16)
                ks_ref[rows, :] = c["ks"].astype(BF16)
                qd_ref[rows, :] = c["qd"].astype(BF16)
                aqk_ref[rows, :] = c["aqk"].astype(BF16)
                egs = [jnp.broadcast_to(jnp.exp(gl), (1, LANES)) for gl in c["gls"]]
                eg_ref[g * 8:(g + 1) * 8, :] = jnp.concatenate(egs + [jnp.zeros((8 - NH, LANES), F32)], axis=0)

    st = lambda w_: pl.BlockSpec((G * HS, w_), lambda i: (i, 0))
    one = [st(DH)] * 4 + [st(HS), pl.BlockSpec((G * 8, LANES), lambda i: (i, 0))]
    shp = [_S((n * HS, DH)), _S((n * HS, DH), BF16), _S((n * HS, DH), BF16), _S((n * HS, DH), BF16),
           _S((n * HS, HS), BF16), _S((n * 8, LANES))]
    outs = _call(
        body, "dn_chunks_fwd", (n // G,),
        [_rows(G * CH, 3 * DN), _rows(G * CH, LANES), _full((2, CH, CH)), _full((1, LANES)), _full((1, LANES))],
        one * 2, shp * 2)(qkv, pab, csum, alr, dtr)
    return tuple(outs[:6]), tuple(outs[6:])


def _scan_order(n, ncx):
    return (lambda i: i), (lambda i: jnp.where(i < ncx, ncx - 1 - i, n - 1 - (i - ncx)))


def _scan_specs(order):
    st = lambda w_: pl.BlockSpec((HS, w_), lambda i: (order(i), 0))
    return dict(st=st(DH), aqk=st(HS), eg=pl.BlockSpec((8, LANES), lambda i: (order(i), 0)),
                tok=pl.BlockSpec((CH, DN), lambda i: (order(i), 0)), state=pl.BlockSpec((1, DN, DH), lambda i: (order(i), 0, 0)))


def _scan_fwd(parts, T, tc):
    n = T // CH
    orders = _scan_order(n, tc // CH)

    def body(*refs):
        S_f, S_b = refs[-2:]

        @pl.when(pl.program_id(0) == 0)
        def _():
            S_f[...] = jnp.zeros_like(S_f)
            S_b[...] = jnp.zeros_like(S_b)

        for d, S in enumerate((S_f, S_b)):
            u_ref, w_ref, ks_ref, qd_ref, aqk_ref, eg_ref = refs[6 * d:6 * d + 6]
            o_ref, ss_ref, vn_ref = refs[12 + 3 * d:15 + 3 * d]
            ss_ref[0] = S[...]
            Sh = [S[_hs(h), :] for h in range(NH)]
            wh, ksh, qdh = _heads(w_ref[...]), _heads(ks_ref[...]), _heads(qd_ref[...])
            vn = u_ref[...] - jnp.concatenate([_dot(wh[h], Sh[h], NN) for h in range(NH)], axis=0)
            vn_ref[...] = vn
            av, vnh = _heads(_dot(aqk_ref[...], vn, NN)), _heads(vn)
            for h in range(NH):
                o_ref[:, _hs(h)] = _dot(qdh[h], Sh[h], NN) + av[h]
                S[_hs(h), :] = Sh[h] * eg_ref[h:h + 1, :] + _dot(ksh[h], vnh[h], TN)

    ins, outs, shp = [], [], []
    for d in range(2):
        sp = _scan_specs(orders[d])
        ins += [sp["st"]] * 4 + [sp["aqk"], sp["eg"]]
        outs += [sp["tok"], sp["state"], sp["st"]]
        shp += [_S((T, DN)), _S((n, DN, DH)), _S((n * HS, DH))]
    res = _call(body, "scan_fwd", (n,), ins, outs, shp,
                scratch=[pltpu.VMEM((DN, DH), F32), pltpu.VMEM((DN, DH), F32)])(*parts[0], *parts[1])
    return tuple(res[:3]), tuple(res[3:])


def _scan_bwd(do, parts, tc):
    T = do.shape[0]
    n = T // CH
    fwd_orders = _scan_order(n, tc // CH)
    orders = [lambda s, f=f: f(n - 1 - s) for f in fwd_orders]

    def body(*refs):
        dS_f, dS_b = refs[-2:]

        @pl.when(pl.program_id(0) == 0)
        def _():
            dS_f[...] = jnp.zeros_like(dS_f)
            dS_b[...] = jnp.zeros_like(dS_b)

        for d, dS in enumerate((dS_f, dS_b)):
            do_ref, w_ref, ks_ref, qd_ref, aqk_ref, eg_ref = refs[6 * d:6 * d + 6]
            dvn_ref, dss_ref = refs[12 + 2 * d:14 + 2 * d]
            dss_ref[0] = dS[...]
            dSh = [dS[_hs(h), :] for h in range(NH)]
            wh, ksh, qdh = _heads(w_ref[...]), _heads(ks_ref[...]), _heads(qd_ref[...])
            do_st = _stack(do_ref[...])
            dvn = _dot(aqk_ref[...], do_st, TN) + jnp.concatenate([_dot(ksh[h], dSh[h], NN) for h in range(NH)], axis=0)
            dvn_ref[...] = dvn
            doh, dvnh = _heads(do_st), _heads(dvn)
            for h in range(NH):
                dS[_hs(h), :] = _dot(qdh[h], doh[h], TN) + dSh[h] * eg_ref[h:h + 1, :] - _dot(wh[h], dvnh[h], TN)

    ins, outs, shp, args = [], [], [], []
    for d in range(2):
        sp = _scan_specs(orders[d])
        ins += [sp["tok"]] + [sp["st"]] * 3 + [sp["aqk"], sp["eg"]]
        outs += [sp["st"], sp["state"]]
        shp += [_S((n * HS, DH)), _S((n, DN, DH))]
        args += [do, *parts[d][1:]]
    res = _call(body, "scan_bwd", (n,), ins, outs, shp,
                scratch=[pltpu.VMEM((DN, DH), F32), pltpu.VMEM((DN, DH), F32)])(*args)
    return tuple(res[:2]), tuple(res[2:])


def _dn_chunks_bwd(qkv, pab, alr, dtr, do, fwd, bwd):
    T = qkv.shape[0]
    n = T // CH
    G = _chunk_group(n)
    csum, csum_t = _scan_consts()

    def body(q_ref, p_ref, do_ref, cs_ref, cst_ref, al_ref, dt_ref, *refs):
        dq_refs, dp_refs, acc_ref = refs[8:10], refs[10:12], refs[12]

        @pl.when(pl.program_id(0) == 0)
        def _():
            acc_ref[...] = jnp.zeros_like(acc_ref)

        lane = _iota((CH, LANES), 1)
        for g in range(G):
            tok, rows = slice(g * CH, (g + 1) * CH), slice(g * HS, (g + 1) * HS)
            do_st = _stack(do_ref[tok, :])
            for d in range(2):
                vn_ref, dvn_ref, ss_ref, dss_ref = refs[4 * d:4 * d + 4]
                c = _dn_chunk(q_ref[tok, :], p_ref[tok, :], al_ref[...], dt_ref[...], cs_ref[d], d)
                dq, dk, dv, dgc, dbeta = _dn_chunk_bwd(
                    c, [ss_ref[g, _hs(h), :] for h in range(NH)], [dss_ref[g, _hs(h), :] for h in range(NH)],
                    do_st, vn_ref[rows, :], dvn_ref[rows, :])
                dgcm = jnp.zeros((CH, LANES), F32)
                dbm = jnp.zeros((CH, LANES), F32)
                for h, (a, b_, c_, e, f) in enumerate(zip(*map(_heads, (dq, dk, dv, dgc, dbeta)))):
                    dq_refs[d][tok, _hs(h)] = a
                    dq_refs[d][tok, _hs(NH + h)] = b_
                    dq_refs[d][tok, _hs(2 * NH + h)] = c_
                    dgcm = jnp.where(lane == d * NH + h, e, dgcm)
                    dbm = jnp.where(lane == 8 + d * NH + h, f, dbm)
                dgm = _dot(cst_ref[d], dgcm, NN, hi=True)
                dsp = dgm * c["nexp"] * jax.nn.sigmoid(c["sp_in"])
                dp_refs[d][tok, :] = dsp + dbm * c["bm"] * (1.0 - c["bm"])
                acc_ref[0:1, :] += _colsum(dgm * c["gm"])
                acc_ref[1:2, :] += _colsum(dsp)

    st = pl.BlockSpec((G * HS, DH), lambda i: (i, 0))
    state = pl.BlockSpec((G, DN, DH), lambda i: (i, 0, 0))
    return _call(
        body, "dn_chunks_bwd", (n // G,),
        [_rows(G * CH, 3 * DN), _rows(G * CH, LANES), _rows(G * CH, DN), _full((2, CH, CH)), _full((2, CH, CH)),
         _full((1, LANES)), _full((1, LANES))] + [st, st, state, state] * 2,
        [_rows(G * CH, 3 * DN)] * 2 + [_rows(G * CH, LANES)] * 2 + [_full((8, LANES))],
        [_S((T, 3 * DN))] * 2 + [_S((T, LANES))] * 2 + [_S((8, LANES))])(
            qkv, pab, do, csum, csum_t, alr, dtr, *fwd, *bwd)


def _head_out(o, z, g):
    on = o * lax.rsqrt(jnp.mean(o * o, axis=-1, keepdims=True) + EPS) * g
    return on * _silu(z)


def _mix_branches(of_ref, ob_ref, z_ref, yp_ref, ys_ref, pg_ref, gdn_ref, wa_ref, wb_ref, wc_ref):
    ons, ya = [], None
    for h in range(NH):
        on = _head_out(of_ref[:, _hs(h)] + ob_ref[:, _hs(h)], z_ref[:, _hs(h)], gdn_ref[...])
        t = _dot(on, wa_ref[_hs(h), :], NN)
        ya = t if ya is None else ya + t
        ons.append(on)
    ys = [ya, _dot(yp_ref[...], wb_ref[...], NN), _dot(ys_ref[...], wc_ref[...], NN)]
    sg = [jax.nn.sigmoid(pg_ref[:, k * D:(k + 1) * D]) for k in range(3)]
    return ons, ys, sg


def _mix_fwd(X, of, ob, z, yp, ys, pg, mv, gdn, wa, wb, wc, wo, tc, tt):
    T = X.shape[0]

    def body(x_ref, of_ref, ob_ref, z_ref, yp_ref, ys_ref, pg_ref, mv_ref, gdn_ref, wa_ref, wb_ref, wc_ref, wo_ref,
             x1_ref):
        _, yb, sg = _mix_branches(of_ref, ob_ref, z_ref, yp_ref, ys_ref, pg_ref, gdn_ref, wa_ref, wb_ref, wc_ref)
        mix = _dot(sg[0] * yb[0] + sg[1] * yb[1] + sg[2] * yb[2], wo_ref[...], NN)
        _, gate = _stream_rows(mv_ref, pl.program_id(0), tt, tc, 2)
        x1_ref[...] = x_ref[...] + gate * mix

    return _call(
        body, "mix_fwd", (T // tt,),
        [_rows(tt, D), _rows(tt, DN), _rows(tt, DN), _rows(tt, DN), _rows(tt, PW), _rows(tt, PW), _rows(tt, 3 * D),
         _full((8, D)), _full((1, DH)), _full(wa.shape), _full(wb.shape), _full(wc.shape), _full(wo.shape)],
        _rows(tt, D), _S((T, D)))(X, of, ob, z, yp, ys, pg, mv, gdn, wa, wb, wc, wo)


def _mix_bwd(dx1, of, ob, z, yp, ys, pg, mv, gdn, wa, wb, wc, wo, tc, tt):
    T = dx1.shape[0]

    def body(dx_ref, of_ref, ob_ref, z_ref, yp_ref, ys_ref, pg_ref, mv_ref, gdn_ref, wa_ref, wb_ref, wc_ref, wo_ref,
             do_ref, dz_ref, dyp_ref, dys_ref, dpg_ref, dwa_ref, dwb_ref, dwc_ref, dwo_ref, dgdn_ref, dm_ref):
        i = pl.program_id(0)

        @pl.when(i == 0)
        def _():
            for r in (dwa_ref, dwb_ref, dwc_ref, dwo_ref, dgdn_ref, dm_ref):
                r[...] = jnp.zeros_like(r)

        ons, yb, sg = _mix_branches(of_ref, ob_ref, z_ref, yp_ref, ys_ref, pg_ref, gdn_ref, wa_ref, wb_ref, wc_ref)
        ymix = sg[0] * yb[0] + sg[1] * yb[1] + sg[2] * yb[2]
        isc, gate = _stream_rows(mv_ref, i, tt, tc, 2)
        dx = dx_ref[...]
        dmix = dx * gate
        _acc_stream(dm_ref, 2, isc, dx * _dot(ymix, wo_ref[...], NN))
        dwo_ref[...] += _dot(ymix, dmix, TN)
        dymix = _dot(dmix, wo_ref[...], NT)
        dyb = []
        for k in range(3):
            dyb.append(dymix * sg[k])
            dpg_ref[:, k * D:(k + 1) * D] = dymix * yb[k] * sg[k] * (1.0 - sg[k])
        dwb_ref[...] += _dot(yp_ref[...], dyb[1], TN)
        dwc_ref[...] += _dot(ys_ref[...], dyb[2], TN)
        dyp_ref[...] = _dot(dyb[1], wb_ref[...], NT)
        dys_ref[...] = _dot(dyb[2], wc_ref[...], NT)
        dg = jnp.zeros((1, DH), F32)
        for h in range(NH):
            dwa_ref[_hs(h), :] += _dot(ons[h], dyb[0], TN)
            don = _dot(dyb[0], wa_ref[_hs(h), :], NT)
            _, vjp = jax.vjp(_head_out, of_ref[:, _hs(h)] + ob_ref[:, _hs(h)], z_ref[:, _hs(h)], gdn_ref[...])
            do_h, dz_h, dg_h = vjp(don)
            do_ref[:, _hs(h)] = do_h
            dz_ref[:, _hs(h)] = dz_h
            dg = dg + dg_h
        dgdn_ref[...] += dg

    return _call(
        body, "mix_bwd", (T // tt,),
        [_rows(tt, D), _rows(tt, DN), _rows(tt, DN), _rows(tt, DN), _rows(tt, PW), _rows(tt, PW), _rows(tt, 3 * D),
         _full((8, D)), _full((1, DH)), _full(wa.shape), _full(wb.shape), _full(wc.shape), _full(wo.shape)],
        [_rows(tt, DN), _rows(tt, DN), _rows(tt, PW), _rows(tt, PW), _rows(tt, 3 * D),
         _full(wa.shape), _full(wb.shape), _full(wc.shape), _full(wo.shape), _full((1, DH)), _full((8, D))],
        [_S((T, DN)), _S((T, DN)), _S((T, PW)), _S((T, PW)), _S((T, 3 * D)),
         _S(wa.shape), _S(wb.shape), _S(wc.shape), _S(wo.shape), _S((1, DH)), _S((8, D))])(
            dx1, of, ob, z, yp, ys, pg, mv, gdn, wa, wb, wc, wo)


def _ffn_fwd(X1, mv, g, wgu, wd, tc, tt):
    T = X1.shape[0]

    def body(x_ref, mv_ref, g_ref, wgu_ref, wd_ref, x2_ref):
        i = pl.program_id(0)
        _, sh = _stream_rows(mv_ref, i, tt, tc, 0)
        _, sc = _stream_rows(mv_ref, i, tt, tc, 1)
        _, gate = _stream_rows(mv_ref, i, tt, tc, 2)
        x = x_ref[...]
        gu = _dot(_modulate(x, g_ref[...], sh, sc), wgu_ref[...], NN)
        x2_ref[...] = x + gate * _dot(_silu(gu[:, :DFF]) * gu[:, DFF:], wd_ref[...], NN)

    return _call(
        body, "ffn_fwd", (T // tt,),
        [_rows(tt, D), _full((8, D)), _full((1, D)), _full(wgu.shape), _full(wd.shape)],
        _rows(tt, D), _S((T, D)))(X1, mv, g, wgu, wd)


def _ffn_bwd(X1, dx2, mv, g, wgu, wd, tc, tt):
    T = X1.shape[0]

    def body(x_ref, dx2_ref, mv_ref, g_ref, wgu_ref, wd_ref, dx1_ref, h_ref, dgu_ref, act_ref, dff_ref, dg_ref, dm_ref):
        i = pl.program_id(0)
        isc, sh = _stream_rows(mv_ref, i, tt, tc, 0)
        _, sc = _stream_rows(mv_ref, i, tt, tc, 1)
        _, gate = _stream_rows(mv_ref, i, tt, tc, 2)
        x, dx2_ = x_ref[...], dx2_ref[...]
        h, vjp = jax.vjp(_modulate, x, g_ref[...], sh, sc)
        hb = h.astype(BF16)
        h_ref[...] = hb
        gu = jnp.dot(hb, wgu_ref[...], preferred_element_type=F32)
        ga, up = gu[:, :DFF], gu[:, DFF:]
        sg = jax.nn.sigmoid(ga)
        act = (ga * sg * up).astype(BF16)
        act_ref[...] = act
        dff = dx2_ * gate
        dff_ref[...] = dff.astype(BF16)
        dact = _dot(dff, wd_ref[...], NT)
        dga = (dact * up * (sg * (1.0 + ga * (1.0 - sg)))).astype(BF16)
        dup = (dact * ga * sg).astype(BF16)
        dgu_ref[:, :DFF] = dga
        dgu_ref[:, DFF:] = dup
        dh = _dot(dga, wgu_ref[:, :DFF], NT) + _dot(dup, wgu_ref[:, DFF:], NT)
        dx, dg, dsh, dsc = vjp(dh)
        dx1_ref[...] = dx2_ + dx

        @pl.when(i == 0)
        def _():
            dg_ref[...] = jnp.zeros_like(dg_ref)
            dm_ref[...] = jnp.zeros_like(dm_ref)

        dg_ref[...] += dg
        _acc_stream(dm_ref, 0, isc, dsh)
        _acc_stream(dm_ref, 1, isc, dsc)
        _acc_stream(dm_ref, 2, isc, dx2_ * jnp.dot(act, wd_ref[...], preferred_element_type=F32))

    return _call(
        body, "ffn_bwd", (T // tt,),
        [_rows(tt, D), _rows(tt, D), _full((8, D)), _full((1, D)), _full(wgu.shape), _full(wd.shape)],
        [_rows(tt, D), _rows(tt, D), _rows(tt, 2 * DFF), _rows(tt, DFF), _rows(tt, D), _full((1, D)), _full((8, D))],
        [_S((T, D)), _S((T, D), BF16), _S((T, 2 * DFF), BF16), _S((T, DFF), BF16), _S((T, D), BF16),
         _S((1, D)), _S((8, D))])(X1, dx2, mv, g, wgu, wd)


def _rms(x, g):
    return x * lax.rsqrt(jnp.mean(x * x, axis=-1, keepdims=True) + EPS) * g


def _loss_head(X2, tgt, gf, tc):
    T = X2.shape[0]

    def body(x_ref, t_ref, g_ref, dx_ref, loss_ref, dg_ref):
        i = pl.program_id(0)

        @pl.when(i == 0)
        def _():
            dx_ref[...] = jnp.zeros_like(dx_ref)
            loss_ref[...] = jnp.zeros_like(loss_ref)
            dg_ref[...] = jnp.zeros_like(dg_ref)

        @pl.when(i > 0)
        def _():
            y, vjp = jax.vjp(_rms, x_ref[...], g_ref[...])
            err = y - t_ref[...]
            dx, dg = vjp(err * (1.0 / D))
            dx_ref[...] = dx
            dg_ref[...] += dg
            loss_ref[...] += (0.5 / D) * jnp.sum(jnp.sum(err * err, axis=1, keepdims=True), axis=0, keepdims=True)

    return _call(
        body, "loss_head", (T // tc,),
        [_rows(tc, D), pl.BlockSpec((tc, D), lambda i: (jnp.maximum(i - 1, 0), 0)), _full((1, D))],
        [_rows(tc, D), _full((8, LANES)), _full((1, D))],
        [_S((T, D)), _S((8, LANES)), _S((1, D))])(X2, tgt, gf)


def _block_diag(pw):
    g, n = pw.shape[0], pw.shape[1]
    out = jnp.zeros((g * n, g * n), pw.dtype)
    for k in range(g):
        out = lax.dynamic_update_slice(out, pw[k], (k * n, k * n))
    return out


def _split_w_in(w):
    parts = [w[:, IN_BOUNDS[k]:IN_BOUNDS[k + 1]] for k in range(8)]
    parts[2] = jnp.pad(parts[2], ((0, 0), (0, LANES - 16)))
    return parts


def _mod_rows(mods_l, k0):
    rows = [mods_l[s, (k0 + k) * D:(k0 + k + 1) * D] for s in (0, 1) for k in range(3)]
    return jnp.stack(rows + [jnp.zeros((D,), F32)] * 2)


def _lane_row(v8):
    return jnp.pad(v8.reshape(1, 8), ((0, 0), (0, LANES - 8)))


def _device_step(x, c, ctx, tgt, wts, tt):
    tc = ctx.shape[0]
    X = jnp.concatenate([ctx, x], axis=0)
    cc8 = jnp.concatenate([wts["c_ctx"][None, :], c, jnp.zeros((6, D), F32)], axis=0)
    w_ada = wts["w_ada"].astype(BF16)
    mods = _mod_fwd(cc8, w_ada, wts["b_ada"].reshape(NL, 1, 6 * D))

    saved = []
    for l in range(NL):
        ws = [w.astype(BF16) for w in _split_w_in(wts["w_in"][l])]
        wbr = [wts[k][l].astype(BF16) for k in ("w_br_a", "w_br_b", "w_br_c", "w_o", "w_gu", "w_down")]
        mv1, mv2 = _mod_rows(mods[l], 0), _mod_rows(mods[l], 3)
        g1, g2 = wts["norm1_g"][l][None, :], wts["norm2_g"][l][None, :]
        cw, scw = wts["dn_conv_w"][l], wts["sc_conv_w"][l]
        alr, dtr = _lane_row(wts["dn_a_log"][l]), _lane_row(wts["dn_dt_bias"][l])
        gdn = wts["dn_norm_g"][l][None, :]
        pwbd, ps = _block_diag(wts["pool_w"][l]), wts["pool_scale"][l][None, :]
        hb, pq, pz, pab, pp, sx, sb, sc_, pg = _inproj_fwd(X, mv1, g1, ws, tc, tt)
        qkv = _dnprep_fwd(pq, cw, tc, tt)
        parts = _dn_chunks_fwd(qkv, pab, alr, dtr)
        (of, ssf, vnf), (ob, ssb, vnb) = _scan_fwd(parts, X.shape[0], tc)
        yp = _pool_fwd(pp, pwbd, ps, tc)
        ys = _sc_fwd(sx, sb, sc_, scw, tc, tt)
        X1 = _mix_fwd(X, of, ob, pz, yp, ys, pg, mv1, gdn, *wbr[:4], tc, tt)
        X2 = _ffn_fwd(X1, mv2, g2, wbr[4], wbr[5], tc, tt)
        saved.append(dict(X=X, X1=X1, ws=ws, wbr=wbr, mv1=mv1, mv2=mv2, g1=g1, g2=g2, cw=cw, scw=scw, alr=alr, dtr=dtr,
                          gdn=gdn, pwbd=pwbd, ps=ps, hb=hb, pq=pq, pz=pz, pab=pab, pp=pp, sx=sx, sb=sb, sc=sc_, pg=pg,
                          qkv=qkv, of=of, ob=ob, ssf=ssf, ssb=ssb, vnf=vnf, vnb=vnb, parts=parts, yp=yp, ys=ys))
        X = X2

    dX, loss, dgf = _loss_head(X, tgt, wts["final_norm_g"][None, :], tc)

    gl = {k: [None] * NL for k in ("w_in", "norm1_g", "norm2_g", "dn_conv_w", "dn_a_log", "dn_dt_bias", "dn_norm_g",
                                   "pool_w", "pool_scale", "sc_conv_w", "w_br_a", "w_br_b", "w_br_c", "w_o", "w_gu",
                                   "w_down")}
    dmods = [None] * NL
    for l in reversed(range(NL)):
        s = saved[l]
        dx1, h2, dgu, act, dff, dg2, dm2 = _ffn_bwd(s["X1"], dX, s["mv2"], s["g2"], s["wbr"][4], s["wbr"][5], tc, tt)
        gl["w_gu"][l] = _dw(h2, dgu, tt)
        gl["w_down"][l] = _dw(act, dff, tt)
        do, dz, dyp, dys, dpg, dwa, dwb, dwc, dwo, dgdn, dmg = _mix_bwd(
            dx1, s["of"], s["ob"], s["pz"], s["yp"], s["ys"], s["pg"], s["mv1"], s["gdn"], *s["wbr"][:4], tc, tt)
        dpp, dpw, dps = _pool_bwd(s["pp"], s["pwbd"], s["ps"], dyp, tc)
        dsx, dsb, dsc, dscw = _sc_bwd(s["sx"], s["sb"], s["sc"], s["scw"], dys, tc, tt)
        (dvnf, dssf), (dvnb, dssb) = _scan_bwd(do, s["parts"], tc)
        dqf, dqb, dpf, dpb, gacc = _dn_chunks_bwd(s["qkv"], s["pab"], s["alr"], s["dtr"], do,
                                                  (s["vnf"], dvnf, s["ssf"], dssf), (s["vnb"], dvnb, s["ssb"], dssb))
        dy = _dnprep_bwd_act(s["pq"], s["cw"], dqf, dqb, tc, tt)
        dpq, dcw = _conv_bwd(dy, s["pq"], s["cw"], tc, tt)
        dps_ = [dpq, dz, dpf, dpb, dpp, dsx, dsb, dsc, dpg]
        dp_w = [0, 1, 2, 2, 3, 4, 5, 6, 7]
        dX, dg1, dm1 = _inproj_bwd(s["X"], s["mv1"], s["g1"], s["ws"], dps_, dp_w, dx1, tc, tt)
        dws = [_dw(s["hb"], dpq, tt), _dw(s["hb"], dz, tt), _dw(s["hb"], dpf + dpb, tt)[:, :16], _dw(s["hb"], dpp, tt),
               _dw(s["hb"], dsx, tt), _dw(s["hb"], dsb, tt), _dw(s["hb"], dsc, tt), _dw(s["hb"], dpg, tt)]
        gl["w_in"][l] = jnp.concatenate(dws, axis=1)
        gl["norm1_g"][l], gl["norm2_g"][l] = dg1[0], dg2[0]
        gl["dn_conv_w"][l], gl["sc_conv_w"][l] = dcw, dscw
        gl["dn_a_log"][l], gl["dn_dt_bias"][l] = gacc[0, :8].reshape(2, NH), gacc[1, :8].reshape(2, NH)
        gl["dn_norm_g"][l] = dgdn[0]
        gl["pool_w"][l] = jnp.stack([dpw[k * GW:(k + 1) * GW, k * GW:(k + 1) * GW] for k in range(4)])
        gl["pool_scale"][l] = dps[0]
        gl["w_br_a"][l], gl["w_br_b"][l], gl["w_br_c"][l], gl["w_o"][l] = dwa, dwb, dwc, dwo
        dm = dm1 + dmg
        row = lambda r: jnp.concatenate([dm[r], dm[r + 1], dm[r + 2], dm2[r], dm2[r + 1], dm2[r + 2]])
        dmods[l] = jnp.stack([row(0), row(3)] + [jnp.zeros((6 * D,), F32)] * 6)

    dwada, dbada, dcc = _mod_bwd(cc8, w_ada, jnp.stack(dmods))
    grads = {k: jnp.stack(v) for k, v in gl.items()}
    grads.update(w_ada=dwada, b_ada=dbada.reshape(NL, 6 * D), c_ctx=dcc[0], final_norm_g=dgf[0])
    return loss, dX[tc:], grads


MESH_ID = pl.DeviceIdType.MESH
HBM_SPEC = pl.BlockSpec(memory_space=pltpu.HBM)


def _me():
    return lax.axis_index("x"), lax.axis_index("y"), lax.axis_index("c")


def _dev_index(p):
    return 4 * p[0] + 2 * p[1] + p[2]


def _allgather(parts):
    n = len(parts)

    def body(*refs):
        ins, outs = refs[:n], refs[n:2 * n]
        send_sems, recv_sems, local_sems = refs[2 * n:]
        x, y, c = _me()
        me, sibling = (x, y, c), (x, y, 1 - c)
        chips = [(1 - x, y), (x, 1 - y), (1 - x, 1 - y)]

        def copy(a, k, block, to, src=None):
            dst = outs[a].at[_dev_index(block)]
            return pltpu.make_async_remote_copy(
                src_ref=dst if src is None else src, dst_ref=dst, send_sem=send_sems.at[a, k], recv_sem=recv_sems.at[a, k],
                device_id=to, device_id_type=MESH_ID)

        mine, first, passed = [], [], []
        for a in range(n):
            mine.append(pltpu.make_async_copy(ins[a], outs[a].at[_dev_index(me)], local_sems.at[a]))
            mine[-1].start()
            first.append(copy(a, 0, me, sibling, src=ins[a]))
            first += [copy(a, 1 + j, me, (*chip, c), src=ins[a]) for j, chip in enumerate(chips)]
        for cp in first:
            cp.start()
        for a in range(n):
            for j, chip in enumerate(chips):
                copy(a, 1 + j, (*chip, c), me).wait_recv()
                passed.append(copy(a, 4 + j, (*chip, c), sibling))
                passed[-1].start()
        for a in range(n):
            copy(a, 0, sibling, me).wait_recv()
            for j, chip in enumerate(chips):
                copy(a, 4 + j, (*chip, 1 - c), me).wait_recv()
        for cp in first + passed:
            cp.wait_send()
        for cp in mine:
            cp.wait()

    return pl.pallas_call(
        body, name="allgather", in_specs=[HBM_SPEC] * n, out_specs=[HBM_SPEC] * n,
        out_shape=[_S((N_DEV,) + p.shape, p.dtype) for p in parts],
        scratch_shapes=[pltpu.SemaphoreType.DMA((n, 7)), pltpu.SemaphoreType.DMA((n, 7)), pltpu.SemaphoreType.DMA((n,))],
    )(*parts)


def _exchange(blocked, small):
    arrs = list(blocked) + [small]
    n = len(arrs)

    def body(*refs):
        ins, outs = refs[:n], refs[n:2 * n]
        send_sems, recv_sems, local_sems = refs[2 * n:]
        x, y, c = _me()
        my = _dev_index((x, y, c))

        def src(a, idx):
            return ins[a] if a == n - 1 else ins[a].at[idx]

        copies = []
        for a in range(n):
            mine = pltpu.make_async_copy(src(a, my), outs[a].at[my], local_sems.at[a])
            mine.start()
            copies.append(mine)
        remote = []
        for k in range(1, N_DEV):
            peer = (x ^ (k >> 2), y ^ ((k >> 1) & 1), c ^ (k & 1))
            for a in range(n):
                cp = pltpu.make_async_remote_copy(
                    src_ref=src(a, _dev_index(peer)), dst_ref=outs[a].at[my], send_sem=send_sems.at[a, k - 1],
                    recv_sem=recv_sems.at[a, k - 1], device_id=peer, device_id_type=MESH_ID)
                cp.start()
                remote.append(cp)
        for cp in remote:
            cp.wait_recv()
        for cp in remote:
            cp.wait_send()
        for cp in copies:
            cp.wait()

    outs = pl.pallas_call(
        body, name="grad_exchange", in_specs=[HBM_SPEC] * n, out_specs=[HBM_SPEC] * n,
        out_shape=[_S(a.shape, a.dtype) for a in blocked] + [_S((N_DEV,) + small.shape, small.dtype)],
        scratch_shapes=[pltpu.SemaphoreType.DMA((n, 7)), pltpu.SemaphoreType.DMA((n, 7)), pltpu.SemaphoreType.DMA((n,))],
    )(*arrs)
    return outs[:-1], outs[-1]


def _adam(w, g, m, v):
    m2 = ADAM_B1 * m + (1.0 - ADAM_B1) * g
    v2 = ADAM_B2 * v + (1.0 - ADAM_B2) * (g * g)
    m_hat = m2 / (1.0 - ADAM_B1 ** ADAM_STEP)
    v_hat = v2 / (1.0 - ADAM_B2 ** ADAM_STEP)
    return -ADAM_LR * (m_hat / (jnp.sqrt(v_hat) + ADAM_EPS) + ADAM_WD * w), m2, v2


def _sum_adam(recv, w, m, v):
    L, R, C = w.shape
    tr = 256 if R % 256 == 0 else R

    def body(r_ref, w_ref, m_ref, v_ref, g_ref, d_ref, m2_ref, v2_ref):
        g = r_ref[0, 0]
        for k in range(1, N_DEV):
            g = g + r_ref[k, 0]
        g_ref[0] = g
        d_ref[0], m2_ref[0], v2_ref[0] = _adam(w_ref[0], g, m_ref[0], v_ref[0])

    blk = pl.BlockSpec((1, tr, C), lambda l, i: (l, i, 0))
    return _call(
        body, "sum_adam", (L, R // tr),
        [pl.BlockSpec((N_DEV, 1, tr, C), lambda l, i: (0, l, i, 0)), blk, blk, blk],
        [blk] * 4, [_S(w.shape)] * 4)(recv, w, m, v)


def _sum_small(recv):
    def body(r_ref, o_ref):
        g = r_ref[0]
        for k in range(1, N_DEV):
            g = g + r_ref[k]
        o_ref[...] = g

    return pl.pallas_call(body, name="sum_small", out_shape=_S(recv.shape[1:]))(recv)


def _adam_small(w, g, m, v):
    def body(w_ref, g_ref, m_ref, v_ref, d_ref, m2_ref, v2_ref):
        d_ref[...], m2_ref[...], v2_ref[...] = _adam(w_ref[...], g_ref[...], m_ref[...], v_ref[...])

    return pl.pallas_call(body, name="adam_small", out_shape=[_S(w.shape)] * 3)(w, g, m, v)


def _pack(arrs, dtype, row_mult):
    parts, offs, r = [], [], 0
    for a in arrs:
        nr = -(-a.size // LANES)
        parts.append(jnp.pad(a.reshape(-1).astype(dtype), (0, nr * LANES - a.size)))
        offs.append(r)
        r += nr
    pad = (-r) % row_mult
    if pad:
        parts.append(jnp.zeros((pad * LANES,), dtype))
    return jnp.concatenate(parts).reshape(r + pad, LANES), offs


def _unpack(packed, offs, shapes, lead=()):
    out = []
    for off, shp in zip(offs, shapes):
        size = int(np.prod(shp))
        nr = -(-size // LANES)
        flat = packed[..., off:off + nr, :].reshape(lead + (nr * LANES,))
        out.append(flat[..., :size].reshape(lead + tuple(shp)))
    return out


BIG = (("w_ada", 2), ("w_in", 2), ("w_br_a", 2), ("w_br_b", 2), ("w_br_c", 2), ("w_o", 1), ("w_gu", 2), ("w_down", 1))
CONV = ("dn_conv_w", "sc_conv_w")
REPL = ("c_ctx", "b_ada", "norm1_g", "norm2_g", "dn_a_log", "dn_dt_bias", "dn_norm_g", "pool_w", "pool_scale",
        "final_norm_g")
WEIGHTS = ("c_ctx", "w_ada", "b_ada", "norm1_g", "norm2_g", "w_in", "dn_conv_w", "dn_a_log", "dn_dt_bias", "dn_norm_g",
           "pool_w", "pool_scale", "sc_conv_w", "w_br_a", "w_br_b", "w_br_c", "w_o", "w_gu", "w_down", "final_norm_g")
TOKEN_TILE = 256


def _join(blocks, axis):
    nd, nl, r, c = blocks.shape
    if axis == 2:
        return blocks.transpose(1, 2, 0, 3).reshape(nl, r, nd * c)
    return blocks.transpose(1, 0, 2, 3).reshape(nl, nd * r, c)


def _split(full, axis):
    nl, r, c = full.shape
    if axis == 2:
        return full.reshape(nl, r, N_DEV, c // N_DEV).transpose(2, 0, 1, 3)
    return full.reshape(nl, N_DEV, r // N_DEV, c).transpose(1, 0, 2, 3)


def kernel(x, c, ctx, c_ctx, w_ada, b_ada, norm1_g, norm2_g, w_in, dn_conv_w, dn_a_log, dn_dt_bias, dn_norm_g, pool_w, pool_scale, sc_conv_w, w_br_a, w_br_b, w_br_c, w_o, w_gu, w_down, final_norm_g, loss_target, m_c_ctx, m_w_ada, m_b_ada, m_norm1_g, m_norm2_g, m_w_in, m_dn_conv_w, m_dn_a_log, m_dn_dt_bias, m_dn_norm_g, m_pool_w, m_pool_scale, m_sc_conv_w, m_w_br_a, m_w_br_b, m_w_br_c, m_w_o, m_w_gu, m_w_down, m_final_norm_g, v_c_ctx, v_w_ada, v_b_ada, v_norm1_g, v_norm2_g, v_w_in, v_dn_conv_w, v_dn_a_log, v_dn_dt_bias, v_dn_norm_g, v_pool_w, v_pool_scale, v_sc_conv_w, v_w_br_a, v_w_br_b, v_w_br_c, v_w_o, v_w_gu, v_w_down, v_final_norm_g):
    loc = dict(c_ctx=c_ctx, w_ada=w_ada, b_ada=b_ada, norm1_g=norm1_g, norm2_g=norm2_g, w_in=w_in, dn_conv_w=dn_conv_w,
               dn_a_log=dn_a_log, dn_dt_bias=dn_dt_bias, dn_norm_g=dn_norm_g, pool_w=pool_w, pool_scale=pool_scale,
               sc_conv_w=sc_conv_w, w_br_a=w_br_a, w_br_b=w_br_b, w_br_c=w_br_c, w_o=w_o, w_gu=w_gu, w_down=w_down,
               final_norm_g=final_norm_g)
    mom_m = dict(c_ctx=m_c_ctx, w_ada=m_w_ada, b_ada=m_b_ada, norm1_g=m_norm1_g, norm2_g=m_norm2_g, w_in=m_w_in,
                 dn_conv_w=m_dn_conv_w, dn_a_log=m_dn_a_log, dn_dt_bias=m_dn_dt_bias, dn_norm_g=m_dn_norm_g,
                 pool_w=m_pool_w, pool_scale=m_pool_scale, sc_conv_w=m_sc_conv_w, w_br_a=m_w_br_a, w_br_b=m_w_br_b,
                 w_br_c=m_w_br_c, w_o=m_w_o, w_gu=m_w_gu, w_down=m_w_down, final_norm_g=m_final_norm_g)
    mom_v = dict(c_ctx=v_c_ctx, w_ada=v_w_ada, b_ada=v_b_ada, norm1_g=v_norm1_g, norm2_g=v_norm2_g, w_in=v_w_in,
                 dn_conv_w=v_dn_conv_w, dn_a_log=v_dn_a_log, dn_dt_bias=v_dn_dt_bias, dn_norm_g=v_dn_norm_g,
                 pool_w=v_pool_w, pool_scale=v_pool_scale, sc_conv_w=v_sc_conv_w, w_br_a=v_w_br_a, w_br_b=v_w_br_b,
                 w_br_c=v_w_br_c, w_o=v_w_o, w_gu=v_w_gu, w_down=v_w_down, final_norm_g=v_final_norm_g)
    my = _dev_index(_me())

    big_pack, big_offs = _pack([loc[k] for k, _ in BIG], BF16, BF16_ROWS)
    conv_pack, conv_offs = _pack([loc[k] for k in CONV], F32, 8)
    big_all, conv_all = _allgather([big_pack, conv_pack])
    full = {k: loc[k] for k in REPL}
    for (k, axis), blocks in zip(BIG, _unpack(big_all, big_offs, [loc[k].shape for k, _ in BIG], (N_DEV,))):
        full[k] = _join(blocks, axis)
    for k, blocks in zip(CONV, _unpack(conv_all, conv_offs, [loc[k].shape for k in CONV], (N_DEV,))):
        full[k] = _join(blocks, 2)

    loss8, grad_x, g = _device_step(x[0], c, ctx[0], loss_target[0], full, TOKEN_TILE)

    small_names = REPL + CONV
    small_pack, small_offs = _pack([g[k] for k in small_names] + [loss8[0:1, 0:1]], F32, 8)
    recv_big, recv_small = _exchange([_split(g[k], axis) for k, axis in BIG], small_pack)
    small_sum = _sum_small(recv_small)
    sums = _unpack(small_sum, small_offs, [g[k].shape for k in small_names] + [(1, 1)])
    grads = dict(zip(small_names, sums[:-1]))
    loss = sums[-1][0, 0]
    for k in CONV:
        w = loc[k].shape[2]
        grads[k] = lax.dynamic_slice_in_dim(grads[k], my * w, w, axis=2)

    delta, new_m, new_v = {}, {}, {}
    for (k, _), recv in zip(BIG, recv_big):
        grads[k], delta[k], new_m[k], new_v[k] = _sum_adam(recv, loc[k], mom_m[k], mom_v[k])
    packs = [_pack([src[k] for k in small_names], F32, 8)[0] for src in (loc, grads, mom_m, mom_v)]
    _, offs = _pack([loc[k] for k in small_names], F32, 8)
    shapes = [loc[k].shape for k in small_names]
    for dst, packed in zip((delta, new_m, new_v), _adam_small(*packs)):
        dst.update(zip(small_names, _unpack(packed, offs, shapes)))

    return (loss, grad_x[None], *[grads[k] for k in WEIGHTS], *[delta[k] for k in WEIGHTS],
            *[new_m[k] for k in WEIGHTS], *[new_v[k] for k in WEIGHTS])
```

```python
import functools

import numpy as np
import jax
import jax.numpy as jnp
from jax import lax
from jax.experimental import pallas as pl
from jax.experimental.pallas import tpu as pltpu

F32 = jnp.float32
BF16 = jnp.bfloat16
HI = lax.Precision.HIGHEST

D = 1024
NL = 2
NH = 4
DH = 128
DN = NH * DH
CH = 64
GW = 64
PW = 256
DFF = 2816
EPS = 1e-6
N_DEV = 8
LANES = 128
BF16_ROWS = 16
VMEM_MB = 56

ADAM_LR, ADAM_B1, ADAM_B2, ADAM_EPS, ADAM_WD, ADAM_STEP = 0.001, 0.9, 0.999, 1e-08, 0.01, 10

IN_BOUNDS = (0, 1536, 2048, 2064, 2320, 2576, 2832, 3088, 6160)
IN_WIDTHS = (1536, 512, 128, 256, 256, 256, 256, 3072)
POOL_WIN = ((1, 0), (2, 1), (4, 3), (8, 7))

NN = ((1,), (0,))
NT = ((1,), (1,))
TN = ((0,), (0,))


def _dot(a, b, dims, hi=False):
    if hi:
        prec = lax.Precision.HIGH if hi == "x3" else HI
        return lax.dot_general(a, b, (dims, ((), ())), precision=prec, preferred_element_type=F32)
    return lax.dot_general(a.astype(BF16), b.astype(BF16), (dims, ((), ())), preferred_element_type=F32)


def _S(shape, dtype=F32):
    return jax.ShapeDtypeStruct(tuple(shape), dtype)


def _full(shape):
    nd = len(shape)
    return pl.BlockSpec(tuple(shape), lambda *_: (0,) * nd)


def _rows(tt, w):
    return pl.BlockSpec((tt, w), lambda i: (i, 0))


def _call(body, name, grid, in_specs, out_specs, out_shape, scratch=()):
    return pl.pallas_call(
        body, name=name, grid=grid, in_specs=in_specs, out_specs=out_specs, out_shape=out_shape,
        scratch_shapes=list(scratch),
        compiler_params=pltpu.CompilerParams(
            dimension_semantics=("arbitrary",) * len(grid), vmem_limit_bytes=VMEM_MB << 20),
    )


def _iota(shape, axis):
    return lax.broadcasted_iota(jnp.int32, shape, axis)


def _colsum(a):
    return jnp.sum(a, axis=0, keepdims=True)


def _silu(x):
    return x * jax.nn.sigmoid(x)


def _modulate(x, g, sh, sc):
    xn = x * lax.rsqrt(jnp.mean(x * x, axis=-1, keepdims=True) + EPS)
    return (xn * g) * (1.0 + sc) + sh


def _stream_rows(mv_ref, i, tt, tc, k):
    isc = (i * tt + _iota((tt, 1), 0)) < tc
    return isc, jnp.where(isc, mv_ref[k:k + 1, :], mv_ref[3 + k:4 + k, :])


def _acc_stream(ref, k, isc, val):
    ref[k:k + 1, :] += _colsum(jnp.where(isc, val, 0.0))
    ref[3 + k:4 + k, :] += _colsum(jnp.where(isc, 0.0, val))


MOD_CT = 1536


def _mod_fwd(cc8, w_ada, b_ada3):
    def body(cc_ref, w_ref, b_ref, o_ref):
        o_ref[0] = _dot(_silu(cc_ref[...]), w_ref[0], NN) + b_ref[0]

    return _call(
        body, "mod_fwd", (NL, 6 * D // MOD_CT),
        [pl.BlockSpec((8, D), lambda l, j: (0, 0)), pl.BlockSpec((1, D, MOD_CT), lambda l, j: (l, 0, j)),
         pl.BlockSpec((1, 1, MOD_CT), lambda l, j: (l, 0, j))],
        pl.BlockSpec((1, 8, MOD_CT), lambda l, j: (l, 0, j)), _S((NL, 8, 6 * D)))(cc8, w_ada, b_ada3)


def _mod_bwd(cc8, w_ada, dmods):
    def body(cc_ref, w_ref, dm_ref, dw_ref, db_ref, dcc_ref):
        first = (pl.program_id(0) == 0) & (pl.program_id(1) == 0)
        cc = cc_ref[...]
        sg = jax.nn.sigmoid(cc)
        dm = dm_ref[0]
        dw_ref[0] = _dot(cc * sg, dm, TN)
        db_ref[0] = dm[0:1, :] + dm[1:2, :]

        @pl.when(first)
        def _():
            dcc_ref[...] = jnp.zeros_like(dcc_ref)

        dcc_ref[...] += _dot(dm, w_ref[0], NT) * (sg * (1.0 + cc * (1.0 - sg)))

    return _call(
        body, "mod_bwd", (NL, 6 * D // MOD_CT),
        [pl.BlockSpec((8, D), lambda l, j: (0, 0)), pl.BlockSpec((1, D, MOD_CT), lambda l, j: (l, 0, j)),
         pl.BlockSpec((1, 8, MOD_CT), lambda l, j: (l, 0, j))],
        [pl.BlockSpec((1, D, MOD_CT), lambda l, j: (l, 0, j)), pl.BlockSpec((1, 1, MOD_CT), lambda l, j: (l, 0, j)),
         pl.BlockSpec((8, D), lambda l, j: (0, 0))],
        [_S((NL, D, 6 * D)), _S((NL, 1, 6 * D)), _S((8, D))])(cc8, w_ada, dmods)


def _inproj_fwd(X, mv, g, ws, tc, tt):
    T = X.shape[0]
    nw = len(ws)

    def body(x_ref, mv_ref, g_ref, *refs):
        w_refs, h_ref, p_refs = refs[:nw], refs[nw], refs[nw + 1:]
        i = pl.program_id(0)
        _, sh = _stream_rows(mv_ref, i, tt, tc, 0)
        _, sc = _stream_rows(mv_ref, i, tt, tc, 1)
        hb = _modulate(x_ref[...], g_ref[...], sh, sc).astype(BF16)
        h_ref[...] = hb
        for w_ref, p_ref in zip(w_refs, p_refs):
            p_ref[...] = jnp.dot(hb, w_ref[...], preferred_element_type=F32)

    return _call(
        body, "inproj_fwd", (T // tt,),
        [_rows(tt, D), _full((8, D)), _full((1, D))] + [_full(w.shape) for w in ws],
        [_rows(tt, D)] + [_rows(tt, w.shape[1]) for w in ws],
        [_S((T, D), BF16)] + [_S((T, w.shape[1])) for w in ws])(X, mv, g, *ws)


def _inproj_bwd(X, mv, g, ws, dps, dp_w, dres, tc, tt):
    T = X.shape[0]
    nw, nd = len(ws), len(dps)

    def body(x_ref, mv_ref, g_ref, dres_ref, *refs):
        w_refs, dp_refs = refs[:nw], refs[nw:nw + nd]
        dx_ref, dg_ref, dm_ref = refs[nw + nd:]
        i = pl.program_id(0)
        isc, sh = _stream_rows(mv_ref, i, tt, tc, 0)
        _, sc = _stream_rows(mv_ref, i, tt, tc, 1)
        dh = None
        for dp_ref, k in zip(dp_refs, dp_w):
            t = _dot(dp_ref[...], w_refs[k][...], NT)
            dh = t if dh is None else dh + t
        _, vjp = jax.vjp(_modulate, x_ref[...], g_ref[...], sh, sc)
        dx, dg, dsh, dsc = vjp(dh)
        dx_ref[...] = dres_ref[...] + dx

        @pl.when(i == 0)
        def _():
            dg_ref[...] = jnp.zeros_like(dg_ref)
            dm_ref[...] = jnp.zeros_like(dm_ref)

        dg_ref[...] += dg
        _acc_stream(dm_ref, 0, isc, dsh)
        _acc_stream(dm_ref, 1, isc, dsc)

    return _call(
        body, "inproj_bwd", (T // tt,),
        [_rows(tt, D), _full((8, D)), _full((1, D)), _rows(tt, D)] + [_full(w.shape) for w in ws]
        + [_rows(tt, dp.shape[1]) for dp in dps],
        [_rows(tt, D), _full((1, D)), _full((8, D))],
        [_S((T, D)), _S((1, D)), _S((8, D))])(X, mv, g, dres, *ws, *dps)


def _dw(A, B, tt):
    T, K = A.shape
    N = B.shape[1]
    tn = 512 if N % 512 == 0 else (256 if N % 256 == 0 else LANES)

    def body(a_ref, b_ref, o_ref):
        @pl.when(pl.program_id(1) == 0)
        def _():
            o_ref[...] = jnp.zeros_like(o_ref)

        o_ref[...] += _dot(a_ref[...], b_ref[...], TN)

    return _call(
        body, "dw", (N // tn, T // tt),
        [pl.BlockSpec((tt, K), lambda j, i: (i, 0)), pl.BlockSpec((tt, tn), lambda j, i: (i, j))],
        pl.BlockSpec((K, tn), lambda j, i: (0, j)), _S((K, N)))(A, B)


def _halo_specs(T, tt, cw, col):
    r8, nb8 = tt // 8, T // 8
    return [pl.BlockSpec((tt, cw), lambda j, i: (i, col(j))),
            pl.BlockSpec((8, cw), lambda j, i: (jnp.maximum(i * r8 - 1, 0), col(j))),
            pl.BlockSpec((8, cw), lambda j, i: (jnp.minimum((i + 1) * r8, nb8 - 1), col(j)))]


def _shifts(a, prev8, next8, i, tt, tc, T):
    r = _iota((tt, 1), 0)
    t = i * tt + r
    dn = jnp.where(r == 0, prev8[7:8, :], pltpu.roll(a, 1, 0))
    dn = jnp.where((t == 0) | (t == tc), 0.0, dn)
    up = jnp.where(r == tt - 1, next8[0:1, :], pltpu.roll(a, tt - 1, 0))
    up = jnp.where((t == T - 1) | (t == tc - 1), 0.0, up)
    return dn, up


def _dn_post(y, part):
    a = _silu(y)
    nrm = lax.rsqrt(jnp.sum(a * a, axis=-1, keepdims=True) + EPS)
    f = jnp.where(part == 0, nrm * (DH ** -0.5), jnp.where(part == 1, nrm, 1.0))
    return a * f


def _conv3(w_ref, dn, mid, up):
    return w_ref[0:1, :] * dn + w_ref[1:2, :] * mid + w_ref[2:3, :] * up


def _dnprep_fwd(pq, cw, tc, tt):
    T = pq.shape[0]

    def body(p_ref, pp_ref, pn_ref, w_ref, a_ref):
        part, i = pl.program_id(0), pl.program_id(1)
        p = p_ref[...]
        dn, up = _shifts(p, pp_ref[...], pn_ref[...], i, tt, tc, T)
        y = _conv3(w_ref, dn, p, up)
        for h in range(NH):
            a_ref[:, _hs(h)] = _dn_post(y[:, _hs(h)], part)

    return _call(
        body, "dnprep_fwd", (3, T // tt),
        _halo_specs(T, tt, DN, lambda j: j) + [pl.BlockSpec((3, DN), lambda j, i: (0, j))],
        pl.BlockSpec((tt, DN), lambda j, i: (i, j)), _S((T, 3 * DN)))(pq, pq, pq, cw)


def _dnprep_bwd_act(pq, cw, da_f, da_b, tc, tt):
    T = pq.shape[0]

    def body(p_ref, pp_ref, pn_ref, w_ref, df_ref, db_ref, dy_ref):
        part, i = pl.program_id(0), pl.program_id(1)
        p = p_ref[...]
        dn, up = _shifts(p, pp_ref[...], pn_ref[...], i, tt, tc, T)
        y = _conv3(w_ref, dn, p, up)
        for h in range(NH):
            _, vjp = jax.vjp(lambda yh: _dn_post(yh, part), y[:, _hs(h)])
            dy_ref[:, _hs(h)] = vjp(df_ref[:, _hs(h)] + db_ref[:, _hs(h)])[0]

    blk = pl.BlockSpec((tt, DN), lambda j, i: (i, j))
    return _call(
        body, "dnprep_bwd_act", (3, T // tt),
        _halo_specs(T, tt, DN, lambda j: j) + [pl.BlockSpec((3, DN), lambda j, i: (0, j)), blk, blk],
        blk, _S((T, 3 * DN)))(pq, pq, pq, cw, da_f, da_b)


def _conv_bwd(dy, p, cw, tc, tt):
    T, W = p.shape
    cb = DN

    def body(dy_ref, dyp_ref, dyn_ref, p_ref, pp_ref, pn_ref, w_ref, dp_ref, dw_ref):
        i = pl.program_id(1)
        dy, p_ = dy_ref[...], p_ref[...]
        ddn, dup = _shifts(dy, dyp_ref[...], dyn_ref[...], i, tt, tc, T)
        dp_ref[...] = _conv3(w_ref, dup, dy, ddn)
        pdn, pup = _shifts(p_, pp_ref[...], pn_ref[...], i, tt, tc, T)

        @pl.when(i == 0)
        def _():
            dw_ref[...] = jnp.zeros_like(dw_ref)

        dw_ref[0:1, :] += _colsum(dy * pdn)
        dw_ref[1:2, :] += _colsum(dy * p_)
        dw_ref[2:3, :] += _colsum(dy * pup)

    wspec = pl.BlockSpec((3, cb), lambda j, i: (0, j))
    return _call(
        body, "conv_bwd", (W // cb, T // tt),
        _halo_specs(T, tt, cb, lambda j: j) * 2 + [wspec],
        [pl.BlockSpec((tt, cb), lambda j, i: (i, j)), wspec], [_S((T, W)), _S((3, W))])(dy, dy, dy, p, p, p, cw)


def _sc_fwd(sx, sb, sc_, cw, tc, tt):
    T = sx.shape[0]

    def body(x_ref, xp_ref, xn_ref, c_ref, cp_ref, cn_ref, b_ref, w_ref, y_ref):
        i = pl.program_id(1)
        u = c_ref[...] * x_ref[...]
        dn, up = _shifts(u, cp_ref[...] * xp_ref[...], cn_ref[...] * xn_ref[...], i, tt, tc, T)
        y_ref[...] = b_ref[...] * _conv3(w_ref, dn, u, up)

    blk = pl.BlockSpec((tt, LANES), lambda j, i: (i, j))
    return _call(
        body, "sc_fwd", (PW // LANES, T // tt),
        _halo_specs(T, tt, LANES, lambda j: j) * 2 + [blk, pl.BlockSpec((3, LANES), lambda j, i: (0, j))],
        blk, _S((T, PW)))(sx, sx, sx, sc_, sc_, sc_, sb, cw)


def _sc_bwd(sx, sb, sc_, cw, dy, tc, tt):
    T = sx.shape[0]

    def body(x_ref, xp_ref, xn_ref, c_ref, cp_ref, cn_ref, b_ref, bp_ref, bn_ref, dy_ref, dyp_ref, dyn_ref, w_ref,
             dx_ref, db_ref, dc_ref, dw_ref):
        i = pl.program_id(1)
        x, c, dy_ = x_ref[...], c_ref[...], dy_ref[...]
        u = c * x
        udn, uup = _shifts(u, cp_ref[...] * xp_ref[...], cn_ref[...] * xn_ref[...], i, tt, tc, T)
        db_ref[...] = dy_ * _conv3(w_ref, udn, u, uup)
        e = dy_ * b_ref[...]
        edn, eup = _shifts(e, dyp_ref[...] * bp_ref[...], dyn_ref[...] * bn_ref[...], i, tt, tc, T)
        du = _conv3(w_ref, eup, e, edn)
        dx_ref[...] = du * c
        dc_ref[...] = du * x

        @pl.when(i == 0)
        def _():
            dw_ref[...] = jnp.zeros_like(dw_ref)

        dw_ref[0:1, :] += _colsum(e * udn)
        dw_ref[1:2, :] += _colsum(e * u)
        dw_ref[2:3, :] += _colsum(e * uup)

    blk = pl.BlockSpec((tt, LANES), lambda j, i: (i, j))
    wspec = pl.BlockSpec((3, LANES), lambda j, i: (0, j))
    return _call(
        body, "sc_bwd", (PW // LANES, T // tt),
        _halo_specs(T, tt, LANES, lambda j: j) * 4 + [wspec],
        [blk, blk, blk, wspec], [_S((T, PW))] * 3 + [_S((3, PW))])(
            sx, sx, sx, sc_, sc_, sc_, sb, sb, sb, dy, dy, dy, cw)


def _group_select(vals):
    g = _iota((1, PW), 1) // (PW // len(POOL_WIN))
    return jnp.where(g == 0, vals[0], jnp.where(g == 1, vals[1], jnp.where(g == 2, vals[2], vals[3])))


def _nested_box(get, mirror):
    acc, outs, pl_, ph_ = get(0), [], 0, 0
    for lo, hi in POOL_WIN:
        if mirror:
            lo, hi = hi, lo
        for k in range(pl_ + 1, lo + 1):
            acc = acc + get(-k)
        for k in range(ph_ + 1, hi + 1):
            acc = acc + get(k)
        pl_, ph_ = lo, hi
        outs.append(acc)
    return _group_select(outs)


def _box_tokens(a, n, mirror):
    idx = _iota((n, 1), 0)

    def get(k):
        if k == 0:
            return a
        return jnp.where((idx + k >= 0) & (idx + k < n), pltpu.roll(a, (-k) % n, 0), 0.0)

    return _nested_box(get, mirror)


def _inv_count(pos, n):
    return _group_select([1.0 / (jnp.minimum(pos + hi, n - 1) - jnp.maximum(pos - lo, 0) + 1).astype(F32)
                          for lo, hi in POOL_WIN])


def _pool_rows(ref, r, R, tc, mirror):
    def get(k):
        rr = r + k
        rc = jnp.clip(rr, 0, R - 1)
        v = ref[pl.ds(pl.multiple_of(tc + rc * GW, GW), GW), :]
        if mirror:
            v = v * _inv_count(jnp.full((1, PW), rc, jnp.int32), R)
        return jnp.where((rr >= 0) & (rr < R), v, 0.0)

    return _nested_box(get, mirror)


def _pool_fwd(u, pwbd, ps, tc):
    T = u.shape[0]
    R = (T - tc) // GW

    def body(u_ref, pw_ref, ps_ref, y_ref):
        pw, scale = pw_ref[...], ps_ref[...]
        uc = u_ref[0:tc, :]
        mc = _box_tokens(uc, tc, False) * _inv_count(_iota((tc, 1), 0), tc)
        y_ref[0:tc, :] = _dot(mc - uc, pw, NN) * scale
        inv_c = _inv_count(_iota((GW, 1), 0), GW)

        def row(r, carry):
            rs = _pool_rows(u_ref, r, R, tc, False) * _inv_count(jnp.full((1, PW), r, jnp.int32), R)
            m = _box_tokens(rs, GW, False) * inv_c
            sl = pl.ds(pl.multiple_of(tc + r * GW, GW), GW)
            y_ref[sl, :] = _dot(m - u_ref[sl, :], pw, NN) * scale
            return carry

        lax.fori_loop(0, R, row, 0)

    return pl.pallas_call(
        body, name="pool_fwd", out_shape=_S((T, PW)),
        compiler_params=pltpu.CompilerParams(vmem_limit_bytes=VMEM_MB << 20))(u, pwbd, ps)


def _pool_bwd(u, pwbd, ps, dy, tc):
    T = u.shape[0]
    R = (T - tc) // GW

    def body(u_ref, pw_ref, ps_ref, dy_ref, du_ref, dpw_ref, dps_ref, dd_ref):
        pw, scale = pw_ref[...], ps_ref[...]
        dpw_ref[...] = jnp.zeros_like(dpw_ref)
        dps_ref[...] = jnp.zeros_like(dps_ref)

        def back(d, dy_):
            dz = dy_ * scale
            dpw_ref[...] += _dot(d, dz, TN)
            dps_ref[...] += _colsum(dy_ * _dot(d, pw, NN))
            return _dot(dz, pw, NT)

        uc = u_ref[0:tc, :]
        inv_cc = _inv_count(_iota((tc, 1), 0), tc)
        ddc = back(_box_tokens(uc, tc, False) * inv_cc - uc, dy_ref[0:tc, :])
        du_ref[0:tc, :] = _box_tokens(ddc * inv_cc, tc, True) - ddc
        inv_c = _inv_count(_iota((GW, 1), 0), GW)

        def row1(r, carry):
            rs = _pool_rows(u_ref, r, R, tc, False) * _inv_count(jnp.full((1, PW), r, jnp.int32), R)
            m = _box_tokens(rs, GW, False) * inv_c
            sl = pl.ds(pl.multiple_of(tc + r * GW, GW), GW)
            dd_ref[sl, :] = back(m - u_ref[sl, :], dy_ref[sl, :])
            return carry

        lax.fori_loop(0, R, row1, 0)

        def row2(r, carry):
            t1 = _pool_rows(dd_ref, r, R, tc, True)
            sl = pl.ds(pl.multiple_of(tc + r * GW, GW), GW)
            du_ref[sl, :] = _box_tokens(t1 * inv_c, GW, True) - dd_ref[sl, :]
            return carry

        lax.fori_loop(0, R, row2, 0)

    return pl.pallas_call(
        body, name="pool_bwd", out_shape=[_S((T, PW)), _S((PW, PW)), _S((1, PW))],
        scratch_shapes=[pltpu.VMEM((T, PW), F32)],
        compiler_params=pltpu.CompilerParams(vmem_limit_bytes=VMEM_MB << 20))(u, pwbd, ps, dy)


def _scan_consts():
    i = np.arange(CH)
    lower = (i[:, None] >= i[None, :]).astype(np.float32)
    return jnp.asarray(np.stack([lower, lower.T])), jnp.asarray(np.stack([lower.T, lower]))


def _gates(pab, al, dtb, csum):
    sp_in = pab + dtb
    sp = jnp.maximum(sp_in, 0.0) + jnp.log(1.0 + jnp.exp(-jnp.abs(sp_in)))
    nexp = -jnp.exp(al)
    gm = nexp * sp
    return gm, jax.nn.sigmoid(pab), _dot(csum, gm, NN, hi=True), sp_in, nexp


def _lane_col(m, j):
    return jnp.sum(jnp.where(_iota(m.shape, 1) == j, m, 0.0), axis=1, keepdims=True)


def _hs(h):
    return slice(h * DH, (h + 1) * DH)


HS = NH * CH
X3 = "x3"


def _stack(x, base=0):
    return jnp.concatenate([x[:, base + h * DH:base + (h + 1) * DH] for h in range(NH)], axis=0)


def _heads(st):
    return [st[h * CH:(h + 1) * CH] for h in range(NH)]


def _rowsum(a):
    return jnp.sum(a, axis=1, keepdims=True)


def _row_of(col):
    e0 = (_iota((8, LANES), 1) == 0).astype(F32)
    return _dot(e0, jnp.broadcast_to(col, (HS, LANES)), NT, hi=True)[0:1, :]


def _dn_chunk(qkv, pab, al, dtb, csum_d, d, tm=None):
    gm, bm, gcm, sp_in, nexp = _gates(pab, al, dtb, csum_d)
    gc = jnp.concatenate([_lane_col(gcm, d * NH + h) for h in range(NH)], axis=0)
    beta = jnp.concatenate([_lane_col(bm, 8 + d * NH + h) for h in range(NH)], axis=0)
    q, k, v = _stack(qkv, 0), _stack(qkv, DN), _stack(qkv, 2 * DN)
    ii, jj = _iota((HS, HS), 0), _iota((HS, HS), 1)
    sh = CH.bit_length() - 1
    same = (ii >> sh) == (jj >> sh)
    incl = same & ((ii >= jj) if d == 0 else (ii <= jj))
    strict = same & ((ii > jj) if d == 0 else (ii < jj))
    last = CH - 1 if d == 0 else 0
    Di = jnp.where(incl, jnp.exp(jnp.where(incl, gc - _row_of(gc), 0.0)), 0.0)
    Ds = jnp.where(strict, Di, 0.0)
    kb = k * beta
    kk = _dot(kb, k, NT)
    if tm is None:
        nm = -(kk * Ds)
        eye = (ii == jj).astype(F32)
        x0 = eye + nm
        mp = nm
        for _ in range(5):
            mp = _dot(mp, mp, NN)
            x0 = x0 + _dot(x0, mp, NN)
        tm = x0 + _dot(x0, eye - _dot(eye - nm, x0, NN, hi=X3), NN)
    E = jnp.exp(gc)
    gls = [gc[h * CH + last:h * CH + last + 1, :] for h in range(NH)]
    xs = jnp.exp(jnp.concatenate([jnp.broadcast_to(g, (CH, 1)) for g in gls], axis=0) - gc)
    qk = _dot(q, k, NT)
    return dict(q=q, k=k, v=v, beta=beta, gm=gm, bm=bm, sp_in=sp_in, nexp=nexp, Di=Di, Ds=Ds, strict=strict, last=last,
                kb=kb, kk=kk, tm=tm, E=E, gls=gls, xs=xs, qk=qk, u=_dot(tm, v * beta, NN, hi=X3),
                w=_dot(tm, kb * E, NN, hi=X3), ks=k * xs, qd=q * E, aqk=qk * Di)


def _dn_chunk_bwd(c, S, dS2, do, vn, dvn):
    q, k, v, beta, tm, E, xs, kb, u, w = (c[n] for n in ("q", "k", "v", "beta", "tm", "E", "xs", "kb", "u", "w"))
    doh, vnh, dvnh = _heads(do), _heads(vn), _heads(dvn)
    dqd = jnp.concatenate([_dot(doh[h], S[h], NT) for h in range(NH)], axis=0)
    dks = jnp.concatenate([_dot(vnh[h], dS2[h], NT) for h in range(NH)], axis=0)
    dw = -jnp.concatenate([_dot(dvnh[h], S[h], NT) for h in range(NH)], axis=0)
    daqk = _dot(do, vn, NT)
    drb = _dot(tm, dvn, TN, hi=X3)
    drw = _dot(tm, dw, TN, hi=X3)
    dA = jnp.where(c["strict"], -(_dot(drb, u, NT) + _dot(drw, w, NT)), 0.0)
    dM1 = dA * c["Ds"]
    dM2 = daqk * c["Di"]
    dkb = _dot(dM1, k, NN) + drw * E
    dk = _dot(dM1, kb, TN) + _dot(dM2, q, TN) + dks * xs
    dq = _dot(dM2, k, NN) + dqd * E
    G = dM1 * c["kk"] + dM2 * c["qk"]
    on_diag = _iota((HS, HS), 0) == _iota((HS, HS), 1)
    col = _rowsum(jnp.where(on_diag, jnp.broadcast_to(_colsum(G), (HS, HS)), 0.0))
    dxx = _rowsum(dks * k) * xs
    dgc = _rowsum(G) - col + (_rowsum(dqd * q) + _rowsum(drw * kb)) * E - dxx
    at_last = _iota((CH, 1), 0) == c["last"]
    ends = []
    for h in range(NH):
        dgl = _colsum(_rowsum(S[h] * dS2[h])) * jnp.exp(c["gls"][h]) + _colsum(dxx[h * CH:(h + 1) * CH])
        ends.append(jnp.where(at_last, dgl, 0.0))
    dgc = dgc + jnp.concatenate(ends, axis=0)
    dbeta = _rowsum(drb * v) + _rowsum(dkb * k)
    return dq, dk + dkb * beta, drb * beta, dgc, dbeta


def _chunk_group(n):
    return 2 if n % 2 == 0 else 1


def _dn_chunks_fwd(qkv, pab, alr, dtr):
    T = qkv.shape[0]
    n = T // CH
    G = _chunk_group(n)
    csum, _ = _scan_consts()

    def body(q_ref, p_ref, cs_ref, al_ref, dt_ref, *outs):
        for g in range(G):
            tok, rows = slice(g * CH, (g + 1) * CH), slice(g * HS, (g + 1) * HS)
            for d in range(2):
                u_ref, w_ref, ks_ref, qd_ref, aqk_ref, eg_ref, tm_ref = outs[7 * d:7 * d + 7]
                c = _dn_chunk(q_ref[tok, :], p_ref[tok, :], al_ref[...], dt_ref[...], cs_ref[d], d)
                tm_ref[rows, :] = c["tm"]
                u_ref[rows, :] = c["u"]
                w_ref[rows, :] = c["w"].astype(BF16)
                ks_ref[rows, :] = c["ks"].astype(BF16)
                qd_ref[rows, :] = c["qd"].astype(BF16)
                aqk_ref[rows, :] = c["aqk"].astype(BF16)
                egs = [jnp.broadcast_to(jnp.exp(gl), (1, LANES)) for gl in c["gls"]]
                eg_ref[g * 8:(g + 1) * 8, :] = jnp.concatenate(egs + [jnp.zeros((8 - NH, LANES), F32)], axis=0)

    st = lambda w_: pl.BlockSpec((G * HS, w_), lambda i: (i, 0))
    one = [st(DH)] * 4 + [st(HS), pl.BlockSpec((G * 8, LANES), lambda i: (i, 0)), st(HS)]
    shp = [_S((n * HS, DH)), _S((n * HS, DH), BF16), _S((n * HS, DH), BF16), _S((n * HS, DH), BF16),
           _S((n * HS, HS), BF16), _S((n * 8, LANES)), _S((n * HS, HS))]
    outs = _call(
        body, "dn_chunks_fwd", (n // G,),
        [_rows(G * CH, 3 * DN), _rows(G * CH, LANES), _full((2, CH, CH)), _full((1, LANES)), _full((1, LANES))],
        one * 2, shp * 2)(qkv, pab, csum, alr, dtr)
    return tuple(outs[:7]), tuple(outs[7:])


def _scan_order(n, ncx):
    return (lambda i: i), (lambda i: jnp.where(i < ncx, ncx - 1 - i, n - 1 - (i - ncx)))


def _scan_specs(order):
    st = lambda w_: pl.BlockSpec((HS, w_), lambda i: (order(i), 0))
    return dict(st=st(DH), aqk=st(HS), eg=pl.BlockSpec((8, LANES), lambda i: (order(i), 0)),
                tok=pl.BlockSpec((CH, DN), lambda i: (order(i), 0)), state=pl.BlockSpec((1, DN, DH), lambda i: (order(i), 0, 0)))


def _scan_fwd(parts, T, tc):
    n = T // CH
    orders = _scan_order(n, tc // CH)

    def body(*refs):
        S_f, S_b = refs[-2:]

        @pl.when(pl.program_id(0) == 0)
        def _():
            S_f[...] = jnp.zeros_like(S_f)
            S_b[...] = jnp.zeros_like(S_b)

        for d, S in enumerate((S_f, S_b)):
            u_ref, w_ref, ks_ref, qd_ref, aqk_ref, eg_ref = refs[6 * d:6 * d + 6]
            o_ref, ss_ref, vn_ref = refs[12 + 3 * d:15 + 3 * d]
            ss_ref[0] = S[...]
            Sh = [S[_hs(h), :] for h in range(NH)]
            wh, ksh, qdh = _heads(w_ref[...]), _heads(ks_ref[...]), _heads(qd_ref[...])
            vn = u_ref[...] - jnp.concatenate([_dot(wh[h], Sh[h], NN) for h in range(NH)], axis=0)
            vn_ref[...] = vn
            av, vnh = _heads(_dot(aqk_ref[...], vn, NN)), _heads(vn)
            for h in range(NH):
                o_ref[:, _hs(h)] = _dot(qdh[h], Sh[h], NN) + av[h]
                S[_hs(h), :] = Sh[h] * eg_ref[h:h + 1, :] + _dot(ksh[h], vnh[h], TN)

    ins, outs, shp = [], [], []
    for d in range(2):
        sp = _scan_specs(orders[d])
        ins += [sp["st"]] * 4 + [sp["aqk"], sp["eg"]]
        outs += [sp["tok"], sp["state"], sp["st"]]
        shp += [_S((T, DN)), _S((n, DN, DH)), _S((n * HS, DH))]
    res = _call(body, "scan_fwd", (n,), ins, outs, shp,
                scratch=[pltpu.VMEM((DN, DH), F32), pltpu.VMEM((DN, DH), F32)])(*parts[0][:6], *parts[1][:6])
    return tuple(res[:3]), tuple(res[3:])


def _scan_bwd(do, parts, tc):
    T = do.shape[0]
    n = T // CH
    fwd_orders = _scan_order(n, tc // CH)
    orders = [lambda s, f=f: f(n - 1 - s) for f in fwd_orders]

    def body(*refs):
        dS_f, dS_b = refs[-2:]

        @pl.when(pl.program_id(0) == 0)
        def _():
            dS_f[...] = jnp.zeros_like(dS_f)
            dS_b[...] = jnp.zeros_like(dS_b)

        for d, dS in enumerate((dS_f, dS_b)):
            do_ref, w_ref, ks_ref, qd_ref, aqk_ref, eg_ref = refs[6 * d:6 * d + 6]
            dvn_ref, dss_ref = refs[12 + 2 * d:14 + 2 * d]
            dss_ref[0] = dS[...]
            dSh = [dS[_hs(h), :] for h in range(NH)]
            wh, ksh, qdh = _heads(w_ref[...]), _heads(ks_ref[...]), _heads(qd_ref[...])
            do_st = _stack(do_ref[...])
            dvn = _dot(aqk_ref[...], do_st, TN) + jnp.concatenate([_dot(ksh[h], dSh[h], NN) for h in range(NH)], axis=0)
            dvn_ref[...] = dvn
            doh, dvnh = _heads(do_st), _heads(dvn)
            for h in range(NH):
                dS[_hs(h), :] = _dot(qdh[h], doh[h], TN) + dSh[h] * eg_ref[h:h + 1, :] - _dot(wh[h], dvnh[h], TN)

    ins, outs, shp, args = [], [], [], []
    for d in range(2):
        sp = _scan_specs(orders[d])
        ins += [sp["tok"]] + [sp["st"]] * 3 + [sp["aqk"], sp["eg"]]
        outs += [sp["st"], sp["state"]]
        shp += [_S((n * HS, DH)), _S((n, DN, DH))]
        args += [do, *parts[d][1:6]]
    res = _call(body, "scan_bwd", (n,), ins, outs, shp,
                scratch=[pltpu.VMEM((DN, DH), F32), pltpu.VMEM((DN, DH), F32)])(*args)
    return tuple(res[:2]), tuple(res[2:])


def _dn_chunks_bwd(qkv, pab, alr, dtr, do, fwd, bwd):
    T = qkv.shape[0]
    n = T // CH
    G = _chunk_group(n)
    csum, csum_t = _scan_consts()

    def body(q_ref, p_ref, do_ref, cs_ref, cst_ref, al_ref, dt_ref, *refs):
        dq_refs, dp_refs, acc_ref = refs[10:12], refs[12:14], refs[14]

        @pl.when(pl.program_id(0) == 0)
        def _():
            acc_ref[...] = jnp.zeros_like(acc_ref)

        lane = _iota((CH, LANES), 1)
        for g in range(G):
            tok, rows = slice(g * CH, (g + 1) * CH), slice(g * HS, (g + 1) * HS)
            do_st = _stack(do_ref[tok, :])
            for d in range(2):
                vn_ref, dvn_ref, ss_ref, dss_ref, tm_ref = refs[5 * d:5 * d + 5]
                c = _dn_chunk(q_ref[tok, :], p_ref[tok, :], al_ref[...], dt_ref[...], cs_ref[d], d, tm=tm_ref[rows, :])
                dq, dk, dv, dgc, dbeta = _dn_chunk_bwd(
                    c, [ss_ref[g, _hs(h), :] for h in range(NH)], [dss_ref[g, _hs(h), :] for h in range(NH)],
                    do_st, vn_ref[rows, :], dvn_ref[rows, :])
                dgcm = jnp.zeros((CH, LANES), F32)
                dbm = jnp.zeros((CH, LANES), F32)
                for h, (a, b_, c_, e, f) in enumerate(zip(*map(_heads, (dq, dk, dv, dgc, dbeta)))):
                    dq_refs[d][tok, _hs(h)] = a
                    dq_refs[d][tok, _hs(NH + h)] = b_
                    dq_refs[d][tok, _hs(2 * NH + h)] = c_
                    dgcm = jnp.where(lane == d * NH + h, e, dgcm)
                    dbm = jnp.where(lane == 8 + d * NH + h, f, dbm)
                dgm = _dot(cst_ref[d], dgcm, NN, hi=True)
                dsp = dgm * c["nexp"] * jax.nn.sigmoid(c["sp_in"])
                dp_refs[d][tok, :] = dsp + dbm * c["bm"] * (1.0 - c["bm"])
                acc_ref[0:1, :] += _colsum(dgm * c["gm"])
                acc_ref[1:2, :] += _colsum(dsp)

    st = pl.BlockSpec((G * HS, DH), lambda i: (i, 0))
    state = pl.BlockSpec((G, DN, DH), lambda i: (i, 0, 0))
    return _call(
        body, "dn_chunks_bwd", (n // G,),
        [_rows(G * CH, 3 * DN), _rows(G * CH, LANES), _rows(G * CH, DN), _full((2, CH, CH)), _full((2, CH, CH)),
         _full((1, LANES)), _full((1, LANES))] + [st, st, state, state, pl.BlockSpec((G * HS, HS), lambda i: (i, 0))] * 2,
        [_rows(G * CH, 3 * DN)] * 2 + [_rows(G * CH, LANES)] * 2 + [_full((8, LANES))],
        [_S((T, 3 * DN))] * 2 + [_S((T, LANES))] * 2 + [_S((8, LANES))])(
            qkv, pab, do, csum, csum_t, alr, dtr, *fwd, *bwd)


def _head_out(o, z, g):
    on = o * lax.rsqrt(jnp.mean(o * o, axis=-1, keepdims=True) + EPS) * g
    return on * _silu(z)


def _mix_branches(of_ref, ob_ref, z_ref, yp_ref, ys_ref, pg_ref, gdn_ref, wa_ref, wb_ref, wc_ref):
    ons, ya = [], None
    for h in range(NH):
        on = _head_out(of_ref[:, _hs(h)] + ob_ref[:, _hs(h)], z_ref[:, _hs(h)], gdn_ref[...])
        t = _dot(on, wa_ref[_hs(h), :], NN)
        ya = t if ya is None else ya + t
        ons.append(on)
    ys = [ya, _dot(yp_ref[...], wb_ref[...], NN), _dot(ys_ref[...], wc_ref[...], NN)]
    sg = [jax.nn.sigmoid(pg_ref[:, k * D:(k + 1) * D]) for k in range(3)]
    return ons, ys, sg


def _mix_fwd(X, of, ob, z, yp, ys, pg, mv, gdn, wa, wb, wc, wo, tc, tt):
    T = X.shape[0]

    def body(x_ref, of_ref, ob_ref, z_ref, yp_ref, ys_ref, pg_ref, mv_ref, gdn_ref, wa_ref, wb_ref, wc_ref, wo_ref,
             x1_ref):
        _, yb, sg = _mix_branches(of_ref, ob_ref, z_ref, yp_ref, ys_ref, pg_ref, gdn_ref, wa_ref, wb_ref, wc_ref)
        mix = _dot(sg[0] * yb[0] + sg[1] * yb[1] + sg[2] * yb[2], wo_ref[...], NN)
        _, gate = _stream_rows(mv_ref, pl.program_id(0), tt, tc, 2)
        x1_ref[...] = x_ref[...] + gate * mix

    return _call(
        body, "mix_fwd", (T // tt,),
        [_rows(tt, D), _rows(tt, DN), _rows(tt, DN), _rows(tt, DN), _rows(tt, PW), _rows(tt, PW), _rows(tt, 3 * D),
         _full((8, D)), _full((1, DH)), _full(wa.shape), _full(wb.shape), _full(wc.shape), _full(wo.shape)],
        _rows(tt, D), _S((T, D)))(X, of, ob, z, yp, ys, pg, mv, gdn, wa, wb, wc, wo)


def _mix_bwd(dx1, of, ob, z, yp, ys, pg, mv, gdn, wa, wb, wc, wo, tc, tt):
    T = dx1.shape[0]

    def body(dx_ref, of_ref, ob_ref, z_ref, yp_ref, ys_ref, pg_ref, mv_ref, gdn_ref, wa_ref, wb_ref, wc_ref, wo_ref,
             do_ref, dz_ref, dyp_ref, dys_ref, dpg_ref, dwa_ref, dwb_ref, dwc_ref, dwo_ref, dgdn_ref, dm_ref):
        i = pl.program_id(0)

        @pl.when(i == 0)
        def _():
            for r in (dwa_ref, dwb_ref, dwc_ref, dwo_ref, dgdn_ref, dm_ref):
                r[...] = jnp.zeros_like(r)

        ons, yb, sg = _mix_branches(of_ref, ob_ref, z_ref, yp_ref, ys_ref, pg_ref, gdn_ref, wa_ref, wb_ref, wc_ref)
        ymix = sg[0] * yb[0] + sg[1] * yb[1] + sg[2] * yb[2]
        isc, gate = _stream_rows(mv_ref, i, tt, tc, 2)
        dx = dx_ref[...]
        dmix = dx * gate
        _acc_stream(dm_ref, 2, isc, dx * _dot(ymix, wo_ref[...], NN))
        dwo_ref[...] += _dot(ymix, dmix, TN)
        dymix = _dot(dmix, wo_ref[...], NT)
        dyb = []
        for k in range(3):
            dyb.append(dymix * sg[k])
            dpg_ref[:, k * D:(k + 1) * D] = dymix * yb[k] * sg[k] * (1.0 - sg[k])
        dwb_ref[...] += _dot(yp_ref[...], dyb[1], TN)
        dwc_ref[...] += _dot(ys_ref[...], dyb[2], TN)
        dyp_ref[...] = _dot(dyb[1], wb_ref[...], NT)
        dys_ref[...] = _dot(dyb[2], wc_ref[...], NT)
        dg = jnp.zeros((1, DH), F32)
        for h in range(NH):
            dwa_ref[_hs(h), :] += _dot(ons[h], dyb[0], TN)
            don = _dot(dyb[0], wa_ref[_hs(h), :], NT)
            _, vjp = jax.vjp(_head_out, of_ref[:, _hs(h)] + ob_ref[:, _hs(h)], z_ref[:, _hs(h)], gdn_ref[...])
            do_h, dz_h, dg_h = vjp(don)
            do_ref[:, _hs(h)] = do_h
            dz_ref[:, _hs(h)] = dz_h
            dg = dg + dg_h
        dgdn_ref[...] += dg

    return _call(
        body, "mix_bwd", (T // tt,),
        [_rows(tt, D), _rows(tt, DN), _rows(tt, DN), _rows(tt, DN), _rows(tt, PW), _rows(tt, PW), _rows(tt, 3 * D),
         _full((8, D)), _full((1, DH)), _full(wa.shape), _full(wb.shape), _full(wc.shape), _full(wo.shape)],
        [_rows(tt, DN), _rows(tt, DN), _rows(tt, PW), _rows(tt, PW), _rows(tt, 3 * D),
         _full(wa.shape), _full(wb.shape), _full(wc.shape), _full(wo.shape), _full((1, DH)), _full((8, D))],
        [_S((T, DN)), _S((T, DN)), _S((T, PW)), _S((T, PW)), _S((T, 3 * D)),
         _S(wa.shape), _S(wb.shape), _S(wc.shape), _S(wo.shape), _S((1, DH)), _S((8, D))])(
            dx1, of, ob, z, yp, ys, pg, mv, gdn, wa, wb, wc, wo)


def _ffn_fwd(X1, mv, g, wgu, wd, tc, tt):
    T = X1.shape[0]

    def body(x_ref, mv_ref, g_ref, wgu_ref, wd_ref, x2_ref):
        i = pl.program_id(0)
        _, sh = _stream_rows(mv_ref, i, tt, tc, 0)
        _, sc = _stream_rows(mv_ref, i, tt, tc, 1)
        _, gate = _stream_rows(mv_ref, i, tt, tc, 2)
        x = x_ref[...]
        gu = _dot(_modulate(x, g_ref[...], sh, sc), wgu_ref[...], NN)
        x2_ref[...] = x + gate * _dot(_silu(gu[:, :DFF]) * gu[:, DFF:], wd_ref[...], NN)

    return _call(
        body, "ffn_fwd", (T // tt,),
        [_rows(tt, D), _full((8, D)), _full((1, D)), _full(wgu.shape), _full(wd.shape)],
        _rows(tt, D), _S((T, D)))(X1, mv, g, wgu, wd)


def _ffn_bwd(X1, dx2, mv, g, wgu, wd, tc, tt):
    T = X1.shape[0]

    def body(x_ref, dx2_ref, mv_ref, g_ref, wgu_ref, wd_ref, dx1_ref, h_ref, dgu_ref, act_ref, dff_ref, dg_ref, dm_ref):
        i = pl.program_id(0)
        isc, sh = _stream_rows(mv_ref, i, tt, tc, 0)
        _, sc = _stream_rows(mv_ref, i, tt, tc, 1)
        _, gate = _stream_rows(mv_ref, i, tt, tc, 2)
        x, dx2_ = x_ref[...], dx2_ref[...]
        h, vjp = jax.vjp(_modulate, x, g_ref[...], sh, sc)
        hb = h.astype(BF16)
        h_ref[...] = hb
        gu = jnp.dot(hb, wgu_ref[...], preferred_element_type=F32)
        ga, up = gu[:, :DFF], gu[:, DFF:]
        sg = jax.nn.sigmoid(ga)
        act = (ga * sg * up).astype(BF16)
        act_ref[...] = act
        dff = dx2_ * gate
        dff_ref[...] = dff.astype(BF16)
        dact = _dot(dff, wd_ref[...], NT)
        dga = (dact * up * (sg * (1.0 + ga * (1.0 - sg)))).astype(BF16)
        dup = (dact * ga * sg).astype(BF16)
        dgu_ref[:, :DFF] = dga
        dgu_ref[:, DFF:] = dup
        dh = _dot(dga, wgu_ref[:, :DFF], NT) + _dot(dup, wgu_ref[:, DFF:], NT)
        dx, dg, dsh, dsc = vjp(dh)
        dx1_ref[...] = dx2_ + dx

        @pl.when(i == 0)
        def _():
            dg_ref[...] = jnp.zeros_like(dg_ref)
            dm_ref[...] = jnp.zeros_like(dm_ref)

        dg_ref[...] += dg
        _acc_stream(dm_ref, 0, isc, dsh)
        _acc_stream(dm_ref, 1, isc, dsc)
        _acc_stream(dm_ref, 2, isc, dx2_ * jnp.dot(act, wd_ref[...], preferred_element_type=F32))

    return _call(
        body, "ffn_bwd", (T // tt,),
        [_rows(tt, D), _rows(tt, D), _full((8, D)), _full((1, D)), _full(wgu.shape), _full(wd.shape)],
        [_rows(tt, D), _rows(tt, D), _rows(tt, 2 * DFF), _rows(tt, DFF), _rows(tt, D), _full((1, D)), _full((8, D))],
        [_S((T, D)), _S((T, D), BF16), _S((T, 2 * DFF), BF16), _S((T, DFF), BF16), _S((T, D), BF16),
         _S((1, D)), _S((8, D))])(X1, dx2, mv, g, wgu, wd)


def _rms(x, g):
    return x * lax.rsqrt(jnp.mean(x * x, axis=-1, keepdims=True) + EPS) * g


def _loss_head(X2, tgt, gf, tc):
    T = X2.shape[0]

    def body(x_ref, t_ref, g_ref, dx_ref, loss_ref, dg_ref):
        i = pl.program_id(0)

        @pl.when(i == 0)
        def _():
            dx_ref[...] = jnp.zeros_like(dx_ref)
            loss_ref[...] = jnp.zeros_like(loss_ref)
            dg_ref[...] = jnp.zeros_like(dg_ref)

        @pl.when(i > 0)
        def _():
            y, vjp = jax.vjp(_rms, x_ref[...], g_ref[...])
            err = y - t_ref[...]
            dx, dg = vjp(err * (1.0 / D))
            dx_ref[...] = dx
            dg_ref[...] += dg
            loss_ref[...] += (0.5 / D) * jnp.sum(jnp.sum(err * err, axis=1, keepdims=True), axis=0, keepdims=True)

    return _call(
        body, "loss_head", (T // tc,),
        [_rows(tc, D), pl.BlockSpec((tc, D), lambda i: (jnp.maximum(i - 1, 0), 0)), _full((1, D))],
        [_rows(tc, D), _full((8, LANES)), _full((1, D))],
        [_S((T, D)), _S((8, LANES)), _S((1, D))])(X2, tgt, gf)


def _block_diag(pw):
    g, n = pw.shape[0], pw.shape[1]
    out = jnp.zeros((g * n, g * n), pw.dtype)
    for k in range(g):
        out = lax.dynamic_update_slice(out, pw[k], (k * n, k * n))
    return out


def _split_w_in(w):
    parts = [w[:, IN_BOUNDS[k]:IN_BOUNDS[k + 1]] for k in range(8)]
    parts[2] = jnp.pad(parts[2], ((0, 0), (0, LANES - 16)))
    return parts


def _mod_rows(mods_l, k0):
    rows = [mods_l[s, (k0 + k) * D:(k0 + k + 1) * D] for s in (0, 1) for k in range(3)]
    return jnp.stack(rows + [jnp.zeros((D,), F32)] * 2)


def _lane_row(v8):
    return jnp.pad(v8.reshape(1, 8), ((0, 0), (0, LANES - 8)))


def _device_step(x, c, ctx, tgt, wts, tt):
    tc = ctx.shape[0]
    X = jnp.concatenate([ctx, x], axis=0)
    cc8 = jnp.concatenate([wts["c_ctx"][None, :], c, jnp.zeros((6, D), F32)], axis=0)
    w_ada = wts["w_ada"].astype(BF16)
    mods = _mod_fwd(cc8, w_ada, wts["b_ada"].reshape(NL, 1, 6 * D))

    saved = []
    for l in range(NL):
        ws = [w.astype(BF16) for w in _split_w_in(wts["w_in"][l])]
        wbr = [wts[k][l].astype(BF16) for k in ("w_br_a", "w_br_b", "w_br_c", "w_o", "w_gu", "w_down")]
        mv1, mv2 = _mod_rows(mods[l], 0), _mod_rows(mods[l], 3)
        g1, g2 = wts["norm1_g"][l][None, :], wts["norm2_g"][l][None, :]
        cw, scw = wts["dn_conv_w"][l], wts["sc_conv_w"][l]
        alr, dtr = _lane_row(wts["dn_a_log"][l]), _lane_row(wts["dn_dt_bias"][l])
        gdn = wts["dn_norm_g"][l][None, :]
        pwbd, ps = _block_diag(wts["pool_w"][l]), wts["pool_scale"][l][None, :]
        hb, pq, pz, pab, pp, sx, sb, sc_, pg = _inproj_fwd(X, mv1, g1, ws, tc, tt)
        qkv = _dnprep_fwd(pq, cw, tc, tt)
        parts = _dn_chunks_fwd(qkv, pab, alr, dtr)
        (of, ssf, vnf), (ob, ssb, vnb) = _scan_fwd(parts, X.shape[0], tc)
        yp = _pool_fwd(pp, pwbd, ps, tc)
        ys = _sc_fwd(sx, sb, sc_, scw, tc, tt)
        X1 = _mix_fwd(X, of, ob, pz, yp, ys, pg, mv1, gdn, *wbr[:4], tc, tt)
        X2 = _ffn_fwd(X1, mv2, g2, wbr[4], wbr[5], tc, tt)
        saved.append(dict(X=X, X1=X1, ws=ws, wbr=wbr, mv1=mv1, mv2=mv2, g1=g1, g2=g2, cw=cw, scw=scw, alr=alr, dtr=dtr,
                          gdn=gdn, pwbd=pwbd, ps=ps, hb=hb, pq=pq, pz=pz, pab=pab, pp=pp, sx=sx, sb=sb, sc=sc_, pg=pg,
                          qkv=qkv, of=of, ob=ob, ssf=ssf, ssb=ssb, vnf=vnf, vnb=vnb, parts=parts, yp=yp, ys=ys))
        X = X2

    dX, loss, dgf = _loss_head(X, tgt, wts["final_norm_g"][None, :], tc)

    gl = {k: [None] * NL for k in ("w_in", "norm1_g", "norm2_g", "dn_conv_w", "dn_a_log", "dn_dt_bias", "dn_norm_g",
                                   "pool_w", "pool_scale", "sc_conv_w", "w_br_a", "w_br_b", "w_br_c", "w_o", "w_gu",
                                   "w_down")}
    dmods = [None] * NL
    for l in reversed(range(NL)):
        s = saved[l]
        dx1, h2, dgu, act, dff, dg2, dm2 = _ffn_bwd(s["X1"], dX, s["mv2"], s["g2"], s["wbr"][4], s["wbr"][5], tc, tt)
        gl["w_gu"][l] = _dw(h2, dgu, tt)
        gl["w_down"][l] = _dw(act, dff, tt)
        do, dz, dyp, dys, dpg, dwa, dwb, dwc, dwo, dgdn, dmg = _mix_bwd(
            dx1, s["of"], s["ob"], s["pz"], s["yp"], s["ys"], s["pg"], s["mv1"], s["gdn"], *s["wbr"][:4], tc, tt)
        dpp, dpw, dps = _pool_bwd(s["pp"], s["pwbd"], s["ps"], dyp, tc)
        dsx, dsb, dsc, dscw = _sc_bwd(s["sx"], s["sb"], s["sc"], s["scw"], dys, tc, tt)
        (dvnf, dssf), (dvnb, dssb) = _scan_bwd(do, s["parts"], tc)
        dqf, dqb, dpf, dpb, gacc = _dn_chunks_bwd(s["qkv"], s["pab"], s["alr"], s["dtr"], do,
                                                  (s["vnf"], dvnf, s["ssf"], dssf, s["parts"][0][6]),
                                                  (s["vnb"], dvnb, s["ssb"], dssb, s["parts"][1][6]))
        dy = _dnprep_bwd_act(s["pq"], s["cw"], dqf, dqb, tc, tt)
        dpq, dcw = _conv_bwd(dy, s["pq"], s["cw"], tc, tt)
        dps_ = [dpq, dz, dpf, dpb, dpp, dsx, dsb, dsc, dpg]
        dp_w = [0, 1, 2, 2, 3, 4, 5, 6, 7]
        dX, dg1, dm1 = _inproj_bwd(s["X"], s["mv1"], s["g1"], s["ws"], dps_, dp_w, dx1, tc, tt)
        dws = [_dw(s["hb"], dpq, tt), _dw(s["hb"], dz, tt), _dw(s["hb"], dpf + dpb, tt)[:, :16], _dw(s["hb"], dpp, tt),
               _dw(s["hb"], dsx, tt), _dw(s["hb"], dsb, tt), _dw(s["hb"], dsc, tt), _dw(s["hb"], dpg, tt)]
        gl["w_in"][l] = jnp.concatenate(dws, axis=1)
        gl["norm1_g"][l], gl["norm2_g"][l] = dg1[0], dg2[0]
        gl["dn_conv_w"][l], gl["sc_conv_w"][l] = dcw, dscw
        gl["dn_a_log"][l], gl["dn_dt_bias"][l] = gacc[0, :8].reshape(2, NH), gacc[1, :8].reshape(2, NH)
        gl["dn_norm_g"][l] = dgdn[0]
        gl["pool_w"][l] = jnp.stack([dpw[k * GW:(k + 1) * GW, k * GW:(k + 1) * GW] for k in range(4)])
        gl["pool_scale"][l] = dps[0]
        gl["w_br_a"][l], gl["w_br_b"][l], gl["w_br_c"][l], gl["w_o"][l] = dwa, dwb, dwc, dwo
        dm = dm1 + dmg
        row = lambda r: jnp.concatenate([dm[r], dm[r + 1], dm[r + 2], dm2[r], dm2[r + 1], dm2[r + 2]])
        dmods[l] = jnp.stack([row(0), row(3)] + [jnp.zeros((6 * D,), F32)] * 6)

    dwada, dbada, dcc = _mod_bwd(cc8, w_ada, jnp.stack(dmods))
    grads = {k: jnp.stack(v) for k, v in gl.items()}
    grads.update(w_ada=dwada, b_ada=dbada.reshape(NL, 6 * D), c_ctx=dcc[0], final_norm_g=dgf[0])
    return loss, dX[tc:], grads


MESH_ID = pl.DeviceIdType.MESH
HBM_SPEC = pl.BlockSpec(memory_space=pltpu.HBM)


def _me():
    return lax.axis_index("x"), lax.axis_index("y"), lax.axis_index("c")


def _dev_index(p):
    return 4 * p[0] + 2 * p[1] + p[2]


def _allgather(parts):
    n = len(parts)

    def body(*refs):
        ins, outs = refs[:n], refs[n:2 * n]
        send_sems, recv_sems, local_sems = refs[2 * n:]
        x, y, c = _me()
        me, sibling = (x, y, c), (x, y, 1 - c)
        chips = [(1 - x, y), (x, 1 - y), (1 - x, 1 - y)]

        def copy(a, k, block, to, src=None):
            dst = outs[a].at[_dev_index(block)]
            return pltpu.make_async_remote_copy(
                src_ref=dst if src is None else src, dst_ref=dst, send_sem=send_sems.at[a, k], recv_sem=recv_sems.at[a, k],
                device_id=to, device_id_type=MESH_ID)

        mine, first, passed = [], [], []
        for a in range(n):
            mine.append(pltpu.make_async_copy(ins[a], outs[a].at[_dev_index(me)], local_sems.at[a]))
            mine[-1].start()
            first.append(copy(a, 0, me, sibling, src=ins[a]))
            first += [copy(a, 1 + j, me, (*chip, c), src=ins[a]) for j, chip in enumerate(chips)]
        for cp in first:
            cp.start()
        for a in range(n):
            for j, chip in enumerate(chips):
                copy(a, 1 + j, (*chip, c), me).wait_recv()
                passed.append(copy(a, 4 + j, (*chip, c), sibling))
                passed[-1].start()
        for a in range(n):
            copy(a, 0, sibling, me).wait_recv()
            for j, chip in enumerate(chips):
                copy(a, 4 + j, (*chip, 1 - c), me).wait_recv()
        for cp in first + passed:
            cp.wait_send()
        for cp in mine:
            cp.wait()

    return pl.pallas_call(
        body, name="allgather", in_specs=[HBM_SPEC] * n, out_specs=[HBM_SPEC] * n,
        out_shape=[_S((N_DEV,) + p.shape, p.dtype) for p in parts],
        scratch_shapes=[pltpu.SemaphoreType.DMA((n, 7)), pltpu.SemaphoreType.DMA((n, 7)), pltpu.SemaphoreType.DMA((n,))],
    )(*parts)


def _broadcast_small(small):
    def body(in_ref, out_ref, send_sems, recv_sems, local_sem):
        x, y, c = _me()
        my = _dev_index((x, y, c))
        mine = pltpu.make_async_copy(in_ref, out_ref.at[my], local_sem)
        mine.start()
        remote = []
        for k in range(1, N_DEV):
            cp = pltpu.make_async_remote_copy(
                src_ref=in_ref, dst_ref=out_ref.at[my], send_sem=send_sems.at[k - 1], recv_sem=recv_sems.at[k - 1],
                device_id=(x ^ (k >> 2), y ^ ((k >> 1) & 1), c ^ (k & 1)), device_id_type=MESH_ID)
            cp.start()
            remote.append(cp)
        for cp in remote:
            cp.wait_recv()
        for cp in remote:
            cp.wait_send()
        mine.wait()

    return pl.pallas_call(
        body, name="small_exchange", in_specs=[HBM_SPEC], out_specs=HBM_SPEC,
        out_shape=_S((N_DEV,) + small.shape, small.dtype),
        scratch_shapes=[pltpu.SemaphoreType.DMA((7,)), pltpu.SemaphoreType.DMA((7,)), pltpu.SemaphoreType.DMA],
    )(small)


N_CHIP = 4


def _pair_exchange(g2):
    def body(g_ref, own_ref, got_ref, send_sems, recv_sems, local_sem):
        x, y, c = _me()
        mine = pltpu.make_async_copy(g_ref.at[c], own_ref, local_sem)
        mine.start()
        cps = [pltpu.make_async_remote_copy(
            src_ref=g_ref.at[1 - c, j], dst_ref=got_ref.at[j], send_sem=send_sems.at[j], recv_sem=recv_sems.at[j],
            device_id=(x, y, 1 - c), device_id_type=MESH_ID) for j in range(N_CHIP)]
        for cp in cps:
            cp.start()
        for cp in cps:
            cp.wait_recv()
        for cp in cps:
            cp.wait_send()
        mine.wait()

    return pl.pallas_call(
        body, name="pair_exchange", in_specs=[HBM_SPEC], out_specs=[HBM_SPEC] * 2,
        out_shape=[_S(g2.shape[1:], g2.dtype)] * 2,
        scratch_shapes=[pltpu.SemaphoreType.DMA((N_CHIP,)), pltpu.SemaphoreType.DMA((N_CHIP,)), pltpu.SemaphoreType.DMA],
    )(g2)


def _chip_exchange(s4):
    def body(s_ref, out_ref, send_sems, recv_sems, local_sem):
        x, y, c = _me()
        my = 2 * x + y
        mine = pltpu.make_async_copy(s_ref.at[my], out_ref.at[my], local_sem)
        mine.start()
        cps = []
        for k in range(1, N_CHIP):
            px, py = x ^ (k >> 1), y ^ (k & 1)
            cps.append(pltpu.make_async_remote_copy(
                src_ref=s_ref.at[2 * px + py], dst_ref=out_ref.at[my], send_sem=send_sems.at[k - 1],
                recv_sem=recv_sems.at[k - 1], device_id=(px, py, c), device_id_type=MESH_ID))
            cps[-1].start()
        for cp in cps:
            cp.wait_recv()
        for cp in cps:
            cp.wait_send()
        mine.wait()

    return pl.pallas_call(
        body, name="chip_exchange", in_specs=[HBM_SPEC], out_specs=HBM_SPEC, out_shape=_S(s4.shape, s4.dtype),
        scratch_shapes=[pltpu.SemaphoreType.DMA((N_CHIP - 1,)), pltpu.SemaphoreType.DMA((N_CHIP - 1,)),
                        pltpu.SemaphoreType.DMA],
    )(s4)


def _row_tile(m):
    return max(t for t in range(BF16_ROWS, 4097, BF16_ROWS) if m % t == 0)


def _pair_sum(own, got):
    nc, m, _ = own.shape
    tr = _row_tile(m)

    def body(a_ref, b_ref, o_ref):
        o_ref[...] = (a_ref[...] + b_ref[...]).astype(BF16)

    blk = pl.BlockSpec((1, tr, LANES), lambda j, i: (j, i, 0))
    return _call(body, "pair_sum", (nc, m // tr), [blk, blk], blk, _S(own.shape, BF16))(own, got)


def _sum_chips(recv):
    nc, m, _ = recv.shape
    tr = _row_tile(m)

    def body(r_ref, o_ref):
        g = r_ref[0].astype(F32)
        for j in range(1, nc):
            g = g + r_ref[j].astype(F32)
        o_ref[...] = g

    return _call(body, "sum_chips", (m // tr,), [pl.BlockSpec((nc, tr, LANES), lambda i: (0, i, 0))],
                 _rows(tr, LANES), _S((m, LANES)))(recv)


def _adam(w, g, m, v):
    m2 = ADAM_B1 * m + (1.0 - ADAM_B1) * g
    v2 = ADAM_B2 * v + (1.0 - ADAM_B2) * (g * g)
    m_hat = m2 / (1.0 - ADAM_B1 ** ADAM_STEP)
    v_hat = v2 / (1.0 - ADAM_B2 ** ADAM_STEP)
    return -ADAM_LR * (m_hat / (jnp.sqrt(v_hat) + ADAM_EPS) + ADAM_WD * w), m2, v2


def _adam_big(w, g, m, v):
    L, R, C = w.shape
    tr = 256 if R % 256 == 0 else R

    def body(w_ref, g_ref, m_ref, v_ref, d_ref, m2_ref, v2_ref):
        d_ref[0], m2_ref[0], v2_ref[0] = _adam(w_ref[0], g_ref[0], m_ref[0], v_ref[0])

    blk = pl.BlockSpec((1, tr, C), lambda l, i: (l, i, 0))
    return _call(body, "adam_big", (L, R // tr), [blk] * 4, [blk] * 3, [_S(w.shape)] * 3)(w, g, m, v)


def _sum_small(recv):
    def body(r_ref, o_ref):
        g = r_ref[0]
        for k in range(1, N_DEV):
            g = g + r_ref[k]
        o_ref[...] = g

    return pl.pallas_call(body, name="sum_small", out_shape=_S(recv.shape[1:]))(recv)


def _adam_small(w, g, m, v):
    def body(w_ref, g_ref, m_ref, v_ref, d_ref, m2_ref, v2_ref):
        d_ref[...], m2_ref[...], v2_ref[...] = _adam(w_ref[...], g_ref[...], m_ref[...], v_ref[...])

    return pl.pallas_call(body, name="adam_small", out_shape=[_S(w.shape)] * 3)(w, g, m, v)


def _pack(arrs, dtype, row_mult):
    parts, offs, r = [], [], 0
    for a in arrs:
        nr = -(-a.size // LANES)
        parts.append(jnp.pad(a.reshape(-1).astype(dtype), (0, nr * LANES - a.size)))
        offs.append(r)
        r += nr
    pad = (-r) % row_mult
    if pad:
        parts.append(jnp.zeros((pad * LANES,), dtype))
    return jnp.concatenate(parts).reshape(r + pad, LANES), offs


def _unpack(packed, offs, shapes, lead=()):
    out = []
    for off, shp in zip(offs, shapes):
        size = int(np.prod(shp))
        nr = -(-size // LANES)
        flat = packed[..., off:off + nr, :].reshape(lead + (nr * LANES,))
        out.append(flat[..., :size].reshape(lead + tuple(shp)))
    return out


BIG = (("w_ada", 2), ("w_in", 2), ("w_br_a", 2), ("w_br_b", 2), ("w_br_c", 2), ("w_o", 1), ("w_gu", 2), ("w_down", 1))
CONV = ("dn_conv_w", "sc_conv_w")
REPL = ("c_ctx", "b_ada", "norm1_g", "norm2_g", "dn_a_log", "dn_dt_bias", "dn_norm_g", "pool_w", "pool_scale",
        "final_norm_g")
WEIGHTS = ("c_ctx", "w_ada", "b_ada", "norm1_g", "norm2_g", "w_in", "dn_conv_w", "dn_a_log", "dn_dt_bias", "dn_norm_g",
           "pool_w", "pool_scale", "sc_conv_w", "w_br_a", "w_br_b", "w_br_c", "w_o", "w_gu", "w_down", "final_norm_g")
TOKEN_TILE = 256


def _join(blocks, axis):
    nd, nl, r, c = blocks.shape
    if axis == 2:
        return blocks.transpose(1, 2, 0, 3).reshape(nl, r, nd * c)
    return blocks.transpose(1, 0, 2, 3).reshape(nl, nd * r, c)


def _split(full, axis):
    nl, r, c = full.shape
    if axis == 2:
        return full.reshape(nl, r, N_CHIP, 2, c // N_DEV).transpose(3, 2, 0, 1, 4)
    return full.reshape(nl, N_CHIP, 2, r // N_DEV, c).transpose(2, 1, 0, 3, 4)


def _pack_blocks(arrs):
    parts, offs, r = [], [], 0
    for a in arrs:
        size = int(np.prod(a.shape[2:]))
        nr = -(-size // LANES)
        parts.append(jnp.pad(a.reshape(2, N_CHIP, size), ((0, 0), (0, 0), (0, nr * LANES - size))))
        offs.append(r)
        r += nr
    pad = (-r) % BF16_ROWS
    if pad:
        parts.append(jnp.zeros((2, N_CHIP, pad * LANES), F32))
    return jnp.concatenate(parts, axis=2).reshape(2, N_CHIP, r + pad, LANES), offs


def kernel(x, c, ctx, c_ctx, w_ada, b_ada, norm1_g, norm2_g, w_in, dn_conv_w, dn_a_log, dn_dt_bias, dn_norm_g, pool_w, pool_scale, sc_conv_w, w_br_a, w_br_b, w_br_c, w_o, w_gu, w_down, final_norm_g, loss_target, m_c_ctx, m_w_ada, m_b_ada, m_norm1_g, m_norm2_g, m_w_in, m_dn_conv_w, m_dn_a_log, m_dn_dt_bias, m_dn_norm_g, m_pool_w, m_pool_scale, m_sc_conv_w, m_w_br_a, m_w_br_b, m_w_br_c, m_w_o, m_w_gu, m_w_down, m_final_norm_g, v_c_ctx, v_w_ada, v_b_ada, v_norm1_g, v_norm2_g, v_w_in, v_dn_conv_w, v_dn_a_log, v_dn_dt_bias, v_dn_norm_g, v_pool_w, v_pool_scale, v_sc_conv_w, v_w_br_a, v_w_br_b, v_w_br_c, v_w_o, v_w_gu, v_w_down, v_final_norm_g):
    loc = dict(c_ctx=c_ctx, w_ada=w_ada, b_ada=b_ada, norm1_g=norm1_g, norm2_g=norm2_g, w_in=w_in, dn_conv_w=dn_conv_w,
               dn_a_log=dn_a_log, dn_dt_bias=dn_dt_bias, dn_norm_g=dn_norm_g, pool_w=pool_w, pool_scale=pool_scale,
               sc_conv_w=sc_conv_w, w_br_a=w_br_a, w_br_b=w_br_b, w_br_c=w_br_c, w_o=w_o, w_gu=w_gu, w_down=w_down,
               final_norm_g=final_norm_g)
    mom_m = dict(c_ctx=m_c_ctx, w_ada=m_w_ada, b_ada=m_b_ada, norm1_g=m_norm1_g, norm2_g=m_norm2_g, w_in=m_w_in,
                 dn_conv_w=m_dn_conv_w, dn_a_log=m_dn_a_log, dn_dt_bias=m_dn_dt_bias, dn_norm_g=m_dn_norm_g,
                 pool_w=m_pool_w, pool_scale=m_pool_scale, sc_conv_w=m_sc_conv_w, w_br_a=m_w_br_a, w_br_b=m_w_br_b,
                 w_br_c=m_w_br_c, w_o=m_w_o, w_gu=m_w_gu, w_down=m_w_down, final_norm_g=m_final_norm_g)
    mom_v = dict(c_ctx=v_c_ctx, w_ada=v_w_ada, b_ada=v_b_ada, norm1_g=v_norm1_g, norm2_g=v_norm2_g, w_in=v_w_in,
                 dn_conv_w=v_dn_conv_w, dn_a_log=v_dn_a_log, dn_dt_bias=v_dn_dt_bias, dn_norm_g=v_dn_norm_g,
                 pool_w=v_pool_w, pool_scale=v_pool_scale, sc_conv_w=v_sc_conv_w, w_br_a=v_w_br_a, w_br_b=v_w_br_b,
                 w_br_c=v_w_br_c, w_o=v_w_o, w_gu=v_w_gu, w_down=v_w_down, final_norm_g=v_final_norm_g)
    my = _dev_index(_me())

    big_pack, big_offs = _pack([loc[k] for k, _ in BIG], BF16, BF16_ROWS)
    conv_pack, conv_offs = _pack([loc[k] for k in CONV], F32, 8)
    big_all, conv_all = _allgather([big_pack, conv_pack])
    full = {k: loc[k] for k in REPL}
    for (k, axis), blocks in zip(BIG, _unpack(big_all, big_offs, [loc[k].shape for k, _ in BIG], (N_DEV,))):
        full[k] = _join(blocks, axis)
    for k, blocks in zip(CONV, _unpack(conv_all, conv_offs, [loc[k].shape for k in CONV], (N_DEV,))):
        full[k] = _join(blocks, 2)

    loss8, grad_x, g = _device_step(x[0], c, ctx[0], loss_target[0], full, TOKEN_TILE)

    g2, g_offs = _pack_blocks([_split(g[k], axis) for k, axis in BIG])
    own, got = _pair_exchange(g2)
    g_sum = _sum_chips(_chip_exchange(_pair_sum(own, got)))
    big_grads = _unpack(g_sum, g_offs, [loc[k].shape for k, _ in BIG])

    small_names = REPL + CONV
    small_pack, small_offs = _pack([g[k] for k in small_names] + [loss8[0:1, 0:1]], F32, 8)
    small_sum = _sum_small(_broadcast_small(small_pack))
    sums = _unpack(small_sum, small_offs, [g[k].shape for k in small_names] + [(1, 1)])
    grads = dict(zip(small_names, sums[:-1]))
    loss = sums[-1][0, 0]
    for k in CONV:
        w = loc[k].shape[2]
        grads[k] = lax.dynamic_slice_in_dim(grads[k], my * w, w, axis=2)

    delta, new_m, new_v = {}, {}, {}
    for (k, _), gk in zip(BIG, big_grads):
        grads[k] = gk
        delta[k], new_m[k], new_v[k] = _adam_big(loc[k], gk, mom_m[k], mom_v[k])
    packs = [_pack([src[k] for k in small_names], F32, 8)[0] for src in (loc, grads, mom_m, mom_v)]
    _, offs = _pack([loc[k] for k in small_names], F32, 8)
    shapes = [loc[k].shape for k in small_names]
    for dst, packed in zip((delta, new_m, new_v), _adam_small(*packs)):
        dst.update(zip(small_names, _unpack(packed, offs, shapes)))

    return (loss, grad_x[None], *[grads[k] for k in WEIGHTS], *[delta[k] for k in WEIGHTS],
            *[new_m[k] for k in WEIGHTS], *[new_v[k] for k in WEIGHTS])
```

```python
import functools

import numpy as np
import jax
import jax.numpy as jnp
from jax import lax
from jax.experimental import pallas as pl
from jax.experimental.pallas import tpu as pltpu

F32 = jnp.float32
BF16 = jnp.bfloat16
HI = lax.Precision.HIGHEST

D = 1024
NL = 2
NH = 4
DH = 128
DN = NH * DH
CH = 64
GW = 64
PW = 256
DFF = 2816
EPS = 1e-6
N_DEV = 8
LANES = 128
BF16_ROWS = 16
VMEM_MB = 56

ADAM_LR, ADAM_B1, ADAM_B2, ADAM_EPS, ADAM_WD, ADAM_STEP = 0.001, 0.9, 0.999, 1e-08, 0.01, 10

IN_BOUNDS = (0, 1536, 2048, 2064, 2320, 2576, 2832, 3088, 6160)
IN_WIDTHS = (1536, 512, 128, 256, 256, 256, 256, 3072)
POOL_WIN = ((1, 0), (2, 1), (4, 3), (8, 7))

NN = ((1,), (0,))
NT = ((1,), (1,))
TN = ((0,), (0,))


def _dot(a, b, dims, hi=False):
    if hi:
        prec = lax.Precision.HIGH if hi == "x3" else HI
        return lax.dot_general(a, b, (dims, ((), ())), precision=prec, preferred_element_type=F32)
    return lax.dot_general(a.astype(BF16), b.astype(BF16), (dims, ((), ())), preferred_element_type=F32)


def _S(shape, dtype=F32):
    return jax.ShapeDtypeStruct(tuple(shape), dtype)


def _full(shape):
    nd = len(shape)
    return pl.BlockSpec(tuple(shape), lambda *_: (0,) * nd)


def _rows(tt, w):
    return pl.BlockSpec((tt, w), lambda i: (i, 0))


def _call(body, name, grid, in_specs, out_specs, out_shape, scratch=()):
    return pl.pallas_call(
        body, name=name, grid=grid, in_specs=in_specs, out_specs=out_specs, out_shape=out_shape,
        scratch_shapes=list(scratch),
        compiler_params=pltpu.CompilerParams(
            dimension_semantics=("arbitrary",) * len(grid), vmem_limit_bytes=VMEM_MB << 20),
    )


def _iota(shape, axis):
    return lax.broadcasted_iota(jnp.int32, shape, axis)


def _colsum(a):
    return jnp.sum(a, axis=0, keepdims=True)


def _silu(x):
    return x * jax.nn.sigmoid(x)


def _modulate(x, g, sh, sc):
    xn = x * lax.rsqrt(jnp.mean(x * x, axis=-1, keepdims=True) + EPS)
    return (xn * g) * (1.0 + sc) + sh


def _stream_rows(mv_ref, i, tt, tc, k):
    isc = (i * tt + _iota((tt, 1), 0)) < tc
    return isc, jnp.where(isc, mv_ref[k:k + 1, :], mv_ref[3 + k:4 + k, :])


def _acc_stream(ref, k, isc, val):
    ref[k:k + 1, :] += _colsum(jnp.where(isc, val, 0.0))
    ref[3 + k:4 + k, :] += _colsum(jnp.where(isc, 0.0, val))


MOD_CT = 1536


def _mod_fwd(cc8, w_ada, b_ada3):
    def body(cc_ref, w_ref, b_ref, o_ref):
        o_ref[0] = _dot(_silu(cc_ref[...]), w_ref[0], NN) + b_ref[0]

    return _call(
        body, "mod_fwd", (NL, 6 * D // MOD_CT),
        [pl.BlockSpec((8, D), lambda l, j: (0, 0)), pl.BlockSpec((1, D, MOD_CT), lambda l, j: (l, 0, j)),
         pl.BlockSpec((1, 1, MOD_CT), lambda l, j: (l, 0, j))],
        pl.BlockSpec((1, 8, MOD_CT), lambda l, j: (l, 0, j)), _S((NL, 8, 6 * D)))(cc8, w_ada, b_ada3)


def _mod_bwd(cc8, w_ada, dmods):
    def body(cc_ref, w_ref, dm_ref, dw_ref, db_ref, dcc_ref):
        first = (pl.program_id(0) == 0) & (pl.program_id(1) == 0)
        cc = cc_ref[...]
        sg = jax.nn.sigmoid(cc)
        dm = dm_ref[0]
        dw_ref[0] = _dot(cc * sg, dm, TN)
        db_ref[0] = dm[0:1, :] + dm[1:2, :]

        @pl.when(first)
        def _():
            dcc_ref[...] = jnp.zeros_like(dcc_ref)

        dcc_ref[...] += _dot(dm, w_ref[0], NT) * (sg * (1.0 + cc * (1.0 - sg)))

    return _call(
        body, "mod_bwd", (NL, 6 * D // MOD_CT),
        [pl.BlockSpec((8, D), lambda l, j: (0, 0)), pl.BlockSpec((1, D, MOD_CT), lambda l, j: (l, 0, j)),
         pl.BlockSpec((1, 8, MOD_CT), lambda l, j: (l, 0, j))],
        [pl.BlockSpec((1, D, MOD_CT), lambda l, j: (l, 0, j)), pl.BlockSpec((1, 1, MOD_CT), lambda l, j: (l, 0, j)),
         pl.BlockSpec((8, D), lambda l, j: (0, 0))],
        [_S((NL, D, 6 * D)), _S((NL, 1, 6 * D)), _S((8, D))])(cc8, w_ada, dmods)


def _inproj_fwd(X, mv, g, ws, tc, tt):
    T = X.shape[0]
    nw = len(ws)

    def body(x_ref, mv_ref, g_ref, *refs):
        w_refs, h_ref, p_refs = refs[:nw], refs[nw], refs[nw + 1:]
        i = pl.program_id(0)
        _, sh = _stream_rows(mv_ref, i, tt, tc, 0)
        _, sc = _stream_rows(mv_ref, i, tt, tc, 1)
        hb = _modulate(x_ref[...], g_ref[...], sh, sc).astype(BF16)
        h_ref[...] = hb
        for w_ref, p_ref in zip(w_refs, p_refs):
            p_ref[...] = jnp.dot(hb, w_ref[...], preferred_element_type=F32)

    return _call(
        body, "inproj_fwd", (T // tt,),
        [_rows(tt, D), _full((8, D)), _full((1, D))] + [_full(w.shape) for w in ws],
        [_rows(tt, D)] + [_rows(tt, w.shape[1]) for w in ws],
        [_S((T, D), BF16)] + [_S((T, w.shape[1])) for w in ws])(X, mv, g, *ws)


def _inproj_bwd(X, mv, g, ws, dps, dp_w, dres, tc, tt):
    T = X.shape[0]
    nw, nd = len(ws), len(dps)

    def body(x_ref, mv_ref, g_ref, dres_ref, *refs):
        w_refs, dp_refs = refs[:nw], refs[nw:nw + nd]
        dx_ref, dg_ref, dm_ref = refs[nw + nd:]
        i = pl.program_id(0)
        isc, sh = _stream_rows(mv_ref, i, tt, tc, 0)
        _, sc = _stream_rows(mv_ref, i, tt, tc, 1)
        dh = None
        for dp_ref, k in zip(dp_refs, dp_w):
            t = _dot(dp_ref[...], w_refs[k][...], NT)
            dh = t if dh is None else dh + t
        _, vjp = jax.vjp(_modulate, x_ref[...], g_ref[...], sh, sc)
        dx, dg, dsh, dsc = vjp(dh)
        dx_ref[...] = dres_ref[...] + dx

        @pl.when(i == 0)
        def _():
            dg_ref[...] = jnp.zeros_like(dg_ref)
            dm_ref[...] = jnp.zeros_like(dm_ref)

        dg_ref[...] += dg
        _acc_stream(dm_ref, 0, isc, dsh)
        _acc_stream(dm_ref, 1, isc, dsc)

    return _call(
        body, "inproj_bwd", (T // tt,),
        [_rows(tt, D), _full((8, D)), _full((1, D)), _rows(tt, D)] + [_full(w.shape) for w in ws]
        + [_rows(tt, dp.shape[1]) for dp in dps],
        [_rows(tt, D), _full((1, D)), _full((8, D))],
        [_S((T, D)), _S((1, D)), _S((8, D))])(X, mv, g, dres, *ws, *dps)


def _dw(A, B, tt):
    T, K = A.shape
    N = B.shape[1]
    tn = 512 if N % 512 == 0 else (256 if N % 256 == 0 else LANES)

    def body(a_ref, b_ref, o_ref):
        @pl.when(pl.program_id(1) == 0)
        def _():
            o_ref[...] = jnp.zeros_like(o_ref)

        o_ref[...] += _dot(a_ref[...], b_ref[...], TN)

    return _call(
        body, "dw", (N // tn, T // tt),
        [pl.BlockSpec((tt, K), lambda j, i: (i, 0)), pl.BlockSpec((tt, tn), lambda j, i: (i, j))],
        pl.BlockSpec((K, tn), lambda j, i: (0, j)), _S((K, N)))(A, B)


def _halo_specs(T, tt, cw, col):
    r8, nb8 = tt // 8, T // 8
    return [pl.BlockSpec((tt, cw), lambda j, i: (i, col(j))),
            pl.BlockSpec((8, cw), lambda j, i: (jnp.maximum(i * r8 - 1, 0), col(j))),
            pl.BlockSpec((8, cw), lambda j, i: (jnp.minimum((i + 1) * r8, nb8 - 1), col(j)))]


def _shifts(a, prev8, next8, i, tt, tc, T):
    r = _iota((tt, 1), 0)
    t = i * tt + r
    dn = jnp.where(r == 0, prev8[7:8, :], pltpu.roll(a, 1, 0))
    dn = jnp.where((t == 0) | (t == tc), 0.0, dn)
    up = jnp.where(r == tt - 1, next8[0:1, :], pltpu.roll(a, tt - 1, 0))
    up = jnp.where((t == T - 1) | (t == tc - 1), 0.0, up)
    return dn, up


def _dn_post(y, part):
    a = _silu(y)
    nrm = lax.rsqrt(jnp.sum(a * a, axis=-1, keepdims=True) + EPS)
    f = jnp.where(part == 0, nrm * (DH ** -0.5), jnp.where(part == 1, nrm, 1.0))
    return a * f


def _conv3(w_ref, dn, mid, up):
    return w_ref[0:1, :] * dn + w_ref[1:2, :] * mid + w_ref[2:3, :] * up


def _dnprep_fwd(pq, cw, tc, tt):
    T = pq.shape[0]

    def body(p_ref, pp_ref, pn_ref, w_ref, a_ref):
        part, i = pl.program_id(0), pl.program_id(1)
        p = p_ref[...]
        dn, up = _shifts(p, pp_ref[...], pn_ref[...], i, tt, tc, T)
        y = _conv3(w_ref, dn, p, up)
        for h in range(NH):
            a_ref[:, _hs(h)] = _dn_post(y[:, _hs(h)], part)

    return _call(
        body, "dnprep_fwd", (3, T // tt),
        _halo_specs(T, tt, DN, lambda j: j) + [pl.BlockSpec((3, DN), lambda j, i: (0, j))],
        pl.BlockSpec((tt, DN), lambda j, i: (i, j)), _S((T, 3 * DN)))(pq, pq, pq, cw)


def _dnprep_bwd_act(pq, cw, da_f, da_b, tc, tt):
    T = pq.shape[0]

    def body(p_ref, pp_ref, pn_ref, w_ref, df_ref, db_ref, dy_ref):
        part, i = pl.program_id(0), pl.program_id(1)
        p = p_ref[...]
        dn, up = _shifts(p, pp_ref[...], pn_ref[...], i, tt, tc, T)
        y = _conv3(w_ref, dn, p, up)
        for h in range(NH):
            _, vjp = jax.vjp(lambda yh: _dn_post(yh, part), y[:, _hs(h)])
            dy_ref[:, _hs(h)] = vjp(df_ref[:, _hs(h)] + db_ref[:, _hs(h)])[0]

    blk = pl.BlockSpec((tt, DN), lambda j, i: (i, j))
    return _call(
        body, "dnprep_bwd_act", (3, T // tt),
        _halo_specs(T, tt, DN, lambda j: j) + [pl.BlockSpec((3, DN), lambda j, i: (0, j)), blk, blk],
        blk, _S((T, 3 * DN)))(pq, pq, pq, cw, da_f, da_b)


def _conv_bwd(dy, p, cw, tc, tt):
    T, W = p.shape
    cb = DN

    def body(dy_ref, dyp_ref, dyn_ref, p_ref, pp_ref, pn_ref, w_ref, dp_ref, dw_ref):
        i = pl.program_id(1)
        dy, p_ = dy_ref[...], p_ref[...]
        ddn, dup = _shifts(dy, dyp_ref[...], dyn_ref[...], i, tt, tc, T)
        dp_ref[...] = _conv3(w_ref, dup, dy, ddn)
        pdn, pup = _shifts(p_, pp_ref[...], pn_ref[...], i, tt, tc, T)

        @pl.when(i == 0)
        def _():
            dw_ref[...] = jnp.zeros_like(dw_ref)

        dw_ref[0:1, :] += _colsum(dy * pdn)
        dw_ref[1:2, :] += _colsum(dy * p_)
        dw_ref[2:3, :] += _colsum(dy * pup)

    wspec = pl.BlockSpec((3, cb), lambda j, i: (0, j))
    return _call(
        body, "conv_bwd", (W // cb, T // tt),
        _halo_specs(T, tt, cb, lambda j: j) * 2 + [wspec],
        [pl.BlockSpec((tt, cb), lambda j, i: (i, j)), wspec], [_S((T, W)), _S((3, W))])(dy, dy, dy, p, p, p, cw)


def _sc_fwd(sx, sb, sc_, cw, tc, tt):
    T = sx.shape[0]

    def body(x_ref, xp_ref, xn_ref, c_ref, cp_ref, cn_ref, b_ref, w_ref, y_ref):
        i = pl.program_id(1)
        u = c_ref[...] * x_ref[...]
        dn, up = _shifts(u, cp_ref[...] * xp_ref[...], cn_ref[...] * xn_ref[...], i, tt, tc, T)
        y_ref[...] = b_ref[...] * _conv3(w_ref, dn, u, up)

    blk = pl.BlockSpec((tt, LANES), lambda j, i: (i, j))
    return _call(
        body, "sc_fwd", (PW // LANES, T // tt),
        _halo_specs(T, tt, LANES, lambda j: j) * 2 + [blk, pl.BlockSpec((3, LANES), lambda j, i: (0, j))],
        blk, _S((T, PW)))(sx, sx, sx, sc_, sc_, sc_, sb, cw)


def _sc_bwd(sx, sb, sc_, cw, dy, tc, tt):
    T = sx.shape[0]

    def body(x_ref, xp_ref, xn_ref, c_ref, cp_ref, cn_ref, b_ref, bp_ref, bn_ref, dy_ref, dyp_ref, dyn_ref, w_ref,
             dx_ref, db_ref, dc_ref, dw_ref):
        i = pl.program_id(1)
        x, c, dy_ = x_ref[...], c_ref[...], dy_ref[...]
        u = c * x
        udn, uup = _shifts(u, cp_ref[...] * xp_ref[...], cn_ref[...] * xn_ref[...], i, tt, tc, T)
        db_ref[...] = dy_ * _conv3(w_ref, udn, u, uup)
        e = dy_ * b_ref[...]
        edn, eup = _shifts(e, dyp_ref[...] * bp_ref[...], dyn_ref[...] * bn_ref[...], i, tt, tc, T)
        du = _conv3(w_ref, eup, e, edn)
        dx_ref[...] = du * c
        dc_ref[...] = du * x

        @pl.when(i == 0)
        def _():
            dw_ref[...] = jnp.zeros_like(dw_ref)

        dw_ref[0:1, :] += _colsum(e * udn)
        dw_ref[1:2, :] += _colsum(e * u)
        dw_ref[2:3, :] += _colsum(e * uup)

    blk = pl.BlockSpec((tt, LANES), lambda j, i: (i, j))
    wspec = pl.BlockSpec((3, LANES), lambda j, i: (0, j))
    return _call(
        body, "sc_bwd", (PW // LANES, T // tt),
        _halo_specs(T, tt, LANES, lambda j: j) * 4 + [wspec],
        [blk, blk, blk, wspec], [_S((T, PW))] * 3 + [_S((3, PW))])(
            sx, sx, sx, sc_, sc_, sc_, sb, sb, sb, dy, dy, dy, cw)


def _group_select(vals):
    g = _iota((1, PW), 1) // (PW // len(POOL_WIN))
    return jnp.where(g == 0, vals[0], jnp.where(g == 1, vals[1], jnp.where(g == 2, vals[2], vals[3])))


def _nested_box(get, mirror):
    acc, outs, pl_, ph_ = get(0), [], 0, 0
    for lo, hi in POOL_WIN:
        if mirror:
            lo, hi = hi, lo
        for k in range(pl_ + 1, lo + 1):
            acc = acc + get(-k)
        for k in range(ph_ + 1, hi + 1):
            acc = acc + get(k)
        pl_, ph_ = lo, hi
        outs.append(acc)
    return _group_select(outs)


def _box_tokens(a, n, mirror):
    idx = _iota((n, 1), 0)

    def get(k):
        if k == 0:
            return a
        return jnp.where((idx + k >= 0) & (idx + k < n), pltpu.roll(a, (-k) % n, 0), 0.0)

    return _nested_box(get, mirror)


def _inv_count(pos, n):
    return _group_select([1.0 / (jnp.minimum(pos + hi, n - 1) - jnp.maximum(pos - lo, 0) + 1).astype(F32)
                          for lo, hi in POOL_WIN])


def _pool_rows(ref, r, R, tc, mirror):
    def get(k):
        rr = r + k
        rc = jnp.clip(rr, 0, R - 1)
        v = ref[pl.ds(pl.multiple_of(tc + rc * GW, GW), GW), :]
        if mirror:
            v = v * _inv_count(jnp.full((1, PW), rc, jnp.int32), R)
        return jnp.where((rr >= 0) & (rr < R), v, 0.0)

    return _nested_box(get, mirror)


def _pool_fwd(u, pwbd, ps, tc):
    T = u.shape[0]
    R = (T - tc) // GW

    def body(u_ref, pw_ref, ps_ref, y_ref):
        pw, scale = pw_ref[...], ps_ref[...]
        uc = u_ref[0:tc, :]
        mc = _box_tokens(uc, tc, False) * _inv_count(_iota((tc, 1), 0), tc)
        y_ref[0:tc, :] = _dot(mc - uc, pw, NN) * scale
        inv_c = _inv_count(_iota((GW, 1), 0), GW)

        def row(r, carry):
            rs = _pool_rows(u_ref, r, R, tc, False) * _inv_count(jnp.full((1, PW), r, jnp.int32), R)
            m = _box_tokens(rs, GW, False) * inv_c
            sl = pl.ds(pl.multiple_of(tc + r * GW, GW), GW)
            y_ref[sl, :] = _dot(m - u_ref[sl, :], pw, NN) * scale
            return carry

        lax.fori_loop(0, R, row, 0)

    return pl.pallas_call(
        body, name="pool_fwd", out_shape=_S((T, PW)),
        compiler_params=pltpu.CompilerParams(vmem_limit_bytes=VMEM_MB << 20))(u, pwbd, ps)


def _pool_bwd(u, pwbd, ps, dy, tc):
    T = u.shape[0]
    R = (T - tc) // GW

    def body(u_ref, pw_ref, ps_ref, dy_ref, du_ref, dpw_ref, dps_ref, dd_ref):
        pw, scale = pw_ref[...], ps_ref[...]
        dpw_ref[...] = jnp.zeros_like(dpw_ref)
        dps_ref[...] = jnp.zeros_like(dps_ref)

        def back(d, dy_):
            dz = dy_ * scale
            dpw_ref[...] += _dot(d, dz, TN)
            dps_ref[...] += _colsum(dy_ * _dot(d, pw, NN))
            return _dot(dz, pw, NT)

        uc = u_ref[0:tc, :]
        inv_cc = _inv_count(_iota((tc, 1), 0), tc)
        ddc = back(_box_tokens(uc, tc, False) * inv_cc - uc, dy_ref[0:tc, :])
        du_ref[0:tc, :] = _box_tokens(ddc * inv_cc, tc, True) - ddc
        inv_c = _inv_count(_iota((GW, 1), 0), GW)

        def row1(r, carry):
            rs = _pool_rows(u_ref, r, R, tc, False) * _inv_count(jnp.full((1, PW), r, jnp.int32), R)
            m = _box_tokens(rs, GW, False) * inv_c
            sl = pl.ds(pl.multiple_of(tc + r * GW, GW), GW)
            dd_ref[sl, :] = back(m - u_ref[sl, :], dy_ref[sl, :])
            return carry

        lax.fori_loop(0, R, row1, 0)

        def row2(r, carry):
            t1 = _pool_rows(dd_ref, r, R, tc, True)
            sl = pl.ds(pl.multiple_of(tc + r * GW, GW), GW)
            du_ref[sl, :] = _box_tokens(t1 * inv_c, GW, True) - dd_ref[sl, :]
            return carry

        lax.fori_loop(0, R, row2, 0)

    return pl.pallas_call(
        body, name="pool_bwd", out_shape=[_S((T, PW)), _S((PW, PW)), _S((1, PW))],
        scratch_shapes=[pltpu.VMEM((T, PW), F32)],
        compiler_params=pltpu.CompilerParams(vmem_limit_bytes=VMEM_MB << 20))(u, pwbd, ps, dy)


def _scan_consts():
    i = np.arange(CH)
    lower = (i[:, None] >= i[None, :]).astype(np.float32)
    return jnp.asarray(np.stack([lower, lower.T])), jnp.asarray(np.stack([lower.T, lower]))


def _gates(pab, al, dtb, csum):
    sp_in = pab + dtb
    sp = jnp.maximum(sp_in, 0.0) + jnp.log(1.0 + jnp.exp(-jnp.abs(sp_in)))
    nexp = -jnp.exp(al)
    gm = nexp * sp
    return gm, jax.nn.sigmoid(pab), _dot(csum, gm, NN, hi=True), sp_in, nexp


def _lane_col(m, j):
    return jnp.sum(jnp.where(_iota(m.shape, 1) == j, m, 0.0), axis=1, keepdims=True)


def _hs(h):
    return slice(h * DH, (h + 1) * DH)


HS = NH * CH
X3 = "x3"


def _stack(x, base=0):
    return jnp.concatenate([x[:, base + h * DH:base + (h + 1) * DH] for h in range(NH)], axis=0)


def _heads(st):
    return [st[h * CH:(h + 1) * CH] for h in range(NH)]


def _rowsum(a):
    return jnp.sum(a, axis=1, keepdims=True)


def _row_of(col):
    e0 = (_iota((8, LANES), 1) == 0).astype(F32)
    return _dot(e0, jnp.broadcast_to(col, (HS, LANES)), NT, hi=True)[0:1, :]


def _dn_chunk(qkv, pab, al, dtb, csum_d, d, tm=None):
    gm, bm, gcm, sp_in, nexp = _gates(pab, al, dtb, csum_d)
    gc = jnp.concatenate([_lane_col(gcm, d * NH + h) for h in range(NH)], axis=0)
    beta = jnp.concatenate([_lane_col(bm, 8 + d * NH + h) for h in range(NH)], axis=0)
    q, k, v = _stack(qkv, 0), _stack(qkv, DN), _stack(qkv, 2 * DN)
    ii, jj = _iota((HS, HS), 0), _iota((HS, HS), 1)
    sh = CH.bit_length() - 1
    same = (ii >> sh) == (jj >> sh)
    incl = same & ((ii >= jj) if d == 0 else (ii <= jj))
    strict = same & ((ii > jj) if d == 0 else (ii < jj))
    last = CH - 1 if d == 0 else 0
    Di = jnp.where(incl, jnp.exp(jnp.where(incl, gc - _row_of(gc), 0.0)), 0.0)
    Ds = jnp.where(strict, Di, 0.0)
    kb = k * beta
    kk = _dot(kb, k, NT)
    if tm is None:
        nm = -(kk * Ds)
        eye = (ii == jj).astype(F32)
        x0 = eye + nm
        mp = nm
        for _ in range(5):
            mp = _dot(mp, mp, NN)
            x0 = x0 + _dot(x0, mp, NN)
        tm = x0 + _dot(x0, eye - _dot(eye - nm, x0, NN, hi=X3), NN)
    E = jnp.exp(gc)
    gls = [gc[h * CH + last:h * CH + last + 1, :] for h in range(NH)]
    xs = jnp.exp(jnp.concatenate([jnp.broadcast_to(g, (CH, 1)) for g in gls], axis=0) - gc)
    qk = _dot(q, k, NT)
    return dict(q=q, k=k, v=v, beta=beta, gm=gm, bm=bm, sp_in=sp_in, nexp=nexp, Di=Di, Ds=Ds, strict=strict, last=last,
                kb=kb, kk=kk, tm=tm, E=E, gls=gls, xs=xs, qk=qk, u=_dot(tm, v * beta, NN, hi=X3),
                w=_dot(tm, kb * E, NN, hi=X3), ks=k * xs, qd=q * E, aqk=qk * Di)


def _dn_chunk_bwd(c, S, dS2, do, vn, dvn):
    q, k, v, beta, tm, E, xs, kb, u, w = (c[n] for n in ("q", "k", "v", "beta", "tm", "E", "xs", "kb", "u", "w"))
    doh, vnh, dvnh = _heads(do), _heads(vn), _heads(dvn)
    dqd = jnp.concatenate([_dot(doh[h], S[h], NT) for h in range(NH)], axis=0)
    dks = jnp.concatenate([_dot(vnh[h], dS2[h], NT) for h in range(NH)], axis=0)
    dw = -jnp.concatenate([_dot(dvnh[h], S[h], NT) for h in range(NH)], axis=0)
    daqk = _dot(do, vn, NT)
    drb = _dot(tm, dvn, TN, hi=X3)
    drw = _dot(tm, dw, TN, hi=X3)
    dA = jnp.where(c["strict"], -(_dot(drb, u, NT) + _dot(drw, w, NT)), 0.0)
    dM1 = dA * c["Ds"]
    dM2 = daqk * c["Di"]
    dkb = _dot(dM1, k, NN) + drw * E
    dk = _dot(dM1, kb, TN) + _dot(dM2, q, TN) + dks * xs
    dq = _dot(dM2, k, NN) + dqd * E
    G = dM1 * c["kk"] + dM2 * c["qk"]
    on_diag = _iota((HS, HS), 0) == _iota((HS, HS), 1)
    col = _rowsum(jnp.where(on_diag, jnp.broadcast_to(_colsum(G), (HS, HS)), 0.0))
    dxx = _rowsum(dks * k) * xs
    dgc = _rowsum(G) - col + (_rowsum(dqd * q) + _rowsum(drw * kb)) * E - dxx
    at_last = _iota((CH, 1), 0) == c["last"]
    ends = []
    for h in range(NH):
        dgl = _colsum(_rowsum(S[h] * dS2[h])) * jnp.exp(c["gls"][h]) + _colsum(dxx[h * CH:(h + 1) * CH])
        ends.append(jnp.where(at_last, dgl, 0.0))
    dgc = dgc + jnp.concatenate(ends, axis=0)
    dbeta = _rowsum(drb * v) + _rowsum(dkb * k)
    return dq, dk + dkb * beta, drb * beta, dgc, dbeta


def _chunk_group(n):
    return 2 if n % 2 == 0 else 1


def _dn_chunks_fwd(qkv, pab, alr, dtr):
    T = qkv.shape[0]
    n = T // CH
    G = _chunk_group(n)
    csum, _ = _scan_consts()

    def body(q_ref, p_ref, cs_ref, al_ref, dt_ref, *outs):
        for g in range(G):
            tok, rows = slice(g * CH, (g + 1) * CH), slice(g * HS, (g + 1) * HS)
            for d in range(2):
                u_ref, w_ref, ks_ref, qd_ref, aqk_ref, eg_ref, tm_ref = outs[7 * d:7 * d + 7]
                c = _dn_chunk(q_ref[tok, :], p_ref[tok, :], al_ref[...], dt_ref[...], cs_ref[d], d)
                tm_ref[rows, :] = c["tm"]
                u_ref[rows, :] = c["u"]
                w_ref[rows, :] = c["w"].astype(BF16)
                ks_ref[rows, :] = c["ks"].astype(BF16)
                qd_ref[rows, :] = c["qd"].astype(BF16)
                aqk_ref[rows, :] = c["aqk"].astype(BF16)
                egs = [jnp.broadcast_to(jnp.exp(gl), (1, LANES)) for gl in c["gls"]]
                eg_ref[g * 8:(g + 1) * 8, :] = jnp.concatenate(egs + [jnp.zeros((8 - NH, LANES), F32)], axis=0)

    st = lambda w_: pl.BlockSpec((G * HS, w_), lambda i: (i, 0))
    one = [st(DH)] * 4 + [st(HS), pl.BlockSpec((G * 8, LANES), lambda i: (i, 0)), st(HS)]
    shp = [_S((n * HS, DH)), _S((n * HS, DH), BF16), _S((n * HS, DH), BF16), _S((n * HS, DH), BF16),
           _S((n * HS, HS), BF16), _S((n * 8, LANES)), _S((n * HS, HS))]
    outs = _call(
        body, "dn_chunks_fwd", (n // G,),
        [_rows(G * CH, 3 * DN), _rows(G * CH, LANES), _full((2, CH, CH)), _full((1, LANES)), _full((1, LANES))],
        one * 2, shp * 2)(qkv, pab, csum, alr, dtr)
    return tuple(outs[:7]), tuple(outs[7:])


def _scan_order(n, ncx):
    return (lambda i: i), (lambda i: jnp.where(i < ncx, ncx - 1 - i, n - 1 - (i - ncx)))


def _scan_specs(order):
    st = lambda w_: pl.BlockSpec((HS, w_), lambda i: (order(i), 0))
    return dict(st=st(DH), aqk=st(HS), eg=pl.BlockSpec((8, LANES), lambda i: (order(i), 0)),
                tok=pl.BlockSpec((CH, DN), lambda i: (order(i), 0)), state=pl.BlockSpec((1, DN, DH), lambda i: (order(i), 0, 0)))


def _scan_fwd(parts, T, tc):
    n = T // CH
    orders = _scan_order(n, tc // CH)

    def body(*refs):
        S_f, S_b = refs[-2:]

        @pl.when(pl.program_id(0) == 0)
        def _():
            S_f[...] = jnp.zeros_like(S_f)
            S_b[...] = jnp.zeros_like(S_b)

        for d, S in enumerate((S_f, S_b)):
            u_ref, w_ref, ks_ref, qd_ref, aqk_ref, eg_ref = refs[6 * d:6 * d + 6]
            o_ref, ss_ref, vn_ref = refs[12 + 3 * d:15 + 3 * d]
            ss_ref[0] = S[...]
            Sh = [S[_hs(h), :] for h in range(NH)]
            wh, ksh, qdh = _heads(w_ref[...]), _heads(ks_ref[...]), _heads(qd_ref[...])
            vn = u_ref[...] - jnp.concatenate([_dot(wh[h], Sh[h], NN) for h in range(NH)], axis=0)
            vn_ref[...] = vn
            av, vnh = _heads(_dot(aqk_ref[...], vn, NN)), _heads(vn)
            for h in range(NH):
                o_ref[:, _hs(h)] = _dot(qdh[h], Sh[h], NN) + av[h]
                S[_hs(h), :] = Sh[h] * eg_ref[h:h + 1, :] + _dot(ksh[h], vnh[h], TN)

    ins, outs, shp = [], [], []
    for d in range(2):
        sp = _scan_specs(orders[d])
        ins += [sp["st"]] * 4 + [sp["aqk"], sp["eg"]]
        outs += [sp["tok"], sp["state"], sp["st"]]
        shp += [_S((T, DN)), _S((n, DN, DH)), _S((n * HS, DH))]
    res = _call(body, "scan_fwd", (n,), ins, outs, shp,
                scratch=[pltpu.VMEM((DN, DH), F32), pltpu.VMEM((DN, DH), F32)])(*parts[0][:6], *parts[1][:6])
    return tuple(res[:3]), tuple(res[3:])


def _scan_bwd(do, parts, tc):
    T = do.shape[0]
    n = T // CH
    fwd_orders = _scan_order(n, tc // CH)
    orders = [lambda s, f=f: f(n - 1 - s) for f in fwd_orders]

    def body(*refs):
        dS_f, dS_b = refs[-2:]

        @pl.when(pl.program_id(0) == 0)
        def _():
            dS_f[...] = jnp.zeros_like(dS_f)
            dS_b[...] = jnp.zeros_like(dS_b)

        for d, dS in enumerate((dS_f, dS_b)):
            do_ref, w_ref, ks_ref, qd_ref, aqk_ref, eg_ref = refs[6 * d:6 * d + 6]
            dvn_ref, dss_ref = refs[12 + 2 * d:14 + 2 * d]
            dss_ref[0] = dS[...]
            dSh = [dS[_hs(h), :] for h in range(NH)]
            wh, ksh, qdh = _heads(w_ref[...]), _heads(ks_ref[...]), _heads(qd_ref[...])
            do_st = _stack(do_ref[...])
            dvn = _dot(aqk_ref[...], do_st, TN) + jnp.concatenate([_dot(ksh[h], dSh[h], NN) for h in range(NH)], axis=0)
            dvn_ref[...] = dvn
            doh, dvnh = _heads(do_st), _heads(dvn)
            for h in range(NH):
                dS[_hs(h), :] = _dot(qdh[h], doh[h], TN) + dSh[h] * eg_ref[h:h + 1, :] - _dot(wh[h], dvnh[h], TN)

    ins, outs, shp, args = [], [], [], []
    for d in range(2):
        sp = _scan_specs(orders[d])
        ins += [sp["tok"]] + [sp["st"]] * 3 + [sp["aqk"], sp["eg"]]
        outs += [sp["st"], sp["state"]]
        shp += [_S((n * HS, DH)), _S((n, DN, DH))]
        args += [do, *parts[d][1:6]]
    res = _call(body, "scan_bwd", (n,), ins, outs, shp,
                scratch=[pltpu.VMEM((DN, DH), F32), pltpu.VMEM((DN, DH), F32)])(*args)
    return tuple(res[:2]), tuple(res[2:])


def _dn_chunks_bwd(qkv, pab, alr, dtr, do, fwd, bwd):
    T = qkv.shape[0]
    n = T // CH
    G = _chunk_group(n)
    csum, csum_t = _scan_consts()

    def body(q_ref, p_ref, do_ref, cs_ref, cst_ref, al_ref, dt_ref, *refs):
        dq_refs, dp_refs, acc_ref = refs[10:12], refs[12:14], refs[14]

        @pl.when(pl.program_id(0) == 0)
        def _():
            acc_ref[...] = jnp.zeros_like(acc_ref)

        lane = _iota((CH, LANES), 1)
        for g in range(G):
            tok, rows = slice(g * CH, (g + 1) * CH), slice(g * HS, (g + 1) * HS)
            do_st = _stack(do_ref[tok, :])
            for d in range(2):
                vn_ref, dvn_ref, ss_ref, dss_ref, tm_ref = refs[5 * d:5 * d + 5]
                c = _dn_chunk(q_ref[tok, :], p_ref[tok, :], al_ref[...], dt_ref[...], cs_ref[d], d, tm=tm_ref[rows, :])
                dq, dk, dv, dgc, dbeta = _dn_chunk_bwd(
                    c, [ss_ref[g, _hs(h), :] for h in range(NH)], [dss_ref[g, _hs(h), :] for h in range(NH)],
                    do_st, vn_ref[rows, :], dvn_ref[rows, :])
                dgcm = jnp.zeros((CH, LANES), F32)
                dbm = jnp.zeros((CH, LANES), F32)
                for h, (a, b_, c_, e, f) in enumerate(zip(*map(_heads, (dq, dk, dv, dgc, dbeta)))):
                    dq_refs[d][tok, _hs(h)] = a
                    dq_refs[d][tok, _hs(NH + h)] = b_
                    dq_refs[d][tok, _hs(2 * NH + h)] = c_
                    dgcm = jnp.where(lane == d * NH + h, e, dgcm)
                    dbm = jnp.where(lane == 8 + d * NH + h, f, dbm)
                dgm = _dot(cst_ref[d], dgcm, NN, hi=True)
                dsp = dgm * c["nexp"] * jax.nn.sigmoid(c["sp_in"])
                dp_refs[d][tok, :] = dsp + dbm * c["bm"] * (1.0 - c["bm"])
                acc_ref[0:1, :] += _colsum(dgm * c["gm"])
                acc_ref[1:2, :] += _colsum(dsp)

    st = pl.BlockSpec((G * HS, DH), lambda i: (i, 0))
    state = pl.BlockSpec((G, DN, DH), lambda i: (i, 0, 0))
    return _call(
        body, "dn_chunks_bwd", (n // G,),
        [_rows(G * CH, 3 * DN), _rows(G * CH, LANES), _rows(G * CH, DN), _full((2, CH, CH)), _full((2, CH, CH)),
         _full((1, LANES)), _full((1, LANES))] + [st, st, state, state, pl.BlockSpec((G * HS, HS), lambda i: (i, 0))] * 2,
        [_rows(G * CH, 3 * DN)] * 2 + [_rows(G * CH, LANES)] * 2 + [_full((8, LANES))],
        [_S((T, 3 * DN))] * 2 + [_S((T, LANES))] * 2 + [_S((8, LANES))])(
            qkv, pab, do, csum, csum_t, alr, dtr, *fwd, *bwd)


def _head_out(o, z, g):
    on = o * lax.rsqrt(jnp.mean(o * o, axis=-1, keepdims=True) + EPS) * g
    return on * _silu(z)


def _mix_branches(of_ref, ob_ref, z_ref, yp_ref, ys_ref, pg_ref, gdn_ref, wa_ref, wb_ref, wc_ref):
    ons, ya = [], None
    for h in range(NH):
        on = _head_out(of_ref[:, _hs(h)] + ob_ref[:, _hs(h)], z_ref[:, _hs(h)], gdn_ref[...])
        t = _dot(on, wa_ref[_hs(h), :], NN)
        ya = t if ya is None else ya + t
        ons.append(on)
    ys = [ya, _dot(yp_ref[...], wb_ref[...], NN), _dot(ys_ref[...], wc_ref[...], NN)]
    sg = [jax.nn.sigmoid(pg_ref[:, k * D:(k + 1) * D]) for k in range(3)]
    return ons, ys, sg


def _mix_fwd(X, of, ob, z, yp, ys, pg, mv, gdn, wa, wb, wc, wo, tc, tt):
    T = X.shape[0]

    def body(x_ref, of_ref, ob_ref, z_ref, yp_ref, ys_ref, pg_ref, mv_ref, gdn_ref, wa_ref, wb_ref, wc_ref, wo_ref,
             x1_ref):
        _, yb, sg = _mix_branches(of_ref, ob_ref, z_ref, yp_ref, ys_ref, pg_ref, gdn_ref, wa_ref, wb_ref, wc_ref)
        mix = _dot(sg[0] * yb[0] + sg[1] * yb[1] + sg[2] * yb[2], wo_ref[...], NN)
        _, gate = _stream_rows(mv_ref, pl.program_id(0), tt, tc, 2)
        x1_ref[...] = x_ref[...] + gate * mix

    return _call(
        body, "mix_fwd", (T // tt,),
        [_rows(tt, D), _rows(tt, DN), _rows(tt, DN), _rows(tt, DN), _rows(tt, PW), _rows(tt, PW), _rows(tt, 3 * D),
         _full((8, D)), _full((1, DH)), _full(wa.shape), _full(wb.shape), _full(wc.shape), _full(wo.shape)],
        _rows(tt, D), _S((T, D)))(X, of, ob, z, yp, ys, pg, mv, gdn, wa, wb, wc, wo)


def _mix_bwd(dx1, of, ob, z, yp, ys, pg, mv, gdn, wa, wb, wc, wo, tc, tt):
    T = dx1.shape[0]

    def body(dx_ref, of_ref, ob_ref, z_ref, yp_ref, ys_ref, pg_ref, mv_ref, gdn_ref, wa_ref, wb_ref, wc_ref, wo_ref,
             do_ref, dz_ref, dyp_ref, dys_ref, dpg_ref, dwa_ref, dwb_ref, dwc_ref, dwo_ref, dgdn_ref, dm_ref):
        i = pl.program_id(0)

        @pl.when(i == 0)
        def _():
            for r in (dwa_ref, dwb_ref, dwc_ref, dwo_ref, dgdn_ref, dm_ref):
                r[...] = jnp.zeros_like(r)

        ons, yb, sg = _mix_branches(of_ref, ob_ref, z_ref, yp_ref, ys_ref, pg_ref, gdn_ref, wa_ref, wb_ref, wc_ref)
        ymix = sg[0] * yb[0] + sg[1] * yb[1] + sg[2] * yb[2]
        isc, gate = _stream_rows(mv_ref, i, tt, tc, 2)
        dx = dx_ref[...]
        dmix = dx * gate
        _acc_stream(dm_ref, 2, isc, dx * _dot(ymix, wo_ref[...], NN))
        dwo_ref[...] += _dot(ymix, dmix, TN)
        dymix = _dot(dmix, wo_ref[...], NT)
        dyb = []
        for k in range(3):
            dyb.append(dymix * sg[k])
            dpg_ref[:, k * D:(k + 1) * D] = dymix * yb[k] * sg[k] * (1.0 - sg[k])
        dwb_ref[...] += _dot(yp_ref[...], dyb[1], TN)
        dwc_ref[...] += _dot(ys_ref[...], dyb[2], TN)
        dyp_ref[...] = _dot(dyb[1], wb_ref[...], NT)
        dys_ref[...] = _dot(dyb[2], wc_ref[...], NT)
        dg = jnp.zeros((1, DH), F32)
        for h in range(NH):
            dwa_ref[_hs(h), :] += _dot(ons[h], dyb[0], TN)
            don = _dot(dyb[0], wa_ref[_hs(h), :], NT)
            _, vjp = jax.vjp(_head_out, of_ref[:, _hs(h)] + ob_ref[:, _hs(h)], z_ref[:, _hs(h)], gdn_ref[...])
            do_h, dz_h, dg_h = vjp(don)
            do_ref[:, _hs(h)] = do_h
            dz_ref[:, _hs(h)] = dz_h
            dg = dg + dg_h
        dgdn_ref[...] += dg

    return _call(
        body, "mix_bwd", (T // tt,),
        [_rows(tt, D), _rows(tt, DN), _rows(tt, DN), _rows(tt, DN), _rows(tt, PW), _rows(tt, PW), _rows(tt, 3 * D),
         _full((8, D)), _full((1, DH)), _full(wa.shape), _full(wb.shape), _full(wc.shape), _full(wo.shape)],
        [_rows(tt, DN), _rows(tt, DN), _rows(tt, PW), _rows(tt, PW), _rows(tt, 3 * D),
         _full(wa.shape), _full(wb.shape), _full(wc.shape), _full(wo.shape), _full((1, DH)), _full((8, D))],
        [_S((T, DN)), _S((T, DN)), _S((T, PW)), _S((T, PW)), _S((T, 3 * D)),
         _S(wa.shape), _S(wb.shape), _S(wc.shape), _S(wo.shape), _S((1, DH)), _S((8, D))])(
            dx1, of, ob, z, yp, ys, pg, mv, gdn, wa, wb, wc, wo)


def _ffn_fwd(X1, mv, g, wgu, wd, tc, tt):
    T = X1.shape[0]

    def body(x_ref, mv_ref, g_ref, wgu_ref, wd_ref, x2_ref):
        i = pl.program_id(0)
        _, sh = _stream_rows(mv_ref, i, tt, tc, 0)
        _, sc = _stream_rows(mv_ref, i, tt, tc, 1)
        _, gate = _stream_rows(mv_ref, i, tt, tc, 2)
        x = x_ref[...]
        gu = _dot(_modulate(x, g_ref[...], sh, sc), wgu_ref[...], NN)
        x2_ref[...] = x + gate * _dot(_silu(gu[:, :DFF]) * gu[:, DFF:], wd_ref[...], NN)

    return _call(
        body, "ffn_fwd", (T // tt,),
        [_rows(tt, D), _full((8, D)), _full((1, D)), _full(wgu.shape), _full(wd.shape)],
        _rows(tt, D), _S((T, D)))(X1, mv, g, wgu, wd)


def _ffn_bwd(X1, dx2, mv, g, wgu, wd, tc, tt):
    T = X1.shape[0]

    def body(x_ref, dx2_ref, mv_ref, g_ref, wgu_ref, wd_ref, dx1_ref, h_ref, dgu_ref, act_ref, dff_ref, dg_ref, dm_ref):
        i = pl.program_id(0)
        isc, sh = _stream_rows(mv_ref, i, tt, tc, 0)
        _, sc = _stream_rows(mv_ref, i, tt, tc, 1)
        _, gate = _stream_rows(mv_ref, i, tt, tc, 2)
        x, dx2_ = x_ref[...], dx2_ref[...]
        h, vjp = jax.vjp(_modulate, x, g_ref[...], sh, sc)
        hb = h.astype(BF16)
        h_ref[...] = hb
        gu = jnp.dot(hb, wgu_ref[...], preferred_element_type=F32)
        ga, up = gu[:, :DFF], gu[:, DFF:]
        sg = jax.nn.sigmoid(ga)
        act = (ga * sg * up).astype(BF16)
        act_ref[...] = act
        dff = dx2_ * gate
        dff_ref[...] = dff.astype(BF16)
        dact = _dot(dff, wd_ref[...], NT)
        dga = (dact * up * (sg * (1.0 + ga * (1.0 - sg)))).astype(BF16)
        dup = (dact * ga * sg).astype(BF16)
        dgu_ref[:, :DFF] = dga
        dgu_ref[:, DFF:] = dup
        dh = _dot(dga, wgu_ref[:, :DFF], NT) + _dot(dup, wgu_ref[:, DFF:], NT)
        dx, dg, dsh, dsc = vjp(dh)
        dx1_ref[...] = dx2_ + dx

        @pl.when(i == 0)
        def _():
            dg_ref[...] = jnp.zeros_like(dg_ref)
            dm_ref[...] = jnp.zeros_like(dm_ref)

        dg_ref[...] += dg
        _acc_stream(dm_ref, 0, isc, dsh)
        _acc_stream(dm_ref, 1, isc, dsc)
        _acc_stream(dm_ref, 2, isc, dx2_ * jnp.dot(act, wd_ref[...], preferred_element_type=F32))

    return _call(
        body, "ffn_bwd", (T // tt,),
        [_rows(tt, D), _rows(tt, D), _full((8, D)), _full((1, D)), _full(wgu.shape), _full(wd.shape)],
        [_rows(tt, D), _rows(tt, D), _rows(tt, 2 * DFF), _rows(tt, DFF), _rows(tt, D), _full((1, D)), _full((8, D))],
        [_S((T, D)), _S((T, D), BF16), _S((T, 2 * DFF), BF16), _S((T, DFF), BF16), _S((T, D), BF16),
         _S((1, D)), _S((8, D))])(X1, dx2, mv, g, wgu, wd)


def _rms(x, g):
    return x * lax.rsqrt(jnp.mean(x * x, axis=-1, keepdims=True) + EPS) * g


def _loss_head(X2, tgt, gf, tc):
    T = X2.shape[0]

    def body(x_ref, t_ref, g_ref, dx_ref, loss_ref, dg_ref):
        i = pl.program_id(0)

        @pl.when(i == 0)
        def _():
            dx_ref[...] = jnp.zeros_like(dx_ref)
            loss_ref[...] = jnp.zeros_like(loss_ref)
            dg_ref[...] = jnp.zeros_like(dg_ref)

        @pl.when(i > 0)
        def _():
            y, vjp = jax.vjp(_rms, x_ref[...], g_ref[...])
            err = y - t_ref[...]
            dx, dg = vjp(err * (1.0 / D))
            dx_ref[...] = dx
            dg_ref[...] += dg
            loss_ref[...] += (0.5 / D) * jnp.sum(jnp.sum(err * err, axis=1, keepdims=True), axis=0, keepdims=True)

    return _call(
        body, "loss_head", (T // tc,),
        [_rows(tc, D), pl.BlockSpec((tc, D), lambda i: (jnp.maximum(i - 1, 0), 0)), _full((1, D))],
        [_rows(tc, D), _full((8, LANES)), _full((1, D))],
        [_S((T, D)), _S((8, LANES)), _S((1, D))])(X2, tgt, gf)


def _block_diag(pw):
    g, n = pw.shape[0], pw.shape[1]
    out = jnp.zeros((g * n, g * n), pw.dtype)
    for k in range(g):
        out = lax.dynamic_update_slice(out, pw[k], (k * n, k * n))
    return out


def _split_w_in(w):
    parts = [w[:, IN_BOUNDS[k]:IN_BOUNDS[k + 1]] for k in range(8)]
    parts[2] = jnp.pad(parts[2], ((0, 0), (0, LANES - 16)))
    return parts


def _mod_rows(mods_l, k0):
    rows = [mods_l[s, (k0 + k) * D:(k0 + k + 1) * D] for s in (0, 1) for k in range(3)]
    return jnp.stack(rows + [jnp.zeros((D,), F32)] * 2)


def _lane_row(v8):
    return jnp.pad(v8.reshape(1, 8), ((0, 0), (0, LANES - 8)))


def _device_step(x, c, ctx, tgt, wts, tt):
    tc = ctx.shape[0]
    X = jnp.concatenate([ctx, x], axis=0)
    cc8 = jnp.concatenate([wts["c_ctx"][None, :], c, jnp.zeros((6, D), F32)], axis=0)
    w_ada = wts["w_ada"].astype(BF16)
    mods = _mod_fwd(cc8, w_ada, wts["b_ada"].reshape(NL, 1, 6 * D))

    saved = []
    for l in range(NL):
        ws = [w.astype(BF16) for w in _split_w_in(wts["w_in"][l])]
        wbr = [wts[k][l].astype(BF16) for k in ("w_br_a", "w_br_b", "w_br_c", "w_o", "w_gu", "w_down")]
        mv1, mv2 = _mod_rows(mods[l], 0), _mod_rows(mods[l], 3)
        g1, g2 = wts["norm1_g"][l][None, :], wts["norm2_g"][l][None, :]
        cw, scw = wts["dn_conv_w"][l], wts["sc_conv_w"][l]
        alr, dtr = _lane_row(wts["dn_a_log"][l]), _lane_row(wts["dn_dt_bias"][l])
        gdn = wts["dn_norm_g"][l][None, :]
        pwbd, ps = _block_diag(wts["pool_w"][l]), wts["pool_scale"][l][None, :]
        hb, pq, pz, pab, pp, sx, sb, sc_, pg = _inproj_fwd(X, mv1, g1, ws, tc, tt)
        qkv = _dnprep_fwd(pq, cw, tc, tt)
        parts = _dn_chunks_fwd(qkv, pab, alr, dtr)
        (of, ssf, vnf), (ob, ssb, vnb) = _scan_fwd(parts, X.shape[0], tc)
        yp = _pool_fwd(pp, pwbd, ps, tc)
        ys = _sc_fwd(sx, sb, sc_, scw, tc, tt)
        X1 = _mix_fwd(X, of, ob, pz, yp, ys, pg, mv1, gdn, *wbr[:4], tc, tt)
        X2 = _ffn_fwd(X1, mv2, g2, wbr[4], wbr[5], tc, tt)
        saved.append(dict(X=X, X1=X1, ws=ws, wbr=wbr, mv1=mv1, mv2=mv2, g1=g1, g2=g2, cw=cw, scw=scw, alr=alr, dtr=dtr,
                          gdn=gdn, pwbd=pwbd, ps=ps, hb=hb, pq=pq, pz=pz, pab=pab, pp=pp, sx=sx, sb=sb, sc=sc_, pg=pg,
                          qkv=qkv, of=of, ob=ob, ssf=ssf, ssb=ssb, vnf=vnf, vnb=vnb, parts=parts, yp=yp, ys=ys))
        X = X2

    dX, loss, dgf = _loss_head(X, tgt, wts["final_norm_g"][None, :], tc)

    gl = {k: [None] * NL for k in ("w_in", "norm1_g", "norm2_g", "dn_conv_w", "dn_a_log", "dn_dt_bias", "dn_norm_g",
                                   "pool_w", "pool_scale", "sc_conv_w", "w_br_a", "w_br_b", "w_br_c", "w_o", "w_gu",
                                   "w_down")}
    dmods = [None] * NL
    for l in reversed(range(NL)):
        s = saved[l]
        dx1, h2, dgu, act, dff, dg2, dm2 = _ffn_bwd(s["X1"], dX, s["mv2"], s["g2"], s["wbr"][4], s["wbr"][5], tc, tt)
        gl["w_gu"][l] = _dw(h2, dgu, tt)
        gl["w_down"][l] = _dw(act, dff, tt)
        do, dz, dyp, dys, dpg, dwa, dwb, dwc, dwo, dgdn, dmg = _mix_bwd(
            dx1, s["of"], s["ob"], s["pz"], s["yp"], s["ys"], s["pg"], s["mv1"], s["gdn"], *s["wbr"][:4], tc, tt)
        dpp, dpw, dps = _pool_bwd(s["pp"], s["pwbd"], s["ps"], dyp, tc)
        dsx, dsb, dsc, dscw = _sc_bwd(s["sx"], s["sb"], s["sc"], s["scw"], dys, tc, tt)
        (dvnf, dssf), (dvnb, dssb) = _scan_bwd(do, s["parts"], tc)
        dqf, dqb, dpf, dpb, gacc = _dn_chunks_bwd(s["qkv"], s["pab"], s["alr"], s["dtr"], do,
                                                  (s["vnf"], dvnf, s["ssf"], dssf, s["parts"][0][6]),
                                                  (s["vnb"], dvnb, s["ssb"], dssb, s["parts"][1][6]))
        dy = _dnprep_bwd_act(s["pq"], s["cw"], dqf, dqb, tc, tt)
        dpq, dcw = _conv_bwd(dy, s["pq"], s["cw"], tc, tt)
        dps_ = [dpq, dz, dpf, dpb, dpp, dsx, dsb, dsc, dpg]
        dp_w = [0, 1, 2, 2, 3, 4, 5, 6, 7]
        dX, dg1, dm1 = _inproj_bwd(s["X"], s["mv1"], s["g1"], s["ws"], dps_, dp_w, dx1, tc, tt)
        dws = [_dw(s["hb"], dpq, tt), _dw(s["hb"], dz, tt), _dw(s["hb"], dpf + dpb, tt)[:, :16], _dw(s["hb"], dpp, tt),
               _dw(s["hb"], dsx, tt), _dw(s["hb"], dsb, tt), _dw(s["hb"], dsc, tt), _dw(s["hb"], dpg, tt)]
        gl["w_in"][l] = jnp.concatenate(dws, axis=1)
        gl["norm1_g"][l], gl["norm2_g"][l] = dg1[0], dg2[0]
        gl["dn_conv_w"][l], gl["sc_conv_w"][l] = dcw, dscw
        gl["dn_a_log"][l], gl["dn_dt_bias"][l] = gacc[0, :8].reshape(2, NH), gacc[1, :8].reshape(2, NH)
        gl["dn_norm_g"][l] = dgdn[0]
        gl["pool_w"][l] = jnp.stack([dpw[k * GW:(k + 1) * GW, k * GW:(k + 1) * GW] for k in range(4)])
        gl["pool_scale"][l] = dps[0]
        gl["w_br_a"][l], gl["w_br_b"][l], gl["w_br_c"][l], gl["w_o"][l] = dwa, dwb, dwc, dwo
        dm = dm1 + dmg
        row = lambda r: jnp.concatenate([dm[r], dm[r + 1], dm[r + 2], dm2[r], dm2[r + 1], dm2[r + 2]])
        dmods[l] = jnp.stack([row(0), row(3)] + [jnp.zeros((6 * D,), F32)] * 6)

    dwada, dbada, dcc = _mod_bwd(cc8, w_ada, jnp.stack(dmods))
    grads = {k: jnp.stack(v) for k, v in gl.items()}
    grads.update(w_ada=dwada, b_ada=dbada.reshape(NL, 6 * D), c_ctx=dcc[0], final_norm_g=dgf[0])
    return loss, dX[tc:], grads


MESH_ID = pl.DeviceIdType.MESH
HBM_SPEC = pl.BlockSpec(memory_space=pltpu.HBM)


def _me():
    return lax.axis_index("x"), lax.axis_index("y"), lax.axis_index("c")


def _dev_index(p):
    return 4 * p[0] + 2 * p[1] + p[2]


def _allgather(parts):
    n = len(parts)

    def body(*refs):
        ins, outs = refs[:n], refs[n:2 * n]
        send_sems, recv_sems, local_sems = refs[2 * n:]
        x, y, c = _me()
        me, sibling = (x, y, c), (x, y, 1 - c)
        chips = [(1 - x, y), (x, 1 - y), (1 - x, 1 - y)]

        def copy(a, k, block, to, src=None):
            dst = outs[a].at[_dev_index(block)]
            return pltpu.make_async_remote_copy(
                src_ref=dst if src is None else src, dst_ref=dst, send_sem=send_sems.at[a, k], recv_sem=recv_sems.at[a, k],
                device_id=to, device_id_type=MESH_ID)

        mine, first, passed = [], [], []
        for a in range(n):
            mine.append(pltpu.make_async_copy(ins[a], outs[a].at[_dev_index(me)], local_sems.at[a]))
            mine[-1].start()
            first.append(copy(a, 0, me, sibling, src=ins[a]))
            first += [copy(a, 1 + j, me, (*chip, c), src=ins[a]) for j, chip in enumerate(chips)]
        for cp in first:
            cp.start()
        for a in range(n):
            for j, chip in enumerate(chips):
                copy(a, 1 + j, (*chip, c), me).wait_recv()
                passed.append(copy(a, 4 + j, (*chip, c), sibling))
                passed[-1].start()
        for a in range(n):
            copy(a, 0, sibling, me).wait_recv()
            for j, chip in enumerate(chips):
                copy(a, 4 + j, (*chip, 1 - c), me).wait_recv()
        for cp in first + passed:
            cp.wait_send()
        for cp in mine:
            cp.wait()

    return pl.pallas_call(
        body, name="allgather", in_specs=[HBM_SPEC] * n, out_specs=[HBM_SPEC] * n,
        out_shape=[_S((N_DEV,) + p.shape, p.dtype) for p in parts],
        scratch_shapes=[pltpu.SemaphoreType.DMA((n, 7)), pltpu.SemaphoreType.DMA((n, 7)), pltpu.SemaphoreType.DMA((n,))],
    )(*parts)


def _broadcast_small(small):
    def body(in_ref, out_ref, send_sems, recv_sems, local_sem):
        x, y, c = _me()
        my = _dev_index((x, y, c))
        mine = pltpu.make_async_copy(in_ref, out_ref.at[my], local_sem)
        mine.start()
        remote = []
        for k in range(1, N_DEV):
            cp = pltpu.make_async_remote_copy(
                src_ref=in_ref, dst_ref=out_ref.at[my], send_sem=send_sems.at[k - 1], recv_sem=recv_sems.at[k - 1],
                device_id=(x ^ (k >> 2), y ^ ((k >> 1) & 1), c ^ (k & 1)), device_id_type=MESH_ID)
            cp.start()
            remote.append(cp)
        for cp in remote:
            cp.wait_recv()
        for cp in remote:
            cp.wait_send()
        mine.wait()

    return pl.pallas_call(
        body, name="small_exchange", in_specs=[HBM_SPEC], out_specs=HBM_SPEC,
        out_shape=_S((N_DEV,) + small.shape, small.dtype),
        scratch_shapes=[pltpu.SemaphoreType.DMA((7,)), pltpu.SemaphoreType.DMA((7,)), pltpu.SemaphoreType.DMA],
    )(small)


N_CHIP = 4


def _pair_exchange(g2s):
    n = len(g2s)

    def body(*refs):
        ins, outs = refs[:n], refs[n:2 * n]
        send_sems, recv_sems = refs[2 * n:]
        x, y, c = _me()
        cps = [pltpu.make_async_remote_copy(
            src_ref=ins[a].at[1 - c, j], dst_ref=outs[a].at[j], send_sem=send_sems.at[a, j], recv_sem=recv_sems.at[a, j],
            device_id=(x, y, 1 - c), device_id_type=MESH_ID) for a in range(n) for j in range(N_CHIP)]
        for cp in cps:
            cp.start()
        for cp in cps:
            cp.wait_recv()
        for cp in cps:
            cp.wait_send()

    return pl.pallas_call(
        body, name="pair_exchange", in_specs=[HBM_SPEC] * n, out_specs=[HBM_SPEC] * n,
        out_shape=[_S(g.shape[1:], g.dtype) for g in g2s],
        scratch_shapes=[pltpu.SemaphoreType.DMA((n, N_CHIP)), pltpu.SemaphoreType.DMA((n, N_CHIP))],
    )(*g2s)


def _chip_exchange(s4s):
    n = len(s4s)

    def body(*refs):
        ins, outs = refs[:n], refs[n:2 * n]
        send_sems, recv_sems, local_sems = refs[2 * n:]
        x, y, c = _me()
        my = 2 * x + y
        mine = [pltpu.make_async_copy(ins[a].at[my], outs[a].at[my], local_sems.at[a]) for a in range(n)]
        for cp in mine:
            cp.start()
        cps = []
        for k in range(1, N_CHIP):
            px, py = x ^ (k >> 1), y ^ (k & 1)
            for a in range(n):
                cps.append(pltpu.make_async_remote_copy(
                    src_ref=ins[a].at[2 * px + py], dst_ref=outs[a].at[my], send_sem=send_sems.at[a, k - 1],
                    recv_sem=recv_sems.at[a, k - 1], device_id=(px, py, c), device_id_type=MESH_ID))
                cps[-1].start()
        for cp in cps:
            cp.wait_recv()
        for cp in cps:
            cp.wait_send()
        for cp in mine:
            cp.wait()

    return pl.pallas_call(
        body, name="chip_exchange", in_specs=[HBM_SPEC] * n, out_specs=[HBM_SPEC] * n,
        out_shape=[_S(s.shape, s.dtype) for s in s4s],
        scratch_shapes=[pltpu.SemaphoreType.DMA((n, N_CHIP - 1)), pltpu.SemaphoreType.DMA((n, N_CHIP - 1)),
                        pltpu.SemaphoreType.DMA((n,))],
    )(*s4s)


def _shard_rows(r):
    return 256 if r % 256 == 0 else r


def _pair_sum(g2, got):
    _, nc, L, R, C = g2.shape
    tr = _shard_rows(R)

    def body(a_ref, b_ref, o_ref):
        o_ref[...] = (a_ref[0] + b_ref[...]).astype(BF16)

    blk = pl.BlockSpec((1, 1, tr, C), lambda j, l, i: (j, l, i, 0))
    return _call(
        body, "pair_sum", (nc, L, R // tr),
        [pl.BlockSpec((1, 1, 1, tr, C), lambda j, l, i: (lax.axis_index("c"), j, l, i, 0)), blk], blk,
        _S(got.shape, BF16))(g2, got)


def _adam(w, g, m, v):
    m2 = ADAM_B1 * m + (1.0 - ADAM_B1) * g
    v2 = ADAM_B2 * v + (1.0 - ADAM_B2) * (g * g)
    m_hat = m2 / (1.0 - ADAM_B1 ** ADAM_STEP)
    v_hat = v2 / (1.0 - ADAM_B2 ** ADAM_STEP)
    return -ADAM_LR * (m_hat / (jnp.sqrt(v_hat) + ADAM_EPS) + ADAM_WD * w), m2, v2


def _sum_adam(recv, w, m, v):
    L, R, C = w.shape
    tr = _shard_rows(R)

    def body(r_ref, w_ref, m_ref, v_ref, g_ref, d_ref, m2_ref, v2_ref):
        g = r_ref[0, 0].astype(F32)
        for j in range(1, N_CHIP):
            g = g + r_ref[j, 0].astype(F32)
        g_ref[0] = g
        d_ref[0], m2_ref[0], v2_ref[0] = _adam(w_ref[0], g, m_ref[0], v_ref[0])

    blk = pl.BlockSpec((1, tr, C), lambda l, i: (l, i, 0))
    return _call(
        body, "sum_adam", (L, R // tr),
        [pl.BlockSpec((N_CHIP, 1, tr, C), lambda l, i: (0, l, i, 0)), blk, blk, blk],
        [blk] * 4, [_S(w.shape)] * 4)(recv, w, m, v)


def _sum_small(recv):
    def body(r_ref, o_ref):
        g = r_ref[0]
        for k in range(1, N_DEV):
            g = g + r_ref[k]
        o_ref[...] = g

    return pl.pallas_call(body, name="sum_small", out_shape=_S(recv.shape[1:]))(recv)


def _adam_small(w, g, m, v):
    def body(w_ref, g_ref, m_ref, v_ref, d_ref, m2_ref, v2_ref):
        d_ref[...], m2_ref[...], v2_ref[...] = _adam(w_ref[...], g_ref[...], m_ref[...], v_ref[...])

    return pl.pallas_call(body, name="adam_small", out_shape=[_S(w.shape)] * 3)(w, g, m, v)


def _pack(arrs, dtype, row_mult):
    parts, offs, r = [], [], 0
    for a in arrs:
        nr = -(-a.size // LANES)
        parts.append(jnp.pad(a.reshape(-1).astype(dtype), (0, nr * LANES - a.size)))
        offs.append(r)
        r += nr
    pad = (-r) % row_mult
    if pad:
        parts.append(jnp.zeros((pad * LANES,), dtype))
    return jnp.concatenate(parts).reshape(r + pad, LANES), offs


def _unpack(packed, offs, shapes, lead=()):
    out = []
    for off, shp in zip(offs, shapes):
        size = int(np.prod(shp))
        nr = -(-size // LANES)
        flat = packed[..., off:off + nr, :].reshape(lead + (nr * LANES,))
        out.append(flat[..., :size].reshape(lead + tuple(shp)))
    return out


BIG = (("w_ada", 2), ("w_in", 2), ("w_br_a", 2), ("w_br_b", 2), ("w_br_c", 2), ("w_o", 1), ("w_gu", 2), ("w_down", 1))
CONV = ("dn_conv_w", "sc_conv_w")
REPL = ("c_ctx", "b_ada", "norm1_g", "norm2_g", "dn_a_log", "dn_dt_bias", "dn_norm_g", "pool_w", "pool_scale",
        "final_norm_g")
WEIGHTS = ("c_ctx", "w_ada", "b_ada", "norm1_g", "norm2_g", "w_in", "dn_conv_w", "dn_a_log", "dn_dt_bias", "dn_norm_g",
           "pool_w", "pool_scale", "sc_conv_w", "w_br_a", "w_br_b", "w_br_c", "w_o", "w_gu", "w_down", "final_norm_g")
TOKEN_TILE = 256


def _join(blocks, axis):
    nd, nl, r, c = blocks.shape
    if axis == 2:
        return blocks.transpose(1, 2, 0, 3).reshape(nl, r, nd * c)
    return blocks.transpose(1, 0, 2, 3).reshape(nl, nd * r, c)


def _split(full, axis):
    nl, r, c = full.shape
    if axis == 2:
        return full.reshape(nl, r, N_CHIP, 2, c // N_DEV).transpose(3, 2, 0, 1, 4)
    return full.reshape(nl, N_CHIP, 2, r // N_DEV, c).transpose(2, 1, 0, 3, 4)


def kernel(x, c, ctx, c_ctx, w_ada, b_ada, norm1_g, norm2_g, w_in, dn_conv_w, dn_a_log, dn_dt_bias, dn_norm_g, pool_w, pool_scale, sc_conv_w, w_br_a, w_br_b, w_br_c, w_o, w_gu, w_down, final_norm_g, loss_target, m_c_ctx, m_w_ada, m_b_ada, m_norm1_g, m_norm2_g, m_w_in, m_dn_conv_w, m_dn_a_log, m_dn_dt_bias, m_dn_norm_g, m_pool_w, m_pool_scale, m_sc_conv_w, m_w_br_a, m_w_br_b, m_w_br_c, m_w_o, m_w_gu, m_w_down, m_final_norm_g, v_c_ctx, v_w_ada, v_b_ada, v_norm1_g, v_norm2_g, v_w_in, v_dn_conv_w, v_dn_a_log, v_dn_dt_bias, v_dn_norm_g, v_pool_w, v_pool_scale, v_sc_conv_w, v_w_br_a, v_w_br_b, v_w_br_c, v_w_o, v_w_gu, v_w_down, v_final_norm_g):
    loc = dict(c_ctx=c_ctx, w_ada=w_ada, b_ada=b_ada, norm1_g=norm1_g, norm2_g=norm2_g, w_in=w_in, dn_conv_w=dn_conv_w,
               dn_a_log=dn_a_log, dn_dt_bias=dn_dt_bias, dn_norm_g=dn_norm_g, pool_w=pool_w, pool_scale=pool_scale,
               sc_conv_w=sc_conv_w, w_br_a=w_br_a, w_br_b=w_br_b, w_br_c=w_br_c, w_o=w_o, w_gu=w_gu, w_down=w_down,
               final_norm_g=final_norm_g)
    mom_m = dict(c_ctx=m_c_ctx, w_ada=m_w_ada, b_ada=m_b_ada, norm1_g=m_norm1_g, norm2_g=m_norm2_g, w_in=m_w_in,
                 dn_conv_w=m_dn_conv_w, dn_a_log=m_dn_a_log, dn_dt_bias=m_dn_dt_bias, dn_norm_g=m_dn_norm_g,
                 pool_w=m_pool_w, pool_scale=m_pool_scale, sc_conv_w=m_sc_conv_w, w_br_a=m_w_br_a, w_br_b=m_w_br_b,
                 w_br_c=m_w_br_c, w_o=m_w_o, w_gu=m_w_gu, w_down=m_w_down, final_norm_g=m_final_norm_g)
    mom_v = dict(c_ctx=v_c_ctx, w_ada=v_w_ada, b_ada=v_b_ada, norm1_g=v_norm1_g, norm2_g=v_norm2_g, w_in=v_w_in,
                 dn_conv_w=v_dn_conv_w, dn_a_log=v_dn_a_log, dn_dt_bias=v_dn_dt_bias, dn_norm_g=v_dn_norm_g,
                 pool_w=v_pool_w, pool_scale=v_pool_scale, sc_conv_w=v_sc_conv_w, w_br_a=v_w_br_a, w_br_b=v_w_br_b,
                 w_br_c=v_w_br_c, w_o=v_w_o, w_gu=v_w_gu, w_down=v_w_down, final_norm_g=v_final_norm_g)
    my = _dev_index(_me())

    big_pack, big_offs = _pack([loc[k] for k, _ in BIG], BF16, BF16_ROWS)
    conv_pack, conv_offs = _pack([loc[k] for k in CONV], F32, 8)
    big_all, conv_all = _allgather([big_pack, conv_pack])
    full = {k: loc[k] for k in REPL}
    for (k, axis), blocks in zip(BIG, _unpack(big_all, big_offs, [loc[k].shape for k, _ in BIG], (N_DEV,))):
        full[k] = _join(blocks, axis)
    for k, blocks in zip(CONV, _unpack(conv_all, conv_offs, [loc[k].shape for k in CONV], (N_DEV,))):
        full[k] = _join(blocks, 2)

    loss8, grad_x, g = _device_step(x[0], c, ctx[0], loss_target[0], full, TOKEN_TILE)

    g2s = [_split(g[k], axis) for k, axis in BIG]
    recv_big = _chip_exchange([_pair_sum(g2, got) for g2, got in zip(g2s, _pair_exchange(g2s))])

    small_names = REPL + CONV
    small_pack, small_offs = _pack([g[k] for k in small_names] + [loss8[0:1, 0:1]], F32, 8)
    small_sum = _sum_small(_broadcast_small(small_pack))
    sums = _unpack(small_sum, small_offs, [g[k].shape for k in small_names] + [(1, 1)])
    grads = dict(zip(small_names, sums[:-1]))
    loss = sums[-1][0, 0]
    for k in CONV:
        w = loc[k].shape[2]
        grads[k] = lax.dynamic_slice_in_dim(grads[k], my * w, w, axis=2)

    delta, new_m, new_v = {}, {}, {}
    for (k, _), recv in zip(BIG, recv_big):
        grads[k], delta[k], new_m[k], new_v[k] = _sum_adam(recv, loc[k], mom_m[k], mom_v[k])
    packs = [_pack([src[k] for k in small_names], F32, 8)[0] for src in (loc, grads, mom_m, mom_v)]
    _, offs = _pack([loc[k] for k in small_names], F32, 8)
    shapes = [loc[k].shape for k in small_names]
    for dst, packed in zip((delta, new_m, new_v), _adam_small(*packs)):
        dst.update(zip(small_names, _unpack(packed, offs, shapes)))

    return (loss, grad_x[None], *[grads[k] for k in WEIGHTS], *[delta[k] for k in WEIGHTS],
            *[new_m[k] for k in WEIGHTS], *[new_v[k] for k in WEIGHTS])
```

```python
import functools

import numpy as np
import jax
import jax.numpy as jnp
from jax import lax
from jax.experimental import pallas as pl
from jax.experimental.pallas import tpu as pltpu

F32 = jnp.float32
BF16 = jnp.bfloat16
HI = lax.Precision.HIGHEST

D = 1024
NL = 2
NH = 4
DH = 128
DN = NH * DH
CH = 64
GW = 64
PW = 256
DFF = 2816
EPS = 1e-6
N_DEV = 8
LANES = 128
BF16_ROWS = 16
VMEM_MB = 56

ADAM_LR, ADAM_B1, ADAM_B2, ADAM_EPS, ADAM_WD, ADAM_STEP = 0.001, 0.9, 0.999, 1e-08, 0.01, 10

IN_BOUNDS = (0, 1536, 2048, 2064, 2320, 2576, 2832, 3088, 6160)
IN_WIDTHS = (1536, 512, 128, 256, 256, 256, 256, 3072)
POOL_WIN = ((1, 0), (2, 1), (4, 3), (8, 7))

NN = ((1,), (0,))
NT = ((1,), (1,))
TN = ((0,), (0,))


def _dot(a, b, dims, hi=False):
    if hi:
        prec = lax.Precision.HIGH if hi == "x3" else HI
        return lax.dot_general(a, b, (dims, ((), ())), precision=prec, preferred_element_type=F32)
    return lax.dot_general(a.astype(BF16), b.astype(BF16), (dims, ((), ())), preferred_element_type=F32)


def _S(shape, dtype=F32):
    return jax.ShapeDtypeStruct(tuple(shape), dtype)


def _full(shape):
    nd = len(shape)
    return pl.BlockSpec(tuple(shape), lambda *_: (0,) * nd)


def _rows(tt, w):
    return pl.BlockSpec((tt, w), lambda i: (i, 0))


def _call(body, name, grid, in_specs, out_specs, out_shape, scratch=()):
    return pl.pallas_call(
        body, name=name, grid=grid, in_specs=in_specs, out_specs=out_specs, out_shape=out_shape,
        scratch_shapes=list(scratch),
        compiler_params=pltpu.CompilerParams(
            dimension_semantics=("arbitrary",) * len(grid), vmem_limit_bytes=VMEM_MB << 20),
    )


def _iota(shape, axis):
    return lax.broadcasted_iota(jnp.int32, shape, axis)


def _colsum(a):
    return jnp.sum(a, axis=0, keepdims=True)


def _silu(x):
    return x * jax.nn.sigmoid(x)


def _modulate(x, g, sh, sc):
    xn = x * lax.rsqrt(jnp.mean(x * x, axis=-1, keepdims=True) + EPS)
    return (xn * g) * (1.0 + sc) + sh


def _stream_rows(mv_ref, i, tt, tc, k):
    isc = (i * tt + _iota((tt, 1), 0)) < tc
    return isc, jnp.where(isc, mv_ref[k:k + 1, :], mv_ref[3 + k:4 + k, :])


def _acc_stream(ref, k, isc, val):
    ref[k:k + 1, :] += _colsum(jnp.where(isc, val, 0.0))
    ref[3 + k:4 + k, :] += _colsum(jnp.where(isc, 0.0, val))


MOD_CT = 1536


def _mod_fwd(cc8, w_ada, b_ada3):
    def body(cc_ref, w_ref, b_ref, o_ref):
        o_ref[0] = _dot(_silu(cc_ref[...]), w_ref[0], NN) + b_ref[0]

    return _call(
        body, "mod_fwd", (NL, 6 * D // MOD_CT),
        [pl.BlockSpec((8, D), lambda l, j: (0, 0)), pl.BlockSpec((1, D, MOD_CT), lambda l, j: (l, 0, j)),
         pl.BlockSpec((1, 1, MOD_CT), lambda l, j: (l, 0, j))],
        pl.BlockSpec((1, 8, MOD_CT), lambda l, j: (l, 0, j)), _S((NL, 8, 6 * D)))(cc8, w_ada, b_ada3)


def _mod_bwd(cc8, w_ada, dmods):
    def body(cc_ref, w_ref, dm_ref, dw_ref, db_ref, dcc_ref):
        first = (pl.program_id(0) == 0) & (pl.program_id(1) == 0)
        cc = cc_ref[...]
        sg = jax.nn.sigmoid(cc)
        dm = dm_ref[0]
        dw_ref[0] = _dot(cc * sg, dm, TN)
        db_ref[0] = dm[0:1, :] + dm[1:2, :]

        @pl.when(first)
        def _():
            dcc_ref[...] = jnp.zeros_like(dcc_ref)

        dcc_ref[...] += _dot(dm, w_ref[0], NT) * (sg * (1.0 + cc * (1.0 - sg)))

    return _call(
        body, "mod_bwd", (NL, 6 * D // MOD_CT),
        [pl.BlockSpec((8, D), lambda l, j: (0, 0)), pl.BlockSpec((1, D, MOD_CT), lambda l, j: (l, 0, j)),
         pl.BlockSpec((1, 8, MOD_CT), lambda l, j: (l, 0, j))],
        [pl.BlockSpec((1, D, MOD_CT), lambda l, j: (l, 0, j)), pl.BlockSpec((1, 1, MOD_CT), lambda l, j: (l, 0, j)),
         pl.BlockSpec((8, D), lambda l, j: (0, 0))],
        [_S((NL, D, 6 * D)), _S((NL, 1, 6 * D)), _S((8, D))])(cc8, w_ada, dmods)


def _inproj_fwd(X, mv, g, ws, tc, tt):
    T = X.shape[0]
    nw = len(ws)

    def body(x_ref, mv_ref, g_ref, *refs):
        w_refs, h_ref, p_refs = refs[:nw], refs[nw], refs[nw + 1:]
        i = pl.program_id(0)
        _, sh = _stream_rows(mv_ref, i, tt, tc, 0)
        _, sc = _stream_rows(mv_ref, i, tt, tc, 1)
        hb = _modulate(x_ref[...], g_ref[...], sh, sc).astype(BF16)
        h_ref[...] = hb
        for w_ref, p_ref in zip(w_refs, p_refs):
            p_ref[...] = jnp.dot(hb, w_ref[...], preferred_element_type=F32)

    return _call(
        body, "inproj_fwd", (T // tt,),
        [_rows(tt, D), _full((8, D)), _full((1, D))] + [_full(w.shape) for w in ws],
        [_rows(tt, D)] + [_rows(tt, w.shape[1]) for w in ws],
        [_S((T, D), BF16)] + [_S((T, w.shape[1])) for w in ws])(X, mv, g, *ws)


def _inproj_bwd(X, mv, g, ws, dps, dp_w, dres, tc, tt):
    T = X.shape[0]
    nw, nd = len(ws), len(dps)

    def body(x_ref, mv_ref, g_ref, dres_ref, *refs):
        w_refs, dp_refs = refs[:nw], refs[nw:nw + nd]
        dx_ref, dg_ref, dm_ref = refs[nw + nd:]
        i = pl.program_id(0)
        isc, sh = _stream_rows(mv_ref, i, tt, tc, 0)
        _, sc = _stream_rows(mv_ref, i, tt, tc, 1)
        dh = None
        for dp_ref, k in zip(dp_refs, dp_w):
            t = _dot(dp_ref[...], w_refs[k][...], NT)
            dh = t if dh is None else dh + t
        _, vjp = jax.vjp(_modulate, x_ref[...], g_ref[...], sh, sc)
        dx, dg, dsh, dsc = vjp(dh)
        dx_ref[...] = dres_ref[...] + dx

        @pl.when(i == 0)
        def _():
            dg_ref[...] = jnp.zeros_like(dg_ref)
            dm_ref[...] = jnp.zeros_like(dm_ref)

        dg_ref[...] += dg
        _acc_stream(dm_ref, 0, isc, dsh)
        _acc_stream(dm_ref, 1, isc, dsc)

    return _call(
        body, "inproj_bwd", (T // tt,),
        [_rows(tt, D), _full((8, D)), _full((1, D)), _rows(tt, D)] + [_full(w.shape) for w in ws]
        + [_rows(tt, dp.shape[1]) for dp in dps],
        [_rows(tt, D), _full((1, D)), _full((8, D))],
        [_S((T, D)), _S((1, D)), _S((8, D))])(X, mv, g, dres, *ws, *dps)


def _dw(A, B, tt):
    T, K = A.shape
    N = B.shape[1]
    tt = 3 * tt if T % (3 * tt) == 0 else tt
    tn = next(t for t in (1024, 512, 256, LANES) if N % t == 0)

    def body(a_ref, b_ref, o_ref):
        @pl.when(pl.program_id(1) == 0)
        def _():
            o_ref[...] = jnp.zeros_like(o_ref)

        o_ref[...] += _dot(a_ref[...], b_ref[...], TN)

    return _call(
        body, "dw", (N // tn, T // tt),
        [pl.BlockSpec((tt, K), lambda j, i: (i, 0)), pl.BlockSpec((tt, tn), lambda j, i: (i, j))],
        pl.BlockSpec((K, tn), lambda j, i: (0, j)), _S((K, N)))(A, B)


def _halo_specs(T, tt, cw, col):
    r8, nb8 = tt // 8, T // 8
    return [pl.BlockSpec((tt, cw), lambda j, i: (i, col(j))),
            pl.BlockSpec((8, cw), lambda j, i: (jnp.maximum(i * r8 - 1, 0), col(j))),
            pl.BlockSpec((8, cw), lambda j, i: (jnp.minimum((i + 1) * r8, nb8 - 1), col(j)))]


def _shifts(a, prev8, next8, i, tt, tc, T):
    r = _iota((tt, 1), 0)
    t = i * tt + r
    dn = jnp.where(r == 0, prev8[7:8, :], pltpu.roll(a, 1, 0))
    dn = jnp.where((t == 0) | (t == tc), 0.0, dn)
    up = jnp.where(r == tt - 1, next8[0:1, :], pltpu.roll(a, tt - 1, 0))
    up = jnp.where((t == T - 1) | (t == tc - 1), 0.0, up)
    return dn, up


def _dn_post(y, part):
    a = _silu(y)
    nrm = lax.rsqrt(jnp.sum(a * a, axis=-1, keepdims=True) + EPS)
    f = jnp.where(part == 0, nrm * (DH ** -0.5), jnp.where(part == 1, nrm, 1.0))
    return a * f


def _conv3(w_ref, dn, mid, up):
    return w_ref[0:1, :] * dn + w_ref[1:2, :] * mid + w_ref[2:3, :] * up


def _dnprep_fwd(pq, cw, tc, tt):
    T = pq.shape[0]

    def body(p_ref, pp_ref, pn_ref, w_ref, a_ref):
        part, i = pl.program_id(0), pl.program_id(1)
        p = p_ref[...]
        dn, up = _shifts(p, pp_ref[...], pn_ref[...], i, tt, tc, T)
        y = _conv3(w_ref, dn, p, up)
        for h in range(NH):
            a_ref[:, _hs(h)] = _dn_post(y[:, _hs(h)], part)

    return _call(
        body, "dnprep_fwd", (3, T // tt),
        _halo_specs(T, tt, DN, lambda j: j) + [pl.BlockSpec((3, DN), lambda j, i: (0, j))],
        pl.BlockSpec((tt, DN), lambda j, i: (i, j)), _S((T, 3 * DN)))(pq, pq, pq, cw)


def _dnprep_bwd_act(pq, cw, da_f, da_b, tc, tt):
    T = pq.shape[0]

    def body(p_ref, pp_ref, pn_ref, w_ref, df_ref, db_ref, dy_ref):
        part, i = pl.program_id(0), pl.program_id(1)
        p = p_ref[...]
        dn, up = _shifts(p, pp_ref[...], pn_ref[...], i, tt, tc, T)
        y = _conv3(w_ref, dn, p, up)
        for h in range(NH):
            _, vjp = jax.vjp(lambda yh: _dn_post(yh, part), y[:, _hs(h)])
            dy_ref[:, _hs(h)] = vjp(df_ref[:, _hs(h)] + db_ref[:, _hs(h)])[0]

    blk = pl.BlockSpec((tt, DN), lambda j, i: (i, j))
    return _call(
        body, "dnprep_bwd_act", (3, T // tt),
        _halo_specs(T, tt, DN, lambda j: j) + [pl.BlockSpec((3, DN), lambda j, i: (0, j)), blk, blk],
        blk, _S((T, 3 * DN)))(pq, pq, pq, cw, da_f, da_b)


def _conv_bwd(dy, p, cw, tc, tt):
    T, W = p.shape
    cb = DN

    def body(dy_ref, dyp_ref, dyn_ref, p_ref, pp_ref, pn_ref, w_ref, dp_ref, dw_ref):
        i = pl.program_id(1)
        dy, p_ = dy_ref[...], p_ref[...]
        ddn, dup = _shifts(dy, dyp_ref[...], dyn_ref[...], i, tt, tc, T)
        dp_ref[...] = _conv3(w_ref, dup, dy, ddn)
        pdn, pup = _shifts(p_, pp_ref[...], pn_ref[...], i, tt, tc, T)

        @pl.when(i == 0)
        def _():
            dw_ref[...] = jnp.zeros_like(dw_ref)

        dw_ref[0:1, :] += _colsum(dy * pdn)
        dw_ref[1:2, :] += _colsum(dy * p_)
        dw_ref[2:3, :] += _colsum(dy * pup)

    wspec = pl.BlockSpec((3, cb), lambda j, i: (0, j))
    return _call(
        body, "conv_bwd", (W // cb, T // tt),
        _halo_specs(T, tt, cb, lambda j: j) * 2 + [wspec],
        [pl.BlockSpec((tt, cb), lambda j, i: (i, j)), wspec], [_S((T, W)), _S((3, W))])(dy, dy, dy, p, p, p, cw)


def _sc_fwd(sx, sb, sc_, cw, tc, tt):
    T = sx.shape[0]

    def body(x_ref, xp_ref, xn_ref, c_ref, cp_ref, cn_ref, b_ref, w_ref, y_ref):
        i = pl.program_id(1)
        u = c_ref[...] * x_ref[...]
        dn, up = _shifts(u, cp_ref[...] * xp_ref[...], cn_ref[...] * xn_ref[...], i, tt, tc, T)
        y_ref[...] = b_ref[...] * _conv3(w_ref, dn, u, up)

    blk = pl.BlockSpec((tt, LANES), lambda j, i: (i, j))
    return _call(
        body, "sc_fwd", (PW // LANES, T // tt),
        _halo_specs(T, tt, LANES, lambda j: j) * 2 + [blk, pl.BlockSpec((3, LANES), lambda j, i: (0, j))],
        blk, _S((T, PW)))(sx, sx, sx, sc_, sc_, sc_, sb, cw)


def _sc_bwd(sx, sb, sc_, cw, dy, tc, tt):
    T = sx.shape[0]

    def body(x_ref, xp_ref, xn_ref, c_ref, cp_ref, cn_ref, b_ref, bp_ref, bn_ref, dy_ref, dyp_ref, dyn_ref, w_ref,
             dx_ref, db_ref, dc_ref, dw_ref):
        i = pl.program_id(1)
        x, c, dy_ = x_ref[...], c_ref[...], dy_ref[...]
        u = c * x
        udn, uup = _shifts(u, cp_ref[...] * xp_ref[...], cn_ref[...] * xn_ref[...], i, tt, tc, T)
        db_ref[...] = dy_ * _conv3(w_ref, udn, u, uup)
        e = dy_ * b_ref[...]
        edn, eup = _shifts(e, dyp_ref[...] * bp_ref[...], dyn_ref[...] * bn_ref[...], i, tt, tc, T)
        du = _conv3(w_ref, eup, e, edn)
        dx_ref[...] = du * c
        dc_ref[...] = du * x

        @pl.when(i == 0)
        def _():
            dw_ref[...] = jnp.zeros_like(dw_ref)

        dw_ref[0:1, :] += _colsum(e * udn)
        dw_ref[1:2, :] += _colsum(e * u)
        dw_ref[2:3, :] += _colsum(e * uup)

    blk = pl.BlockSpec((tt, LANES), lambda j, i: (i, j))
    wspec = pl.BlockSpec((3, LANES), lambda j, i: (0, j))
    return _call(
        body, "sc_bwd", (PW // LANES, T // tt),
        _halo_specs(T, tt, LANES, lambda j: j) * 4 + [wspec],
        [blk, blk, blk, wspec], [_S((T, PW))] * 3 + [_S((3, PW))])(
            sx, sx, sx, sc_, sc_, sc_, sb, sb, sb, dy, dy, dy, cw)


def _group_select(vals):
    g = _iota((1, PW), 1) // (PW // len(POOL_WIN))
    return jnp.where(g == 0, vals[0], jnp.where(g == 1, vals[1], jnp.where(g == 2, vals[2], vals[3])))


def _nested_box(get, mirror):
    acc, outs, pl_, ph_ = get(0), [], 0, 0
    for lo, hi in POOL_WIN:
        if mirror:
            lo, hi = hi, lo
        for k in range(pl_ + 1, lo + 1):
            acc = acc + get(-k)
        for k in range(ph_ + 1, hi + 1):
            acc = acc + get(k)
        pl_, ph_ = lo, hi
        outs.append(acc)
    return _group_select(outs)


def _box_tokens(a, n, mirror):
    idx = _iota((n, 1), 0)

    def get(k):
        if k == 0:
            return a
        return jnp.where((idx + k >= 0) & (idx + k < n), pltpu.roll(a, (-k) % n, 0), 0.0)

    return _nested_box(get, mirror)


def _inv_count(pos, n):
    return _group_select([1.0 / (jnp.minimum(pos + hi, n - 1) - jnp.maximum(pos - lo, 0) + 1).astype(F32)
                          for lo, hi in POOL_WIN])


def _pool_rows(ref, r, R, tc, mirror):
    def get(k):
        rr = r + k
        rc = jnp.clip(rr, 0, R - 1)
        v = ref[pl.ds(pl.multiple_of(tc + rc * GW, GW), GW), :]
        if mirror:
            v = v * _inv_count(jnp.full((1, PW), rc, jnp.int32), R)
        return jnp.where((rr >= 0) & (rr < R), v, 0.0)

    return _nested_box(get, mirror)


def _pool_fwd(u, pwbd, ps, tc):
    T = u.shape[0]
    R = (T - tc) // GW

    def body(u_ref, pw_ref, ps_ref, y_ref):
        pw, scale = pw_ref[...], ps_ref[...]
        uc = u_ref[0:tc, :]
        mc = _box_tokens(uc, tc, False) * _inv_count(_iota((tc, 1), 0), tc)
        y_ref[0:tc, :] = _dot(mc - uc, pw, NN) * scale
        inv_c = _inv_count(_iota((GW, 1), 0), GW)

        def row(r, carry):
            rs = _pool_rows(u_ref, r, R, tc, False) * _inv_count(jnp.full((1, PW), r, jnp.int32), R)
            m = _box_tokens(rs, GW, False) * inv_c
            sl = pl.ds(pl.multiple_of(tc + r * GW, GW), GW)
            y_ref[sl, :] = _dot(m - u_ref[sl, :], pw, NN) * scale
            return carry

        lax.fori_loop(0, R, row, 0)

    return pl.pallas_call(
        body, name="pool_fwd", out_shape=_S((T, PW)),
        compiler_params=pltpu.CompilerParams(vmem_limit_bytes=VMEM_MB << 20))(u, pwbd, ps)


def _pool_bwd(u, pwbd, ps, dy, tc):
    T = u.shape[0]
    R = (T - tc) // GW

    def body(u_ref, pw_ref, ps_ref, dy_ref, du_ref, dpw_ref, dps_ref, dd_ref):
        pw, scale = pw_ref[...], ps_ref[...]
        dpw_ref[...] = jnp.zeros_like(dpw_ref)
        dps_ref[...] = jnp.zeros_like(dps_ref)

        def back(d, dy_):
            dz = dy_ * scale
            dpw_ref[...] += _dot(d, dz, TN)
            dps_ref[...] += _colsum(dy_ * _dot(d, pw, NN))
            return _dot(dz, pw, NT)

        uc = u_ref[0:tc, :]
        inv_cc = _inv_count(_iota((tc, 1), 0), tc)
        ddc = back(_box_tokens(uc, tc, False) * inv_cc - uc, dy_ref[0:tc, :])
        du_ref[0:tc, :] = _box_tokens(ddc * inv_cc, tc, True) - ddc
        inv_c = _inv_count(_iota((GW, 1), 0), GW)

        def row1(r, carry):
            rs = _pool_rows(u_ref, r, R, tc, False) * _inv_count(jnp.full((1, PW), r, jnp.int32), R)
            m = _box_tokens(rs, GW, False) * inv_c
            sl = pl.ds(pl.multiple_of(tc + r * GW, GW), GW)
            dd_ref[sl, :] = back(m - u_ref[sl, :], dy_ref[sl, :])
            return carry

        lax.fori_loop(0, R, row1, 0)

        def row2(r, carry):
            t1 = _pool_rows(dd_ref, r, R, tc, True)
            sl = pl.ds(pl.multiple_of(tc + r * GW, GW), GW)
            du_ref[sl, :] = _box_tokens(t1 * inv_c, GW, True) - dd_ref[sl, :]
            return carry

        lax.fori_loop(0, R, row2, 0)

    return pl.pallas_call(
        body, name="pool_bwd", out_shape=[_S((T, PW)), _S((PW, PW)), _S((1, PW))],
        scratch_shapes=[pltpu.VMEM((T, PW), F32)],
        compiler_params=pltpu.CompilerParams(vmem_limit_bytes=VMEM_MB << 20))(u, pwbd, ps, dy)


def _scan_consts():
    i = np.arange(CH)
    lower = (i[:, None] >= i[None, :]).astype(np.float32)
    return jnp.asarray(np.stack([lower, lower.T])), jnp.asarray(np.stack([lower.T, lower]))


def _gates(pab, al, dtb, csum):
    sp_in = pab + dtb
    sp = jnp.maximum(sp_in, 0.0) + jnp.log(1.0 + jnp.exp(-jnp.abs(sp_in)))
    nexp = -jnp.exp(al)
    gm = nexp * sp
    return gm, jax.nn.sigmoid(pab), _dot(csum, gm, NN, hi=True), sp_in, nexp


def _lane_col(m, j):
    return jnp.sum(jnp.where(_iota(m.shape, 1) == j, m, 0.0), axis=1, keepdims=True)


def _hs(h):
    return slice(h * DH, (h + 1) * DH)


HS = NH * CH
X3 = "x3"


def _stack(x, base=0):
    return jnp.concatenate([x[:, base + h * DH:base + (h + 1) * DH] for h in range(NH)], axis=0)


def _heads(st):
    return [st[h * CH:(h + 1) * CH] for h in range(NH)]


def _rowsum(a):
    return jnp.sum(a, axis=1, keepdims=True)


def _row_of(col):
    e0 = (_iota((8, LANES), 1) == 0).astype(F32)
    return _dot(e0, jnp.broadcast_to(col, (HS, LANES)), NT, hi=True)[0:1, :]


def _inverses(nms):
    eye = (_iota((HS, HS), 0) == _iota((HS, HS), 1)).astype(F32)
    x0s, mps = [eye + nm for nm in nms], list(nms)
    for _ in range(5):
        mps = [_dot(mp, mp, NN) for mp in mps]
        x0s = [x0 + _dot(x0, mp, NN) for x0, mp in zip(x0s, mps)]
    rs = [eye - _dot(eye - nm, x0, NN, hi=X3) for nm, x0 in zip(nms, x0s)]
    return [x0 + _dot(x0, r, NN) for x0, r in zip(x0s, rs)]


def _dn_chunk_pre(qkv, pab, al, dtb, csum_d, d):
    gm, bm, gcm, sp_in, nexp = _gates(pab, al, dtb, csum_d)
    gc = jnp.concatenate([_lane_col(gcm, d * NH + h) for h in range(NH)], axis=0)
    beta = jnp.concatenate([_lane_col(bm, 8 + d * NH + h) for h in range(NH)], axis=0)
    q, k, v = _stack(qkv, 0), _stack(qkv, DN), _stack(qkv, 2 * DN)
    ii, jj = _iota((HS, HS), 0), _iota((HS, HS), 1)
    sh = CH.bit_length() - 1
    same = (ii >> sh) == (jj >> sh)
    incl = same & ((ii >= jj) if d == 0 else (ii <= jj))
    strict = same & ((ii > jj) if d == 0 else (ii < jj))
    Di = jnp.where(incl, jnp.exp(jnp.where(incl, gc - _row_of(gc), 0.0)), 0.0)
    Ds = jnp.where(strict, Di, 0.0)
    kb = k * beta
    kk = _dot(kb, k, NT)
    return dict(q=q, k=k, v=v, beta=beta, gc=gc, gm=gm, bm=bm, sp_in=sp_in, nexp=nexp, Di=Di, Ds=Ds, strict=strict,
                last=CH - 1 if d == 0 else 0, kb=kb, kk=kk)


def _dn_chunk_post(c, tm):
    q, k, v, beta, gc, kb, last = (c[n] for n in ("q", "k", "v", "beta", "gc", "kb", "last"))
    E = jnp.exp(gc)
    gls = [gc[h * CH + last:h * CH + last + 1, :] for h in range(NH)]
    xs = jnp.exp(jnp.concatenate([jnp.broadcast_to(g, (CH, 1)) for g in gls], axis=0) - gc)
    qk = _dot(q, k, NT)
    return dict(c, tm=tm, E=E, gls=gls, xs=xs, qk=qk, u=_dot(tm, v * beta, NN, hi=X3), w=_dot(tm, kb * E, NN, hi=X3),
                ks=k * xs, qd=q * E, aqk=qk * c["Di"])


def _dn_chunk_bwd(c, S, dS2, do, vn, dvn):
    q, k, v, beta, tm, E, xs, kb, u, w = (c[n] for n in ("q", "k", "v", "beta", "tm", "E", "xs", "kb", "u", "w"))
    doh, vnh, dvnh = _heads(do), _heads(vn), _heads(dvn)
    dqd = jnp.concatenate([_dot(doh[h], S[h], NT) for h in range(NH)], axis=0)
    dks = jnp.concatenate([_dot(vnh[h], dS2[h], NT) for h in range(NH)], axis=0)
    dw = -jnp.concatenate([_dot(dvnh[h], S[h], NT) for h in range(NH)], axis=0)
    daqk = _dot(do, vn, NT)
    drb = _dot(tm, dvn, TN, hi=X3)
    drw = _dot(tm, dw, TN, hi=X3)
    dA = jnp.where(c["strict"], -(_dot(drb, u, NT) + _dot(drw, w, NT)), 0.0)
    dM1 = dA * c["Ds"]
    dM2 = daqk * c["Di"]
    dkb = _dot(dM1, k, NN) + drw * E
    dk = _dot(dM1, kb, TN) + _dot(dM2, q, TN) + dks * xs
    dq = _dot(dM2, k, NN) + dqd * E
    G = dM1 * c["kk"] + dM2 * c["qk"]
    on_diag = _iota((HS, HS), 0) == _iota((HS, HS), 1)
    col = _rowsum(jnp.where(on_diag, jnp.broadcast_to(_colsum(G), (HS, HS)), 0.0))
    dxx = _rowsum(dks * k) * xs
    dgc = _rowsum(G) - col + (_rowsum(dqd * q) + _rowsum(drw * kb)) * E - dxx
    at_last = _iota((CH, 1), 0) == c["last"]
    ends = []
    for h in range(NH):
        dgl = _colsum(_rowsum(S[h] * dS2[h])) * jnp.exp(c["gls"][h]) + _colsum(dxx[h * CH:(h + 1) * CH])
        ends.append(jnp.where(at_last, dgl, 0.0))
    dgc = dgc + jnp.concatenate(ends, axis=0)
    dbeta = _rowsum(drb * v) + _rowsum(dkb * k)
    return dq, dk + dkb * beta, drb * beta, dgc, dbeta


def _chunk_group(n, want=2):
    g = want
    while n % g:
        g //= 2
    return g


def _dn_chunks_fwd(qkv, pab, alr, dtr):
    T = qkv.shape[0]
    n = T // CH
    G = _chunk_group(n, 4)
    csum, _ = _scan_consts()

    def body(q_ref, p_ref, cs_ref, al_ref, dt_ref, *outs):
        inst = [(g, d) for g in range(G) for d in range(2)]
        pres = [_dn_chunk_pre(q_ref[g * CH:(g + 1) * CH, :], p_ref[g * CH:(g + 1) * CH, :], al_ref[...], dt_ref[...],
                              cs_ref[d], d) for g, d in inst]
        tms = _inverses([-(p["kk"] * p["Ds"]) for p in pres])
        for (g, d), pre, tm in zip(inst, pres, tms):
            rows = slice(g * HS, (g + 1) * HS)
            u_ref, w_ref, ks_ref, qd_ref, aqk_ref, eg_ref, tm_ref = outs[7 * d:7 * d + 7]
            c = _dn_chunk_post(pre, tm)
            tm_ref[rows, :] = tm
            u_ref[rows, :] = c["u"]
            w_ref[rows, :] = c["w"].astype(BF16)
            ks_ref[rows, :] = c["ks"].astype(BF16)
            qd_ref[rows, :] = c["qd"].astype(BF16)
            aqk_ref[rows, :] = c["aqk"].astype(BF16)
            egs = [jnp.broadcast_to(jnp.exp(gl), (1, LANES)) for gl in c["gls"]]
            eg_ref[g * 8:(g + 1) * 8, :] = jnp.concatenate(egs + [jnp.zeros((8 - NH, LANES), F32)], axis=0)

    st = lambda w_: pl.BlockSpec((G * HS, w_), lambda i: (i, 0))
    one = [st(DH)] * 4 + [st(HS), pl.BlockSpec((G * 8, LANES), lambda i: (i, 0)), st(HS)]
    shp = [_S((n * HS, DH)), _S((n * HS, DH), BF16), _S((n * HS, DH), BF16), _S((n * HS, DH), BF16),
           _S((n * HS, HS), BF16), _S((n * 8, LANES)), _S((n * HS, HS))]
    outs = _call(
        body, "dn_chunks_fwd", (n // G,),
        [_rows(G * CH, 3 * DN), _rows(G * CH, LANES), _full((2, CH, CH)), _full((1, LANES)), _full((1, LANES))],
        one * 2, shp * 2)(qkv, pab, csum, alr, dtr)
    return tuple(outs[:7]), tuple(outs[7:])


def _scan_order(n, ncx):
    return (lambda i: i), (lambda i: jnp.where(i < ncx, ncx - 1 - i, n - 1 - (i - ncx)))


def _scan_specs(order):
    st = lambda w_: pl.BlockSpec((HS, w_), lambda i: (order(i), 0))
    return dict(st=st(DH), aqk=st(HS), eg=pl.BlockSpec((8, LANES), lambda i: (order(i), 0)),
                tok=pl.BlockSpec((CH, DN), lambda i: (order(i), 0)), state=pl.BlockSpec((1, DN, DH), lambda i: (order(i), 0, 0)))


def _scan_fwd(parts, T, tc):
    n = T // CH
    orders = _scan_order(n, tc // CH)

    def body(*refs):
        S_f, S_b = refs[-2:]

        @pl.when(pl.program_id(0) == 0)
        def _():
            S_f[...] = jnp.zeros_like(S_f)
            S_b[...] = jnp.zeros_like(S_b)

        for d, S in enumerate((S_f, S_b)):
            u_ref, w_ref, ks_ref, qd_ref, aqk_ref, eg_ref = refs[6 * d:6 * d + 6]
            o_ref, ss_ref, vn_ref = refs[12 + 3 * d:15 + 3 * d]
            ss_ref[0] = S[...]
            Sh = [S[_hs(h), :] for h in range(NH)]
            wh, ksh, qdh = _heads(w_ref[...]), _heads(ks_ref[...]), _heads(qd_ref[...])
            vn = u_ref[...] - jnp.concatenate([_dot(wh[h], Sh[h], NN) for h in range(NH)], axis=0)
            vn_ref[...] = vn
            av, vnh = _heads(_dot(aqk_ref[...], vn, NN)), _heads(vn)
            for h in range(NH):
                o_ref[:, _hs(h)] = _dot(qdh[h], Sh[h], NN) + av[h]
                S[_hs(h), :] = Sh[h] * eg_ref[h:h + 1, :] + _dot(ksh[h], vnh[h], TN)

    ins, outs, shp = [], [], []
    for d in range(2):
        sp = _scan_specs(orders[d])
        ins += [sp["st"]] * 4 + [sp["aqk"], sp["eg"]]
        outs += [sp["tok"], sp["state"], sp["st"]]
        shp += [_S((T, DN)), _S((n, DN, DH)), _S((n * HS, DH))]
    res = _call(body, "scan_fwd", (n,), ins, outs, shp,
                scratch=[pltpu.VMEM((DN, DH), F32), pltpu.VMEM((DN, DH), F32)])(*parts[0][:6], *parts[1][:6])
    return tuple(res[:3]), tuple(res[3:])


def _scan_bwd(do, parts, tc):
    T = do.shape[0]
    n = T // CH
    fwd_orders = _scan_order(n, tc // CH)
    orders = [lambda s, f=f: f(n - 1 - s) for f in fwd_orders]

    def body(*refs):
        dS_f, dS_b = refs[-2:]

        @pl.when(pl.program_id(0) == 0)
        def _():
            dS_f[...] = jnp.zeros_like(dS_f)
            dS_b[...] = jnp.zeros_like(dS_b)

        for d, dS in enumerate((dS_f, dS_b)):
            do_ref, w_ref, ks_ref, qd_ref, aqk_ref, eg_ref = refs[6 * d:6 * d + 6]
            dvn_ref, dss_ref = refs[12 + 2 * d:14 + 2 * d]
            dss_ref[0] = dS[...]
            dSh = [dS[_hs(h), :] for h in range(NH)]
            wh, ksh, qdh = _heads(w_ref[...]), _heads(ks_ref[...]), _heads(qd_ref[...])
            do_st = _stack(do_ref[...])
            dvn = _dot(aqk_ref[...], do_st, TN) + jnp.concatenate([_dot(ksh[h], dSh[h], NN) for h in range(NH)], axis=0)
            dvn_ref[...] = dvn
            doh, dvnh = _heads(do_st), _heads(dvn)
            for h in range(NH):
                dS[_hs(h), :] = _dot(qdh[h], doh[h], TN) + dSh[h] * eg_ref[h:h + 1, :] - _dot(wh[h], dvnh[h], TN)

    ins, outs, shp, args = [], [], [], []
    for d in range(2):
        sp = _scan_specs(orders[d])
        ins += [sp["tok"]] + [sp["st"]] * 3 + [sp["aqk"], sp["eg"]]
        outs += [sp["st"], sp["state"]]
        shp += [_S((n * HS, DH)), _S((n, DN, DH))]
        args += [do, *parts[d][1:6]]
    res = _call(body, "scan_bwd", (n,), ins, outs, shp,
                scratch=[pltpu.VMEM((DN, DH), F32), pltpu.VMEM((DN, DH), F32)])(*args)
    return tuple(res[:2]), tuple(res[2:])


def _dn_chunks_bwd(qkv, pab, alr, dtr, do, fwd, bwd):
    T = qkv.shape[0]
    n = T // CH
    G = _chunk_group(n)
    csum, csum_t = _scan_consts()

    def body(q_ref, p_ref, do_ref, cs_ref, cst_ref, al_ref, dt_ref, *refs):
        dq_refs, dp_refs, acc_ref = refs[10:12], refs[12:14], refs[14]

        @pl.when(pl.program_id(0) == 0)
        def _():
            acc_ref[...] = jnp.zeros_like(acc_ref)

        lane = _iota((CH, LANES), 1)
        for g in range(G):
            tok, rows = slice(g * CH, (g + 1) * CH), slice(g * HS, (g + 1) * HS)
            do_st = _stack(do_ref[tok, :])
            for d in range(2):
                vn_ref, dvn_ref, ss_ref, dss_ref, tm_ref = refs[5 * d:5 * d + 5]
                c = _dn_chunk_post(_dn_chunk_pre(q_ref[tok, :], p_ref[tok, :], al_ref[...], dt_ref[...], cs_ref[d], d),
                                   tm_ref[rows, :])
                dq, dk, dv, dgc, dbeta = _dn_chunk_bwd(
                    c, [ss_ref[g, _hs(h), :] for h in range(NH)], [dss_ref[g, _hs(h), :] for h in range(NH)],
                    do_st, vn_ref[rows, :], dvn_ref[rows, :])
                dgcm = jnp.zeros((CH, LANES), F32)
                dbm = jnp.zeros((CH, LANES), F32)
                for h, (a, b_, c_, e, f) in enumerate(zip(*map(_heads, (dq, dk, dv, dgc, dbeta)))):
                    dq_refs[d][tok, _hs(h)] = a
                    dq_refs[d][tok, _hs(NH + h)] = b_
                    dq_refs[d][tok, _hs(2 * NH + h)] = c_
                    dgcm = jnp.where(lane == d * NH + h, e, dgcm)
                    dbm = jnp.where(lane == 8 + d * NH + h, f, dbm)
                dgm = _dot(cst_ref[d], dgcm, NN, hi=True)
                dsp = dgm * c["nexp"] * jax.nn.sigmoid(c["sp_in"])
                dp_refs[d][tok, :] = dsp + dbm * c["bm"] * (1.0 - c["bm"])
                acc_ref[0:1, :] += _colsum(dgm * c["gm"])
                acc_ref[1:2, :] += _colsum(dsp)

    st = pl.BlockSpec((G * HS, DH), lambda i: (i, 0))
    state = pl.BlockSpec((G, DN, DH), lambda i: (i, 0, 0))
    return _call(
        body, "dn_chunks_bwd", (n // G,),
        [_rows(G * CH, 3 * DN), _rows(G * CH, LANES), _rows(G * CH, DN), _full((2, CH, CH)), _full((2, CH, CH)),
         _full((1, LANES)), _full((1, LANES))] + [st, st, state, state, pl.BlockSpec((G * HS, HS), lambda i: (i, 0))] * 2,
        [_rows(G * CH, 3 * DN)] * 2 + [_rows(G * CH, LANES)] * 2 + [_full((8, LANES))],
        [_S((T, 3 * DN))] * 2 + [_S((T, LANES))] * 2 + [_S((8, LANES))])(
            qkv, pab, do, csum, csum_t, alr, dtr, *fwd, *bwd)


def _head_out(o, z, g):
    on = o * lax.rsqrt(jnp.mean(o * o, axis=-1, keepdims=True) + EPS) * g
    return on * _silu(z)


def _mix_branches(of_ref, ob_ref, z_ref, yp_ref, ys_ref, pg_ref, gdn_ref, wa_ref, wb_ref, wc_ref):
    ons, ya = [], None
    for h in range(NH):
        on = _head_out(of_ref[:, _hs(h)] + ob_ref[:, _hs(h)], z_ref[:, _hs(h)], gdn_ref[...])
        t = _dot(on, wa_ref[_hs(h), :], NN)
        ya = t if ya is None else ya + t
        ons.append(on)
    ys = [ya, _dot(yp_ref[...], wb_ref[...], NN), _dot(ys_ref[...], wc_ref[...], NN)]
    sg = [jax.nn.sigmoid(pg_ref[:, k * D:(k + 1) * D]) for k in range(3)]
    return ons, ys, sg


def _mix_fwd(X, of, ob, z, yp, ys, pg, mv, gdn, wa, wb, wc, wo, tc, tt):
    T = X.shape[0]

    def body(x_ref, of_ref, ob_ref, z_ref, yp_ref, ys_ref, pg_ref, mv_ref, gdn_ref, wa_ref, wb_ref, wc_ref, wo_ref,
             x1_ref):
        _, yb, sg = _mix_branches(of_ref, ob_ref, z_ref, yp_ref, ys_ref, pg_ref, gdn_ref, wa_ref, wb_ref, wc_ref)
        mix = _dot(sg[0] * yb[0] + sg[1] * yb[1] + sg[2] * yb[2], wo_ref[...], NN)
        _, gate = _stream_rows(mv_ref, pl.program_id(0), tt, tc, 2)
        x1_ref[...] = x_ref[...] + gate * mix

    return _call(
        body, "mix_fwd", (T // tt,),
        [_rows(tt, D), _rows(tt, DN), _rows(tt, DN), _rows(tt, DN), _rows(tt, PW), _rows(tt, PW), _rows(tt, 3 * D),
         _full((8, D)), _full((1, DH)), _full(wa.shape), _full(wb.shape), _full(wc.shape), _full(wo.shape)],
        _rows(tt, D), _S((T, D)))(X, of, ob, z, yp, ys, pg, mv, gdn, wa, wb, wc, wo)


def _mix_bwd(dx1, of, ob, z, yp, ys, pg, mv, gdn, wa, wb, wc, wo, tc, tt):
    T = dx1.shape[0]

    def body(dx_ref, of_ref, ob_ref, z_ref, yp_ref, ys_ref, pg_ref, mv_ref, gdn_ref, wa_ref, wb_ref, wc_ref, wo_ref,
             do_ref, dz_ref, dyp_ref, dys_ref, dpg_ref, dwa_ref, dwb_ref, dwc_ref, dwo_ref, dgdn_ref, dm_ref):
        i = pl.program_id(0)

        @pl.when(i == 0)
        def _():
            for r in (dwa_ref, dwb_ref, dwc_ref, dwo_ref, dgdn_ref, dm_ref):
                r[...] = jnp.zeros_like(r)

        ons, yb, sg = _mix_branches(of_ref, ob_ref, z_ref, yp_ref, ys_ref, pg_ref, gdn_ref, wa_ref, wb_ref, wc_ref)
        ymix = sg[0] * yb[0] + sg[1] * yb[1] + sg[2] * yb[2]
        isc, gate = _stream_rows(mv_ref, i, tt, tc, 2)
        dx = dx_ref[...]
        dmix = dx * gate
        _acc_stream(dm_ref, 2, isc, dx * _dot(ymix, wo_ref[...], NN))
        dwo_ref[...] += _dot(ymix, dmix, TN)
        dymix = _dot(dmix, wo_ref[...], NT)
        dyb = []
        for k in range(3):
            dyb.append(dymix * sg[k])
            dpg_ref[:, k * D:(k + 1) * D] = dymix * yb[k] * sg[k] * (1.0 - sg[k])
        dwb_ref[...] += _dot(yp_ref[...], dyb[1], TN)
        dwc_ref[...] += _dot(ys_ref[...], dyb[2], TN)
        dyp_ref[...] = _dot(dyb[1], wb_ref[...], NT)
        dys_ref[...] = _dot(dyb[2], wc_ref[...], NT)
        dg = jnp.zeros((1, DH), F32)
        for h in range(NH):
            dwa_ref[_hs(h), :] += _dot(ons[h], dyb[0], TN)
            don = _dot(dyb[0], wa_ref[_hs(h), :], NT)
            _, vjp = jax.vjp(_head_out, of_ref[:, _hs(h)] + ob_ref[:, _hs(h)], z_ref[:, _hs(h)], gdn_ref[...])
            do_h, dz_h, dg_h = vjp(don)
            do_ref[:, _hs(h)] = do_h
            dz_ref[:, _hs(h)] = dz_h
            dg = dg + dg_h
        dgdn_ref[...] += dg

    return _call(
        body, "mix_bwd", (T // tt,),
        [_rows(tt, D), _rows(tt, DN), _rows(tt, DN), _rows(tt, DN), _rows(tt, PW), _rows(tt, PW), _rows(tt, 3 * D),
         _full((8, D)), _full((1, DH)), _full(wa.shape), _full(wb.shape), _full(wc.shape), _full(wo.shape)],
        [_rows(tt, DN), _rows(tt, DN), _rows(tt, PW), _rows(tt, PW), _rows(tt, 3 * D),
         _full(wa.shape), _full(wb.shape), _full(wc.shape), _full(wo.shape), _full((1, DH)), _full((8, D))],
        [_S((T, DN)), _S((T, DN)), _S((T, PW)), _S((T, PW)), _S((T, 3 * D)),
         _S(wa.shape), _S(wb.shape), _S(wc.shape), _S(wo.shape), _S((1, DH)), _S((8, D))])(
            dx1, of, ob, z, yp, ys, pg, mv, gdn, wa, wb, wc, wo)


def _ffn_fwd(X1, mv, g, wgu, wd, tc, tt):
    T = X1.shape[0]

    def body(x_ref, mv_ref, g_ref, wgu_ref, wd_ref, x2_ref):
        i = pl.program_id(0)
        _, sh = _stream_rows(mv_ref, i, tt, tc, 0)
        _, sc = _stream_rows(mv_ref, i, tt, tc, 1)
        _, gate = _stream_rows(mv_ref, i, tt, tc, 2)
        x = x_ref[...]
        gu = _dot(_modulate(x, g_ref[...], sh, sc), wgu_ref[...], NN)
        x2_ref[...] = x + gate * _dot(_silu(gu[:, :DFF]) * gu[:, DFF:], wd_ref[...], NN)

    return _call(
        body, "ffn_fwd", (T // tt,),
        [_rows(tt, D), _full((8, D)), _full((1, D)), _full(wgu.shape), _full(wd.shape)],
        _rows(tt, D), _S((T, D)))(X1, mv, g, wgu, wd)


def _ffn_bwd(X1, dx2, mv, g, wgu, wd, tc, tt):
    T = X1.shape[0]

    def body(x_ref, dx2_ref, mv_ref, g_ref, wgu_ref, wd_ref, dx1_ref, h_ref, dgu_ref, act_ref, dff_ref, dg_ref, dm_ref):
        i = pl.program_id(0)
        isc, sh = _stream_rows(mv_ref, i, tt, tc, 0)
        _, sc = _stream_rows(mv_ref, i, tt, tc, 1)
        _, gate = _stream_rows(mv_ref, i, tt, tc, 2)
        x, dx2_ = x_ref[...], dx2_ref[...]
        h, vjp = jax.vjp(_modulate, x, g_ref[...], sh, sc)
        hb = h.astype(BF16)
        h_ref[...] = hb
        gu = jnp.dot(hb, wgu_ref[...], preferred_element_type=F32)
        ga, up = gu[:, :DFF], gu[:, DFF:]
        sg = jax.nn.sigmoid(ga)
        act = (ga * sg * up).astype(BF16)
        act_ref[...] = act
        dff = dx2_ * gate
        dff_ref[...] = dff.astype(BF16)
        dact = _dot(dff, wd_ref[...], NT)
        dga = (dact * up * (sg * (1.0 + ga * (1.0 - sg)))).astype(BF16)
        dup = (dact * ga * sg).astype(BF16)
        dgu_ref[:, :DFF] = dga
        dgu_ref[:, DFF:] = dup
        dh = _dot(dga, wgu_ref[:, :DFF], NT) + _dot(dup, wgu_ref[:, DFF:], NT)
        dx, dg, dsh, dsc = vjp(dh)
        dx1_ref[...] = dx2_ + dx

        @pl.when(i == 0)
        def _():
            dg_ref[...] = jnp.zeros_like(dg_ref)
            dm_ref[...] = jnp.zeros_like(dm_ref)

        dg_ref[...] += dg
        _acc_stream(dm_ref, 0, isc, dsh)
        _acc_stream(dm_ref, 1, isc, dsc)
        _acc_stream(dm_ref, 2, isc, dx2_ * jnp.dot(act, wd_ref[...], preferred_element_type=F32))

    return _call(
        body, "ffn_bwd", (T // tt,),
        [_rows(tt, D), _rows(tt, D), _full((8, D)), _full((1, D)), _full(wgu.shape), _full(wd.shape)],
        [_rows(tt, D), _rows(tt, D), _rows(tt, 2 * DFF), _rows(tt, DFF), _rows(tt, D), _full((1, D)), _full((8, D))],
        [_S((T, D)), _S((T, D), BF16), _S((T, 2 * DFF), BF16), _S((T, DFF), BF16), _S((T, D), BF16),
         _S((1, D)), _S((8, D))])(X1, dx2, mv, g, wgu, wd)


def _rms(x, g):
    return x * lax.rsqrt(jnp.mean(x * x, axis=-1, keepdims=True) + EPS) * g


def _loss_head(X2, tgt, gf, tc):
    T = X2.shape[0]

    def body(x_ref, t_ref, g_ref, dx_ref, loss_ref, dg_ref):
        i = pl.program_id(0)

        @pl.when(i == 0)
        def _():
            dx_ref[...] = jnp.zeros_like(dx_ref)
            loss_ref[...] = jnp.zeros_like(loss_ref)
            dg_ref[...] = jnp.zeros_like(dg_ref)

        @pl.when(i > 0)
        def _():
            y, vjp = jax.vjp(_rms, x_ref[...], g_ref[...])
            err = y - t_ref[...]
            dx, dg = vjp(err * (1.0 / D))
            dx_ref[...] = dx
            dg_ref[...] += dg
            loss_ref[...] += (0.5 / D) * jnp.sum(jnp.sum(err * err, axis=1, keepdims=True), axis=0, keepdims=True)

    return _call(
        body, "loss_head", (T // tc,),
        [_rows(tc, D), pl.BlockSpec((tc, D), lambda i: (jnp.maximum(i - 1, 0), 0)), _full((1, D))],
        [_rows(tc, D), _full((8, LANES)), _full((1, D))],
        [_S((T, D)), _S((8, LANES)), _S((1, D))])(X2, tgt, gf)


def _block_diag(pw):
    g, n = pw.shape[0], pw.shape[1]
    out = jnp.zeros((g * n, g * n), pw.dtype)
    for k in range(g):
        out = lax.dynamic_update_slice(out, pw[k], (k * n, k * n))
    return out


def _split_w_in(w):
    parts = [w[:, IN_BOUNDS[k]:IN_BOUNDS[k + 1]] for k in range(8)]
    parts[2] = jnp.pad(parts[2], ((0, 0), (0, LANES - 16)))
    return parts


def _mod_rows(mods_l, k0):
    rows = [mods_l[s, (k0 + k) * D:(k0 + k + 1) * D] for s in (0, 1) for k in range(3)]
    return jnp.stack(rows + [jnp.zeros((D,), F32)] * 2)


def _lane_row(v8):
    return jnp.pad(v8.reshape(1, 8), ((0, 0), (0, LANES - 8)))


def _device_step(x, c, ctx, tgt, wts, tt):
    tc = ctx.shape[0]
    X = jnp.concatenate([ctx, x], axis=0)
    cc8 = jnp.concatenate([wts["c_ctx"][None, :], c, jnp.zeros((6, D), F32)], axis=0)
    w_ada = wts["w_ada"].astype(BF16)
    mods = _mod_fwd(cc8, w_ada, wts["b_ada"].reshape(NL, 1, 6 * D))

    saved = []
    for l in range(NL):
        ws = [w.astype(BF16) for w in _split_w_in(wts["w_in"][l])]
        wbr = [wts[k][l].astype(BF16) for k in ("w_br_a", "w_br_b", "w_br_c", "w_o", "w_gu", "w_down")]
        mv1, mv2 = _mod_rows(mods[l], 0), _mod_rows(mods[l], 3)
        g1, g2 = wts["norm1_g"][l][None, :], wts["norm2_g"][l][None, :]
        cw, scw = wts["dn_conv_w"][l], wts["sc_conv_w"][l]
        alr, dtr = _lane_row(wts["dn_a_log"][l]), _lane_row(wts["dn_dt_bias"][l])
        gdn = wts["dn_norm_g"][l][None, :]
        pwbd, ps = _block_diag(wts["pool_w"][l]), wts["pool_scale"][l][None, :]
        hb, pq, pz, pab, pp, sx, sb, sc_, pg = _inproj_fwd(X, mv1, g1, ws, tc, tt)
        qkv = _dnprep_fwd(pq, cw, tc, tt)
        parts = _dn_chunks_fwd(qkv, pab, alr, dtr)
        (of, ssf, vnf), (ob, ssb, vnb) = _scan_fwd(parts, X.shape[0], tc)
        yp = _pool_fwd(pp, pwbd, ps, tc)
        ys = _sc_fwd(sx, sb, sc_, scw, tc, tt)
        X1 = _mix_fwd(X, of, ob, pz, yp, ys, pg, mv1, gdn, *wbr[:4], tc, tt)
        X2 = _ffn_fwd(X1, mv2, g2, wbr[4], wbr[5], tc, tt)
        saved.append(dict(X=X, X1=X1, ws=ws, wbr=wbr, mv1=mv1, mv2=mv2, g1=g1, g2=g2, cw=cw, scw=scw, alr=alr, dtr=dtr,
                          gdn=gdn, pwbd=pwbd, ps=ps, hb=hb, pq=pq, pz=pz, pab=pab, pp=pp, sx=sx, sb=sb, sc=sc_, pg=pg,
                          qkv=qkv, of=of, ob=ob, ssf=ssf, ssb=ssb, vnf=vnf, vnb=vnb, parts=parts, yp=yp, ys=ys))
        X = X2

    dX, loss, dgf = _loss_head(X, tgt, wts["final_norm_g"][None, :], tc)

    gl = {k: [None] * NL for k in ("w_in", "norm1_g", "norm2_g", "dn_conv_w", "dn_a_log", "dn_dt_bias", "dn_norm_g",
                                   "pool_w", "pool_scale", "sc_conv_w", "w_br_a", "w_br_b", "w_br_c", "w_o", "w_gu",
                                   "w_down")}
    dmods = [None] * NL
    for l in reversed(range(NL)):
        s = saved[l]
        dx1, h2, dgu, act, dff, dg2, dm2 = _ffn_bwd(s["X1"], dX, s["mv2"], s["g2"], s["wbr"][4], s["wbr"][5], tc, tt)
        gl["w_gu"][l] = _dw(h2, dgu, tt)
        gl["w_down"][l] = _dw(act, dff, tt)
        do, dz, dyp, dys, dpg, dwa, dwb, dwc, dwo, dgdn, dmg = _mix_bwd(
            dx1, s["of"], s["ob"], s["pz"], s["yp"], s["ys"], s["pg"], s["mv1"], s["gdn"], *s["wbr"][:4], tc, tt)
        dpp, dpw, dps = _pool_bwd(s["pp"], s["pwbd"], s["ps"], dyp, tc)
        dsx, dsb, dsc, dscw = _sc_bwd(s["sx"], s["sb"], s["sc"], s["scw"], dys, tc, tt)
        (dvnf, dssf), (dvnb, dssb) = _scan_bwd(do, s["parts"], tc)
        dqf, dqb, dpf, dpb, gacc = _dn_chunks_bwd(s["qkv"], s["pab"], s["alr"], s["dtr"], do,
                                                  (s["vnf"], dvnf, s["ssf"], dssf, s["parts"][0][6]),
                                                  (s["vnb"], dvnb, s["ssb"], dssb, s["parts"][1][6]))
        dy = _dnprep_bwd_act(s["pq"], s["cw"], dqf, dqb, tc, tt)
        dpq, dcw = _conv_bwd(dy, s["pq"], s["cw"], tc, tt)
        dps_ = [dpq, dz, dpf, dpb, dpp, dsx, dsb, dsc, dpg]
        dp_w = [0, 1, 2, 2, 3, 4, 5, 6, 7]
        dX, dg1, dm1 = _inproj_bwd(s["X"], s["mv1"], s["g1"], s["ws"], dps_, dp_w, dx1, tc, tt)
        dws = [_dw(s["hb"], dpq, tt), _dw(s["hb"], dz, tt), _dw(s["hb"], dpf + dpb, tt)[:, :16], _dw(s["hb"], dpp, tt),
               _dw(s["hb"], dsx, tt), _dw(s["hb"], dsb, tt), _dw(s["hb"], dsc, tt), _dw(s["hb"], dpg, tt)]
        gl["w_in"][l] = jnp.concatenate(dws, axis=1)
        gl["norm1_g"][l], gl["norm2_g"][l] = dg1[0], dg2[0]
        gl["dn_conv_w"][l], gl["sc_conv_w"][l] = dcw, dscw
        gl["dn_a_log"][l], gl["dn_dt_bias"][l] = gacc[0, :8].reshape(2, NH), gacc[1, :8].reshape(2, NH)
        gl["dn_norm_g"][l] = dgdn[0]
        gl["pool_w"][l] = jnp.stack([dpw[k * GW:(k + 1) * GW, k * GW:(k + 1) * GW] for k in range(4)])
        gl["pool_scale"][l] = dps[0]
        gl["w_br_a"][l], gl["w_br_b"][l], gl["w_br_c"][l], gl["w_o"][l] = dwa, dwb, dwc, dwo
        dm = dm1 + dmg
        row = lambda r: jnp.concatenate([dm[r], dm[r + 1], dm[r + 2], dm2[r], dm2[r + 1], dm2[r + 2]])
        dmods[l] = jnp.stack([row(0), row(3)] + [jnp.zeros((6 * D,), F32)] * 6)

    dwada, dbada, dcc = _mod_bwd(cc8, w_ada, jnp.stack(dmods))
    grads = {k: jnp.stack(v) for k, v in gl.items()}
    grads.update(w_ada=dwada, b_ada=dbada.reshape(NL, 6 * D), c_ctx=dcc[0], final_norm_g=dgf[0])
    return loss, dX[tc:], grads


MESH_ID = pl.DeviceIdType.MESH
HBM_SPEC = pl.BlockSpec(memory_space=pltpu.HBM)


def _me():
    return lax.axis_index("x"), lax.axis_index("y"), lax.axis_index("c")


def _dev_index(p):
    return 4 * p[0] + 2 * p[1] + p[2]


def _allgather(parts):
    n = len(parts)

    def body(*refs):
        ins, outs = refs[:n], refs[n:2 * n]
        send_sems, recv_sems, local_sems = refs[2 * n:]
        x, y, c = _me()
        me, sibling = (x, y, c), (x, y, 1 - c)
        chips = [(1 - x, y), (x, 1 - y), (1 - x, 1 - y)]

        def copy(a, k, block, to, src=None):
            dst = outs[a].at[_dev_index(block)]
            return pltpu.make_async_remote_copy(
                src_ref=dst if src is None else src, dst_ref=dst, send_sem=send_sems.at[a, k], recv_sem=recv_sems.at[a, k],
                device_id=to, device_id_type=MESH_ID)

        mine, first, passed = [], [], []
        for a in range(n):
            mine.append(pltpu.make_async_copy(ins[a], outs[a].at[_dev_index(me)], local_sems.at[a]))
            mine[-1].start()
            first.append(copy(a, 0, me, sibling, src=ins[a]))
            first += [copy(a, 1 + j, me, (*chip, c), src=ins[a]) for j, chip in enumerate(chips)]
        for cp in first:
            cp.start()
        for a in range(n):
            for j, chip in enumerate(chips):
                copy(a, 1 + j, (*chip, c), me).wait_recv()
                passed.append(copy(a, 4 + j, (*chip, c), sibling))
                passed[-1].start()
        for a in range(n):
            copy(a, 0, sibling, me).wait_recv()
            for j, chip in enumerate(chips):
                copy(a, 4 + j, (*chip, 1 - c), me).wait_recv()
        for cp in first + passed:
            cp.wait_send()
        for cp in mine:
            cp.wait()

    return pl.pallas_call(
        body, name="allgather", in_specs=[HBM_SPEC] * n, out_specs=[HBM_SPEC] * n,
        out_shape=[_S((N_DEV,) + p.shape, p.dtype) for p in parts],
        scratch_shapes=[pltpu.SemaphoreType.DMA((n, 7)), pltpu.SemaphoreType.DMA((n, 7)), pltpu.SemaphoreType.DMA((n,))],
    )(*parts)


def _broadcast_small(small):
    def body(in_ref, out_ref, send_sems, recv_sems, local_sem):
        x, y, c = _me()
        my = _dev_index((x, y, c))
        mine = pltpu.make_async_copy(in_ref, out_ref.at[my], local_sem)
        mine.start()
        remote = []
        for k in range(1, N_DEV):
            cp = pltpu.make_async_remote_copy(
                src_ref=in_ref, dst_ref=out_ref.at[my], send_sem=send_sems.at[k - 1], recv_sem=recv_sems.at[k - 1],
                device_id=(x ^ (k >> 2), y ^ ((k >> 1) & 1), c ^ (k & 1)), device_id_type=MESH_ID)
            cp.start()
            remote.append(cp)
        for cp in remote:
            cp.wait_recv()
        for cp in remote:
            cp.wait_send()
        mine.wait()

    return pl.pallas_call(
        body, name="small_exchange", in_specs=[HBM_SPEC], out_specs=HBM_SPEC,
        out_shape=_S((N_DEV,) + small.shape, small.dtype),
        scratch_shapes=[pltpu.SemaphoreType.DMA((7,)), pltpu.SemaphoreType.DMA((7,)), pltpu.SemaphoreType.DMA],
    )(small)


N_CHIP = 4


def _pair_exchange(g2s):
    n = len(g2s)

    def body(*refs):
        ins, outs = refs[:n], refs[n:2 * n]
        send_sems, recv_sems = refs[2 * n:]
        x, y, c = _me()
        cps = [pltpu.make_async_remote_copy(
            src_ref=ins[a].at[1 - c, j], dst_ref=outs[a].at[j], send_sem=send_sems.at[a, j], recv_sem=recv_sems.at[a, j],
            device_id=(x, y, 1 - c), device_id_type=MESH_ID) for a in range(n) for j in range(N_CHIP)]
        for cp in cps:
            cp.start()
        for cp in cps:
            cp.wait_recv()
        for cp in cps:
            cp.wait_send()

    return pl.pallas_call(
        body, name="pair_exchange", in_specs=[HBM_SPEC] * n, out_specs=[HBM_SPEC] * n,
        out_shape=[_S(g.shape[1:], g.dtype) for g in g2s],
        scratch_shapes=[pltpu.SemaphoreType.DMA((n, N_CHIP)), pltpu.SemaphoreType.DMA((n, N_CHIP))],
    )(*g2s)


def _chip_exchange(s4s):
    n = len(s4s)

    def body(*refs):
        ins, outs = refs[:n], refs[n:2 * n]
        send_sems, recv_sems, local_sems = refs[2 * n:]
        x, y, c = _me()
        my = 2 * x + y
        mine = [pltpu.make_async_copy(ins[a].at[my], outs[a].at[my], local_sems.at[a]) for a in range(n)]
        for cp in mine:
            cp.start()
        cps = []
        for k in range(1, N_CHIP):
            px, py = x ^ (k >> 1), y ^ (k & 1)
            for a in range(n):
                cps.append(pltpu.make_async_remote_copy(
                    src_ref=ins[a].at[2 * px + py], dst_ref=outs[a].at[my], send_sem=send_sems.at[a, k - 1],
                    recv_sem=recv_sems.at[a, k - 1], device_id=(px, py, c), device_id_type=MESH_ID))
                cps[-1].start()
        for cp in cps:
            cp.wait_recv()
        for cp in cps:
            cp.wait_send()
        for cp in mine:
            cp.wait()

    return pl.pallas_call(
        body, name="chip_exchange", in_specs=[HBM_SPEC] * n, out_specs=[HBM_SPEC] * n,
        out_shape=[_S(s.shape, s.dtype) for s in s4s],
        scratch_shapes=[pltpu.SemaphoreType.DMA((n, N_CHIP - 1)), pltpu.SemaphoreType.DMA((n, N_CHIP - 1)),
                        pltpu.SemaphoreType.DMA((n,))],
    )(*s4s)


def _shard_rows(r):
    return 256 if r % 256 == 0 else r


def _pair_sum(g2, got):
    _, nc, L, R, C = g2.shape
    tr = _shard_rows(R)

    def body(a_ref, b_ref, o_ref):
        o_ref[...] = (a_ref[0] + b_ref[...]).astype(BF16)

    blk = pl.BlockSpec((1, 1, tr, C), lambda j, l, i: (j, l, i, 0))
    return _call(
        body, "pair_sum", (nc, L, R // tr),
        [pl.BlockSpec((1, 1, 1, tr, C), lambda j, l, i: (lax.axis_index("c"), j, l, i, 0)), blk], blk,
        _S(got.shape, BF16))(g2, got)


def _adam(w, g, m, v):
    m2 = ADAM_B1 * m + (1.0 - ADAM_B1) * g
    v2 = ADAM_B2 * v + (1.0 - ADAM_B2) * (g * g)
    m_hat = m2 / (1.0 - ADAM_B1 ** ADAM_STEP)
    v_hat = v2 / (1.0 - ADAM_B2 ** ADAM_STEP)
    return -ADAM_LR * (m_hat / (jnp.sqrt(v_hat) + ADAM_EPS) + ADAM_WD * w), m2, v2


def _sum_adam(recv, w, m, v):
    L, R, C = w.shape
    tr = _shard_rows(R)

    def body(r_ref, w_ref, m_ref, v_ref, g_ref, d_ref, m2_ref, v2_ref):
        g = r_ref[0, 0].astype(F32)
        for j in range(1, N_CHIP):
            g = g + r_ref[j, 0].astype(F32)
        g_ref[0] = g
        d_ref[0], m2_ref[0], v2_ref[0] = _adam(w_ref[0], g, m_ref[0], v_ref[0])

    blk = pl.BlockSpec((1, tr, C), lambda l, i: (l, i, 0))
    return _call(
        body, "sum_adam", (L, R // tr),
        [pl.BlockSpec((N_CHIP, 1, tr, C), lambda l, i: (0, l, i, 0)), blk, blk, blk],
        [blk] * 4, [_S(w.shape)] * 4)(recv, w, m, v)


def _sum_small(recv):
    def body(r_ref, o_ref):
        g = r_ref[0]
        for k in range(1, N_DEV):
            g = g + r_ref[k]
        o_ref[...] = g

    return pl.pallas_call(body, name="sum_small", out_shape=_S(recv.shape[1:]))(recv)


def _adam_small(w, g, m, v):
    def body(w_ref, g_ref, m_ref, v_ref, d_ref, m2_ref, v2_ref):
        d_ref[...], m2_ref[...], v2_ref[...] = _adam(w_ref[...], g_ref[...], m_ref[...], v_ref[...])

    return pl.pallas_call(body, name="adam_small", out_shape=[_S(w.shape)] * 3)(w, g, m, v)


def _pack(arrs, dtype, row_mult):
    parts, offs, r = [], [], 0
    for a in arrs:
        nr = -(-a.size // LANES)
        parts.append(jnp.pad(a.reshape(-1).astype(dtype), (0, nr * LANES - a.size)))
        offs.append(r)
        r += nr
    pad = (-r) % row_mult
    if pad:
        parts.append(jnp.zeros((pad * LANES,), dtype))
    return jnp.concatenate(parts).reshape(r + pad, LANES), offs


def _unpack(packed, offs, shapes, lead=()):
    out = []
    for off, shp in zip(offs, shapes):
        size = int(np.prod(shp))
        nr = -(-size // LANES)
        flat = packed[..., off:off + nr, :].reshape(lead + (nr * LANES,))
        out.append(flat[..., :size].reshape(lead + tuple(shp)))
    return out


BIG = (("w_ada", 2), ("w_in", 2), ("w_br_a", 2), ("w_br_b", 2), ("w_br_c", 2), ("w_o", 1), ("w_gu", 2), ("w_down", 1))
CONV = ("dn_conv_w", "sc_conv_w")
REPL = ("c_ctx", "b_ada", "norm1_g", "norm2_g", "dn_a_log", "dn_dt_bias", "dn_norm_g", "pool_w", "pool_scale",
        "final_norm_g")
WEIGHTS = ("c_ctx", "w_ada", "b_ada", "norm1_g", "norm2_g", "w_in", "dn_conv_w", "dn_a_log", "dn_dt_bias", "dn_norm_g",
           "pool_w", "pool_scale", "sc_conv_w", "w_br_a", "w_br_b", "w_br_c", "w_o", "w_gu", "w_down", "final_norm_g")
TOKEN_TILE = 256


def _join(blocks, axis):
    nd, nl, r, c = blocks.shape
    if axis == 2:
        return blocks.transpose(1, 2, 0, 3).reshape(nl, r, nd * c)
    return blocks.transpose(1, 0, 2, 3).reshape(nl, nd * r, c)


def _split(full, axis):
    nl, r, c = full.shape
    if axis == 2:
        return full.reshape(nl, r, N_CHIP, 2, c // N_DEV).transpose(3, 2, 0, 1, 4)
    return full.reshape(nl, N_CHIP, 2, r // N_DEV, c).transpose(2, 1, 0, 3, 4)


def kernel(x, c, ctx, c_ctx, w_ada, b_ada, norm1_g, norm2_g, w_in, dn_conv_w, dn_a_log, dn_dt_bias, dn_norm_g, pool_w, pool_scale, sc_conv_w, w_br_a, w_br_b, w_br_c, w_o, w_gu, w_down, final_norm_g, loss_target, m_c_ctx, m_w_ada, m_b_ada, m_norm1_g, m_norm2_g, m_w_in, m_dn_conv_w, m_dn_a_log, m_dn_dt_bias, m_dn_norm_g, m_pool_w, m_pool_scale, m_sc_conv_w, m_w_br_a, m_w_br_b, m_w_br_c, m_w_o, m_w_gu, m_w_down, m_final_norm_g, v_c_ctx, v_w_ada, v_b_ada, v_norm1_g, v_norm2_g, v_w_in, v_dn_conv_w, v_dn_a_log, v_dn_dt_bias, v_dn_norm_g, v_pool_w, v_pool_scale, v_sc_conv_w, v_w_br_a, v_w_br_b, v_w_br_c, v_w_o, v_w_gu, v_w_down, v_final_norm_g):
    loc = dict(c_ctx=c_ctx, w_ada=w_ada, b_ada=b_ada, norm1_g=norm1_g, norm2_g=norm2_g, w_in=w_in, dn_conv_w=dn_conv_w,
               dn_a_log=dn_a_log, dn_dt_bias=dn_dt_bias, dn_norm_g=dn_norm_g, pool_w=pool_w, pool_scale=pool_scale,
               sc_conv_w=sc_conv_w, w_br_a=w_br_a, w_br_b=w_br_b, w_br_c=w_br_c, w_o=w_o, w_gu=w_gu, w_down=w_down,
               final_norm_g=final_norm_g)
    mom_m = dict(c_ctx=m_c_ctx, w_ada=m_w_ada, b_ada=m_b_ada, norm1_g=m_norm1_g, norm2_g=m_norm2_g, w_in=m_w_in,
                 dn_conv_w=m_dn_conv_w, dn_a_log=m_dn_a_log, dn_dt_bias=m_dn_dt_bias, dn_norm_g=m_dn_norm_g,
                 pool_w=m_pool_w, pool_scale=m_pool_scale, sc_conv_w=m_sc_conv_w, w_br_a=m_w_br_a, w_br_b=m_w_br_b,
                 w_br_c=m_w_br_c, w_o=m_w_o, w_gu=m_w_gu, w_down=m_w_down, final_norm_g=m_final_norm_g)
    mom_v = dict(c_ctx=v_c_ctx, w_ada=v_w_ada, b_ada=v_b_ada, norm1_g=v_norm1_g, norm2_g=v_norm2_g, w_in=v_w_in,
                 dn_conv_w=v_dn_conv_w, dn_a_log=v_dn_a_log, dn_dt_bias=v_dn_dt_bias, dn_norm_g=v_dn_norm_g,
                 pool_w=v_pool_w, pool_scale=v_pool_scale, sc_conv_w=v_sc_conv_w, w_br_a=v_w_br_a, w_br_b=v_w_br_b,
                 w_br_c=v_w_br_c, w_o=v_w_o, w_gu=v_w_gu, w_down=v_w_down, final_norm_g=v_final_norm_g)
    my = _dev_index(_me())

    big_pack, big_offs = _pack([loc[k] for k, _ in BIG], BF16, BF16_ROWS)
    conv_pack, conv_offs = _pack([loc[k] for k in CONV], F32, 8)
    big_all, conv_all = _allgather([big_pack, conv_pack])
    full = {k: loc[k] for k in REPL}
    for (k, axis), blocks in zip(BIG, _unpack(big_all, big_offs, [loc[k].shape for k, _ in BIG], (N_DEV,))):
        full[k] = _join(blocks, axis)
    for k, blocks in zip(CONV, _unpack(conv_all, conv_offs, [loc[k].shape for k in CONV], (N_DEV,))):
        full[k] = _join(blocks, 2)

    loss8, grad_x, g = _device_step(x[0], c, ctx[0], loss_target[0], full, TOKEN_TILE)

    g2s = [_split(g[k], axis) for k, axis in BIG]
    recv_big = _chip_exchange([_pair_sum(g2, got) for g2, got in zip(g2s, _pair_exchange(g2s))])

    small_names = REPL + CONV
    small_pack, small_offs = _pack([g[k] for k in small_names] + [loss8[0:1, 0:1]], F32, 8)
    small_sum = _sum_small(_broadcast_small(small_pack))
    sums = _unpack(small_sum, small_offs, [g[k].shape for k in small_names] + [(1, 1)])
    grads = dict(zip(small_names, sums[:-1]))
    loss = sums[-1][0, 0]
    for k in CONV:
        w = loc[k].shape[2]
        grads[k] = lax.dynamic_slice_in_dim(grads[k], my * w, w, axis=2)

    delta, new_m, new_v = {}, {}, {}
    for (k, _), recv in zip(BIG, recv_big):
        grads[k], delta[k], new_m[k], new_v[k] = _sum_adam(recv, loc[k], mom_m[k], mom_v[k])
    packs = [_pack([src[k] for k in small_names], F32, 8)[0] for src in (loc, grads, mom_m, mom_v)]
    _, offs = _pack([loc[k] for k in small_names], F32, 8)
    shapes = [loc[k].shape for k in small_names]
    for dst, packed in zip((delta, new_m, new_v), _adam_small(*packs)):
        dst.update(zip(small_names, _unpack(packed, offs, shapes)))

    return (loss, grad_x[None], *[grads[k] for k in WEIGHTS], *[delta[k] for k in WEIGHTS],
            *[new_m[k] for k in WEIGHTS], *[new_v[k] for k in WEIGHTS])
```

```python
import functools

import numpy as np
import jax
import jax.numpy as jnp
from jax import lax
from jax.experimental import pallas as pl
from jax.experimental.pallas import tpu as pltpu

F32 = jnp.float32
BF16 = jnp.bfloat16
HI = lax.Precision.HIGHEST

D = 1024
NL = 2
NH = 4
DH = 128
DN = NH * DH
CH = 64
GW = 64
PW = 256
DFF = 2816
EPS = 1e-6
N_DEV = 8
LANES = 128
BF16_ROWS = 16
VMEM_MB = 56

ADAM_LR, ADAM_B1, ADAM_B2, ADAM_EPS, ADAM_WD, ADAM_STEP = 0.001, 0.9, 0.999, 1e-08, 0.01, 10

IN_BOUNDS = (0, 1536, 2048, 2064, 2320, 2576, 2832, 3088, 6160)
IN_WIDTHS = (1536, 512, 128, 256, 256, 256, 256, 3072)
POOL_WIN = ((1, 0), (2, 1), (4, 3), (8, 7))

NN = ((1,), (0,))
NT = ((1,), (1,))
TN = ((0,), (0,))


def _dot(a, b, dims, hi=False):
    if hi:
        prec = lax.Precision.HIGH if hi == "x3" else HI
        return lax.dot_general(a, b, (dims, ((), ())), precision=prec, preferred_element_type=F32)
    return lax.dot_general(a.astype(BF16), b.astype(BF16), (dims, ((), ())), preferred_element_type=F32)


def _S(shape, dtype=F32):
    return jax.ShapeDtypeStruct(tuple(shape), dtype)


def _full(shape):
    nd = len(shape)
    return pl.BlockSpec(tuple(shape), lambda *_: (0,) * nd)


def _rows(tt, w):
    return pl.BlockSpec((tt, w), lambda i: (i, 0))


def _call(body, name, grid, in_specs, out_specs, out_shape, scratch=()):
    return pl.pallas_call(
        body, name=name, grid=grid, in_specs=in_specs, out_specs=out_specs, out_shape=out_shape,
        scratch_shapes=list(scratch),
        compiler_params=pltpu.CompilerParams(
            dimension_semantics=("arbitrary",) * len(grid), vmem_limit_bytes=VMEM_MB << 20),
    )


def _iota(shape, axis):
    return lax.broadcasted_iota(jnp.int32, shape, axis)


def _colsum(a):
    return jnp.sum(a, axis=0, keepdims=True)


def _silu(x):
    return x * jax.nn.sigmoid(x)


def _modulate(x, g, sh, sc):
    xn = x * lax.rsqrt(jnp.mean(x * x, axis=-1, keepdims=True) + EPS)
    return (xn * g) * (1.0 + sc) + sh


def _stream_rows(mv_ref, i, tt, tc, k):
    isc = (i * tt + _iota((tt, 1), 0)) < tc
    return isc, jnp.where(isc, mv_ref[k:k + 1, :], mv_ref[3 + k:4 + k, :])


def _acc_stream(ref, k, isc, val):
    ref[k:k + 1, :] += _colsum(jnp.where(isc, val, 0.0))
    ref[3 + k:4 + k, :] += _colsum(jnp.where(isc, 0.0, val))


MOD_CT = 1536


def _mod_fwd(cc8, w_ada, b_ada3):
    def body(cc_ref, w_ref, b_ref, o_ref):
        o_ref[0] = _dot(_silu(cc_ref[...]), w_ref[0], NN) + b_ref[0]

    return _call(
        body, "mod_fwd", (NL, 6 * D // MOD_CT),
        [pl.BlockSpec((8, D), lambda l, j: (0, 0)), pl.BlockSpec((1, D, MOD_CT), lambda l, j: (l, 0, j)),
         pl.BlockSpec((1, 1, MOD_CT), lambda l, j: (l, 0, j))],
        pl.BlockSpec((1, 8, MOD_CT), lambda l, j: (l, 0, j)), _S((NL, 8, 6 * D)))(cc8, w_ada, b_ada3)


def _mod_bwd(cc8, w_ada, dmods):
    def body(cc_ref, w_ref, dm_ref, dw_ref, db_ref, dcc_ref):
        first = (pl.program_id(0) == 0) & (pl.program_id(1) == 0)
        cc = cc_ref[...]
        sg = jax.nn.sigmoid(cc)
        dm = dm_ref[0]
        dw_ref[0] = _dot(cc * sg, dm, TN)
        db_ref[0] = dm[0:1, :] + dm[1:2, :]

        @pl.when(first)
        def _():
            dcc_ref[...] = jnp.zeros_like(dcc_ref)

        dcc_ref[...] += _dot(dm, w_ref[0], NT) * (sg * (1.0 + cc * (1.0 - sg)))

    return _call(
        body, "mod_bwd", (NL, 6 * D // MOD_CT),
        [pl.BlockSpec((8, D), lambda l, j: (0, 0)), pl.BlockSpec((1, D, MOD_CT), lambda l, j: (l, 0, j)),
         pl.BlockSpec((1, 8, MOD_CT), lambda l, j: (l, 0, j))],
        [pl.BlockSpec((1, D, MOD_CT), lambda l, j: (l, 0, j)), pl.BlockSpec((1, 1, MOD_CT), lambda l, j: (l, 0, j)),
         pl.BlockSpec((8, D), lambda l, j: (0, 0))],
        [_S((NL, D, 6 * D)), _S((NL, 1, 6 * D)), _S((8, D))])(cc8, w_ada, dmods)


def _inproj_fwd(X, mv, g, ws, tc, tt):
    T = X.shape[0]
    nw = len(ws)

    def body(x_ref, mv_ref, g_ref, *refs):
        w_refs, h_ref, p_refs = refs[:nw], refs[nw], refs[nw + 1:]
        i = pl.program_id(0)
        _, sh = _stream_rows(mv_ref, i, tt, tc, 0)
        _, sc = _stream_rows(mv_ref, i, tt, tc, 1)
        hb = _modulate(x_ref[...], g_ref[...], sh, sc).astype(BF16)
        h_ref[...] = hb
        for w_ref, p_ref in zip(w_refs, p_refs):
            p_ref[...] = jnp.dot(hb, w_ref[...], preferred_element_type=F32)

    return _call(
        body, "inproj_fwd", (T // tt,),
        [_rows(tt, D), _full((8, D)), _full((1, D))] + [_full(w.shape) for w in ws],
        [_rows(tt, D)] + [_rows(tt, w.shape[1]) for w in ws],
        [_S((T, D), BF16)] + [_S((T, w.shape[1])) for w in ws])(X, mv, g, *ws)


def _inproj_bwd(X, mv, g, ws, dps, dp_w, dres, tc, tt):
    T = X.shape[0]
    nw, nd = len(ws), len(dps)

    def body(x_ref, mv_ref, g_ref, dres_ref, *refs):
        w_refs, dp_refs = refs[:nw], refs[nw:nw + nd]
        dx_ref, dg_ref, dm_ref = refs[nw + nd:]
        i = pl.program_id(0)
        isc, sh = _stream_rows(mv_ref, i, tt, tc, 0)
        _, sc = _stream_rows(mv_ref, i, tt, tc, 1)
        dh = None
        for dp_ref, k in zip(dp_refs, dp_w):
            t = _dot(dp_ref[...], w_refs[k][...], NT)
            dh = t if dh is None else dh + t
        _, vjp = jax.vjp(_modulate, x_ref[...], g_ref[...], sh, sc)
        dx, dg, dsh, dsc = vjp(dh)
        dx_ref[...] = dres_ref[...] + dx

        @pl.when(i == 0)
        def _():
            dg_ref[...] = jnp.zeros_like(dg_ref)
            dm_ref[...] = jnp.zeros_like(dm_ref)

        dg_ref[...] += dg
        _acc_stream(dm_ref, 0, isc, dsh)
        _acc_stream(dm_ref, 1, isc, dsc)

    return _call(
        body, "inproj_bwd", (T // tt,),
        [_rows(tt, D), _full((8, D)), _full((1, D)), _rows(tt, D)] + [_full(w.shape) for w in ws]
        + [_rows(tt, dp.shape[1]) for dp in dps],
        [_rows(tt, D), _full((1, D)), _full((8, D))],
        [_S((T, D)), _S((1, D)), _S((8, D))])(X, mv, g, dres, *ws, *dps)


def _dw(A, B, tt):
    T, K = A.shape
    N = B.shape[1]
    tt = 3 * tt if T % (3 * tt) == 0 else tt
    tn = next(t for t in (1024, 512, 256, LANES) if N % t == 0)

    def body(a_ref, b_ref, o_ref):
        @pl.when(pl.program_id(1) == 0)
        def _():
            o_ref[...] = jnp.zeros_like(o_ref)

        o_ref[...] += _dot(a_ref[...], b_ref[...], TN)

    return _call(
        body, "dw", (N // tn, T // tt),
        [pl.BlockSpec((tt, K), lambda j, i: (i, 0)), pl.BlockSpec((tt, tn), lambda j, i: (i, j))],
        pl.BlockSpec((K, tn), lambda j, i: (0, j)), _S((K, N)))(A, B)


def _halo_specs(T, tt, cw, col):
    r8, nb8 = tt // 8, T // 8
    return [pl.BlockSpec((tt, cw), lambda j, i: (i, col(j))),
            pl.BlockSpec((8, cw), lambda j, i: (jnp.maximum(i * r8 - 1, 0), col(j))),
            pl.BlockSpec((8, cw), lambda j, i: (jnp.minimum((i + 1) * r8, nb8 - 1), col(j)))]


def _shifts(a, prev8, next8, i, tt, tc, T):
    r = _iota((tt, 1), 0)
    t = i * tt + r
    dn = jnp.where(r == 0, prev8[7:8, :], pltpu.roll(a, 1, 0))
    dn = jnp.where((t == 0) | (t == tc), 0.0, dn)
    up = jnp.where(r == tt - 1, next8[0:1, :], pltpu.roll(a, tt - 1, 0))
    up = jnp.where((t == T - 1) | (t == tc - 1), 0.0, up)
    return dn, up


def _dn_post(y, part):
    a = _silu(y)
    nrm = lax.rsqrt(jnp.sum(a * a, axis=-1, keepdims=True) + EPS)
    f = jnp.where(part == 0, nrm * (DH ** -0.5), jnp.where(part == 1, nrm, 1.0))
    return a * f


def _conv3(w_ref, dn, mid, up):
    return w_ref[0:1, :] * dn + w_ref[1:2, :] * mid + w_ref[2:3, :] * up


def _dnprep_fwd(pq, cw, tc, tt):
    T = pq.shape[0]

    def body(p_ref, pp_ref, pn_ref, w_ref, a_ref):
        part, i = pl.program_id(0), pl.program_id(1)
        p = p_ref[...]
        dn, up = _shifts(p, pp_ref[...], pn_ref[...], i, tt, tc, T)
        y = _conv3(w_ref, dn, p, up)
        for h in range(NH):
            a_ref[:, _hs(h)] = _dn_post(y[:, _hs(h)], part)

    return _call(
        body, "dnprep_fwd", (3, T // tt),
        _halo_specs(T, tt, DN, lambda j: j) + [pl.BlockSpec((3, DN), lambda j, i: (0, j))],
        pl.BlockSpec((tt, DN), lambda j, i: (i, j)), _S((T, 3 * DN)))(pq, pq, pq, cw)


def _dnprep_bwd_act(pq, cw, da_f, da_b, tc, tt):
    T = pq.shape[0]

    def body(p_ref, pp_ref, pn_ref, w_ref, df_ref, db_ref, dy_ref):
        part, i = pl.program_id(0), pl.program_id(1)
        p = p_ref[...]
        dn, up = _shifts(p, pp_ref[...], pn_ref[...], i, tt, tc, T)
        y = _conv3(w_ref, dn, p, up)
        for h in range(NH):
            _, vjp = jax.vjp(lambda yh: _dn_post(yh, part), y[:, _hs(h)])
            dy_ref[:, _hs(h)] = vjp(df_ref[:, _hs(h)] + db_ref[:, _hs(h)])[0]

    blk = pl.BlockSpec((tt, DN), lambda j, i: (i, j))
    return _call(
        body, "dnprep_bwd_act", (3, T // tt),
        _halo_specs(T, tt, DN, lambda j: j) + [pl.BlockSpec((3, DN), lambda j, i: (0, j)), blk, blk],
        blk, _S((T, 3 * DN)))(pq, pq, pq, cw, da_f, da_b)


def _conv_bwd(dy, p, cw, tc, tt):
    T, W = p.shape
    cb = DN

    def body(dy_ref, dyp_ref, dyn_ref, p_ref, pp_ref, pn_ref, w_ref, dp_ref, dw_ref):
        i = pl.program_id(1)
        dy, p_ = dy_ref[...], p_ref[...]
        ddn, dup = _shifts(dy, dyp_ref[...], dyn_ref[...], i, tt, tc, T)
        dp_ref[...] = _conv3(w_ref, dup, dy, ddn)
        pdn, pup = _shifts(p_, pp_ref[...], pn_ref[...], i, tt, tc, T)

        @pl.when(i == 0)
        def _():
            dw_ref[...] = jnp.zeros_like(dw_ref)

        dw_ref[0:1, :] += _colsum(dy * pdn)
        dw_ref[1:2, :] += _colsum(dy * p_)
        dw_ref[2:3, :] += _colsum(dy * pup)

    wspec = pl.BlockSpec((3, cb), lambda j, i: (0, j))
    return _call(
        body, "conv_bwd", (W // cb, T // tt),
        _halo_specs(T, tt, cb, lambda j: j) * 2 + [wspec],
        [pl.BlockSpec((tt, cb), lambda j, i: (i, j)), wspec], [_S((T, W)), _S((3, W))])(dy, dy, dy, p, p, p, cw)


def _sc_fwd(sx, sb, sc_, cw, tc, tt):
    T = sx.shape[0]

    def body(x_ref, xp_ref, xn_ref, c_ref, cp_ref, cn_ref, b_ref, w_ref, y_ref):
        i = pl.program_id(1)
        u = c_ref[...] * x_ref[...]
        dn, up = _shifts(u, cp_ref[...] * xp_ref[...], cn_ref[...] * xn_ref[...], i, tt, tc, T)
        y_ref[...] = b_ref[...] * _conv3(w_ref, dn, u, up)

    blk = pl.BlockSpec((tt, LANES), lambda j, i: (i, j))
    return _call(
        body, "sc_fwd", (PW // LANES, T // tt),
        _halo_specs(T, tt, LANES, lambda j: j) * 2 + [blk, pl.BlockSpec((3, LANES), lambda j, i: (0, j))],
        blk, _S((T, PW)))(sx, sx, sx, sc_, sc_, sc_, sb, cw)


def _sc_bwd(sx, sb, sc_, cw, dy, tc, tt):
    T = sx.shape[0]

    def body(x_ref, xp_ref, xn_ref, c_ref, cp_ref, cn_ref, b_ref, bp_ref, bn_ref, dy_ref, dyp_ref, dyn_ref, w_ref,
             dx_ref, db_ref, dc_ref, dw_ref):
        i = pl.program_id(1)
        x, c, dy_ = x_ref[...], c_ref[...], dy_ref[...]
        u = c * x
        udn, uup = _shifts(u, cp_ref[...] * xp_ref[...], cn_ref[...] * xn_ref[...], i, tt, tc, T)
        db_ref[...] = dy_ * _conv3(w_ref, udn, u, uup)
        e = dy_ * b_ref[...]
        edn, eup = _shifts(e, dyp_ref[...] * bp_ref[...], dyn_ref[...] * bn_ref[...], i, tt, tc, T)
        du = _conv3(w_ref, eup, e, edn)
        dx_ref[...] = du * c
        dc_ref[...] = du * x

        @pl.when(i == 0)
        def _():
            dw_ref[...] = jnp.zeros_like(dw_ref)

        dw_ref[0:1, :] += _colsum(e * udn)
        dw_ref[1:2, :] += _colsum(e * u)
        dw_ref[2:3, :] += _colsum(e * uup)

    blk = pl.BlockSpec((tt, LANES), lambda j, i: (i, j))
    wspec = pl.BlockSpec((3, LANES), lambda j, i: (0, j))
    return _call(
        body, "sc_bwd", (PW // LANES, T // tt),
        _halo_specs(T, tt, LANES, lambda j: j) * 4 + [wspec],
        [blk, blk, blk, wspec], [_S((T, PW))] * 3 + [_S((3, PW))])(
            sx, sx, sx, sc_, sc_, sc_, sb, sb, sb, dy, dy, dy, cw)


def _group_select(vals):
    g = _iota((1, PW), 1) // (PW // len(POOL_WIN))
    return jnp.where(g == 0, vals[0], jnp.where(g == 1, vals[1], jnp.where(g == 2, vals[2], vals[3])))


def _nested_box(get, mirror):
    acc, outs, pl_, ph_ = get(0), [], 0, 0
    for lo, hi in POOL_WIN:
        if mirror:
            lo, hi = hi, lo
        for k in range(pl_ + 1, lo + 1):
            acc = acc + get(-k)
        for k in range(ph_ + 1, hi + 1):
            acc = acc + get(k)
        pl_, ph_ = lo, hi
        outs.append(acc)
    return _group_select(outs)


def _box_tokens(a, n, mirror):
    idx = _iota((n, 1), 0)

    def get(k):
        if k == 0:
            return a
        return jnp.where((idx + k >= 0) & (idx + k < n), pltpu.roll(a, (-k) % n, 0), 0.0)

    return _nested_box(get, mirror)


def _inv_count(pos, n):
    return _group_select([1.0 / (jnp.minimum(pos + hi, n - 1) - jnp.maximum(pos - lo, 0) + 1).astype(F32)
                          for lo, hi in POOL_WIN])


def _pool_rows(ref, r, R, tc, mirror):
    def get(k):
        rr = r + k
        rc = jnp.clip(rr, 0, R - 1)
        v = ref[pl.ds(pl.multiple_of(tc + rc * GW, GW), GW), :]
        if mirror:
            v = v * _inv_count(jnp.full((1, PW), rc, jnp.int32), R)
        return jnp.where((rr >= 0) & (rr < R), v, 0.0)

    return _nested_box(get, mirror)


def _pool_fwd(u, pwbd, ps, tc):
    T = u.shape[0]
    R = (T - tc) // GW

    def body(u_ref, pw_ref, ps_ref, y_ref):
        pw, scale = pw_ref[...], ps_ref[...]
        uc = u_ref[0:tc, :]
        mc = _box_tokens(uc, tc, False) * _inv_count(_iota((tc, 1), 0), tc)
        y_ref[0:tc, :] = _dot(mc - uc, pw, NN) * scale
        inv_c = _inv_count(_iota((GW, 1), 0), GW)

        def row(r, carry):
            rs = _pool_rows(u_ref, r, R, tc, False) * _inv_count(jnp.full((1, PW), r, jnp.int32), R)
            m = _box_tokens(rs, GW, False) * inv_c
            sl = pl.ds(pl.multiple_of(tc + r * GW, GW), GW)
            y_ref[sl, :] = _dot(m - u_ref[sl, :], pw, NN) * scale
            return carry

        lax.fori_loop(0, R, row, 0)

    return pl.pallas_call(
        body, name="pool_fwd", out_shape=_S((T, PW)),
        compiler_params=pltpu.CompilerParams(vmem_limit_bytes=VMEM_MB << 20))(u, pwbd, ps)


def _pool_bwd(u, pwbd, ps, dy, tc):
    T = u.shape[0]
    R = (T - tc) // GW

    def body(u_ref, pw_ref, ps_ref, dy_ref, du_ref, dpw_ref, dps_ref, dd_ref):
        pw, scale = pw_ref[...], ps_ref[...]
        dpw_ref[...] = jnp.zeros_like(dpw_ref)
        dps_ref[...] = jnp.zeros_like(dps_ref)

        def back(d, dy_):
            dz = dy_ * scale
            dpw_ref[...] += _dot(d, dz, TN)
            dps_ref[...] += _colsum(dy_ * _dot(d, pw, NN))
            return _dot(dz, pw, NT)

        uc = u_ref[0:tc, :]
        inv_cc = _inv_count(_iota((tc, 1), 0), tc)
        ddc = back(_box_tokens(uc, tc, False) * inv_cc - uc, dy_ref[0:tc, :])
        du_ref[0:tc, :] = _box_tokens(ddc * inv_cc, tc, True) - ddc
        inv_c = _inv_count(_iota((GW, 1), 0), GW)

        def row1(r, carry):
            rs = _pool_rows(u_ref, r, R, tc, False) * _inv_count(jnp.full((1, PW), r, jnp.int32), R)
            m = _box_tokens(rs, GW, False) * inv_c
            sl = pl.ds(pl.multiple_of(tc + r * GW, GW), GW)
            dd_ref[sl, :] = back(m - u_ref[sl, :], dy_ref[sl, :])
            return carry

        lax.fori_loop(0, R, row1, 0)

        def row2(r, carry):
            t1 = _pool_rows(dd_ref, r, R, tc, True)
            sl = pl.ds(pl.multiple_of(tc + r * GW, GW), GW)
            du_ref[sl, :] = _box_tokens(t1 * inv_c, GW, True) - dd_ref[sl, :]
            return carry

        lax.fori_loop(0, R, row2, 0)

    return pl.pallas_call(
        body, name="pool_bwd", out_shape=[_S((T, PW)), _S((PW, PW)), _S((1, PW))],
        scratch_shapes=[pltpu.VMEM((T, PW), F32)],
        compiler_params=pltpu.CompilerParams(vmem_limit_bytes=VMEM_MB << 20))(u, pwbd, ps, dy)


def _scan_consts():
    i = np.arange(CH)
    lower = (i[:, None] >= i[None, :]).astype(np.float32)
    return jnp.asarray(np.stack([lower, lower.T])), jnp.asarray(np.stack([lower.T, lower]))


def _gates(pab, al, dtb, csum):
    sp_in = pab + dtb
    sp = jnp.maximum(sp_in, 0.0) + jnp.log(1.0 + jnp.exp(-jnp.abs(sp_in)))
    nexp = -jnp.exp(al)
    gm = nexp * sp
    return gm, jax.nn.sigmoid(pab), _dot(csum, gm, NN, hi=True), sp_in, nexp


def _lane_col(m, j):
    return jnp.sum(jnp.where(_iota(m.shape, 1) == j, m, 0.0), axis=1, keepdims=True)


def _hs(h):
    return slice(h * DH, (h + 1) * DH)


HS = NH * CH
X3 = "x3"


def _stack(x, base=0):
    return jnp.concatenate([x[:, base + h * DH:base + (h + 1) * DH] for h in range(NH)], axis=0)


def _heads(st):
    return [st[h * CH:(h + 1) * CH] for h in range(NH)]


def _rowsum(a):
    return jnp.sum(a, axis=1, keepdims=True)


def _row_of(col):
    e0 = (_iota((8, LANES), 1) == 0).astype(F32)
    return _dot(e0, jnp.broadcast_to(col, (HS, LANES)), NT, hi=True)[0:1, :]


def _inverses(nms):
    eye = (_iota((HS, HS), 0) == _iota((HS, HS), 1)).astype(F32)
    x0s, mps = [eye + nm for nm in nms], list(nms)
    for _ in range(5):
        mps = [_dot(mp, mp, NN) for mp in mps]
        x0s = [x0 + _dot(x0, mp, NN) for x0, mp in zip(x0s, mps)]
    rs = [eye - _dot(eye - nm, x0, NN, hi=X3) for nm, x0 in zip(nms, x0s)]
    return [x0 + _dot(x0, r, NN) for x0, r in zip(x0s, rs)]


def _dn_chunk_pre(qkv, pab, al, dtb, csum_d, d):
    gm, bm, gcm, sp_in, nexp = _gates(pab, al, dtb, csum_d)
    gc = jnp.concatenate([_lane_col(gcm, d * NH + h) for h in range(NH)], axis=0)
    beta = jnp.concatenate([_lane_col(bm, 8 + d * NH + h) for h in range(NH)], axis=0)
    q, k, v = _stack(qkv, 0), _stack(qkv, DN), _stack(qkv, 2 * DN)
    ii, jj = _iota((HS, HS), 0), _iota((HS, HS), 1)
    sh = CH.bit_length() - 1
    same = (ii >> sh) == (jj >> sh)
    incl = same & ((ii >= jj) if d == 0 else (ii <= jj))
    strict = same & ((ii > jj) if d == 0 else (ii < jj))
    Di = jnp.where(incl, jnp.exp(jnp.where(incl, gc - _row_of(gc), 0.0)), 0.0)
    Ds = jnp.where(strict, Di, 0.0)
    kb = k * beta
    kk = _dot(kb, k, NT)
    return dict(q=q, k=k, v=v, beta=beta, gc=gc, gm=gm, bm=bm, sp_in=sp_in, nexp=nexp, Di=Di, Ds=Ds, strict=strict,
                last=CH - 1 if d == 0 else 0, kb=kb, kk=kk)


def _dn_chunk_post(c, tm):
    q, k, v, beta, gc, kb, last = (c[n] for n in ("q", "k", "v", "beta", "gc", "kb", "last"))
    E = jnp.exp(gc)
    gls = [gc[h * CH + last:h * CH + last + 1, :] for h in range(NH)]
    xs = jnp.exp(jnp.concatenate([jnp.broadcast_to(g, (CH, 1)) for g in gls], axis=0) - gc)
    qk = _dot(q, k, NT)
    return dict(c, tm=tm, E=E, gls=gls, xs=xs, qk=qk, u=_dot(tm, v * beta, NN, hi=X3), w=_dot(tm, kb * E, NN, hi=X3),
                ks=k * xs, qd=q * E, aqk=qk * c["Di"])


def _dn_chunks_bwd_math(cs, Ss, dS2s, dos, vns, dvns):
    I = range(len(cs))
    q, k, v, beta, tm, E, xs, kb, u, w = ([c[n] for c in cs] for n in ("q", "k", "v", "beta", "tm", "E", "xs", "kb", "u", "w"))
    doh, vnh, dvnh = ([_heads(a) for a in l] for l in (dos, vns, dvns))
    cat = lambda parts: jnp.concatenate(parts, axis=0)
    dqd = [cat([_dot(doh[i][h], Ss[i][h], NT) for h in range(NH)]) for i in I]
    dks = [cat([_dot(vnh[i][h], dS2s[i][h], NT) for h in range(NH)]) for i in I]
    dw = [-cat([_dot(dvnh[i][h], Ss[i][h], NT) for h in range(NH)]) for i in I]
    daqk = [_dot(dos[i], vns[i], NT) for i in I]
    drb = [_dot(tm[i], dvns[i], TN, hi=X3) for i in I]
    drw = [_dot(tm[i], dw[i], TN, hi=X3) for i in I]
    dA = [jnp.where(cs[i]["strict"], -(_dot(drb[i], u[i], NT) + _dot(drw[i], w[i], NT)), 0.0) for i in I]
    dM1 = [dA[i] * cs[i]["Ds"] for i in I]
    dM2 = [daqk[i] * cs[i]["Di"] for i in I]
    dkb = [_dot(dM1[i], k[i], NN) + drw[i] * E[i] for i in I]
    dk = [_dot(dM1[i], kb[i], TN) + _dot(dM2[i], q[i], TN) + dks[i] * xs[i] for i in I]
    dq = [_dot(dM2[i], k[i], NN) + dqd[i] * E[i] for i in I]
    on_diag = _iota((HS, HS), 0) == _iota((HS, HS), 1)
    out = []
    for i in I:
        G = dM1[i] * cs[i]["kk"] + dM2[i] * cs[i]["qk"]
        col = _rowsum(jnp.where(on_diag, jnp.broadcast_to(_colsum(G), (HS, HS)), 0.0))
        dxx = _rowsum(dks[i] * k[i]) * xs[i]
        dgc = _rowsum(G) - col + (_rowsum(dqd[i] * q[i]) + _rowsum(drw[i] * kb[i])) * E[i] - dxx
        at_last = _iota((CH, 1), 0) == cs[i]["last"]
        ends = []
        for h in range(NH):
            dgl = (_colsum(_rowsum(Ss[i][h] * dS2s[i][h])) * jnp.exp(cs[i]["gls"][h])
                   + _colsum(dxx[h * CH:(h + 1) * CH]))
            ends.append(jnp.where(at_last, dgl, 0.0))
        dbeta = _rowsum(drb[i] * v[i]) + _rowsum(dkb[i] * k[i])
        out.append((dq[i], dk[i] + dkb[i] * beta[i], drb[i] * beta[i], dgc + cat(ends), dbeta))
    return out


def _chunk_group(n, want=2):
    g = want
    while n % g:
        g //= 2
    return g


def _dn_chunks_fwd(qkv, pab, alr, dtr):
    T = qkv.shape[0]
    n = T // CH
    G = _chunk_group(n, 4)
    csum, _ = _scan_consts()

    def body(q_ref, p_ref, cs_ref, al_ref, dt_ref, *outs):
        inst = [(g, d) for g in range(G) for d in range(2)]
        pres = [_dn_chunk_pre(q_ref[g * CH:(g + 1) * CH, :], p_ref[g * CH:(g + 1) * CH, :], al_ref[...], dt_ref[...],
                              cs_ref[d], d) for g, d in inst]
        tms = _inverses([-(p["kk"] * p["Ds"]) for p in pres])
        for (g, d), pre, tm in zip(inst, pres, tms):
            rows = slice(g * HS, (g + 1) * HS)
            u_ref, w_ref, ks_ref, qd_ref, aqk_ref, eg_ref, tm_ref = outs[7 * d:7 * d + 7]
            c = _dn_chunk_post(pre, tm)
            tm_ref[rows, :] = tm
            u_ref[rows, :] = c["u"]
            w_ref[rows, :] = c["w"].astype(BF16)
            ks_ref[rows, :] = c["ks"].astype(BF16)
            qd_ref[rows, :] = c["qd"].astype(BF16)
            aqk_ref[rows, :] = c["aqk"].astype(BF16)
            egs = [jnp.broadcast_to(jnp.exp(gl), (1, LANES)) for gl in c["gls"]]
            eg_ref[g * 8:(g + 1) * 8, :] = jnp.concatenate(egs + [jnp.zeros((8 - NH, LANES), F32)], axis=0)

    st = lambda w_: pl.BlockSpec((G * HS, w_), lambda i: (i, 0))
    one = [st(DH)] * 4 + [st(HS), pl.BlockSpec((G * 8, LANES), lambda i: (i, 0)), st(HS)]
    shp = [_S((n * HS, DH)), _S((n * HS, DH), BF16), _S((n * HS, DH), BF16), _S((n * HS, DH), BF16),
           _S((n * HS, HS), BF16), _S((n * 8, LANES)), _S((n * HS, HS))]
    outs = _call(
        body, "dn_chunks_fwd", (n // G,),
        [_rows(G * CH, 3 * DN), _rows(G * CH, LANES), _full((2, CH, CH)), _full((1, LANES)), _full((1, LANES))],
        one * 2, shp * 2)(qkv, pab, csum, alr, dtr)
    return tuple(outs[:7]), tuple(outs[7:])


def _scan_order(n, ncx):
    return (lambda i: i), (lambda i: jnp.where(i < ncx, ncx - 1 - i, n - 1 - (i - ncx)))


def _scan_specs(order):
    st = lambda w_: pl.BlockSpec((HS, w_), lambda i: (order(i), 0))
    return dict(st=st(DH), aqk=st(HS), eg=pl.BlockSpec((8, LANES), lambda i: (order(i), 0)),
                tok=pl.BlockSpec((CH, DN), lambda i: (order(i), 0)), state=pl.BlockSpec((1, DN, DH), lambda i: (order(i), 0, 0)))


def _scan_fwd(parts, T, tc):
    n = T // CH
    orders = _scan_order(n, tc // CH)

    def body(*refs):
        S_f, S_b = refs[-2:]

        @pl.when(pl.program_id(0) == 0)
        def _():
            S_f[...] = jnp.zeros_like(S_f)
            S_b[...] = jnp.zeros_like(S_b)

        for d, S in enumerate((S_f, S_b)):
            u_ref, w_ref, ks_ref, qd_ref, aqk_ref, eg_ref = refs[6 * d:6 * d + 6]
            o_ref, ss_ref, vn_ref = refs[12 + 3 * d:15 + 3 * d]
            ss_ref[0] = S[...]
            Sh = [S[_hs(h), :] for h in range(NH)]
            wh, ksh, qdh = _heads(w_ref[...]), _heads(ks_ref[...]), _heads(qd_ref[...])
            vn = u_ref[...] - jnp.concatenate([_dot(wh[h], Sh[h], NN) for h in range(NH)], axis=0)
            vn_ref[...] = vn
            av, vnh = _heads(_dot(aqk_ref[...], vn, NN)), _heads(vn)
            for h in range(NH):
                o_ref[:, _hs(h)] = _dot(qdh[h], Sh[h], NN) + av[h]
                S[_hs(h), :] = Sh[h] * eg_ref[h:h + 1, :] + _dot(ksh[h], vnh[h], TN)

    ins, outs, shp = [], [], []
    for d in range(2):
        sp = _scan_specs(orders[d])
        ins += [sp["st"]] * 4 + [sp["aqk"], sp["eg"]]
        outs += [sp["tok"], sp["state"], sp["st"]]
        shp += [_S((T, DN)), _S((n, DN, DH)), _S((n * HS, DH))]
    res = _call(body, "scan_fwd", (n,), ins, outs, shp,
                scratch=[pltpu.VMEM((DN, DH), F32), pltpu.VMEM((DN, DH), F32)])(*parts[0][:6], *parts[1][:6])
    return tuple(res[:3]), tuple(res[3:])


def _scan_bwd(do, parts, tc):
    T = do.shape[0]
    n = T // CH
    fwd_orders = _scan_order(n, tc // CH)
    orders = [lambda s, f=f: f(n - 1 - s) for f in fwd_orders]

    def body(*refs):
        dS_f, dS_b = refs[-2:]

        @pl.when(pl.program_id(0) == 0)
        def _():
            dS_f[...] = jnp.zeros_like(dS_f)
            dS_b[...] = jnp.zeros_like(dS_b)

        for d, dS in enumerate((dS_f, dS_b)):
            do_ref, w_ref, ks_ref, qd_ref, aqk_ref, eg_ref = refs[6 * d:6 * d + 6]
            dvn_ref, dss_ref = refs[12 + 2 * d:14 + 2 * d]
            dss_ref[0] = dS[...]
            dSh = [dS[_hs(h), :] for h in range(NH)]
            wh, ksh, qdh = _heads(w_ref[...]), _heads(ks_ref[...]), _heads(qd_ref[...])
            do_st = _stack(do_ref[...])
            dvn = _dot(aqk_ref[...], do_st, TN) + jnp.concatenate([_dot(ksh[h], dSh[h], NN) for h in range(NH)], axis=0)
            dvn_ref[...] = dvn
            doh, dvnh = _heads(do_st), _heads(dvn)
            for h in range(NH):
                dS[_hs(h), :] = _dot(qdh[h], doh[h], TN) + dSh[h] * eg_ref[h:h + 1, :] - _dot(wh[h], dvnh[h], TN)

    ins, outs, shp, args = [], [], [], []
    for d in range(2):
        sp = _scan_specs(orders[d])
        ins += [sp["tok"]] + [sp["st"]] * 3 + [sp["aqk"], sp["eg"]]
        outs += [sp["st"], sp["state"]]
        shp += [_S((n * HS, DH)), _S((n, DN, DH))]
        args += [do, *parts[d][1:6]]
    res = _call(body, "scan_bwd", (n,), ins, outs, shp,
                scratch=[pltpu.VMEM((DN, DH), F32), pltpu.VMEM((DN, DH), F32)])(*args)
    return tuple(res[:2]), tuple(res[2:])


def _dn_chunks_bwd(qkv, pab, alr, dtr, do, fwd, bwd):
    T = qkv.shape[0]
    n = T // CH
    G = _chunk_group(n)
    csum, csum_t = _scan_consts()

    def body(q_ref, p_ref, do_ref, cs_ref, cst_ref, al_ref, dt_ref, *refs):
        dq_refs, dp_refs, acc_ref = refs[10:12], refs[12:14], refs[14]

        @pl.when(pl.program_id(0) == 0)
        def _():
            acc_ref[...] = jnp.zeros_like(acc_ref)

        lane = _iota((CH, LANES), 1)
        inst = [(g, d) for g in range(G) for d in range(2)]
        cs, Ss, dS2s, dos, vns, dvns = [], [], [], [], [], []
        for g, d in inst:
            tok, rows = slice(g * CH, (g + 1) * CH), slice(g * HS, (g + 1) * HS)
            vn_ref, dvn_ref, ss_ref, dss_ref, tm_ref = refs[5 * d:5 * d + 5]
            cs.append(_dn_chunk_post(
                _dn_chunk_pre(q_ref[tok, :], p_ref[tok, :], al_ref[...], dt_ref[...], cs_ref[d], d), tm_ref[rows, :]))
            Ss.append([ss_ref[g, _hs(h), :] for h in range(NH)])
            dS2s.append([dss_ref[g, _hs(h), :] for h in range(NH)])
            dos.append(_stack(do_ref[tok, :]))
            vns.append(vn_ref[rows, :])
            dvns.append(dvn_ref[rows, :])
        for (g, d), c, (dq, dk, dv, dgc, dbeta) in zip(inst, cs, _dn_chunks_bwd_math(cs, Ss, dS2s, dos, vns, dvns)):
            tok = slice(g * CH, (g + 1) * CH)
            dgcm = jnp.zeros((CH, LANES), F32)
            dbm = jnp.zeros((CH, LANES), F32)
            for h, (a, b_, c_, e, f) in enumerate(zip(*map(_heads, (dq, dk, dv, dgc, dbeta)))):
                dq_refs[d][tok, _hs(h)] = a
                dq_refs[d][tok, _hs(NH + h)] = b_
                dq_refs[d][tok, _hs(2 * NH + h)] = c_
                dgcm = jnp.where(lane == d * NH + h, e, dgcm)
                dbm = jnp.where(lane == 8 + d * NH + h, f, dbm)
            dgm = _dot(cst_ref[d], dgcm, NN, hi=True)
            dsp = dgm * c["nexp"] * jax.nn.sigmoid(c["sp_in"])
            dp_refs[d][tok, :] = dsp + dbm * c["bm"] * (1.0 - c["bm"])
            acc_ref[0:1, :] += _colsum(dgm * c["gm"])
            acc_ref[1:2, :] += _colsum(dsp)

    st = pl.BlockSpec((G * HS, DH), lambda i: (i, 0))
    state = pl.BlockSpec((G, DN, DH), lambda i: (i, 0, 0))
    return _call(
        body, "dn_chunks_bwd", (n // G,),
        [_rows(G * CH, 3 * DN), _rows(G * CH, LANES), _rows(G * CH, DN), _full((2, CH, CH)), _full((2, CH, CH)),
         _full((1, LANES)), _full((1, LANES))] + [st, st, state, state, pl.BlockSpec((G * HS, HS), lambda i: (i, 0))] * 2,
        [_rows(G * CH, 3 * DN)] * 2 + [_rows(G * CH, LANES)] * 2 + [_full((8, LANES))],
        [_S((T, 3 * DN))] * 2 + [_S((T, LANES))] * 2 + [_S((8, LANES))])(
            qkv, pab, do, csum, csum_t, alr, dtr, *fwd, *bwd)


def _head_out(o, z, g):
    on = o * lax.rsqrt(jnp.mean(o * o, axis=-1, keepdims=True) + EPS) * g
    return on * _silu(z)


def _mix_branches(of_ref, ob_ref, z_ref, yp_ref, ys_ref, pg_ref, gdn_ref, wa_ref, wb_ref, wc_ref):
    ons, ya = [], None
    for h in range(NH):
        on = _head_out(of_ref[:, _hs(h)] + ob_ref[:, _hs(h)], z_ref[:, _hs(h)], gdn_ref[...])
        t = _dot(on, wa_ref[_hs(h), :], NN)
        ya = t if ya is None else ya + t
        ons.append(on)
    ys = [ya, _dot(yp_ref[...], wb_ref[...], NN), _dot(ys_ref[...], wc_ref[...], NN)]
    sg = [jax.nn.sigmoid(pg_ref[:, k * D:(k + 1) * D]) for k in range(3)]
    return ons, ys, sg


def _mix_fwd(X, of, ob, z, yp, ys, pg, mv, gdn, wa, wb, wc, wo, tc, tt):
    T = X.shape[0]

    def body(x_ref, of_ref, ob_ref, z_ref, yp_ref, ys_ref, pg_ref, mv_ref, gdn_ref, wa_ref, wb_ref, wc_ref, wo_ref,
             x1_ref):
        _, yb, sg = _mix_branches(of_ref, ob_ref, z_ref, yp_ref, ys_ref, pg_ref, gdn_ref, wa_ref, wb_ref, wc_ref)
        mix = _dot(sg[0] * yb[0] + sg[1] * yb[1] + sg[2] * yb[2], wo_ref[...], NN)
        _, gate = _stream_rows(mv_ref, pl.program_id(0), tt, tc, 2)
        x1_ref[...] = x_ref[...] + gate * mix

    return _call(
        body, "mix_fwd", (T // tt,),
        [_rows(tt, D), _rows(tt, DN), _rows(tt, DN), _rows(tt, DN), _rows(tt, PW), _rows(tt, PW), _rows(tt, 3 * D),
         _full((8, D)), _full((1, DH)), _full(wa.shape), _full(wb.shape), _full(wc.shape), _full(wo.shape)],
        _rows(tt, D), _S((T, D)))(X, of, ob, z, yp, ys, pg, mv, gdn, wa, wb, wc, wo)


def _mix_bwd(dx1, of, ob, z, yp, ys, pg, mv, gdn, wa, wb, wc, wo, tc, tt):
    T = dx1.shape[0]

    def body(dx_ref, of_ref, ob_ref, z_ref, yp_ref, ys_ref, pg_ref, mv_ref, gdn_ref, wa_ref, wb_ref, wc_ref, wo_ref,
             do_ref, dz_ref, dyp_ref, dys_ref, dpg_ref, dwa_ref, dwb_ref, dwc_ref, dwo_ref, dgdn_ref, dm_ref):
        i = pl.program_id(0)

        @pl.when(i == 0)
        def _():
            for r in (dwa_ref, dwb_ref, dwc_ref, dwo_ref, dgdn_ref, dm_ref):
                r[...] = jnp.zeros_like(r)

        ons, yb, sg = _mix_branches(of_ref, ob_ref, z_ref, yp_ref, ys_ref, pg_ref, gdn_ref, wa_ref, wb_ref, wc_ref)
        ymix = sg[0] * yb[0] + sg[1] * yb[1] + sg[2] * yb[2]
        isc, gate = _stream_rows(mv_ref, i, tt, tc, 2)
        dx = dx_ref[...]
        dmix = dx * gate
        _acc_stream(dm_ref, 2, isc, dx * _dot(ymix, wo_ref[...], NN))
        dwo_ref[...] += _dot(ymix, dmix, TN)
        dymix = _dot(dmix, wo_ref[...], NT)
        dyb = []
        for k in range(3):
            dyb.append(dymix * sg[k])
            dpg_ref[:, k * D:(k + 1) * D] = dymix * yb[k] * sg[k] * (1.0 - sg[k])
        dwb_ref[...] += _dot(yp_ref[...], dyb[1], TN)
        dwc_ref[...] += _dot(ys_ref[...], dyb[2], TN)
        dyp_ref[...] = _dot(dyb[1], wb_ref[...], NT)
        dys_ref[...] = _dot(dyb[2], wc_ref[...], NT)
        dg = jnp.zeros((1, DH), F32)
        for h in range(NH):
            dwa_ref[_hs(h), :] += _dot(ons[h], dyb[0], TN)
            don = _dot(dyb[0], wa_ref[_hs(h), :], NT)
            _, vjp = jax.vjp(_head_out, of_ref[:, _hs(h)] + ob_ref[:, _hs(h)], z_ref[:, _hs(h)], gdn_ref[...])
            do_h, dz_h, dg_h = vjp(don)
            do_ref[:, _hs(h)] = do_h
            dz_ref[:, _hs(h)] = dz_h
            dg = dg + dg_h
        dgdn_ref[...] += dg

    return _call(
        body, "mix_bwd", (T // tt,),
        [_rows(tt, D), _rows(tt, DN), _rows(tt, DN), _rows(tt, DN), _rows(tt, PW), _rows(tt, PW), _rows(tt, 3 * D),
         _full((8, D)), _full((1, DH)), _full(wa.shape), _full(wb.shape), _full(wc.shape), _full(wo.shape)],
        [_rows(tt, DN), _rows(tt, DN), _rows(tt, PW), _rows(tt, PW), _rows(tt, 3 * D),
         _full(wa.shape), _full(wb.shape), _full(wc.shape), _full(wo.shape), _full((1, DH)), _full((8, D))],
        [_S((T, DN)), _S((T, DN)), _S((T, PW)), _S((T, PW)), _S((T, 3 * D)),
         _S(wa.shape), _S(wb.shape), _S(wc.shape), _S(wo.shape), _S((1, DH)), _S((8, D))])(
            dx1, of, ob, z, yp, ys, pg, mv, gdn, wa, wb, wc, wo)


def _ffn_fwd(X1, mv, g, wgu, wd, tc, tt):
    T = X1.shape[0]

    def body(x_ref, mv_ref, g_ref, wgu_ref, wd_ref, x2_ref, ff_ref):
        i = pl.program_id(0)
        _, sh = _stream_rows(mv_ref, i, tt, tc, 0)
        _, sc = _stream_rows(mv_ref, i, tt, tc, 1)
        _, gate = _stream_rows(mv_ref, i, tt, tc, 2)
        x = x_ref[...]
        gu = _dot(_modulate(x, g_ref[...], sh, sc), wgu_ref[...], NN)
        ff = _dot(_silu(gu[:, :DFF]) * gu[:, DFF:], wd_ref[...], NN)
        ff_ref[...] = ff
        x2_ref[...] = x + gate * ff

    return _call(
        body, "ffn_fwd", (T // tt,),
        [_rows(tt, D), _full((8, D)), _full((1, D)), _full(wgu.shape), _full(wd.shape)],
        [_rows(tt, D)] * 2, [_S((T, D))] * 2)(X1, mv, g, wgu, wd)


def _ffn_bwd(X1, ff, dx2, mv, g, wgu, wd, tc, tt):
    T = X1.shape[0]

    def body(x_ref, ff_ref, dx2_ref, mv_ref, g_ref, wgu_ref, wd_ref, dx1_ref, h_ref, dgu_ref, act_ref, dff_ref, dg_ref,
             dm_ref):
        i = pl.program_id(0)
        isc, sh = _stream_rows(mv_ref, i, tt, tc, 0)
        _, sc = _stream_rows(mv_ref, i, tt, tc, 1)
        _, gate = _stream_rows(mv_ref, i, tt, tc, 2)
        x, dx2_ = x_ref[...], dx2_ref[...]
        h, vjp = jax.vjp(_modulate, x, g_ref[...], sh, sc)
        hb = h.astype(BF16)
        h_ref[...] = hb
        gu = jnp.dot(hb, wgu_ref[...], preferred_element_type=F32)
        ga, up = gu[:, :DFF], gu[:, DFF:]
        sg = jax.nn.sigmoid(ga)
        act = (ga * sg * up).astype(BF16)
        act_ref[...] = act
        dff = dx2_ * gate
        dff_ref[...] = dff.astype(BF16)
        dact = _dot(dff, wd_ref[...], NT)
        dga = (dact * up * (sg * (1.0 + ga * (1.0 - sg)))).astype(BF16)
        dup = (dact * ga * sg).astype(BF16)
        dgu_ref[:, :DFF] = dga
        dgu_ref[:, DFF:] = dup
        dh = _dot(dga, wgu_ref[:, :DFF], NT) + _dot(dup, wgu_ref[:, DFF:], NT)
        dx, dg, dsh, dsc = vjp(dh)
        dx1_ref[...] = dx2_ + dx

        @pl.when(i == 0)
        def _():
            dg_ref[...] = jnp.zeros_like(dg_ref)
            dm_ref[...] = jnp.zeros_like(dm_ref)

        dg_ref[...] += dg
        _acc_stream(dm_ref, 0, isc, dsh)
        _acc_stream(dm_ref, 1, isc, dsc)
        _acc_stream(dm_ref, 2, isc, dx2_ * ff_ref[...])

    return _call(
        body, "ffn_bwd", (T // tt,),
        [_rows(tt, D), _rows(tt, D), _rows(tt, D), _full((8, D)), _full((1, D)), _full(wgu.shape), _full(wd.shape)],
        [_rows(tt, D), _rows(tt, D), _rows(tt, 2 * DFF), _rows(tt, DFF), _rows(tt, D), _full((1, D)), _full((8, D))],
        [_S((T, D)), _S((T, D), BF16), _S((T, 2 * DFF), BF16), _S((T, DFF), BF16), _S((T, D), BF16),
         _S((1, D)), _S((8, D))])(X1, ff, dx2, mv, g, wgu, wd)


def _rms(x, g):
    return x * lax.rsqrt(jnp.mean(x * x, axis=-1, keepdims=True) + EPS) * g


def _loss_head(X2, tgt, gf, tc):
    T = X2.shape[0]

    def body(x_ref, t_ref, g_ref, dx_ref, loss_ref, dg_ref):
        i = pl.program_id(0)

        @pl.when(i == 0)
        def _():
            dx_ref[...] = jnp.zeros_like(dx_ref)
            loss_ref[...] = jnp.zeros_like(loss_ref)
            dg_ref[...] = jnp.zeros_like(dg_ref)

        @pl.when(i > 0)
        def _():
            y, vjp = jax.vjp(_rms, x_ref[...], g_ref[...])
            err = y - t_ref[...]
            dx, dg = vjp(err * (1.0 / D))
            dx_ref[...] = dx
            dg_ref[...] += dg
            loss_ref[...] += (0.5 / D) * jnp.sum(jnp.sum(err * err, axis=1, keepdims=True), axis=0, keepdims=True)

    return _call(
        body, "loss_head", (T // tc,),
        [_rows(tc, D), pl.BlockSpec((tc, D), lambda i: (jnp.maximum(i - 1, 0), 0)), _full((1, D))],
        [_rows(tc, D), _full((8, LANES)), _full((1, D))],
        [_S((T, D)), _S((8, LANES)), _S((1, D))])(X2, tgt, gf)


def _block_diag(pw):
    g, n = pw.shape[0], pw.shape[1]
    out = jnp.zeros((g * n, g * n), pw.dtype)
    for k in range(g):
        out = lax.dynamic_update_slice(out, pw[k], (k * n, k * n))
    return out


def _split_w_in(w):
    parts = [w[:, IN_BOUNDS[k]:IN_BOUNDS[k + 1]] for k in range(8)]
    parts[2] = jnp.pad(parts[2], ((0, 0), (0, LANES - 16)))
    return parts


def _mod_rows(mods_l, k0):
    rows = [mods_l[s, (k0 + k) * D:(k0 + k + 1) * D] for s in (0, 1) for k in range(3)]
    return jnp.stack(rows + [jnp.zeros((D,), F32)] * 2)


def _lane_row(v8):
    return jnp.pad(v8.reshape(1, 8), ((0, 0), (0, LANES - 8)))


def _device_step(x, c, ctx, tgt, wts, tt):
    tc = ctx.shape[0]
    X = jnp.concatenate([ctx, x], axis=0)
    cc8 = jnp.concatenate([wts["c_ctx"][None, :], c, jnp.zeros((6, D), F32)], axis=0)
    w_ada = wts["w_ada"].astype(BF16)
    mods = _mod_fwd(cc8, w_ada, wts["b_ada"].reshape(NL, 1, 6 * D))

    saved = []
    for l in range(NL):
        ws = [w.astype(BF16) for w in _split_w_in(wts["w_in"][l])]
        wbr = [wts[k][l].astype(BF16) for k in ("w_br_a", "w_br_b", "w_br_c", "w_o", "w_gu", "w_down")]
        mv1, mv2 = _mod_rows(mods[l], 0), _mod_rows(mods[l], 3)
        g1, g2 = wts["norm1_g"][l][None, :], wts["norm2_g"][l][None, :]
        cw, scw = wts["dn_conv_w"][l], wts["sc_conv_w"][l]
        alr, dtr = _lane_row(wts["dn_a_log"][l]), _lane_row(wts["dn_dt_bias"][l])
        gdn = wts["dn_norm_g"][l][None, :]
        pwbd, ps = _block_diag(wts["pool_w"][l]), wts["pool_scale"][l][None, :]
        hb, pq, pz, pab, pp, sx, sb, sc_, pg = _inproj_fwd(X, mv1, g1, ws, tc, tt)
        qkv = _dnprep_fwd(pq, cw, tc, tt)
        parts = _dn_chunks_fwd(qkv, pab, alr, dtr)
        (of, ssf, vnf), (ob, ssb, vnb) = _scan_fwd(parts, X.shape[0], tc)
        yp = _pool_fwd(pp, pwbd, ps, tc)
        ys = _sc_fwd(sx, sb, sc_, scw, tc, tt)
        X1 = _mix_fwd(X, of, ob, pz, yp, ys, pg, mv1, gdn, *wbr[:4], tc, tt)
        X2, ff = _ffn_fwd(X1, mv2, g2, wbr[4], wbr[5], tc, tt)
        saved.append(dict(X=X, X1=X1, ff=ff, ws=ws, wbr=wbr, mv1=mv1, mv2=mv2, g1=g1, g2=g2, cw=cw, scw=scw, alr=alr, dtr=dtr,
                          gdn=gdn, pwbd=pwbd, ps=ps, hb=hb, pq=pq, pz=pz, pab=pab, pp=pp, sx=sx, sb=sb, sc=sc_, pg=pg,
                          qkv=qkv, of=of, ob=ob, ssf=ssf, ssb=ssb, vnf=vnf, vnb=vnb, parts=parts, yp=yp, ys=ys))
        X = X2

    dX, loss, dgf = _loss_head(X, tgt, wts["final_norm_g"][None, :], tc)

    gl = {k: [None] * NL for k in ("w_in", "norm1_g", "norm2_g", "dn_conv_w", "dn_a_log", "dn_dt_bias", "dn_norm_g",
                                   "pool_w", "pool_scale", "sc_conv_w", "w_br_a", "w_br_b", "w_br_c", "w_o", "w_gu",
                                   "w_down")}
    dmods = [None] * NL
    for l in reversed(range(NL)):
        s = saved[l]
        dx1, h2, dgu, act, dff, dg2, dm2 = _ffn_bwd(s["X1"], s["ff"], dX, s["mv2"], s["g2"], s["wbr"][4], s["wbr"][5], tc,
                                                    tt)
        gl["w_gu"][l] = _dw(h2, dgu, tt)
        gl["w_down"][l] = _dw(act, dff, tt)
        do, dz, dyp, dys, dpg, dwa, dwb, dwc, dwo, dgdn, dmg = _mix_bwd(
            dx1, s["of"], s["ob"], s["pz"], s["yp"], s["ys"], s["pg"], s["mv1"], s["gdn"], *s["wbr"][:4], tc, tt)
        dpp, dpw, dps = _pool_bwd(s["pp"], s["pwbd"], s["ps"], dyp, tc)
        dsx, dsb, dsc, dscw = _sc_bwd(s["sx"], s["sb"], s["sc"], s["scw"], dys, tc, tt)
        (dvnf, dssf), (dvnb, dssb) = _scan_bwd(do, s["parts"], tc)
        dqf, dqb, dpf, dpb, gacc = _dn_chunks_bwd(s["qkv"], s["pab"], s["alr"], s["dtr"], do,
                                                  (s["vnf"], dvnf, s["ssf"], dssf, s["parts"][0][6]),
                                                  (s["vnb"], dvnb, s["ssb"], dssb, s["parts"][1][6]))
        dy = _dnprep_bwd_act(s["pq"], s["cw"], dqf, dqb, tc, tt)
        dpq, dcw = _conv_bwd(dy, s["pq"], s["cw"], tc, tt)
        dps_ = [dpq, dz, dpf, dpb, dpp, dsx, dsb, dsc, dpg]
        dp_w = [0, 1, 2, 2, 3, 4, 5, 6, 7]
        dX, dg1, dm1 = _inproj_bwd(s["X"], s["mv1"], s["g1"], s["ws"], dps_, dp_w, dx1, tc, tt)
        dws = [_dw(s["hb"], dpq, tt), _dw(s["hb"], dz, tt), _dw(s["hb"], dpf + dpb, tt)[:, :16], _dw(s["hb"], dpp, tt),
               _dw(s["hb"], dsx, tt), _dw(s["hb"], dsb, tt), _dw(s["hb"], dsc, tt), _dw(s["hb"], dpg, tt)]
        gl["w_in"][l] = jnp.concatenate(dws, axis=1)
        gl["norm1_g"][l], gl["norm2_g"][l] = dg1[0], dg2[0]
        gl["dn_conv_w"][l], gl["sc_conv_w"][l] = dcw, dscw
        gl["dn_a_log"][l], gl["dn_dt_bias"][l] = gacc[0, :8].reshape(2, NH), gacc[1, :8].reshape(2, NH)
        gl["dn_norm_g"][l] = dgdn[0]
        gl["pool_w"][l] = jnp.stack([dpw[k * GW:(k + 1) * GW, k * GW:(k + 1) * GW] for k in range(4)])
        gl["pool_scale"][l] = dps[0]
        gl["w_br_a"][l], gl["w_br_b"][l], gl["w_br_c"][l], gl["w_o"][l] = dwa, dwb, dwc, dwo
        dm = dm1 + dmg
        row = lambda r: jnp.concatenate([dm[r], dm[r + 1], dm[r + 2], dm2[r], dm2[r + 1], dm2[r + 2]])
        dmods[l] = jnp.stack([row(0), row(3)] + [jnp.zeros((6 * D,), F32)] * 6)

    dwada, dbada, dcc = _mod_bwd(cc8, w_ada, jnp.stack(dmods))
    grads = {k: jnp.stack(v) for k, v in gl.items()}
    grads.update(w_ada=dwada, b_ada=dbada.reshape(NL, 6 * D), c_ctx=dcc[0], final_norm_g=dgf[0])
    return loss, dX[tc:], grads


MESH_ID = pl.DeviceIdType.MESH
HBM_SPEC = pl.BlockSpec(memory_space=pltpu.HBM)


def _me():
    return lax.axis_index("x"), lax.axis_index("y"), lax.axis_index("c")


def _dev_index(p):
    return 4 * p[0] + 2 * p[1] + p[2]


def _allgather(parts):
    n = len(parts)

    def body(*refs):
        ins, outs = refs[:n], refs[n:2 * n]
        send_sems, recv_sems, local_sems = refs[2 * n:]
        x, y, c = _me()
        me, sibling = (x, y, c), (x, y, 1 - c)
        chips = [(1 - x, y), (x, 1 - y), (1 - x, 1 - y)]

        def copy(a, k, block, to, src=None):
            dst = outs[a].at[_dev_index(block)]
            return pltpu.make_async_remote_copy(
                src_ref=dst if src is None else src, dst_ref=dst, send_sem=send_sems.at[a, k], recv_sem=recv_sems.at[a, k],
                device_id=to, device_id_type=MESH_ID)

        mine, first, passed = [], [], []
        for a in range(n):
            mine.append(pltpu.make_async_copy(ins[a], outs[a].at[_dev_index(me)], local_sems.at[a]))
            mine[-1].start()
            first.append(copy(a, 0, me, sibling, src=ins[a]))
            first += [copy(a, 1 + j, me, (*chip, c), src=ins[a]) for j, chip in enumerate(chips)]
        for cp in first:
            cp.start()
        for a in range(n):
            for j, chip in enumerate(chips):
                copy(a, 1 + j, (*chip, c), me).wait_recv()
                passed.append(copy(a, 4 + j, (*chip, c), sibling))
                passed[-1].start()
        for a in range(n):
            copy(a, 0, sibling, me).wait_recv()
            for j, chip in enumerate(chips):
                copy(a, 4 + j, (*chip, 1 - c), me).wait_recv()
        for cp in first + passed:
            cp.wait_send()
        for cp in mine:
            cp.wait()

    return pl.pallas_call(
        body, name="allgather", in_specs=[HBM_SPEC] * n, out_specs=[HBM_SPEC] * n,
        out_shape=[_S((N_DEV,) + p.shape, p.dtype) for p in parts],
        scratch_shapes=[pltpu.SemaphoreType.DMA((n, 7)), pltpu.SemaphoreType.DMA((n, 7)), pltpu.SemaphoreType.DMA((n,))],
    )(*parts)


def _broadcast_small(small):
    def body(in_ref, out_ref, send_sems, recv_sems, local_sem):
        x, y, c = _me()
        my = _dev_index((x, y, c))
        mine = pltpu.make_async_copy(in_ref, out_ref.at[my], local_sem)
        mine.start()
        remote = []
        for k in range(1, N_DEV):
            cp = pltpu.make_async_remote_copy(
                src_ref=in_ref, dst_ref=out_ref.at[my], send_sem=send_sems.at[k - 1], recv_sem=recv_sems.at[k - 1],
                device_id=(x ^ (k >> 2), y ^ ((k >> 1) & 1), c ^ (k & 1)), device_id_type=MESH_ID)
            cp.start()
            remote.append(cp)
        for cp in remote:
            cp.wait_recv()
        for cp in remote:
            cp.wait_send()
        mine.wait()

    return pl.pallas_call(
        body, name="small_exchange", in_specs=[HBM_SPEC], out_specs=HBM_SPEC,
        out_shape=_S((N_DEV,) + small.shape, small.dtype),
        scratch_shapes=[pltpu.SemaphoreType.DMA((7,)), pltpu.SemaphoreType.DMA((7,)), pltpu.SemaphoreType.DMA],
    )(small)


N_CHIP = 4


def _pair_exchange(g2s):
    n = len(g2s)

    def body(*refs):
        ins, outs = refs[:n], refs[n:2 * n]
        send_sems, recv_sems = refs[2 * n:]
        x, y, c = _me()
        cps = [pltpu.make_async_remote_copy(
            src_ref=ins[a].at[1 - c, j], dst_ref=outs[a].at[j], send_sem=send_sems.at[a, j], recv_sem=recv_sems.at[a, j],
            device_id=(x, y, 1 - c), device_id_type=MESH_ID) for a in range(n) for j in range(N_CHIP)]
        for cp in cps:
            cp.start()
        for cp in cps:
            cp.wait_recv()
        for cp in cps:
            cp.wait_send()

    return pl.pallas_call(
        body, name="pair_exchange", in_specs=[HBM_SPEC] * n, out_specs=[HBM_SPEC] * n,
        out_shape=[_S(g.shape[1:], g.dtype) for g in g2s],
        scratch_shapes=[pltpu.SemaphoreType.DMA((n, N_CHIP)), pltpu.SemaphoreType.DMA((n, N_CHIP))],
    )(*g2s)


def _chip_exchange(s4s):
    n = len(s4s)

    def body(*refs):
        ins, outs = refs[:n], refs[n:2 * n]
        send_sems, recv_sems, local_sems = refs[2 * n:]
        x, y, c = _me()
        my = 2 * x + y
        mine = [pltpu.make_async_copy(ins[a].at[my], outs[a].at[my], local_sems.at[a]) for a in range(n)]
        for cp in mine:
            cp.start()
        cps = []
        for k in range(1, N_CHIP):
            px, py = x ^ (k >> 1), y ^ (k & 1)
            for a in range(n):
                cps.append(pltpu.make_async_remote_copy(
                    src_ref=ins[a].at[2 * px + py], dst_ref=outs[a].at[my], send_sem=send_sems.at[a, k - 1],
                    recv_sem=recv_sems.at[a, k - 1], device_id=(px, py, c), device_id_type=MESH_ID))
                cps[-1].start()
        for cp in cps:
            cp.wait_recv()
        for cp in cps:
            cp.wait_send()
        for cp in mine:
            cp.wait()

    return pl.pallas_call(
        body, name="chip_exchange", in_specs=[HBM_SPEC] * n, out_specs=[HBM_SPEC] * n,
        out_shape=[_S(s.shape, s.dtype) for s in s4s],
        scratch_shapes=[pltpu.SemaphoreType.DMA((n, N_CHIP - 1)), pltpu.SemaphoreType.DMA((n, N_CHIP - 1)),
                        pltpu.SemaphoreType.DMA((n,))],
    )(*s4s)


def _shard_rows(r):
    return 256 if r % 256 == 0 else r


def _pair_sum(g2, got):
    _, nc, L, R, C = g2.shape
    tr = _shard_rows(R)

    def body(a_ref, b_ref, o_ref):
        o_ref[...] = (a_ref[0] + b_ref[...]).astype(BF16)

    blk = pl.BlockSpec((1, 1, tr, C), lambda j, l, i: (j, l, i, 0))
    return _call(
        body, "pair_sum", (nc, L, R // tr),
        [pl.BlockSpec((1, 1, 1, tr, C), lambda j, l, i: (lax.axis_index("c"), j, l, i, 0)), blk], blk,
        _S(got.shape, BF16))(g2, got)


def _adam(w, g, m, v):
    m2 = ADAM_B1 * m + (1.0 - ADAM_B1) * g
    v2 = ADAM_B2 * v + (1.0 - ADAM_B2) * (g * g)
    m_hat = m2 / (1.0 - ADAM_B1 ** ADAM_STEP)
    v_hat = v2 / (1.0 - ADAM_B2 ** ADAM_STEP)
    return -ADAM_LR * (m_hat / (jnp.sqrt(v_hat) + ADAM_EPS) + ADAM_WD * w), m2, v2


def _sum_adam(recv, w, m, v):
    L, R, C = w.shape
    tr = _shard_rows(R)

    def body(r_ref, w_ref, m_ref, v_ref, g_ref, d_ref, m2_ref, v2_ref):
        g = r_ref[0, 0].astype(F32)
        for j in range(1, N_CHIP):
            g = g + r_ref[j, 0].astype(F32)
        g_ref[0] = g
        d_ref[0], m2_ref[0], v2_ref[0] = _adam(w_ref[0], g, m_ref[0], v_ref[0])

    blk = pl.BlockSpec((1, tr, C), lambda l, i: (l, i, 0))
    return _call(
        body, "sum_adam", (L, R // tr),
        [pl.BlockSpec((N_CHIP, 1, tr, C), lambda l, i: (0, l, i, 0)), blk, blk, blk],
        [blk] * 4, [_S(w.shape)] * 4)(recv, w, m, v)


def _sum_small(recv):
    def body(r_ref, o_ref):
        g = r_ref[0]
        for k in range(1, N_DEV):
            g = g + r_ref[k]
        o_ref[...] = g

    return pl.pallas_call(body, name="sum_small", out_shape=_S(recv.shape[1:]))(recv)


def _adam_small(w, g, m, v):
    def body(w_ref, g_ref, m_ref, v_ref, d_ref, m2_ref, v2_ref):
        d_ref[...], m2_ref[...], v2_ref[...] = _adam(w_ref[...], g_ref[...], m_ref[...], v_ref[...])

    return pl.pallas_call(body, name="adam_small", out_shape=[_S(w.shape)] * 3)(w, g, m, v)


def _pack(arrs, dtype, row_mult):
    parts, offs, r = [], [], 0
    for a in arrs:
        nr = -(-a.size // LANES)
        parts.append(jnp.pad(a.reshape(-1).astype(dtype), (0, nr * LANES - a.size)))
        offs.append(r)
        r += nr
    pad = (-r) % row_mult
    if pad:
        parts.append(jnp.zeros((pad * LANES,), dtype))
    return jnp.concatenate(parts).reshape(r + pad, LANES), offs


def _unpack(packed, offs, shapes, lead=()):
    out = []
    for off, shp in zip(offs, shapes):
        size = int(np.prod(shp))
        nr = -(-size // LANES)
        flat = packed[..., off:off + nr, :].reshape(lead + (nr * LANES,))
        out.append(flat[..., :size].reshape(lead + tuple(shp)))
    return out


BIG = (("w_ada", 2), ("w_in", 2), ("w_br_a", 2), ("w_br_b", 2), ("w_br_c", 2), ("w_o", 1), ("w_gu", 2), ("w_down", 1))
CONV = ("dn_conv_w", "sc_conv_w")
REPL = ("c_ctx", "b_ada", "norm1_g", "norm2_g", "dn_a_log", "dn_dt_bias", "dn_norm_g", "pool_w", "pool_scale",
        "final_norm_g")
WEIGHTS = ("c_ctx", "w_ada", "b_ada", "norm1_g", "norm2_g", "w_in", "dn_conv_w", "dn_a_log", "dn_dt_bias", "dn_norm_g",
           "pool_w", "pool_scale", "sc_conv_w", "w_br_a", "w_br_b", "w_br_c", "w_o", "w_gu", "w_down", "final_norm_g")
TOKEN_TILE = 256


def _join(blocks, axis):
    nd, nl, r, c = blocks.shape
    if axis == 2:
        return blocks.transpose(1, 2, 0, 3).reshape(nl, r, nd * c)
    return blocks.transpose(1, 0, 2, 3).reshape(nl, nd * r, c)


def _split(full, axis):
    nl, r, c = full.shape
    if axis == 2:
        return full.reshape(nl, r, N_CHIP, 2, c // N_DEV).transpose(3, 2, 0, 1, 4)
    return full.reshape(nl, N_CHIP, 2, r // N_DEV, c).transpose(2, 1, 0, 3, 4)


def kernel(x, c, ctx, c_ctx, w_ada, b_ada, norm1_g, norm2_g, w_in, dn_conv_w, dn_a_log, dn_dt_bias, dn_norm_g, pool_w, pool_scale, sc_conv_w, w_br_a, w_br_b, w_br_c, w_o, w_gu, w_down, final_norm_g, loss_target, m_c_ctx, m_w_ada, m_b_ada, m_norm1_g, m_norm2_g, m_w_in, m_dn_conv_w, m_dn_a_log, m_dn_dt_bias, m_dn_norm_g, m_pool_w, m_pool_scale, m_sc_conv_w, m_w_br_a, m_w_br_b, m_w_br_c, m_w_o, m_w_gu, m_w_down, m_final_norm_g, v_c_ctx, v_w_ada, v_b_ada, v_norm1_g, v_norm2_g, v_w_in, v_dn_conv_w, v_dn_a_log, v_dn_dt_bias, v_dn_norm_g, v_pool_w, v_pool_scale, v_sc_conv_w, v_w_br_a, v_w_br_b, v_w_br_c, v_w_o, v_w_gu, v_w_down, v_final_norm_g):
    loc = dict(c_ctx=c_ctx, w_ada=w_ada, b_ada=b_ada, norm1_g=norm1_g, norm2_g=norm2_g, w_in=w_in, dn_conv_w=dn_conv_w,
               dn_a_log=dn_a_log, dn_dt_bias=dn_dt_bias, dn_norm_g=dn_norm_g, pool_w=pool_w, pool_scale=pool_scale,
               sc_conv_w=sc_conv_w, w_br_a=w_br_a, w_br_b=w_br_b, w_br_c=w_br_c, w_o=w_o, w_gu=w_gu, w_down=w_down,
               final_norm_g=final_norm_g)
    mom_m = dict(c_ctx=m_c_ctx, w_ada=m_w_ada, b_ada=m_b_ada, norm1_g=m_norm1_g, norm2_g=m_norm2_g, w_in=m_w_in,
                 dn_conv_w=m_dn_conv_w, dn_a_log=m_dn_a_log, dn_dt_bias=m_dn_dt_bias, dn_norm_g=m_dn_norm_g,
                 pool_w=m_pool_w, pool_scale=m_pool_scale, sc_conv_w=m_sc_conv_w, w_br_a=m_w_br_a, w_br_b=m_w_br_b,
                 w_br_c=m_w_br_c, w_o=m_w_o, w_gu=m_w_gu, w_down=m_w_down, final_norm_g=m_final_norm_g)
    mom_v = dict(c_ctx=v_c_ctx, w_ada=v_w_ada, b_ada=v_b_ada, norm1_g=v_norm1_g, norm2_g=v_norm2_g, w_in=v_w_in,
                 dn_conv_w=v_dn_conv_w, dn_a_log=v_dn_a_log, dn_dt_bias=v_dn_dt_bias, dn_norm_g=v_dn_norm_g,
                 pool_w=v_pool_w, pool_scale=v_pool_scale, sc_conv_w=v_sc_conv_w, w_br_a=v_w_br_a, w_br_b=v_w_br_b,
                 w_br_c=v_w_br_c, w_o=v_w_o, w_gu=v_w_gu, w_down=v_w_down, final_norm_g=v_final_norm_g)
    my = _dev_index(_me())

    big_pack, big_offs = _pack([loc[k] for k, _ in BIG], BF16, BF16_ROWS)
    conv_pack, conv_offs = _pack([loc[k] for k in CONV], F32, 8)
    big_all, conv_all = _allgather([big_pack, conv_pack])
    full = {k: loc[k] for k in REPL}
    for (k, axis), blocks in zip(BIG, _unpack(big_all, big_offs, [loc[k].shape for k, _ in BIG], (N_DEV,))):
        full[k] = _join(blocks, axis)
    for k, blocks in zip(CONV, _unpack(conv_all, conv_offs, [loc[k].shape for k in CONV], (N_DEV,))):
        full[k] = _join(blocks, 2)

    loss8, grad_x, g = _device_step(x[0], c, ctx[0], loss_target[0], full, TOKEN_TILE)

    g2s = [_split(g[k], axis) for k, axis in BIG]
    recv_big = _chip_exchange([_pair_sum(g2, got) for g2, got in zip(g2s, _pair_exchange(g2s))])

    small_names = REPL + CONV
    small_pack, small_offs = _pack([g[k] for k in small_names] + [loss8[0:1, 0:1]], F32, 8)
    small_sum = _sum_small(_broadcast_small(small_pack))
    sums = _unpack(small_sum, small_offs, [g[k].shape for k in small_names] + [(1, 1)])
    grads = dict(zip(small_names, sums[:-1]))
    loss = sums[-1][0, 0]
    for k in CONV:
        w = loc[k].shape[2]
        grads[k] = lax.dynamic_slice_in_dim(grads[k], my * w, w, axis=2)

    delta, new_m, new_v = {}, {}, {}
    for (k, _), recv in zip(BIG, recv_big):
        grads[k], delta[k], new_m[k], new_v[k] = _sum_adam(recv, loc[k], mom_m[k], mom_v[k])
    packs = [_pack([src[k] for k in small_names], F32, 8)[0] for src in (loc, grads, mom_m, mom_v)]
    _, offs = _pack([loc[k] for k in small_names], F32, 8)
    shapes = [loc[k].shape for k in small_names]
    for dst, packed in zip((delta, new_m, new_v), _adam_small(*packs)):
        dst.update(zip(small_names, _unpack(packed, offs, shapes)))

    return (loss, grad_x[None], *[grads[k] for k in WEIGHTS], *[delta[k] for k in WEIGHTS],
            *[new_m[k] for k in WEIGHTS], *[new_v[k] for k in WEIGHTS])
```

```python
import functools

import numpy as np
import jax
import jax.numpy as jnp
from jax import lax
from jax.experimental import pallas as pl
from jax.experimental.pallas import tpu as pltpu

F32 = jnp.float32
BF16 = jnp.bfloat16
HI = lax.Precision.HIGHEST

D = 1024
NL = 2
NH = 4
DH = 128
DN = NH * DH
CH = 64
GW = 64
PW = 256
DFF = 2816
EPS = 1e-6
N_DEV = 8
N_CHIP = 4
MESH_ID = pl.DeviceIdType.MESH
HBM_SPEC = pl.BlockSpec(memory_space=pltpu.HBM)
LANES = 128
BF16_ROWS = 16
VMEM_MB = 56

ADAM_LR, ADAM_B1, ADAM_B2, ADAM_EPS, ADAM_WD, ADAM_STEP = 0.001, 0.9, 0.999, 1e-08, 0.01, 10

IN_BOUNDS = (0, 1536, 2048, 2064, 2320, 2576, 2832, 3088, 6160)
IN_WIDTHS = (1536, 512, 128, 256, 256, 256, 256, 3072)
POOL_WIN = ((1, 0), (2, 1), (4, 3), (8, 7))

NN = ((1,), (0,))
NT = ((1,), (1,))
TN = ((0,), (0,))


def _dot(a, b, dims, hi=False):
    if hi:
        prec = lax.Precision.HIGH if hi == "x3" else HI
        return lax.dot_general(a, b, (dims, ((), ())), precision=prec, preferred_element_type=F32)
    return lax.dot_general(a.astype(BF16), b.astype(BF16), (dims, ((), ())), preferred_element_type=F32)


def _S(shape, dtype=F32):
    return jax.ShapeDtypeStruct(tuple(shape), dtype)


def _full(shape):
    nd = len(shape)
    return pl.BlockSpec(tuple(shape), lambda *_: (0,) * nd)


def _rows(tt, w):
    return pl.BlockSpec((tt, w), lambda i: (i, 0))


class _Rider:
    def __init__(self, ins, out_shapes, sems, copies):
        self.ins, self.out_shapes, self.sems, self.copies = list(ins), list(out_shapes), list(sems), copies

    def start(self, ins, outs, sems):
        remote, local = self.copies(ins, outs, sems)
        for cp in local + remote:
            cp.start()

    def wait(self, ins, outs, sems):
        remote, local = self.copies(ins, outs, sems)
        for cp in remote:
            cp.wait_recv()
        for cp in remote:
            cp.wait_send()
        for cp in local:
            cp.wait()


def _call(body, name, grid, in_specs, out_specs, out_shape, scratch=(), rider=None):
    params = pltpu.CompilerParams(dimension_semantics=("arbitrary",) * len(grid), vmem_limit_bytes=VMEM_MB << 20)
    if rider is None:
        return pl.pallas_call(body, name=name, grid=grid, in_specs=in_specs, out_specs=out_specs, out_shape=out_shape,
                              scratch_shapes=list(scratch), compiler_params=params)
    single = not isinstance(out_shape, (list, tuple))
    out_specs, out_shape = ([out_specs], [out_shape]) if single else (list(out_specs), list(out_shape))
    n_in, n_out, n_scr = len(in_specs), len(out_shape), len(scratch)
    r_in, r_out = len(rider.ins), len(rider.out_shapes)

    def hosted(*refs):
        ins, refs = refs[:n_in + r_in], refs[n_in + r_in:]
        outs, scr = refs[:n_out + r_out], refs[n_out + r_out:]
        riding = (ins[n_in:], outs[n_out:], scr[n_scr:])

        @pl.when(pl.program_id(0) == 0)
        def _():
            rider.start(*riding)

        body(*ins[:n_in], *outs[:n_out], *scr[:n_scr])

        @pl.when(pl.program_id(0) == grid[0] - 1)
        def _():
            rider.wait(*riding)

    call = pl.pallas_call(
        hosted, name=name, grid=grid, in_specs=list(in_specs) + [HBM_SPEC] * r_in,
        out_specs=out_specs + [HBM_SPEC] * r_out, out_shape=out_shape + rider.out_shapes,
        scratch_shapes=list(scratch) + rider.sems, compiler_params=params)

    def run(*args):
        res = call(*args, *rider.ins)
        own = res[:n_out]
        return (own[0] if single else own), list(res[n_out:])

    return run


def _iota(shape, axis):
    return lax.broadcasted_iota(jnp.int32, shape, axis)


def _colsum(a):
    return jnp.sum(a, axis=0, keepdims=True)


def _silu(x):
    return x * jax.nn.sigmoid(x)


def _modulate(x, g, sh, sc):
    xn = x * lax.rsqrt(jnp.mean(x * x, axis=-1, keepdims=True) + EPS)
    return (xn * g) * (1.0 + sc) + sh


def _stream_rows(mv_ref, i, tt, tc, k):
    isc = (i * tt + _iota((tt, 1), 0)) < tc
    return isc, jnp.where(isc, mv_ref[k:k + 1, :], mv_ref[3 + k:4 + k, :])


def _acc_stream(ref, k, isc, val):
    ref[k:k + 1, :] += _colsum(jnp.where(isc, val, 0.0))
    ref[3 + k:4 + k, :] += _colsum(jnp.where(isc, 0.0, val))


MOD_CT = 1536


def _mod_fwd(cc8, w_ada, b_ada3):
    def body(cc_ref, w_ref, b_ref, o_ref):
        o_ref[0] = _dot(_silu(cc_ref[...]), w_ref[0], NN) + b_ref[0]

    return _call(
        body, "mod_fwd", (NL, 6 * D // MOD_CT),
        [pl.BlockSpec((8, D), lambda l, j: (0, 0)), pl.BlockSpec((1, D, MOD_CT), lambda l, j: (l, 0, j)),
         pl.BlockSpec((1, 1, MOD_CT), lambda l, j: (l, 0, j))],
        pl.BlockSpec((1, 8, MOD_CT), lambda l, j: (l, 0, j)), _S((NL, 8, 6 * D)))(cc8, w_ada, b_ada3)


def _mod_bwd(cc8, w_ada, dmods):
    def body(cc_ref, w_ref, dm_ref, dw_ref, db_ref, dcc_ref):
        first = (pl.program_id(0) == 0) & (pl.program_id(1) == 0)
        cc = cc_ref[...]
        sg = jax.nn.sigmoid(cc)
        dm = dm_ref[0]
        dw_ref[0] = _dot(cc * sg, dm, TN)
        db_ref[0] = dm[0:1, :] + dm[1:2, :]

        @pl.when(first)
        def _():
            dcc_ref[...] = jnp.zeros_like(dcc_ref)

        dcc_ref[...] += _dot(dm, w_ref[0], NT) * (sg * (1.0 + cc * (1.0 - sg)))

    return _call(
        body, "mod_bwd", (NL, 6 * D // MOD_CT),
        [pl.BlockSpec((8, D), lambda l, j: (0, 0)), pl.BlockSpec((1, D, MOD_CT), lambda l, j: (l, 0, j)),
         pl.BlockSpec((1, 8, MOD_CT), lambda l, j: (l, 0, j))],
        [pl.BlockSpec((1, D, MOD_CT), lambda l, j: (l, 0, j)), pl.BlockSpec((1, 1, MOD_CT), lambda l, j: (l, 0, j)),
         pl.BlockSpec((8, D), lambda l, j: (0, 0))],
        [_S((NL, D, 6 * D)), _S((NL, 1, 6 * D)), _S((8, D))])(cc8, w_ada, dmods)


def _inproj_fwd(X, mv, g, ws, tc, tt):
    T = X.shape[0]
    nw = len(ws)

    def body(x_ref, mv_ref, g_ref, *refs):
        w_refs, h_ref, p_refs = refs[:nw], refs[nw], refs[nw + 1:]
        i = pl.program_id(0)
        _, sh = _stream_rows(mv_ref, i, tt, tc, 0)
        _, sc = _stream_rows(mv_ref, i, tt, tc, 1)
        hb = _modulate(x_ref[...], g_ref[...], sh, sc).astype(BF16)
        h_ref[...] = hb
        for w_ref, p_ref in zip(w_refs, p_refs):
            p_ref[...] = jnp.dot(hb, w_ref[...], preferred_element_type=F32)

    return _call(
        body, "inproj_fwd", (T // tt,),
        [_rows(tt, D), _full((8, D)), _full((1, D))] + [_full(w.shape) for w in ws],
        [_rows(tt, D)] + [_rows(tt, w.shape[1]) for w in ws],
        [_S((T, D), BF16)] + [_S((T, w.shape[1])) for w in ws])(X, mv, g, *ws)


def _inproj_bwd(X, mv, g, ws, dps, dp_w, dres, tc, tt):
    T = X.shape[0]
    nw, nd = len(ws), len(dps)

    def body(x_ref, mv_ref, g_ref, dres_ref, *refs):
        w_refs, dp_refs = refs[:nw], refs[nw:nw + nd]
        dx_ref, dg_ref, dm_ref = refs[nw + nd:]
        i = pl.program_id(0)
        isc, sh = _stream_rows(mv_ref, i, tt, tc, 0)
        _, sc = _stream_rows(mv_ref, i, tt, tc, 1)
        dh = None
        for dp_ref, k in zip(dp_refs, dp_w):
            t = _dot(dp_ref[...], w_refs[k][...], NT)
            dh = t if dh is None else dh + t
        _, vjp = jax.vjp(_modulate, x_ref[...], g_ref[...], sh, sc)
        dx, dg, dsh, dsc = vjp(dh)
        dx_ref[...] = dres_ref[...] + dx

        @pl.when(i == 0)
        def _():
            dg_ref[...] = jnp.zeros_like(dg_ref)
            dm_ref[...] = jnp.zeros_like(dm_ref)

        dg_ref[...] += dg
        _acc_stream(dm_ref, 0, isc, dsh)
        _acc_stream(dm_ref, 1, isc, dsc)

    return _call(
        body, "inproj_bwd", (T // tt,),
        [_rows(tt, D), _full((8, D)), _full((1, D)), _rows(tt, D)] + [_full(w.shape) for w in ws]
        + [_rows(tt, dp.shape[1]) for dp in dps],
        [_rows(tt, D), _full((1, D)), _full((8, D))],
        [_S((T, D)), _S((1, D)), _S((8, D))])(X, mv, g, dres, *ws, *dps)


def _dw(A, B, tt):
    T, K = A.shape
    N = B.shape[1]
    tt = 3 * tt if T % (3 * tt) == 0 else tt
    tn = next(t for t in (1024, 512, 256, LANES) if N % t == 0)

    def body(a_ref, b_ref, o_ref):
        @pl.when(pl.program_id(1) == 0)
        def _():
            o_ref[...] = jnp.zeros_like(o_ref)

        o_ref[...] += _dot(a_ref[...], b_ref[...], TN)

    return _call(
        body, "dw", (N // tn, T // tt),
        [pl.BlockSpec((tt, K), lambda j, i: (i, 0)), pl.BlockSpec((tt, tn), lambda j, i: (i, j))],
        pl.BlockSpec((K, tn), lambda j, i: (0, j)), _S((K, N)))(A, B)


def _halo_specs(T, tt, cw, col):
    r8, nb8 = tt // 8, T // 8
    return [pl.BlockSpec((tt, cw), lambda j, i: (i, col(j))),
            pl.BlockSpec((8, cw), lambda j, i: (jnp.maximum(i * r8 - 1, 0), col(j))),
            pl.BlockSpec((8, cw), lambda j, i: (jnp.minimum((i + 1) * r8, nb8 - 1), col(j)))]


def _shifts(a, prev8, next8, i, tt, tc, T):
    r = _iota((tt, 1), 0)
    t = i * tt + r
    dn = jnp.where(r == 0, prev8[7:8, :], pltpu.roll(a, 1, 0))
    dn = jnp.where((t == 0) | (t == tc), 0.0, dn)
    up = jnp.where(r == tt - 1, next8[0:1, :], pltpu.roll(a, tt - 1, 0))
    up = jnp.where((t == T - 1) | (t == tc - 1), 0.0, up)
    return dn, up


def _dn_post(y, part):
    a = _silu(y)
    nrm = lax.rsqrt(jnp.sum(a * a, axis=-1, keepdims=True) + EPS)
    f = jnp.where(part == 0, nrm * (DH ** -0.5), jnp.where(part == 1, nrm, 1.0))
    return a * f


def _conv3(w_ref, dn, mid, up):
    return w_ref[0:1, :] * dn + w_ref[1:2, :] * mid + w_ref[2:3, :] * up


def _dnprep_fwd(pq, cw, tc, tt):
    T = pq.shape[0]

    def body(p_ref, pp_ref, pn_ref, w_ref, a_ref):
        part, i = pl.program_id(0), pl.program_id(1)
        p = p_ref[...]
        dn, up = _shifts(p, pp_ref[...], pn_ref[...], i, tt, tc, T)
        y = _conv3(w_ref, dn, p, up)
        for h in range(NH):
            a_ref[:, _hs(h)] = _dn_post(y[:, _hs(h)], part)

    return _call(
        body, "dnprep_fwd", (3, T // tt),
        _halo_specs(T, tt, DN, lambda j: j) + [pl.BlockSpec((3, DN), lambda j, i: (0, j))],
        pl.BlockSpec((tt, DN), lambda j, i: (i, j)), _S((T, 3 * DN)))(pq, pq, pq, cw)


def _dnprep_bwd_act(pq, cw, da_f, da_b, tc, tt):
    T = pq.shape[0]

    def body(p_ref, pp_ref, pn_ref, w_ref, df_ref, db_ref, dy_ref):
        part, i = pl.program_id(0), pl.program_id(1)
        p = p_ref[...]
        dn, up = _shifts(p, pp_ref[...], pn_ref[...], i, tt, tc, T)
        y = _conv3(w_ref, dn, p, up)
        for h in range(NH):
            _, vjp = jax.vjp(lambda yh: _dn_post(yh, part), y[:, _hs(h)])
            dy_ref[:, _hs(h)] = vjp(df_ref[:, _hs(h)] + db_ref[:, _hs(h)])[0]

    blk = pl.BlockSpec((tt, DN), lambda j, i: (i, j))
    return _call(
        body, "dnprep_bwd_act", (3, T // tt),
        _halo_specs(T, tt, DN, lambda j: j) + [pl.BlockSpec((3, DN), lambda j, i: (0, j)), blk, blk],
        blk, _S((T, 3 * DN)))(pq, pq, pq, cw, da_f, da_b)


def _conv_bwd(dy, p, cw, tc, tt):
    T, W = p.shape
    cb = DN

    def body(dy_ref, dyp_ref, dyn_ref, p_ref, pp_ref, pn_ref, w_ref, dp_ref, dw_ref):
        i = pl.program_id(1)
        dy, p_ = dy_ref[...], p_ref[...]
        ddn, dup = _shifts(dy, dyp_ref[...], dyn_ref[...], i, tt, tc, T)
        dp_ref[...] = _conv3(w_ref, dup, dy, ddn)
        pdn, pup = _shifts(p_, pp_ref[...], pn_ref[...], i, tt, tc, T)

        @pl.when(i == 0)
        def _():
            dw_ref[...] = jnp.zeros_like(dw_ref)

        dw_ref[0:1, :] += _colsum(dy * pdn)
        dw_ref[1:2, :] += _colsum(dy * p_)
        dw_ref[2:3, :] += _colsum(dy * pup)

    wspec = pl.BlockSpec((3, cb), lambda j, i: (0, j))
    return _call(
        body, "conv_bwd", (W // cb, T // tt),
        _halo_specs(T, tt, cb, lambda j: j) * 2 + [wspec],
        [pl.BlockSpec((tt, cb), lambda j, i: (i, j)), wspec], [_S((T, W)), _S((3, W))])(dy, dy, dy, p, p, p, cw)


def _sc_fwd(sx, sb, sc_, cw, tc, tt):
    T = sx.shape[0]

    def body(x_ref, xp_ref, xn_ref, c_ref, cp_ref, cn_ref, b_ref, w_ref, y_ref):
        i = pl.program_id(1)
        u = c_ref[...] * x_ref[...]
        dn, up = _shifts(u, cp_ref[...] * xp_ref[...], cn_ref[...] * xn_ref[...], i, tt, tc, T)
        y_ref[...] = b_ref[...] * _conv3(w_ref, dn, u, up)

    blk = pl.BlockSpec((tt, LANES), lambda j, i: (i, j))
    return _call(
        body, "sc_fwd", (PW // LANES, T // tt),
        _halo_specs(T, tt, LANES, lambda j: j) * 2 + [blk, pl.BlockSpec((3, LANES), lambda j, i: (0, j))],
        blk, _S((T, PW)))(sx, sx, sx, sc_, sc_, sc_, sb, cw)


def _sc_bwd(sx, sb, sc_, cw, dy, tc, tt):
    T = sx.shape[0]

    def body(x_ref, xp_ref, xn_ref, c_ref, cp_ref, cn_ref, b_ref, bp_ref, bn_ref, dy_ref, dyp_ref, dyn_ref, w_ref,
             dx_ref, db_ref, dc_ref, dw_ref):
        i = pl.program_id(1)
        x, c, dy_ = x_ref[...], c_ref[...], dy_ref[...]
        u = c * x
        udn, uup = _shifts(u, cp_ref[...] * xp_ref[...], cn_ref[...] * xn_ref[...], i, tt, tc, T)
        db_ref[...] = dy_ * _conv3(w_ref, udn, u, uup)
        e = dy_ * b_ref[...]
        edn, eup = _shifts(e, dyp_ref[...] * bp_ref[...], dyn_ref[...] * bn_ref[...], i, tt, tc, T)
        du = _conv3(w_ref, eup, e, edn)
        dx_ref[...] = du * c
        dc_ref[...] = du * x

        @pl.when(i == 0)
        def _():
            dw_ref[...] = jnp.zeros_like(dw_ref)

        dw_ref[0:1, :] += _colsum(e * udn)
        dw_ref[1:2, :] += _colsum(e * u)
        dw_ref[2:3, :] += _colsum(e * uup)

    blk = pl.BlockSpec((tt, LANES), lambda j, i: (i, j))
    wspec = pl.BlockSpec((3, LANES), lambda j, i: (0, j))
    return _call(
        body, "sc_bwd", (PW // LANES, T // tt),
        _halo_specs(T, tt, LANES, lambda j: j) * 4 + [wspec],
        [blk, blk, blk, wspec], [_S((T, PW))] * 3 + [_S((3, PW))])(
            sx, sx, sx, sc_, sc_, sc_, sb, sb, sb, dy, dy, dy, cw)


def _group_select(vals):
    g = _iota((1, PW), 1) // (PW // len(POOL_WIN))
    return jnp.where(g == 0, vals[0], jnp.where(g == 1, vals[1], jnp.where(g == 2, vals[2], vals[3])))


def _nested_box(get, mirror):
    acc, outs, pl_, ph_ = get(0), [], 0, 0
    for lo, hi in POOL_WIN:
        if mirror:
            lo, hi = hi, lo
        for k in range(pl_ + 1, lo + 1):
            acc = acc + get(-k)
        for k in range(ph_ + 1, hi + 1):
            acc = acc + get(k)
        pl_, ph_ = lo, hi
        outs.append(acc)
    return _group_select(outs)


def _box_tokens(a, n, mirror):
    idx = _iota((n, 1), 0)

    def get(k):
        if k == 0:
            return a
        return jnp.where((idx + k >= 0) & (idx + k < n), pltpu.roll(a, (-k) % n, 0), 0.0)

    return _nested_box(get, mirror)


def _inv_count(pos, n):
    return _group_select([1.0 / (jnp.minimum(pos + hi, n - 1) - jnp.maximum(pos - lo, 0) + 1).astype(F32)
                          for lo, hi in POOL_WIN])


def _pool_rows(ref, r, R, tc, mirror):
    def get(k):
        rr = r + k
        rc = jnp.clip(rr, 0, R - 1)
        v = ref[pl.ds(pl.multiple_of(tc + rc * GW, GW), GW), :]
        if mirror:
            v = v * _inv_count(jnp.full((1, PW), rc, jnp.int32), R)
        return jnp.where((rr >= 0) & (rr < R), v, 0.0)

    return _nested_box(get, mirror)


def _pool_fwd(u, pwbd, ps, tc):
    T = u.shape[0]
    R = (T - tc) // GW

    def body(u_ref, pw_ref, ps_ref, y_ref):
        pw, scale = pw_ref[...], ps_ref[...]
        uc = u_ref[0:tc, :]
        mc = _box_tokens(uc, tc, False) * _inv_count(_iota((tc, 1), 0), tc)
        y_ref[0:tc, :] = _dot(mc - uc, pw, NN) * scale
        inv_c = _inv_count(_iota((GW, 1), 0), GW)

        def row(r, carry):
            rs = _pool_rows(u_ref, r, R, tc, False) * _inv_count(jnp.full((1, PW), r, jnp.int32), R)
            m = _box_tokens(rs, GW, False) * inv_c
            sl = pl.ds(pl.multiple_of(tc + r * GW, GW), GW)
            y_ref[sl, :] = _dot(m - u_ref[sl, :], pw, NN) * scale
            return carry

        lax.fori_loop(0, R, row, 0)

    return pl.pallas_call(
        body, name="pool_fwd", out_shape=_S((T, PW)),
        compiler_params=pltpu.CompilerParams(vmem_limit_bytes=VMEM_MB << 20))(u, pwbd, ps)


def _pool_bwd(u, pwbd, ps, dy, tc):
    T = u.shape[0]
    R = (T - tc) // GW

    def body(u_ref, pw_ref, ps_ref, dy_ref, du_ref, dpw_ref, dps_ref, dd_ref):
        pw, scale = pw_ref[...], ps_ref[...]
        dpw_ref[...] = jnp.zeros_like(dpw_ref)
        dps_ref[...] = jnp.zeros_like(dps_ref)

        def back(d, dy_):
            dz = dy_ * scale
            dpw_ref[...] += _dot(d, dz, TN)
            dps_ref[...] += _colsum(dy_ * _dot(d, pw, NN))
            return _dot(dz, pw, NT)

        uc = u_ref[0:tc, :]
        inv_cc = _inv_count(_iota((tc, 1), 0), tc)
        ddc = back(_box_tokens(uc, tc, False) * inv_cc - uc, dy_ref[0:tc, :])
        du_ref[0:tc, :] = _box_tokens(ddc * inv_cc, tc, True) - ddc
        inv_c = _inv_count(_iota((GW, 1), 0), GW)

        def row1(r, carry):
            rs = _pool_rows(u_ref, r, R, tc, False) * _inv_count(jnp.full((1, PW), r, jnp.int32), R)
            m = _box_tokens(rs, GW, False) * inv_c
            sl = pl.ds(pl.multiple_of(tc + r * GW, GW), GW)
            dd_ref[sl, :] = back(m - u_ref[sl, :], dy_ref[sl, :])
            return carry

        lax.fori_loop(0, R, row1, 0)

        def row2(r, carry):
            t1 = _pool_rows(dd_ref, r, R, tc, True)
            sl = pl.ds(pl.multiple_of(tc + r * GW, GW), GW)
            du_ref[sl, :] = _box_tokens(t1 * inv_c, GW, True) - dd_ref[sl, :]
            return carry

        lax.fori_loop(0, R, row2, 0)

    return pl.pallas_call(
        body, name="pool_bwd", out_shape=[_S((T, PW)), _S((PW, PW)), _S((1, PW))],
        scratch_shapes=[pltpu.VMEM((T, PW), F32)],
        compiler_params=pltpu.CompilerParams(vmem_limit_bytes=VMEM_MB << 20))(u, pwbd, ps, dy)


def _scan_consts():
    i = np.arange(CH)
    lower = (i[:, None] >= i[None, :]).astype(np.float32)
    return jnp.asarray(np.stack([lower, lower.T])), jnp.asarray(np.stack([lower.T, lower]))


def _gates(pab, al, dtb, csum):
    sp_in = pab + dtb
    sp = jnp.maximum(sp_in, 0.0) + jnp.log(1.0 + jnp.exp(-jnp.abs(sp_in)))
    nexp = -jnp.exp(al)
    gm = nexp * sp
    return gm, jax.nn.sigmoid(pab), _dot(csum, gm, NN, hi=True), sp_in, nexp


def _lane_col(m, j):
    return jnp.sum(jnp.where(_iota(m.shape, 1) == j, m, 0.0), axis=1, keepdims=True)


def _hs(h):
    return slice(h * DH, (h + 1) * DH)


HS = NH * CH
X3 = "x3"


def _stack(x, base=0):
    return jnp.concatenate([x[:, base + h * DH:base + (h + 1) * DH] for h in range(NH)], axis=0)


def _heads(st):
    return [st[h * CH:(h + 1) * CH] for h in range(NH)]


def _rowsum(a):
    return jnp.sum(a, axis=1, keepdims=True)


def _row_of(col):
    e0 = (_iota((8, LANES), 1) == 0).astype(F32)
    return _dot(e0, jnp.broadcast_to(col, (HS, LANES)), NT, hi=True)[0:1, :]


def _inverses(nms):
    eye = (_iota((HS, HS), 0) == _iota((HS, HS), 1)).astype(F32)
    x0s, mps = [eye + nm for nm in nms], list(nms)
    for _ in range(5):
        mps = [_dot(mp, mp, NN) for mp in mps]
        x0s = [x0 + _dot(x0, mp, NN) for x0, mp in zip(x0s, mps)]
    rs = [eye - _dot(eye - nm, x0, NN, hi=X3) for nm, x0 in zip(nms, x0s)]
    return [x0 + _dot(x0, r, NN) for x0, r in zip(x0s, rs)]


def _dn_chunk_pre(qkv, pab, al, dtb, csum_d, d):
    gm, bm, gcm, sp_in, nexp = _gates(pab, al, dtb, csum_d)
    gc = jnp.concatenate([_lane_col(gcm, d * NH + h) for h in range(NH)], axis=0)
    beta = jnp.concatenate([_lane_col(bm, 8 + d * NH + h) for h in range(NH)], axis=0)
    q, k, v = _stack(qkv, 0), _stack(qkv, DN), _stack(qkv, 2 * DN)
    ii, jj = _iota((HS, HS), 0), _iota((HS, HS), 1)
    sh = CH.bit_length() - 1
    same = (ii >> sh) == (jj >> sh)
    incl = same & ((ii >= jj) if d == 0 else (ii <= jj))
    strict = same & ((ii > jj) if d == 0 else (ii < jj))
    Di = jnp.where(incl, jnp.exp(jnp.where(incl, gc - _row_of(gc), 0.0)), 0.0)
    Ds = jnp.where(strict, Di, 0.0)
    kb = k * beta
    kk = _dot(kb, k, NT)
    return dict(q=q, k=k, v=v, beta=beta, gc=gc, gm=gm, bm=bm, sp_in=sp_in, nexp=nexp, Di=Di, Ds=Ds, strict=strict,
                last=CH - 1 if d == 0 else 0, kb=kb, kk=kk)


def _dn_chunk_post(c, tm):
    q, k, v, beta, gc, kb, last = (c[n] for n in ("q", "k", "v", "beta", "gc", "kb", "last"))
    E = jnp.exp(gc)
    gls = [gc[h * CH + last:h * CH + last + 1, :] for h in range(NH)]
    xs = jnp.exp(jnp.concatenate([jnp.broadcast_to(g, (CH, 1)) for g in gls], axis=0) - gc)
    qk = _dot(q, k, NT)
    return dict(c, tm=tm, E=E, gls=gls, xs=xs, qk=qk, u=_dot(tm, v * beta, NN, hi=X3), w=_dot(tm, kb * E, NN, hi=X3),
                ks=k * xs, qd=q * E, aqk=qk * c["Di"])


def _dn_chunks_bwd_math(cs, Ss, dS2s, dos, vns, dvns):
    I = range(len(cs))
    q, k, v, beta, tm, E, xs, kb, u, w = ([c[n] for c in cs] for n in ("q", "k", "v", "beta", "tm", "E", "xs", "kb", "u", "w"))
    doh, vnh, dvnh = ([_heads(a) for a in l] for l in (dos, vns, dvns))
    cat = lambda parts: jnp.concatenate(parts, axis=0)
    dqd = [cat([_dot(doh[i][h], Ss[i][h], NT) for h in range(NH)]) for i in I]
    dks = [cat([_dot(vnh[i][h], dS2s[i][h], NT) for h in range(NH)]) for i in I]
    dw = [-cat([_dot(dvnh[i][h], Ss[i][h], NT) for h in range(NH)]) for i in I]
    daqk = [_dot(dos[i], vns[i], NT) for i in I]
    drb = [_dot(tm[i], dvns[i], TN, hi=X3) for i in I]
    drw = [_dot(tm[i], dw[i], TN, hi=X3) for i in I]
    dA = [jnp.where(cs[i]["strict"], -(_dot(drb[i], u[i], NT) + _dot(drw[i], w[i], NT)), 0.0) for i in I]
    dM1 = [dA[i] * cs[i]["Ds"] for i in I]
    dM2 = [daqk[i] * cs[i]["Di"] for i in I]
    dkb = [_dot(dM1[i], k[i], NN) + drw[i] * E[i] for i in I]
    dk = [_dot(dM1[i], kb[i], TN) + _dot(dM2[i], q[i], TN) + dks[i] * xs[i] for i in I]
    dq = [_dot(dM2[i], k[i], NN) + dqd[i] * E[i] for i in I]
    on_diag = _iota((HS, HS), 0) == _iota((HS, HS), 1)
    out = []
    for i in I:
        G = dM1[i] * cs[i]["kk"] + dM2[i] * cs[i]["qk"]
        col = _rowsum(jnp.where(on_diag, jnp.broadcast_to(_colsum(G), (HS, HS)), 0.0))
        dxx = _rowsum(dks[i] * k[i]) * xs[i]
        dgc = _rowsum(G) - col + (_rowsum(dqd[i] * q[i]) + _rowsum(drw[i] * kb[i])) * E[i] - dxx
        at_last = _iota((CH, 1), 0) == cs[i]["last"]
        ends = []
        for h in range(NH):
            dgl = (_colsum(_rowsum(Ss[i][h] * dS2s[i][h])) * jnp.exp(cs[i]["gls"][h])
                   + _colsum(dxx[h * CH:(h + 1) * CH]))
            ends.append(jnp.where(at_last, dgl, 0.0))
        dbeta = _rowsum(drb[i] * v[i]) + _rowsum(dkb[i] * k[i])
        out.append((dq[i], dk[i] + dkb[i] * beta[i], drb[i] * beta[i], dgc + cat(ends), dbeta))
    return out


def _chunk_group(n, want=2):
    g = want
    while n % g:
        g //= 2
    return g


def _dn_chunks_fwd(qkv, pab, alr, dtr, rider=None):
    T = qkv.shape[0]
    n = T // CH
    G = _chunk_group(n, 4)
    csum, _ = _scan_consts()

    def body(q_ref, p_ref, cs_ref, al_ref, dt_ref, *outs):
        inst = [(g, d) for g in range(G) for d in range(2)]
        pres = [_dn_chunk_pre(q_ref[g * CH:(g + 1) * CH, :], p_ref[g * CH:(g + 1) * CH, :], al_ref[...], dt_ref[...],
                              cs_ref[d], d) for g, d in inst]
        tms = _inverses([-(p["kk"] * p["Ds"]) for p in pres])
        for (g, d), pre, tm in zip(inst, pres, tms):
            rows = slice(g * HS, (g + 1) * HS)
            u_ref, w_ref, ks_ref, qd_ref, aqk_ref, eg_ref, tm_ref = outs[7 * d:7 * d + 7]
            c = _dn_chunk_post(pre, tm)
            tm_ref[rows, :] = tm
            u_ref[rows, :] = c["u"]
            w_ref[rows, :] = c["w"].astype(BF16)
            ks_ref[rows, :] = c["ks"].astype(BF16)
            qd_ref[rows, :] = c["qd"].astype(BF16)
            aqk_ref[rows, :] = c["aqk"].astype(BF16)
            egs = [jnp.broadcast_to(jnp.exp(gl), (1, LANES)) for gl in c["gls"]]
            eg_ref[g * 8:(g + 1) * 8, :] = jnp.concatenate(egs + [jnp.zeros((8 - NH, LANES), F32)], axis=0)

    st = lambda w_: pl.BlockSpec((G * HS, w_), lambda i: (i, 0))
    one = [st(DH)] * 4 + [st(HS), pl.BlockSpec((G * 8, LANES), lambda i: (i, 0)), st(HS)]
    shp = [_S((n * HS, DH)), _S((n * HS, DH), BF16), _S((n * HS, DH), BF16), _S((n * HS, DH), BF16),
           _S((n * HS, HS), BF16), _S((n * 8, LANES)), _S((n * HS, HS))]
    res = _call(
        body, "dn_chunks_fwd", (n // G,),
        [_rows(G * CH, 3 * DN), _rows(G * CH, LANES), _full((2, CH, CH)), _full((1, LANES)), _full((1, LANES))],
        one * 2, shp * 2, rider=rider)(qkv, pab, csum, alr, dtr)
    outs, riding = (res, None) if rider is None else res
    parts = tuple(outs[:7]), tuple(outs[7:])
    return parts if rider is None else (parts, riding)


def _scan_order(n, ncx):
    return (lambda i: i), (lambda i: jnp.where(i < ncx, ncx - 1 - i, n - 1 - (i - ncx)))


def _scan_specs(order):
    st = lambda w_: pl.BlockSpec((HS, w_), lambda i: (order(i), 0))
    return dict(st=st(DH), aqk=st(HS), eg=pl.BlockSpec((8, LANES), lambda i: (order(i), 0)),
                tok=pl.BlockSpec((CH, DN), lambda i: (order(i), 0)), state=pl.BlockSpec((1, DN, DH), lambda i: (order(i), 0, 0)))


def _scan_fwd(parts, T, tc, rider=None):
    n = T // CH
    orders = _scan_order(n, tc // CH)

    def body(*refs):
        S_f, S_b = refs[-2:]

        @pl.when(pl.program_id(0) == 0)
        def _():
            S_f[...] = jnp.zeros_like(S_f)
            S_b[...] = jnp.zeros_like(S_b)

        for d, S in enumerate((S_f, S_b)):
            u_ref, w_ref, ks_ref, qd_ref, aqk_ref, eg_ref = refs[6 * d:6 * d + 6]
            o_ref, ss_ref, vn_ref = refs[12 + 3 * d:15 + 3 * d]
            ss_ref[0] = S[...]
            Sh = [S[_hs(h), :] for h in range(NH)]
            wh, ksh, qdh = _heads(w_ref[...]), _heads(ks_ref[...]), _heads(qd_ref[...])
            vn = u_ref[...] - jnp.concatenate([_dot(wh[h], Sh[h], NN) for h in range(NH)], axis=0)
            vn_ref[...] = vn
            av, vnh = _heads(_dot(aqk_ref[...], vn, NN)), _heads(vn)
            for h in range(NH):
                o_ref[:, _hs(h)] = _dot(qdh[h], Sh[h], NN) + av[h]
                S[_hs(h), :] = Sh[h] * eg_ref[h:h + 1, :] + _dot(ksh[h], vnh[h], TN)

    ins, outs, shp = [], [], []
    for d in range(2):
        sp = _scan_specs(orders[d])
        ins += [sp["st"]] * 4 + [sp["aqk"], sp["eg"]]
        outs += [sp["tok"], sp["state"], sp["st"]]
        shp += [_S((T, DN)), _S((n, DN, DH)), _S((n * HS, DH))]
    res = _call(body, "scan_fwd", (n,), ins, outs, shp, scratch=[pltpu.VMEM((DN, DH), F32), pltpu.VMEM((DN, DH), F32)],
                rider=rider)(*parts[0][:6], *parts[1][:6])
    res, riding = (res, None) if rider is None else res
    out = tuple(res[:3]), tuple(res[3:])
    return out if rider is None else (out, riding)


def _scan_bwd(do, parts, tc):
    T = do.shape[0]
    n = T // CH
    fwd_orders = _scan_order(n, tc // CH)
    orders = [lambda s, f=f: f(n - 1 - s) for f in fwd_orders]

    def body(*refs):
        dS_f, dS_b = refs[-2:]

        @pl.when(pl.program_id(0) == 0)
        def _():
            dS_f[...] = jnp.zeros_like(dS_f)
            dS_b[...] = jnp.zeros_like(dS_b)

        for d, dS in enumerate((dS_f, dS_b)):
            do_ref, w_ref, ks_ref, qd_ref, aqk_ref, eg_ref = refs[6 * d:6 * d + 6]
            dvn_ref, dss_ref = refs[12 + 2 * d:14 + 2 * d]
            dss_ref[0] = dS[...]
            dSh = [dS[_hs(h), :] for h in range(NH)]
            wh, ksh, qdh = _heads(w_ref[...]), _heads(ks_ref[...]), _heads(qd_ref[...])
            do_st = _stack(do_ref[...])
            dvn = _dot(aqk_ref[...], do_st, TN) + jnp.concatenate([_dot(ksh[h], dSh[h], NN) for h in range(NH)], axis=0)
            dvn_ref[...] = dvn
            doh, dvnh = _heads(do_st), _heads(dvn)
            for h in range(NH):
                dS[_hs(h), :] = _dot(qdh[h], doh[h], TN) + dSh[h] * eg_ref[h:h + 1, :] - _dot(wh[h], dvnh[h], TN)

    ins, outs, shp, args = [], [], [], []
    for d in range(2):
        sp = _scan_specs(orders[d])
        ins += [sp["tok"]] + [sp["st"]] * 3 + [sp["aqk"], sp["eg"]]
        outs += [sp["st"], sp["state"]]
        shp += [_S((n * HS, DH)), _S((n, DN, DH))]
        args += [do, *parts[d][1:6]]
    res = _call(body, "scan_bwd", (n,), ins, outs, shp,
                scratch=[pltpu.VMEM((DN, DH), F32), pltpu.VMEM((DN, DH), F32)])(*args)
    return tuple(res[:2]), tuple(res[2:])


def _dn_chunks_bwd(qkv, pab, alr, dtr, do, fwd, bwd, rider=None):
    T = qkv.shape[0]
    n = T // CH
    G = _chunk_group(n)
    csum, csum_t = _scan_consts()

    def body(q_ref, p_ref, do_ref, cs_ref, cst_ref, al_ref, dt_ref, *refs):
        dq_refs, dp_refs, acc_ref = refs[10:12], refs[12:14], refs[14]

        @pl.when(pl.program_id(0) == 0)
        def _():
            acc_ref[...] = jnp.zeros_like(acc_ref)

        lane = _iota((CH, LANES), 1)
        inst = [(g, d) for g in range(G) for d in range(2)]
        cs, Ss, dS2s, dos, vns, dvns = [], [], [], [], [], []
        for g, d in inst:
            tok, rows = slice(g * CH, (g + 1) * CH), slice(g * HS, (g + 1) * HS)
            vn_ref, dvn_ref, ss_ref, dss_ref, tm_ref = refs[5 * d:5 * d + 5]
            cs.append(_dn_chunk_post(
                _dn_chunk_pre(q_ref[tok, :], p_ref[tok, :], al_ref[...], dt_ref[...], cs_ref[d], d), tm_ref[rows, :]))
            Ss.append([ss_ref[g, _hs(h), :] for h in range(NH)])
            dS2s.append([dss_ref[g, _hs(h), :] for h in range(NH)])
            dos.append(_stack(do_ref[tok, :]))
            vns.append(vn_ref[rows, :])
            dvns.append(dvn_ref[rows, :])
        for (g, d), c, (dq, dk, dv, dgc, dbeta) in zip(inst, cs, _dn_chunks_bwd_math(cs, Ss, dS2s, dos, vns, dvns)):
            tok = slice(g * CH, (g + 1) * CH)
            dgcm = jnp.zeros((CH, LANES), F32)
            dbm = jnp.zeros((CH, LANES), F32)
            for h, (a, b_, c_, e, f) in enumerate(zip(*map(_heads, (dq, dk, dv, dgc, dbeta)))):
                dq_refs[d][tok, _hs(h)] = a
                dq_refs[d][tok, _hs(NH + h)] = b_
                dq_refs[d][tok, _hs(2 * NH + h)] = c_
                dgcm = jnp.where(lane == d * NH + h, e, dgcm)
                dbm = jnp.where(lane == 8 + d * NH + h, f, dbm)
            dgm = _dot(cst_ref[d], dgcm, NN, hi=True)
            dsp = dgm * c["nexp"] * jax.nn.sigmoid(c["sp_in"])
            dp_refs[d][tok, :] = dsp + dbm * c["bm"] * (1.0 - c["bm"])
            acc_ref[0:1, :] += _colsum(dgm * c["gm"])
            acc_ref[1:2, :] += _colsum(dsp)

    st = pl.BlockSpec((G * HS, DH), lambda i: (i, 0))
    state = pl.BlockSpec((G, DN, DH), lambda i: (i, 0, 0))
    return _call(
        body, "dn_chunks_bwd", (n // G,),
        [_rows(G * CH, 3 * DN), _rows(G * CH, LANES), _rows(G * CH, DN), _full((2, CH, CH)), _full((2, CH, CH)),
         _full((1, LANES)), _full((1, LANES))] + [st, st, state, state, pl.BlockSpec((G * HS, HS), lambda i: (i, 0))] * 2,
        [_rows(G * CH, 3 * DN)] * 2 + [_rows(G * CH, LANES)] * 2 + [_full((8, LANES))],
        [_S((T, 3 * DN))] * 2 + [_S((T, LANES))] * 2 + [_S((8, LANES))], rider=rider)(
            qkv, pab, do, csum, csum_t, alr, dtr, *fwd, *bwd)


def _head_out(o, z, g):
    on = o * lax.rsqrt(jnp.mean(o * o, axis=-1, keepdims=True) + EPS) * g
    return on * _silu(z)


def _mix_branches(of_ref, ob_ref, z_ref, yp_ref, ys_ref, pg_ref, gdn_ref, wa_ref, wb_ref, wc_ref):
    ons, ya = [], None
    for h in range(NH):
        on = _head_out(of_ref[:, _hs(h)] + ob_ref[:, _hs(h)], z_ref[:, _hs(h)], gdn_ref[...])
        t = _dot(on, wa_ref[_hs(h), :], NN)
        ya = t if ya is None else ya + t
        ons.append(on)
    ys = [ya, _dot(yp_ref[...], wb_ref[...], NN), _dot(ys_ref[...], wc_ref[...], NN)]
    sg = [jax.nn.sigmoid(pg_ref[:, k * D:(k + 1) * D]) for k in range(3)]
    return ons, ys, sg


def _mix_fwd(X, of, ob, z, yp, ys, pg, mv, gdn, wa, wb, wc, wo, tc, tt):
    T = X.shape[0]

    def body(x_ref, of_ref, ob_ref, z_ref, yp_ref, ys_ref, pg_ref, mv_ref, gdn_ref, wa_ref, wb_ref, wc_ref, wo_ref,
             x1_ref):
        _, yb, sg = _mix_branches(of_ref, ob_ref, z_ref, yp_ref, ys_ref, pg_ref, gdn_ref, wa_ref, wb_ref, wc_ref)
        mix = _dot(sg[0] * yb[0] + sg[1] * yb[1] + sg[2] * yb[2], wo_ref[...], NN)
        _, gate = _stream_rows(mv_ref, pl.program_id(0), tt, tc, 2)
        x1_ref[...] = x_ref[...] + gate * mix

    return _call(
        body, "mix_fwd", (T // tt,),
        [_rows(tt, D), _rows(tt, DN), _rows(tt, DN), _rows(tt, DN), _rows(tt, PW), _rows(tt, PW), _rows(tt, 3 * D),
         _full((8, D)), _full((1, DH)), _full(wa.shape), _full(wb.shape), _full(wc.shape), _full(wo.shape)],
        _rows(tt, D), _S((T, D)))(X, of, ob, z, yp, ys, pg, mv, gdn, wa, wb, wc, wo)


def _mix_bwd(dx1, of, ob, z, yp, ys, pg, mv, gdn, wa, wb, wc, wo, tc, tt):
    T = dx1.shape[0]

    def body(dx_ref, of_ref, ob_ref, z_ref, yp_ref, ys_ref, pg_ref, mv_ref, gdn_ref, wa_ref, wb_ref, wc_ref, wo_ref,
             do_ref, dz_ref, dyp_ref, dys_ref, dpg_ref, dwa_ref, dwb_ref, dwc_ref, dwo_ref, dgdn_ref, dm_ref):
        i = pl.program_id(0)

        @pl.when(i == 0)
        def _():
            for r in (dwa_ref, dwb_ref, dwc_ref, dwo_ref, dgdn_ref, dm_ref):
                r[...] = jnp.zeros_like(r)

        ons, yb, sg = _mix_branches(of_ref, ob_ref, z_ref, yp_ref, ys_ref, pg_ref, gdn_ref, wa_ref, wb_ref, wc_ref)
        ymix = sg[0] * yb[0] + sg[1] * yb[1] + sg[2] * yb[2]
        isc, gate = _stream_rows(mv_ref, i, tt, tc, 2)
        dx = dx_ref[...]
        dmix = dx * gate
        _acc_stream(dm_ref, 2, isc, dx * _dot(ymix, wo_ref[...], NN))
        dwo_ref[...] += _dot(ymix, dmix, TN)
        dymix = _dot(dmix, wo_ref[...], NT)
        dyb = []
        for k in range(3):
            dyb.append(dymix * sg[k])
            dpg_ref[:, k * D:(k + 1) * D] = dymix * yb[k] * sg[k] * (1.0 - sg[k])
        dwb_ref[...] += _dot(yp_ref[...], dyb[1], TN)
        dwc_ref[...] += _dot(ys_ref[...], dyb[2], TN)
        dyp_ref[...] = _dot(dyb[1], wb_ref[...], NT)
        dys_ref[...] = _dot(dyb[2], wc_ref[...], NT)
        dg = jnp.zeros((1, DH), F32)
        for h in range(NH):
            dwa_ref[_hs(h), :] += _dot(ons[h], dyb[0], TN)
            don = _dot(dyb[0], wa_ref[_hs(h), :], NT)
            _, vjp = jax.vjp(_head_out, of_ref[:, _hs(h)] + ob_ref[:, _hs(h)], z_ref[:, _hs(h)], gdn_ref[...])
            do_h, dz_h, dg_h = vjp(don)
            do_ref[:, _hs(h)] = do_h
            dz_ref[:, _hs(h)] = dz_h
            dg = dg + dg_h
        dgdn_ref[...] += dg

    return _call(
        body, "mix_bwd", (T // tt,),
        [_rows(tt, D), _rows(tt, DN), _rows(tt, DN), _rows(tt, DN), _rows(tt, PW), _rows(tt, PW), _rows(tt, 3 * D),
         _full((8, D)), _full((1, DH)), _full(wa.shape), _full(wb.shape), _full(wc.shape), _full(wo.shape)],
        [_rows(tt, DN), _rows(tt, DN), _rows(tt, PW), _rows(tt, PW), _rows(tt, 3 * D),
         _full(wa.shape), _full(wb.shape), _full(wc.shape), _full(wo.shape), _full((1, DH)), _full((8, D))],
        [_S((T, DN)), _S((T, DN)), _S((T, PW)), _S((T, PW)), _S((T, 3 * D)),
         _S(wa.shape), _S(wb.shape), _S(wc.shape), _S(wo.shape), _S((1, DH)), _S((8, D))])(
            dx1, of, ob, z, yp, ys, pg, mv, gdn, wa, wb, wc, wo)


def _ffn_fwd(X1, mv, g, wgu, wd, tc, tt):
    T = X1.shape[0]

    def body(x_ref, mv_ref, g_ref, wgu_ref, wd_ref, x2_ref, ff_ref):
        i = pl.program_id(0)
        _, sh = _stream_rows(mv_ref, i, tt, tc, 0)
        _, sc = _stream_rows(mv_ref, i, tt, tc, 1)
        _, gate = _stream_rows(mv_ref, i, tt, tc, 2)
        x = x_ref[...]
        gu = _dot(_modulate(x, g_ref[...], sh, sc), wgu_ref[...], NN)
        ff = _dot(_silu(gu[:, :DFF]) * gu[:, DFF:], wd_ref[...], NN)
        ff_ref[...] = ff
        x2_ref[...] = x + gate * ff

    return _call(
        body, "ffn_fwd", (T // tt,),
        [_rows(tt, D), _full((8, D)), _full((1, D)), _full(wgu.shape), _full(wd.shape)],
        [_rows(tt, D)] * 2, [_S((T, D))] * 2)(X1, mv, g, wgu, wd)


def _ffn_bwd(X1, ff, dx2, mv, g, wgu, wd, tc, tt, rider=None):
    T = X1.shape[0]

    def body(x_ref, ff_ref, dx2_ref, mv_ref, g_ref, wgu_ref, wd_ref, dx1_ref, h_ref, dgu_ref, act_ref, dff_ref, dg_ref,
             dm_ref):
        i = pl.program_id(0)
        isc, sh = _stream_rows(mv_ref, i, tt, tc, 0)
        _, sc = _stream_rows(mv_ref, i, tt, tc, 1)
        _, gate = _stream_rows(mv_ref, i, tt, tc, 2)
        x, dx2_ = x_ref[...], dx2_ref[...]
        h, vjp = jax.vjp(_modulate, x, g_ref[...], sh, sc)
        hb = h.astype(BF16)
        h_ref[...] = hb
        gu = jnp.dot(hb, wgu_ref[...], preferred_element_type=F32)
        ga, up = gu[:, :DFF], gu[:, DFF:]
        sg = jax.nn.sigmoid(ga)
        act = (ga * sg * up).astype(BF16)
        act_ref[...] = act
        dff = dx2_ * gate
        dff_ref[...] = dff.astype(BF16)
        dact = _dot(dff, wd_ref[...], NT)
        dga = (dact * up * (sg * (1.0 + ga * (1.0 - sg)))).astype(BF16)
        dup = (dact * ga * sg).astype(BF16)
        dgu_ref[:, :DFF] = dga
        dgu_ref[:, DFF:] = dup
        dh = _dot(dga, wgu_ref[:, :DFF], NT) + _dot(dup, wgu_ref[:, DFF:], NT)
        dx, dg, dsh, dsc = vjp(dh)
        dx1_ref[...] = dx2_ + dx

        @pl.when(i == 0)
        def _():
            dg_ref[...] = jnp.zeros_like(dg_ref)
            dm_ref[...] = jnp.zeros_like(dm_ref)

        dg_ref[...] += dg
        _acc_stream(dm_ref, 0, isc, dsh)
        _acc_stream(dm_ref, 1, isc, dsc)
        _acc_stream(dm_ref, 2, isc, dx2_ * ff_ref[...])

    return _call(
        body, "ffn_bwd", (T // tt,),
        [_rows(tt, D), _rows(tt, D), _rows(tt, D), _full((8, D)), _full((1, D)), _full(wgu.shape), _full(wd.shape)],
        [_rows(tt, D), _rows(tt, D), _rows(tt, 2 * DFF), _rows(tt, DFF), _rows(tt, D), _full((1, D)), _full((8, D))],
        [_S((T, D)), _S((T, D), BF16), _S((T, 2 * DFF), BF16), _S((T, DFF), BF16), _S((T, D), BF16),
         _S((1, D)), _S((8, D))], rider=rider)(X1, ff, dx2, mv, g, wgu, wd)


def _rms(x, g):
    return x * lax.rsqrt(jnp.mean(x * x, axis=-1, keepdims=True) + EPS) * g


def _loss_head(X2, tgt, gf, tc):
    T = X2.shape[0]

    def body(x_ref, t_ref, g_ref, dx_ref, loss_ref, dg_ref):
        i = pl.program_id(0)

        @pl.when(i == 0)
        def _():
            dx_ref[...] = jnp.zeros_like(dx_ref)
            loss_ref[...] = jnp.zeros_like(loss_ref)
            dg_ref[...] = jnp.zeros_like(dg_ref)

        @pl.when(i > 0)
        def _():
            y, vjp = jax.vjp(_rms, x_ref[...], g_ref[...])
            err = y - t_ref[...]
            dx, dg = vjp(err * (1.0 / D))
            dx_ref[...] = dx
            dg_ref[...] += dg
            loss_ref[...] += (0.5 / D) * jnp.sum(jnp.sum(err * err, axis=1, keepdims=True), axis=0, keepdims=True)

    return _call(
        body, "loss_head", (T // tc,),
        [_rows(tc, D), pl.BlockSpec((tc, D), lambda i: (jnp.maximum(i - 1, 0), 0)), _full((1, D))],
        [_rows(tc, D), _full((8, LANES)), _full((1, D))],
        [_S((T, D)), _S((8, LANES)), _S((1, D))])(X2, tgt, gf)


def _block_diag(pw):
    g, n = pw.shape[0], pw.shape[1]
    out = jnp.zeros((g * n, g * n), pw.dtype)
    for k in range(g):
        out = lax.dynamic_update_slice(out, pw[k], (k * n, k * n))
    return out


def _split_w_in(w):
    parts = [w[:, IN_BOUNDS[k]:IN_BOUNDS[k + 1]] for k in range(8)]
    parts[2] = jnp.pad(parts[2], ((0, 0), (0, LANES - 16)))
    return parts


def _mod_rows(mods_l, k0):
    rows = [mods_l[s, (k0 + k) * D:(k0 + k + 1) * D] for s in (0, 1) for k in range(3)]
    return jnp.stack(rows + [jnp.zeros((D,), F32)] * 2)


def _lane_row(v8):
    return jnp.pad(v8.reshape(1, 8), ((0, 0), (0, LANES - 8)))


LAYERED = ("w_in", "w_br_a", "w_br_b", "w_br_c", "w_o", "w_gu", "w_down")
LATE = ("w_br_a", "w_br_b", "w_br_c", "w_o", "w_gu", "w_down")


def _device_step(x, c, ctx, tgt, wts, tt, comm=None):
    tc = ctx.shape[0]
    X = jnp.concatenate([ctx, x], axis=0)
    cc8 = jnp.concatenate([wts["c_ctx"][None, :], c, jnp.zeros((6, D), F32)], axis=0)
    w_ada = wts["w_ada"].astype(BF16)
    mods = _mod_fwd(cc8, w_ada, wts["b_ada"].reshape(NL, 1, 6 * D))

    saved = []
    for l in range(NL):
        ws = [w.astype(BF16) for w in _split_w_in(wts["w_in"][l])]
        mv1, mv2 = _mod_rows(mods[l], 0), _mod_rows(mods[l], 3)
        g1, g2 = wts["norm1_g"][l][None, :], wts["norm2_g"][l][None, :]
        cw, scw = wts["dn_conv_w"][l], wts["sc_conv_w"][l]
        alr, dtr = _lane_row(wts["dn_a_log"][l]), _lane_row(wts["dn_dt_bias"][l])
        gdn = wts["dn_norm_g"][l][None, :]
        pwbd, ps = _block_diag(wts["pool_w"][l]), wts["pool_scale"][l][None, :]
        hb, pq, pz, pab, pp, sx, sb, sc_, pg = _inproj_fwd(X, mv1, g1, ws, tc, tt)
        qkv = _dnprep_fwd(pq, cw, tc, tt)
        if comm is not None and l == 0:
            parts, riding = _dn_chunks_fwd(qkv, pab, alr, dtr, rider=comm.late_weights_chips())
            ((of, ssf, vnf), (ob, ssb, vnb)), riding = _scan_fwd(parts, X.shape[0], tc,
                                                                 rider=comm.late_weights_pair(riding))
            wts = dict(wts, **comm.late_weights(riding))
        else:
            parts = _dn_chunks_fwd(qkv, pab, alr, dtr)
            (of, ssf, vnf), (ob, ssb, vnb) = _scan_fwd(parts, X.shape[0], tc)
        wbr = [wts[k][l].astype(BF16) for k in LATE]
        yp = _pool_fwd(pp, pwbd, ps, tc)
        ys = _sc_fwd(sx, sb, sc_, scw, tc, tt)
        X1 = _mix_fwd(X, of, ob, pz, yp, ys, pg, mv1, gdn, *wbr[:4], tc, tt)
        X2, ff = _ffn_fwd(X1, mv2, g2, wbr[4], wbr[5], tc, tt)
        saved.append(dict(X=X, X1=X1, ff=ff, ws=ws, wbr=wbr, mv1=mv1, mv2=mv2, g1=g1, g2=g2, cw=cw, scw=scw, alr=alr, dtr=dtr,
                          gdn=gdn, pwbd=pwbd, ps=ps, hb=hb, pq=pq, pz=pz, pab=pab, pp=pp, sx=sx, sb=sb, sc=sc_, pg=pg,
                          qkv=qkv, of=of, ob=ob, ssf=ssf, ssb=ssb, vnf=vnf, vnb=vnb, parts=parts, yp=yp, ys=ys))
        X = X2

    dX, loss, dgf = _loss_head(X, tgt, wts["final_norm_g"][None, :], tc)

    gl = {k: [None] * NL for k in ("w_in", "norm1_g", "norm2_g", "dn_conv_w", "dn_a_log", "dn_dt_bias", "dn_norm_g",
                                   "pool_w", "pool_scale", "sc_conv_w", "w_br_a", "w_br_b", "w_br_c", "w_o", "w_gu",
                                   "w_down")}
    dmods = [None] * NL
    early = None
    for l in reversed(range(NL)):
        s = saved[l]
        hide = comm is not None and l == 0
        res = _ffn_bwd(s["X1"], s["ff"], dX, s["mv2"], s["g2"], s["wbr"][4], s["wbr"][5], tc, tt,
                       rider=comm.grad_pair_rider([gl[k][1] for k in LAYERED]) if hide else None)
        if hide:
            res, got = res
            chip_rider = comm.grad_chip_rider(comm.pair_sums(got))
        dx1, h2, dgu, act, dff, dg2, dm2 = res
        gl["w_gu"][l] = _dw(h2, dgu, tt)
        gl["w_down"][l] = _dw(act, dff, tt)
        do, dz, dyp, dys, dpg, dwa, dwb, dwc, dwo, dgdn, dmg = _mix_bwd(
            dx1, s["of"], s["ob"], s["pz"], s["yp"], s["ys"], s["pg"], s["mv1"], s["gdn"], *s["wbr"][:4], tc, tt)
        dpp, dpw, dps = _pool_bwd(s["pp"], s["pwbd"], s["ps"], dyp, tc)
        dsx, dsb, dsc, dscw = _sc_bwd(s["sx"], s["sb"], s["sc"], s["scw"], dys, tc, tt)
        (dvnf, dssf), (dvnb, dssb) = _scan_bwd(do, s["parts"], tc)
        res = _dn_chunks_bwd(s["qkv"], s["pab"], s["alr"], s["dtr"], do,
                             (s["vnf"], dvnf, s["ssf"], dssf, s["parts"][0][6]),
                             (s["vnb"], dvnb, s["ssb"], dssb, s["parts"][1][6]), rider=chip_rider if hide else None)
        if hide:
            res, early = res
        dqf, dqb, dpf, dpb, gacc = res
        dy = _dnprep_bwd_act(s["pq"], s["cw"], dqf, dqb, tc, tt)
        dpq, dcw = _conv_bwd(dy, s["pq"], s["cw"], tc, tt)
        dps_ = [dpq, dz, dpf, dpb, dpp, dsx, dsb, dsc, dpg]
        dp_w = [0, 1, 2, 2, 3, 4, 5, 6, 7]
        dX, dg1, dm1 = _inproj_bwd(s["X"], s["mv1"], s["g1"], s["ws"], dps_, dp_w, dx1, tc, tt)
        dws = [_dw(s["hb"], dpq, tt), _dw(s["hb"], dz, tt), _dw(s["hb"], dpf + dpb, tt)[:, :16], _dw(s["hb"], dpp, tt),
               _dw(s["hb"], dsx, tt), _dw(s["hb"], dsb, tt), _dw(s["hb"], dsc, tt), _dw(s["hb"], dpg, tt)]
        gl["w_in"][l] = jnp.concatenate(dws, axis=1)
        gl["norm1_g"][l], gl["norm2_g"][l] = dg1[0], dg2[0]
        gl["dn_conv_w"][l], gl["sc_conv_w"][l] = dcw, dscw
        gl["dn_a_log"][l], gl["dn_dt_bias"][l] = gacc[0, :8].reshape(2, NH), gacc[1, :8].reshape(2, NH)
        gl["dn_norm_g"][l] = dgdn[0]
        gl["pool_w"][l] = jnp.stack([dpw[k * GW:(k + 1) * GW, k * GW:(k + 1) * GW] for k in range(4)])
        gl["pool_scale"][l] = dps[0]
        gl["w_br_a"][l], gl["w_br_b"][l], gl["w_br_c"][l], gl["w_o"][l] = dwa, dwb, dwc, dwo
        dm = dm1 + dmg
        row = lambda r: jnp.concatenate([dm[r], dm[r + 1], dm[r + 2], dm2[r], dm2[r + 1], dm2[r + 2]])
        dmods[l] = jnp.stack([row(0), row(3)] + [jnp.zeros((6 * D,), F32)] * 6)

    dwada, dbada, dcc = _mod_bwd(cc8, w_ada, jnp.stack(dmods))
    grads = {k: (v if k in LAYERED else jnp.stack(v)) for k, v in gl.items()}
    grads.update(w_ada=dwada, b_ada=dbada.reshape(NL, 6 * D), c_ctx=dcc[0], final_norm_g=dgf[0])
    return loss, dX[tc:], grads, early


def _me():
    return lax.axis_index("x"), lax.axis_index("y"), lax.axis_index("c")


def _dev_index(p):
    return 4 * p[0] + 2 * p[1] + p[2]


def _allgather(parts):
    n = len(parts)

    def body(*refs):
        ins, outs = refs[:n], refs[n:2 * n]
        send_sems, recv_sems, local_sems = refs[2 * n:]
        x, y, c = _me()
        me, sibling = (x, y, c), (x, y, 1 - c)
        chips = [(1 - x, y), (x, 1 - y), (1 - x, 1 - y)]

        def copy(a, k, block, to, src=None):
            dst = outs[a].at[_dev_index(block)]
            return pltpu.make_async_remote_copy(
                src_ref=dst if src is None else src, dst_ref=dst, send_sem=send_sems.at[a, k], recv_sem=recv_sems.at[a, k],
                device_id=to, device_id_type=MESH_ID)

        mine, first, passed = [], [], []
        for a in range(n):
            mine.append(pltpu.make_async_copy(ins[a], outs[a].at[_dev_index(me)], local_sems.at[a]))
            mine[-1].start()
            first.append(copy(a, 0, me, sibling, src=ins[a]))
            first += [copy(a, 1 + j, me, (*chip, c), src=ins[a]) for j, chip in enumerate(chips)]
        for cp in first:
            cp.start()
        for a in range(n):
            for j, chip in enumerate(chips):
                copy(a, 1 + j, (*chip, c), me).wait_recv()
                passed.append(copy(a, 4 + j, (*chip, c), sibling))
                passed[-1].start()
        for a in range(n):
            copy(a, 0, sibling, me).wait_recv()
            for j, chip in enumerate(chips):
                copy(a, 4 + j, (*chip, 1 - c), me).wait_recv()
        for cp in first + passed:
            cp.wait_send()
        for cp in mine:
            cp.wait()

    return pl.pallas_call(
        body, name="allgather", in_specs=[HBM_SPEC] * n, out_specs=[HBM_SPEC] * n,
        out_shape=[_S((N_DEV,) + p.shape, p.dtype) for p in parts],
        scratch_shapes=[pltpu.SemaphoreType.DMA((n, 7)), pltpu.SemaphoreType.DMA((n, 7)), pltpu.SemaphoreType.DMA((n,))],
    )(*parts)


def _broadcast_small(small):
    def body(in_ref, out_ref, send_sems, recv_sems, local_sem):
        x, y, c = _me()
        my = _dev_index((x, y, c))
        mine = pltpu.make_async_copy(in_ref, out_ref.at[my], local_sem)
        mine.start()
        remote = []
        for k in range(1, N_DEV):
            cp = pltpu.make_async_remote_copy(
                src_ref=in_ref, dst_ref=out_ref.at[my], send_sem=send_sems.at[k - 1], recv_sem=recv_sems.at[k - 1],
                device_id=(x ^ (k >> 2), y ^ ((k >> 1) & 1), c ^ (k & 1)), device_id_type=MESH_ID)
            cp.start()
            remote.append(cp)
        for cp in remote:
            cp.wait_recv()
        for cp in remote:
            cp.wait_send()
        mine.wait()

    return pl.pallas_call(
        body, name="small_exchange", in_specs=[HBM_SPEC], out_specs=HBM_SPEC,
        out_shape=_S((N_DEV,) + small.shape, small.dtype),
        scratch_shapes=[pltpu.SemaphoreType.DMA((7,)), pltpu.SemaphoreType.DMA((7,)), pltpu.SemaphoreType.DMA],
    )(small)


def _run_rider(rider, name):
    ni, no = len(rider.ins), len(rider.out_shapes)

    def body(*refs):
        riding = (refs[:ni], refs[ni:ni + no], refs[ni + no:])
        rider.start(*riding)
        rider.wait(*riding)

    return list(pl.pallas_call(
        body, name=name, in_specs=[HBM_SPEC] * ni, out_specs=[HBM_SPEC] * no, out_shape=rider.out_shapes,
        scratch_shapes=rider.sems)(*rider.ins))


def _chip_peers(x, y):
    return [(k - 1, (x ^ (k >> 1), y ^ (k & 1))) for k in range(1, N_CHIP)]


def _pair_exchange(g2s):
    n = len(g2s)

    def copies(ins, outs, sems):
        x, y, c = _me()
        return [pltpu.make_async_remote_copy(
            src_ref=ins[a].at[1 - c, j], dst_ref=outs[a].at[j], send_sem=sems[0].at[a, j], recv_sem=sems[1].at[a, j],
            device_id=(x, y, 1 - c), device_id_type=MESH_ID) for a in range(n) for j in range(N_CHIP)], []

    return _Rider(g2s, [_S(g.shape[1:], g.dtype) for g in g2s],
                  [pltpu.SemaphoreType.DMA((n, N_CHIP)), pltpu.SemaphoreType.DMA((n, N_CHIP))], copies)


def _chip_exchange(s4s):
    n = len(s4s)

    def copies(ins, outs, sems):
        x, y, c = _me()
        my = 2 * x + y
        local = [pltpu.make_async_copy(ins[a].at[my], outs[a].at[my], sems[2].at[a]) for a in range(n)]
        remote = [pltpu.make_async_remote_copy(
            src_ref=ins[a].at[2 * px + py], dst_ref=outs[a].at[my], send_sem=sems[0].at[a, k], recv_sem=sems[1].at[a, k],
            device_id=(px, py, c), device_id_type=MESH_ID) for k, (px, py) in _chip_peers(x, y) for a in range(n)]
        return remote, local

    return _Rider(s4s, [_S(s.shape, s.dtype) for s in s4s],
                  [pltpu.SemaphoreType.DMA((n, N_CHIP - 1)), pltpu.SemaphoreType.DMA((n, N_CHIP - 1)),
                   pltpu.SemaphoreType.DMA((n,))], copies)


def _chip_gather(pack):
    def copies(ins, outs, sems):
        x, y, c = _me()
        my = 2 * x + y
        local = [pltpu.make_async_copy(ins[0], outs[0].at[my], sems[2])]
        remote = [pltpu.make_async_remote_copy(
            src_ref=ins[0], dst_ref=outs[0].at[my], send_sem=sems[0].at[k], recv_sem=sems[1].at[k],
            device_id=(px, py, c), device_id_type=MESH_ID) for k, (px, py) in _chip_peers(x, y)]
        return remote, local

    return _Rider([pack], [_S((N_CHIP,) + pack.shape, pack.dtype)],
                  [pltpu.SemaphoreType.DMA((N_CHIP - 1,)), pltpu.SemaphoreType.DMA((N_CHIP - 1,)), pltpu.SemaphoreType.DMA],
                  copies)


def _pair_gather(chips):
    def copies(ins, outs, sems):
        x, y, c = _me()
        local = [pltpu.make_async_copy(ins[0].at[j], outs[0].at[j, c], sems[2].at[j]) for j in range(N_CHIP)]
        remote = [pltpu.make_async_remote_copy(
            src_ref=ins[0].at[j], dst_ref=outs[0].at[j, c], send_sem=sems[0].at[j], recv_sem=sems[1].at[j],
            device_id=(x, y, 1 - c), device_id_type=MESH_ID) for j in range(N_CHIP)]
        return remote, local

    return _Rider([chips], [_S((N_CHIP, 2) + chips.shape[1:], chips.dtype)],
                  [pltpu.SemaphoreType.DMA((N_CHIP,)), pltpu.SemaphoreType.DMA((N_CHIP,)), pltpu.SemaphoreType.DMA((N_CHIP,))],
                  copies)


def _shard_rows(r):
    return 256 if r % 256 == 0 else r


def _pair_sum(g2, got):
    _, nc, L, R, C = g2.shape
    tr = _shard_rows(R)

    def body(a_ref, b_ref, o_ref):
        o_ref[...] = (a_ref[0] + b_ref[...]).astype(BF16)

    blk = pl.BlockSpec((1, 1, tr, C), lambda j, l, i: (j, l, i, 0))
    return _call(
        body, "pair_sum", (nc, L, R // tr),
        [pl.BlockSpec((1, 1, 1, tr, C), lambda j, l, i: (lax.axis_index("c"), j, l, i, 0)), blk], blk,
        _S(got.shape, BF16))(g2, got)


def _adam(w, g, m, v):
    m2 = ADAM_B1 * m + (1.0 - ADAM_B1) * g
    v2 = ADAM_B2 * v + (1.0 - ADAM_B2) * (g * g)
    m_hat = m2 / (1.0 - ADAM_B1 ** ADAM_STEP)
    v_hat = v2 / (1.0 - ADAM_B2 ** ADAM_STEP)
    return -ADAM_LR * (m_hat / (jnp.sqrt(v_hat) + ADAM_EPS) + ADAM_WD * w), m2, v2


def _sum_adam(recvs, w, m, v):
    L, R, C = w.shape
    tr = _shard_rows(R)
    nr = len(recvs)

    def body(*refs):
        w_ref, m_ref, v_ref, g_ref, d_ref, m2_ref, v2_ref = refs[nr:]
        g = None
        for li, r_ref in enumerate(refs[:nr]):
            s = r_ref[0, 0].astype(F32)
            for j in range(1, N_CHIP):
                s = s + r_ref[j, 0].astype(F32)
            g = s if g is None else jnp.where(pl.program_id(0) == li, s, g)
        g_ref[0] = g
        d_ref[0], m2_ref[0], v2_ref[0] = _adam(w_ref[0], g, m_ref[0], v_ref[0])

    blk = pl.BlockSpec((1, tr, C), lambda l, i: (l, i, 0))
    rspec = pl.BlockSpec((N_CHIP, 1, tr, C), (lambda l, i: (0, l, i, 0)) if nr == 1 else (lambda l, i: (0, 0, i, 0)))
    return _call(body, "sum_adam", (L, R // tr), [rspec] * nr + [blk, blk, blk], [blk] * 4, [_S(w.shape)] * 4)(
        *recvs, w, m, v)


def _sum_small(recv):
    def body(r_ref, o_ref):
        g = r_ref[0]
        for k in range(1, N_DEV):
            g = g + r_ref[k]
        o_ref[...] = g

    return pl.pallas_call(body, name="sum_small", out_shape=_S(recv.shape[1:]))(recv)


def _adam_small(w, g, m, v):
    def body(w_ref, g_ref, m_ref, v_ref, d_ref, m2_ref, v2_ref):
        d_ref[...], m2_ref[...], v2_ref[...] = _adam(w_ref[...], g_ref[...], m_ref[...], v_ref[...])

    return pl.pallas_call(body, name="adam_small", out_shape=[_S(w.shape)] * 3)(w, g, m, v)


def _pack(arrs, dtype, row_mult):
    parts, offs, r = [], [], 0
    for a in arrs:
        nr = -(-a.size // LANES)
        parts.append(jnp.pad(a.reshape(-1).astype(dtype), (0, nr * LANES - a.size)))
        offs.append(r)
        r += nr
    pad = (-r) % row_mult
    if pad:
        parts.append(jnp.zeros((pad * LANES,), dtype))
    return jnp.concatenate(parts).reshape(r + pad, LANES), offs


def _unpack(packed, offs, shapes, lead=()):
    out = []
    for off, shp in zip(offs, shapes):
        size = int(np.prod(shp))
        nr = -(-size // LANES)
        flat = packed[..., off:off + nr, :].reshape(lead + (nr * LANES,))
        out.append(flat[..., :size].reshape(lead + tuple(shp)))
    return out


BIG = (("w_ada", 2), ("w_in", 2), ("w_br_a", 2), ("w_br_b", 2), ("w_br_c", 2), ("w_o", 1), ("w_gu", 2), ("w_down", 1))
CONV = ("dn_conv_w", "sc_conv_w")
REPL = ("c_ctx", "b_ada", "norm1_g", "norm2_g", "dn_a_log", "dn_dt_bias", "dn_norm_g", "pool_w", "pool_scale",
        "final_norm_g")
WEIGHTS = ("c_ctx", "w_ada", "b_ada", "norm1_g", "norm2_g", "w_in", "dn_conv_w", "dn_a_log", "dn_dt_bias", "dn_norm_g",
           "pool_w", "pool_scale", "sc_conv_w", "w_br_a", "w_br_b", "w_br_c", "w_o", "w_gu", "w_down", "final_norm_g")
TOKEN_TILE = 256


def _join(blocks, axis):
    nd, nl, r, c = blocks.shape
    if axis == 2:
        return blocks.transpose(1, 2, 0, 3).reshape(nl, r, nd * c)
    return blocks.transpose(1, 0, 2, 3).reshape(nl, nd * r, c)


def _split(full, axis):
    nl, r, c = full.shape
    if axis == 2:
        return full.reshape(nl, r, N_CHIP, 2, c // N_DEV).transpose(3, 2, 0, 1, 4)
    return full.reshape(nl, N_CHIP, 2, r // N_DEV, c).transpose(2, 1, 0, 3, 4)


class _Comm:
    def __init__(self, late_shards):
        self.shapes = [a.shape for a in late_shards]
        self.pack, self.offs = _pack(late_shards, BF16, BF16_ROWS)
        self.g2s = None

    def late_weights_chips(self):
        return _chip_gather(self.pack)

    def late_weights_pair(self, riding):
        return _pair_gather(riding[0])

    def late_weights(self, riding):
        shards = _unpack(riding[0].reshape((N_DEV,) + self.pack.shape), self.offs, self.shapes, (N_DEV,))
        return {k: _join(blocks, dict(BIG)[k]) for k, blocks in zip(LATE, shards)}

    def grad_pair_rider(self, layer_grads):
        self.g2s = [_split(g[None], dict(BIG)[k]) for k, g in zip(LAYERED, layer_grads)]
        return _pair_exchange(self.g2s)

    def pair_sums(self, got):
        return [_pair_sum(g2, gt) for g2, gt in zip(self.g2s, got)]

    def grad_chip_rider(self, sums):
        return _chip_exchange(sums)


def kernel(x, c, ctx, c_ctx, w_ada, b_ada, norm1_g, norm2_g, w_in, dn_conv_w, dn_a_log, dn_dt_bias, dn_norm_g, pool_w, pool_scale, sc_conv_w, w_br_a, w_br_b, w_br_c, w_o, w_gu, w_down, final_norm_g, loss_target, m_c_ctx, m_w_ada, m_b_ada, m_norm1_g, m_norm2_g, m_w_in, m_dn_conv_w, m_dn_a_log, m_dn_dt_bias, m_dn_norm_g, m_pool_w, m_pool_scale, m_sc_conv_w, m_w_br_a, m_w_br_b, m_w_br_c, m_w_o, m_w_gu, m_w_down, m_final_norm_g, v_c_ctx, v_w_ada, v_b_ada, v_norm1_g, v_norm2_g, v_w_in, v_dn_conv_w, v_dn_a_log, v_dn_dt_bias, v_dn_norm_g, v_pool_w, v_pool_scale, v_sc_conv_w, v_w_br_a, v_w_br_b, v_w_br_c, v_w_o, v_w_gu, v_w_down, v_final_norm_g):
    loc = dict(c_ctx=c_ctx, w_ada=w_ada, b_ada=b_ada, norm1_g=norm1_g, norm2_g=norm2_g, w_in=w_in, dn_conv_w=dn_conv_w,
               dn_a_log=dn_a_log, dn_dt_bias=dn_dt_bias, dn_norm_g=dn_norm_g, pool_w=pool_w, pool_scale=pool_scale,
               sc_conv_w=sc_conv_w, w_br_a=w_br_a, w_br_b=w_br_b, w_br_c=w_br_c, w_o=w_o, w_gu=w_gu, w_down=w_down,
               final_norm_g=final_norm_g)
    mom_m = dict(c_ctx=m_c_ctx, w_ada=m_w_ada, b_ada=m_b_ada, norm1_g=m_norm1_g, norm2_g=m_norm2_g, w_in=m_w_in,
                 dn_conv_w=m_dn_conv_w, dn_a_log=m_dn_a_log, dn_dt_bias=m_dn_dt_bias, dn_norm_g=m_dn_norm_g,
                 pool_w=m_pool_w, pool_scale=m_pool_scale, sc_conv_w=m_sc_conv_w, w_br_a=m_w_br_a, w_br_b=m_w_br_b,
                 w_br_c=m_w_br_c, w_o=m_w_o, w_gu=m_w_gu, w_down=m_w_down, final_norm_g=m_final_norm_g)
    mom_v = dict(c_ctx=v_c_ctx, w_ada=v_w_ada, b_ada=v_b_ada, norm1_g=v_norm1_g, norm2_g=v_norm2_g, w_in=v_w_in,
                 dn_conv_w=v_dn_conv_w, dn_a_log=v_dn_a_log, dn_dt_bias=v_dn_dt_bias, dn_norm_g=v_dn_norm_g,
                 pool_w=v_pool_w, pool_scale=v_pool_scale, sc_conv_w=v_sc_conv_w, w_br_a=v_w_br_a, w_br_b=v_w_br_b,
                 w_br_c=v_w_br_c, w_o=v_w_o, w_gu=v_w_gu, w_down=v_w_down, final_norm_g=v_final_norm_g)
    my = _dev_index(_me())

    axis_of = dict(BIG)
    first = [k for k, _ in BIG if k not in LATE]
    big_pack, big_offs = _pack([loc[k] for k in first], BF16, BF16_ROWS)
    conv_pack, conv_offs = _pack([loc[k] for k in CONV], F32, 8)
    big_all, conv_all = _allgather([big_pack, conv_pack])
    full = {k: loc[k] for k in REPL}
    for k, blocks in zip(first, _unpack(big_all, big_offs, [loc[k].shape for k in first], (N_DEV,))):
        full[k] = _join(blocks, axis_of[k])
    for k, blocks in zip(CONV, _unpack(conv_all, conv_offs, [loc[k].shape for k in CONV], (N_DEV,))):
        full[k] = _join(blocks, 2)

    loss8, grad_x, g, recv_l1 = _device_step(x[0], c, ctx[0], loss_target[0], full, TOKEN_TILE,
                                             comm=_Comm([loc[k] for k in LATE]))

    tail = [_split(g[k][0][None], axis_of[k]) for k in LAYERED] + [_split(g["w_ada"], 2)]
    got = _run_rider(_pair_exchange(tail), "pair_exchange")
    recv_tail = _run_rider(_chip_exchange([_pair_sum(a, b) for a, b in zip(tail, got)]), "chip_exchange")
    recv_big = {k: [recv_tail[i], recv_l1[i]] for i, k in enumerate(LAYERED)}
    recv_big["w_ada"] = [recv_tail[-1]]

    small_names = REPL + CONV
    small_pack, small_offs = _pack([g[k] for k in small_names] + [loss8[0:1, 0:1]], F32, 8)
    small_sum = _sum_small(_broadcast_small(small_pack))
    sums = _unpack(small_sum, small_offs, [g[k].shape for k in small_names] + [(1, 1)])
    grads = dict(zip(small_names, sums[:-1]))
    loss = sums[-1][0, 0]
    for k in CONV:
        w = loc[k].shape[2]
        grads[k] = lax.dynamic_slice_in_dim(grads[k], my * w, w, axis=2)

    delta, new_m, new_v = {}, {}, {}
    for k, _ in BIG:
        grads[k], delta[k], new_m[k], new_v[k] = _sum_adam(recv_big[k], loc[k], mom_m[k], mom_v[k])
    packs = [_pack([src[k] for k in small_names], F32, 8)[0] for src in (loc, grads, mom_m, mom_v)]
    _, offs = _pack([loc[k] for k in small_names], F32, 8)
    shapes = [loc[k].shape for k in small_names]
    for dst, packed in zip((delta, new_m, new_v), _adam_small(*packs)):
        dst.update(zip(small_names, _unpack(packed, offs, shapes)))

    return (loss, grad_x[None], *[grads[k] for k in WEIGHTS], *[delta[k] for k in WEIGHTS],
            *[new_m[k] for k in WEIGHTS], *[new_v[k] for k in WEIGHTS])
```

```python
import functools

import numpy as np
import jax
import jax.numpy as jnp
from jax import lax
from jax.experimental import pallas as pl
from jax.experimental.pallas import tpu as pltpu

F32 = jnp.float32
BF16 = jnp.bfloat16
HI = lax.Precision.HIGHEST

D = 1024
NL = 2
NH = 4
DH = 128
DN = NH * DH
CH = 64
GW = 64
PW = 256
DFF = 2816
EPS = 1e-6
N_DEV = 8
N_CHIP = 4
MESH_ID = pl.DeviceIdType.MESH
HBM_SPEC = pl.BlockSpec(memory_space=pltpu.HBM)
LANES = 128
BF16_ROWS = 16
VMEM_MB = 56

ADAM_LR, ADAM_B1, ADAM_B2, ADAM_EPS, ADAM_WD, ADAM_STEP = 0.001, 0.9, 0.999, 1e-08, 0.01, 10

IN_BOUNDS = (0, 1536, 2048, 2064, 2320, 2576, 2832, 3088, 6160)
IN_WIDTHS = (1536, 512, 128, 256, 256, 256, 256, 3072)
POOL_WIN = ((1, 0), (2, 1), (4, 3), (8, 7))

NN = ((1,), (0,))
NT = ((1,), (1,))
TN = ((0,), (0,))


def _dot(a, b, dims, hi=False):
    if hi:
        prec = lax.Precision.HIGH if hi == "x3" else HI
        return lax.dot_general(a, b, (dims, ((), ())), precision=prec, preferred_element_type=F32)
    return lax.dot_general(a.astype(BF16), b.astype(BF16), (dims, ((), ())), preferred_element_type=F32)


def _S(shape, dtype=F32):
    return jax.ShapeDtypeStruct(tuple(shape), dtype)


def _full(shape):
    nd = len(shape)
    return pl.BlockSpec(tuple(shape), lambda *_: (0,) * nd)


def _rows(tt, w):
    return pl.BlockSpec((tt, w), lambda i: (i, 0))


class _Rider:
    def __init__(self, ins, out_shapes, sems, copies):
        self.ins, self.out_shapes, self.sems, self.copies = list(ins), list(out_shapes), list(sems), copies

    def start(self, ins, outs, sems):
        remote, local = self.copies(ins, outs, sems)
        for cp in local + remote:
            cp.start()

    def wait(self, ins, outs, sems):
        remote, local = self.copies(ins, outs, sems)
        for cp in remote:
            cp.wait_recv()
        for cp in remote:
            cp.wait_send()
        for cp in local:
            cp.wait()


def _call(body, name, grid, in_specs, out_specs, out_shape, scratch=(), rider=None):
    params = pltpu.CompilerParams(dimension_semantics=("arbitrary",) * len(grid), vmem_limit_bytes=VMEM_MB << 20)
    if rider is None:
        return pl.pallas_call(body, name=name, grid=grid, in_specs=in_specs, out_specs=out_specs, out_shape=out_shape,
                              scratch_shapes=list(scratch), compiler_params=params)
    single = not isinstance(out_shape, (list, tuple))
    out_specs, out_shape = ([out_specs], [out_shape]) if single else (list(out_specs), list(out_shape))
    n_in, n_out, n_scr = len(in_specs), len(out_shape), len(scratch)
    r_in, r_out = len(rider.ins), len(rider.out_shapes)

    def hosted(*refs):
        ins, refs = refs[:n_in + r_in], refs[n_in + r_in:]
        outs, scr = refs[:n_out + r_out], refs[n_out + r_out:]
        riding = (ins[n_in:], outs[n_out:], scr[n_scr:])

        @pl.when(pl.program_id(0) == 0)
        def _():
            rider.start(*riding)

        body(*ins[:n_in], *outs[:n_out], *scr[:n_scr])

        @pl.when(pl.program_id(0) == grid[0] - 1)
        def _():
            rider.wait(*riding)

    call = pl.pallas_call(
        hosted, name=name, grid=grid, in_specs=list(in_specs) + [HBM_SPEC] * r_in,
        out_specs=out_specs + [HBM_SPEC] * r_out, out_shape=out_shape + rider.out_shapes,
        scratch_shapes=list(scratch) + rider.sems, compiler_params=params)

    def run(*args):
        res = call(*args, *rider.ins)
        own = res[:n_out]
        return (own[0] if single else own), list(res[n_out:])

    return run


def _iota(shape, axis):
    return lax.broadcasted_iota(jnp.int32, shape, axis)


def _colsum(a):
    return jnp.sum(a, axis=0, keepdims=True)


def _silu(x):
    return x * jax.nn.sigmoid(x)


def _modulate(x, g, sh, sc):
    xn = x * lax.rsqrt(jnp.mean(x * x, axis=-1, keepdims=True) + EPS)
    return (xn * g) * (1.0 + sc) + sh


def _stream_rows(mv_ref, i, tt, tc, k):
    isc = (i * tt + _iota((tt, 1), 0)) < tc
    return isc, jnp.where(isc, mv_ref[k:k + 1, :], mv_ref[3 + k:4 + k, :])


def _acc_stream(ref, k, isc, val):
    ref[k:k + 1, :] += _colsum(jnp.where(isc, val, 0.0))
    ref[3 + k:4 + k, :] += _colsum(jnp.where(isc, 0.0, val))


MOD_CT = 1536


def _mod_fwd(cc8, w_ada, b_ada3):
    def body(cc_ref, w_ref, b_ref, o_ref):
        o_ref[0] = _dot(_silu(cc_ref[...]), w_ref[0], NN) + b_ref[0]

    return _call(
        body, "mod_fwd", (NL, 6 * D // MOD_CT),
        [pl.BlockSpec((8, D), lambda l, j: (0, 0)), pl.BlockSpec((1, D, MOD_CT), lambda l, j: (l, 0, j)),
         pl.BlockSpec((1, 1, MOD_CT), lambda l, j: (l, 0, j))],
        pl.BlockSpec((1, 8, MOD_CT), lambda l, j: (l, 0, j)), _S((NL, 8, 6 * D)))(cc8, w_ada, b_ada3)


def _mod_bwd(cc8, w_ada, dmods):
    def body(cc_ref, w_ref, dm_ref, dw_ref, db_ref, dcc_ref):
        first = (pl.program_id(0) == 0) & (pl.program_id(1) == 0)
        cc = cc_ref[...]
        sg = jax.nn.sigmoid(cc)
        dm = dm_ref[0]
        dw_ref[0] = _dot(cc * sg, dm, TN)
        db_ref[0] = dm[0:1, :] + dm[1:2, :]

        @pl.when(first)
        def _():
            dcc_ref[...] = jnp.zeros_like(dcc_ref)

        dcc_ref[...] += _dot(dm, w_ref[0], NT) * (sg * (1.0 + cc * (1.0 - sg)))

    return _call(
        body, "mod_bwd", (NL, 6 * D // MOD_CT),
        [pl.BlockSpec((8, D), lambda l, j: (0, 0)), pl.BlockSpec((1, D, MOD_CT), lambda l, j: (l, 0, j)),
         pl.BlockSpec((1, 8, MOD_CT), lambda l, j: (l, 0, j))],
        [pl.BlockSpec((1, D, MOD_CT), lambda l, j: (l, 0, j)), pl.BlockSpec((1, 1, MOD_CT), lambda l, j: (l, 0, j)),
         pl.BlockSpec((8, D), lambda l, j: (0, 0))],
        [_S((NL, D, 6 * D)), _S((NL, 1, 6 * D)), _S((8, D))])(cc8, w_ada, dmods)


def _inproj_fwd(X, mv, g, ws, tc, tt):
    T = X.shape[0]
    nw = len(ws)

    def body(x_ref, mv_ref, g_ref, *refs):
        w_refs, h_ref, p_refs = refs[:nw], refs[nw], refs[nw + 1:]
        i = pl.program_id(0)
        _, sh = _stream_rows(mv_ref, i, tt, tc, 0)
        _, sc = _stream_rows(mv_ref, i, tt, tc, 1)
        hb = _modulate(x_ref[...], g_ref[...], sh, sc).astype(BF16)
        h_ref[...] = hb
        for w_ref, p_ref in zip(w_refs, p_refs):
            p_ref[...] = jnp.dot(hb, w_ref[...], preferred_element_type=F32)

    return _call(
        body, "inproj_fwd", (T // tt,),
        [_rows(tt, D), _full((8, D)), _full((1, D))] + [_full(w.shape) for w in ws],
        [_rows(tt, D)] + [_rows(tt, w.shape[1]) for w in ws],
        [_S((T, D), BF16)] + [_S((T, w.shape[1])) for w in ws])(X, mv, g, *ws)


def _inproj_bwd(X, mv, g, ws, dps, dp_w, dres, tc, tt):
    T = X.shape[0]
    nw, nd = len(ws), len(dps)

    def body(x_ref, mv_ref, g_ref, dres_ref, *refs):
        w_refs, dp_refs = refs[:nw], refs[nw:nw + nd]
        dx_ref, dg_ref, dm_ref = refs[nw + nd:]
        i = pl.program_id(0)
        isc, sh = _stream_rows(mv_ref, i, tt, tc, 0)
        _, sc = _stream_rows(mv_ref, i, tt, tc, 1)
        dh = None
        for dp_ref, k in zip(dp_refs, dp_w):
            t = _dot(dp_ref[...], w_refs[k][...], NT)
            dh = t if dh is None else dh + t
        _, vjp = jax.vjp(_modulate, x_ref[...], g_ref[...], sh, sc)
        dx, dg, dsh, dsc = vjp(dh)
        dx_ref[...] = dres_ref[...] + dx

        @pl.when(i == 0)
        def _():
            dg_ref[...] = jnp.zeros_like(dg_ref)
            dm_ref[...] = jnp.zeros_like(dm_ref)

        dg_ref[...] += dg
        _acc_stream(dm_ref, 0, isc, dsh)
        _acc_stream(dm_ref, 1, isc, dsc)

    return _call(
        body, "inproj_bwd", (T // tt,),
        [_rows(tt, D), _full((8, D)), _full((1, D)), _rows(tt, D)] + [_full(w.shape) for w in ws]
        + [_rows(tt, dp.shape[1]) for dp in dps],
        [_rows(tt, D), _full((1, D)), _full((8, D))],
        [_S((T, D)), _S((1, D)), _S((8, D))])(X, mv, g, dres, *ws, *dps)


def _dw(A, B, tt):
    T, K = A.shape
    N = B.shape[1]
    tt = 3 * tt if T % (3 * tt) == 0 else tt
    tn = next(t for t in (1024, 512, 256, LANES) if N % t == 0)

    def body(a_ref, b_ref, o_ref):
        @pl.when(pl.program_id(1) == 0)
        def _():
            o_ref[...] = jnp.zeros_like(o_ref)

        o_ref[...] += _dot(a_ref[...], b_ref[...], TN)

    return _call(
        body, "dw", (N // tn, T // tt),
        [pl.BlockSpec((tt, K), lambda j, i: (i, 0)), pl.BlockSpec((tt, tn), lambda j, i: (i, j))],
        pl.BlockSpec((K, tn), lambda j, i: (0, j)), _S((K, N)))(A, B)


def _halo_specs(T, tt, cw, col):
    r8, nb8 = tt // 8, T // 8
    return [pl.BlockSpec((tt, cw), lambda j, i: (i, col(j))),
            pl.BlockSpec((8, cw), lambda j, i: (jnp.maximum(i * r8 - 1, 0), col(j))),
            pl.BlockSpec((8, cw), lambda j, i: (jnp.minimum((i + 1) * r8, nb8 - 1), col(j)))]


def _shifts(a, prev8, next8, i, tt, tc, T):
    r = _iota((tt, 1), 0)
    t = i * tt + r
    dn = jnp.where(r == 0, prev8[7:8, :], pltpu.roll(a, 1, 0))
    dn = jnp.where((t == 0) | (t == tc), 0.0, dn)
    up = jnp.where(r == tt - 1, next8[0:1, :], pltpu.roll(a, tt - 1, 0))
    up = jnp.where((t == T - 1) | (t == tc - 1), 0.0, up)
    return dn, up


def _dn_post(y, part):
    a = _silu(y)
    nrm = lax.rsqrt(jnp.sum(a * a, axis=-1, keepdims=True) + EPS)
    f = jnp.where(part == 0, nrm * (DH ** -0.5), jnp.where(part == 1, nrm, 1.0))
    return a * f


def _conv3(w_ref, dn, mid, up):
    return w_ref[0:1, :] * dn + w_ref[1:2, :] * mid + w_ref[2:3, :] * up


def _dnprep_fwd(pq, cw, tc, tt):
    T = pq.shape[0]

    def body(p_ref, pp_ref, pn_ref, w_ref, a_ref):
        part, i = pl.program_id(0), pl.program_id(1)
        p = p_ref[...]
        dn, up = _shifts(p, pp_ref[...], pn_ref[...], i, tt, tc, T)
        y = _conv3(w_ref, dn, p, up)
        for h in range(NH):
            a_ref[:, _hs(h)] = _dn_post(y[:, _hs(h)], part)

    return _call(
        body, "dnprep_fwd", (3, T // tt),
        _halo_specs(T, tt, DN, lambda j: j) + [pl.BlockSpec((3, DN), lambda j, i: (0, j))],
        pl.BlockSpec((tt, DN), lambda j, i: (i, j)), _S((T, 3 * DN)))(pq, pq, pq, cw)


def _dnprep_bwd_act(pq, cw, da_f, da_b, tc, tt):
    T = pq.shape[0]

    def body(p_ref, pp_ref, pn_ref, w_ref, df_ref, db_ref, dy_ref):
        part, i = pl.program_id(0), pl.program_id(1)
        p = p_ref[...]
        dn, up = _shifts(p, pp_ref[...], pn_ref[...], i, tt, tc, T)
        y = _conv3(w_ref, dn, p, up)
        for h in range(NH):
            _, vjp = jax.vjp(lambda yh: _dn_post(yh, part), y[:, _hs(h)])
            dy_ref[:, _hs(h)] = vjp(df_ref[:, _hs(h)] + db_ref[:, _hs(h)])[0]

    blk = pl.BlockSpec((tt, DN), lambda j, i: (i, j))
    return _call(
        body, "dnprep_bwd_act", (3, T // tt),
        _halo_specs(T, tt, DN, lambda j: j) + [pl.BlockSpec((3, DN), lambda j, i: (0, j)), blk, blk],
        blk, _S((T, 3 * DN)))(pq, pq, pq, cw, da_f, da_b)


def _conv_bwd(dy, p, cw, tc, tt):
    T, W = p.shape
    cb = DN

    def body(dy_ref, dyp_ref, dyn_ref, p_ref, pp_ref, pn_ref, w_ref, dp_ref, dw_ref):
        i = pl.program_id(1)
        dy, p_ = dy_ref[...], p_ref[...]
        ddn, dup = _shifts(dy, dyp_ref[...], dyn_ref[...], i, tt, tc, T)
        dp_ref[...] = _conv3(w_ref, dup, dy, ddn)
        pdn, pup = _shifts(p_, pp_ref[...], pn_ref[...], i, tt, tc, T)

        @pl.when(i == 0)
        def _():
            dw_ref[...] = jnp.zeros_like(dw_ref)

        dw_ref[0:1, :] += _colsum(dy * pdn)
        dw_ref[1:2, :] += _colsum(dy * p_)
        dw_ref[2:3, :] += _colsum(dy * pup)

    wspec = pl.BlockSpec((3, cb), lambda j, i: (0, j))
    return _call(
        body, "conv_bwd", (W // cb, T // tt),
        _halo_specs(T, tt, cb, lambda j: j) * 2 + [wspec],
        [pl.BlockSpec((tt, cb), lambda j, i: (i, j)), wspec], [_S((T, W)), _S((3, W))])(dy, dy, dy, p, p, p, cw)


def _sc_fwd(sx, sb, sc_, cw, tc, tt):
    T = sx.shape[0]

    def body(x_ref, xp_ref, xn_ref, c_ref, cp_ref, cn_ref, b_ref, w_ref, y_ref):
        i = pl.program_id(1)
        u = c_ref[...] * x_ref[...]
        dn, up = _shifts(u, cp_ref[...] * xp_ref[...], cn_ref[...] * xn_ref[...], i, tt, tc, T)
        y_ref[...] = b_ref[...] * _conv3(w_ref, dn, u, up)

    blk = pl.BlockSpec((tt, LANES), lambda j, i: (i, j))
    return _call(
        body, "sc_fwd", (PW // LANES, T // tt),
        _halo_specs(T, tt, LANES, lambda j: j) * 2 + [blk, pl.BlockSpec((3, LANES), lambda j, i: (0, j))],
        blk, _S((T, PW)))(sx, sx, sx, sc_, sc_, sc_, sb, cw)


def _sc_bwd(sx, sb, sc_, cw, dy, tc, tt):
    T = sx.shape[0]

    def body(x_ref, xp_ref, xn_ref, c_ref, cp_ref, cn_ref, b_ref, bp_ref, bn_ref, dy_ref, dyp_ref, dyn_ref, w_ref,
             dx_ref, db_ref, dc_ref, dw_ref):
        i = pl.program_id(1)
        x, c, dy_ = x_ref[...], c_ref[...], dy_ref[...]
        u = c * x
        udn, uup = _shifts(u, cp_ref[...] * xp_ref[...], cn_ref[...] * xn_ref[...], i, tt, tc, T)
        db_ref[...] = dy_ * _conv3(w_ref, udn, u, uup)
        e = dy_ * b_ref[...]
        edn, eup = _shifts(e, dyp_ref[...] * bp_ref[...], dyn_ref[...] * bn_ref[...], i, tt, tc, T)
        du = _conv3(w_ref, eup, e, edn)
        dx_ref[...] = du * c
        dc_ref[...] = du * x

        @pl.when(i == 0)
        def _():
            dw_ref[...] = jnp.zeros_like(dw_ref)

        dw_ref[0:1, :] += _colsum(e * udn)
        dw_ref[1:2, :] += _colsum(e * u)
        dw_ref[2:3, :] += _colsum(e * uup)

    blk = pl.BlockSpec((tt, LANES), lambda j, i: (i, j))
    wspec = pl.BlockSpec((3, LANES), lambda j, i: (0, j))
    return _call(
        body, "sc_bwd", (PW // LANES, T // tt),
        _halo_specs(T, tt, LANES, lambda j: j) * 4 + [wspec],
        [blk, blk, blk, wspec], [_S((T, PW))] * 3 + [_S((3, PW))])(
            sx, sx, sx, sc_, sc_, sc_, sb, sb, sb, dy, dy, dy, cw)


def _group_select(vals):
    g = _iota((1, PW), 1) // (PW // len(POOL_WIN))
    return jnp.where(g == 0, vals[0], jnp.where(g == 1, vals[1], jnp.where(g == 2, vals[2], vals[3])))


def _nested_box(get, mirror):
    acc, outs, pl_, ph_ = get(0), [], 0, 0
    for lo, hi in POOL_WIN:
        if mirror:
            lo, hi = hi, lo
        for k in range(pl_ + 1, lo + 1):
            acc = acc + get(-k)
        for k in range(ph_ + 1, hi + 1):
            acc = acc + get(k)
        pl_, ph_ = lo, hi
        outs.append(acc)
    return _group_select(outs)


def _box_tokens(a, n, mirror):
    idx = _iota((n, 1), 0)

    def get(k):
        if k == 0:
            return a
        return jnp.where((idx + k >= 0) & (idx + k < n), pltpu.roll(a, (-k) % n, 0), 0.0)

    return _nested_box(get, mirror)


def _inv_count(pos, n):
    return _group_select([1.0 / (jnp.minimum(pos + hi, n - 1) - jnp.maximum(pos - lo, 0) + 1).astype(F32)
                          for lo, hi in POOL_WIN])


def _pool_rows(ref, r, R, tc, mirror):
    def get(k):
        rr = r + k
        rc = jnp.clip(rr, 0, R - 1)
        v = ref[pl.ds(pl.multiple_of(tc + rc * GW, GW), GW), :]
        if mirror:
            v = v * _inv_count(jnp.full((1, PW), rc, jnp.int32), R)
        return jnp.where((rr >= 0) & (rr < R), v, 0.0)

    return _nested_box(get, mirror)


def _pool_fwd(u, pwbd, ps, tc):
    T = u.shape[0]
    R = (T - tc) // GW

    def body(u_ref, pw_ref, ps_ref, y_ref):
        pw, scale = pw_ref[...], ps_ref[...]
        uc = u_ref[0:tc, :]
        mc = _box_tokens(uc, tc, False) * _inv_count(_iota((tc, 1), 0), tc)
        y_ref[0:tc, :] = _dot(mc - uc, pw, NN) * scale
        inv_c = _inv_count(_iota((GW, 1), 0), GW)

        def row(r, carry):
            rs = _pool_rows(u_ref, r, R, tc, False) * _inv_count(jnp.full((1, PW), r, jnp.int32), R)
            m = _box_tokens(rs, GW, False) * inv_c
            sl = pl.ds(pl.multiple_of(tc + r * GW, GW), GW)
            y_ref[sl, :] = _dot(m - u_ref[sl, :], pw, NN) * scale
            return carry

        lax.fori_loop(0, R, row, 0)

    return pl.pallas_call(
        body, name="pool_fwd", out_shape=_S((T, PW)),
        compiler_params=pltpu.CompilerParams(vmem_limit_bytes=VMEM_MB << 20))(u, pwbd, ps)


def _pool_bwd(u, pwbd, ps, dy, tc):
    T = u.shape[0]
    R = (T - tc) // GW

    def body(u_ref, pw_ref, ps_ref, dy_ref, du_ref, dpw_ref, dps_ref, dd_ref):
        pw, scale = pw_ref[...], ps_ref[...]
        dpw_ref[...] = jnp.zeros_like(dpw_ref)
        dps_ref[...] = jnp.zeros_like(dps_ref)

        def back(d, dy_):
            dz = dy_ * scale
            dpw_ref[...] += _dot(d, dz, TN)
            dps_ref[...] += _colsum(dy_ * _dot(d, pw, NN))
            return _dot(dz, pw, NT)

        uc = u_ref[0:tc, :]
        inv_cc = _inv_count(_iota((tc, 1), 0), tc)
        ddc = back(_box_tokens(uc, tc, False) * inv_cc - uc, dy_ref[0:tc, :])
        du_ref[0:tc, :] = _box_tokens(ddc * inv_cc, tc, True) - ddc
        inv_c = _inv_count(_iota((GW, 1), 0), GW)

        def row1(r, carry):
            rs = _pool_rows(u_ref, r, R, tc, False) * _inv_count(jnp.full((1, PW), r, jnp.int32), R)
            m = _box_tokens(rs, GW, False) * inv_c
            sl = pl.ds(pl.multiple_of(tc + r * GW, GW), GW)
            dd_ref[sl, :] = back(m - u_ref[sl, :], dy_ref[sl, :])
            return carry

        lax.fori_loop(0, R, row1, 0)

        def row2(r, carry):
            t1 = _pool_rows(dd_ref, r, R, tc, True)
            sl = pl.ds(pl.multiple_of(tc + r * GW, GW), GW)
            du_ref[sl, :] = _box_tokens(t1 * inv_c, GW, True) - dd_ref[sl, :]
            return carry

        lax.fori_loop(0, R, row2, 0)

    return pl.pallas_call(
        body, name="pool_bwd", out_shape=[_S((T, PW)), _S((PW, PW)), _S((1, PW))],
        scratch_shapes=[pltpu.VMEM((T, PW), F32)],
        compiler_params=pltpu.CompilerParams(vmem_limit_bytes=VMEM_MB << 20))(u, pwbd, ps, dy)


def _scan_consts():
    i = np.arange(CH)
    lower = (i[:, None] >= i[None, :]).astype(np.float32)
    return jnp.asarray(np.stack([lower, lower.T])), jnp.asarray(np.stack([lower.T, lower]))


def _gates(pab, al, dtb, csum):
    sp_in = pab + dtb
    sp = jnp.maximum(sp_in, 0.0) + jnp.log(1.0 + jnp.exp(-jnp.abs(sp_in)))
    nexp = -jnp.exp(al)
    gm = nexp * sp
    return gm, jax.nn.sigmoid(pab), _dot(csum, gm, NN, hi=True), sp_in, nexp


def _lane_col(m, j):
    return jnp.sum(jnp.where(_iota(m.shape, 1) == j, m, 0.0), axis=1, keepdims=True)


def _hs(h):
    return slice(h * DH, (h + 1) * DH)


HS = NH * CH
X3 = "x3"


def _stack(x, base=0):
    return jnp.concatenate([x[:, base + h * DH:base + (h + 1) * DH] for h in range(NH)], axis=0)


def _heads(st):
    return [st[h * CH:(h + 1) * CH] for h in range(NH)]


def _rowsum(a):
    return jnp.sum(a, axis=1, keepdims=True)


def _row_of(col):
    e0 = (_iota((8, LANES), 1) == 0).astype(F32)
    return _dot(e0, jnp.broadcast_to(col, (HS, LANES)), NT, hi=True)[0:1, :]


def _inverses(nms):
    eye = (_iota((HS, HS), 0) == _iota((HS, HS), 1)).astype(F32)
    x0s, mps = [eye + nm for nm in nms], list(nms)
    for _ in range(5):
        mps = [_dot(mp, mp, NN) for mp in mps]
        x0s = [x0 + _dot(x0, mp, NN) for x0, mp in zip(x0s, mps)]
    rs = [eye - _dot(eye - nm, x0, NN, hi=X3) for nm, x0 in zip(nms, x0s)]
    return [x0 + _dot(x0, r, NN) for x0, r in zip(x0s, rs)]


def _dn_chunk_pre(qkv, pab, al, dtb, csum_d, d):
    gm, bm, gcm, sp_in, nexp = _gates(pab, al, dtb, csum_d)
    gc = jnp.concatenate([_lane_col(gcm, d * NH + h) for h in range(NH)], axis=0)
    beta = jnp.concatenate([_lane_col(bm, 8 + d * NH + h) for h in range(NH)], axis=0)
    q, k, v = _stack(qkv, 0), _stack(qkv, DN), _stack(qkv, 2 * DN)
    ii, jj = _iota((HS, HS), 0), _iota((HS, HS), 1)
    sh = CH.bit_length() - 1
    same = (ii >> sh) == (jj >> sh)
    incl = same & ((ii >= jj) if d == 0 else (ii <= jj))
    strict = same & ((ii > jj) if d == 0 else (ii < jj))
    Di = jnp.where(incl, jnp.exp(jnp.where(incl, gc - _row_of(gc), 0.0)), 0.0)
    Ds = jnp.where(strict, Di, 0.0)
    kb = k * beta
    kk = _dot(kb, k, NT)
    return dict(q=q, k=k, v=v, beta=beta, gc=gc, gm=gm, bm=bm, sp_in=sp_in, nexp=nexp, Di=Di, Ds=Ds, strict=strict,
                last=CH - 1 if d == 0 else 0, kb=kb, kk=kk)


def _dn_chunk_post(c, tm):
    q, k, v, beta, gc, kb, last = (c[n] for n in ("q", "k", "v", "beta", "gc", "kb", "last"))
    E = jnp.exp(gc)
    gls = [gc[h * CH + last:h * CH + last + 1, :] for h in range(NH)]
    xs = jnp.exp(jnp.concatenate([jnp.broadcast_to(g, (CH, 1)) for g in gls], axis=0) - gc)
    qk = _dot(q, k, NT)
    return dict(c, tm=tm, E=E, gls=gls, xs=xs, qk=qk, u=_dot(tm, v * beta, NN, hi=X3), w=_dot(tm, kb * E, NN, hi=X3),
                ks=k * xs, qd=q * E, aqk=qk * c["Di"])


def _dn_chunks_bwd_math(cs, Ss, dS2s, dos, vns, dvns):
    I = range(len(cs))
    q, k, v, beta, tm, E, xs, kb, u, w = ([c[n] for c in cs] for n in ("q", "k", "v", "beta", "tm", "E", "xs", "kb", "u", "w"))
    doh, vnh, dvnh = ([_heads(a) for a in l] for l in (dos, vns, dvns))
    cat = lambda parts: jnp.concatenate(parts, axis=0)
    dqd = [cat([_dot(doh[i][h], Ss[i][h], NT) for h in range(NH)]) for i in I]
    dks = [cat([_dot(vnh[i][h], dS2s[i][h], NT) for h in range(NH)]) for i in I]
    dw = [-cat([_dot(dvnh[i][h], Ss[i][h], NT) for h in range(NH)]) for i in I]
    daqk = [_dot(dos[i], vns[i], NT) for i in I]
    drb = [_dot(tm[i], dvns[i], TN, hi=X3) for i in I]
    drw = [_dot(tm[i], dw[i], TN, hi=X3) for i in I]
    dA = [jnp.where(cs[i]["strict"], -(_dot(drb[i], u[i], NT) + _dot(drw[i], w[i], NT)), 0.0) for i in I]
    dM1 = [dA[i] * cs[i]["Ds"] for i in I]
    dM2 = [daqk[i] * cs[i]["Di"] for i in I]
    dkb = [_dot(dM1[i], k[i], NN) + drw[i] * E[i] for i in I]
    dk = [_dot(dM1[i], kb[i], TN) + _dot(dM2[i], q[i], TN) + dks[i] * xs[i] for i in I]
    dq = [_dot(dM2[i], k[i], NN) + dqd[i] * E[i] for i in I]
    on_diag = _iota((HS, HS), 0) == _iota((HS, HS), 1)
    out = []
    for i in I:
        G = dM1[i] * cs[i]["kk"] + dM2[i] * cs[i]["qk"]
        col = _rowsum(jnp.where(on_diag, jnp.broadcast_to(_colsum(G), (HS, HS)), 0.0))
        dxx = _rowsum(dks[i] * k[i]) * xs[i]
        dgc = _rowsum(G) - col + (_rowsum(dqd[i] * q[i]) + _rowsum(drw[i] * kb[i])) * E[i] - dxx
        at_last = _iota((CH, 1), 0) == cs[i]["last"]
        ends = []
        for h in range(NH):
            dgl = (_colsum(_rowsum(Ss[i][h] * dS2s[i][h])) * jnp.exp(cs[i]["gls"][h])
                   + _colsum(dxx[h * CH:(h + 1) * CH]))
            ends.append(jnp.where(at_last, dgl, 0.0))
        dbeta = _rowsum(drb[i] * v[i]) + _rowsum(dkb[i] * k[i])
        out.append((dq[i], dk[i] + dkb[i] * beta[i], drb[i] * beta[i], dgc + cat(ends), dbeta))
    return out


def _chunk_group(n, want=2):
    g = want
    while n % g:
        g //= 2
    return g


def _dn_chunks_fwd(qkv, pab, alr, dtr, rider=None):
    T = qkv.shape[0]
    n = T // CH
    G = _chunk_group(n, 4)
    csum, _ = _scan_consts()

    def body(q_ref, p_ref, cs_ref, al_ref, dt_ref, *outs):
        inst = [(g, d) for g in range(G) for d in range(2)]
        pres = [_dn_chunk_pre(q_ref[g * CH:(g + 1) * CH, :], p_ref[g * CH:(g + 1) * CH, :], al_ref[...], dt_ref[...],
                              cs_ref[d], d) for g, d in inst]
        tms = _inverses([-(p["kk"] * p["Ds"]) for p in pres])
        for (g, d), pre, tm in zip(inst, pres, tms):
            rows = slice(g * HS, (g + 1) * HS)
            u_ref, w_ref, ks_ref, qd_ref, aqk_ref, eg_ref, tm_ref = outs[7 * d:7 * d + 7]
            c = _dn_chunk_post(pre, tm)
            tm_ref[rows, :] = tm
            u_ref[rows, :] = c["u"]
            w_ref[rows, :] = c["w"].astype(BF16)
            ks_ref[rows, :] = c["ks"].astype(BF16)
            qd_ref[rows, :] = c["qd"].astype(BF16)
            aqk_ref[rows, :] = c["aqk"].astype(BF16)
            egs = [jnp.broadcast_to(jnp.exp(gl), (1, LANES)) for gl in c["gls"]]
            eg_ref[g * 8:(g + 1) * 8, :] = jnp.concatenate(egs + [jnp.zeros((8 - NH, LANES), F32)], axis=0)

    st = lambda w_: pl.BlockSpec((G * HS, w_), lambda i: (i, 0))
    one = [st(DH)] * 4 + [st(HS), pl.BlockSpec((G * 8, LANES), lambda i: (i, 0)), st(HS)]
    shp = [_S((n * HS, DH)), _S((n * HS, DH), BF16), _S((n * HS, DH), BF16), _S((n * HS, DH), BF16),
           _S((n * HS, HS), BF16), _S((n * 8, LANES)), _S((n * HS, HS))]
    res = _call(
        body, "dn_chunks_fwd", (n // G,),
        [_rows(G * CH, 3 * DN), _rows(G * CH, LANES), _full((2, CH, CH)), _full((1, LANES)), _full((1, LANES))],
        one * 2, shp * 2, rider=rider)(qkv, pab, csum, alr, dtr)
    outs, riding = (res, None) if rider is None else res
    parts = tuple(outs[:7]), tuple(outs[7:])
    return parts if rider is None else (parts, riding)


def _scan_order(n, ncx):
    return (lambda i: i), (lambda i: jnp.where(i < ncx, ncx - 1 - i, n - 1 - (i - ncx)))


def _scan_specs(order):
    st = lambda w_: pl.BlockSpec((HS, w_), lambda i: (order(i), 0))
    return dict(st=st(DH), aqk=st(HS), eg=pl.BlockSpec((8, LANES), lambda i: (order(i), 0)),
                tok=pl.BlockSpec((CH, DN), lambda i: (order(i), 0)), state=pl.BlockSpec((1, DN, DH), lambda i: (order(i), 0, 0)))


def _scan_fwd(parts, T, tc, rider=None):
    n = T // CH
    orders = _scan_order(n, tc // CH)

    def body(*refs):
        S_f, S_b = refs[-2:]

        @pl.when(pl.program_id(0) == 0)
        def _():
            S_f[...] = jnp.zeros_like(S_f)
            S_b[...] = jnp.zeros_like(S_b)

        for d, S in enumerate((S_f, S_b)):
            u_ref, w_ref, ks_ref, qd_ref, aqk_ref, eg_ref = refs[6 * d:6 * d + 6]
            o_ref, ss_ref, vn_ref = refs[12 + 3 * d:15 + 3 * d]
            ss_ref[0] = S[...]
            Sh = [S[_hs(h), :] for h in range(NH)]
            wh, ksh, qdh = _heads(w_ref[...]), _heads(ks_ref[...]), _heads(qd_ref[...])
            vn = u_ref[...] - jnp.concatenate([_dot(wh[h], Sh[h], NN) for h in range(NH)], axis=0)
            vn_ref[...] = vn
            av, vnh = _heads(_dot(aqk_ref[...], vn, NN)), _heads(vn)
            for h in range(NH):
                o_ref[:, _hs(h)] = _dot(qdh[h], Sh[h], NN) + av[h]
                S[_hs(h), :] = Sh[h] * eg_ref[h:h + 1, :] + _dot(ksh[h], vnh[h], TN)

    ins, outs, shp = [], [], []
    for d in range(2):
        sp = _scan_specs(orders[d])
        ins += [sp["st"]] * 4 + [sp["aqk"], sp["eg"]]
        outs += [sp["tok"], sp["state"], sp["st"]]
        shp += [_S((T, DN)), _S((n, DN, DH)), _S((n * HS, DH))]
    res = _call(body, "scan_fwd", (n,), ins, outs, shp, scratch=[pltpu.VMEM((DN, DH), F32), pltpu.VMEM((DN, DH), F32)],
                rider=rider)(*parts[0][:6], *parts[1][:6])
    res, riding = (res, None) if rider is None else res
    out = tuple(res[:3]), tuple(res[3:])
    return out if rider is None else (out, riding)


def _scan_bwd(do, parts, tc):
    T = do.shape[0]
    n = T // CH
    fwd_orders = _scan_order(n, tc // CH)
    orders = [lambda s, f=f: f(n - 1 - s) for f in fwd_orders]

    def body(*refs):
        dS_f, dS_b = refs[-2:]

        @pl.when(pl.program_id(0) == 0)
        def _():
            dS_f[...] = jnp.zeros_like(dS_f)
            dS_b[...] = jnp.zeros_like(dS_b)

        for d, dS in enumerate((dS_f, dS_b)):
            do_ref, w_ref, ks_ref, qd_ref, aqk_ref, eg_ref = refs[6 * d:6 * d + 6]
            dvn_ref, dss_ref = refs[12 + 2 * d:14 + 2 * d]
            dss_ref[0] = dS[...]
            dSh = [dS[_hs(h), :] for h in range(NH)]
            wh, ksh, qdh = _heads(w_ref[...]), _heads(ks_ref[...]), _heads(qd_ref[...])
            do_st = _stack(do_ref[...])
            dvn = _dot(aqk_ref[...], do_st, TN) + jnp.concatenate([_dot(ksh[h], dSh[h], NN) for h in range(NH)], axis=0)
            dvn_ref[...] = dvn
            doh, dvnh = _heads(do_st), _heads(dvn)
            for h in range(NH):
                dS[_hs(h), :] = _dot(qdh[h], doh[h], TN) + dSh[h] * eg_ref[h:h + 1, :] - _dot(wh[h], dvnh[h], TN)

    ins, outs, shp, args = [], [], [], []
    for d in range(2):
        sp = _scan_specs(orders[d])
        ins += [sp["tok"]] + [sp["st"]] * 3 + [sp["aqk"], sp["eg"]]
        outs += [sp["st"], sp["state"]]
        shp += [_S((n * HS, DH)), _S((n, DN, DH))]
        args += [do, *parts[d][1:6]]
    res = _call(body, "scan_bwd", (n,), ins, outs, shp,
                scratch=[pltpu.VMEM((DN, DH), F32), pltpu.VMEM((DN, DH), F32)])(*args)
    return tuple(res[:2]), tuple(res[2:])


def _dn_chunks_bwd(qkv, pab, alr, dtr, do, fwd, bwd, rider=None):
    T = qkv.shape[0]
    n = T // CH
    G = _chunk_group(n)
    csum, csum_t = _scan_consts()

    def body(q_ref, p_ref, do_ref, cs_ref, cst_ref, al_ref, dt_ref, *refs):
        dq_refs, dp_refs, acc_ref = refs[10:12], refs[12:14], refs[14]

        @pl.when(pl.program_id(0) == 0)
        def _():
            acc_ref[...] = jnp.zeros_like(acc_ref)

        lane = _iota((CH, LANES), 1)
        inst = [(g, d) for g in range(G) for d in range(2)]
        cs, Ss, dS2s, dos, vns, dvns = [], [], [], [], [], []
        for g, d in inst:
            tok, rows = slice(g * CH, (g + 1) * CH), slice(g * HS, (g + 1) * HS)
            vn_ref, dvn_ref, ss_ref, dss_ref, tm_ref = refs[5 * d:5 * d + 5]
            cs.append(_dn_chunk_post(
                _dn_chunk_pre(q_ref[tok, :], p_ref[tok, :], al_ref[...], dt_ref[...], cs_ref[d], d), tm_ref[rows, :]))
            Ss.append([ss_ref[g, _hs(h), :] for h in range(NH)])
            dS2s.append([dss_ref[g, _hs(h), :] for h in range(NH)])
            dos.append(_stack(do_ref[tok, :]))
            vns.append(vn_ref[rows, :])
            dvns.append(dvn_ref[rows, :])
        for (g, d), c, (dq, dk, dv, dgc, dbeta) in zip(inst, cs, _dn_chunks_bwd_math(cs, Ss, dS2s, dos, vns, dvns)):
            tok = slice(g * CH, (g + 1) * CH)
            dgcm = jnp.zeros((CH, LANES), F32)
            dbm = jnp.zeros((CH, LANES), F32)
            for h, (a, b_, c_, e, f) in enumerate(zip(*map(_heads, (dq, dk, dv, dgc, dbeta)))):
                dq_refs[d][tok, _hs(h)] = a
                dq_refs[d][tok, _hs(NH + h)] = b_
                dq_refs[d][tok, _hs(2 * NH + h)] = c_
                dgcm = jnp.where(lane == d * NH + h, e, dgcm)
                dbm = jnp.where(lane == 8 + d * NH + h, f, dbm)
            dgm = _dot(cst_ref[d], dgcm, NN, hi=True)
            dsp = dgm * c["nexp"] * jax.nn.sigmoid(c["sp_in"])
            dp_refs[d][tok, :] = dsp + dbm * c["bm"] * (1.0 - c["bm"])
            acc_ref[0:1, :] += _colsum(dgm * c["gm"])
            acc_ref[1:2, :] += _colsum(dsp)

    st = pl.BlockSpec((G * HS, DH), lambda i: (i, 0))
    state = pl.BlockSpec((G, DN, DH), lambda i: (i, 0, 0))
    return _call(
        body, "dn_chunks_bwd", (n // G,),
        [_rows(G * CH, 3 * DN), _rows(G * CH, LANES), _rows(G * CH, DN), _full((2, CH, CH)), _full((2, CH, CH)),
         _full((1, LANES)), _full((1, LANES))] + [st, st, state, state, pl.BlockSpec((G * HS, HS), lambda i: (i, 0))] * 2,
        [_rows(G * CH, 3 * DN)] * 2 + [_rows(G * CH, LANES)] * 2 + [_full((8, LANES))],
        [_S((T, 3 * DN))] * 2 + [_S((T, LANES))] * 2 + [_S((8, LANES))], rider=rider)(
            qkv, pab, do, csum, csum_t, alr, dtr, *fwd, *bwd)


def _head_out(o, z, g):
    on = o * lax.rsqrt(jnp.mean(o * o, axis=-1, keepdims=True) + EPS) * g
    return on * _silu(z)


def _mix_branches(of_ref, ob_ref, z_ref, yp_ref, ys_ref, pg_ref, gdn_ref, wa_ref, wb_ref, wc_ref):
    ons, ya = [], None
    for h in range(NH):
        on = _head_out(of_ref[:, _hs(h)] + ob_ref[:, _hs(h)], z_ref[:, _hs(h)], gdn_ref[...])
        t = _dot(on, wa_ref[_hs(h), :], NN)
        ya = t if ya is None else ya + t
        ons.append(on)
    ys = [ya, _dot(yp_ref[...], wb_ref[...], NN), _dot(ys_ref[...], wc_ref[...], NN)]
    sg = [jax.nn.sigmoid(pg_ref[:, k * D:(k + 1) * D]) for k in range(3)]
    return ons, ys, sg


def _mix_fwd(X, of, ob, z, yp, ys, pg, mv, gdn, wa, wb, wc, wo, tc, tt):
    T = X.shape[0]

    def body(x_ref, of_ref, ob_ref, z_ref, yp_ref, ys_ref, pg_ref, mv_ref, gdn_ref, wa_ref, wb_ref, wc_ref, wo_ref,
             x1_ref):
        _, yb, sg = _mix_branches(of_ref, ob_ref, z_ref, yp_ref, ys_ref, pg_ref, gdn_ref, wa_ref, wb_ref, wc_ref)
        mix = _dot(sg[0] * yb[0] + sg[1] * yb[1] + sg[2] * yb[2], wo_ref[...], NN)
        _, gate = _stream_rows(mv_ref, pl.program_id(0), tt, tc, 2)
        x1_ref[...] = x_ref[...] + gate * mix

    return _call(
        body, "mix_fwd", (T // tt,),
        [_rows(tt, D), _rows(tt, DN), _rows(tt, DN), _rows(tt, DN), _rows(tt, PW), _rows(tt, PW), _rows(tt, 3 * D),
         _full((8, D)), _full((1, DH)), _full(wa.shape), _full(wb.shape), _full(wc.shape), _full(wo.shape)],
        _rows(tt, D), _S((T, D)))(X, of, ob, z, yp, ys, pg, mv, gdn, wa, wb, wc, wo)


def _mix_bwd(dx1, of, ob, z, yp, ys, pg, mv, gdn, wa, wb, wc, wo, tc, tt):
    T = dx1.shape[0]

    def body(dx_ref, of_ref, ob_ref, z_ref, yp_ref, ys_ref, pg_ref, mv_ref, gdn_ref, wa_ref, wb_ref, wc_ref, wo_ref,
             do_ref, dz_ref, dyp_ref, dys_ref, dpg_ref, dwa_ref, dwb_ref, dwc_ref, dwo_ref, dgdn_ref, dm_ref):
        i = pl.program_id(0)

        @pl.when(i == 0)
        def _():
            for r in (dwa_ref, dwb_ref, dwc_ref, dwo_ref, dgdn_ref, dm_ref):
                r[...] = jnp.zeros_like(r)

        ons, yb, sg = _mix_branches(of_ref, ob_ref, z_ref, yp_ref, ys_ref, pg_ref, gdn_ref, wa_ref, wb_ref, wc_ref)
        ymix = sg[0] * yb[0] + sg[1] * yb[1] + sg[2] * yb[2]
        isc, gate = _stream_rows(mv_ref, i, tt, tc, 2)
        dx = dx_ref[...]
        dmix = dx * gate
        _acc_stream(dm_ref, 2, isc, dx * _dot(ymix, wo_ref[...], NN))
        dwo_ref[...] += _dot(ymix, dmix, TN)
        dymix = _dot(dmix, wo_ref[...], NT)
        dyb = []
        for k in range(3):
            dyb.append(dymix * sg[k])
            dpg_ref[:, k * D:(k + 1) * D] = dymix * yb[k] * sg[k] * (1.0 - sg[k])
        dwb_ref[...] += _dot(yp_ref[...], dyb[1], TN)
        dwc_ref[...] += _dot(ys_ref[...], dyb[2], TN)
        dyp_ref[...] = _dot(dyb[1], wb_ref[...], NT)
        dys_ref[...] = _dot(dyb[2], wc_ref[...], NT)
        dg = jnp.zeros((1, DH), F32)
        for h in range(NH):
            dwa_ref[_hs(h), :] += _dot(ons[h], dyb[0], TN)
            don = _dot(dyb[0], wa_ref[_hs(h), :], NT)
            _, vjp = jax.vjp(_head_out, of_ref[:, _hs(h)] + ob_ref[:, _hs(h)], z_ref[:, _hs(h)], gdn_ref[...])
            do_h, dz_h, dg_h = vjp(don)
            do_ref[:, _hs(h)] = do_h
            dz_ref[:, _hs(h)] = dz_h
            dg = dg + dg_h
        dgdn_ref[...] += dg

    return _call(
        body, "mix_bwd", (T // tt,),
        [_rows(tt, D), _rows(tt, DN), _rows(tt, DN), _rows(tt, DN), _rows(tt, PW), _rows(tt, PW), _rows(tt, 3 * D),
         _full((8, D)), _full((1, DH)), _full(wa.shape), _full(wb.shape), _full(wc.shape), _full(wo.shape)],
        [_rows(tt, DN), _rows(tt, DN), _rows(tt, PW), _rows(tt, PW), _rows(tt, 3 * D),
         _full(wa.shape), _full(wb.shape), _full(wc.shape), _full(wo.shape), _full((1, DH)), _full((8, D))],
        [_S((T, DN)), _S((T, DN)), _S((T, PW)), _S((T, PW)), _S((T, 3 * D)),
         _S(wa.shape), _S(wb.shape), _S(wc.shape), _S(wo.shape), _S((1, DH)), _S((8, D))])(
            dx1, of, ob, z, yp, ys, pg, mv, gdn, wa, wb, wc, wo)


def _ffn_fwd(X1, mv, g, wgu, wd, tc, tt):
    T = X1.shape[0]

    def body(x_ref, mv_ref, g_ref, wgu_ref, wd_ref, x2_ref, ff_ref):
        i = pl.program_id(0)
        _, sh = _stream_rows(mv_ref, i, tt, tc, 0)
        _, sc = _stream_rows(mv_ref, i, tt, tc, 1)
        _, gate = _stream_rows(mv_ref, i, tt, tc, 2)
        x = x_ref[...]
        gu = _dot(_modulate(x, g_ref[...], sh, sc), wgu_ref[...], NN)
        ff = _dot(_silu(gu[:, :DFF]) * gu[:, DFF:], wd_ref[...], NN)
        ff_ref[...] = ff
        x2_ref[...] = x + gate * ff

    return _call(
        body, "ffn_fwd", (T // tt,),
        [_rows(tt, D), _full((8, D)), _full((1, D)), _full(wgu.shape), _full(wd.shape)],
        [_rows(tt, D)] * 2, [_S((T, D))] * 2)(X1, mv, g, wgu, wd)


def _ffn_bwd(X1, ff, dx2, mv, g, wgu, wd, tc, tt, rider=None):
    T = X1.shape[0]

    def body(x_ref, ff_ref, dx2_ref, mv_ref, g_ref, wgu_ref, wd_ref, dx1_ref, h_ref, dgu_ref, act_ref, dff_ref, dg_ref,
             dm_ref):
        i = pl.program_id(0)
        isc, sh = _stream_rows(mv_ref, i, tt, tc, 0)
        _, sc = _stream_rows(mv_ref, i, tt, tc, 1)
        _, gate = _stream_rows(mv_ref, i, tt, tc, 2)
        x, dx2_ = x_ref[...], dx2_ref[...]
        h, vjp = jax.vjp(_modulate, x, g_ref[...], sh, sc)
        hb = h.astype(BF16)
        h_ref[...] = hb
        gu = jnp.dot(hb, wgu_ref[...], preferred_element_type=F32)
        ga, up = gu[:, :DFF], gu[:, DFF:]
        sg = jax.nn.sigmoid(ga)
        act = (ga * sg * up).astype(BF16)
        act_ref[...] = act
        dff = dx2_ * gate
        dff_ref[...] = dff.astype(BF16)
        dact = _dot(dff, wd_ref[...], NT)
        dga = (dact * up * (sg * (1.0 + ga * (1.0 - sg)))).astype(BF16)
        dup = (dact * ga * sg).astype(BF16)
        dgu_ref[:, :DFF] = dga
        dgu_ref[:, DFF:] = dup
        dh = _dot(dga, wgu_ref[:, :DFF], NT) + _dot(dup, wgu_ref[:, DFF:], NT)
        dx, dg, dsh, dsc = vjp(dh)
        dx1_ref[...] = dx2_ + dx

        @pl.when(i == 0)
        def _():
            dg_ref[...] = jnp.zeros_like(dg_ref)
            dm_ref[...] = jnp.zeros_like(dm_ref)

        dg_ref[...] += dg
        _acc_stream(dm_ref, 0, isc, dsh)
        _acc_stream(dm_ref, 1, isc, dsc)
        _acc_stream(dm_ref, 2, isc, dx2_ * ff_ref[...])

    return _call(
        body, "ffn_bwd", (T // tt,),
        [_rows(tt, D), _rows(tt, D), _rows(tt, D), _full((8, D)), _full((1, D)), _full(wgu.shape), _full(wd.shape)],
        [_rows(tt, D), _rows(tt, D), _rows(tt, 2 * DFF), _rows(tt, DFF), _rows(tt, D), _full((1, D)), _full((8, D))],
        [_S((T, D)), _S((T, D), BF16), _S((T, 2 * DFF), BF16), _S((T, DFF), BF16), _S((T, D), BF16),
         _S((1, D)), _S((8, D))], rider=rider)(X1, ff, dx2, mv, g, wgu, wd)


def _rms(x, g):
    return x * lax.rsqrt(jnp.mean(x * x, axis=-1, keepdims=True) + EPS) * g


def _loss_head(X2, tgt, gf, tc):
    T = X2.shape[0]

    def body(x_ref, t_ref, g_ref, dx_ref, loss_ref, dg_ref):
        i = pl.program_id(0)

        @pl.when(i == 0)
        def _():
            dx_ref[...] = jnp.zeros_like(dx_ref)
            loss_ref[...] = jnp.zeros_like(loss_ref)
            dg_ref[...] = jnp.zeros_like(dg_ref)

        @pl.when(i > 0)
        def _():
            y, vjp = jax.vjp(_rms, x_ref[...], g_ref[...])
            err = y - t_ref[...]
            dx, dg = vjp(err * (1.0 / D))
            dx_ref[...] = dx
            dg_ref[...] += dg
            loss_ref[...] += (0.5 / D) * jnp.sum(jnp.sum(err * err, axis=1, keepdims=True), axis=0, keepdims=True)

    return _call(
        body, "loss_head", (T // tc,),
        [_rows(tc, D), pl.BlockSpec((tc, D), lambda i: (jnp.maximum(i - 1, 0), 0)), _full((1, D))],
        [_rows(tc, D), _full((8, LANES)), _full((1, D))],
        [_S((T, D)), _S((8, LANES)), _S((1, D))])(X2, tgt, gf)


def _block_diag(pw):
    g, n = pw.shape[0], pw.shape[1]
    out = jnp.zeros((g * n, g * n), pw.dtype)
    for k in range(g):
        out = lax.dynamic_update_slice(out, pw[k], (k * n, k * n))
    return out


def _split_w_in(w):
    parts = [w[:, IN_BOUNDS[k]:IN_BOUNDS[k + 1]] for k in range(8)]
    parts[2] = jnp.pad(parts[2], ((0, 0), (0, LANES - 16)))
    return parts


def _mod_rows(mods_l, k0):
    rows = [mods_l[s, (k0 + k) * D:(k0 + k + 1) * D] for s in (0, 1) for k in range(3)]
    return jnp.stack(rows + [jnp.zeros((D,), F32)] * 2)


def _lane_row(v8):
    return jnp.pad(v8.reshape(1, 8), ((0, 0), (0, LANES - 8)))


LAYERED = ("w_in", "w_br_a", "w_br_b", "w_br_c", "w_o", "w_gu", "w_down")
LATE = ("w_br_a", "w_br_b", "w_br_c", "w_o", "w_gu", "w_down")


def _device_step(x, c, ctx, tgt, wts, tt, comm=None):
    tc = ctx.shape[0]
    X = jnp.concatenate([ctx, x], axis=0)
    cc8 = jnp.concatenate([wts["c_ctx"][None, :], c, jnp.zeros((6, D), F32)], axis=0)
    w_ada = wts["w_ada"].astype(BF16)
    mods = _mod_fwd(cc8, w_ada, wts["b_ada"].reshape(NL, 1, 6 * D))

    saved = []
    for l in range(NL):
        ws = [w.astype(BF16) for w in _split_w_in(wts["w_in"][l])]
        mv1, mv2 = _mod_rows(mods[l], 0), _mod_rows(mods[l], 3)
        g1, g2 = wts["norm1_g"][l][None, :], wts["norm2_g"][l][None, :]
        cw, scw = wts["dn_conv_w"][l], wts["sc_conv_w"][l]
        alr, dtr = _lane_row(wts["dn_a_log"][l]), _lane_row(wts["dn_dt_bias"][l])
        gdn = wts["dn_norm_g"][l][None, :]
        pwbd, ps = _block_diag(wts["pool_w"][l]), wts["pool_scale"][l][None, :]
        hb, pq, pz, pab, pp, sx, sb, sc_, pg = _inproj_fwd(X, mv1, g1, ws, tc, tt)
        qkv = _dnprep_fwd(pq, cw, tc, tt)
        if comm is not None and l == 0:
            parts, riding = _dn_chunks_fwd(qkv, pab, alr, dtr, rider=comm.late_weights_chips())
            ((of, ssf, vnf), (ob, ssb, vnb)), riding = _scan_fwd(parts, X.shape[0], tc,
                                                                 rider=comm.late_weights_pair(riding))
            wts = dict(wts, **comm.late_weights(riding))
        else:
            parts = _dn_chunks_fwd(qkv, pab, alr, dtr)
            (of, ssf, vnf), (ob, ssb, vnb) = _scan_fwd(parts, X.shape[0], tc)
        wbr = [wts[k][l].astype(BF16) for k in LATE]
        yp = _pool_fwd(pp, pwbd, ps, tc)
        ys = _sc_fwd(sx, sb, sc_, scw, tc, tt)
        X1 = _mix_fwd(X, of, ob, pz, yp, ys, pg, mv1, gdn, *wbr[:4], tc, tt)
        X2, ff = _ffn_fwd(X1, mv2, g2, wbr[4], wbr[5], tc, tt)
        saved.append(dict(X=X, X1=X1, ff=ff, ws=ws, wbr=wbr, mv1=mv1, mv2=mv2, g1=g1, g2=g2, cw=cw, scw=scw, alr=alr, dtr=dtr,
                          gdn=gdn, pwbd=pwbd, ps=ps, hb=hb, pq=pq, pz=pz, pab=pab, pp=pp, sx=sx, sb=sb, sc=sc_, pg=pg,
                          qkv=qkv, of=of, ob=ob, ssf=ssf, ssb=ssb, vnf=vnf, vnb=vnb, parts=parts, yp=yp, ys=ys))
        X = X2

    dX, loss, dgf = _loss_head(X, tgt, wts["final_norm_g"][None, :], tc)

    gl = {k: [None] * NL for k in ("w_in", "norm1_g", "norm2_g", "dn_conv_w", "dn_a_log", "dn_dt_bias", "dn_norm_g",
                                   "pool_w", "pool_scale", "sc_conv_w", "w_br_a", "w_br_b", "w_br_c", "w_o", "w_gu",
                                   "w_down")}
    dmods = [None] * NL
    early = None
    for l in reversed(range(NL)):
        s = saved[l]
        hide = comm is not None and l == 0
        res = _ffn_bwd(s["X1"], s["ff"], dX, s["mv2"], s["g2"], s["wbr"][4], s["wbr"][5], tc, tt,
                       rider=comm.grad_pair_rider([gl[k][1] for k in LAYERED]) if hide else None)
        if hide:
            res, got = res
            chip_rider = comm.grad_chip_rider(got)
        dx1, h2, dgu, act, dff, dg2, dm2 = res
        gl["w_gu"][l] = _dw(h2, dgu, tt)
        gl["w_down"][l] = _dw(act, dff, tt)
        do, dz, dyp, dys, dpg, dwa, dwb, dwc, dwo, dgdn, dmg = _mix_bwd(
            dx1, s["of"], s["ob"], s["pz"], s["yp"], s["ys"], s["pg"], s["mv1"], s["gdn"], *s["wbr"][:4], tc, tt)
        dpp, dpw, dps = _pool_bwd(s["pp"], s["pwbd"], s["ps"], dyp, tc)
        dsx, dsb, dsc, dscw = _sc_bwd(s["sx"], s["sb"], s["sc"], s["scw"], dys, tc, tt)
        (dvnf, dssf), (dvnb, dssb) = _scan_bwd(do, s["parts"], tc)
        res = _dn_chunks_bwd(s["qkv"], s["pab"], s["alr"], s["dtr"], do,
                             (s["vnf"], dvnf, s["ssf"], dssf, s["parts"][0][6]),
                             (s["vnb"], dvnb, s["ssb"], dssb, s["parts"][1][6]), rider=chip_rider if hide else None)
        if hide:
            res, early = res
            early = comm.grad_chip_done(early)
        dqf, dqb, dpf, dpb, gacc = res
        dy = _dnprep_bwd_act(s["pq"], s["cw"], dqf, dqb, tc, tt)
        dpq, dcw = _conv_bwd(dy, s["pq"], s["cw"], tc, tt)
        dps_ = [dpq, dz, dpf, dpb, dpp, dsx, dsb, dsc, dpg]
        dp_w = [0, 1, 2, 2, 3, 4, 5, 6, 7]
        dX, dg1, dm1 = _inproj_bwd(s["X"], s["mv1"], s["g1"], s["ws"], dps_, dp_w, dx1, tc, tt)
        dws = [_dw(s["hb"], dpq, tt), _dw(s["hb"], dz, tt), _dw(s["hb"], dpf + dpb, tt)[:, :16], _dw(s["hb"], dpp, tt),
               _dw(s["hb"], dsx, tt), _dw(s["hb"], dsb, tt), _dw(s["hb"], dsc, tt), _dw(s["hb"], dpg, tt)]
        gl["w_in"][l] = jnp.concatenate(dws, axis=1)
        gl["norm1_g"][l], gl["norm2_g"][l] = dg1[0], dg2[0]
        gl["dn_conv_w"][l], gl["sc_conv_w"][l] = dcw, dscw
        gl["dn_a_log"][l], gl["dn_dt_bias"][l] = gacc[0, :8].reshape(2, NH), gacc[1, :8].reshape(2, NH)
        gl["dn_norm_g"][l] = dgdn[0]
        gl["pool_w"][l] = jnp.stack([dpw[k * GW:(k + 1) * GW, k * GW:(k + 1) * GW] for k in range(4)])
        gl["pool_scale"][l] = dps[0]
        gl["w_br_a"][l], gl["w_br_b"][l], gl["w_br_c"][l], gl["w_o"][l] = dwa, dwb, dwc, dwo
        dm = dm1 + dmg
        row = lambda r: jnp.concatenate([dm[r], dm[r + 1], dm[r + 2], dm2[r], dm2[r + 1], dm2[r + 2]])
        dmods[l] = jnp.stack([row(0), row(3)] + [jnp.zeros((6 * D,), F32)] * 6)

    dwada, dbada, dcc = _mod_bwd(cc8, w_ada, jnp.stack(dmods))
    grads = {k: (v if k in LAYERED else jnp.stack(v)) for k, v in gl.items()}
    grads.update(w_ada=dwada, b_ada=dbada.reshape(NL, 6 * D), c_ctx=dcc[0], final_norm_g=dgf[0])
    return loss, dX[tc:], grads, early


def _me():
    return lax.axis_index("x"), lax.axis_index("y"), lax.axis_index("c")


def _dev_index(p):
    return 4 * p[0] + 2 * p[1] + p[2]


def _allgather(parts):
    n = len(parts)

    def body(*refs):
        ins, outs = refs[:n], refs[n:2 * n]
        send_sems, recv_sems = refs[2 * n:]
        x, y, c = _me()
        me, sibling = (x, y, c), (x, y, 1 - c)
        chips = [(1 - x, y), (x, 1 - y), (1 - x, 1 - y)]

        def copy(a, k, block, to, src=None):
            dst = outs[a].at[_dev_index(block)]
            return pltpu.make_async_remote_copy(
                src_ref=dst if src is None else src, dst_ref=dst, send_sem=send_sems.at[a, k], recv_sem=recv_sems.at[a, k],
                device_id=to, device_id_type=MESH_ID)

        first, passed = [], []
        for a in range(n):
            first.append(copy(a, 0, me, sibling, src=ins[a]))
            first += [copy(a, 1 + j, me, (*chip, c), src=ins[a]) for j, chip in enumerate(chips)]
        for cp in first:
            cp.start()
        for a in range(n):
            for j, chip in enumerate(chips):
                copy(a, 1 + j, (*chip, c), me).wait_recv()
                passed.append(copy(a, 4 + j, (*chip, c), sibling))
                passed[-1].start()
        for a in range(n):
            copy(a, 0, sibling, me).wait_recv()
            for j, chip in enumerate(chips):
                copy(a, 4 + j, (*chip, 1 - c), me).wait_recv()
        for cp in first + passed:
            cp.wait_send()

    outs = pl.pallas_call(
        body, name="allgather", in_specs=[HBM_SPEC] * n, out_specs=[HBM_SPEC] * n,
        out_shape=[_S((N_DEV,) + p.shape, p.dtype) for p in parts],
        scratch_shapes=[pltpu.SemaphoreType.DMA((n, 7)), pltpu.SemaphoreType.DMA((n, 7))],
    )(*parts)
    return [_with_own(o, p, _dev_index(_me())) for o, p in zip(outs, parts)]


def _with_own(gathered, own, index):
    return lax.dynamic_update_index_in_dim(gathered, own, index, 0)


def _broadcast_small(small):
    def body(in_ref, out_ref, send_sems, recv_sems, local_sem):
        x, y, c = _me()
        my = _dev_index((x, y, c))
        mine = pltpu.make_async_copy(in_ref, out_ref.at[my], local_sem)
        mine.start()
        remote = []
        for k in range(1, N_DEV):
            cp = pltpu.make_async_remote_copy(
                src_ref=in_ref, dst_ref=out_ref.at[my], send_sem=send_sems.at[k - 1], recv_sem=recv_sems.at[k - 1],
                device_id=(x ^ (k >> 2), y ^ ((k >> 1) & 1), c ^ (k & 1)), device_id_type=MESH_ID)
            cp.start()
            remote.append(cp)
        for cp in remote:
            cp.wait_recv()
        for cp in remote:
            cp.wait_send()
        mine.wait()

    return pl.pallas_call(
        body, name="small_exchange", in_specs=[HBM_SPEC], out_specs=HBM_SPEC,
        out_shape=_S((N_DEV,) + small.shape, small.dtype),
        scratch_shapes=[pltpu.SemaphoreType.DMA((7,)), pltpu.SemaphoreType.DMA((7,)), pltpu.SemaphoreType.DMA],
    )(small)


def _run_rider(rider, name):
    ni, no = len(rider.ins), len(rider.out_shapes)

    def body(*refs):
        riding = (refs[:ni], refs[ni:ni + no], refs[ni + no:])
        rider.start(*riding)
        rider.wait(*riding)

    return list(pl.pallas_call(
        body, name=name, in_specs=[HBM_SPEC] * ni, out_specs=[HBM_SPEC] * no, out_shape=rider.out_shapes,
        scratch_shapes=rider.sems)(*rider.ins))


def _chip_peers(x, y):
    return [(k - 1, (x ^ (k >> 1), y ^ (k & 1))) for k in range(1, N_CHIP)]


def _pair_exchange(g2s):
    n = len(g2s)

    def copies(ins, outs, sems):
        x, y, c = _me()
        return [pltpu.make_async_remote_copy(
            src_ref=ins[a].at[1 - c, j], dst_ref=outs[a].at[j], send_sem=sems[0].at[a, j], recv_sem=sems[1].at[a, j],
            device_id=(x, y, 1 - c), device_id_type=MESH_ID) for a in range(n) for j in range(N_CHIP)], []

    return _Rider(g2s, [_S(g.shape[1:], g.dtype) for g in g2s],
                  [pltpu.SemaphoreType.DMA((n, N_CHIP)), pltpu.SemaphoreType.DMA((n, N_CHIP))], copies)


def _my_chip():
    x, y, _ = _me()
    return 2 * x + y


def _chip_exchange(s4s):
    n = len(s4s)

    def copies(ins, outs, sems):
        x, y, c = _me()
        my = 2 * x + y
        return [pltpu.make_async_remote_copy(
            src_ref=ins[a].at[2 * px + py], dst_ref=outs[a].at[my], send_sem=sems[0].at[a, k], recv_sem=sems[1].at[a, k],
            device_id=(px, py, c), device_id_type=MESH_ID) for k, (px, py) in _chip_peers(x, y) for a in range(n)], []

    return _Rider(s4s, [_S(s.shape, s.dtype) for s in s4s],
                  [pltpu.SemaphoreType.DMA((n, N_CHIP - 1)), pltpu.SemaphoreType.DMA((n, N_CHIP - 1))], copies)


def _chip_exchange_done(s4s, recvs):
    my = _my_chip()
    return [_with_own(r, lax.dynamic_index_in_dim(s, my, 0, keepdims=False), my) for s, r in zip(s4s, recvs)]


def _chip_gather(pack):
    def copies(ins, outs, sems):
        x, y, c = _me()
        return [pltpu.make_async_remote_copy(
            src_ref=ins[0], dst_ref=outs[0].at[2 * x + y], send_sem=sems[0].at[k], recv_sem=sems[1].at[k],
            device_id=(px, py, c), device_id_type=MESH_ID) for k, (px, py) in _chip_peers(x, y)], []

    return _Rider([pack], [_S((N_CHIP,) + pack.shape, pack.dtype)],
                  [pltpu.SemaphoreType.DMA((N_CHIP - 1,)), pltpu.SemaphoreType.DMA((N_CHIP - 1,))], copies)


def _pair_gather(chips):
    def copies(ins, outs, sems):
        x, y, c = _me()
        return [pltpu.make_async_remote_copy(
            src_ref=ins[0].at[j], dst_ref=outs[0].at[j], send_sem=sems[0].at[j], recv_sem=sems[1].at[j],
            device_id=(x, y, 1 - c), device_id_type=MESH_ID) for j in range(N_CHIP)], []

    return _Rider([chips], [_S(chips.shape, chips.dtype)],
                  [pltpu.SemaphoreType.DMA((N_CHIP,)), pltpu.SemaphoreType.DMA((N_CHIP,))], copies)


def _shard_rows(r):
    return 256 if r % 256 == 0 else r


def _pair_sum(g2, got):
    _, nc, L, R, C = g2.shape
    tr = _shard_rows(R)

    def body(a_ref, b_ref, o_ref):
        o_ref[...] = (a_ref[0] + b_ref[...]).astype(BF16)

    blk = pl.BlockSpec((1, 1, tr, C), lambda j, l, i: (j, l, i, 0))
    return _call(
        body, "pair_sum", (nc, L, R // tr),
        [pl.BlockSpec((1, 1, 1, tr, C), lambda j, l, i: (lax.axis_index("c"), j, l, i, 0)), blk], blk,
        _S(got.shape, BF16))(g2, got)


def _adam(w, g, m, v):
    m2 = ADAM_B1 * m + (1.0 - ADAM_B1) * g
    v2 = ADAM_B2 * v + (1.0 - ADAM_B2) * (g * g)
    m_hat = m2 / (1.0 - ADAM_B1 ** ADAM_STEP)
    v_hat = v2 / (1.0 - ADAM_B2 ** ADAM_STEP)
    return -ADAM_LR * (m_hat / (jnp.sqrt(v_hat) + ADAM_EPS) + ADAM_WD * w), m2, v2


def _sum_adam(recvs, w, m, v):
    L, R, C = w.shape
    tr = _shard_rows(R)
    nr = len(recvs)

    def body(*refs):
        w_ref, m_ref, v_ref, g_ref, d_ref, m2_ref, v2_ref = refs[nr:]
        g = None
        for li, r_ref in enumerate(refs[:nr]):
            s = r_ref[0, 0].astype(F32)
            for j in range(1, N_CHIP):
                s = s + r_ref[j, 0].astype(F32)
            g = s if g is None else jnp.where(pl.program_id(0) == li, s, g)
        g_ref[0] = g
        d_ref[0], m2_ref[0], v2_ref[0] = _adam(w_ref[0], g, m_ref[0], v_ref[0])

    blk = pl.BlockSpec((1, tr, C), lambda l, i: (l, i, 0))
    rspec = pl.BlockSpec((N_CHIP, 1, tr, C), (lambda l, i: (0, l, i, 0)) if nr == 1 else (lambda l, i: (0, 0, i, 0)))
    return _call(body, "sum_adam", (L, R // tr), [rspec] * nr + [blk, blk, blk], [blk] * 4, [_S(w.shape)] * 4)(
        *recvs, w, m, v)


def _sum_small(recv):
    def body(r_ref, o_ref):
        g = r_ref[0]
        for k in range(1, N_DEV):
            g = g + r_ref[k]
        o_ref[...] = g

    return pl.pallas_call(body, name="sum_small", out_shape=_S(recv.shape[1:]))(recv)


def _adam_small(w, g, m, v):
    def body(w_ref, g_ref, m_ref, v_ref, d_ref, m2_ref, v2_ref):
        d_ref[...], m2_ref[...], v2_ref[...] = _adam(w_ref[...], g_ref[...], m_ref[...], v_ref[...])

    return pl.pallas_call(body, name="adam_small", out_shape=[_S(w.shape)] * 3)(w, g, m, v)


def _pack(arrs, dtype, row_mult):
    parts, offs, r = [], [], 0
    for a in arrs:
        nr = -(-a.size // LANES)
        parts.append(jnp.pad(a.reshape(-1).astype(dtype), (0, nr * LANES - a.size)))
        offs.append(r)
        r += nr
    pad = (-r) % row_mult
    if pad:
        parts.append(jnp.zeros((pad * LANES,), dtype))
    return jnp.concatenate(parts).reshape(r + pad, LANES), offs


def _unpack(packed, offs, shapes, lead=()):
    out = []
    for off, shp in zip(offs, shapes):
        size = int(np.prod(shp))
        nr = -(-size // LANES)
        flat = packed[..., off:off + nr, :].reshape(lead + (nr * LANES,))
        out.append(flat[..., :size].reshape(lead + tuple(shp)))
    return out


BIG = (("w_ada", 2), ("w_in", 2), ("w_br_a", 2), ("w_br_b", 2), ("w_br_c", 2), ("w_o", 1), ("w_gu", 2), ("w_down", 1))
CONV = ("dn_conv_w", "sc_conv_w")
REPL = ("c_ctx", "b_ada", "norm1_g", "norm2_g", "dn_a_log", "dn_dt_bias", "dn_norm_g", "pool_w", "pool_scale",
        "final_norm_g")
WEIGHTS = ("c_ctx", "w_ada", "b_ada", "norm1_g", "norm2_g", "w_in", "dn_conv_w", "dn_a_log", "dn_dt_bias", "dn_norm_g",
           "pool_w", "pool_scale", "sc_conv_w", "w_br_a", "w_br_b", "w_br_c", "w_o", "w_gu", "w_down", "final_norm_g")
TOKEN_TILE = 256


def _join(blocks, axis):
    nd, nl, r, c = blocks.shape
    if axis == 2:
        return blocks.transpose(1, 2, 0, 3).reshape(nl, r, nd * c)
    return blocks.transpose(1, 0, 2, 3).reshape(nl, nd * r, c)


def _split(full, axis):
    nl, r, c = full.shape
    if axis == 2:
        return full.reshape(nl, r, N_CHIP, 2, c // N_DEV).transpose(3, 2, 0, 1, 4)
    return full.reshape(nl, N_CHIP, 2, r // N_DEV, c).transpose(2, 1, 0, 3, 4)


class _Comm:
    def __init__(self, late_shards):
        self.shapes = [a.shape for a in late_shards]
        self.pack, self.offs = _pack(late_shards, BF16, BF16_ROWS)
        self.g2s = None

    def late_weights_chips(self):
        return _chip_gather(self.pack)

    def late_weights_pair(self, riding):
        self.chips = _with_own(riding[0], self.pack, _my_chip())
        return _pair_gather(self.chips)

    def late_weights(self, riding):
        on_south = lax.axis_index("c") == 0
        both = jnp.stack([jnp.where(on_south, self.chips, riding[0]), jnp.where(on_south, riding[0], self.chips)], axis=1)
        shards = _unpack(both.reshape((N_DEV,) + self.pack.shape), self.offs, self.shapes, (N_DEV,))
        return {k: _join(blocks, dict(BIG)[k]) for k, blocks in zip(LATE, shards)}

    def grad_pair_rider(self, layer_grads):
        self.g2s = [_split(g[None], dict(BIG)[k]) for k, g in zip(LAYERED, layer_grads)]
        return _pair_exchange(self.g2s)

    def grad_chip_rider(self, got):
        self.sums = [_pair_sum(g2, gt) for g2, gt in zip(self.g2s, got)]
        return _chip_exchange(self.sums)

    def grad_chip_done(self, riding):
        return _chip_exchange_done(self.sums, riding)


def kernel(x, c, ctx, c_ctx, w_ada, b_ada, norm1_g, norm2_g, w_in, dn_conv_w, dn_a_log, dn_dt_bias, dn_norm_g, pool_w, pool_scale, sc_conv_w, w_br_a, w_br_b, w_br_c, w_o, w_gu, w_down, final_norm_g, loss_target, m_c_ctx, m_w_ada, m_b_ada, m_norm1_g, m_norm2_g, m_w_in, m_dn_conv_w, m_dn_a_log, m_dn_dt_bias, m_dn_norm_g, m_pool_w, m_pool_scale, m_sc_conv_w, m_w_br_a, m_w_br_b, m_w_br_c, m_w_o, m_w_gu, m_w_down, m_final_norm_g, v_c_ctx, v_w_ada, v_b_ada, v_norm1_g, v_norm2_g, v_w_in, v_dn_conv_w, v_dn_a_log, v_dn_dt_bias, v_dn_norm_g, v_pool_w, v_pool_scale, v_sc_conv_w, v_w_br_a, v_w_br_b, v_w_br_c, v_w_o, v_w_gu, v_w_down, v_final_norm_g):
    loc = dict(c_ctx=c_ctx, w_ada=w_ada, b_ada=b_ada, norm1_g=norm1_g, norm2_g=norm2_g, w_in=w_in, dn_conv_w=dn_conv_w,
               dn_a_log=dn_a_log, dn_dt_bias=dn_dt_bias, dn_norm_g=dn_norm_g, pool_w=pool_w, pool_scale=pool_scale,
               sc_conv_w=sc_conv_w, w_br_a=w_br_a, w_br_b=w_br_b, w_br_c=w_br_c, w_o=w_o, w_gu=w_gu, w_down=w_down,
               final_norm_g=final_norm_g)
    mom_m = dict(c_ctx=m_c_ctx, w_ada=m_w_ada, b_ada=m_b_ada, norm1_g=m_norm1_g, norm2_g=m_norm2_g, w_in=m_w_in,
                 dn_conv_w=m_dn_conv_w, dn_a_log=m_dn_a_log, dn_dt_bias=m_dn_dt_bias, dn_norm_g=m_dn_norm_g,
                 pool_w=m_pool_w, pool_scale=m_pool_scale, sc_conv_w=m_sc_conv_w, w_br_a=m_w_br_a, w_br_b=m_w_br_b,
                 w_br_c=m_w_br_c, w_o=m_w_o, w_gu=m_w_gu, w_down=m_w_down, final_norm_g=m_final_norm_g)
    mom_v = dict(c_ctx=v_c_ctx, w_ada=v_w_ada, b_ada=v_b_ada, norm1_g=v_norm1_g, norm2_g=v_norm2_g, w_in=v_w_in,
                 dn_conv_w=v_dn_conv_w, dn_a_log=v_dn_a_log, dn_dt_bias=v_dn_dt_bias, dn_norm_g=v_dn_norm_g,
                 pool_w=v_pool_w, pool_scale=v_pool_scale, sc_conv_w=v_sc_conv_w, w_br_a=v_w_br_a, w_br_b=v_w_br_b,
                 w_br_c=v_w_br_c, w_o=v_w_o, w_gu=v_w_gu, w_down=v_w_down, final_norm_g=v_final_norm_g)
    my = _dev_index(_me())

    axis_of = dict(BIG)
    first = [k for k, _ in BIG if k not in LATE]
    big_pack, big_offs = _pack([loc[k] for k in first], BF16, BF16_ROWS)
    conv_pack, conv_offs = _pack([loc[k] for k in CONV], F32, 8)
    big_all, conv_all = _allgather([big_pack, conv_pack])
    full = {k: loc[k] for k in REPL}
    for k, blocks in zip(first, _unpack(big_all, big_offs, [loc[k].shape for k in first], (N_DEV,))):
        full[k] = _join(blocks, axis_of[k])
    for k, blocks in zip(CONV, _unpack(conv_all, conv_offs, [loc[k].shape for k in CONV], (N_DEV,))):
        full[k] = _join(blocks, 2)

    loss8, grad_x, g, recv_l1 = _device_step(x[0], c, ctx[0], loss_target[0], full, TOKEN_TILE,
                                             comm=_Comm([loc[k] for k in LATE]))

    tail = [_split(g[k][0][None], axis_of[k]) for k in LAYERED] + [_split(g["w_ada"], 2)]
    got = _run_rider(_pair_exchange(tail), "pair_exchange")
    sums = [_pair_sum(a, b) for a, b in zip(tail, got)]
    recv_tail = _chip_exchange_done(sums, _run_rider(_chip_exchange(sums), "chip_exchange"))
    recv_big = {k: [recv_tail[i], recv_l1[i]] for i, k in enumerate(LAYERED)}
    recv_big["w_ada"] = [recv_tail[-1]]

    small_names = REPL + CONV
    small_pack, small_offs = _pack([g[k] for k in small_names] + [loss8[0:1, 0:1]], F32, 8)
    small_sum = _sum_small(_broadcast_small(small_pack))
    sums = _unpack(small_sum, small_offs, [g[k].shape for k in small_names] + [(1, 1)])
    grads = dict(zip(small_names, sums[:-1]))
    loss = sums[-1][0, 0]
    for k in CONV:
        w = loc[k].shape[2]
        grads[k] = lax.dynamic_slice_in_dim(grads[k], my * w, w, axis=2)

    delta, new_m, new_v = {}, {}, {}
    for k, _ in BIG:
        grads[k], delta[k], new_m[k], new_v[k] = _sum_adam(recv_big[k], loc[k], mom_m[k], mom_v[k])
    packs = [_pack([src[k] for k in small_names], F32, 8)[0] for src in (loc, grads, mom_m, mom_v)]
    _, offs = _pack([loc[k] for k in small_names], F32, 8)
    shapes = [loc[k].shape for k in small_names]
    for dst, packed in zip((delta, new_m, new_v), _adam_small(*packs)):
        dst.update(zip(small_names, _unpack(packed, offs, shapes)))

    return (loss, grad_x[None], *[grads[k] for k in WEIGHTS], *[delta[k] for k in WEIGHTS],
            *[new_m[k] for k in WEIGHTS], *[new_v[k] for k in WEIGHTS])
```

```python
import functools

import numpy as np
import jax
import jax.numpy as jnp
from jax import lax
from jax.experimental import pallas as pl
from jax.experimental.pallas import tpu as pltpu

F32 = jnp.float32
BF16 = jnp.bfloat16
HI = lax.Precision.HIGHEST

D = 1024
NL = 2
NH = 4
DH = 128
DN = NH * DH
CH = 64
GW = 64
PW = 256
DFF = 2816
EPS = 1e-6
N_DEV = 8
N_CHIP = 4
MESH_ID = pl.DeviceIdType.MESH
HBM_SPEC = pl.BlockSpec(memory_space=pltpu.HBM)
LANES = 128
BF16_ROWS = 16
VMEM_MB = 56

ADAM_LR, ADAM_B1, ADAM_B2, ADAM_EPS, ADAM_WD, ADAM_STEP = 0.001, 0.9, 0.999, 1e-08, 0.01, 10

IN_BOUNDS = (0, 1536, 2048, 2064, 2320, 2576, 2832, 3088, 6160)
IN_WIDTHS = (1536, 512, 128, 256, 256, 256, 256, 3072)
POOL_WIN = ((1, 0), (2, 1), (4, 3), (8, 7))

NN = ((1,), (0,))
NT = ((1,), (1,))
TN = ((0,), (0,))


def _dot(a, b, dims, hi=False):
    if hi:
        prec = lax.Precision.HIGH if hi == "x3" else HI
        return lax.dot_general(a, b, (dims, ((), ())), precision=prec, preferred_element_type=F32)
    return lax.dot_general(a.astype(BF16), b.astype(BF16), (dims, ((), ())), preferred_element_type=F32)


def _S(shape, dtype=F32):
    return jax.ShapeDtypeStruct(tuple(shape), dtype)


def _full(shape):
    nd = len(shape)
    return pl.BlockSpec(tuple(shape), lambda *_: (0,) * nd)


def _rows(tt, w):
    return pl.BlockSpec((tt, w), lambda i: (i, 0))


class _Rider:
    def __init__(self, ins, out_shapes, sems, copies):
        self.ins, self.out_shapes, self.sems, self.copies = list(ins), list(out_shapes), list(sems), copies

    def start(self, ins, outs, sems):
        remote, local = self.copies(ins, outs, sems)
        for cp in local + remote:
            cp.start()

    def wait(self, ins, outs, sems):
        remote, local = self.copies(ins, outs, sems)
        for cp in remote:
            cp.wait_recv()
        for cp in remote:
            cp.wait_send()
        for cp in local:
            cp.wait()


def _call(body, name, grid, in_specs, out_specs, out_shape, scratch=(), rider=None):
    params = pltpu.CompilerParams(dimension_semantics=("arbitrary",) * len(grid), vmem_limit_bytes=VMEM_MB << 20)
    if rider is None:
        return pl.pallas_call(body, name=name, grid=grid, in_specs=in_specs, out_specs=out_specs, out_shape=out_shape,
                              scratch_shapes=list(scratch), compiler_params=params)
    single = not isinstance(out_shape, (list, tuple))
    out_specs, out_shape = ([out_specs], [out_shape]) if single else (list(out_specs), list(out_shape))
    n_in, n_out, n_scr = len(in_specs), len(out_shape), len(scratch)
    r_in, r_out = len(rider.ins), len(rider.out_shapes)

    def hosted(*refs):
        ins, refs = refs[:n_in + r_in], refs[n_in + r_in:]
        outs, scr = refs[:n_out + r_out], refs[n_out + r_out:]
        riding = (ins[n_in:], outs[n_out:], scr[n_scr:])

        @pl.when(pl.program_id(0) == 0)
        def _():
            rider.start(*riding)

        body(*ins[:n_in], *outs[:n_out], *scr[:n_scr])

        @pl.when(pl.program_id(0) == grid[0] - 1)
        def _():
            rider.wait(*riding)

    call = pl.pallas_call(
        hosted, name=name, grid=grid, in_specs=list(in_specs) + [HBM_SPEC] * r_in,
        out_specs=out_specs + [HBM_SPEC] * r_out, out_shape=out_shape + rider.out_shapes,
        scratch_shapes=list(scratch) + rider.sems, compiler_params=params)

    def run(*args):
        res = call(*args, *rider.ins)
        own = res[:n_out]
        return (own[0] if single else own), list(res[n_out:])

    return run


def _iota(shape, axis):
    return lax.broadcasted_iota(jnp.int32, shape, axis)


def _colsum(a):
    return jnp.sum(a, axis=0, keepdims=True)


def _silu(x):
    return x * jax.nn.sigmoid(x)


def _modulate(x, g, sh, sc):
    xn = x * lax.rsqrt(jnp.mean(x * x, axis=-1, keepdims=True) + EPS)
    return (xn * g) * (1.0 + sc) + sh


def _stream_rows(mv_ref, i, tt, tc, k):
    isc = (i * tt + _iota((tt, 1), 0)) < tc
    return isc, jnp.where(isc, mv_ref[k:k + 1, :], mv_ref[3 + k:4 + k, :])


def _acc_stream(ref, k, isc, val):
    ref[k:k + 1, :] += _colsum(jnp.where(isc, val, 0.0))
    ref[3 + k:4 + k, :] += _colsum(jnp.where(isc, 0.0, val))


CC_ROWS = 16
CTX_ROW = 8


def _mod_cols(n):
    return 1536 if n % 1536 == 0 else n


def _mod_fwd(cc, w_ada, b_ada3):
    n = w_ada.shape[2]
    ct = _mod_cols(n)

    def body(cc_ref, w_ref, b_ref, o_ref):
        o_ref[0] = _dot(_silu(cc_ref[...]), w_ref[0], NN) + b_ref[0]

    return _call(
        body, "mod_fwd", (NL, n // ct),
        [pl.BlockSpec((CC_ROWS, D), lambda l, j: (0, 0)), pl.BlockSpec((1, D, ct), lambda l, j: (l, 0, j)),
         pl.BlockSpec((1, 1, ct), lambda l, j: (l, 0, j))],
        pl.BlockSpec((1, CC_ROWS, ct), lambda l, j: (l, 0, j)), _S((NL, CC_ROWS, n)))(cc, w_ada, b_ada3)


def _mod_bwd(cc, w_ada, dmods):
    n = w_ada.shape[2]
    ct = _mod_cols(n)

    def body(cc_ref, w_ref, dm_ref, dw_ref, dcc_ref):
        first = (pl.program_id(0) == 0) & (pl.program_id(1) == 0)
        cc_ = cc_ref[...]
        sg = jax.nn.sigmoid(cc_)
        dm = dm_ref[0]
        dw_ref[0] = _dot(cc_ * sg, dm, TN)

        @pl.when(first)
        def _():
            dcc_ref[...] = jnp.zeros_like(dcc_ref)

        dcc_ref[...] += _dot(dm, w_ref[0], NT) * (sg * (1.0 + cc_ * (1.0 - sg)))

    return _call(
        body, "mod_bwd", (NL, n // ct),
        [pl.BlockSpec((CC_ROWS, D), lambda l, j: (0, 0)), pl.BlockSpec((1, D, ct), lambda l, j: (l, 0, j)),
         pl.BlockSpec((1, CC_ROWS, ct), lambda l, j: (l, 0, j))],
        [pl.BlockSpec((1, D, ct), lambda l, j: (l, 0, j)), pl.BlockSpec((CC_ROWS, D), lambda l, j: (0, 0))],
        [_S((NL, D, n)), _S((CC_ROWS, D))])(cc, w_ada, dmods)


def _inproj_fwd(X, mv, g, ws, tc, tt):
    T = X.shape[0]
    nw = len(ws)

    def body(x_ref, mv_ref, g_ref, *refs):
        w_refs, h_ref, p_refs = refs[:nw], refs[nw], refs[nw + 1:]
        i = pl.program_id(0)
        _, sh = _stream_rows(mv_ref, i, tt, tc, 0)
        _, sc = _stream_rows(mv_ref, i, tt, tc, 1)
        hb = _modulate(x_ref[...], g_ref[...], sh, sc).astype(BF16)
        h_ref[...] = hb
        for w_ref, p_ref in zip(w_refs, p_refs):
            p_ref[...] = jnp.dot(hb, w_ref[...], preferred_element_type=F32)

    return _call(
        body, "inproj_fwd", (T // tt,),
        [_rows(tt, D), _full((8, D)), _full((1, D))] + [_full(w.shape) for w in ws],
        [_rows(tt, D)] + [_rows(tt, w.shape[1]) for w in ws],
        [_S((T, D), BF16)] + [_S((T, w.shape[1])) for w in ws])(X, mv, g, *ws)


def _inproj_bwd(X, mv, g, ws, dps, dp_w, dres, tc, tt):
    T = X.shape[0]
    nw, nd = len(ws), len(dps)

    def body(x_ref, mv_ref, g_ref, dres_ref, *refs):
        w_refs, dp_refs = refs[:nw], refs[nw:nw + nd]
        dx_ref, dg_ref, dm_ref = refs[nw + nd:]
        i = pl.program_id(0)
        isc, sh = _stream_rows(mv_ref, i, tt, tc, 0)
        _, sc = _stream_rows(mv_ref, i, tt, tc, 1)
        dh = None
        for dp_ref, k in zip(dp_refs, dp_w):
            t = _dot(dp_ref[...], w_refs[k][...], NT)
            dh = t if dh is None else dh + t
        _, vjp = jax.vjp(_modulate, x_ref[...], g_ref[...], sh, sc)
        dx, dg, dsh, dsc = vjp(dh)
        dx_ref[...] = dres_ref[...] + dx

        @pl.when(i == 0)
        def _():
            dg_ref[...] = jnp.zeros_like(dg_ref)
            dm_ref[...] = jnp.zeros_like(dm_ref)

        dg_ref[...] += dg
        _acc_stream(dm_ref, 0, isc, dsh)
        _acc_stream(dm_ref, 1, isc, dsc)

    return _call(
        body, "inproj_bwd", (T // tt,),
        [_rows(tt, D), _full((8, D)), _full((1, D)), _rows(tt, D)] + [_full(w.shape) for w in ws]
        + [_rows(tt, dp.shape[1]) for dp in dps],
        [_rows(tt, D), _full((1, D)), _full((8, D))],
        [_S((T, D)), _S((1, D)), _S((8, D))])(X, mv, g, dres, *ws, *dps)


def _dw(A, B, tt):
    T, K = A.shape
    N = B.shape[1]
    tt = 3 * tt if T % (3 * tt) == 0 else tt
    tn = next(t for t in (1024, 512, 256, LANES) if N % t == 0)

    def body(a_ref, b_ref, o_ref):
        @pl.when(pl.program_id(1) == 0)
        def _():
            o_ref[...] = jnp.zeros_like(o_ref)

        o_ref[...] += _dot(a_ref[...], b_ref[...], TN)

    return _call(
        body, "dw", (N // tn, T // tt),
        [pl.BlockSpec((tt, K), lambda j, i: (i, 0)), pl.BlockSpec((tt, tn), lambda j, i: (i, j))],
        pl.BlockSpec((K, tn), lambda j, i: (0, j)), _S((K, N)))(A, B)


def _halo_specs(T, tt, cw, col):
    r8, nb8 = tt // 8, T // 8
    return [pl.BlockSpec((tt, cw), lambda j, i: (i, col(j))),
            pl.BlockSpec((8, cw), lambda j, i: (jnp.maximum(i * r8 - 1, 0), col(j))),
            pl.BlockSpec((8, cw), lambda j, i: (jnp.minimum((i + 1) * r8, nb8 - 1), col(j)))]


def _shifts(a, prev8, next8, i, tt, tc, T):
    r = _iota((tt, 1), 0)
    t = i * tt + r
    dn = jnp.where(r == 0, prev8[7:8, :], pltpu.roll(a, 1, 0))
    dn = jnp.where((t == 0) | (t == tc), 0.0, dn)
    up = jnp.where(r == tt - 1, next8[0:1, :], pltpu.roll(a, tt - 1, 0))
    up = jnp.where((t == T - 1) | (t == tc - 1), 0.0, up)
    return dn, up


def _dn_post(y, part):
    a = _silu(y)
    nrm = lax.rsqrt(jnp.sum(a * a, axis=-1, keepdims=True) + EPS)
    f = jnp.where(part == 0, nrm * (DH ** -0.5), jnp.where(part == 1, nrm, 1.0))
    return a * f


def _conv3(w_ref, dn, mid, up):
    return w_ref[0:1, :] * dn + w_ref[1:2, :] * mid + w_ref[2:3, :] * up


def _dnprep_fwd(pq, cw, tc, tt):
    T = pq.shape[0]

    def body(p_ref, pp_ref, pn_ref, w_ref, a_ref):
        part, i = pl.program_id(0), pl.program_id(1)
        p = p_ref[...]
        dn, up = _shifts(p, pp_ref[...], pn_ref[...], i, tt, tc, T)
        y = _conv3(w_ref, dn, p, up)
        for h in range(NH):
            a_ref[:, _hs(h)] = _dn_post(y[:, _hs(h)], part)

    return _call(
        body, "dnprep_fwd", (3, T // tt),
        _halo_specs(T, tt, DN, lambda j: j) + [pl.BlockSpec((3, DN), lambda j, i: (0, j))],
        pl.BlockSpec((tt, DN), lambda j, i: (i, j)), _S((T, 3 * DN)))(pq, pq, pq, cw)


def _dnprep_bwd_act(pq, cw, da_f, da_b, tc, tt):
    T = pq.shape[0]

    def body(p_ref, pp_ref, pn_ref, w_ref, df_ref, db_ref, dy_ref):
        part, i = pl.program_id(0), pl.program_id(1)
        p = p_ref[...]
        dn, up = _shifts(p, pp_ref[...], pn_ref[...], i, tt, tc, T)
        y = _conv3(w_ref, dn, p, up)
        for h in range(NH):
            _, vjp = jax.vjp(lambda yh: _dn_post(yh, part), y[:, _hs(h)])
            dy_ref[:, _hs(h)] = vjp(df_ref[:, _hs(h)] + db_ref[:, _hs(h)])[0]

    blk = pl.BlockSpec((tt, DN), lambda j, i: (i, j))
    return _call(
        body, "dnprep_bwd_act", (3, T // tt),
        _halo_specs(T, tt, DN, lambda j: j) + [pl.BlockSpec((3, DN), lambda j, i: (0, j)), blk, blk],
        blk, _S((T, 3 * DN)))(pq, pq, pq, cw, da_f, da_b)


def _conv_bwd(dy, p, cw, tc, tt):
    T, W = p.shape
    cb = DN

    def body(dy_ref, dyp_ref, dyn_ref, p_ref, pp_ref, pn_ref, w_ref, dp_ref, dw_ref):
        i = pl.program_id(1)
        dy, p_ = dy_ref[...], p_ref[...]
        ddn, dup = _shifts(dy, dyp_ref[...], dyn_ref[...], i, tt, tc, T)
        dp_ref[...] = _conv3(w_ref, dup, dy, ddn)
        pdn, pup = _shifts(p_, pp_ref[...], pn_ref[...], i, tt, tc, T)

        @pl.when(i == 0)
        def _():
            dw_ref[...] = jnp.zeros_like(dw_ref)

        dw_ref[0:1, :] += _colsum(dy * pdn)
        dw_ref[1:2, :] += _colsum(dy * p_)
        dw_ref[2:3, :] += _colsum(dy * pup)

    wspec = pl.BlockSpec((3, cb), lambda j, i: (0, j))
    return _call(
        body, "conv_bwd", (W // cb, T // tt),
        _halo_specs(T, tt, cb, lambda j: j) * 2 + [wspec],
        [pl.BlockSpec((tt, cb), lambda j, i: (i, j)), wspec], [_S((T, W)), _S((3, W))])(dy, dy, dy, p, p, p, cw)


def _sc_fwd(sx, sb, sc_, cw, tc, tt):
    T = sx.shape[0]

    def body(x_ref, xp_ref, xn_ref, c_ref, cp_ref, cn_ref, b_ref, w_ref, y_ref):
        i = pl.program_id(1)
        u = c_ref[...] * x_ref[...]
        dn, up = _shifts(u, cp_ref[...] * xp_ref[...], cn_ref[...] * xn_ref[...], i, tt, tc, T)
        y_ref[...] = b_ref[...] * _conv3(w_ref, dn, u, up)

    blk = pl.BlockSpec((tt, LANES), lambda j, i: (i, j))
    return _call(
        body, "sc_fwd", (PW // LANES, T // tt),
        _halo_specs(T, tt, LANES, lambda j: j) * 2 + [blk, pl.BlockSpec((3, LANES), lambda j, i: (0, j))],
        blk, _S((T, PW)))(sx, sx, sx, sc_, sc_, sc_, sb, cw)


def _sc_bwd(sx, sb, sc_, cw, dy, tc, tt):
    T = sx.shape[0]

    def body(x_ref, xp_ref, xn_ref, c_ref, cp_ref, cn_ref, b_ref, bp_ref, bn_ref, dy_ref, dyp_ref, dyn_ref, w_ref,
             dx_ref, db_ref, dc_ref, dw_ref):
        i = pl.program_id(1)
        x, c, dy_ = x_ref[...], c_ref[...], dy_ref[...]
        u = c * x
        udn, uup = _shifts(u, cp_ref[...] * xp_ref[...], cn_ref[...] * xn_ref[...], i, tt, tc, T)
        db_ref[...] = dy_ * _conv3(w_ref, udn, u, uup)
        e = dy_ * b_ref[...]
        edn, eup = _shifts(e, dyp_ref[...] * bp_ref[...], dyn_ref[...] * bn_ref[...], i, tt, tc, T)
        du = _conv3(w_ref, eup, e, edn)
        dx_ref[...] = du * c
        dc_ref[...] = du * x

        @pl.when(i == 0)
        def _():
            dw_ref[...] = jnp.zeros_like(dw_ref)

        dw_ref[0:1, :] += _colsum(e * udn)
        dw_ref[1:2, :] += _colsum(e * u)
        dw_ref[2:3, :] += _colsum(e * uup)

    blk = pl.BlockSpec((tt, LANES), lambda j, i: (i, j))
    wspec = pl.BlockSpec((3, LANES), lambda j, i: (0, j))
    return _call(
        body, "sc_bwd", (PW // LANES, T // tt),
        _halo_specs(T, tt, LANES, lambda j: j) * 4 + [wspec],
        [blk, blk, blk, wspec], [_S((T, PW))] * 3 + [_S((3, PW))])(
            sx, sx, sx, sc_, sc_, sc_, sb, sb, sb, dy, dy, dy, cw)


def _group_select(vals):
    g = _iota((1, PW), 1) // (PW // len(POOL_WIN))
    return jnp.where(g == 0, vals[0], jnp.where(g == 1, vals[1], jnp.where(g == 2, vals[2], vals[3])))


def _nested_box(get, mirror):
    acc, outs, pl_, ph_ = get(0), [], 0, 0
    for lo, hi in POOL_WIN:
        if mirror:
            lo, hi = hi, lo
        for k in range(pl_ + 1, lo + 1):
            acc = acc + get(-k)
        for k in range(ph_ + 1, hi + 1):
            acc = acc + get(k)
        pl_, ph_ = lo, hi
        outs.append(acc)
    return _group_select(outs)


def _box_tokens(a, n, mirror):
    idx = _iota((n, 1), 0)

    def get(k):
        if k == 0:
            return a
        return jnp.where((idx + k >= 0) & (idx + k < n), pltpu.roll(a, (-k) % n, 0), 0.0)

    return _nested_box(get, mirror)


def _inv_count(pos, n):
    return _group_select([1.0 / (jnp.minimum(pos + hi, n - 1) - jnp.maximum(pos - lo, 0) + 1).astype(F32)
                          for lo, hi in POOL_WIN])


def _pool_rows(ref, r, R, tc, mirror):
    def get(k):
        rr = r + k
        rc = jnp.clip(rr, 0, R - 1)
        v = ref[pl.ds(pl.multiple_of(tc + rc * GW, GW), GW), :]
        if mirror:
            v = v * _inv_count(jnp.full((1, PW), rc, jnp.int32), R)
        return jnp.where((rr >= 0) & (rr < R), v, 0.0)

    return _nested_box(get, mirror)


def _pool_fwd(u, pwbd, ps, tc):
    T = u.shape[0]
    R = (T - tc) // GW

    def body(u_ref, pw_ref, ps_ref, y_ref):
        pw, scale = pw_ref[...], ps_ref[...]
        uc = u_ref[0:tc, :]
        mc = _box_tokens(uc, tc, False) * _inv_count(_iota((tc, 1), 0), tc)
        y_ref[0:tc, :] = _dot(mc - uc, pw, NN) * scale
        inv_c = _inv_count(_iota((GW, 1), 0), GW)

        def row(r, carry):
            rs = _pool_rows(u_ref, r, R, tc, False) * _inv_count(jnp.full((1, PW), r, jnp.int32), R)
            m = _box_tokens(rs, GW, False) * inv_c
            sl = pl.ds(pl.multiple_of(tc + r * GW, GW), GW)
            y_ref[sl, :] = _dot(m - u_ref[sl, :], pw, NN) * scale
            return carry

        lax.fori_loop(0, R, row, 0)

    return pl.pallas_call(
        body, name="pool_fwd", out_shape=_S((T, PW)),
        compiler_params=pltpu.CompilerParams(vmem_limit_bytes=VMEM_MB << 20))(u, pwbd, ps)


def _pool_bwd(u, pwbd, ps, dy, tc):
    T = u.shape[0]
    R = (T - tc) // GW

    def body(u_ref, pw_ref, ps_ref, dy_ref, du_ref, dpw_ref, dps_ref, dd_ref):
        pw, scale = pw_ref[...], ps_ref[...]
        dpw_ref[...] = jnp.zeros_like(dpw_ref)
        dps_ref[...] = jnp.zeros_like(dps_ref)

        def back(d, dy_):
            dz = dy_ * scale
            dpw_ref[...] += _dot(d, dz, TN)
            dps_ref[...] += _colsum(dy_ * _dot(d, pw, NN))
            return _dot(dz, pw, NT)

        uc = u_ref[0:tc, :]
        inv_cc = _inv_count(_iota((tc, 1), 0), tc)
        ddc = back(_box_tokens(uc, tc, False) * inv_cc - uc, dy_ref[0:tc, :])
        du_ref[0:tc, :] = _box_tokens(ddc * inv_cc, tc, True) - ddc
        inv_c = _inv_count(_iota((GW, 1), 0), GW)

        def row1(r, carry):
            rs = _pool_rows(u_ref, r, R, tc, False) * _inv_count(jnp.full((1, PW), r, jnp.int32), R)
            m = _box_tokens(rs, GW, False) * inv_c
            sl = pl.ds(pl.multiple_of(tc + r * GW, GW), GW)
            dd_ref[sl, :] = back(m - u_ref[sl, :], dy_ref[sl, :])
            return carry

        lax.fori_loop(0, R, row1, 0)

        def row2(r, carry):
            t1 = _pool_rows(dd_ref, r, R, tc, True)
            sl = pl.ds(pl.multiple_of(tc + r * GW, GW), GW)
            du_ref[sl, :] = _box_tokens(t1 * inv_c, GW, True) - dd_ref[sl, :]
            return carry

        lax.fori_loop(0, R, row2, 0)

    return pl.pallas_call(
        body, name="pool_bwd", out_shape=[_S((T, PW)), _S((PW, PW)), _S((1, PW))],
        scratch_shapes=[pltpu.VMEM((T, PW), F32)],
        compiler_params=pltpu.CompilerParams(vmem_limit_bytes=VMEM_MB << 20))(u, pwbd, ps, dy)


def _scan_consts():
    i = np.arange(CH)
    lower = (i[:, None] >= i[None, :]).astype(np.float32)
    return jnp.asarray(np.stack([lower, lower.T])), jnp.asarray(np.stack([lower.T, lower]))


def _gates(pab, al, dtb, csum):
    sp_in = pab + dtb
    sp = jnp.maximum(sp_in, 0.0) + jnp.log(1.0 + jnp.exp(-jnp.abs(sp_in)))
    nexp = -jnp.exp(al)
    gm = nexp * sp
    return gm, jax.nn.sigmoid(pab), _dot(csum, gm, NN, hi=True), sp_in, nexp


def _lane_col(m, j):
    return jnp.sum(jnp.where(_iota(m.shape, 1) == j, m, 0.0), axis=1, keepdims=True)


def _hs(h):
    return slice(h * DH, (h + 1) * DH)


HS = NH * CH
X3 = "x3"


def _stack(x, base=0):
    return jnp.concatenate([x[:, base + h * DH:base + (h + 1) * DH] for h in range(NH)], axis=0)


def _heads(st):
    return [st[h * CH:(h + 1) * CH] for h in range(NH)]


def _rowsum(a):
    return jnp.sum(a, axis=1, keepdims=True)


def _row_of(col):
    e0 = (_iota((8, LANES), 1) == 0).astype(F32)
    return _dot(e0, jnp.broadcast_to(col, (HS, LANES)), NT, hi=True)[0:1, :]


def _inverses(nms):
    eye = (_iota((HS, HS), 0) == _iota((HS, HS), 1)).astype(F32)
    x0s, mps = [eye + nm for nm in nms], list(nms)
    for _ in range(5):
        mps = [_dot(mp, mp, NN) for mp in mps]
        x0s = [x0 + _dot(x0, mp, NN) for x0, mp in zip(x0s, mps)]
    rs = [eye - _dot(eye - nm, x0, NN, hi=X3) for nm, x0 in zip(nms, x0s)]
    return [x0 + _dot(x0, r, NN) for x0, r in zip(x0s, rs)]


def _dn_chunk_pre(qkv, pab, al, dtb, csum_d, d):
    gm, bm, gcm, sp_in, nexp = _gates(pab, al, dtb, csum_d)
    gc = jnp.concatenate([_lane_col(gcm, d * NH + h) for h in range(NH)], axis=0)
    beta = jnp.concatenate([_lane_col(bm, 8 + d * NH + h) for h in range(NH)], axis=0)
    q, k, v = _stack(qkv, 0), _stack(qkv, DN), _stack(qkv, 2 * DN)
    ii, jj = _iota((HS, HS), 0), _iota((HS, HS), 1)
    sh = CH.bit_length() - 1
    same = (ii >> sh) == (jj >> sh)
    incl = same & ((ii >= jj) if d == 0 else (ii <= jj))
    strict = same & ((ii > jj) if d == 0 else (ii < jj))
    Di = jnp.where(incl, jnp.exp(jnp.where(incl, gc - _row_of(gc), 0.0)), 0.0)
    Ds = jnp.where(strict, Di, 0.0)
    kb = k * beta
    kk = _dot(kb, k, NT)
    return dict(q=q, k=k, v=v, beta=beta, gc=gc, gm=gm, bm=bm, sp_in=sp_in, nexp=nexp, Di=Di, Ds=Ds, strict=strict,
                last=CH - 1 if d == 0 else 0, kb=kb, kk=kk)


def _dn_chunk_post(c, tm):
    q, k, v, beta, gc, kb, last = (c[n] for n in ("q", "k", "v", "beta", "gc", "kb", "last"))
    E = jnp.exp(gc)
    gls = [gc[h * CH + last:h * CH + last + 1, :] for h in range(NH)]
    xs = jnp.exp(jnp.concatenate([jnp.broadcast_to(g, (CH, 1)) for g in gls], axis=0) - gc)
    qk = _dot(q, k, NT)
    return dict(c, tm=tm, E=E, gls=gls, xs=xs, qk=qk, u=_dot(tm, v * beta, NN, hi=X3), w=_dot(tm, kb * E, NN, hi=X3),
                ks=k * xs, qd=q * E, aqk=qk * c["Di"])


def _dn_chunks_bwd_math(cs, Ss, dS2s, dos, vns, dvns):
    I = range(len(cs))
    q, k, v, beta, tm, E, xs, kb, u, w = ([c[n] for c in cs] for n in ("q", "k", "v", "beta", "tm", "E", "xs", "kb", "u", "w"))
    doh, vnh, dvnh = ([_heads(a) for a in l] for l in (dos, vns, dvns))
    cat = lambda parts: jnp.concatenate(parts, axis=0)
    dqd = [cat([_dot(doh[i][h], Ss[i][h], NT) for h in range(NH)]) for i in I]
    dks = [cat([_dot(vnh[i][h], dS2s[i][h], NT) for h in range(NH)]) for i in I]
    dw = [-cat([_dot(dvnh[i][h], Ss[i][h], NT) for h in range(NH)]) for i in I]
    daqk = [_dot(dos[i], vns[i], NT) for i in I]
    drb = [_dot(tm[i], dvns[i], TN, hi=X3) for i in I]
    drw = [_dot(tm[i], dw[i], TN, hi=X3) for i in I]
    dA = [jnp.where(cs[i]["strict"], -(_dot(drb[i], u[i], NT) + _dot(drw[i], w[i], NT)), 0.0) for i in I]
    dM1 = [dA[i] * cs[i]["Ds"] for i in I]
    dM2 = [daqk[i] * cs[i]["Di"] for i in I]
    dkb = [_dot(dM1[i], k[i], NN) + drw[i] * E[i] for i in I]
    dk = [_dot(dM1[i], kb[i], TN) + _dot(dM2[i], q[i], TN) + dks[i] * xs[i] for i in I]
    dq = [_dot(dM2[i], k[i], NN) + dqd[i] * E[i] for i in I]
    on_diag = _iota((HS, HS), 0) == _iota((HS, HS), 1)
    out = []
    for i in I:
        G = dM1[i] * cs[i]["kk"] + dM2[i] * cs[i]["qk"]
        col = _rowsum(jnp.where(on_diag, jnp.broadcast_to(_colsum(G), (HS, HS)), 0.0))
        dxx = _rowsum(dks[i] * k[i]) * xs[i]
        dgc = _rowsum(G) - col + (_rowsum(dqd[i] * q[i]) + _rowsum(drw[i] * kb[i])) * E[i] - dxx
        at_last = _iota((CH, 1), 0) == cs[i]["last"]
        ends = []
        for h in range(NH):
            dgl = (_colsum(_rowsum(Ss[i][h] * dS2s[i][h])) * jnp.exp(cs[i]["gls"][h])
                   + _colsum(dxx[h * CH:(h + 1) * CH]))
            ends.append(jnp.where(at_last, dgl, 0.0))
        dbeta = _rowsum(drb[i] * v[i]) + _rowsum(dkb[i] * k[i])
        out.append((dq[i], dk[i] + dkb[i] * beta[i], drb[i] * beta[i], dgc + cat(ends), dbeta))
    return out


def _chunk_group(n, want=2):
    g = want
    while n % g:
        g //= 2
    return g


def _dn_chunks_fwd(qkv, pab, alr, dtr, rider=None):
    T = qkv.shape[0]
    n = T // CH
    G = _chunk_group(n, 4)
    csum, _ = _scan_consts()

    def body(q_ref, p_ref, cs_ref, al_ref, dt_ref, *outs):
        inst = [(g, d) for g in range(G) for d in range(2)]
        pres = [_dn_chunk_pre(q_ref[g * CH:(g + 1) * CH, :], p_ref[g * CH:(g + 1) * CH, :], al_ref[...], dt_ref[...],
                              cs_ref[d], d) for g, d in inst]
        tms = _inverses([-(p["kk"] * p["Ds"]) for p in pres])
        for (g, d), pre, tm in zip(inst, pres, tms):
            rows = slice(g * HS, (g + 1) * HS)
            u_ref, w_ref, ks_ref, qd_ref, aqk_ref, eg_ref, tm_ref = outs[7 * d:7 * d + 7]
            c = _dn_chunk_post(pre, tm)
            tm_ref[rows, :] = tm
            u_ref[rows, :] = c["u"]
            w_ref[rows, :] = c["w"].astype(BF16)
            ks_ref[rows, :] = c["ks"].astype(BF16)
            qd_ref[rows, :] = c["qd"].astype(BF16)
            aqk_ref[rows, :] = c["aqk"].astype(BF16)
            egs = [jnp.broadcast_to(jnp.exp(gl), (1, LANES)) for gl in c["gls"]]
            eg_ref[g * 8:(g + 1) * 8, :] = jnp.concatenate(egs + [jnp.zeros((8 - NH, LANES), F32)], axis=0)

    st = lambda w_: pl.BlockSpec((G * HS, w_), lambda i: (i, 0))
    one = [st(DH)] * 4 + [st(HS), pl.BlockSpec((G * 8, LANES), lambda i: (i, 0)), st(HS)]
    shp = [_S((n * HS, DH)), _S((n * HS, DH), BF16), _S((n * HS, DH), BF16), _S((n * HS, DH), BF16),
           _S((n * HS, HS), BF16), _S((n * 8, LANES)), _S((n * HS, HS))]
    res = _call(
        body, "dn_chunks_fwd", (n // G,),
        [_rows(G * CH, 3 * DN), _rows(G * CH, LANES), _full((2, CH, CH)), _full((1, LANES)), _full((1, LANES))],
        one * 2, shp * 2, rider=rider)(qkv, pab, csum, alr, dtr)
    outs, riding = (res, None) if rider is None else res
    parts = tuple(outs[:7]), tuple(outs[7:])
    return parts if rider is None else (parts, riding)


def _scan_order(n, ncx):
    return (lambda i: i), (lambda i: jnp.where(i < ncx, ncx - 1 - i, n - 1 - (i - ncx)))


def _scan_specs(order):
    st = lambda w_: pl.BlockSpec((HS, w_), lambda i: (order(i), 0))
    return dict(st=st(DH), aqk=st(HS), eg=pl.BlockSpec((8, LANES), lambda i: (order(i), 0)),
                tok=pl.BlockSpec((CH, DN), lambda i: (order(i), 0)), state=pl.BlockSpec((1, DN, DH), lambda i: (order(i), 0, 0)))


def _scan_fwd(parts, T, tc, rider=None):
    n = T // CH
    orders = _scan_order(n, tc // CH)

    def body(*refs):
        S_f, S_b = refs[-2:]

        @pl.when(pl.program_id(0) == 0)
        def _():
            S_f[...] = jnp.zeros_like(S_f)
            S_b[...] = jnp.zeros_like(S_b)

        for d, S in enumerate((S_f, S_b)):
            u_ref, w_ref, ks_ref, qd_ref, aqk_ref, eg_ref = refs[6 * d:6 * d + 6]
            o_ref, ss_ref, vn_ref = refs[12 + 3 * d:15 + 3 * d]
            ss_ref[0] = S[...]
            Sh = [S[_hs(h), :] for h in range(NH)]
            wh, ksh, qdh = _heads(w_ref[...]), _heads(ks_ref[...]), _heads(qd_ref[...])
            vn = u_ref[...] - jnp.concatenate([_dot(wh[h], Sh[h], NN) for h in range(NH)], axis=0)
            vn_ref[...] = vn
            av, vnh = _heads(_dot(aqk_ref[...], vn, NN)), _heads(vn)
            for h in range(NH):
                o_ref[:, _hs(h)] = _dot(qdh[h], Sh[h], NN) + av[h]
                S[_hs(h), :] = Sh[h] * eg_ref[h:h + 1, :] + _dot(ksh[h], vnh[h], TN)

    ins, outs, shp = [], [], []
    for d in range(2):
        sp = _scan_specs(orders[d])
        ins += [sp["st"]] * 4 + [sp["aqk"], sp["eg"]]
        outs += [sp["tok"], sp["state"], sp["st"]]
        shp += [_S((T, DN)), _S((n, DN, DH)), _S((n * HS, DH))]
    res = _call(body, "scan_fwd", (n,), ins, outs, shp, scratch=[pltpu.VMEM((DN, DH), F32), pltpu.VMEM((DN, DH), F32)],
                rider=rider)(*parts[0][:6], *parts[1][:6])
    res, riding = (res, None) if rider is None else res
    out = tuple(res[:3]), tuple(res[3:])
    return out if rider is None else (out, riding)


def _scan_bwd(do, parts, tc):
    T = do.shape[0]
    n = T // CH
    fwd_orders = _scan_order(n, tc // CH)
    orders = [lambda s, f=f: f(n - 1 - s) for f in fwd_orders]

    def body(*refs):
        dS_f, dS_b = refs[-2:]

        @pl.when(pl.program_id(0) == 0)
        def _():
            dS_f[...] = jnp.zeros_like(dS_f)
            dS_b[...] = jnp.zeros_like(dS_b)

        for d, dS in enumerate((dS_f, dS_b)):
            do_ref, w_ref, ks_ref, qd_ref, aqk_ref, eg_ref = refs[6 * d:6 * d + 6]
            dvn_ref, dss_ref = refs[12 + 2 * d:14 + 2 * d]
            dss_ref[0] = dS[...]
            dSh = [dS[_hs(h), :] for h in range(NH)]
            wh, ksh, qdh = _heads(w_ref[...]), _heads(ks_ref[...]), _heads(qd_ref[...])
            do_st = _stack(do_ref[...])
            dvn = _dot(aqk_ref[...], do_st, TN) + jnp.concatenate([_dot(ksh[h], dSh[h], NN) for h in range(NH)], axis=0)
            dvn_ref[...] = dvn
            doh, dvnh = _heads(do_st), _heads(dvn)
            for h in range(NH):
                dS[_hs(h), :] = _dot(qdh[h], doh[h], TN) + dSh[h] * eg_ref[h:h + 1, :] - _dot(wh[h], dvnh[h], TN)

    ins, outs, shp, args = [], [], [], []
    for d in range(2):
        sp = _scan_specs(orders[d])
        ins += [sp["tok"]] + [sp["st"]] * 3 + [sp["aqk"], sp["eg"]]
        outs += [sp["st"], sp["state"]]
        shp += [_S((n * HS, DH)), _S((n, DN, DH))]
        args += [do, *parts[d][1:6]]
    res = _call(body, "scan_bwd", (n,), ins, outs, shp,
                scratch=[pltpu.VMEM((DN, DH), F32), pltpu.VMEM((DN, DH), F32)])(*args)
    return tuple(res[:2]), tuple(res[2:])


def _dn_chunks_bwd(qkv, pab, alr, dtr, do, fwd, bwd, rider=None):
    T = qkv.shape[0]
    n = T // CH
    G = _chunk_group(n)
    csum, csum_t = _scan_consts()

    def body(q_ref, p_ref, do_ref, cs_ref, cst_ref, al_ref, dt_ref, *refs):
        dq_refs, dp_refs, acc_ref = refs[10:12], refs[12:14], refs[14]

        @pl.when(pl.program_id(0) == 0)
        def _():
            acc_ref[...] = jnp.zeros_like(acc_ref)

        lane = _iota((CH, LANES), 1)
        inst = [(g, d) for g in range(G) for d in range(2)]
        cs, Ss, dS2s, dos, vns, dvns = [], [], [], [], [], []
        for g, d in inst:
            tok, rows = slice(g * CH, (g + 1) * CH), slice(g * HS, (g + 1) * HS)
            vn_ref, dvn_ref, ss_ref, dss_ref, tm_ref = refs[5 * d:5 * d + 5]
            cs.append(_dn_chunk_post(
                _dn_chunk_pre(q_ref[tok, :], p_ref[tok, :], al_ref[...], dt_ref[...], cs_ref[d], d), tm_ref[rows, :]))
            Ss.append([ss_ref[g, _hs(h), :] for h in range(NH)])
            dS2s.append([dss_ref[g, _hs(h), :] for h in range(NH)])
            dos.append(_stack(do_ref[tok, :]))
            vns.append(vn_ref[rows, :])
            dvns.append(dvn_ref[rows, :])
        for (g, d), c, (dq, dk, dv, dgc, dbeta) in zip(inst, cs, _dn_chunks_bwd_math(cs, Ss, dS2s, dos, vns, dvns)):
            tok = slice(g * CH, (g + 1) * CH)
            dgcm = jnp.zeros((CH, LANES), F32)
            dbm = jnp.zeros((CH, LANES), F32)
            for h, (a, b_, c_, e, f) in enumerate(zip(*map(_heads, (dq, dk, dv, dgc, dbeta)))):
                dq_refs[d][tok, _hs(h)] = a
                dq_refs[d][tok, _hs(NH + h)] = b_
                dq_refs[d][tok, _hs(2 * NH + h)] = c_
                dgcm = jnp.where(lane == d * NH + h, e, dgcm)
                dbm = jnp.where(lane == 8 + d * NH + h, f, dbm)
            dgm = _dot(cst_ref[d], dgcm, NN, hi=True)
            dsp = dgm * c["nexp"] * jax.nn.sigmoid(c["sp_in"])
            dp_refs[d][tok, :] = dsp + dbm * c["bm"] * (1.0 - c["bm"])
            acc_ref[0:1, :] += _colsum(dgm * c["gm"])
            acc_ref[1:2, :] += _colsum(dsp)

    st = pl.BlockSpec((G * HS, DH), lambda i: (i, 0))
    state = pl.BlockSpec((G, DN, DH), lambda i: (i, 0, 0))
    return _call(
        body, "dn_chunks_bwd", (n // G,),
        [_rows(G * CH, 3 * DN), _rows(G * CH, LANES), _rows(G * CH, DN), _full((2, CH, CH)), _full((2, CH, CH)),
         _full((1, LANES)), _full((1, LANES))] + [st, st, state, state, pl.BlockSpec((G * HS, HS), lambda i: (i, 0))] * 2,
        [_rows(G * CH, 3 * DN)] * 2 + [_rows(G * CH, LANES)] * 2 + [_full((8, LANES))],
        [_S((T, 3 * DN))] * 2 + [_S((T, LANES))] * 2 + [_S((8, LANES))], rider=rider)(
            qkv, pab, do, csum, csum_t, alr, dtr, *fwd, *bwd)


def _head_out(o, z, g):
    on = o * lax.rsqrt(jnp.mean(o * o, axis=-1, keepdims=True) + EPS) * g
    return on * _silu(z)


def _mix_branches(of_ref, ob_ref, z_ref, yp_ref, ys_ref, pg_ref, gdn_ref, wa_ref, wb_ref, wc_ref):
    ons, ya = [], None
    for h in range(NH):
        on = _head_out(of_ref[:, _hs(h)] + ob_ref[:, _hs(h)], z_ref[:, _hs(h)], gdn_ref[...])
        t = _dot(on, wa_ref[_hs(h), :], NN)
        ya = t if ya is None else ya + t
        ons.append(on)
    ys = [ya, _dot(yp_ref[...], wb_ref[...], NN), _dot(ys_ref[...], wc_ref[...], NN)]
    sg = [jax.nn.sigmoid(pg_ref[:, k * D:(k + 1) * D]) for k in range(3)]
    return ons, ys, sg


def _mix_fwd(X, of, ob, z, yp, ys, pg, mv, gdn, wa, wb, wc, wo, tc, tt):
    T = X.shape[0]

    def body(x_ref, of_ref, ob_ref, z_ref, yp_ref, ys_ref, pg_ref, mv_ref, gdn_ref, wa_ref, wb_ref, wc_ref, wo_ref,
             x1_ref):
        _, yb, sg = _mix_branches(of_ref, ob_ref, z_ref, yp_ref, ys_ref, pg_ref, gdn_ref, wa_ref, wb_ref, wc_ref)
        mix = _dot(sg[0] * yb[0] + sg[1] * yb[1] + sg[2] * yb[2], wo_ref[...], NN)
        _, gate = _stream_rows(mv_ref, pl.program_id(0), tt, tc, 2)
        x1_ref[...] = x_ref[...] + gate * mix

    return _call(
        body, "mix_fwd", (T // tt,),
        [_rows(tt, D), _rows(tt, DN), _rows(tt, DN), _rows(tt, DN), _rows(tt, PW), _rows(tt, PW), _rows(tt, 3 * D),
         _full((8, D)), _full((1, DH)), _full(wa.shape), _full(wb.shape), _full(wc.shape), _full(wo.shape)],
        _rows(tt, D), _S((T, D)))(X, of, ob, z, yp, ys, pg, mv, gdn, wa, wb, wc, wo)


def _mix_bwd(dx1, of, ob, z, yp, ys, pg, mv, gdn, wa, wb, wc, wo, tc, tt):
    T = dx1.shape[0]

    def body(dx_ref, of_ref, ob_ref, z_ref, yp_ref, ys_ref, pg_ref, mv_ref, gdn_ref, wa_ref, wb_ref, wc_ref, wo_ref,
             do_ref, dz_ref, dyp_ref, dys_ref, dpg_ref, dwa_ref, dwb_ref, dwc_ref, dwo_ref, dgdn_ref, dm_ref):
        i = pl.program_id(0)

        @pl.when(i == 0)
        def _():
            for r in (dwa_ref, dwb_ref, dwc_ref, dwo_ref, dgdn_ref, dm_ref):
                r[...] = jnp.zeros_like(r)

        ons, yb, sg = _mix_branches(of_ref, ob_ref, z_ref, yp_ref, ys_ref, pg_ref, gdn_ref, wa_ref, wb_ref, wc_ref)
        ymix = sg[0] * yb[0] + sg[1] * yb[1] + sg[2] * yb[2]
        isc, gate = _stream_rows(mv_ref, i, tt, tc, 2)
        dx = dx_ref[...]
        dmix = dx * gate
        _acc_stream(dm_ref, 2, isc, dx * _dot(ymix, wo_ref[...], NN))
        dwo_ref[...] += _dot(ymix, dmix, TN)
        dymix = _dot(dmix, wo_ref[...], NT)
        dyb = []
        for k in range(3):
            dyb.append(dymix * sg[k])
            dpg_ref[:, k * D:(k + 1) * D] = dymix * yb[k] * sg[k] * (1.0 - sg[k])
        dwb_ref[...] += _dot(yp_ref[...], dyb[1], TN)
        dwc_ref[...] += _dot(ys_ref[...], dyb[2], TN)
        dyp_ref[...] = _dot(dyb[1], wb_ref[...], NT)
        dys_ref[...] = _dot(dyb[2], wc_ref[...], NT)
        dg = jnp.zeros((1, DH), F32)
        for h in range(NH):
            dwa_ref[_hs(h), :] += _dot(ons[h], dyb[0], TN)
            don = _dot(dyb[0], wa_ref[_hs(h), :], NT)
            _, vjp = jax.vjp(_head_out, of_ref[:, _hs(h)] + ob_ref[:, _hs(h)], z_ref[:, _hs(h)], gdn_ref[...])
            do_h, dz_h, dg_h = vjp(don)
            do_ref[:, _hs(h)] = do_h
            dz_ref[:, _hs(h)] = dz_h
            dg = dg + dg_h
        dgdn_ref[...] += dg

    return _call(
        body, "mix_bwd", (T // tt,),
        [_rows(tt, D), _rows(tt, DN), _rows(tt, DN), _rows(tt, DN), _rows(tt, PW), _rows(tt, PW), _rows(tt, 3 * D),
         _full((8, D)), _full((1, DH)), _full(wa.shape), _full(wb.shape), _full(wc.shape), _full(wo.shape)],
        [_rows(tt, DN), _rows(tt, DN), _rows(tt, PW), _rows(tt, PW), _rows(tt, 3 * D),
         _full(wa.shape), _full(wb.shape), _full(wc.shape), _full(wo.shape), _full((1, DH)), _full((8, D))],
        [_S((T, DN)), _S((T, DN)), _S((T, PW)), _S((T, PW)), _S((T, 3 * D)),
         _S(wa.shape), _S(wb.shape), _S(wc.shape), _S(wo.shape), _S((1, DH)), _S((8, D))])(
            dx1, of, ob, z, yp, ys, pg, mv, gdn, wa, wb, wc, wo)


def _ffn_fwd(X1, mv, g, wgu, wd, tc, tt):
    T = X1.shape[0]

    def body(x_ref, mv_ref, g_ref, wgu_ref, wd_ref, x2_ref, ff_ref):
        i = pl.program_id(0)
        _, sh = _stream_rows(mv_ref, i, tt, tc, 0)
        _, sc = _stream_rows(mv_ref, i, tt, tc, 1)
        _, gate = _stream_rows(mv_ref, i, tt, tc, 2)
        x = x_ref[...]
        gu = _dot(_modulate(x, g_ref[...], sh, sc), wgu_ref[...], NN)
        ff = _dot(_silu(gu[:, :DFF]) * gu[:, DFF:], wd_ref[...], NN)
        ff_ref[...] = ff
        x2_ref[...] = x + gate * ff

    return _call(
        body, "ffn_fwd", (T // tt,),
        [_rows(tt, D), _full((8, D)), _full((1, D)), _full(wgu.shape), _full(wd.shape)],
        [_rows(tt, D)] * 2, [_S((T, D))] * 2)(X1, mv, g, wgu, wd)


def _ffn_bwd(X1, ff, dx2, mv, g, wgu, wd, tc, tt, rider=None):
    T = X1.shape[0]

    def body(x_ref, ff_ref, dx2_ref, mv_ref, g_ref, wgu_ref, wd_ref, dx1_ref, h_ref, dgu_ref, act_ref, dff_ref, dg_ref,
             dm_ref):
        i = pl.program_id(0)
        isc, sh = _stream_rows(mv_ref, i, tt, tc, 0)
        _, sc = _stream_rows(mv_ref, i, tt, tc, 1)
        _, gate = _stream_rows(mv_ref, i, tt, tc, 2)
        x, dx2_ = x_ref[...], dx2_ref[...]
        h, vjp = jax.vjp(_modulate, x, g_ref[...], sh, sc)
        hb = h.astype(BF16)
        h_ref[...] = hb
        gu = jnp.dot(hb, wgu_ref[...], preferred_element_type=F32)
        ga, up = gu[:, :DFF], gu[:, DFF:]
        sg = jax.nn.sigmoid(ga)
        act = (ga * sg * up).astype(BF16)
        act_ref[...] = act
        dff = dx2_ * gate
        dff_ref[...] = dff.astype(BF16)
        dact = _dot(dff, wd_ref[...], NT)
        dga = (dact * up * (sg * (1.0 + ga * (1.0 - sg)))).astype(BF16)
        dup = (dact * ga * sg).astype(BF16)
        dgu_ref[:, :DFF] = dga
        dgu_ref[:, DFF:] = dup
        dh = _dot(dga, wgu_ref[:, :DFF], NT) + _dot(dup, wgu_ref[:, DFF:], NT)
        dx, dg, dsh, dsc = vjp(dh)
        dx1_ref[...] = dx2_ + dx

        @pl.when(i == 0)
        def _():
            dg_ref[...] = jnp.zeros_like(dg_ref)
            dm_ref[...] = jnp.zeros_like(dm_ref)

        dg_ref[...] += dg
        _acc_stream(dm_ref, 0, isc, dsh)
        _acc_stream(dm_ref, 1, isc, dsc)
        _acc_stream(dm_ref, 2, isc, dx2_ * ff_ref[...])

    return _call(
        body, "ffn_bwd", (T // tt,),
        [_rows(tt, D), _rows(tt, D), _rows(tt, D), _full((8, D)), _full((1, D)), _full(wgu.shape), _full(wd.shape)],
        [_rows(tt, D), _rows(tt, D), _rows(tt, 2 * DFF), _rows(tt, DFF), _rows(tt, D), _full((1, D)), _full((8, D))],
        [_S((T, D)), _S((T, D), BF16), _S((T, 2 * DFF), BF16), _S((T, DFF), BF16), _S((T, D), BF16),
         _S((1, D)), _S((8, D))], rider=rider)(X1, ff, dx2, mv, g, wgu, wd)


def _rms(x, g):
    return x * lax.rsqrt(jnp.mean(x * x, axis=-1, keepdims=True) + EPS) * g


def _loss_head(X2, tgt, gf, tc):
    T = X2.shape[0]

    def body(x_ref, t_ref, g_ref, dx_ref, loss_ref, dg_ref):
        i = pl.program_id(0)

        @pl.when(i == 0)
        def _():
            dx_ref[...] = jnp.zeros_like(dx_ref)
            loss_ref[...] = jnp.zeros_like(loss_ref)
            dg_ref[...] = jnp.zeros_like(dg_ref)

        @pl.when(i > 0)
        def _():
            y, vjp = jax.vjp(_rms, x_ref[...], g_ref[...])
            err = y - t_ref[...]
            dx, dg = vjp(err * (1.0 / D))
            dx_ref[...] = dx
            dg_ref[...] += dg
            loss_ref[...] += (0.5 / D) * jnp.sum(jnp.sum(err * err, axis=1, keepdims=True), axis=0, keepdims=True)

    return _call(
        body, "loss_head", (T // tc,),
        [_rows(tc, D), pl.BlockSpec((tc, D), lambda i: (jnp.maximum(i - 1, 0), 0)), _full((1, D))],
        [_rows(tc, D), _full((8, LANES)), _full((1, D))],
        [_S((T, D)), _S((8, LANES)), _S((1, D))])(X2, tgt, gf)


def _block_diag(pw):
    g, n = pw.shape[0], pw.shape[1]
    out = jnp.zeros((g * n, g * n), pw.dtype)
    for k in range(g):
        out = lax.dynamic_update_slice(out, pw[k], (k * n, k * n))
    return out


def _split_w_in(w):
    parts = [w[:, IN_BOUNDS[k]:IN_BOUNDS[k + 1]] for k in range(8)]
    parts[2] = jnp.pad(parts[2], ((0, 0), (0, LANES - 16)))
    return parts


def _mod_rows(mods_l, k0):
    rows = [mods_l[s, (k0 + k) * D:(k0 + k + 1) * D] for s in (0, 1) for k in range(3)]
    return jnp.stack(rows + [jnp.zeros((D,), F32)] * 2)


def _lane_row(v8):
    return jnp.pad(v8.reshape(1, 8), ((0, 0), (0, LANES - 8)))


LAYERED = ("w_in", "w_br_a", "w_br_b", "w_br_c", "w_o", "w_gu", "w_down")
LATE = ("w_br_a", "w_br_b", "w_br_c", "w_o", "w_gu", "w_down")


def _device_step(x, c, ctx, tgt, wts, tt, comm=None):
    tc = ctx.shape[0]
    X = jnp.concatenate([ctx, x], axis=0)
    if comm is None:
        row = 0
        cc = jnp.concatenate([c, jnp.zeros((CTX_ROW - 1, D), F32), wts["c_ctx"][None, :],
                              jnp.zeros((CC_ROWS - CTX_ROW - 1, D), F32)], axis=0)
        w_ada = wts["w_ada"].astype(BF16)
        mods16 = _mod_fwd(cc, w_ada, wts["b_ada"].reshape(NL, 1, 6 * D))
    else:
        row, mods16 = comm.adaln_fwd(c, wts["c_ctx"])
    mods = jnp.stack([mods16[:, CTX_ROW], lax.dynamic_index_in_dim(mods16, row, 1, keepdims=False)], axis=1)

    saved = []
    for l in range(NL):
        ws = [w.astype(BF16) for w in _split_w_in(wts["w_in"][l])]
        mv1, mv2 = _mod_rows(mods[l], 0), _mod_rows(mods[l], 3)
        g1, g2 = wts["norm1_g"][l][None, :], wts["norm2_g"][l][None, :]
        cw, scw = wts["dn_conv_w"][l], wts["sc_conv_w"][l]
        alr, dtr = _lane_row(wts["dn_a_log"][l]), _lane_row(wts["dn_dt_bias"][l])
        gdn = wts["dn_norm_g"][l][None, :]
        pwbd, ps = _block_diag(wts["pool_w"][l]), wts["pool_scale"][l][None, :]
        hb, pq, pz, pab, pp, sx, sb, sc_, pg = _inproj_fwd(X, mv1, g1, ws, tc, tt)
        qkv = _dnprep_fwd(pq, cw, tc, tt)
        if comm is not None and l == 0:
            parts, riding = _dn_chunks_fwd(qkv, pab, alr, dtr, rider=comm.late_weights_chips())
            ((of, ssf, vnf), (ob, ssb, vnb)), riding = _scan_fwd(parts, X.shape[0], tc,
                                                                 rider=comm.late_weights_pair(riding))
            wts = dict(wts, **comm.late_weights(riding))
        else:
            parts = _dn_chunks_fwd(qkv, pab, alr, dtr)
            (of, ssf, vnf), (ob, ssb, vnb) = _scan_fwd(parts, X.shape[0], tc)
        wbr = [wts[k][l].astype(BF16) for k in LATE]
        yp = _pool_fwd(pp, pwbd, ps, tc)
        ys = _sc_fwd(sx, sb, sc_, scw, tc, tt)
        X1 = _mix_fwd(X, of, ob, pz, yp, ys, pg, mv1, gdn, *wbr[:4], tc, tt)
        X2, ff = _ffn_fwd(X1, mv2, g2, wbr[4], wbr[5], tc, tt)
        saved.append(dict(X=X, X1=X1, ff=ff, ws=ws, wbr=wbr, mv1=mv1, mv2=mv2, g1=g1, g2=g2, cw=cw, scw=scw, alr=alr, dtr=dtr,
                          gdn=gdn, pwbd=pwbd, ps=ps, hb=hb, pq=pq, pz=pz, pab=pab, pp=pp, sx=sx, sb=sb, sc=sc_, pg=pg,
                          qkv=qkv, of=of, ob=ob, ssf=ssf, ssb=ssb, vnf=vnf, vnb=vnb, parts=parts, yp=yp, ys=ys))
        X = X2

    dX, loss, dgf = _loss_head(X, tgt, wts["final_norm_g"][None, :], tc)

    gl = {k: [None] * NL for k in ("w_in", "norm1_g", "norm2_g", "dn_conv_w", "dn_a_log", "dn_dt_bias", "dn_norm_g",
                                   "pool_w", "pool_scale", "sc_conv_w", "w_br_a", "w_br_b", "w_br_c", "w_o", "w_gu",
                                   "w_down")}
    dmods = [None] * NL
    early = None
    for l in reversed(range(NL)):
        s = saved[l]
        hide = comm is not None and l == 0
        res = _ffn_bwd(s["X1"], s["ff"], dX, s["mv2"], s["g2"], s["wbr"][4], s["wbr"][5], tc, tt,
                       rider=comm.grad_pair_rider([gl[k][1] for k in LAYERED]) if hide else None)
        if hide:
            res, got = res
            chip_rider = comm.grad_chip_rider(got)
        dx1, h2, dgu, act, dff, dg2, dm2 = res
        gl["w_gu"][l] = _dw(h2, dgu, tt)
        gl["w_down"][l] = _dw(act, dff, tt)
        do, dz, dyp, dys, dpg, dwa, dwb, dwc, dwo, dgdn, dmg = _mix_bwd(
            dx1, s["of"], s["ob"], s["pz"], s["yp"], s["ys"], s["pg"], s["mv1"], s["gdn"], *s["wbr"][:4], tc, tt)
        dpp, dpw, dps = _pool_bwd(s["pp"], s["pwbd"], s["ps"], dyp, tc)
        dsx, dsb, dsc, dscw = _sc_bwd(s["sx"], s["sb"], s["sc"], s["scw"], dys, tc, tt)
        (dvnf, dssf), (dvnb, dssb) = _scan_bwd(do, s["parts"], tc)
        res = _dn_chunks_bwd(s["qkv"], s["pab"], s["alr"], s["dtr"], do,
                             (s["vnf"], dvnf, s["ssf"], dssf, s["parts"][0][6]),
                             (s["vnb"], dvnb, s["ssb"], dssb, s["parts"][1][6]), rider=chip_rider if hide else None)
        if hide:
            res, early = res
            early = comm.grad_chip_done(early)
        dqf, dqb, dpf, dpb, gacc = res
        dy = _dnprep_bwd_act(s["pq"], s["cw"], dqf, dqb, tc, tt)
        dpq, dcw = _conv_bwd(dy, s["pq"], s["cw"], tc, tt)
        dps_ = [dpq, dz, dpf, dpb, dpp, dsx, dsb, dsc, dpg]
        dp_w = [0, 1, 2, 2, 3, 4, 5, 6, 7]
        dX, dg1, dm1 = _inproj_bwd(s["X"], s["mv1"], s["g1"], s["ws"], dps_, dp_w, dx1, tc, tt)
        dws = [_dw(s["hb"], dpq, tt), _dw(s["hb"], dz, tt), _dw(s["hb"], dpf + dpb, tt)[:, :16], _dw(s["hb"], dpp, tt),
               _dw(s["hb"], dsx, tt), _dw(s["hb"], dsb, tt), _dw(s["hb"], dsc, tt), _dw(s["hb"], dpg, tt)]
        gl["w_in"][l] = jnp.concatenate(dws, axis=1)
        gl["norm1_g"][l], gl["norm2_g"][l] = dg1[0], dg2[0]
        gl["dn_conv_w"][l], gl["sc_conv_w"][l] = dcw, dscw
        gl["dn_a_log"][l], gl["dn_dt_bias"][l] = gacc[0, :8].reshape(2, NH), gacc[1, :8].reshape(2, NH)
        gl["dn_norm_g"][l] = dgdn[0]
        gl["pool_w"][l] = jnp.stack([dpw[k * GW:(k + 1) * GW, k * GW:(k + 1) * GW] for k in range(4)])
        gl["pool_scale"][l] = dps[0]
        gl["w_br_a"][l], gl["w_br_b"][l], gl["w_br_c"][l], gl["w_o"][l] = dwa, dwb, dwc, dwo
        dm = dm1 + dmg
        cat = lambda r: jnp.concatenate([dm[r], dm[r + 1], dm[r + 2], dm2[r], dm2[r + 1], dm2[r + 2]])
        dmods[l] = jnp.stack([cat(0), cat(3)])

    dmods = jnp.stack(dmods)
    grads = {k: (v if k in LAYERED else jnp.stack(v)) for k, v in gl.items()}
    if comm is None:
        dm16 = jnp.zeros((NL, CC_ROWS, 6 * D), F32).at[:, CTX_ROW].set(dmods[:, 0]).at[:, row].set(dmods[:, 1])
        dwada, dcc = _mod_bwd(cc, w_ada, dm16)
        grads.update(w_ada=dwada, b_ada=dmods[:, 0] + dmods[:, 1], c_ctx=dcc[CTX_ROW])
    else:
        grads.update(comm.adaln_bwd(dmods))
    grads.update(final_norm_g=dgf[0])
    return loss, dX[tc:], grads, early


def _me():
    return lax.axis_index("x"), lax.axis_index("y"), lax.axis_index("c")


def _dev_index(p):
    return 4 * p[0] + 2 * p[1] + p[2]


def _allgather(parts):
    n = len(parts)

    def body(*refs):
        ins, outs = refs[:n], refs[n:2 * n]
        send_sems, recv_sems = refs[2 * n:]
        x, y, c = _me()
        me, sibling = (x, y, c), (x, y, 1 - c)
        chips = [(1 - x, y), (x, 1 - y), (1 - x, 1 - y)]

        def copy(a, k, block, to, src=None):
            dst = outs[a].at[_dev_index(block)]
            return pltpu.make_async_remote_copy(
                src_ref=dst if src is None else src, dst_ref=dst, send_sem=send_sems.at[a, k], recv_sem=recv_sems.at[a, k],
                device_id=to, device_id_type=MESH_ID)

        first, passed = [], []
        for a in range(n):
            first.append(copy(a, 0, me, sibling, src=ins[a]))
            first += [copy(a, 1 + j, me, (*chip, c), src=ins[a]) for j, chip in enumerate(chips)]
        for cp in first:
            cp.start()
        for a in range(n):
            for j, chip in enumerate(chips):
                copy(a, 1 + j, (*chip, c), me).wait_recv()
                passed.append(copy(a, 4 + j, (*chip, c), sibling))
                passed[-1].start()
        for a in range(n):
            copy(a, 0, sibling, me).wait_recv()
            for j, chip in enumerate(chips):
                copy(a, 4 + j, (*chip, 1 - c), me).wait_recv()
        for cp in first + passed:
            cp.wait_send()

    outs = pl.pallas_call(
        body, name="allgather", in_specs=[HBM_SPEC] * n, out_specs=[HBM_SPEC] * n,
        out_shape=[_S((N_DEV,) + p.shape, p.dtype) for p in parts],
        scratch_shapes=[pltpu.SemaphoreType.DMA((n, 7)), pltpu.SemaphoreType.DMA((n, 7))],
    )(*parts)
    return [_with_own(o, p, _dev_index(_me())) for o, p in zip(outs, parts)]


def _with_own(gathered, own, index):
    return lax.dynamic_update_index_in_dim(gathered, own, index, 0)


def _broadcast_small(small, name="small_exchange"):
    def body(in_ref, out_ref, send_sems, recv_sems, local_sem):
        x, y, c = _me()
        my = _dev_index((x, y, c))
        mine = pltpu.make_async_copy(in_ref, out_ref.at[my], local_sem)
        mine.start()
        remote = []
        for k in range(1, N_DEV):
            cp = pltpu.make_async_remote_copy(
                src_ref=in_ref, dst_ref=out_ref.at[my], send_sem=send_sems.at[k - 1], recv_sem=recv_sems.at[k - 1],
                device_id=(x ^ (k >> 2), y ^ ((k >> 1) & 1), c ^ (k & 1)), device_id_type=MESH_ID)
            cp.start()
            remote.append(cp)
        for cp in remote:
            cp.wait_recv()
        for cp in remote:
            cp.wait_send()
        mine.wait()

    return pl.pallas_call(
        body, name=name, in_specs=[HBM_SPEC], out_specs=HBM_SPEC,
        out_shape=_S((N_DEV,) + small.shape, small.dtype),
        scratch_shapes=[pltpu.SemaphoreType.DMA((7,)), pltpu.SemaphoreType.DMA((7,)), pltpu.SemaphoreType.DMA],
    )(small)


def _run_rider(rider, name):
    ni, no = len(rider.ins), len(rider.out_shapes)

    def body(*refs):
        riding = (refs[:ni], refs[ni:ni + no], refs[ni + no:])
        rider.start(*riding)
        rider.wait(*riding)

    return list(pl.pallas_call(
        body, name=name, in_specs=[HBM_SPEC] * ni, out_specs=[HBM_SPEC] * no, out_shape=rider.out_shapes,
        scratch_shapes=rider.sems)(*rider.ins))


def _chip_peers(x, y):
    return [(k - 1, (x ^ (k >> 1), y ^ (k & 1))) for k in range(1, N_CHIP)]


def _pair_exchange(g2s):
    n = len(g2s)

    def copies(ins, outs, sems):
        x, y, c = _me()
        return [pltpu.make_async_remote_copy(
            src_ref=ins[a].at[1 - c, j], dst_ref=outs[a].at[j], send_sem=sems[0].at[a, j], recv_sem=sems[1].at[a, j],
            device_id=(x, y, 1 - c), device_id_type=MESH_ID) for a in range(n) for j in range(N_CHIP)], []

    return _Rider(g2s, [_S(g.shape[1:], g.dtype) for g in g2s],
                  [pltpu.SemaphoreType.DMA((n, N_CHIP)), pltpu.SemaphoreType.DMA((n, N_CHIP))], copies)


def _my_chip():
    x, y, _ = _me()
    return 2 * x + y


def _chip_exchange(s4s):
    n = len(s4s)

    def copies(ins, outs, sems):
        x, y, c = _me()
        my = 2 * x + y
        return [pltpu.make_async_remote_copy(
            src_ref=ins[a].at[2 * px + py], dst_ref=outs[a].at[my], send_sem=sems[0].at[a, k], recv_sem=sems[1].at[a, k],
            device_id=(px, py, c), device_id_type=MESH_ID) for k, (px, py) in _chip_peers(x, y) for a in range(n)], []

    return _Rider(s4s, [_S(s.shape, s.dtype) for s in s4s],
                  [pltpu.SemaphoreType.DMA((n, N_CHIP - 1)), pltpu.SemaphoreType.DMA((n, N_CHIP - 1))], copies)


def _chip_exchange_done(s4s, recvs):
    my = _my_chip()
    return [_with_own(r, lax.dynamic_index_in_dim(s, my, 0, keepdims=False), my) for s, r in zip(s4s, recvs)]


def _chip_gather(pack):
    def copies(ins, outs, sems):
        x, y, c = _me()
        return [pltpu.make_async_remote_copy(
            src_ref=ins[0], dst_ref=outs[0].at[2 * x + y], send_sem=sems[0].at[k], recv_sem=sems[1].at[k],
            device_id=(px, py, c), device_id_type=MESH_ID) for k, (px, py) in _chip_peers(x, y)], []

    return _Rider([pack], [_S((N_CHIP,) + pack.shape, pack.dtype)],
                  [pltpu.SemaphoreType.DMA((N_CHIP - 1,)), pltpu.SemaphoreType.DMA((N_CHIP - 1,))], copies)


def _pair_gather(chips):
    def copies(ins, outs, sems):
        x, y, c = _me()
        return [pltpu.make_async_remote_copy(
            src_ref=ins[0].at[j], dst_ref=outs[0].at[j], send_sem=sems[0].at[j], recv_sem=sems[1].at[j],
            device_id=(x, y, 1 - c), device_id_type=MESH_ID) for j in range(N_CHIP)], []

    return _Rider([chips], [_S(chips.shape, chips.dtype)],
                  [pltpu.SemaphoreType.DMA((N_CHIP,)), pltpu.SemaphoreType.DMA((N_CHIP,))], copies)


def _shard_rows(r):
    return 256 if r % 256 == 0 else r


def _pair_sum(g2, got):
    _, nc, L, R, C = g2.shape
    tr = _shard_rows(R)

    def body(a_ref, b_ref, o_ref):
        o_ref[...] = (a_ref[0] + b_ref[...]).astype(BF16)

    blk = pl.BlockSpec((1, 1, tr, C), lambda j, l, i: (j, l, i, 0))
    return _call(
        body, "pair_sum", (nc, L, R // tr),
        [pl.BlockSpec((1, 1, 1, tr, C), lambda j, l, i: (lax.axis_index("c"), j, l, i, 0)), blk], blk,
        _S(got.shape, BF16))(g2, got)


def _adam(w, g, m, v):
    m2 = ADAM_B1 * m + (1.0 - ADAM_B1) * g
    v2 = ADAM_B2 * v + (1.0 - ADAM_B2) * (g * g)
    m_hat = m2 / (1.0 - ADAM_B1 ** ADAM_STEP)
    v_hat = v2 / (1.0 - ADAM_B2 ** ADAM_STEP)
    return -ADAM_LR * (m_hat / (jnp.sqrt(v_hat) + ADAM_EPS) + ADAM_WD * w), m2, v2


def _sum_adam(recvs, w, m, v):
    L, R, C = w.shape
    tr = _shard_rows(R)
    nr = len(recvs)

    def body(*refs):
        w_ref, m_ref, v_ref, g_ref, d_ref, m2_ref, v2_ref = refs[nr:]
        g = None
        for li, r_ref in enumerate(refs[:nr]):
            s = r_ref[0, 0].astype(F32)
            for j in range(1, N_CHIP):
                s = s + r_ref[j, 0].astype(F32)
            g = s if g is None else jnp.where(pl.program_id(0) == li, s, g)
        g_ref[0] = g
        d_ref[0], m2_ref[0], v2_ref[0] = _adam(w_ref[0], g, m_ref[0], v_ref[0])

    blk = pl.BlockSpec((1, tr, C), lambda l, i: (l, i, 0))
    rspec = pl.BlockSpec((N_CHIP, 1, tr, C), (lambda l, i: (0, l, i, 0)) if nr == 1 else (lambda l, i: (0, 0, i, 0)))
    return _call(body, "sum_adam", (L, R // tr), [rspec] * nr + [blk, blk, blk], [blk] * 4, [_S(w.shape)] * 4)(
        *recvs, w, m, v)


def _adam_big(w, g, m, v):
    L, R, C = w.shape
    tr = _shard_rows(R)

    def body(w_ref, g_ref, m_ref, v_ref, d_ref, m2_ref, v2_ref):
        d_ref[0], m2_ref[0], v2_ref[0] = _adam(w_ref[0], g_ref[0], m_ref[0], v_ref[0])

    blk = pl.BlockSpec((1, tr, C), lambda l, i: (l, i, 0))
    return _call(body, "adam_big", (L, R // tr), [blk] * 4, [blk] * 3, [_S(w.shape)] * 3)(w, g, m, v)


def _sum_small(recv):
    def body(r_ref, o_ref):
        g = r_ref[0]
        for k in range(1, recv.shape[0]):
            g = g + r_ref[k]
        o_ref[...] = g

    return pl.pallas_call(body, name="sum_small", out_shape=_S(recv.shape[1:]))(recv)


def _adam_small(w, g, m, v):
    def body(w_ref, g_ref, m_ref, v_ref, d_ref, m2_ref, v2_ref):
        d_ref[...], m2_ref[...], v2_ref[...] = _adam(w_ref[...], g_ref[...], m_ref[...], v_ref[...])

    return pl.pallas_call(body, name="adam_small", out_shape=[_S(w.shape)] * 3)(w, g, m, v)


def _pack(arrs, dtype, row_mult):
    parts, offs, r = [], [], 0
    for a in arrs:
        nr = -(-a.size // LANES)
        parts.append(jnp.pad(a.reshape(-1).astype(dtype), (0, nr * LANES - a.size)))
        offs.append(r)
        r += nr
    pad = (-r) % row_mult
    if pad:
        parts.append(jnp.zeros((pad * LANES,), dtype))
    return jnp.concatenate(parts).reshape(r + pad, LANES), offs


def _unpack(packed, offs, shapes, lead=()):
    out = []
    for off, shp in zip(offs, shapes):
        size = int(np.prod(shp))
        nr = -(-size // LANES)
        flat = packed[..., off:off + nr, :].reshape(lead + (nr * LANES,))
        out.append(flat[..., :size].reshape(lead + tuple(shp)))
    return out


BIG = (("w_ada", 2), ("w_in", 2), ("w_br_a", 2), ("w_br_b", 2), ("w_br_c", 2), ("w_o", 1), ("w_gu", 2), ("w_down", 1))
CONV = ("dn_conv_w", "sc_conv_w")
REPL = ("c_ctx", "b_ada", "norm1_g", "norm2_g", "dn_a_log", "dn_dt_bias", "dn_norm_g", "pool_w", "pool_scale",
        "final_norm_g")
WEIGHTS = ("c_ctx", "w_ada", "b_ada", "norm1_g", "norm2_g", "w_in", "dn_conv_w", "dn_a_log", "dn_dt_bias", "dn_norm_g",
           "pool_w", "pool_scale", "sc_conv_w", "w_br_a", "w_br_b", "w_br_c", "w_o", "w_gu", "w_down", "final_norm_g")
TOKEN_TILE = 256


def _join(blocks, axis):
    nd, nl, r, c = blocks.shape
    if axis == 2:
        return blocks.transpose(1, 2, 0, 3).reshape(nl, r, nd * c)
    return blocks.transpose(1, 0, 2, 3).reshape(nl, nd * r, c)


def _split(full, axis):
    nl, r, c = full.shape
    if axis == 2:
        return full.reshape(nl, r, N_CHIP, 2, c // N_DEV).transpose(3, 2, 0, 1, 4)
    return full.reshape(nl, N_CHIP, 2, r // N_DEV, c).transpose(2, 1, 0, 3, 4)


class _Comm:
    def __init__(self, late_shards, w_ada, b_ada):
        self.shapes = [a.shape for a in late_shards]
        self.pack, self.offs = _pack(late_shards, BF16, BF16_ROWS)
        self.w_ada, self.b_ada = w_ada.astype(BF16), b_ada
        self.g2s = None

    def adaln_fwd(self, c, c_ctx):
        my = _dev_index(_me())
        ncol = self.w_ada.shape[2]
        c_all = _broadcast_small(c.reshape(8, LANES), "c_exchange").reshape(N_DEV, D)
        self.cc = jnp.concatenate([c_all, c_ctx[None, :], jnp.zeros((CC_ROWS - N_DEV - 1, D), F32)], axis=0)
        b_cols = lax.dynamic_slice_in_dim(self.b_ada, my * ncol, ncol, axis=1).reshape(NL, 1, ncol)
        cols = _mod_fwd(self.cc, self.w_ada, b_cols)
        got = _broadcast_small(cols.reshape(-1, LANES), "mods_exchange").reshape(N_DEV, NL, CC_ROWS, ncol)
        return my, got.transpose(1, 2, 0, 3).reshape(NL, CC_ROWS, N_DEV * ncol)

    def adaln_bwd(self, dmods):
        my = _dev_index(_me())
        ncol = self.w_ada.shape[2]
        got = _broadcast_small(dmods.reshape(-1, LANES), "dmods_exchange")
        rows = got.reshape(N_DEV, NL, 2, 6 * D)
        ctx_sum = _sum_small(rows[:, :, 0].reshape(N_DEV, -1, LANES)).reshape(NL, 1, 6 * D)
        db = _sum_small(rows.transpose(0, 2, 1, 3).reshape(2 * N_DEV, -1, LANES)).reshape(NL, 6 * D)
        dm = jnp.concatenate([rows[:, :, 1].transpose(1, 0, 2), ctx_sum,
                              jnp.zeros((NL, CC_ROWS - N_DEV - 1, 6 * D), F32)], axis=1)
        dw, dcc = _mod_bwd(self.cc, self.w_ada, lax.dynamic_slice_in_dim(dm, my * ncol, ncol, axis=2))
        return dict(w_ada=dw, b_ada=db, c_ctx=dcc[CTX_ROW])

    def late_weights_chips(self):
        return _chip_gather(self.pack)

    def late_weights_pair(self, riding):
        self.chips = _with_own(riding[0], self.pack, _my_chip())
        return _pair_gather(self.chips)

    def late_weights(self, riding):
        on_south = lax.axis_index("c") == 0
        both = jnp.stack([jnp.where(on_south, self.chips, riding[0]), jnp.where(on_south, riding[0], self.chips)], axis=1)
        shards = _unpack(both.reshape((N_DEV,) + self.pack.shape), self.offs, self.shapes, (N_DEV,))
        return {k: _join(blocks, dict(BIG)[k]) for k, blocks in zip(LATE, shards)}

    def grad_pair_rider(self, layer_grads):
        self.g2s = [_split(g[None], dict(BIG)[k]) for k, g in zip(LAYERED, layer_grads)]
        return _pair_exchange(self.g2s)

    def grad_chip_rider(self, got):
        self.sums = [_pair_sum(g2, gt) for g2, gt in zip(self.g2s, got)]
        return _chip_exchange(self.sums)

    def grad_chip_done(self, riding):
        return _chip_exchange_done(self.sums, riding)


def kernel(x, c, ctx, c_ctx, w_ada, b_ada, norm1_g, norm2_g, w_in, dn_conv_w, dn_a_log, dn_dt_bias, dn_norm_g, pool_w, pool_scale, sc_conv_w, w_br_a, w_br_b, w_br_c, w_o, w_gu, w_down, final_norm_g, loss_target, m_c_ctx, m_w_ada, m_b_ada, m_norm1_g, m_norm2_g, m_w_in, m_dn_conv_w, m_dn_a_log, m_dn_dt_bias, m_dn_norm_g, m_pool_w, m_pool_scale, m_sc_conv_w, m_w_br_a, m_w_br_b, m_w_br_c, m_w_o, m_w_gu, m_w_down, m_final_norm_g, v_c_ctx, v_w_ada, v_b_ada, v_norm1_g, v_norm2_g, v_w_in, v_dn_conv_w, v_dn_a_log, v_dn_dt_bias, v_dn_norm_g, v_pool_w, v_pool_scale, v_sc_conv_w, v_w_br_a, v_w_br_b, v_w_br_c, v_w_o, v_w_gu, v_w_down, v_final_norm_g):
    loc = dict(c_ctx=c_ctx, w_ada=w_ada, b_ada=b_ada, norm1_g=norm1_g, norm2_g=norm2_g, w_in=w_in, dn_conv_w=dn_conv_w,
               dn_a_log=dn_a_log, dn_dt_bias=dn_dt_bias, dn_norm_g=dn_norm_g, pool_w=pool_w, pool_scale=pool_scale,
               sc_conv_w=sc_conv_w, w_br_a=w_br_a, w_br_b=w_br_b, w_br_c=w_br_c, w_o=w_o, w_gu=w_gu, w_down=w_down,
               final_norm_g=final_norm_g)
    mom_m = dict(c_ctx=m_c_ctx, w_ada=m_w_ada, b_ada=m_b_ada, norm1_g=m_norm1_g, norm2_g=m_norm2_g, w_in=m_w_in,
                 dn_conv_w=m_dn_conv_w, dn_a_log=m_dn_a_log, dn_dt_bias=m_dn_dt_bias, dn_norm_g=m_dn_norm_g,
                 pool_w=m_pool_w, pool_scale=m_pool_scale, sc_conv_w=m_sc_conv_w, w_br_a=m_w_br_a, w_br_b=m_w_br_b,
                 w_br_c=m_w_br_c, w_o=m_w_o, w_gu=m_w_gu, w_down=m_w_down, final_norm_g=m_final_norm_g)
    mom_v = dict(c_ctx=v_c_ctx, w_ada=v_w_ada, b_ada=v_b_ada, norm1_g=v_norm1_g, norm2_g=v_norm2_g, w_in=v_w_in,
                 dn_conv_w=v_dn_conv_w, dn_a_log=v_dn_a_log, dn_dt_bias=v_dn_dt_bias, dn_norm_g=v_dn_norm_g,
                 pool_w=v_pool_w, pool_scale=v_pool_scale, sc_conv_w=v_sc_conv_w, w_br_a=v_w_br_a, w_br_b=v_w_br_b,
                 w_br_c=v_w_br_c, w_o=v_w_o, w_gu=v_w_gu, w_down=v_w_down, final_norm_g=v_final_norm_g)
    my = _dev_index(_me())

    axis_of = dict(BIG)
    first = [k for k, _ in BIG if k not in LATE and k != "w_ada"]
    big_pack, big_offs = _pack([loc[k] for k in first], BF16, BF16_ROWS)
    conv_pack, conv_offs = _pack([loc[k] for k in CONV], F32, 8)
    big_all, conv_all = _allgather([big_pack, conv_pack])
    full = {k: loc[k] for k in REPL}
    for k, blocks in zip(first, _unpack(big_all, big_offs, [loc[k].shape for k in first], (N_DEV,))):
        full[k] = _join(blocks, axis_of[k])
    for k, blocks in zip(CONV, _unpack(conv_all, conv_offs, [loc[k].shape for k in CONV], (N_DEV,))):
        full[k] = _join(blocks, 2)

    loss8, grad_x, g, recv_l1 = _device_step(x[0], c, ctx[0], loss_target[0], full, TOKEN_TILE,
                                             comm=_Comm([loc[k] for k in LATE], w_ada, b_ada))

    tail = [_split(g[k][0][None], axis_of[k]) for k in LAYERED]
    got = _run_rider(_pair_exchange(tail), "pair_exchange")
    sums = [_pair_sum(a, b) for a, b in zip(tail, got)]
    recv_tail = _chip_exchange_done(sums, _run_rider(_chip_exchange(sums), "chip_exchange"))

    small_names = REPL + CONV
    summed = [k for k in small_names if k != "b_ada"]
    small_pack, small_offs = _pack([g[k] for k in summed] + [loss8[0:1, 0:1]], F32, 8)
    small_sum = _sum_small(_broadcast_small(small_pack))
    sums = _unpack(small_sum, small_offs, [g[k].shape for k in summed] + [(1, 1)])
    grads = dict(zip(summed, sums[:-1]), b_ada=g["b_ada"], w_ada=g["w_ada"])
    loss = sums[-1][0, 0]
    for k in CONV:
        w = loc[k].shape[2]
        grads[k] = lax.dynamic_slice_in_dim(grads[k], my * w, w, axis=2)

    delta, new_m, new_v = {}, {}, {}
    for i, k in enumerate(LAYERED):
        grads[k], delta[k], new_m[k], new_v[k] = _sum_adam([recv_tail[i], recv_l1[i]], loc[k], mom_m[k], mom_v[k])
    delta["w_ada"], new_m["w_ada"], new_v["w_ada"] = _adam_big(w_ada, g["w_ada"], m_w_ada, v_w_ada)
    packs = [_pack([src[k] for k in small_names], F32, 8)[0] for src in (loc, grads, mom_m, mom_v)]
    _, offs = _pack([loc[k] for k in small_names], F32, 8)
    shapes = [loc[k].shape for k in small_names]
    for dst, packed in zip((delta, new_m, new_v), _adam_small(*packs)):
        dst.update(zip(small_names, _unpack(packed, offs, shapes)))

    return (loss, grad_x[None], *[grads[k] for k in WEIGHTS], *[delta[k] for k in WEIGHTS],
            *[new_m[k] for k in WEIGHTS], *[new_v[k] for k in WEIGHTS])
```

```python
import functools

import numpy as np
import jax
import jax.numpy as jnp
from jax import lax
from jax.experimental import pallas as pl
from jax.experimental.pallas import tpu as pltpu

F32 = jnp.float32
BF16 = jnp.bfloat16
HI = lax.Precision.HIGHEST

D = 1024
NL = 2
NH = 4
DH = 128
DN = NH * DH
CH = 64
GW = 64
PW = 256
DFF = 2816
EPS = 1e-6
N_DEV = 8
N_CHIP = 4
MESH_ID = pl.DeviceIdType.MESH
HBM_SPEC = pl.BlockSpec(memory_space=pltpu.HBM)
LANES = 128
BF16_ROWS = 16
VMEM_MB = 56

ADAM_LR, ADAM_B1, ADAM_B2, ADAM_EPS, ADAM_WD, ADAM_STEP = 0.001, 0.9, 0.999, 1e-08, 0.01, 10

IN_BOUNDS = (0, 1536, 2048, 2064, 2320, 2576, 2832, 3088, 6160)
IN_WIDTHS = (1536, 512, 128, 256, 256, 256, 256, 3072)
POOL_WIN = ((1, 0), (2, 1), (4, 3), (8, 7))

NN = ((1,), (0,))
NT = ((1,), (1,))
TN = ((0,), (0,))


def _dot(a, b, dims, hi=False):
    if hi:
        prec = lax.Precision.HIGH if hi == "x3" else HI
        return lax.dot_general(a, b, (dims, ((), ())), precision=prec, preferred_element_type=F32)
    return lax.dot_general(a.astype(BF16), b.astype(BF16), (dims, ((), ())), preferred_element_type=F32)


def _S(shape, dtype=F32):
    return jax.ShapeDtypeStruct(tuple(shape), dtype)


def _full(shape):
    nd = len(shape)
    return pl.BlockSpec(tuple(shape), lambda *_: (0,) * nd)


def _rows(tt, w):
    return pl.BlockSpec((tt, w), lambda i: (i, 0))


class _Rider:
    def __init__(self, ins, out_shapes, sems, copies):
        self.ins, self.out_shapes, self.sems, self.copies = list(ins), list(out_shapes), list(sems), copies

    def start(self, ins, outs, sems):
        remote, local = self.copies(ins, outs, sems)
        for cp in local + remote:
            cp.start()

    def wait(self, ins, outs, sems):
        remote, local = self.copies(ins, outs, sems)
        for cp in remote:
            cp.wait_recv()
        for cp in remote:
            cp.wait_send()
        for cp in local:
            cp.wait()


def _call(body, name, grid, in_specs, out_specs, out_shape, scratch=(), rider=None):
    params = pltpu.CompilerParams(dimension_semantics=("arbitrary",) * len(grid), vmem_limit_bytes=VMEM_MB << 20)
    if rider is None:
        return pl.pallas_call(body, name=name, grid=grid, in_specs=in_specs, out_specs=out_specs, out_shape=out_shape,
                              scratch_shapes=list(scratch), compiler_params=params)
    single = not isinstance(out_shape, (list, tuple))
    out_specs, out_shape = ([out_specs], [out_shape]) if single else (list(out_specs), list(out_shape))
    n_in, n_out, n_scr = len(in_specs), len(out_shape), len(scratch)
    r_in, r_out = len(rider.ins), len(rider.out_shapes)

    def hosted(*refs):
        ins, refs = refs[:n_in + r_in], refs[n_in + r_in:]
        outs, scr = refs[:n_out + r_out], refs[n_out + r_out:]
        riding = (ins[n_in:], outs[n_out:], scr[n_scr:])

        @pl.when(pl.program_id(0) == 0)
        def _():
            rider.start(*riding)

        body(*ins[:n_in], *outs[:n_out], *scr[:n_scr])

        @pl.when(pl.program_id(0) == grid[0] - 1)
        def _():
            rider.wait(*riding)

    call = pl.pallas_call(
        hosted, name=name, grid=grid, in_specs=list(in_specs) + [HBM_SPEC] * r_in,
        out_specs=out_specs + [HBM_SPEC] * r_out, out_shape=out_shape + rider.out_shapes,
        scratch_shapes=list(scratch) + rider.sems, compiler_params=params)

    def run(*args):
        res = call(*args, *rider.ins)
        own = res[:n_out]
        return (own[0] if single else own), list(res[n_out:])

    return run


def _iota(shape, axis):
    return lax.broadcasted_iota(jnp.int32, shape, axis)


def _colsum(a):
    return jnp.sum(a, axis=0, keepdims=True)


def _silu(x):
    return x * jax.nn.sigmoid(x)


def _modulate(x, g, sh, sc):
    xn = x * lax.rsqrt(jnp.mean(x * x, axis=-1, keepdims=True) + EPS)
    return (xn * g) * (1.0 + sc) + sh


def _stream_rows(mv_ref, i, tt, tc, k):
    isc = (i * tt + _iota((tt, 1), 0)) < tc
    return isc, jnp.where(isc, mv_ref[k:k + 1, :], mv_ref[3 + k:4 + k, :])


def _acc_stream(ref, k, isc, val):
    ref[k:k + 1, :] += _colsum(jnp.where(isc, val, 0.0))
    ref[3 + k:4 + k, :] += _colsum(jnp.where(isc, 0.0, val))


CC_ROWS = 16
CTX_ROW = 8


def _mod_cols(n):
    return 1536 if n % 1536 == 0 else n


def _mod_fwd(cc, w_ada, b_ada3):
    n = w_ada.shape[2]
    ct = _mod_cols(n)

    def body(cc_ref, w_ref, b_ref, o_ref):
        o_ref[0] = _dot(_silu(cc_ref[...]), w_ref[0], NN) + b_ref[0]

    return _call(
        body, "mod_fwd", (NL, n // ct),
        [pl.BlockSpec((CC_ROWS, D), lambda l, j: (0, 0)), pl.BlockSpec((1, D, ct), lambda l, j: (l, 0, j)),
         pl.BlockSpec((1, 1, ct), lambda l, j: (l, 0, j))],
        pl.BlockSpec((1, CC_ROWS, ct), lambda l, j: (l, 0, j)), _S((NL, CC_ROWS, n)))(cc, w_ada, b_ada3)


def _mod_bwd(cc, w_ada, dmods):
    n = w_ada.shape[2]
    ct = _mod_cols(n)

    def body(cc_ref, w_ref, dm_ref, dw_ref, dcc_ref):
        first = (pl.program_id(0) == 0) & (pl.program_id(1) == 0)
        cc_ = cc_ref[...]
        sg = jax.nn.sigmoid(cc_)
        dm = dm_ref[0]
        dw_ref[0] = _dot(cc_ * sg, dm, TN)

        @pl.when(first)
        def _():
            dcc_ref[...] = jnp.zeros_like(dcc_ref)

        dcc_ref[...] += _dot(dm, w_ref[0], NT) * (sg * (1.0 + cc_ * (1.0 - sg)))

    return _call(
        body, "mod_bwd", (NL, n // ct),
        [pl.BlockSpec((CC_ROWS, D), lambda l, j: (0, 0)), pl.BlockSpec((1, D, ct), lambda l, j: (l, 0, j)),
         pl.BlockSpec((1, CC_ROWS, ct), lambda l, j: (l, 0, j))],
        [pl.BlockSpec((1, D, ct), lambda l, j: (l, 0, j)), pl.BlockSpec((CC_ROWS, D), lambda l, j: (0, 0))],
        [_S((NL, D, n)), _S((CC_ROWS, D))])(cc, w_ada, dmods)


def _inproj_fwd(X, mv, g, ws, tc, tt):
    T = X.shape[0]
    nw = len(ws)

    def body(x_ref, mv_ref, g_ref, *refs):
        w_refs, h_ref, p_refs = refs[:nw], refs[nw], refs[nw + 1:]
        i = pl.program_id(0)
        _, sh = _stream_rows(mv_ref, i, tt, tc, 0)
        _, sc = _stream_rows(mv_ref, i, tt, tc, 1)
        hb = _modulate(x_ref[...], g_ref[...], sh, sc).astype(BF16)
        h_ref[...] = hb
        for w_ref, p_ref in zip(w_refs, p_refs):
            p_ref[...] = jnp.dot(hb, w_ref[...], preferred_element_type=F32)

    return _call(
        body, "inproj_fwd", (T // tt,),
        [_rows(tt, D), _full((8, D)), _full((1, D))] + [_full(w.shape) for w in ws],
        [_rows(tt, D)] + [_rows(tt, w.shape[1]) for w in ws],
        [_S((T, D), BF16)] + [_S((T, w.shape[1])) for w in ws])(X, mv, g, *ws)


def _inproj_bwd(X, mv, g, ws, dps, dp_w, dres, tc, tt):
    T = X.shape[0]
    nw, nd = len(ws), len(dps)

    def body(x_ref, mv_ref, g_ref, dres_ref, *refs):
        w_refs, dp_refs = refs[:nw], refs[nw:nw + nd]
        dx_ref, dg_ref, dm_ref = refs[nw + nd:]
        i = pl.program_id(0)
        isc, sh = _stream_rows(mv_ref, i, tt, tc, 0)
        _, sc = _stream_rows(mv_ref, i, tt, tc, 1)
        dh = None
        for dp_ref, k in zip(dp_refs, dp_w):
            t = _dot(dp_ref[...], w_refs[k][...], NT)
            dh = t if dh is None else dh + t
        _, vjp = jax.vjp(_modulate, x_ref[...], g_ref[...], sh, sc)
        dx, dg, dsh, dsc = vjp(dh)
        dx_ref[...] = dres_ref[...] + dx

        @pl.when(i == 0)
        def _():
            dg_ref[...] = jnp.zeros_like(dg_ref)
            dm_ref[...] = jnp.zeros_like(dm_ref)

        dg_ref[...] += dg
        _acc_stream(dm_ref, 0, isc, dsh)
        _acc_stream(dm_ref, 1, isc, dsc)

    return _call(
        body, "inproj_bwd", (T // tt,),
        [_rows(tt, D), _full((8, D)), _full((1, D)), _rows(tt, D)] + [_full(w.shape) for w in ws]
        + [_rows(tt, dp.shape[1]) for dp in dps],
        [_rows(tt, D), _full((1, D)), _full((8, D))],
        [_S((T, D)), _S((1, D)), _S((8, D))])(X, mv, g, dres, *ws, *dps)


def _dw(A, B, tt):
    T, K = A.shape
    N = B.shape[1]
    tt = 3 * tt if T % (3 * tt) == 0 else tt
    tn = next(t for t in (1024, 512, 256, LANES) if N % t == 0)

    def body(a_ref, b_ref, o_ref):
        @pl.when(pl.program_id(1) == 0)
        def _():
            o_ref[...] = jnp.zeros_like(o_ref)

        o_ref[...] += _dot(a_ref[...], b_ref[...], TN)

    return _call(
        body, "dw", (N // tn, T // tt),
        [pl.BlockSpec((tt, K), lambda j, i: (i, 0)), pl.BlockSpec((tt, tn), lambda j, i: (i, j))],
        pl.BlockSpec((K, tn), lambda j, i: (0, j)), _S((K, N)))(A, B)


def _halo_specs(T, tt, cw, col):
    r8, nb8 = tt // 8, T // 8
    return [pl.BlockSpec((tt, cw), lambda j, i: (i, col(j))),
            pl.BlockSpec((8, cw), lambda j, i: (jnp.maximum(i * r8 - 1, 0), col(j))),
            pl.BlockSpec((8, cw), lambda j, i: (jnp.minimum((i + 1) * r8, nb8 - 1), col(j)))]


def _shifts(a, prev8, next8, i, tt, tc, T):
    r = _iota((tt, 1), 0)
    t = i * tt + r
    dn = jnp.where(r == 0, prev8[7:8, :], pltpu.roll(a, 1, 0))
    dn = jnp.where((t == 0) | (t == tc), 0.0, dn)
    up = jnp.where(r == tt - 1, next8[0:1, :], pltpu.roll(a, tt - 1, 0))
    up = jnp.where((t == T - 1) | (t == tc - 1), 0.0, up)
    return dn, up


def _dn_post(y, part):
    a = _silu(y)
    nrm = lax.rsqrt(jnp.sum(a * a, axis=-1, keepdims=True) + EPS)
    f = jnp.where(part == 0, nrm * (DH ** -0.5), jnp.where(part == 1, nrm, 1.0))
    return a * f


def _conv3(w_ref, dn, mid, up):
    return w_ref[0:1, :] * dn + w_ref[1:2, :] * mid + w_ref[2:3, :] * up


def _dnprep_fwd(pq, cw, tc, tt):
    T = pq.shape[0]

    def body(p_ref, pp_ref, pn_ref, w_ref, a_ref):
        part, i = pl.program_id(0), pl.program_id(1)
        p = p_ref[...]
        dn, up = _shifts(p, pp_ref[...], pn_ref[...], i, tt, tc, T)
        y = _conv3(w_ref, dn, p, up)
        for h in range(NH):
            a_ref[:, _hs(h)] = _dn_post(y[:, _hs(h)], part)

    return _call(
        body, "dnprep_fwd", (3, T // tt),
        _halo_specs(T, tt, DN, lambda j: j) + [pl.BlockSpec((3, DN), lambda j, i: (0, j))],
        pl.BlockSpec((tt, DN), lambda j, i: (i, j)), _S((T, 3 * DN)))(pq, pq, pq, cw)


def _dnprep_bwd_act(pq, cw, da_f, da_b, tc, tt):
    T = pq.shape[0]

    def body(p_ref, pp_ref, pn_ref, w_ref, df_ref, db_ref, dy_ref):
        part, i = pl.program_id(0), pl.program_id(1)
        p = p_ref[...]
        dn, up = _shifts(p, pp_ref[...], pn_ref[...], i, tt, tc, T)
        y = _conv3(w_ref, dn, p, up)
        for h in range(NH):
            _, vjp = jax.vjp(lambda yh: _dn_post(yh, part), y[:, _hs(h)])
            dy_ref[:, _hs(h)] = vjp(df_ref[:, _hs(h)] + db_ref[:, _hs(h)])[0]

    blk = pl.BlockSpec((tt, DN), lambda j, i: (i, j))
    return _call(
        body, "dnprep_bwd_act", (3, T // tt),
        _halo_specs(T, tt, DN, lambda j: j) + [pl.BlockSpec((3, DN), lambda j, i: (0, j)), blk, blk],
        blk, _S((T, 3 * DN)))(pq, pq, pq, cw, da_f, da_b)


def _conv_bwd(dy, p, cw, tc, tt):
    T, W = p.shape
    cb = DN

    def body(dy_ref, dyp_ref, dyn_ref, p_ref, pp_ref, pn_ref, w_ref, dp_ref, dw_ref):
        i = pl.program_id(1)
        dy, p_ = dy_ref[...], p_ref[...]
        ddn, dup = _shifts(dy, dyp_ref[...], dyn_ref[...], i, tt, tc, T)
        dp_ref[...] = _conv3(w_ref, dup, dy, ddn)
        pdn, pup = _shifts(p_, pp_ref[...], pn_ref[...], i, tt, tc, T)

        @pl.when(i == 0)
        def _():
            dw_ref[...] = jnp.zeros_like(dw_ref)

        dw_ref[0:1, :] += _colsum(dy * pdn)
        dw_ref[1:2, :] += _colsum(dy * p_)
        dw_ref[2:3, :] += _colsum(dy * pup)

    wspec = pl.BlockSpec((3, cb), lambda j, i: (0, j))
    return _call(
        body, "conv_bwd", (W // cb, T // tt),
        _halo_specs(T, tt, cb, lambda j: j) * 2 + [wspec],
        [pl.BlockSpec((tt, cb), lambda j, i: (i, j)), wspec], [_S((T, W)), _S((3, W))])(dy, dy, dy, p, p, p, cw)


def _sc_fwd(sx, sb, sc_, cw, tc, tt):
    T = sx.shape[0]

    def body(x_ref, xp_ref, xn_ref, c_ref, cp_ref, cn_ref, b_ref, w_ref, y_ref):
        i = pl.program_id(1)
        u = c_ref[...] * x_ref[...]
        dn, up = _shifts(u, cp_ref[...] * xp_ref[...], cn_ref[...] * xn_ref[...], i, tt, tc, T)
        y_ref[...] = b_ref[...] * _conv3(w_ref, dn, u, up)

    blk = pl.BlockSpec((tt, LANES), lambda j, i: (i, j))
    return _call(
        body, "sc_fwd", (PW // LANES, T // tt),
        _halo_specs(T, tt, LANES, lambda j: j) * 2 + [blk, pl.BlockSpec((3, LANES), lambda j, i: (0, j))],
        blk, _S((T, PW)))(sx, sx, sx, sc_, sc_, sc_, sb, cw)


def _sc_bwd(sx, sb, sc_, cw, dy, tc, tt):
    T = sx.shape[0]

    def body(x_ref, xp_ref, xn_ref, c_ref, cp_ref, cn_ref, b_ref, bp_ref, bn_ref, dy_ref, dyp_ref, dyn_ref, w_ref,
             dx_ref, db_ref, dc_ref, dw_ref):
        i = pl.program_id(1)
        x, c, dy_ = x_ref[...], c_ref[...], dy_ref[...]
        u = c * x
        udn, uup = _shifts(u, cp_ref[...] * xp_ref[...], cn_ref[...] * xn_ref[...], i, tt, tc, T)
        db_ref[...] = dy_ * _conv3(w_ref, udn, u, uup)
        e = dy_ * b_ref[...]
        edn, eup = _shifts(e, dyp_ref[...] * bp_ref[...], dyn_ref[...] * bn_ref[...], i, tt, tc, T)
        du = _conv3(w_ref, eup, e, edn)
        dx_ref[...] = du * c
        dc_ref[...] = du * x

        @pl.when(i == 0)
        def _():
            dw_ref[...] = jnp.zeros_like(dw_ref)

        dw_ref[0:1, :] += _colsum(e * udn)
        dw_ref[1:2, :] += _colsum(e * u)
        dw_ref[2:3, :] += _colsum(e * uup)

    blk = pl.BlockSpec((tt, LANES), lambda j, i: (i, j))
    wspec = pl.BlockSpec((3, LANES), lambda j, i: (0, j))
    return _call(
        body, "sc_bwd", (PW // LANES, T // tt),
        _halo_specs(T, tt, LANES, lambda j: j) * 4 + [wspec],
        [blk, blk, blk, wspec], [_S((T, PW))] * 3 + [_S((3, PW))])(
            sx, sx, sx, sc_, sc_, sc_, sb, sb, sb, dy, dy, dy, cw)


def _group_select(vals):
    g = _iota((1, PW), 1) // (PW // len(POOL_WIN))
    return jnp.where(g == 0, vals[0], jnp.where(g == 1, vals[1], jnp.where(g == 2, vals[2], vals[3])))


def _nested_box(get, mirror):
    acc, outs, pl_, ph_ = get(0), [], 0, 0
    for lo, hi in POOL_WIN:
        if mirror:
            lo, hi = hi, lo
        for k in range(pl_ + 1, lo + 1):
            acc = acc + get(-k)
        for k in range(ph_ + 1, hi + 1):
            acc = acc + get(k)
        pl_, ph_ = lo, hi
        outs.append(acc)
    return _group_select(outs)


def _box_tokens(a, n, mirror):
    idx = _iota((n, 1), 0)

    def get(k):
        if k == 0:
            return a
        return jnp.where((idx + k >= 0) & (idx + k < n), pltpu.roll(a, (-k) % n, 0), 0.0)

    return _nested_box(get, mirror)


def _inv_count(pos, n):
    return _group_select([1.0 / (jnp.minimum(pos + hi, n - 1) - jnp.maximum(pos - lo, 0) + 1).astype(F32)
                          for lo, hi in POOL_WIN])


def _pool_rows(ref, r, R, tc, mirror):
    def get(k):
        rr = r + k
        rc = jnp.clip(rr, 0, R - 1)
        v = ref[pl.ds(pl.multiple_of(tc + rc * GW, GW), GW), :]
        if mirror:
            v = v * _inv_count(jnp.full((1, PW), rc, jnp.int32), R)
        return jnp.where((rr >= 0) & (rr < R), v, 0.0)

    return _nested_box(get, mirror)


def _pool_fwd(u, pwbd, ps, tc):
    T = u.shape[0]
    R = (T - tc) // GW

    def body(u_ref, pw_ref, ps_ref, y_ref):
        pw, scale = pw_ref[...], ps_ref[...]
        uc = u_ref[0:tc, :]
        mc = _box_tokens(uc, tc, False) * _inv_count(_iota((tc, 1), 0), tc)
        y_ref[0:tc, :] = _dot(mc - uc, pw, NN) * scale
        inv_c = _inv_count(_iota((GW, 1), 0), GW)

        def row(r, carry):
            rs = _pool_rows(u_ref, r, R, tc, False) * _inv_count(jnp.full((1, PW), r, jnp.int32), R)
            m = _box_tokens(rs, GW, False) * inv_c
            sl = pl.ds(pl.multiple_of(tc + r * GW, GW), GW)
            y_ref[sl, :] = _dot(m - u_ref[sl, :], pw, NN) * scale
            return carry

        lax.fori_loop(0, R, row, 0)

    return pl.pallas_call(
        body, name="pool_fwd", out_shape=_S((T, PW)),
        compiler_params=pltpu.CompilerParams(vmem_limit_bytes=VMEM_MB << 20))(u, pwbd, ps)


def _pool_bwd(u, pwbd, ps, dy, tc):
    T = u.shape[0]
    R = (T - tc) // GW

    def body(u_ref, pw_ref, ps_ref, dy_ref, du_ref, dpw_ref, dps_ref, dd_ref):
        pw, scale = pw_ref[...], ps_ref[...]
        dpw_ref[...] = jnp.zeros_like(dpw_ref)
        dps_ref[...] = jnp.zeros_like(dps_ref)

        def back(d, dy_):
            dz = dy_ * scale
            dpw_ref[...] += _dot(d, dz, TN)
            dps_ref[...] += _colsum(dy_ * _dot(d, pw, NN))
            return _dot(dz, pw, NT)

        uc = u_ref[0:tc, :]
        inv_cc = _inv_count(_iota((tc, 1), 0), tc)
        ddc = back(_box_tokens(uc, tc, False) * inv_cc - uc, dy_ref[0:tc, :])
        du_ref[0:tc, :] = _box_tokens(ddc * inv_cc, tc, True) - ddc
        inv_c = _inv_count(_iota((GW, 1), 0), GW)

        def row1(r, carry):
            rs = _pool_rows(u_ref, r, R, tc, False) * _inv_count(jnp.full((1, PW), r, jnp.int32), R)
            m = _box_tokens(rs, GW, False) * inv_c
            sl = pl.ds(pl.multiple_of(tc + r * GW, GW), GW)
            dd_ref[sl, :] = back(m - u_ref[sl, :], dy_ref[sl, :])
            return carry

        lax.fori_loop(0, R, row1, 0)

        def row2(r, carry):
            t1 = _pool_rows(dd_ref, r, R, tc, True)
            sl = pl.ds(pl.multiple_of(tc + r * GW, GW), GW)
            du_ref[sl, :] = _box_tokens(t1 * inv_c, GW, True) - dd_ref[sl, :]
            return carry

        lax.fori_loop(0, R, row2, 0)

    return pl.pallas_call(
        body, name="pool_bwd", out_shape=[_S((T, PW)), _S((PW, PW)), _S((1, PW))],
        scratch_shapes=[pltpu.VMEM((T, PW), F32)],
        compiler_params=pltpu.CompilerParams(vmem_limit_bytes=VMEM_MB << 20))(u, pwbd, ps, dy)


def _scan_consts():
    i = np.arange(CH)
    lower = (i[:, None] >= i[None, :]).astype(np.float32)
    return jnp.asarray(np.stack([lower, lower.T])), jnp.asarray(np.stack([lower.T, lower]))


def _gates(pab, al, dtb, csum):
    sp_in = pab + dtb
    sp = jnp.maximum(sp_in, 0.0) + jnp.log(1.0 + jnp.exp(-jnp.abs(sp_in)))
    nexp = -jnp.exp(al)
    gm = nexp * sp
    return gm, jax.nn.sigmoid(pab), _dot(csum, gm, NN, hi=True), sp_in, nexp


def _lane_col(m, j):
    return jnp.sum(jnp.where(_iota(m.shape, 1) == j, m, 0.0), axis=1, keepdims=True)


def _hs(h):
    return slice(h * DH, (h + 1) * DH)


HS = NH * CH
X3 = "x3"


def _stack(x, base=0):
    return jnp.concatenate([x[:, base + h * DH:base + (h + 1) * DH] for h in range(NH)], axis=0)


def _heads(st):
    return [st[h * CH:(h + 1) * CH] for h in range(NH)]


def _rowsum(a):
    return jnp.sum(a, axis=1, keepdims=True)


def _row_of(col):
    e0 = (_iota((8, LANES), 1) == 0).astype(F32)
    return _dot(e0, jnp.broadcast_to(col, (HS, LANES)), NT, hi=True)[0:1, :]


def _inverses(nms):
    eye = (_iota((HS, HS), 0) == _iota((HS, HS), 1)).astype(F32)
    x0s, mps = [eye + nm for nm in nms], list(nms)
    for _ in range(5):
        mps = [_dot(mp, mp, NN) for mp in mps]
        x0s = [x0 + _dot(x0, mp, NN) for x0, mp in zip(x0s, mps)]
    rs = [eye - _dot(eye - nm, x0, NN, hi=X3) for nm, x0 in zip(nms, x0s)]
    return [x0 + _dot(x0, r, NN) for x0, r in zip(x0s, rs)]


def _dn_chunk_pre(qkv, pab, al, dtb, csum_d, d):
    gm, bm, gcm, sp_in, nexp = _gates(pab, al, dtb, csum_d)
    gc = jnp.concatenate([_lane_col(gcm, d * NH + h) for h in range(NH)], axis=0)
    beta = jnp.concatenate([_lane_col(bm, 8 + d * NH + h) for h in range(NH)], axis=0)
    q, k, v = _stack(qkv, 0), _stack(qkv, DN), _stack(qkv, 2 * DN)
    ii, jj = _iota((HS, HS), 0), _iota((HS, HS), 1)
    sh = CH.bit_length() - 1
    same = (ii >> sh) == (jj >> sh)
    incl = same & ((ii >= jj) if d == 0 else (ii <= jj))
    strict = same & ((ii > jj) if d == 0 else (ii < jj))
    Di = jnp.where(incl, jnp.exp(jnp.where(incl, gc - _row_of(gc), 0.0)), 0.0)
    Ds = jnp.where(strict, Di, 0.0)
    kb = k * beta
    kk = _dot(kb, k, NT)
    return dict(q=q, k=k, v=v, beta=beta, gc=gc, gm=gm, bm=bm, sp_in=sp_in, nexp=nexp, Di=Di, Ds=Ds, strict=strict,
                last=CH - 1 if d == 0 else 0, kb=kb, kk=kk)


def _dn_chunk_post(c, tm, uw=None):
    q, k, v, beta, gc, kb, last = (c[n] for n in ("q", "k", "v", "beta", "gc", "kb", "last"))
    E = jnp.exp(gc)
    gls = [gc[h * CH + last:h * CH + last + 1, :] for h in range(NH)]
    xs = jnp.exp(jnp.concatenate([jnp.broadcast_to(g, (CH, 1)) for g in gls], axis=0) - gc)
    qk = _dot(q, k, NT)
    u, w = uw if uw is not None else (_dot(tm, v * beta, NN, hi=X3), _dot(tm, kb * E, NN, hi=X3))
    return dict(c, tm=tm, E=E, gls=gls, xs=xs, qk=qk, u=u, w=w, ks=k * xs, qd=q * E, aqk=qk * c["Di"])


def _dn_chunks_bwd_math(cs, Ss, dS2s, dos, vns, dvns):
    I = range(len(cs))
    q, k, v, beta, tm, E, xs, kb, u, w = ([c[n] for c in cs] for n in ("q", "k", "v", "beta", "tm", "E", "xs", "kb", "u", "w"))
    doh, vnh, dvnh = ([_heads(a) for a in l] for l in (dos, vns, dvns))
    cat = lambda parts: jnp.concatenate(parts, axis=0)
    dqd = [cat([_dot(doh[i][h], Ss[i][h], NT) for h in range(NH)]) for i in I]
    dks = [cat([_dot(vnh[i][h], dS2s[i][h], NT) for h in range(NH)]) for i in I]
    dw = [-cat([_dot(dvnh[i][h], Ss[i][h], NT) for h in range(NH)]) for i in I]
    daqk = [_dot(dos[i], vns[i], NT) for i in I]
    drb = [_dot(tm[i], dvns[i], TN, hi=X3) for i in I]
    drw = [_dot(tm[i], dw[i], TN, hi=X3) for i in I]
    dA = [jnp.where(cs[i]["strict"], -(_dot(drb[i], u[i], NT) + _dot(drw[i], w[i], NT)), 0.0) for i in I]
    dM1 = [dA[i] * cs[i]["Ds"] for i in I]
    dM2 = [daqk[i] * cs[i]["Di"] for i in I]
    dkb = [_dot(dM1[i], k[i], NN) + drw[i] * E[i] for i in I]
    dk = [_dot(dM1[i], kb[i], TN) + _dot(dM2[i], q[i], TN) + dks[i] * xs[i] for i in I]
    dq = [_dot(dM2[i], k[i], NN) + dqd[i] * E[i] for i in I]
    on_diag = _iota((HS, HS), 0) == _iota((HS, HS), 1)
    out = []
    for i in I:
        G = dM1[i] * cs[i]["kk"] + dM2[i] * cs[i]["qk"]
        col = _rowsum(jnp.where(on_diag, jnp.broadcast_to(_colsum(G), (HS, HS)), 0.0))
        dxx = _rowsum(dks[i] * k[i]) * xs[i]
        dgc = _rowsum(G) - col + (_rowsum(dqd[i] * q[i]) + _rowsum(drw[i] * kb[i])) * E[i] - dxx
        at_last = _iota((CH, 1), 0) == cs[i]["last"]
        ends = []
        for h in range(NH):
            dgl = (_colsum(_rowsum(Ss[i][h] * dS2s[i][h])) * jnp.exp(cs[i]["gls"][h])
                   + _colsum(dxx[h * CH:(h + 1) * CH]))
            ends.append(jnp.where(at_last, dgl, 0.0))
        dbeta = _rowsum(drb[i] * v[i]) + _rowsum(dkb[i] * k[i])
        out.append((dq[i], dk[i] + dkb[i] * beta[i], drb[i] * beta[i], dgc + cat(ends), dbeta))
    return out


def _chunk_group(n, want=2):
    g = want
    while n % g:
        g //= 2
    return g


def _dn_chunks_fwd(qkv, pab, alr, dtr, rider=None):
    T = qkv.shape[0]
    n = T // CH
    G = _chunk_group(n, 4)
    csum, _ = _scan_consts()

    def body(q_ref, p_ref, cs_ref, al_ref, dt_ref, *outs):
        inst = [(g, d) for g in range(G) for d in range(2)]
        pres = [_dn_chunk_pre(q_ref[g * CH:(g + 1) * CH, :], p_ref[g * CH:(g + 1) * CH, :], al_ref[...], dt_ref[...],
                              cs_ref[d], d) for g, d in inst]
        tms = _inverses([-(p["kk"] * p["Ds"]) for p in pres])
        for (g, d), pre, tm in zip(inst, pres, tms):
            rows = slice(g * HS, (g + 1) * HS)
            u_ref, w_ref, ks_ref, qd_ref, aqk_ref, eg_ref, tm_ref = outs[7 * d:7 * d + 7]
            c = _dn_chunk_post(pre, tm)
            tm_ref[rows, :] = tm
            u_ref[rows, :] = c["u"]
            w_ref[rows, :] = c["w"].astype(BF16)
            ks_ref[rows, :] = c["ks"].astype(BF16)
            qd_ref[rows, :] = c["qd"].astype(BF16)
            aqk_ref[rows, :] = c["aqk"].astype(BF16)
            egs = [jnp.broadcast_to(jnp.exp(gl), (1, LANES)) for gl in c["gls"]]
            eg_ref[g * 8:(g + 1) * 8, :] = jnp.concatenate(egs + [jnp.zeros((8 - NH, LANES), F32)], axis=0)

    st = lambda w_: pl.BlockSpec((G * HS, w_), lambda i: (i, 0))
    one = [st(DH)] * 4 + [st(HS), pl.BlockSpec((G * 8, LANES), lambda i: (i, 0)), st(HS)]
    shp = [_S((n * HS, DH)), _S((n * HS, DH), BF16), _S((n * HS, DH), BF16), _S((n * HS, DH), BF16),
           _S((n * HS, HS), BF16), _S((n * 8, LANES)), _S((n * HS, HS))]
    res = _call(
        body, "dn_chunks_fwd", (n // G,),
        [_rows(G * CH, 3 * DN), _rows(G * CH, LANES), _full((2, CH, CH)), _full((1, LANES)), _full((1, LANES))],
        one * 2, shp * 2, rider=rider)(qkv, pab, csum, alr, dtr)
    outs, riding = (res, None) if rider is None else res
    parts = tuple(outs[:7]), tuple(outs[7:])
    return parts if rider is None else (parts, riding)


def _scan_plan(n, ncx):
    sg = 2 if n % 2 == 0 and ncx % 2 == 0 else 1
    ng, ncg = n // sg, ncx // sg
    return sg, ((lambda i: i), (lambda i: jnp.where(i < ncg, ncg - 1 - i, ng - 1 - (i - ncg))))


def _scan_specs(order, sg):
    st = lambda w_: pl.BlockSpec((sg * HS, w_), lambda i: (order(i), 0))
    return dict(st=st(DH), aqk=st(HS), eg=pl.BlockSpec((sg * 8, LANES), lambda i: (order(i), 0)),
                tok=pl.BlockSpec((sg * CH, DN), lambda i: (order(i), 0)),
                state=pl.BlockSpec((sg, DN, DH), lambda i: (order(i), 0, 0)))


def _scan_fwd(parts, T, tc, rider=None):
    n = T // CH
    sg, orders = _scan_plan(n, tc // CH)

    def body(*refs):
        S_f, S_b = refs[-2:]

        @pl.when(pl.program_id(0) == 0)
        def _():
            S_f[...] = jnp.zeros_like(S_f)
            S_b[...] = jnp.zeros_like(S_b)

        for g in range(sg):
            for d, S in enumerate((S_f, S_b)):
                u_ref, w_ref, ks_ref, qd_ref, aqk_ref, eg_ref = refs[6 * d:6 * d + 6]
                o_ref, ss_ref, vn_ref = refs[12 + 3 * d:15 + 3 * d]
                k = g if d == 0 else sg - 1 - g
                rows = slice(k * HS, (k + 1) * HS)
                ss_ref[k] = S[...]
                Sh = [S[_hs(h), :] for h in range(NH)]
                wh, ksh, qdh = _heads(w_ref[rows, :]), _heads(ks_ref[rows, :]), _heads(qd_ref[rows, :])
                vn = u_ref[rows, :] - jnp.concatenate([_dot(wh[h], Sh[h], NN) for h in range(NH)], axis=0)
                vn_ref[rows, :] = vn
                av, vnh = _heads(_dot(aqk_ref[rows, :], vn, NN)), _heads(vn)
                for h in range(NH):
                    o_ref[k * CH:(k + 1) * CH, _hs(h)] = _dot(qdh[h], Sh[h], NN) + av[h]
                    S[_hs(h), :] = Sh[h] * eg_ref[k * 8 + h:k * 8 + h + 1, :] + _dot(ksh[h], vnh[h], TN)

    ins, outs, shp = [], [], []
    for d in range(2):
        sp = _scan_specs(orders[d], sg)
        ins += [sp["st"]] * 4 + [sp["aqk"], sp["eg"]]
        outs += [sp["tok"], sp["state"], sp["st"]]
        shp += [_S((T, DN)), _S((n, DN, DH)), _S((n * HS, DH))]
    res = _call(body, "scan_fwd", (n // sg,), ins, outs, shp,
                scratch=[pltpu.VMEM((DN, DH), F32), pltpu.VMEM((DN, DH), F32)], rider=rider)(*parts[0][:6], *parts[1][:6])
    res, riding = (res, None) if rider is None else res
    out = tuple(res[:3]), tuple(res[3:])
    return out if rider is None else (out, riding)


def _scan_bwd(do, parts, tc):
    T = do.shape[0]
    n = T // CH
    sg, fwd_orders = _scan_plan(n, tc // CH)
    orders = [lambda s, f=f: f(n // sg - 1 - s) for f in fwd_orders]

    def body(*refs):
        dS_f, dS_b = refs[-2:]

        @pl.when(pl.program_id(0) == 0)
        def _():
            dS_f[...] = jnp.zeros_like(dS_f)
            dS_b[...] = jnp.zeros_like(dS_b)

        for g in range(sg):
            for d, dS in enumerate((dS_f, dS_b)):
                do_ref, w_ref, ks_ref, qd_ref, aqk_ref, eg_ref = refs[6 * d:6 * d + 6]
                dvn_ref, dss_ref = refs[12 + 2 * d:14 + 2 * d]
                k = sg - 1 - g if d == 0 else g
                rows = slice(k * HS, (k + 1) * HS)
                dss_ref[k] = dS[...]
                dSh = [dS[_hs(h), :] for h in range(NH)]
                wh, ksh, qdh = _heads(w_ref[rows, :]), _heads(ks_ref[rows, :]), _heads(qd_ref[rows, :])
                do_st = _stack(do_ref[k * CH:(k + 1) * CH, :])
                dvn = (_dot(aqk_ref[rows, :], do_st, TN)
                       + jnp.concatenate([_dot(ksh[h], dSh[h], NN) for h in range(NH)], axis=0))
                dvn_ref[rows, :] = dvn
                doh, dvnh = _heads(do_st), _heads(dvn)
                for h in range(NH):
                    dS[_hs(h), :] = (_dot(qdh[h], doh[h], TN) + dSh[h] * eg_ref[k * 8 + h:k * 8 + h + 1, :]
                                     - _dot(wh[h], dvnh[h], TN))

    ins, outs, shp, args = [], [], [], []
    for d in range(2):
        sp = _scan_specs(orders[d], sg)
        ins += [sp["tok"]] + [sp["st"]] * 3 + [sp["aqk"], sp["eg"]]
        outs += [sp["st"], sp["state"]]
        shp += [_S((n * HS, DH)), _S((n, DN, DH))]
        args += [do, *parts[d][1:6]]
    res = _call(body, "scan_bwd", (n // sg,), ins, outs, shp,
                scratch=[pltpu.VMEM((DN, DH), F32), pltpu.VMEM((DN, DH), F32)])(*args)
    return tuple(res[:2]), tuple(res[2:])


def _dn_chunks_bwd(qkv, pab, alr, dtr, do, fwd, bwd, rider=None):
    T = qkv.shape[0]
    n = T // CH
    G = _chunk_group(n)
    csum, csum_t = _scan_consts()

    def body(q_ref, p_ref, do_ref, cs_ref, cst_ref, al_ref, dt_ref, *refs):
        dq_refs, dp_refs, acc_ref = refs[14:16], refs[16:18], refs[18]

        @pl.when(pl.program_id(0) == 0)
        def _():
            acc_ref[...] = jnp.zeros_like(acc_ref)

        lane = _iota((CH, LANES), 1)
        inst = [(g, d) for g in range(G) for d in range(2)]
        cs, Ss, dS2s, dos, vns, dvns = [], [], [], [], [], []
        for g, d in inst:
            tok, rows = slice(g * CH, (g + 1) * CH), slice(g * HS, (g + 1) * HS)
            vn_ref, dvn_ref, ss_ref, dss_ref, tm_ref, u_ref, w_ref = refs[7 * d:7 * d + 7]
            cs.append(_dn_chunk_post(
                _dn_chunk_pre(q_ref[tok, :], p_ref[tok, :], al_ref[...], dt_ref[...], cs_ref[d], d), tm_ref[rows, :],
                uw=(u_ref[rows, :], w_ref[rows, :])))
            Ss.append([ss_ref[g, _hs(h), :] for h in range(NH)])
            dS2s.append([dss_ref[g, _hs(h), :] for h in range(NH)])
            dos.append(_stack(do_ref[tok, :]))
            vns.append(vn_ref[rows, :])
            dvns.append(dvn_ref[rows, :])
        for (g, d), c, (dq, dk, dv, dgc, dbeta) in zip(inst, cs, _dn_chunks_bwd_math(cs, Ss, dS2s, dos, vns, dvns)):
            tok = slice(g * CH, (g + 1) * CH)
            dgcm = jnp.zeros((CH, LANES), F32)
            dbm = jnp.zeros((CH, LANES), F32)
            for h, (a, b_, c_, e, f) in enumerate(zip(*map(_heads, (dq, dk, dv, dgc, dbeta)))):
                dq_refs[d][tok, _hs(h)] = a
                dq_refs[d][tok, _hs(NH + h)] = b_
                dq_refs[d][tok, _hs(2 * NH + h)] = c_
                dgcm = jnp.where(lane == d * NH + h, e, dgcm)
                dbm = jnp.where(lane == 8 + d * NH + h, f, dbm)
            dgm = _dot(cst_ref[d], dgcm, NN, hi=True)
            dsp = dgm * c["nexp"] * jax.nn.sigmoid(c["sp_in"])
            dp_refs[d][tok, :] = dsp + dbm * c["bm"] * (1.0 - c["bm"])
            acc_ref[0:1, :] += _colsum(dgm * c["gm"])
            acc_ref[1:2, :] += _colsum(dsp)

    st = pl.BlockSpec((G * HS, DH), lambda i: (i, 0))
    state = pl.BlockSpec((G, DN, DH), lambda i: (i, 0, 0))
    return _call(
        body, "dn_chunks_bwd", (n // G,),
        [_rows(G * CH, 3 * DN), _rows(G * CH, LANES), _rows(G * CH, DN), _full((2, CH, CH)), _full((2, CH, CH)),
         _full((1, LANES)), _full((1, LANES))]
        + [st, st, state, state, pl.BlockSpec((G * HS, HS), lambda i: (i, 0)), st, st] * 2,
        [_rows(G * CH, 3 * DN)] * 2 + [_rows(G * CH, LANES)] * 2 + [_full((8, LANES))],
        [_S((T, 3 * DN))] * 2 + [_S((T, LANES))] * 2 + [_S((8, LANES))], rider=rider)(
            qkv, pab, do, csum, csum_t, alr, dtr, *fwd, *bwd)


def _head_out(o, z, g):
    on = o * lax.rsqrt(jnp.mean(o * o, axis=-1, keepdims=True) + EPS) * g
    return on * _silu(z)


def _mix_branches(of_ref, ob_ref, z_ref, yp_ref, ys_ref, pg_ref, gdn_ref, wa_ref, wb_ref, wc_ref):
    ons, ya = [], None
    for h in range(NH):
        on = _head_out(of_ref[:, _hs(h)] + ob_ref[:, _hs(h)], z_ref[:, _hs(h)], gdn_ref[...])
        t = _dot(on, wa_ref[_hs(h), :], NN)
        ya = t if ya is None else ya + t
        ons.append(on)
    ys = [ya, _dot(yp_ref[...], wb_ref[...], NN), _dot(ys_ref[...], wc_ref[...], NN)]
    sg = [jax.nn.sigmoid(pg_ref[:, k * D:(k + 1) * D]) for k in range(3)]
    return ons, ys, sg


def _mix_fwd(X, of, ob, z, yp, ys, pg, mv, gdn, wa, wb, wc, wo, tc, tt):
    T = X.shape[0]

    def body(x_ref, of_ref, ob_ref, z_ref, yp_ref, ys_ref, pg_ref, mv_ref, gdn_ref, wa_ref, wb_ref, wc_ref, wo_ref,
             x1_ref):
        _, yb, sg = _mix_branches(of_ref, ob_ref, z_ref, yp_ref, ys_ref, pg_ref, gdn_ref, wa_ref, wb_ref, wc_ref)
        mix = _dot(sg[0] * yb[0] + sg[1] * yb[1] + sg[2] * yb[2], wo_ref[...], NN)
        _, gate = _stream_rows(mv_ref, pl.program_id(0), tt, tc, 2)
        x1_ref[...] = x_ref[...] + gate * mix

    return _call(
        body, "mix_fwd", (T // tt,),
        [_rows(tt, D), _rows(tt, DN), _rows(tt, DN), _rows(tt, DN), _rows(tt, PW), _rows(tt, PW), _rows(tt, 3 * D),
         _full((8, D)), _full((1, DH)), _full(wa.shape), _full(wb.shape), _full(wc.shape), _full(wo.shape)],
        _rows(tt, D), _S((T, D)))(X, of, ob, z, yp, ys, pg, mv, gdn, wa, wb, wc, wo)


def _mix_bwd(dx1, of, ob, z, yp, ys, pg, mv, gdn, wa, wb, wc, wo, tc, tt):
    T = dx1.shape[0]

    def body(dx_ref, of_ref, ob_ref, z_ref, yp_ref, ys_ref, pg_ref, mv_ref, gdn_ref, wa_ref, wb_ref, wc_ref, wo_ref,
             do_ref, dz_ref, dyp_ref, dys_ref, dpg_ref, dwa_ref, dwb_ref, dwc_ref, dwo_ref, dgdn_ref, dm_ref):
        i = pl.program_id(0)

        @pl.when(i == 0)
        def _():
            for r in (dwa_ref, dwb_ref, dwc_ref, dwo_ref, dgdn_ref, dm_ref):
                r[...] = jnp.zeros_like(r)

        ons, yb, sg = _mix_branches(of_ref, ob_ref, z_ref, yp_ref, ys_ref, pg_ref, gdn_ref, wa_ref, wb_ref, wc_ref)
        ymix = sg[0] * yb[0] + sg[1] * yb[1] + sg[2] * yb[2]
        isc, gate = _stream_rows(mv_ref, i, tt, tc, 2)
        dx = dx_ref[...]
        dmix = dx * gate
        _acc_stream(dm_ref, 2, isc, dx * _dot(ymix, wo_ref[...], NN))
        dwo_ref[...] += _dot(ymix, dmix, TN)
        dymix = _dot(dmix, wo_ref[...], NT)
        dyb = []
        for k in range(3):
            dyb.append(dymix * sg[k])
            dpg_ref[:, k * D:(k + 1) * D] = dymix * yb[k] * sg[k] * (1.0 - sg[k])
        dwb_ref[...] += _dot(yp_ref[...], dyb[1], TN)
        dwc_ref[...] += _dot(ys_ref[...], dyb[2], TN)
        dyp_ref[...] = _dot(dyb[1], wb_ref[...], NT)
        dys_ref[...] = _dot(dyb[2], wc_ref[...], NT)
        dg = jnp.zeros((1, DH), F32)
        for h in range(NH):
            dwa_ref[_hs(h), :] += _dot(ons[h], dyb[0], TN)
            don = _dot(dyb[0], wa_ref[_hs(h), :], NT)
            _, vjp = jax.vjp(_head_out, of_ref[:, _hs(h)] + ob_ref[:, _hs(h)], z_ref[:, _hs(h)], gdn_ref[...])
            do_h, dz_h, dg_h = vjp(don)
            do_ref[:, _hs(h)] = do_h
            dz_ref[:, _hs(h)] = dz_h
            dg = dg + dg_h
        dgdn_ref[...] += dg

    return _call(
        body, "mix_bwd", (T // tt,),
        [_rows(tt, D), _rows(tt, DN), _rows(tt, DN), _rows(tt, DN), _rows(tt, PW), _rows(tt, PW), _rows(tt, 3 * D),
         _full((8, D)), _full((1, DH)), _full(wa.shape), _full(wb.shape), _full(wc.shape), _full(wo.shape)],
        [_rows(tt, DN), _rows(tt, DN), _rows(tt, PW), _rows(tt, PW), _rows(tt, 3 * D),
         _full(wa.shape), _full(wb.shape), _full(wc.shape), _full(wo.shape), _full((1, DH)), _full((8, D))],
        [_S((T, DN)), _S((T, DN)), _S((T, PW)), _S((T, PW)), _S((T, 3 * D)),
         _S(wa.shape), _S(wb.shape), _S(wc.shape), _S(wo.shape), _S((1, DH)), _S((8, D))])(
            dx1, of, ob, z, yp, ys, pg, mv, gdn, wa, wb, wc, wo)


def _ffn_fwd(X1, mv, g, wgu, wd, tc, tt):
    T = X1.shape[0]

    def body(x_ref, mv_ref, g_ref, wgu_ref, wd_ref, x2_ref, ff_ref):
        i = pl.program_id(0)
        _, sh = _stream_rows(mv_ref, i, tt, tc, 0)
        _, sc = _stream_rows(mv_ref, i, tt, tc, 1)
        _, gate = _stream_rows(mv_ref, i, tt, tc, 2)
        x = x_ref[...]
        gu = _dot(_modulate(x, g_ref[...], sh, sc), wgu_ref[...], NN)
        ff = _dot(_silu(gu[:, :DFF]) * gu[:, DFF:], wd_ref[...], NN)
        ff_ref[...] = ff
        x2_ref[...] = x + gate * ff

    return _call(
        body, "ffn_fwd", (T // tt,),
        [_rows(tt, D), _full((8, D)), _full((1, D)), _full(wgu.shape), _full(wd.shape)],
        [_rows(tt, D)] * 2, [_S((T, D))] * 2)(X1, mv, g, wgu, wd)


def _ffn_bwd(X1, ff, dx2, mv, g, wgu, wd, tc, tt, rider=None):
    T = X1.shape[0]

    def body(x_ref, ff_ref, dx2_ref, mv_ref, g_ref, wgu_ref, wd_ref, dx1_ref, h_ref, dgu_ref, act_ref, dff_ref, dg_ref,
             dm_ref):
        i = pl.program_id(0)
        isc, sh = _stream_rows(mv_ref, i, tt, tc, 0)
        _, sc = _stream_rows(mv_ref, i, tt, tc, 1)
        _, gate = _stream_rows(mv_ref, i, tt, tc, 2)
        x, dx2_ = x_ref[...], dx2_ref[...]
        h, vjp = jax.vjp(_modulate, x, g_ref[...], sh, sc)
        hb = h.astype(BF16)
        h_ref[...] = hb
        gu = jnp.dot(hb, wgu_ref[...], preferred_element_type=F32)
        ga, up = gu[:, :DFF], gu[:, DFF:]
        sg = jax.nn.sigmoid(ga)
        act = (ga * sg * up).astype(BF16)
        act_ref[...] = act
        dff = dx2_ * gate
        dff_ref[...] = dff.astype(BF16)
        dact = _dot(dff, wd_ref[...], NT)
        dga = (dact * up * (sg * (1.0 + ga * (1.0 - sg)))).astype(BF16)
        dup = (dact * ga * sg).astype(BF16)
        dgu_ref[:, :DFF] = dga
        dgu_ref[:, DFF:] = dup
        dh = _dot(dga, wgu_ref[:, :DFF], NT) + _dot(dup, wgu_ref[:, DFF:], NT)
        dx, dg, dsh, dsc = vjp(dh)
        dx1_ref[...] = dx2_ + dx

        @pl.when(i == 0)
        def _():
            dg_ref[...] = jnp.zeros_like(dg_ref)
            dm_ref[...] = jnp.zeros_like(dm_ref)

        dg_ref[...] += dg
        _acc_stream(dm_ref, 0, isc, dsh)
        _acc_stream(dm_ref, 1, isc, dsc)
        _acc_stream(dm_ref, 2, isc, dx2_ * ff_ref[...])

    return _call(
        body, "ffn_bwd", (T // tt,),
        [_rows(tt, D), _rows(tt, D), _rows(tt, D), _full((8, D)), _full((1, D)), _full(wgu.shape), _full(wd.shape)],
        [_rows(tt, D), _rows(tt, D), _rows(tt, 2 * DFF), _rows(tt, DFF), _rows(tt, D), _full((1, D)), _full((8, D))],
        [_S((T, D)), _S((T, D), BF16), _S((T, 2 * DFF), BF16), _S((T, DFF), BF16), _S((T, D), BF16),
         _S((1, D)), _S((8, D))], rider=rider)(X1, ff, dx2, mv, g, wgu, wd)


def _rms(x, g):
    return x * lax.rsqrt(jnp.mean(x * x, axis=-1, keepdims=True) + EPS) * g


def _loss_head(X2, tgt, gf, tc):
    T = X2.shape[0]

    def body(x_ref, t_ref, g_ref, dx_ref, loss_ref, dg_ref):
        i = pl.program_id(0)

        @pl.when(i == 0)
        def _():
            dx_ref[...] = jnp.zeros_like(dx_ref)
            loss_ref[...] = jnp.zeros_like(loss_ref)
            dg_ref[...] = jnp.zeros_like(dg_ref)

        @pl.when(i > 0)
        def _():
            y, vjp = jax.vjp(_rms, x_ref[...], g_ref[...])
            err = y - t_ref[...]
            dx, dg = vjp(err * (1.0 / D))
            dx_ref[...] = dx
            dg_ref[...] += dg
            loss_ref[...] += (0.5 / D) * jnp.sum(jnp.sum(err * err, axis=1, keepdims=True), axis=0, keepdims=True)

    return _call(
        body, "loss_head", (T // tc,),
        [_rows(tc, D), pl.BlockSpec((tc, D), lambda i: (jnp.maximum(i - 1, 0), 0)), _full((1, D))],
        [_rows(tc, D), _full((8, LANES)), _full((1, D))],
        [_S((T, D)), _S((8, LANES)), _S((1, D))])(X2, tgt, gf)


def _block_diag(pw):
    g, n = pw.shape[0], pw.shape[1]
    out = jnp.zeros((g * n, g * n), pw.dtype)
    for k in range(g):
        out = lax.dynamic_update_slice(out, pw[k], (k * n, k * n))
    return out


def _split_w_in(w):
    parts = [w[:, IN_BOUNDS[k]:IN_BOUNDS[k + 1]] for k in range(8)]
    parts[2] = jnp.pad(parts[2], ((0, 0), (0, LANES - 16)))
    return parts


def _mod_rows(mods_l, k0):
    rows = [mods_l[s, (k0 + k) * D:(k0 + k + 1) * D] for s in (0, 1) for k in range(3)]
    return jnp.stack(rows + [jnp.zeros((D,), F32)] * 2)


def _lane_row(v8):
    return jnp.pad(v8.reshape(1, 8), ((0, 0), (0, LANES - 8)))


LAYERED = ("w_in", "w_br_a", "w_br_b", "w_br_c", "w_o", "w_gu", "w_down")
LATE = ("w_br_a", "w_br_b", "w_br_c", "w_o", "w_gu", "w_down")


def _device_step(x, c, ctx, tgt, wts, tt, comm=None):
    tc = ctx.shape[0]
    X = jnp.concatenate([ctx, x], axis=0)
    if comm is None:
        row = 0
        cc = jnp.concatenate([c, jnp.zeros((CTX_ROW - 1, D), F32), wts["c_ctx"][None, :],
                              jnp.zeros((CC_ROWS - CTX_ROW - 1, D), F32)], axis=0)
        w_ada = wts["w_ada"].astype(BF16)
        mods16 = _mod_fwd(cc, w_ada, wts["b_ada"].reshape(NL, 1, 6 * D))
    else:
        row, mods16 = comm.adaln_fwd(c, wts["c_ctx"])
    mods = jnp.stack([mods16[:, CTX_ROW], lax.dynamic_index_in_dim(mods16, row, 1, keepdims=False)], axis=1)

    saved = []
    for l in range(NL):
        ws = [w.astype(BF16) for w in _split_w_in(wts["w_in"][l])]
        mv1, mv2 = _mod_rows(mods[l], 0), _mod_rows(mods[l], 3)
        g1, g2 = wts["norm1_g"][l][None, :], wts["norm2_g"][l][None, :]
        cw, scw = wts["dn_conv_w"][l], wts["sc_conv_w"][l]
        alr, dtr = _lane_row(wts["dn_a_log"][l]), _lane_row(wts["dn_dt_bias"][l])
        gdn = wts["dn_norm_g"][l][None, :]
        pwbd, ps = _block_diag(wts["pool_w"][l]), wts["pool_scale"][l][None, :]
        hb, pq, pz, pab, pp, sx, sb, sc_, pg = _inproj_fwd(X, mv1, g1, ws, tc, tt)
        qkv = _dnprep_fwd(pq, cw, tc, tt)
        if comm is not None and l == 0:
            parts, riding = _dn_chunks_fwd(qkv, pab, alr, dtr, rider=comm.late_weights_chips())
            ((of, ssf, vnf), (ob, ssb, vnb)), riding = _scan_fwd(parts, X.shape[0], tc,
                                                                 rider=comm.late_weights_pair(riding))
            wts = dict(wts, **comm.late_weights(riding))
        else:
            parts = _dn_chunks_fwd(qkv, pab, alr, dtr)
            (of, ssf, vnf), (ob, ssb, vnb) = _scan_fwd(parts, X.shape[0], tc)
        wbr = [wts[k][l].astype(BF16) for k in LATE]
        yp = _pool_fwd(pp, pwbd, ps, tc)
        ys = _sc_fwd(sx, sb, sc_, scw, tc, tt)
        X1 = _mix_fwd(X, of, ob, pz, yp, ys, pg, mv1, gdn, *wbr[:4], tc, tt)
        X2, ff = _ffn_fwd(X1, mv2, g2, wbr[4], wbr[5], tc, tt)
        saved.append(dict(X=X, X1=X1, ff=ff, ws=ws, wbr=wbr, mv1=mv1, mv2=mv2, g1=g1, g2=g2, cw=cw, scw=scw, alr=alr, dtr=dtr,
                          gdn=gdn, pwbd=pwbd, ps=ps, hb=hb, pq=pq, pz=pz, pab=pab, pp=pp, sx=sx, sb=sb, sc=sc_, pg=pg,
                          qkv=qkv, of=of, ob=ob, ssf=ssf, ssb=ssb, vnf=vnf, vnb=vnb, parts=parts, yp=yp, ys=ys))
        X = X2

    dX, loss, dgf = _loss_head(X, tgt, wts["final_norm_g"][None, :], tc)

    gl = {k: [None] * NL for k in ("w_in", "norm1_g", "norm2_g", "dn_conv_w", "dn_a_log", "dn_dt_bias", "dn_norm_g",
                                   "pool_w", "pool_scale", "sc_conv_w", "w_br_a", "w_br_b", "w_br_c", "w_o", "w_gu",
                                   "w_down")}
    dmods = [None] * NL
    early = None
    for l in reversed(range(NL)):
        s = saved[l]
        hide = comm is not None and l == 0
        res = _ffn_bwd(s["X1"], s["ff"], dX, s["mv2"], s["g2"], s["wbr"][4], s["wbr"][5], tc, tt,
                       rider=comm.grad_pair_rider([gl[k][1] for k in LAYERED]) if hide else None)
        if hide:
            res, got = res
            chip_rider = comm.grad_chip_rider(got)
        dx1, h2, dgu, act, dff, dg2, dm2 = res
        gl["w_gu"][l] = _dw(h2, dgu, tt)
        gl["w_down"][l] = _dw(act, dff, tt)
        do, dz, dyp, dys, dpg, dwa, dwb, dwc, dwo, dgdn, dmg = _mix_bwd(
            dx1, s["of"], s["ob"], s["pz"], s["yp"], s["ys"], s["pg"], s["mv1"], s["gdn"], *s["wbr"][:4], tc, tt)
        dpp, dpw, dps = _pool_bwd(s["pp"], s["pwbd"], s["ps"], dyp, tc)
        dsx, dsb, dsc, dscw = _sc_bwd(s["sx"], s["sb"], s["sc"], s["scw"], dys, tc, tt)
        (dvnf, dssf), (dvnb, dssb) = _scan_bwd(do, s["parts"], tc)
        res = _dn_chunks_bwd(s["qkv"], s["pab"], s["alr"], s["dtr"], do,
                             (s["vnf"], dvnf, s["ssf"], dssf, s["parts"][0][6], *s["parts"][0][:2]),
                             (s["vnb"], dvnb, s["ssb"], dssb, s["parts"][1][6], *s["parts"][1][:2]),
                             rider=chip_rider if hide else None)
        if hide:
            res, early = res
            early = comm.grad_chip_done(early)
        dqf, dqb, dpf, dpb, gacc = res
        dy = _dnprep_bwd_act(s["pq"], s["cw"], dqf, dqb, tc, tt)
        dpq, dcw = _conv_bwd(dy, s["pq"], s["cw"], tc, tt)
        dps_ = [dpq, dz, dpf, dpb, dpp, dsx, dsb, dsc, dpg]
        dp_w = [0, 1, 2, 2, 3, 4, 5, 6, 7]
        dX, dg1, dm1 = _inproj_bwd(s["X"], s["mv1"], s["g1"], s["ws"], dps_, dp_w, dx1, tc, tt)
        dws = [_dw(s["hb"], dpq, tt), _dw(s["hb"], dz, tt), _dw(s["hb"], dpf + dpb, tt)[:, :16], _dw(s["hb"], dpp, tt),
               _dw(s["hb"], dsx, tt), _dw(s["hb"], dsb, tt), _dw(s["hb"], dsc, tt), _dw(s["hb"], dpg, tt)]
        gl["w_in"][l] = jnp.concatenate(dws, axis=1)
        gl["norm1_g"][l], gl["norm2_g"][l] = dg1[0], dg2[0]
        gl["dn_conv_w"][l], gl["sc_conv_w"][l] = dcw, dscw
        gl["dn_a_log"][l], gl["dn_dt_bias"][l] = gacc[0, :8].reshape(2, NH), gacc[1, :8].reshape(2, NH)
        gl["dn_norm_g"][l] = dgdn[0]
        gl["pool_w"][l] = jnp.stack([dpw[k * GW:(k + 1) * GW, k * GW:(k + 1) * GW] for k in range(4)])
        gl["pool_scale"][l] = dps[0]
        gl["w_br_a"][l], gl["w_br_b"][l], gl["w_br_c"][l], gl["w_o"][l] = dwa, dwb, dwc, dwo
        dm = dm1 + dmg
        cat = lambda r: jnp.concatenate([dm[r], dm[r + 1], dm[r + 2], dm2[r], dm2[r + 1], dm2[r + 2]])
        dmods[l] = jnp.stack([cat(0), cat(3)])

    dmods = jnp.stack(dmods)
    grads = {k: (v if k in LAYERED else jnp.stack(v)) for k, v in gl.items()}
    if comm is None:
        dm16 = jnp.zeros((NL, CC_ROWS, 6 * D), F32).at[:, CTX_ROW].set(dmods[:, 0]).at[:, row].set(dmods[:, 1])
        dwada, dcc = _mod_bwd(cc, w_ada, dm16)
        grads.update(w_ada=dwada, b_ada=dmods[:, 0] + dmods[:, 1], c_ctx=dcc[CTX_ROW])
    else:
        grads.update(comm.adaln_bwd(dmods))
    grads.update(final_norm_g=dgf[0])
    return loss, dX[tc:], grads, early


def _me():
    return lax.axis_index("x"), lax.axis_index("y"), lax.axis_index("c")


def _dev_index(p):
    return 4 * p[0] + 2 * p[1] + p[2]


def _allgather(parts):
    n = len(parts)

    def body(*refs):
        ins, outs = refs[:n], refs[n:2 * n]
        send_sems, recv_sems = refs[2 * n:]
        x, y, c = _me()
        me, sibling = (x, y, c), (x, y, 1 - c)
        chips = [(1 - x, y), (x, 1 - y), (1 - x, 1 - y)]

        def copy(a, k, block, to, src=None):
            dst = outs[a].at[_dev_index(block)]
            return pltpu.make_async_remote_copy(
                src_ref=dst if src is None else src, dst_ref=dst, send_sem=send_sems.at[a, k], recv_sem=recv_sems.at[a, k],
                device_id=to, device_id_type=MESH_ID)

        first, passed = [], []
        for a in range(n):
            first.append(copy(a, 0, me, sibling, src=ins[a]))
            first += [copy(a, 1 + j, me, (*chip, c), src=ins[a]) for j, chip in enumerate(chips)]
        for cp in first:
            cp.start()
        for a in range(n):
            for j, chip in enumerate(chips):
                copy(a, 1 + j, (*chip, c), me).wait_recv()
                passed.append(copy(a, 4 + j, (*chip, c), sibling))
                passed[-1].start()
        for a in range(n):
            copy(a, 0, sibling, me).wait_recv()
            for j, chip in enumerate(chips):
                copy(a, 4 + j, (*chip, 1 - c), me).wait_recv()
        for cp in first + passed:
            cp.wait_send()

    outs = pl.pallas_call(
        body, name="allgather", in_specs=[HBM_SPEC] * n, out_specs=[HBM_SPEC] * n,
        out_shape=[_S((N_DEV,) + p.shape, p.dtype) for p in parts],
        scratch_shapes=[pltpu.SemaphoreType.DMA((n, 7)), pltpu.SemaphoreType.DMA((n, 7))],
    )(*parts)
    return [_with_own(o, p, _dev_index(_me())) for o, p in zip(outs, parts)]


def _with_own(gathered, own, index):
    return lax.dynamic_update_index_in_dim(gathered, own, index, 0)


def _broadcast_small(small, name="small_exchange"):
    def body(in_ref, out_ref, send_sems, recv_sems, local_sem):
        x, y, c = _me()
        my = _dev_index((x, y, c))
        mine = pltpu.make_async_copy(in_ref, out_ref.at[my], local_sem)
        mine.start()
        remote = []
        for k in range(1, N_DEV):
            cp = pltpu.make_async_remote_copy(
                src_ref=in_ref, dst_ref=out_ref.at[my], send_sem=send_sems.at[k - 1], recv_sem=recv_sems.at[k - 1],
                device_id=(x ^ (k >> 2), y ^ ((k >> 1) & 1), c ^ (k & 1)), device_id_type=MESH_ID)
            cp.start()
            remote.append(cp)
        for cp in remote:
            cp.wait_recv()
        for cp in remote:
            cp.wait_send()
        mine.wait()

    return pl.pallas_call(
        body, name=name, in_specs=[HBM_SPEC], out_specs=HBM_SPEC,
        out_shape=_S((N_DEV,) + small.shape, small.dtype),
        scratch_shapes=[pltpu.SemaphoreType.DMA((7,)), pltpu.SemaphoreType.DMA((7,)), pltpu.SemaphoreType.DMA],
    )(small)


def _run_rider(rider, name):
    ni, no = len(rider.ins), len(rider.out_shapes)

    def body(*refs):
        riding = (refs[:ni], refs[ni:ni + no], refs[ni + no:])
        rider.start(*riding)
        rider.wait(*riding)

    return list(pl.pallas_call(
        body, name=name, in_specs=[HBM_SPEC] * ni, out_specs=[HBM_SPEC] * no, out_shape=rider.out_shapes,
        scratch_shapes=rider.sems)(*rider.ins))


def _chip_peers(x, y):
    return [(k - 1, (x ^ (k >> 1), y ^ (k & 1))) for k in range(1, N_CHIP)]


def _pair_exchange(g2s):
    n = len(g2s)

    def copies(ins, outs, sems):
        x, y, c = _me()
        return [pltpu.make_async_remote_copy(
            src_ref=ins[a].at[1 - c, j], dst_ref=outs[a].at[j], send_sem=sems[0].at[a, j], recv_sem=sems[1].at[a, j],
            device_id=(x, y, 1 - c), device_id_type=MESH_ID) for a in range(n) for j in range(N_CHIP)], []

    return _Rider(g2s, [_S(g.shape[1:], g.dtype) for g in g2s],
                  [pltpu.SemaphoreType.DMA((n, N_CHIP)), pltpu.SemaphoreType.DMA((n, N_CHIP))], copies)


def _my_chip():
    x, y, _ = _me()
    return 2 * x + y


def _chip_exchange(s4s):
    n = len(s4s)

    def copies(ins, outs, sems):
        x, y, c = _me()
        my = 2 * x + y
        return [pltpu.make_async_remote_copy(
            src_ref=ins[a].at[2 * px + py], dst_ref=outs[a].at[my], send_sem=sems[0].at[a, k], recv_sem=sems[1].at[a, k],
            device_id=(px, py, c), device_id_type=MESH_ID) for k, (px, py) in _chip_peers(x, y) for a in range(n)], []

    return _Rider(s4s, [_S(s.shape, s.dtype) for s in s4s],
                  [pltpu.SemaphoreType.DMA((n, N_CHIP - 1)), pltpu.SemaphoreType.DMA((n, N_CHIP - 1))], copies)


def _chip_exchange_done(s4s, recvs):
    my = _my_chip()
    return [_with_own(r, lax.dynamic_index_in_dim(s, my, 0, keepdims=False), my) for s, r in zip(s4s, recvs)]


def _chip_gather(pack):
    def copies(ins, outs, sems):
        x, y, c = _me()
        return [pltpu.make_async_remote_copy(
            src_ref=ins[0], dst_ref=outs[0].at[2 * x + y], send_sem=sems[0].at[k], recv_sem=sems[1].at[k],
            device_id=(px, py, c), device_id_type=MESH_ID) for k, (px, py) in _chip_peers(x, y)], []

    return _Rider([pack], [_S((N_CHIP,) + pack.shape, pack.dtype)],
                  [pltpu.SemaphoreType.DMA((N_CHIP - 1,)), pltpu.SemaphoreType.DMA((N_CHIP - 1,))], copies)


def _pair_gather(chips):
    def copies(ins, outs, sems):
        x, y, c = _me()
        return [pltpu.make_async_remote_copy(
            src_ref=ins[0].at[j], dst_ref=outs[0].at[j], send_sem=sems[0].at[j], recv_sem=sems[1].at[j],
            device_id=(x, y, 1 - c), device_id_type=MESH_ID) for j in range(N_CHIP)], []

    return _Rider([chips], [_S(chips.shape, chips.dtype)],
                  [pltpu.SemaphoreType.DMA((N_CHIP,)), pltpu.SemaphoreType.DMA((N_CHIP,))], copies)


def _shard_rows(r):
    return 256 if r % 256 == 0 else r


def _pair_sum(g2, got):
    _, nc, L, R, C = g2.shape
    tr = _shard_rows(R)

    def body(a_ref, b_ref, o_ref):
        o_ref[...] = (a_ref[0] + b_ref[...]).astype(BF16)

    blk = pl.BlockSpec((1, 1, tr, C), lambda j, l, i: (j, l, i, 0))
    return _call(
        body, "pair_sum", (nc, L, R // tr),
        [pl.BlockSpec((1, 1, 1, tr, C), lambda j, l, i: (lax.axis_index("c"), j, l, i, 0)), blk], blk,
        _S(got.shape, BF16))(g2, got)


def _adam(w, g, m, v):
    m2 = ADAM_B1 * m + (1.0 - ADAM_B1) * g
    v2 = ADAM_B2 * v + (1.0 - ADAM_B2) * (g * g)
    m_hat = m2 / (1.0 - ADAM_B1 ** ADAM_STEP)
    v_hat = v2 / (1.0 - ADAM_B2 ** ADAM_STEP)
    return -ADAM_LR * (m_hat / (jnp.sqrt(v_hat) + ADAM_EPS) + ADAM_WD * w), m2, v2


def _sum_adam(recvs, w, m, v):
    L, R, C = w.shape
    tr = _shard_rows(R)
    nr = len(recvs)

    def body(*refs):
        w_ref, m_ref, v_ref, g_ref, d_ref, m2_ref, v2_ref = refs[nr:]
        g = None
        for li, r_ref in enumerate(refs[:nr]):
            s = r_ref[0, 0].astype(F32)
            for j in range(1, N_CHIP):
                s = s + r_ref[j, 0].astype(F32)
            g = s if g is None else jnp.where(pl.program_id(0) == li, s, g)
        g_ref[0] = g
        d_ref[0], m2_ref[0], v2_ref[0] = _adam(w_ref[0], g, m_ref[0], v_ref[0])

    blk = pl.BlockSpec((1, tr, C), lambda l, i: (l, i, 0))
    rspec = pl.BlockSpec((N_CHIP, 1, tr, C), (lambda l, i: (0, l, i, 0)) if nr == 1 else (lambda l, i: (0, 0, i, 0)))
    return _call(body, "sum_adam", (L, R // tr), [rspec] * nr + [blk, blk, blk], [blk] * 4, [_S(w.shape)] * 4)(
        *recvs, w, m, v)


def _adam_big(w, g, m, v):
    L, R, C = w.shape
    tr = _shard_rows(R)

    def body(w_ref, g_ref, m_ref, v_ref, d_ref, m2_ref, v2_ref):
        d_ref[0], m2_ref[0], v2_ref[0] = _adam(w_ref[0], g_ref[0], m_ref[0], v_ref[0])

    blk = pl.BlockSpec((1, tr, C), lambda l, i: (l, i, 0))
    return _call(body, "adam_big", (L, R // tr), [blk] * 4, [blk] * 3, [_S(w.shape)] * 3)(w, g, m, v)


def _sum_small(recv):
    def body(r_ref, o_ref):
        g = r_ref[0]
        for k in range(1, recv.shape[0]):
            g = g + r_ref[k]
        o_ref[...] = g

    return pl.pallas_call(body, name="sum_small", out_shape=_S(recv.shape[1:]))(recv)


def _adam_small(w, g, m, v):
    def body(w_ref, g_ref, m_ref, v_ref, d_ref, m2_ref, v2_ref):
        d_ref[...], m2_ref[...], v2_ref[...] = _adam(w_ref[...], g_ref[...], m_ref[...], v_ref[...])

    return pl.pallas_call(body, name="adam_small", out_shape=[_S(w.shape)] * 3)(w, g, m, v)


def _pack(arrs, dtype, row_mult):
    parts, offs, r = [], [], 0
    for a in arrs:
        nr = -(-a.size // LANES)
        parts.append(jnp.pad(a.reshape(-1).astype(dtype), (0, nr * LANES - a.size)))
        offs.append(r)
        r += nr
    pad = (-r) % row_mult
    if pad:
        parts.append(jnp.zeros((pad * LANES,), dtype))
    return jnp.concatenate(parts).reshape(r + pad, LANES), offs


def _unpack(packed, offs, shapes, lead=()):
    out = []
    for off, shp in zip(offs, shapes):
        size = int(np.prod(shp))
        nr = -(-size // LANES)
        flat = packed[..., off:off + nr, :].reshape(lead + (nr * LANES,))
        out.append(flat[..., :size].reshape(lead + tuple(shp)))
    return out


BIG = (("w_ada", 2), ("w_in", 2), ("w_br_a", 2), ("w_br_b", 2), ("w_br_c", 2), ("w_o", 1), ("w_gu", 2), ("w_down", 1))
CONV = ("dn_conv_w", "sc_conv_w")
REPL = ("c_ctx", "b_ada", "norm1_g", "norm2_g", "dn_a_log", "dn_dt_bias", "dn_norm_g", "pool_w", "pool_scale",
        "final_norm_g")
WEIGHTS = ("c_ctx", "w_ada", "b_ada", "norm1_g", "norm2_g", "w_in", "dn_conv_w", "dn_a_log", "dn_dt_bias", "dn_norm_g",
           "pool_w", "pool_scale", "sc_conv_w", "w_br_a", "w_br_b", "w_br_c", "w_o", "w_gu", "w_down", "final_norm_g")
TOKEN_TILE = 256


def _join(blocks, axis):
    nd, nl, r, c = blocks.shape
    if axis == 2:
        return blocks.transpose(1, 2, 0, 3).reshape(nl, r, nd * c)
    return blocks.transpose(1, 0, 2, 3).reshape(nl, nd * r, c)


def _split(full, axis):
    nl, r, c = full.shape
    if axis == 2:
        return full.reshape(nl, r, N_CHIP, 2, c // N_DEV).transpose(3, 2, 0, 1, 4)
    return full.reshape(nl, N_CHIP, 2, r // N_DEV, c).transpose(2, 1, 0, 3, 4)


class _Comm:
    def __init__(self, late_shards, w_ada, b_ada):
        self.shapes = [a.shape for a in late_shards]
        self.pack, self.offs = _pack(late_shards, BF16, BF16_ROWS)
        self.w_ada, self.b_ada = w_ada.astype(BF16), b_ada
        self.g2s = None

    def adaln_fwd(self, c, c_ctx):
        my = _dev_index(_me())
        ncol = self.w_ada.shape[2]
        c_all = _broadcast_small(c.reshape(8, LANES), "c_exchange").reshape(N_DEV, D)
        self.cc = jnp.concatenate([c_all, c_ctx[None, :], jnp.zeros((CC_ROWS - N_DEV - 1, D), F32)], axis=0)
        b_cols = lax.dynamic_slice_in_dim(self.b_ada, my * ncol, ncol, axis=1).reshape(NL, 1, ncol)
        cols = _mod_fwd(self.cc, self.w_ada, b_cols)
        got = _broadcast_small(cols.reshape(-1, LANES), "mods_exchange").reshape(N_DEV, NL, CC_ROWS, ncol)
        return my, got.transpose(1, 2, 0, 3).reshape(NL, CC_ROWS, N_DEV * ncol)

    def adaln_bwd(self, dmods):
        my = _dev_index(_me())
        ncol = self.w_ada.shape[2]
        got = _broadcast_small(dmods.reshape(-1, LANES), "dmods_exchange")
        rows = got.reshape(N_DEV, NL, 2, 6 * D)
        ctx_sum = _sum_small(rows[:, :, 0].reshape(N_DEV, -1, LANES)).reshape(NL, 1, 6 * D)
        db = _sum_small(rows.transpose(0, 2, 1, 3).reshape(2 * N_DEV, -1, LANES)).reshape(NL, 6 * D)
        dm = jnp.concatenate([rows[:, :, 1].transpose(1, 0, 2), ctx_sum,
                              jnp.zeros((NL, CC_ROWS - N_DEV - 1, 6 * D), F32)], axis=1)
        dw, dcc = _mod_bwd(self.cc, self.w_ada, lax.dynamic_slice_in_dim(dm, my * ncol, ncol, axis=2))
        return dict(w_ada=dw, b_ada=db, c_ctx=dcc[CTX_ROW])

    def late_weights_chips(self):
        return _chip_gather(self.pack)

    def late_weights_pair(self, riding):
        self.chips = _with_own(riding[0], self.pack, _my_chip())
        return _pair_gather(self.chips)

    def late_weights(self, riding):
        on_south = lax.axis_index("c") == 0
        both = jnp.stack([jnp.where(on_south, self.chips, riding[0]), jnp.where(on_south, riding[0], self.chips)], axis=1)
        shards = _unpack(both.reshape((N_DEV,) + self.pack.shape), self.offs, self.shapes, (N_DEV,))
        return {k: _join(blocks, dict(BIG)[k]) for k, blocks in zip(LATE, shards)}

    def grad_pair_rider(self, layer_grads):
        self.g2s = [_split(g[None], dict(BIG)[k]) for k, g in zip(LAYERED, layer_grads)]
        return _pair_exchange(self.g2s)

    def grad_chip_rider(self, got):
        self.sums = [_pair_sum(g2, gt) for g2, gt in zip(self.g2s, got)]
        return _chip_exchange(self.sums)

    def grad_chip_done(self, riding):
        return _chip_exchange_done(self.sums, riding)


def kernel(x, c, ctx, c_ctx, w_ada, b_ada, norm1_g, norm2_g, w_in, dn_conv_w, dn_a_log, dn_dt_bias, dn_norm_g, pool_w, pool_scale, sc_conv_w, w_br_a, w_br_b, w_br_c, w_o, w_gu, w_down, final_norm_g, loss_target, m_c_ctx, m_w_ada, m_b_ada, m_norm1_g, m_norm2_g, m_w_in, m_dn_conv_w, m_dn_a_log, m_dn_dt_bias, m_dn_norm_g, m_pool_w, m_pool_scale, m_sc_conv_w, m_w_br_a, m_w_br_b, m_w_br_c, m_w_o, m_w_gu, m_w_down, m_final_norm_g, v_c_ctx, v_w_ada, v_b_ada, v_norm1_g, v_norm2_g, v_w_in, v_dn_conv_w, v_dn_a_log, v_dn_dt_bias, v_dn_norm_g, v_pool_w, v_pool_scale, v_sc_conv_w, v_w_br_a, v_w_br_b, v_w_br_c, v_w_o, v_w_gu, v_w_down, v_final_norm_g):
    loc = dict(c_ctx=c_ctx, w_ada=w_ada, b_ada=b_ada, norm1_g=norm1_g, norm2_g=norm2_g, w_in=w_in, dn_conv_w=dn_conv_w,
               dn_a_log=dn_a_log, dn_dt_bias=dn_dt_bias, dn_norm_g=dn_norm_g, pool_w=pool_w, pool_scale=pool_scale,
               sc_conv_w=sc_conv_w, w_br_a=w_br_a, w_br_b=w_br_b, w_br_c=w_br_c, w_o=w_o, w_gu=w_gu, w_down=w_down,
               final_norm_g=final_norm_g)
    mom_m = dict(c_ctx=m_c_ctx, w_ada=m_w_ada, b_ada=m_b_ada, norm1_g=m_norm1_g, norm2_g=m_norm2_g, w_in=m_w_in,
                 dn_conv_w=m_dn_conv_w, dn_a_log=m_dn_a_log, dn_dt_bias=m_dn_dt_bias, dn_norm_g=m_dn_norm_g,
                 pool_w=m_pool_w, pool_scale=m_pool_scale, sc_conv_w=m_sc_conv_w, w_br_a=m_w_br_a, w_br_b=m_w_br_b,
                 w_br_c=m_w_br_c, w_o=m_w_o, w_gu=m_w_gu, w_down=m_w_down, final_norm_g=m_final_norm_g)
    mom_v = dict(c_ctx=v_c_ctx, w_ada=v_w_ada, b_ada=v_b_ada, norm1_g=v_norm1_g, norm2_g=v_norm2_g, w_in=v_w_in,
                 dn_conv_w=v_dn_conv_w, dn_a_log=v_dn_a_log, dn_dt_bias=v_dn_dt_bias, dn_norm_g=v_dn_norm_g,
                 pool_w=v_pool_w, pool_scale=v_pool_scale, sc_conv_w=v_sc_conv_w, w_br_a=v_w_br_a, w_br_b=v_w_br_b,
                 w_br_c=v_w_br_c, w_o=v_w_o, w_gu=v_w_gu, w_down=v_w_down, final_norm_g=v_final_norm_g)
    my = _dev_index(_me())

    axis_of = dict(BIG)
    first = [k for k, _ in BIG if k not in LATE and k != "w_ada"]
    big_pack, big_offs = _pack([loc[k] for k in first], BF16, BF16_ROWS)
    conv_pack, conv_offs = _pack([loc[k] for k in CONV], F32, 8)
    big_all, conv_all = _allgather([big_pack, conv_pack])
    full = {k: loc[k] for k in REPL}
    for k, blocks in zip(first, _unpack(big_all, big_offs, [loc[k].shape for k in first], (N_DEV,))):
        full[k] = _join(blocks, axis_of[k])
    for k, blocks in zip(CONV, _unpack(conv_all, conv_offs, [loc[k].shape for k in CONV], (N_DEV,))):
        full[k] = _join(blocks, 2)

    loss8, grad_x, g, recv_l1 = _device_step(x[0], c, ctx[0], loss_target[0], full, TOKEN_TILE,
                                             comm=_Comm([loc[k] for k in LATE], w_ada, b_ada))

    tail = [_split(g[k][0][None], axis_of[k]) for k in LAYERED]
    got = _run_rider(_pair_exchange(tail), "pair_exchange")
    sums = [_pair_sum(a, b) for a, b in zip(tail, got)]
    recv_tail = _chip_exchange_done(sums, _run_rider(_chip_exchange(sums), "chip_exchange"))

    small_names = REPL + CONV
    summed = [k for k in small_names if k != "b_ada"]
    small_pack, small_offs = _pack([g[k] for k in summed] + [loss8[0:1, 0:1]], F32, 8)
    small_sum = _sum_small(_broadcast_small(small_pack))
    sums = _unpack(small_sum, small_offs, [g[k].shape for k in summed] + [(1, 1)])
    grads = dict(zip(summed, sums[:-1]), b_ada=g["b_ada"], w_ada=g["w_ada"])
    loss = sums[-1][0, 0]
    for k in CONV:
        w = loc[k].shape[2]
        grads[k] = lax.dynamic_slice_in_dim(grads[k], my * w, w, axis=2)

    delta, new_m, new_v = {}, {}, {}
    for i, k in enumerate(LAYERED):
        grads[k], delta[k], new_m[k], new_v[k] = _sum_adam([recv_tail[i], recv_l1[i]], loc[k], mom_m[k], mom_v[k])
    delta["w_ada"], new_m["w_ada"], new_v["w_ada"] = _adam_big(w_ada, g["w_ada"], m_w_ada, v_w_ada)
    packs = [_pack([src[k] for k in small_names], F32, 8)[0] for src in (loc, grads, mom_m, mom_v)]
    _, offs = _pack([loc[k] for k in small_names], F32, 8)
    shapes = [loc[k].shape for k in small_names]
    for dst, packed in zip((delta, new_m, new_v), _adam_small(*packs)):
        dst.update(zip(small_names, _unpack(packed, offs, shapes)))

    return (loss, grad_x[None], *[grads[k] for k in WEIGHTS], *[delta[k] for k in WEIGHTS],
            *[new_m[k] for k in WEIGHTS], *[new_v[k] for k in WEIGHTS])
```

```python
import functools

import numpy as np
import jax
import jax.numpy as jnp
from jax import lax
from jax.experimental import pallas as pl
from jax.experimental.pallas import tpu as pltpu

F32 = jnp.float32
BF16 = jnp.bfloat16
HI = lax.Precision.HIGHEST

D = 1024
NL = 2
NH = 4
DH = 128
DN = NH * DH
CH = 64
GW = 64
PW = 256
DFF = 2816
EPS = 1e-6
N_DEV = 8
N_CHIP = 4
MESH_ID = pl.DeviceIdType.MESH
HBM_SPEC = pl.BlockSpec(memory_space=pltpu.HBM)
LANES = 128
BF16_ROWS = 16
VMEM_MB = 56

ADAM_LR, ADAM_B1, ADAM_B2, ADAM_EPS, ADAM_WD, ADAM_STEP = 0.001, 0.9, 0.999, 1e-08, 0.01, 10

IN_BOUNDS = (0, 1536, 2048, 2064, 2320, 2576, 2832, 3088, 6160)
IN_WIDTHS = (1536, 512, 128, 256, 256, 256, 256, 3072)
POOL_WIN = ((1, 0), (2, 1), (4, 3), (8, 7))

NN = ((1,), (0,))
NT = ((1,), (1,))
TN = ((0,), (0,))


def _dot(a, b, dims, hi=False):
    if hi:
        prec = lax.Precision.HIGH if hi == "x3" else HI
        return lax.dot_general(a, b, (dims, ((), ())), precision=prec, preferred_element_type=F32)
    return lax.dot_general(a.astype(BF16), b.astype(BF16), (dims, ((), ())), preferred_element_type=F32)


def _S(shape, dtype=F32):
    return jax.ShapeDtypeStruct(tuple(shape), dtype)


def _full(shape):
    nd = len(shape)
    return pl.BlockSpec(tuple(shape), lambda *_: (0,) * nd)


def _rows(tt, w):
    return pl.BlockSpec((tt, w), lambda i: (i, 0))


class _Rider:
    def __init__(self, ins, out_shapes, sems, copies):
        self.ins, self.out_shapes, self.sems, self.copies = list(ins), list(out_shapes), list(sems), copies

    def start(self, ins, outs, sems):
        remote, local = self.copies(ins, outs, sems)
        for cp in local + remote:
            cp.start()

    def wait(self, ins, outs, sems):
        remote, local = self.copies(ins, outs, sems)
        for cp in remote:
            cp.wait_recv()
        for cp in remote:
            cp.wait_send()
        for cp in local:
            cp.wait()


def _call(body, name, grid, in_specs, out_specs, out_shape, scratch=(), rider=None):
    params = pltpu.CompilerParams(dimension_semantics=("arbitrary",) * len(grid), vmem_limit_bytes=VMEM_MB << 20)
    if rider is None:
        return pl.pallas_call(body, name=name, grid=grid, in_specs=in_specs, out_specs=out_specs, out_shape=out_shape,
                              scratch_shapes=list(scratch), compiler_params=params)
    single = not isinstance(out_shape, (list, tuple))
    out_specs, out_shape = ([out_specs], [out_shape]) if single else (list(out_specs), list(out_shape))
    n_in, n_out, n_scr = len(in_specs), len(out_shape), len(scratch)
    r_in, r_out = len(rider.ins), len(rider.out_shapes)

    def hosted(*refs):
        ins, refs = refs[:n_in + r_in], refs[n_in + r_in:]
        outs, scr = refs[:n_out + r_out], refs[n_out + r_out:]
        riding = (ins[n_in:], outs[n_out:], scr[n_scr:])

        @pl.when(pl.program_id(0) == 0)
        def _():
            rider.start(*riding)

        body(*ins[:n_in], *outs[:n_out], *scr[:n_scr])

        @pl.when(pl.program_id(0) == grid[0] - 1)
        def _():
            rider.wait(*riding)

    call = pl.pallas_call(
        hosted, name=name, grid=grid, in_specs=list(in_specs) + [HBM_SPEC] * r_in,
        out_specs=out_specs + [HBM_SPEC] * r_out, out_shape=out_shape + rider.out_shapes,
        scratch_shapes=list(scratch) + rider.sems, compiler_params=params)

    def run(*args):
        res = call(*args, *rider.ins)
        own = res[:n_out]
        return (own[0] if single else own), list(res[n_out:])

    return run


def _iota(shape, axis):
    return lax.broadcasted_iota(jnp.int32, shape, axis)


def _colsum(a):
    return jnp.sum(a, axis=0, keepdims=True)


def _silu(x):
    return x * jax.nn.sigmoid(x)


def _modulate(x, g, sh, sc):
    xn = x * lax.rsqrt(jnp.mean(x * x, axis=-1, keepdims=True) + EPS)
    return (xn * g) * (1.0 + sc) + sh


def _stream_rows(mv_ref, i, tt, tc, k):
    isc = (i * tt + _iota((tt, 1), 0)) < tc
    return isc, jnp.where(isc, mv_ref[k:k + 1, :], mv_ref[3 + k:4 + k, :])


def _acc_stream(ref, k, isc, val):
    ref[k:k + 1, :] += _colsum(jnp.where(isc, val, 0.0))
    ref[3 + k:4 + k, :] += _colsum(jnp.where(isc, 0.0, val))


CC_ROWS = 16
CTX_ROW = 8


def _mod_cols(n):
    return 1536 if n % 1536 == 0 else n


def _mod_fwd(cc, w_ada, b_ada3):
    n = w_ada.shape[2]
    ct = _mod_cols(n)

    def body(cc_ref, w_ref, b_ref, o_ref):
        o_ref[0] = _dot(_silu(cc_ref[...]), w_ref[0], NN) + b_ref[0]

    return _call(
        body, "mod_fwd", (NL, n // ct),
        [pl.BlockSpec((CC_ROWS, D), lambda l, j: (0, 0)), pl.BlockSpec((1, D, ct), lambda l, j: (l, 0, j)),
         pl.BlockSpec((1, 1, ct), lambda l, j: (l, 0, j))],
        pl.BlockSpec((1, CC_ROWS, ct), lambda l, j: (l, 0, j)), _S((NL, CC_ROWS, n)))(cc, w_ada, b_ada3)


def _mod_bwd(cc, w_ada, dmods):
    n = w_ada.shape[2]
    ct = _mod_cols(n)

    def body(cc_ref, w_ref, dm_ref, dw_ref, dcc_ref):
        first = (pl.program_id(0) == 0) & (pl.program_id(1) == 0)
        cc_ = cc_ref[...]
        sg = jax.nn.sigmoid(cc_)
        dm = dm_ref[0]
        dw_ref[0] = _dot(cc_ * sg, dm, TN)

        @pl.when(first)
        def _():
            dcc_ref[...] = jnp.zeros_like(dcc_ref)

        dcc_ref[...] += _dot(dm, w_ref[0], NT) * (sg * (1.0 + cc_ * (1.0 - sg)))

    return _call(
        body, "mod_bwd", (NL, n // ct),
        [pl.BlockSpec((CC_ROWS, D), lambda l, j: (0, 0)), pl.BlockSpec((1, D, ct), lambda l, j: (l, 0, j)),
         pl.BlockSpec((1, CC_ROWS, ct), lambda l, j: (l, 0, j))],
        [pl.BlockSpec((1, D, ct), lambda l, j: (l, 0, j)), pl.BlockSpec((CC_ROWS, D), lambda l, j: (0, 0))],
        [_S((NL, D, n)), _S((CC_ROWS, D))])(cc, w_ada, dmods)


def _inproj_fwd(X, mv, g, ws, tc, tt):
    T = X.shape[0]
    nw = len(ws)

    def body(x_ref, mv_ref, g_ref, *refs):
        w_refs, h_ref, p_refs = refs[:nw], refs[nw], refs[nw + 1:]
        i = pl.program_id(0)
        _, sh = _stream_rows(mv_ref, i, tt, tc, 0)
        _, sc = _stream_rows(mv_ref, i, tt, tc, 1)
        hb = _modulate(x_ref[...], g_ref[...], sh, sc).astype(BF16)
        h_ref[...] = hb
        for w_ref, p_ref in zip(w_refs, p_refs):
            p_ref[...] = jnp.dot(hb, w_ref[...], preferred_element_type=F32)

    return _call(
        body, "inproj_fwd", (T // tt,),
        [_rows(tt, D), _full((8, D)), _full((1, D))] + [_full(w.shape) for w in ws],
        [_rows(tt, D)] + [_rows(tt, w.shape[1]) for w in ws],
        [_S((T, D), BF16)] + [_S((T, w.shape[1])) for w in ws])(X, mv, g, *ws)


def _inproj_bwd(X, mv, g, ws, dps, dp_w, dres, tc, tt):
    T = X.shape[0]
    nw, nd = len(ws), len(dps)

    def body(x_ref, mv_ref, g_ref, dres_ref, *refs):
        w_refs, dp_refs = refs[:nw], refs[nw:nw + nd]
        dx_ref, dg_ref, dm_ref = refs[nw + nd:]
        i = pl.program_id(0)
        isc, sh = _stream_rows(mv_ref, i, tt, tc, 0)
        _, sc = _stream_rows(mv_ref, i, tt, tc, 1)
        dh = None
        for dp_ref, k in zip(dp_refs, dp_w):
            t = _dot(dp_ref[...], w_refs[k][...], NT)
            dh = t if dh is None else dh + t
        _, vjp = jax.vjp(_modulate, x_ref[...], g_ref[...], sh, sc)
        dx, dg, dsh, dsc = vjp(dh)
        dx_ref[...] = dres_ref[...] + dx

        @pl.when(i == 0)
        def _():
            dg_ref[...] = jnp.zeros_like(dg_ref)
            dm_ref[...] = jnp.zeros_like(dm_ref)

        dg_ref[...] += dg
        _acc_stream(dm_ref, 0, isc, dsh)
        _acc_stream(dm_ref, 1, isc, dsc)

    return _call(
        body, "inproj_bwd", (T // tt,),
        [_rows(tt, D), _full((8, D)), _full((1, D)), _rows(tt, D)] + [_full(w.shape) for w in ws]
        + [_rows(tt, dp.shape[1]) for dp in dps],
        [_rows(tt, D), _full((1, D)), _full((8, D))],
        [_S((T, D)), _S((1, D)), _S((8, D))])(X, mv, g, dres, *ws, *dps)


def _dw(A, B, tt):
    T, K = A.shape
    N = B.shape[1]
    tt = 3 * tt if T % (3 * tt) == 0 else tt
    tn = next(t for t in (1024, 512, 256, LANES) if N % t == 0)

    def body(a_ref, b_ref, o_ref):
        @pl.when(pl.program_id(1) == 0)
        def _():
            o_ref[...] = jnp.zeros_like(o_ref)

        o_ref[...] += _dot(a_ref[...], b_ref[...], TN)

    return _call(
        body, "dw", (N // tn, T // tt),
        [pl.BlockSpec((tt, K), lambda j, i: (i, 0)), pl.BlockSpec((tt, tn), lambda j, i: (i, j))],
        pl.BlockSpec((K, tn), lambda j, i: (0, j)), _S((K, N)))(A, B)


def _halo_specs(T, tt, cw, col):
    r8, nb8 = tt // 8, T // 8
    return [pl.BlockSpec((tt, cw), lambda j, i: (i, col(j))),
            pl.BlockSpec((8, cw), lambda j, i: (jnp.maximum(i * r8 - 1, 0), col(j))),
            pl.BlockSpec((8, cw), lambda j, i: (jnp.minimum((i + 1) * r8, nb8 - 1), col(j)))]


def _shifts(a, prev8, next8, i, tt, tc, T):
    r = _iota((tt, 1), 0)
    t = i * tt + r
    dn = jnp.where(r == 0, prev8[7:8, :], pltpu.roll(a, 1, 0))
    dn = jnp.where((t == 0) | (t == tc), 0.0, dn)
    up = jnp.where(r == tt - 1, next8[0:1, :], pltpu.roll(a, tt - 1, 0))
    up = jnp.where((t == T - 1) | (t == tc - 1), 0.0, up)
    return dn, up


def _dn_post(y, part):
    a = _silu(y)
    nrm = lax.rsqrt(jnp.sum(a * a, axis=-1, keepdims=True) + EPS)
    f = jnp.where(part == 0, nrm * (DH ** -0.5), jnp.where(part == 1, nrm, 1.0))
    return a * f


def _conv3(w_ref, dn, mid, up):
    return w_ref[0:1, :] * dn + w_ref[1:2, :] * mid + w_ref[2:3, :] * up


def _dnprep_fwd(pq, cw, tc, tt):
    T = pq.shape[0]

    def body(p_ref, pp_ref, pn_ref, w_ref, a_ref):
        part, i = pl.program_id(0), pl.program_id(1)
        p = p_ref[...]
        dn, up = _shifts(p, pp_ref[...], pn_ref[...], i, tt, tc, T)
        y = _conv3(w_ref, dn, p, up)
        for h in range(NH):
            a_ref[:, _hs(h)] = _dn_post(y[:, _hs(h)], part)

    return _call(
        body, "dnprep_fwd", (3, T // tt),
        _halo_specs(T, tt, DN, lambda j: j) + [pl.BlockSpec((3, DN), lambda j, i: (0, j))],
        pl.BlockSpec((tt, DN), lambda j, i: (i, j)), _S((T, 3 * DN)))(pq, pq, pq, cw)


def _dnprep_bwd_act(pq, cw, da_f, da_b, tc, tt):
    T = pq.shape[0]

    def body(p_ref, pp_ref, pn_ref, w_ref, df_ref, db_ref, dy_ref):
        part, i = pl.program_id(0), pl.program_id(1)
        p = p_ref[...]
        dn, up = _shifts(p, pp_ref[...], pn_ref[...], i, tt, tc, T)
        y = _conv3(w_ref, dn, p, up)
        for h in range(NH):
            _, vjp = jax.vjp(lambda yh: _dn_post(yh, part), y[:, _hs(h)])
            dy_ref[:, _hs(h)] = vjp(df_ref[:, _hs(h)] + db_ref[:, _hs(h)])[0]

    blk = pl.BlockSpec((tt, DN), lambda j, i: (i, j))
    return _call(
        body, "dnprep_bwd_act", (3, T // tt),
        _halo_specs(T, tt, DN, lambda j: j) + [pl.BlockSpec((3, DN), lambda j, i: (0, j)), blk, blk],
        blk, _S((T, 3 * DN)))(pq, pq, pq, cw, da_f, da_b)


def _conv_bwd(dy, p, cw, tc, tt):
    T, W = p.shape
    cb = DN

    def body(dy_ref, dyp_ref, dyn_ref, p_ref, pp_ref, pn_ref, w_ref, dp_ref, dw_ref):
        i = pl.program_id(1)
        dy, p_ = dy_ref[...], p_ref[...]
        ddn, dup = _shifts(dy, dyp_ref[...], dyn_ref[...], i, tt, tc, T)
        dp_ref[...] = _conv3(w_ref, dup, dy, ddn)
        pdn, pup = _shifts(p_, pp_ref[...], pn_ref[...], i, tt, tc, T)

        @pl.when(i == 0)
        def _():
            dw_ref[...] = jnp.zeros_like(dw_ref)

        dw_ref[0:1, :] += _colsum(dy * pdn)
        dw_ref[1:2, :] += _colsum(dy * p_)
        dw_ref[2:3, :] += _colsum(dy * pup)

    wspec = pl.BlockSpec((3, cb), lambda j, i: (0, j))
    return _call(
        body, "conv_bwd", (W // cb, T // tt),
        _halo_specs(T, tt, cb, lambda j: j) * 2 + [wspec],
        [pl.BlockSpec((tt, cb), lambda j, i: (i, j)), wspec], [_S((T, W)), _S((3, W))])(dy, dy, dy, p, p, p, cw)


def _sc_fwd(sx, sb, sc_, cw, tc, tt):
    T = sx.shape[0]

    def body(x_ref, xp_ref, xn_ref, c_ref, cp_ref, cn_ref, b_ref, w_ref, y_ref):
        i = pl.program_id(1)
        u = c_ref[...] * x_ref[...]
        dn, up = _shifts(u, cp_ref[...] * xp_ref[...], cn_ref[...] * xn_ref[...], i, tt, tc, T)
        y_ref[...] = b_ref[...] * _conv3(w_ref, dn, u, up)

    blk = pl.BlockSpec((tt, LANES), lambda j, i: (i, j))
    return _call(
        body, "sc_fwd", (PW // LANES, T // tt),
        _halo_specs(T, tt, LANES, lambda j: j) * 2 + [blk, pl.BlockSpec((3, LANES), lambda j, i: (0, j))],
        blk, _S((T, PW)))(sx, sx, sx, sc_, sc_, sc_, sb, cw)


def _sc_bwd(sx, sb, sc_, cw, dy, tc, tt):
    T = sx.shape[0]

    def body(x_ref, xp_ref, xn_ref, c_ref, cp_ref, cn_ref, b_ref, bp_ref, bn_ref, dy_ref, dyp_ref, dyn_ref, w_ref,
             dx_ref, db_ref, dc_ref, dw_ref):
        i = pl.program_id(1)
        x, c, dy_ = x_ref[...], c_ref[...], dy_ref[...]
        u = c * x
        udn, uup = _shifts(u, cp_ref[...] * xp_ref[...], cn_ref[...] * xn_ref[...], i, tt, tc, T)
        db_ref[...] = dy_ * _conv3(w_ref, udn, u, uup)
        e = dy_ * b_ref[...]
        edn, eup = _shifts(e, dyp_ref[...] * bp_ref[...], dyn_ref[...] * bn_ref[...], i, tt, tc, T)
        du = _conv3(w_ref, eup, e, edn)
        dx_ref[...] = du * c
        dc_ref[...] = du * x

        @pl.when(i == 0)
        def _():
            dw_ref[...] = jnp.zeros_like(dw_ref)

        dw_ref[0:1, :] += _colsum(e * udn)
        dw_ref[1:2, :] += _colsum(e * u)
        dw_ref[2:3, :] += _colsum(e * uup)

    blk = pl.BlockSpec((tt, LANES), lambda j, i: (i, j))
    wspec = pl.BlockSpec((3, LANES), lambda j, i: (0, j))
    return _call(
        body, "sc_bwd", (PW // LANES, T // tt),
        _halo_specs(T, tt, LANES, lambda j: j) * 4 + [wspec],
        [blk, blk, blk, wspec], [_S((T, PW))] * 3 + [_S((3, PW))])(
            sx, sx, sx, sc_, sc_, sc_, sb, sb, sb, dy, dy, dy, cw)


def _group_select(vals):
    g = _iota((1, PW), 1) // (PW // len(POOL_WIN))
    return jnp.where(g == 0, vals[0], jnp.where(g == 1, vals[1], jnp.where(g == 2, vals[2], vals[3])))


def _nested_box(get, mirror):
    acc, outs, pl_, ph_ = get(0), [], 0, 0
    for lo, hi in POOL_WIN:
        if mirror:
            lo, hi = hi, lo
        for k in range(pl_ + 1, lo + 1):
            acc = acc + get(-k)
        for k in range(ph_ + 1, hi + 1):
            acc = acc + get(k)
        pl_, ph_ = lo, hi
        outs.append(acc)
    return _group_select(outs)


def _box_tokens(a, n, mirror):
    idx = _iota((n, 1), 0)

    def get(k):
        if k == 0:
            return a
        return jnp.where((idx + k >= 0) & (idx + k < n), pltpu.roll(a, (-k) % n, 0), 0.0)

    return _nested_box(get, mirror)


def _inv_count(pos, n):
    return _group_select([1.0 / (jnp.minimum(pos + hi, n - 1) - jnp.maximum(pos - lo, 0) + 1).astype(F32)
                          for lo, hi in POOL_WIN])


def _pool_rows(ref, r, R, tc, mirror):
    def get(k):
        rr = r + k
        rc = jnp.clip(rr, 0, R - 1)
        v = ref[pl.ds(pl.multiple_of(tc + rc * GW, GW), GW), :]
        if mirror:
            v = v * _inv_count(jnp.full((1, PW), rc, jnp.int32), R)
        return jnp.where((rr >= 0) & (rr < R), v, 0.0)

    return _nested_box(get, mirror)


def _pool_fwd(u, pwbd, ps, tc):
    T = u.shape[0]
    R = (T - tc) // GW

    def body(u_ref, pw_ref, ps_ref, y_ref):
        pw, scale = pw_ref[...], ps_ref[...]
        uc = u_ref[0:tc, :]
        mc = _box_tokens(uc, tc, False) * _inv_count(_iota((tc, 1), 0), tc)
        y_ref[0:tc, :] = _dot(mc - uc, pw, NN) * scale
        inv_c = _inv_count(_iota((GW, 1), 0), GW)

        def row(r, carry):
            rs = _pool_rows(u_ref, r, R, tc, False) * _inv_count(jnp.full((1, PW), r, jnp.int32), R)
            m = _box_tokens(rs, GW, False) * inv_c
            sl = pl.ds(pl.multiple_of(tc + r * GW, GW), GW)
            y_ref[sl, :] = _dot(m - u_ref[sl, :], pw, NN) * scale
            return carry

        lax.fori_loop(0, R, row, 0)

    return pl.pallas_call(
        body, name="pool_fwd", out_shape=_S((T, PW)),
        compiler_params=pltpu.CompilerParams(vmem_limit_bytes=VMEM_MB << 20))(u, pwbd, ps)


def _pool_bwd(u, pwbd, ps, dy, tc):
    T = u.shape[0]
    R = (T - tc) // GW

    def body(u_ref, pw_ref, ps_ref, dy_ref, du_ref, dpw_ref, dps_ref, dd_ref):
        pw, scale = pw_ref[...], ps_ref[...]
        dpw_ref[...] = jnp.zeros_like(dpw_ref)
        dps_ref[...] = jnp.zeros_like(dps_ref)

        def back(d, dy_):
            dz = dy_ * scale
            dpw_ref[...] += _dot(d, dz, TN)
            dps_ref[...] += _colsum(dy_ * _dot(d, pw, NN))
            return _dot(dz, pw, NT)

        uc = u_ref[0:tc, :]
        inv_cc = _inv_count(_iota((tc, 1), 0), tc)
        ddc = back(_box_tokens(uc, tc, False) * inv_cc - uc, dy_ref[0:tc, :])
        du_ref[0:tc, :] = _box_tokens(ddc * inv_cc, tc, True) - ddc
        inv_c = _inv_count(_iota((GW, 1), 0), GW)

        def row1(r, carry):
            rs = _pool_rows(u_ref, r, R, tc, False) * _inv_count(jnp.full((1, PW), r, jnp.int32), R)
            m = _box_tokens(rs, GW, False) * inv_c
            sl = pl.ds(pl.multiple_of(tc + r * GW, GW), GW)
            dd_ref[sl, :] = back(m - u_ref[sl, :], dy_ref[sl, :])
            return carry

        lax.fori_loop(0, R, row1, 0)

        def row2(r, carry):
            t1 = _pool_rows(dd_ref, r, R, tc, True)
            sl = pl.ds(pl.multiple_of(tc + r * GW, GW), GW)
            du_ref[sl, :] = _box_tokens(t1 * inv_c, GW, True) - dd_ref[sl, :]
            return carry

        lax.fori_loop(0, R, row2, 0)

    return pl.pallas_call(
        body, name="pool_bwd", out_shape=[_S((T, PW)), _S((PW, PW)), _S((1, PW))],
        scratch_shapes=[pltpu.VMEM((T, PW), F32)],
        compiler_params=pltpu.CompilerParams(vmem_limit_bytes=VMEM_MB << 20))(u, pwbd, ps, dy)


def _scan_consts():
    i = np.arange(CH)
    lower = (i[:, None] >= i[None, :]).astype(np.float32)
    return jnp.asarray(np.stack([lower, lower.T])), jnp.asarray(np.stack([lower.T, lower]))


def _gates(pab, al, dtb, csum):
    sp_in = pab + dtb
    sp = jnp.maximum(sp_in, 0.0) + jnp.log(1.0 + jnp.exp(-jnp.abs(sp_in)))
    nexp = -jnp.exp(al)
    gm = nexp * sp
    return gm, jax.nn.sigmoid(pab), _dot(csum, gm, NN, hi=True), sp_in, nexp


def _lane_col(m, j):
    return jnp.sum(jnp.where(_iota(m.shape, 1) == j, m, 0.0), axis=1, keepdims=True)


def _hs(h):
    return slice(h * DH, (h + 1) * DH)


HS = NH * CH
X3 = "x3"


def _stack(x, base=0):
    return jnp.concatenate([x[:, base + h * DH:base + (h + 1) * DH] for h in range(NH)], axis=0)


def _heads(st):
    return [st[h * CH:(h + 1) * CH] for h in range(NH)]


def _rowsum(a):
    return jnp.sum(a, axis=1, keepdims=True)


def _row_of(col):
    e0 = (_iota((8, LANES), 1) == 0).astype(F32)
    return _dot(e0, jnp.broadcast_to(col, (HS, LANES)), NT, hi=True)[0:1, :]


def _inverses(nms):
    eye = (_iota((HS, HS), 0) == _iota((HS, HS), 1)).astype(F32)
    x0s, mps = [eye + nm for nm in nms], list(nms)
    for _ in range(5):
        mps = [_dot(mp, mp, NN) for mp in mps]
        x0s = [x0 + _dot(x0, mp, NN) for x0, mp in zip(x0s, mps)]
    rs = [eye - _dot(eye - nm, x0, NN, hi=X3) for nm, x0 in zip(nms, x0s)]
    return [x0 + _dot(x0, r, NN) for x0, r in zip(x0s, rs)]


def _dn_chunk_pre(qkv, pab, al, dtb, csum_d, d):
    gm, bm, gcm, sp_in, nexp = _gates(pab, al, dtb, csum_d)
    gc = jnp.concatenate([_lane_col(gcm, d * NH + h) for h in range(NH)], axis=0)
    beta = jnp.concatenate([_lane_col(bm, 8 + d * NH + h) for h in range(NH)], axis=0)
    q, k, v = _stack(qkv, 0), _stack(qkv, DN), _stack(qkv, 2 * DN)
    ii, jj = _iota((HS, HS), 0), _iota((HS, HS), 1)
    sh = CH.bit_length() - 1
    same = (ii >> sh) == (jj >> sh)
    incl = same & ((ii >= jj) if d == 0 else (ii <= jj))
    strict = same & ((ii > jj) if d == 0 else (ii < jj))
    Di = jnp.where(incl, jnp.exp(jnp.where(incl, gc - _row_of(gc), 0.0)), 0.0)
    Ds = jnp.where(strict, Di, 0.0)
    kb = k * beta
    kk = _dot(kb, k, NT)
    return dict(q=q, k=k, v=v, beta=beta, gc=gc, gm=gm, bm=bm, sp_in=sp_in, nexp=nexp, Di=Di, Ds=Ds, strict=strict,
                last=CH - 1 if d == 0 else 0, kb=kb, kk=kk)


def _dn_chunk_post(c, tm, uw=None):
    q, k, v, beta, gc, kb, last = (c[n] for n in ("q", "k", "v", "beta", "gc", "kb", "last"))
    E = jnp.exp(gc)
    gls = [gc[h * CH + last:h * CH + last + 1, :] for h in range(NH)]
    xs = jnp.exp(jnp.concatenate([jnp.broadcast_to(g, (CH, 1)) for g in gls], axis=0) - gc)
    qk = _dot(q, k, NT)
    u, w = uw if uw is not None else (_dot(tm, v * beta, NN, hi=X3), _dot(tm, kb * E, NN, hi=X3))
    return dict(c, tm=tm, E=E, gls=gls, xs=xs, qk=qk, u=u, w=w, ks=k * xs, qd=q * E, aqk=qk * c["Di"])


def _dn_chunks_bwd_math(cs, Ss, dS2s, dos, vns, dvns):
    I = range(len(cs))
    q, k, v, beta, tm, E, xs, kb, u, w = ([c[n] for c in cs] for n in ("q", "k", "v", "beta", "tm", "E", "xs", "kb", "u", "w"))
    doh, vnh, dvnh = ([_heads(a) for a in l] for l in (dos, vns, dvns))
    cat = lambda parts: jnp.concatenate(parts, axis=0)
    dqd = [cat([_dot(doh[i][h], Ss[i][h], NT) for h in range(NH)]) for i in I]
    dks = [cat([_dot(vnh[i][h], dS2s[i][h], NT) for h in range(NH)]) for i in I]
    dw = [-cat([_dot(dvnh[i][h], Ss[i][h], NT) for h in range(NH)]) for i in I]
    daqk = [_dot(dos[i], vns[i], NT) for i in I]
    drb = [_dot(tm[i], dvns[i], TN, hi=X3) for i in I]
    drw = [_dot(tm[i], dw[i], TN, hi=X3) for i in I]
    dA = [jnp.where(cs[i]["strict"], -(_dot(drb[i], u[i], NT) + _dot(drw[i], w[i], NT)), 0.0) for i in I]
    dM1 = [dA[i] * cs[i]["Ds"] for i in I]
    dM2 = [daqk[i] * cs[i]["Di"] for i in I]
    dkb = [_dot(dM1[i], k[i], NN) + drw[i] * E[i] for i in I]
    dk = [_dot(dM1[i], kb[i], TN) + _dot(dM2[i], q[i], TN) + dks[i] * xs[i] for i in I]
    dq = [_dot(dM2[i], k[i], NN) + dqd[i] * E[i] for i in I]
    on_diag = _iota((HS, HS), 0) == _iota((HS, HS), 1)
    out = []
    for i in I:
        G = dM1[i] * cs[i]["kk"] + dM2[i] * cs[i]["qk"]
        col = _rowsum(jnp.where(on_diag, jnp.broadcast_to(_colsum(G), (HS, HS)), 0.0))
        dxx = _rowsum(dks[i] * k[i]) * xs[i]
        dgc = _rowsum(G) - col + (_rowsum(dqd[i] * q[i]) + _rowsum(drw[i] * kb[i])) * E[i] - dxx
        at_last = _iota((CH, 1), 0) == cs[i]["last"]
        ends = []
        for h in range(NH):
            dgl = (_colsum(_rowsum(Ss[i][h] * dS2s[i][h])) * jnp.exp(cs[i]["gls"][h])
                   + _colsum(dxx[h * CH:(h + 1) * CH]))
            ends.append(jnp.where(at_last, dgl, 0.0))
        dbeta = _rowsum(drb[i] * v[i]) + _rowsum(dkb[i] * k[i])
        out.append((dq[i], dk[i] + dkb[i] * beta[i], drb[i] * beta[i], dgc + cat(ends), dbeta))
    return out


def _chunk_group(n, want=2):
    g = want
    while n % g:
        g //= 2
    return g


def _dn_chunks_fwd(qkv, pab, alr, dtr, rider=None):
    T = qkv.shape[0]
    n = T // CH
    G = _chunk_group(n, 4)
    csum, _ = _scan_consts()

    def body(q_ref, p_ref, cs_ref, al_ref, dt_ref, *outs):
        inst = [(g, d) for g in range(G) for d in range(2)]
        pres = [_dn_chunk_pre(q_ref[g * CH:(g + 1) * CH, :], p_ref[g * CH:(g + 1) * CH, :], al_ref[...], dt_ref[...],
                              cs_ref[d], d) for g, d in inst]
        tms = _inverses([-(p["kk"] * p["Ds"]) for p in pres])
        for (g, d), pre, tm in zip(inst, pres, tms):
            rows = slice(g * HS, (g + 1) * HS)
            u_ref, w_ref, ks_ref, qd_ref, aqk_ref, eg_ref, tm_ref = outs[7 * d:7 * d + 7]
            c = _dn_chunk_post(pre, tm)
            tm_ref[rows, :] = tm
            u_ref[rows, :] = c["u"]
            w_ref[rows, :] = c["w"].astype(BF16)
            ks_ref[rows, :] = c["ks"].astype(BF16)
            qd_ref[rows, :] = c["qd"].astype(BF16)
            aqk_ref[rows, :] = c["aqk"].astype(BF16)
            egs = [jnp.broadcast_to(jnp.exp(gl), (1, LANES)) for gl in c["gls"]]
            eg_ref[g * 8:(g + 1) * 8, :] = jnp.concatenate(egs + [jnp.zeros((8 - NH, LANES), F32)], axis=0)

    st = lambda w_: pl.BlockSpec((G * HS, w_), lambda i: (i, 0))
    one = [st(DH)] * 4 + [st(HS), pl.BlockSpec((G * 8, LANES), lambda i: (i, 0)), st(HS)]
    shp = [_S((n * HS, DH)), _S((n * HS, DH), BF16), _S((n * HS, DH), BF16), _S((n * HS, DH), BF16),
           _S((n * HS, HS), BF16), _S((n * 8, LANES)), _S((n * HS, HS))]
    res = _call(
        body, "dn_chunks_fwd", (n // G,),
        [_rows(G * CH, 3 * DN), _rows(G * CH, LANES), _full((2, CH, CH)), _full((1, LANES)), _full((1, LANES))],
        one * 2, shp * 2, rider=rider)(qkv, pab, csum, alr, dtr)
    outs, riding = (res, None) if rider is None else res
    parts = tuple(outs[:7]), tuple(outs[7:])
    return parts if rider is None else (parts, riding)


def _scan_plan(n, ncx):
    sg = 2 if n % 2 == 0 and ncx % 2 == 0 else 1
    ng, ncg = n // sg, ncx // sg
    return sg, ((lambda i: i), (lambda i: jnp.where(i < ncg, ncg - 1 - i, ng - 1 - (i - ncg))))


def _scan_specs(order, sg):
    st = lambda w_: pl.BlockSpec((sg * HS, w_), lambda i: (order(i), 0))
    return dict(st=st(DH), aqk=st(HS), eg=pl.BlockSpec((sg * 8, LANES), lambda i: (order(i), 0)),
                tok=pl.BlockSpec((sg * CH, DN), lambda i: (order(i), 0)),
                state=pl.BlockSpec((sg, DN, DH), lambda i: (order(i), 0, 0)))


def _scan_fwd(parts, T, tc, rider=None):
    n = T // CH
    sg, orders = _scan_plan(n, tc // CH)

    def body(*refs):
        S_f, S_b = refs[-2:]

        @pl.when(pl.program_id(0) == 0)
        def _():
            S_f[...] = jnp.zeros_like(S_f)
            S_b[...] = jnp.zeros_like(S_b)

        for g in range(sg):
            for d, S in enumerate((S_f, S_b)):
                u_ref, w_ref, ks_ref, qd_ref, aqk_ref, eg_ref = refs[6 * d:6 * d + 6]
                o_ref, ss_ref, vn_ref = refs[12 + 3 * d:15 + 3 * d]
                k = g if d == 0 else sg - 1 - g
                rows = slice(k * HS, (k + 1) * HS)
                ss_ref[k] = S[...]
                Sh = [S[_hs(h), :] for h in range(NH)]
                wh, ksh, qdh = _heads(w_ref[rows, :]), _heads(ks_ref[rows, :]), _heads(qd_ref[rows, :])
                vn = u_ref[rows, :] - jnp.concatenate([_dot(wh[h], Sh[h], NN) for h in range(NH)], axis=0)
                vn_ref[rows, :] = vn
                av, vnh = _heads(_dot(aqk_ref[rows, :], vn, NN)), _heads(vn)
                for h in range(NH):
                    o_ref[k * CH:(k + 1) * CH, _hs(h)] = _dot(qdh[h], Sh[h], NN) + av[h]
                    S[_hs(h), :] = Sh[h] * eg_ref[k * 8 + h:k * 8 + h + 1, :] + _dot(ksh[h], vnh[h], TN)

    ins, outs, shp = [], [], []
    for d in range(2):
        sp = _scan_specs(orders[d], sg)
        ins += [sp["st"]] * 4 + [sp["aqk"], sp["eg"]]
        outs += [sp["tok"], sp["state"], sp["st"]]
        shp += [_S((T, DN)), _S((n, DN, DH)), _S((n * HS, DH))]
    res = _call(body, "scan_fwd", (n // sg,), ins, outs, shp,
                scratch=[pltpu.VMEM((DN, DH), F32), pltpu.VMEM((DN, DH), F32)], rider=rider)(*parts[0][:6], *parts[1][:6])
    res, riding = (res, None) if rider is None else res
    out = tuple(res[:3]), tuple(res[3:])
    return out if rider is None else (out, riding)


def _scan_bwd(do, parts, tc):
    T = do.shape[0]
    n = T // CH
    sg, fwd_orders = _scan_plan(n, tc // CH)
    orders = [lambda s, f=f: f(n // sg - 1 - s) for f in fwd_orders]

    def body(*refs):
        dS_f, dS_b = refs[-2:]

        @pl.when(pl.program_id(0) == 0)
        def _():
            dS_f[...] = jnp.zeros_like(dS_f)
            dS_b[...] = jnp.zeros_like(dS_b)

        for g in range(sg):
            for d, dS in enumerate((dS_f, dS_b)):
                do_ref, w_ref, ks_ref, qd_ref, aqk_ref, eg_ref = refs[6 * d:6 * d + 6]
                dvn_ref, dss_ref = refs[12 + 2 * d:14 + 2 * d]
                k = sg - 1 - g if d == 0 else g
                rows = slice(k * HS, (k + 1) * HS)
                dss_ref[k] = dS[...]
                dSh = [dS[_hs(h), :] for h in range(NH)]
                wh, ksh, qdh = _heads(w_ref[rows, :]), _heads(ks_ref[rows, :]), _heads(qd_ref[rows, :])
                do_st = _stack(do_ref[k * CH:(k + 1) * CH, :])
                dvn = (_dot(aqk_ref[rows, :], do_st, TN)
                       + jnp.concatenate([_dot(ksh[h], dSh[h], NN) for h in range(NH)], axis=0))
                dvn_ref[rows, :] = dvn
                doh, dvnh = _heads(do_st), _heads(dvn)
                for h in range(NH):
                    dS[_hs(h), :] = (_dot(qdh[h], doh[h], TN) + dSh[h] * eg_ref[k * 8 + h:k * 8 + h + 1, :]
                                     - _dot(wh[h], dvnh[h], TN))

    ins, outs, shp, args = [], [], [], []
    for d in range(2):
        sp = _scan_specs(orders[d], sg)
        ins += [sp["tok"]] + [sp["st"]] * 3 + [sp["aqk"], sp["eg"]]
        outs += [sp["st"], sp["state"]]
        shp += [_S((n * HS, DH)), _S((n, DN, DH))]
        args += [do, *parts[d][1:6]]
    res = _call(body, "scan_bwd", (n // sg,), ins, outs, shp,
                scratch=[pltpu.VMEM((DN, DH), F32), pltpu.VMEM((DN, DH), F32)])(*args)
    return tuple(res[:2]), tuple(res[2:])


def _dn_chunks_bwd(qkv, pab, alr, dtr, do, fwd, bwd, rider=None):
    T = qkv.shape[0]
    n = T // CH
    G = _chunk_group(n)
    csum, csum_t = _scan_consts()

    def body(q_ref, p_ref, do_ref, cs_ref, cst_ref, al_ref, dt_ref, *refs):
        dq_refs, dp_refs, acc_ref = refs[14:16], refs[16:18], refs[18]

        @pl.when(pl.program_id(0) == 0)
        def _():
            acc_ref[...] = jnp.zeros_like(acc_ref)

        lane = _iota((CH, LANES), 1)
        inst = [(g, d) for g in range(G) for d in range(2)]
        cs, Ss, dS2s, dos, vns, dvns = [], [], [], [], [], []
        for g, d in inst:
            tok, rows = slice(g * CH, (g + 1) * CH), slice(g * HS, (g + 1) * HS)
            vn_ref, dvn_ref, ss_ref, dss_ref, tm_ref, u_ref, w_ref = refs[7 * d:7 * d + 7]
            cs.append(_dn_chunk_post(
                _dn_chunk_pre(q_ref[tok, :], p_ref[tok, :], al_ref[...], dt_ref[...], cs_ref[d], d), tm_ref[rows, :],
                uw=(u_ref[rows, :], w_ref[rows, :])))
            Ss.append([ss_ref[g, _hs(h), :] for h in range(NH)])
            dS2s.append([dss_ref[g, _hs(h), :] for h in range(NH)])
            dos.append(_stack(do_ref[tok, :]))
            vns.append(vn_ref[rows, :])
            dvns.append(dvn_ref[rows, :])
        for (g, d), c, (dq, dk, dv, dgc, dbeta) in zip(inst, cs, _dn_chunks_bwd_math(cs, Ss, dS2s, dos, vns, dvns)):
            tok = slice(g * CH, (g + 1) * CH)
            dgcm = jnp.zeros((CH, LANES), F32)
            dbm = jnp.zeros((CH, LANES), F32)
            for h, (a, b_, c_, e, f) in enumerate(zip(*map(_heads, (dq, dk, dv, dgc, dbeta)))):
                dq_refs[d][tok, _hs(h)] = a
                dq_refs[d][tok, _hs(NH + h)] = b_
                dq_refs[d][tok, _hs(2 * NH + h)] = c_
                dgcm = jnp.where(lane == d * NH + h, e, dgcm)
                dbm = jnp.where(lane == 8 + d * NH + h, f, dbm)
            dgm = _dot(cst_ref[d], dgcm, NN, hi=True)
            dsp = dgm * c["nexp"] * jax.nn.sigmoid(c["sp_in"])
            dp_refs[d][tok, :] = dsp + dbm * c["bm"] * (1.0 - c["bm"])
            acc_ref[0:1, :] += _colsum(dgm * c["gm"])
            acc_ref[1:2, :] += _colsum(dsp)

    st = pl.BlockSpec((G * HS, DH), lambda i: (i, 0))
    state = pl.BlockSpec((G, DN, DH), lambda i: (i, 0, 0))
    return _call(
        body, "dn_chunks_bwd", (n // G,),
        [_rows(G * CH, 3 * DN), _rows(G * CH, LANES), _rows(G * CH, DN), _full((2, CH, CH)), _full((2, CH, CH)),
         _full((1, LANES)), _full((1, LANES))]
        + [st, st, state, state, pl.BlockSpec((G * HS, HS), lambda i: (i, 0)), st, st] * 2,
        [_rows(G * CH, 3 * DN)] * 2 + [_rows(G * CH, LANES)] * 2 + [_full((8, LANES))],
        [_S((T, 3 * DN))] * 2 + [_S((T, LANES))] * 2 + [_S((8, LANES))], rider=rider)(
            qkv, pab, do, csum, csum_t, alr, dtr, *fwd, *bwd)


def _head_out(o, z, g):
    on = o * lax.rsqrt(jnp.mean(o * o, axis=-1, keepdims=True) + EPS) * g
    return on * _silu(z)


def _mix_branches(of_ref, ob_ref, z_ref, yp_ref, ys_ref, pg_ref, gdn_ref, wa_ref, wb_ref, wc_ref):
    ons, ya = [], None
    for h in range(NH):
        on = _head_out(of_ref[:, _hs(h)] + ob_ref[:, _hs(h)], z_ref[:, _hs(h)], gdn_ref[...])
        t = _dot(on, wa_ref[_hs(h), :], NN)
        ya = t if ya is None else ya + t
        ons.append(on)
    ys = [ya, _dot(yp_ref[...], wb_ref[...], NN), _dot(ys_ref[...], wc_ref[...], NN)]
    sg = [jax.nn.sigmoid(pg_ref[:, k * D:(k + 1) * D]) for k in range(3)]
    return ons, ys, sg


def _mix_fwd(X, of, ob, z, yp, ys, pg, mv, gdn, wa, wb, wc, wo, tc, tt):
    T = X.shape[0]

    def body(x_ref, of_ref, ob_ref, z_ref, yp_ref, ys_ref, pg_ref, mv_ref, gdn_ref, wa_ref, wb_ref, wc_ref, wo_ref,
             x1_ref):
        _, yb, sg = _mix_branches(of_ref, ob_ref, z_ref, yp_ref, ys_ref, pg_ref, gdn_ref, wa_ref, wb_ref, wc_ref)
        mix = _dot(sg[0] * yb[0] + sg[1] * yb[1] + sg[2] * yb[2], wo_ref[...], NN)
        _, gate = _stream_rows(mv_ref, pl.program_id(0), tt, tc, 2)
        x1_ref[...] = x_ref[...] + gate * mix

    return _call(
        body, "mix_fwd", (T // tt,),
        [_rows(tt, D), _rows(tt, DN), _rows(tt, DN), _rows(tt, DN), _rows(tt, PW), _rows(tt, PW), _rows(tt, 3 * D),
         _full((8, D)), _full((1, DH)), _full(wa.shape), _full(wb.shape), _full(wc.shape), _full(wo.shape)],
        _rows(tt, D), _S((T, D)))(X, of, ob, z, yp, ys, pg, mv, gdn, wa, wb, wc, wo)


def _mix_bwd(dx1, of, ob, z, yp, ys, pg, mv, gdn, wa, wb, wc, wo, tc, tt):
    T = dx1.shape[0]

    def body(dx_ref, of_ref, ob_ref, z_ref, yp_ref, ys_ref, pg_ref, mv_ref, gdn_ref, wa_ref, wb_ref, wc_ref, wo_ref,
             do_ref, dz_ref, dyp_ref, dys_ref, dpg_ref, dwa_ref, dwb_ref, dwc_ref, dwo_ref, dgdn_ref, dm_ref):
        i = pl.program_id(0)

        @pl.when(i == 0)
        def _():
            for r in (dwa_ref, dwb_ref, dwc_ref, dwo_ref, dgdn_ref, dm_ref):
                r[...] = jnp.zeros_like(r)

        ons, yb, sg = _mix_branches(of_ref, ob_ref, z_ref, yp_ref, ys_ref, pg_ref, gdn_ref, wa_ref, wb_ref, wc_ref)
        ymix = sg[0] * yb[0] + sg[1] * yb[1] + sg[2] * yb[2]
        isc, gate = _stream_rows(mv_ref, i, tt, tc, 2)
        dx = dx_ref[...]
        dmix = dx * gate
        _acc_stream(dm_ref, 2, isc, dx * _dot(ymix, wo_ref[...], NN))
        dwo_ref[...] += _dot(ymix, dmix, TN)
        dymix = _dot(dmix, wo_ref[...], NT)
        dyb = []
        for k in range(3):
            dyb.append(dymix * sg[k])
            dpg_ref[:, k * D:(k + 1) * D] = dymix * yb[k] * sg[k] * (1.0 - sg[k])
        dwb_ref[...] += _dot(yp_ref[...], dyb[1], TN)
        dwc_ref[...] += _dot(ys_ref[...], dyb[2], TN)
        dyp_ref[...] = _dot(dyb[1], wb_ref[...], NT)
        dys_ref[...] = _dot(dyb[2], wc_ref[...], NT)
        dg = jnp.zeros((1, DH), F32)
        for h in range(NH):
            dwa_ref[_hs(h), :] += _dot(ons[h], dyb[0], TN)
            don = _dot(dyb[0], wa_ref[_hs(h), :], NT)
            _, vjp = jax.vjp(_head_out, of_ref[:, _hs(h)] + ob_ref[:, _hs(h)], z_ref[:, _hs(h)], gdn_ref[...])
            do_h, dz_h, dg_h = vjp(don)
            do_ref[:, _hs(h)] = do_h
            dz_ref[:, _hs(h)] = dz_h
            dg = dg + dg_h
        dgdn_ref[...] += dg

    return _call(
        body, "mix_bwd", (T // tt,),
        [_rows(tt, D), _rows(tt, DN), _rows(tt, DN), _rows(tt, DN), _rows(tt, PW), _rows(tt, PW), _rows(tt, 3 * D),
         _full((8, D)), _full((1, DH)), _full(wa.shape), _full(wb.shape), _full(wc.shape), _full(wo.shape)],
        [_rows(tt, DN), _rows(tt, DN), _rows(tt, PW), _rows(tt, PW), _rows(tt, 3 * D),
         _full(wa.shape), _full(wb.shape), _full(wc.shape), _full(wo.shape), _full((1, DH)), _full((8, D))],
        [_S((T, DN)), _S((T, DN)), _S((T, PW)), _S((T, PW)), _S((T, 3 * D)),
         _S(wa.shape), _S(wb.shape), _S(wc.shape), _S(wo.shape), _S((1, DH)), _S((8, D))])(
            dx1, of, ob, z, yp, ys, pg, mv, gdn, wa, wb, wc, wo)


def _ffn_fwd(X1, mv, g, wgu, wd, tc, tt):
    T = X1.shape[0]

    def body(x_ref, mv_ref, g_ref, wgu_ref, wd_ref, x2_ref, ff_ref):
        i = pl.program_id(0)
        _, sh = _stream_rows(mv_ref, i, tt, tc, 0)
        _, sc = _stream_rows(mv_ref, i, tt, tc, 1)
        _, gate = _stream_rows(mv_ref, i, tt, tc, 2)
        x = x_ref[...]
        gu = _dot(_modulate(x, g_ref[...], sh, sc), wgu_ref[...], NN)
        ff = _dot(_silu(gu[:, :DFF]) * gu[:, DFF:], wd_ref[...], NN)
        ff_ref[...] = ff
        x2_ref[...] = x + gate * ff

    return _call(
        body, "ffn_fwd", (T // tt,),
        [_rows(tt, D), _full((8, D)), _full((1, D)), _full(wgu.shape), _full(wd.shape)],
        [_rows(tt, D)] * 2, [_S((T, D))] * 2)(X1, mv, g, wgu, wd)


def _ffn_bwd(X1, ff, dx2, mv, g, wgu, wd, tc, tt, rider=None):
    T = X1.shape[0]

    def body(x_ref, ff_ref, dx2_ref, mv_ref, g_ref, wgu_ref, wd_ref, dx1_ref, h_ref, dgu_ref, act_ref, dff_ref, dg_ref,
             dm_ref):
        i = pl.program_id(0)
        isc, sh = _stream_rows(mv_ref, i, tt, tc, 0)
        _, sc = _stream_rows(mv_ref, i, tt, tc, 1)
        _, gate = _stream_rows(mv_ref, i, tt, tc, 2)
        x, dx2_ = x_ref[...], dx2_ref[...]
        h, vjp = jax.vjp(_modulate, x, g_ref[...], sh, sc)
        hb = h.astype(BF16)
        h_ref[...] = hb
        gu = jnp.dot(hb, wgu_ref[...], preferred_element_type=F32)
        ga, up = gu[:, :DFF], gu[:, DFF:]
        sg = jax.nn.sigmoid(ga)
        act = (ga * sg * up).astype(BF16)
        act_ref[...] = act
        dff = dx2_ * gate
        dff_ref[...] = dff.astype(BF16)
        dact = _dot(dff, wd_ref[...], NT)
        dga = (dact * up * (sg * (1.0 + ga * (1.0 - sg)))).astype(BF16)
        dup = (dact * ga * sg).astype(BF16)
        dgu_ref[:, :DFF] = dga
        dgu_ref[:, DFF:] = dup
        dh = _dot(dga, wgu_ref[:, :DFF], NT) + _dot(dup, wgu_ref[:, DFF:], NT)
        dx, dg, dsh, dsc = vjp(dh)
        dx1_ref[...] = dx2_ + dx

        @pl.when(i == 0)
        def _():
            dg_ref[...] = jnp.zeros_like(dg_ref)
            dm_ref[...] = jnp.zeros_like(dm_ref)

        dg_ref[...] += dg
        _acc_stream(dm_ref, 0, isc, dsh)
        _acc_stream(dm_ref, 1, isc, dsc)
        _acc_stream(dm_ref, 2, isc, dx2_ * ff_ref[...])

    return _call(
        body, "ffn_bwd", (T // tt,),
        [_rows(tt, D), _rows(tt, D), _rows(tt, D), _full((8, D)), _full((1, D)), _full(wgu.shape), _full(wd.shape)],
        [_rows(tt, D), _rows(tt, D), _rows(tt, 2 * DFF), _rows(tt, DFF), _rows(tt, D), _full((1, D)), _full((8, D))],
        [_S((T, D)), _S((T, D), BF16), _S((T, 2 * DFF), BF16), _S((T, DFF), BF16), _S((T, D), BF16),
         _S((1, D)), _S((8, D))], rider=rider)(X1, ff, dx2, mv, g, wgu, wd)


def _rms(x, g):
    return x * lax.rsqrt(jnp.mean(x * x, axis=-1, keepdims=True) + EPS) * g


def _loss_head(X2, tgt, gf, tc):
    T = X2.shape[0]

    def body(x_ref, t_ref, g_ref, dx_ref, loss_ref, dg_ref):
        i = pl.program_id(0)

        @pl.when(i == 0)
        def _():
            dx_ref[...] = jnp.zeros_like(dx_ref)
            loss_ref[...] = jnp.zeros_like(loss_ref)
            dg_ref[...] = jnp.zeros_like(dg_ref)

        @pl.when(i > 0)
        def _():
            y, vjp = jax.vjp(_rms, x_ref[...], g_ref[...])
            err = y - t_ref[...]
            dx, dg = vjp(err * (1.0 / D))
            dx_ref[...] = dx
            dg_ref[...] += dg
            loss_ref[...] += (0.5 / D) * jnp.sum(jnp.sum(err * err, axis=1, keepdims=True), axis=0, keepdims=True)

    return _call(
        body, "loss_head", (T // tc,),
        [_rows(tc, D), pl.BlockSpec((tc, D), lambda i: (jnp.maximum(i - 1, 0), 0)), _full((1, D))],
        [_rows(tc, D), _full((8, LANES)), _full((1, D))],
        [_S((T, D)), _S((8, LANES)), _S((1, D))])(X2, tgt, gf)


def _block_diag(pw):
    g, n = pw.shape[0], pw.shape[1]
    out = jnp.zeros((g * n, g * n), pw.dtype)
    for k in range(g):
        out = lax.dynamic_update_slice(out, pw[k], (k * n, k * n))
    return out


IN_TRUE = tuple(IN_BOUNDS[k + 1] - IN_BOUNDS[k] for k in range(8))


def _overlaps(widths, cw):
    starts = np.cumsum([0] + list(widths))
    out = []
    for k in range(N_DEV):
        for i in range(len(widths)):
            a, b = max(k * cw, starts[i]), min((k + 1) * cw, starts[i + 1])
            if a < b:
                out.append((k, i, int(a - k * cw), int(a - starts[i]), int(b - a)))
    return out


def _shards_to_cols(gathered, l, widths, padded):
    nd, _, R, cw = gathered.shape
    tr = _shard_rows(R)

    def body(s_ref, *o_refs):
        for i, o_ref in enumerate(o_refs):
            if padded[i] > widths[i]:
                o_ref[...] = jnp.zeros_like(o_ref)
        for k, i, so, go, n in _overlaps(widths, cw):
            o_refs[i][:, go:go + n] = s_ref[k, 0, :, so:so + n].astype(BF16)

    return _call(
        body, "shards_to_cols", (R // tr,), [pl.BlockSpec((nd, 1, tr, cw), lambda i: (0, l, i, 0))],
        [_rows(tr, p) for p in padded], [_S((R, p), BF16) for p in padded])(gathered)


def _cols_to_shards(groups, widths, cw):
    R = groups[0].shape[0]
    tr = _shard_rows(R)

    def body(*refs):
        o_ref = refs[-1]
        for k, i, so, go, n in _overlaps(widths, cw):
            o_ref[k % 2, k // 2, 0, :, so:so + n] = refs[i][:, go:go + n]

    return _call(
        body, "cols_to_shards", (R // tr,), [_rows(tr, g.shape[1]) for g in groups],
        pl.BlockSpec((2, N_CHIP, 1, tr, cw), lambda i: (0, 0, 0, i, 0)), _S((2, N_CHIP, 1, R, cw)))(*groups)


def _mod_rows(mods_l, k0):
    rows = [mods_l[s, (k0 + k) * D:(k0 + k + 1) * D] for s in (0, 1) for k in range(3)]
    return jnp.stack(rows + [jnp.zeros((D,), F32)] * 2)


def _lane_row(v8):
    return jnp.pad(v8.reshape(1, 8), ((0, 0), (0, LANES - 8)))


LAYERED = ("w_in", "w_br_a", "w_br_b", "w_br_c", "w_o", "w_gu", "w_down")
LATE = ("w_br_a", "w_br_b", "w_br_c", "w_o", "w_gu", "w_down")


def _device_step(x, c, ctx, tgt, wts, tt, comm=None):
    tc = ctx.shape[0]
    X = jnp.concatenate([ctx, x], axis=0)
    if comm is None:
        row = 0
        cc = jnp.concatenate([c, jnp.zeros((CTX_ROW - 1, D), F32), wts["c_ctx"][None, :],
                              jnp.zeros((CC_ROWS - CTX_ROW - 1, D), F32)], axis=0)
        w_ada = wts["w_ada"].astype(BF16)
        mods16 = _mod_fwd(cc, w_ada, wts["b_ada"].reshape(NL, 1, 6 * D))
    else:
        row, mods16 = comm.adaln_fwd(c, wts["c_ctx"])
    mods = jnp.stack([mods16[:, CTX_ROW], lax.dynamic_index_in_dim(mods16, row, 1, keepdims=False)], axis=1)

    saved = []
    for l in range(NL):
        ws = _shards_to_cols(wts["w_in"], l, IN_TRUE, IN_WIDTHS)
        mv1, mv2 = _mod_rows(mods[l], 0), _mod_rows(mods[l], 3)
        g1, g2 = wts["norm1_g"][l][None, :], wts["norm2_g"][l][None, :]
        cw, scw = wts["dn_conv_w"][l], wts["sc_conv_w"][l]
        alr, dtr = _lane_row(wts["dn_a_log"][l]), _lane_row(wts["dn_dt_bias"][l])
        gdn = wts["dn_norm_g"][l][None, :]
        pwbd, ps = _block_diag(wts["pool_w"][l]), wts["pool_scale"][l][None, :]
        hb, pq, pz, pab, pp, sx, sb, sc_, pg = _inproj_fwd(X, mv1, g1, ws, tc, tt)
        qkv = _dnprep_fwd(pq, cw, tc, tt)
        if comm is not None and l == 0:
            parts, riding = _dn_chunks_fwd(qkv, pab, alr, dtr, rider=comm.late_weights_chips())
            ((of, ssf, vnf), (ob, ssb, vnb)), riding = _scan_fwd(parts, X.shape[0], tc,
                                                                 rider=comm.late_weights_pair(riding))
            wts = dict(wts, **comm.late_weights(riding))
        else:
            parts = _dn_chunks_fwd(qkv, pab, alr, dtr)
            (of, ssf, vnf), (ob, ssb, vnb) = _scan_fwd(parts, X.shape[0], tc)
        wbr = [_shards_to_cols(wts[k], l, (2 * DFF,), (2 * DFF,))[0] if k == "w_gu" else wts[k][l].astype(BF16)
               for k in LATE]
        yp = _pool_fwd(pp, pwbd, ps, tc)
        ys = _sc_fwd(sx, sb, sc_, scw, tc, tt)
        X1 = _mix_fwd(X, of, ob, pz, yp, ys, pg, mv1, gdn, *wbr[:4], tc, tt)
        X2, ff = _ffn_fwd(X1, mv2, g2, wbr[4], wbr[5], tc, tt)
        saved.append(dict(X=X, X1=X1, ff=ff, ws=ws, wbr=wbr, mv1=mv1, mv2=mv2, g1=g1, g2=g2, cw=cw, scw=scw, alr=alr, dtr=dtr,
                          gdn=gdn, pwbd=pwbd, ps=ps, hb=hb, pq=pq, pz=pz, pab=pab, pp=pp, sx=sx, sb=sb, sc=sc_, pg=pg,
                          qkv=qkv, of=of, ob=ob, ssf=ssf, ssb=ssb, vnf=vnf, vnb=vnb, parts=parts, yp=yp, ys=ys))
        X = X2

    dX, loss, dgf = _loss_head(X, tgt, wts["final_norm_g"][None, :], tc)

    gl = {k: [None] * NL for k in ("w_in", "norm1_g", "norm2_g", "dn_conv_w", "dn_a_log", "dn_dt_bias", "dn_norm_g",
                                   "pool_w", "pool_scale", "sc_conv_w", "w_br_a", "w_br_b", "w_br_c", "w_o", "w_gu",
                                   "w_down")}
    dmods = [None] * NL
    early = None
    for l in reversed(range(NL)):
        s = saved[l]
        hide = comm is not None and l == 0
        res = _ffn_bwd(s["X1"], s["ff"], dX, s["mv2"], s["g2"], s["wbr"][4], s["wbr"][5], tc, tt,
                       rider=comm.grad_pair_rider([gl[k][1] for k in LAYERED]) if hide else None)
        if hide:
            res, got = res
            chip_rider = comm.grad_chip_rider(got)
        dx1, h2, dgu, act, dff, dg2, dm2 = res
        gl["w_gu"][l] = _cols_to_shards([_dw(h2, dgu, tt)], (2 * DFF,), 2 * DFF // N_DEV)
        gl["w_down"][l] = _dw(act, dff, tt)
        do, dz, dyp, dys, dpg, dwa, dwb, dwc, dwo, dgdn, dmg = _mix_bwd(
            dx1, s["of"], s["ob"], s["pz"], s["yp"], s["ys"], s["pg"], s["mv1"], s["gdn"], *s["wbr"][:4], tc, tt)
        dpp, dpw, dps = _pool_bwd(s["pp"], s["pwbd"], s["ps"], dyp, tc)
        dsx, dsb, dsc, dscw = _sc_bwd(s["sx"], s["sb"], s["sc"], s["scw"], dys, tc, tt)
        (dvnf, dssf), (dvnb, dssb) = _scan_bwd(do, s["parts"], tc)
        res = _dn_chunks_bwd(s["qkv"], s["pab"], s["alr"], s["dtr"], do,
                             (s["vnf"], dvnf, s["ssf"], dssf, s["parts"][0][6], *s["parts"][0][:2]),
                             (s["vnb"], dvnb, s["ssb"], dssb, s["parts"][1][6], *s["parts"][1][:2]),
                             rider=chip_rider if hide else None)
        if hide:
            res, early = res
            early = comm.grad_chip_done(early)
        dqf, dqb, dpf, dpb, gacc = res
        dy = _dnprep_bwd_act(s["pq"], s["cw"], dqf, dqb, tc, tt)
        dpq, dcw = _conv_bwd(dy, s["pq"], s["cw"], tc, tt)
        dps_ = [dpq, dz, dpf, dpb, dpp, dsx, dsb, dsc, dpg]
        dp_w = [0, 1, 2, 2, 3, 4, 5, 6, 7]
        dX, dg1, dm1 = _inproj_bwd(s["X"], s["mv1"], s["g1"], s["ws"], dps_, dp_w, dx1, tc, tt)
        dws = [_dw(s["hb"], dp, tt) for dp in (dpq, dz, dpf + dpb, dpp, dsx, dsb, dsc, dpg)]
        gl["w_in"][l] = _cols_to_shards(dws, IN_TRUE, IN_BOUNDS[-1] // N_DEV)
        gl["norm1_g"][l], gl["norm2_g"][l] = dg1[0], dg2[0]
        gl["dn_conv_w"][l], gl["sc_conv_w"][l] = dcw, dscw
        gl["dn_a_log"][l], gl["dn_dt_bias"][l] = gacc[0, :8].reshape(2, NH), gacc[1, :8].reshape(2, NH)
        gl["dn_norm_g"][l] = dgdn[0]
        gl["pool_w"][l] = jnp.stack([dpw[k * GW:(k + 1) * GW, k * GW:(k + 1) * GW] for k in range(4)])
        gl["pool_scale"][l] = dps[0]
        gl["w_br_a"][l], gl["w_br_b"][l], gl["w_br_c"][l], gl["w_o"][l] = dwa, dwb, dwc, dwo
        dm = dm1 + dmg
        cat = lambda r: jnp.concatenate([dm[r], dm[r + 1], dm[r + 2], dm2[r], dm2[r + 1], dm2[r + 2]])
        dmods[l] = jnp.stack([cat(0), cat(3)])

    dmods = jnp.stack(dmods)
    grads = {k: (v if k in LAYERED else jnp.stack(v)) for k, v in gl.items()}
    if comm is None:
        dm16 = jnp.zeros((NL, CC_ROWS, 6 * D), F32).at[:, CTX_ROW].set(dmods[:, 0]).at[:, row].set(dmods[:, 1])
        dwada, dcc = _mod_bwd(cc, w_ada, dm16)
        grads.update(w_ada=dwada, b_ada=dmods[:, 0] + dmods[:, 1], c_ctx=dcc[CTX_ROW])
    else:
        grads.update(comm.adaln_bwd(dmods))
    grads.update(final_norm_g=dgf[0])
    return loss, dX[tc:], grads, early


def _me():
    return lax.axis_index("x"), lax.axis_index("y"), lax.axis_index("c")


def _dev_index(p):
    return 4 * p[0] + 2 * p[1] + p[2]


def _allgather(parts):
    n = len(parts)

    def body(*refs):
        ins, outs = refs[:n], refs[n:2 * n]
        send_sems, recv_sems = refs[2 * n:]
        x, y, c = _me()
        me, sibling = (x, y, c), (x, y, 1 - c)
        chips = [(1 - x, y), (x, 1 - y), (1 - x, 1 - y)]

        def copy(a, k, block, to, src=None):
            dst = outs[a].at[_dev_index(block)]
            return pltpu.make_async_remote_copy(
                src_ref=dst if src is None else src, dst_ref=dst, send_sem=send_sems.at[a, k], recv_sem=recv_sems.at[a, k],
                device_id=to, device_id_type=MESH_ID)

        first, passed = [], []
        for a in range(n):
            first.append(copy(a, 0, me, sibling, src=ins[a]))
            first += [copy(a, 1 + j, me, (*chip, c), src=ins[a]) for j, chip in enumerate(chips)]
        for cp in first:
            cp.start()
        for a in range(n):
            for j, chip in enumerate(chips):
                copy(a, 1 + j, (*chip, c), me).wait_recv()
                passed.append(copy(a, 4 + j, (*chip, c), sibling))
                passed[-1].start()
        for a in range(n):
            copy(a, 0, sibling, me).wait_recv()
            for j, chip in enumerate(chips):
                copy(a, 4 + j, (*chip, 1 - c), me).wait_recv()
        for cp in first + passed:
            cp.wait_send()

    outs = pl.pallas_call(
        body, name="allgather", in_specs=[HBM_SPEC] * n, out_specs=[HBM_SPEC] * n,
        out_shape=[_S((N_DEV,) + p.shape, p.dtype) for p in parts],
        scratch_shapes=[pltpu.SemaphoreType.DMA((n, 7)), pltpu.SemaphoreType.DMA((n, 7))],
    )(*parts)
    return [_with_own(o, p, _dev_index(_me())) for o, p in zip(outs, parts)]


def _with_own(gathered, own, index):
    return lax.dynamic_update_index_in_dim(gathered, own, index, 0)


def _broadcast_small(small, name="small_exchange"):
    def body(in_ref, out_ref, send_sems, recv_sems, local_sem):
        x, y, c = _me()
        my = _dev_index((x, y, c))
        mine = pltpu.make_async_copy(in_ref, out_ref.at[my], local_sem)
        mine.start()
        remote = []
        for k in range(1, N_DEV):
            cp = pltpu.make_async_remote_copy(
                src_ref=in_ref, dst_ref=out_ref.at[my], send_sem=send_sems.at[k - 1], recv_sem=recv_sems.at[k - 1],
                device_id=(x ^ (k >> 2), y ^ ((k >> 1) & 1), c ^ (k & 1)), device_id_type=MESH_ID)
            cp.start()
            remote.append(cp)
        for cp in remote:
            cp.wait_recv()
        for cp in remote:
            cp.wait_send()
        mine.wait()

    return pl.pallas_call(
        body, name=name, in_specs=[HBM_SPEC], out_specs=HBM_SPEC,
        out_shape=_S((N_DEV,) + small.shape, small.dtype),
        scratch_shapes=[pltpu.SemaphoreType.DMA((7,)), pltpu.SemaphoreType.DMA((7,)), pltpu.SemaphoreType.DMA],
    )(small)


def _run_rider(rider, name):
    ni, no = len(rider.ins), len(rider.out_shapes)

    def body(*refs):
        riding = (refs[:ni], refs[ni:ni + no], refs[ni + no:])
        rider.start(*riding)
        rider.wait(*riding)

    return list(pl.pallas_call(
        body, name=name, in_specs=[HBM_SPEC] * ni, out_specs=[HBM_SPEC] * no, out_shape=rider.out_shapes,
        scratch_shapes=rider.sems)(*rider.ins))


def _chip_peers(x, y):
    return [(k - 1, (x ^ (k >> 1), y ^ (k & 1))) for k in range(1, N_CHIP)]


def _pair_exchange(g2s):
    n = len(g2s)

    def copies(ins, outs, sems):
        x, y, c = _me()
        return [pltpu.make_async_remote_copy(
            src_ref=ins[a].at[1 - c, j], dst_ref=outs[a].at[j], send_sem=sems[0].at[a, j], recv_sem=sems[1].at[a, j],
            device_id=(x, y, 1 - c), device_id_type=MESH_ID) for a in range(n) for j in range(N_CHIP)], []

    return _Rider(g2s, [_S(g.shape[1:], g.dtype) for g in g2s],
                  [pltpu.SemaphoreType.DMA((n, N_CHIP)), pltpu.SemaphoreType.DMA((n, N_CHIP))], copies)


def _my_chip():
    x, y, _ = _me()
    return 2 * x + y


def _chip_exchange(s4s):
    n = len(s4s)

    def copies(ins, outs, sems):
        x, y, c = _me()
        my = 2 * x + y
        return [pltpu.make_async_remote_copy(
            src_ref=ins[a].at[2 * px + py], dst_ref=outs[a].at[my], send_sem=sems[0].at[a, k], recv_sem=sems[1].at[a, k],
            device_id=(px, py, c), device_id_type=MESH_ID) for k, (px, py) in _chip_peers(x, y) for a in range(n)], []

    return _Rider(s4s, [_S(s.shape, s.dtype) for s in s4s],
                  [pltpu.SemaphoreType.DMA((n, N_CHIP - 1)), pltpu.SemaphoreType.DMA((n, N_CHIP - 1))], copies)


def _chip_exchange_done(s4s, recvs):
    my = _my_chip()
    return [_with_own(r, lax.dynamic_index_in_dim(s, my, 0, keepdims=False), my) for s, r in zip(s4s, recvs)]


def _chip_gather(arrs):
    n = len(arrs)

    def copies(ins, outs, sems):
        x, y, c = _me()
        return [pltpu.make_async_remote_copy(
            src_ref=ins[a], dst_ref=outs[a].at[2 * x + y], send_sem=sems[0].at[a, k], recv_sem=sems[1].at[a, k],
            device_id=(px, py, c), device_id_type=MESH_ID) for k, (px, py) in _chip_peers(x, y) for a in range(n)], []

    return _Rider(arrs, [_S((N_CHIP,) + a.shape, a.dtype) for a in arrs],
                  [pltpu.SemaphoreType.DMA((n, N_CHIP - 1)), pltpu.SemaphoreType.DMA((n, N_CHIP - 1))], copies)


def _pair_gather(chips):
    n = len(chips)

    def copies(ins, outs, sems):
        x, y, c = _me()
        return [pltpu.make_async_remote_copy(
            src_ref=ins[a].at[j], dst_ref=outs[a].at[j], send_sem=sems[0].at[a, j], recv_sem=sems[1].at[a, j],
            device_id=(x, y, 1 - c), device_id_type=MESH_ID) for a in range(n) for j in range(N_CHIP)], []

    return _Rider(chips, [_S(a.shape, a.dtype) for a in chips],
                  [pltpu.SemaphoreType.DMA((n, N_CHIP)), pltpu.SemaphoreType.DMA((n, N_CHIP))], copies)


def _shard_rows(r):
    return 256 if r % 256 == 0 else r


def _pair_sum(g2, got):
    _, nc, L, R, C = g2.shape
    tr = _shard_rows(R)

    def body(a_ref, b_ref, o_ref):
        o_ref[...] = (a_ref[0] + b_ref[...]).astype(BF16)

    blk = pl.BlockSpec((1, 1, tr, C), lambda j, l, i: (j, l, i, 0))
    return _call(
        body, "pair_sum", (nc, L, R // tr),
        [pl.BlockSpec((1, 1, 1, tr, C), lambda j, l, i: (lax.axis_index("c"), j, l, i, 0)), blk], blk,
        _S(got.shape, BF16))(g2, got)


def _adam(w, g, m, v):
    m2 = ADAM_B1 * m + (1.0 - ADAM_B1) * g
    v2 = ADAM_B2 * v + (1.0 - ADAM_B2) * (g * g)
    m_hat = m2 / (1.0 - ADAM_B1 ** ADAM_STEP)
    v_hat = v2 / (1.0 - ADAM_B2 ** ADAM_STEP)
    return -ADAM_LR * (m_hat / (jnp.sqrt(v_hat) + ADAM_EPS) + ADAM_WD * w), m2, v2


def _sum_adam(recvs, w, m, v):
    L, R, C = w.shape
    tr = _shard_rows(R)
    nr = len(recvs)

    def body(*refs):
        w_ref, m_ref, v_ref, g_ref, d_ref, m2_ref, v2_ref = refs[nr:]
        g = None
        for li, r_ref in enumerate(refs[:nr]):
            s = r_ref[0, 0].astype(F32)
            for j in range(1, N_CHIP):
                s = s + r_ref[j, 0].astype(F32)
            g = s if g is None else jnp.where(pl.program_id(0) == li, s, g)
        g_ref[0] = g
        d_ref[0], m2_ref[0], v2_ref[0] = _adam(w_ref[0], g, m_ref[0], v_ref[0])

    blk = pl.BlockSpec((1, tr, C), lambda l, i: (l, i, 0))
    rspec = pl.BlockSpec((N_CHIP, 1, tr, C), (lambda l, i: (0, l, i, 0)) if nr == 1 else (lambda l, i: (0, 0, i, 0)))
    return _call(body, "sum_adam", (L, R // tr), [rspec] * nr + [blk, blk, blk], [blk] * 4, [_S(w.shape)] * 4)(
        *recvs, w, m, v)


def _adam_big(w, g, m, v):
    L, R, C = w.shape
    tr = _shard_rows(R)

    def body(w_ref, g_ref, m_ref, v_ref, d_ref, m2_ref, v2_ref):
        d_ref[0], m2_ref[0], v2_ref[0] = _adam(w_ref[0], g_ref[0], m_ref[0], v_ref[0])

    blk = pl.BlockSpec((1, tr, C), lambda l, i: (l, i, 0))
    return _call(body, "adam_big", (L, R // tr), [blk] * 4, [blk] * 3, [_S(w.shape)] * 3)(w, g, m, v)


def _sum_small(recv):
    def body(r_ref, o_ref):
        g = r_ref[0]
        for k in range(1, recv.shape[0]):
            g = g + r_ref[k]
        o_ref[...] = g

    return pl.pallas_call(body, name="sum_small", out_shape=_S(recv.shape[1:]))(recv)


def _adam_small(w, g, m, v):
    def body(w_ref, g_ref, m_ref, v_ref, d_ref, m2_ref, v2_ref):
        d_ref[...], m2_ref[...], v2_ref[...] = _adam(w_ref[...], g_ref[...], m_ref[...], v_ref[...])

    return pl.pallas_call(body, name="adam_small", out_shape=[_S(w.shape)] * 3)(w, g, m, v)


def _pack(arrs, dtype, row_mult):
    parts, offs, r = [], [], 0
    for a in arrs:
        nr = -(-a.size // LANES)
        parts.append(jnp.pad(a.reshape(-1).astype(dtype), (0, nr * LANES - a.size)))
        offs.append(r)
        r += nr
    pad = (-r) % row_mult
    if pad:
        parts.append(jnp.zeros((pad * LANES,), dtype))
    return jnp.concatenate(parts).reshape(r + pad, LANES), offs


def _unpack(packed, offs, shapes, lead=()):
    out = []
    for off, shp in zip(offs, shapes):
        size = int(np.prod(shp))
        nr = -(-size // LANES)
        flat = packed[..., off:off + nr, :].reshape(lead + (nr * LANES,))
        out.append(flat[..., :size].reshape(lead + tuple(shp)))
    return out


BIG = (("w_ada", 2), ("w_in", 2), ("w_br_a", 2), ("w_br_b", 2), ("w_br_c", 2), ("w_o", 1), ("w_gu", 2), ("w_down", 1))
CONV = ("dn_conv_w", "sc_conv_w")
REPL = ("c_ctx", "b_ada", "norm1_g", "norm2_g", "dn_a_log", "dn_dt_bias", "dn_norm_g", "pool_w", "pool_scale",
        "final_norm_g")
WEIGHTS = ("c_ctx", "w_ada", "b_ada", "norm1_g", "norm2_g", "w_in", "dn_conv_w", "dn_a_log", "dn_dt_bias", "dn_norm_g",
           "pool_w", "pool_scale", "sc_conv_w", "w_br_a", "w_br_b", "w_br_c", "w_o", "w_gu", "w_down", "final_norm_g")
TOKEN_TILE = 256


def _join(blocks, axis):
    nd, nl, r, c = blocks.shape
    if axis == 2:
        return blocks.transpose(1, 2, 0, 3).reshape(nl, r, nd * c)
    return blocks.transpose(1, 0, 2, 3).reshape(nl, nd * r, c)


def _split(full, axis):
    nl, r, c = full.shape
    if axis == 2:
        return full.reshape(nl, r, N_CHIP, 2, c // N_DEV).transpose(3, 2, 0, 1, 4)
    return full.reshape(nl, N_CHIP, 2, r // N_DEV, c).transpose(2, 1, 0, 3, 4)


PRESPLIT = ("w_in", "w_gu")


def _presplit(layer_grads):
    return [g if k in PRESPLIT else _split(g[None], dict(BIG)[k]) for k, g in zip(LAYERED, layer_grads)]


class _Comm:
    def __init__(self, late_shards, w_ada, b_ada):
        self.packed = [k for k in LATE if k not in PRESPLIT]
        self.shapes = [late_shards[k].shape for k in self.packed]
        pack, self.offs = _pack([late_shards[k] for k in self.packed], BF16, BF16_ROWS)
        self.late = [pack, late_shards["w_gu"].astype(BF16)]
        self.w_ada, self.b_ada = w_ada.astype(BF16), b_ada
        self.g2s = None

    def adaln_fwd(self, c, c_ctx):
        my = _dev_index(_me())
        ncol = self.w_ada.shape[2]
        c_all = _broadcast_small(c.reshape(8, LANES), "c_exchange").reshape(N_DEV, D)
        self.cc = jnp.concatenate([c_all, c_ctx[None, :], jnp.zeros((CC_ROWS - N_DEV - 1, D), F32)], axis=0)
        b_cols = lax.dynamic_slice_in_dim(self.b_ada, my * ncol, ncol, axis=1).reshape(NL, 1, ncol)
        cols = _mod_fwd(self.cc, self.w_ada, b_cols)
        got = _broadcast_small(cols.reshape(-1, LANES), "mods_exchange").reshape(N_DEV, NL, CC_ROWS, ncol)
        return my, got.transpose(1, 2, 0, 3).reshape(NL, CC_ROWS, N_DEV * ncol)

    def adaln_bwd(self, dmods):
        my = _dev_index(_me())
        ncol = self.w_ada.shape[2]
        got = _broadcast_small(dmods.reshape(-1, LANES), "dmods_exchange")
        rows = got.reshape(N_DEV, NL, 2, 6 * D)
        ctx_sum = _sum_small(rows[:, :, 0].reshape(N_DEV, -1, LANES)).reshape(NL, 1, 6 * D)
        db = _sum_small(rows.transpose(0, 2, 1, 3).reshape(2 * N_DEV, -1, LANES)).reshape(NL, 6 * D)
        dm = jnp.concatenate([rows[:, :, 1].transpose(1, 0, 2), ctx_sum,
                              jnp.zeros((NL, CC_ROWS - N_DEV - 1, 6 * D), F32)], axis=1)
        dw, dcc = _mod_bwd(self.cc, self.w_ada, lax.dynamic_slice_in_dim(dm, my * ncol, ncol, axis=2))
        return dict(w_ada=dw, b_ada=db, c_ctx=dcc[CTX_ROW])

    def late_weights_chips(self):
        return _chip_gather(self.late)

    def late_weights_pair(self, riding):
        self.chips = [_with_own(r, a, _my_chip()) for r, a in zip(riding, self.late)]
        return _pair_gather(self.chips)

    def late_weights(self, riding):
        on_south = lax.axis_index("c") == 0
        by_dev = []
        for mine, other in zip(self.chips, riding):
            both = jnp.stack([jnp.where(on_south, mine, other), jnp.where(on_south, other, mine)], axis=1)
            by_dev.append(both.reshape((N_DEV,) + mine.shape[1:]))
        shards = _unpack(by_dev[0], self.offs, self.shapes, (N_DEV,))
        return dict({k: _join(blocks, dict(BIG)[k]) for k, blocks in zip(self.packed, shards)}, w_gu=by_dev[1])

    def grad_pair_rider(self, layer_grads):
        self.g2s = _presplit(layer_grads)
        return _pair_exchange(self.g2s)

    def grad_chip_rider(self, got):
        self.sums = [_pair_sum(g2, gt) for g2, gt in zip(self.g2s, got)]
        return _chip_exchange(self.sums)

    def grad_chip_done(self, riding):
        return _chip_exchange_done(self.sums, riding)


def kernel(x, c, ctx, c_ctx, w_ada, b_ada, norm1_g, norm2_g, w_in, dn_conv_w, dn_a_log, dn_dt_bias, dn_norm_g, pool_w, pool_scale, sc_conv_w, w_br_a, w_br_b, w_br_c, w_o, w_gu, w_down, final_norm_g, loss_target, m_c_ctx, m_w_ada, m_b_ada, m_norm1_g, m_norm2_g, m_w_in, m_dn_conv_w, m_dn_a_log, m_dn_dt_bias, m_dn_norm_g, m_pool_w, m_pool_scale, m_sc_conv_w, m_w_br_a, m_w_br_b, m_w_br_c, m_w_o, m_w_gu, m_w_down, m_final_norm_g, v_c_ctx, v_w_ada, v_b_ada, v_norm1_g, v_norm2_g, v_w_in, v_dn_conv_w, v_dn_a_log, v_dn_dt_bias, v_dn_norm_g, v_pool_w, v_pool_scale, v_sc_conv_w, v_w_br_a, v_w_br_b, v_w_br_c, v_w_o, v_w_gu, v_w_down, v_final_norm_g):
    loc = dict(c_ctx=c_ctx, w_ada=w_ada, b_ada=b_ada, norm1_g=norm1_g, norm2_g=norm2_g, w_in=w_in, dn_conv_w=dn_conv_w,
               dn_a_log=dn_a_log, dn_dt_bias=dn_dt_bias, dn_norm_g=dn_norm_g, pool_w=pool_w, pool_scale=pool_scale,
               sc_conv_w=sc_conv_w, w_br_a=w_br_a, w_br_b=w_br_b, w_br_c=w_br_c, w_o=w_o, w_gu=w_gu, w_down=w_down,
               final_norm_g=final_norm_g)
    mom_m = dict(c_ctx=m_c_ctx, w_ada=m_w_ada, b_ada=m_b_ada, norm1_g=m_norm1_g, norm2_g=m_norm2_g, w_in=m_w_in,
                 dn_conv_w=m_dn_conv_w, dn_a_log=m_dn_a_log, dn_dt_bias=m_dn_dt_bias, dn_norm_g=m_dn_norm_g,
                 pool_w=m_pool_w, pool_scale=m_pool_scale, sc_conv_w=m_sc_conv_w, w_br_a=m_w_br_a, w_br_b=m_w_br_b,
                 w_br_c=m_w_br_c, w_o=m_w_o, w_gu=m_w_gu, w_down=m_w_down, final_norm_g=m_final_norm_g)
    mom_v = dict(c_ctx=v_c_ctx, w_ada=v_w_ada, b_ada=v_b_ada, norm1_g=v_norm1_g, norm2_g=v_norm2_g, w_in=v_w_in,
                 dn_conv_w=v_dn_conv_w, dn_a_log=v_dn_a_log, dn_dt_bias=v_dn_dt_bias, dn_norm_g=v_dn_norm_g,
                 pool_w=v_pool_w, pool_scale=v_pool_scale, sc_conv_w=v_sc_conv_w, w_br_a=v_w_br_a, w_br_b=v_w_br_b,
                 w_br_c=v_w_br_c, w_o=v_w_o, w_gu=v_w_gu, w_down=v_w_down, final_norm_g=v_final_norm_g)
    my = _dev_index(_me())

    conv_pack, conv_offs = _pack([loc[k] for k in CONV], F32, 8)
    w_in_all, conv_all = _allgather([w_in.astype(BF16), conv_pack])
    full = dict({k: loc[k] for k in REPL}, w_in=w_in_all)
    for k, blocks in zip(CONV, _unpack(conv_all, conv_offs, [loc[k].shape for k in CONV], (N_DEV,))):
        full[k] = _join(blocks, 2)

    loss8, grad_x, g, recv_l1 = _device_step(x[0], c, ctx[0], loss_target[0], full, TOKEN_TILE,
                                             comm=_Comm({k: loc[k] for k in LATE}, w_ada, b_ada))

    tail = _presplit([g[k][0] for k in LAYERED])
    got = _run_rider(_pair_exchange(tail), "pair_exchange")
    sums = [_pair_sum(a, b) for a, b in zip(tail, got)]
    recv_tail = _chip_exchange_done(sums, _run_rider(_chip_exchange(sums), "chip_exchange"))

    small_names = REPL + CONV
    summed = [k for k in small_names if k != "b_ada"]
    small_pack, small_offs = _pack([g[k] for k in summed] + [loss8[0:1, 0:1]], F32, 8)
    small_sum = _sum_small(_broadcast_small(small_pack))
    sums = _unpack(small_sum, small_offs, [g[k].shape for k in summed] + [(1, 1)])
    grads = dict(zip(summed, sums[:-1]), b_ada=g["b_ada"], w_ada=g["w_ada"])
    loss = sums[-1][0, 0]
    for k in CONV:
        w = loc[k].shape[2]
        grads[k] = lax.dynamic_slice_in_dim(grads[k], my * w, w, axis=2)

    delta, new_m, new_v = {}, {}, {}
    for i, k in enumerate(LAYERED):
        grads[k], delta[k], new_m[k], new_v[k] = _sum_adam([recv_tail[i], recv_l1[i]], loc[k], mom_m[k], mom_v[k])
    delta["w_ada"], new_m["w_ada"], new_v["w_ada"] = _adam_big(w_ada, g["w_ada"], m_w_ada, v_w_ada)
    packs = [_pack([src[k] for k in small_names], F32, 8)[0] for src in (loc, grads, mom_m, mom_v)]
    _, offs = _pack([loc[k] for k in small_names], F32, 8)
    shapes = [loc[k].shape for k in small_names]
    for dst, packed in zip((delta, new_m, new_v), _adam_small(*packs)):
        dst.update(zip(small_names, _unpack(packed, offs, shapes)))

    return (loss, grad_x[None], *[grads[k] for k in WEIGHTS], *[delta[k] for k in WEIGHTS],
            *[new_m[k] for k in WEIGHTS], *[new_v[k] for k in WEIGHTS])
```

```python
import functools

import numpy as np
import jax
import jax.numpy as jnp
from jax import lax
from jax.experimental import pallas as pl
from jax.experimental.pallas import tpu as pltpu

F32 = jnp.float32
BF16 = jnp.bfloat16
HI = lax.Precision.HIGHEST

D = 1024
NL = 2
NH = 4
DH = 128
DN = NH * DH
CH = 64
GW = 64
PW = 256
DFF = 2816
EPS = 1e-6
N_DEV = 8
N_CHIP = 4
MESH_ID = pl.DeviceIdType.MESH
HBM_SPEC = pl.BlockSpec(memory_space=pltpu.HBM)
LANES = 128
BF16_ROWS = 16
VMEM_MB = 56

ADAM_LR, ADAM_B1, ADAM_B2, ADAM_EPS, ADAM_WD, ADAM_STEP = 0.001, 0.9, 0.999, 1e-08, 0.01, 10

IN_BOUNDS = (0, 1536, 2048, 2064, 2320, 2576, 2832, 3088, 6160)
IN_WIDTHS = (1536, 512, 128, 256, 256, 256, 256, 3072)
POOL_WIN = ((1, 0), (2, 1), (4, 3), (8, 7))

NN = ((1,), (0,))
NT = ((1,), (1,))
TN = ((0,), (0,))


def _dot(a, b, dims, hi=False):
    if hi:
        prec = lax.Precision.HIGH if hi == "x3" else HI
        return lax.dot_general(a, b, (dims, ((), ())), precision=prec, preferred_element_type=F32)
    return lax.dot_general(a.astype(BF16), b.astype(BF16), (dims, ((), ())), preferred_element_type=F32)


def _S(shape, dtype=F32):
    return jax.ShapeDtypeStruct(tuple(shape), dtype)


def _full(shape):
    nd = len(shape)
    return pl.BlockSpec(tuple(shape), lambda *_: (0,) * nd)


def _rows(tt, w):
    return pl.BlockSpec((tt, w), lambda i: (i, 0))


class _Rider:
    def __init__(self, ins, out_shapes, sems, copies):
        self.ins, self.out_shapes, self.sems, self.copies = list(ins), list(out_shapes), list(sems), copies

    def start(self, ins, outs, sems):
        remote, local = self.copies(ins, outs, sems)
        for cp in local + remote:
            cp.start()

    def wait(self, ins, outs, sems):
        remote, local = self.copies(ins, outs, sems)
        for cp in remote:
            cp.wait_recv()
        for cp in remote:
            cp.wait_send()
        for cp in local:
            cp.wait()


def _call(body, name, grid, in_specs, out_specs, out_shape, scratch=(), rider=None):
    params = pltpu.CompilerParams(dimension_semantics=("arbitrary",) * len(grid), vmem_limit_bytes=VMEM_MB << 20)
    if rider is None:
        return pl.pallas_call(body, name=name, grid=grid, in_specs=in_specs, out_specs=out_specs, out_shape=out_shape,
                              scratch_shapes=list(scratch), compiler_params=params)
    single = not isinstance(out_shape, (list, tuple))
    out_specs, out_shape = ([out_specs], [out_shape]) if single else (list(out_specs), list(out_shape))
    n_in, n_out, n_scr = len(in_specs), len(out_shape), len(scratch)
    r_in, r_out = len(rider.ins), len(rider.out_shapes)

    def hosted(*refs):
        ins, refs = refs[:n_in + r_in], refs[n_in + r_in:]
        outs, scr = refs[:n_out + r_out], refs[n_out + r_out:]
        riding = (ins[n_in:], outs[n_out:], scr[n_scr:])

        @pl.when(pl.program_id(0) == 0)
        def _():
            rider.start(*riding)

        body(*ins[:n_in], *outs[:n_out], *scr[:n_scr])

        @pl.when(pl.program_id(0) == grid[0] - 1)
        def _():
            rider.wait(*riding)

    call = pl.pallas_call(
        hosted, name=name, grid=grid, in_specs=list(in_specs) + [HBM_SPEC] * r_in,
        out_specs=out_specs + [HBM_SPEC] * r_out, out_shape=out_shape + rider.out_shapes,
        scratch_shapes=list(scratch) + rider.sems, compiler_params=params)

    def run(*args):
        res = call(*args, *rider.ins)
        own = res[:n_out]
        return (own[0] if single else own), list(res[n_out:])

    return run


def _iota(shape, axis):
    return lax.broadcasted_iota(jnp.int32, shape, axis)


def _colsum(a):
    return jnp.sum(a, axis=0, keepdims=True)


def _silu(x):
    return x * jax.nn.sigmoid(x)


def _modulate(x, g, sh, sc):
    xn = x * lax.rsqrt(jnp.mean(x * x, axis=-1, keepdims=True) + EPS)
    return (xn * g) * (1.0 + sc) + sh


def _stream_rows(mv_ref, i, tt, tc, k):
    isc = (i * tt + _iota((tt, 1), 0)) < tc
    return isc, jnp.where(isc, mv_ref[k:k + 1, :], mv_ref[3 + k:4 + k, :])


def _acc_stream(ref, k, isc, val):
    ref[k:k + 1, :] += _colsum(jnp.where(isc, val, 0.0))
    ref[3 + k:4 + k, :] += _colsum(jnp.where(isc, 0.0, val))


CC_ROWS = 16
CTX_ROW = 8


def _mod_cols(n):
    return 1536 if n % 1536 == 0 else n


def _mod_fwd(cc, w_ada, b_ada3):
    n = w_ada.shape[2]
    ct = _mod_cols(n)

    def body(cc_ref, w_ref, b_ref, o_ref):
        o_ref[0] = _dot(_silu(cc_ref[...]), w_ref[0], NN) + b_ref[0]

    return _call(
        body, "mod_fwd", (NL, n // ct),
        [pl.BlockSpec((CC_ROWS, D), lambda l, j: (0, 0)), pl.BlockSpec((1, D, ct), lambda l, j: (l, 0, j)),
         pl.BlockSpec((1, 1, ct), lambda l, j: (l, 0, j))],
        pl.BlockSpec((1, CC_ROWS, ct), lambda l, j: (l, 0, j)), _S((NL, CC_ROWS, n)))(cc, w_ada, b_ada3)


def _mod_bwd(cc, w_ada, dmods):
    n = w_ada.shape[2]
    ct = _mod_cols(n)

    def body(cc_ref, w_ref, dm_ref, dw_ref, dcc_ref):
        first = (pl.program_id(0) == 0) & (pl.program_id(1) == 0)
        cc_ = cc_ref[...]
        sg = jax.nn.sigmoid(cc_)
        dm = dm_ref[0]
        dw_ref[0] = _dot(cc_ * sg, dm, TN)

        @pl.when(first)
        def _():
            dcc_ref[...] = jnp.zeros_like(dcc_ref)

        dcc_ref[...] += _dot(dm, w_ref[0], NT) * (sg * (1.0 + cc_ * (1.0 - sg)))

    return _call(
        body, "mod_bwd", (NL, n // ct),
        [pl.BlockSpec((CC_ROWS, D), lambda l, j: (0, 0)), pl.BlockSpec((1, D, ct), lambda l, j: (l, 0, j)),
         pl.BlockSpec((1, CC_ROWS, ct), lambda l, j: (l, 0, j))],
        [pl.BlockSpec((1, D, ct), lambda l, j: (l, 0, j)), pl.BlockSpec((CC_ROWS, D), lambda l, j: (0, 0))],
        [_S((NL, D, n)), _S((CC_ROWS, D))])(cc, w_ada, dmods)


def _cols(n, tt):
    return pl.BlockSpec((n, tt), lambda i: (0, i))


def _inproj_fwd(X, mv, g, ws, tc, tt):
    T = X.shape[0]
    nw = len(ws)

    def body(x_ref, mv_ref, g_ref, *refs):
        w_refs, ht_ref, p_refs = refs[:nw], refs[nw], refs[nw + 1:]
        i = pl.program_id(0)
        _, sh = _stream_rows(mv_ref, i, tt, tc, 0)
        _, sc = _stream_rows(mv_ref, i, tt, tc, 1)
        h = _modulate(x_ref[...], g_ref[...], sh, sc)
        ht_ref[...] = h.T.astype(BF16)
        hb = h.astype(BF16)
        for w_ref, p_ref in zip(w_refs, p_refs):
            p_ref[...] = jnp.dot(hb, w_ref[...], preferred_element_type=F32)

    return _call(
        body, "inproj_fwd", (T // tt,),
        [_rows(tt, D), _full((8, D)), _full((1, D))] + [_full(w.shape) for w in ws],
        [_cols(D, tt)] + [_rows(tt, w.shape[1]) for w in ws],
        [_S((D, T), BF16)] + [_S((T, w.shape[1])) for w in ws])(X, mv, g, *ws)


def _inproj_bwd(X, mv, g, ws, dps, dp_w, dres, tc, tt):
    T = X.shape[0]
    nw, nd = len(ws), len(dps)

    def body(x_ref, mv_ref, g_ref, dres_ref, *refs):
        w_refs, dp_refs = refs[:nw], refs[nw:nw + nd]
        dx_ref, dg_ref, dm_ref = refs[nw + nd:]
        i = pl.program_id(0)
        isc, sh = _stream_rows(mv_ref, i, tt, tc, 0)
        _, sc = _stream_rows(mv_ref, i, tt, tc, 1)
        dh = None
        for dp_ref, k in zip(dp_refs, dp_w):
            t = _dot(dp_ref[...], w_refs[k][...], NT)
            dh = t if dh is None else dh + t
        _, vjp = jax.vjp(_modulate, x_ref[...], g_ref[...], sh, sc)
        dx, dg, dsh, dsc = vjp(dh)
        dx_ref[...] = dres_ref[...] + dx

        @pl.when(i == 0)
        def _():
            dg_ref[...] = jnp.zeros_like(dg_ref)
            dm_ref[...] = jnp.zeros_like(dm_ref)

        dg_ref[...] += dg
        _acc_stream(dm_ref, 0, isc, dsh)
        _acc_stream(dm_ref, 1, isc, dsc)

    return _call(
        body, "inproj_bwd", (T // tt,),
        [_rows(tt, D), _full((8, D)), _full((1, D)), _rows(tt, D)] + [_full(w.shape) for w in ws]
        + [_rows(tt, dp.shape[1]) for dp in dps],
        [_rows(tt, D), _full((1, D)), _full((8, D))],
        [_S((T, D)), _S((1, D)), _S((8, D))])(X, mv, g, dres, *ws, *dps)


def _dw(At, B, tt):
    K, T = At.shape
    N = B.shape[1]
    tt = 3 * tt if T % (3 * tt) == 0 else tt
    tn = next(t for t in (1024, 512, 256, LANES) if N % t == 0)

    def body(a_ref, b_ref, o_ref):
        @pl.when(pl.program_id(1) == 0)
        def _():
            o_ref[...] = jnp.zeros_like(o_ref)

        o_ref[...] += _dot(a_ref[...], b_ref[...], NN)

    return _call(
        body, "dw", (N // tn, T // tt),
        [pl.BlockSpec((K, tt), lambda j, i: (0, i)), pl.BlockSpec((tt, tn), lambda j, i: (i, j))],
        pl.BlockSpec((K, tn), lambda j, i: (0, j)), _S((K, N)))(At, B)


def _halo_specs(T, tt, cw, col):
    r8, nb8 = tt // 8, T // 8
    return [pl.BlockSpec((tt, cw), lambda j, i: (i, col(j))),
            pl.BlockSpec((8, cw), lambda j, i: (jnp.maximum(i * r8 - 1, 0), col(j))),
            pl.BlockSpec((8, cw), lambda j, i: (jnp.minimum((i + 1) * r8, nb8 - 1), col(j)))]


def _shifts(a, prev8, next8, i, tt, tc, T):
    r = _iota((tt, 1), 0)
    t = i * tt + r
    dn = jnp.where(r == 0, prev8[7:8, :], pltpu.roll(a, 1, 0))
    dn = jnp.where((t == 0) | (t == tc), 0.0, dn)
    up = jnp.where(r == tt - 1, next8[0:1, :], pltpu.roll(a, tt - 1, 0))
    up = jnp.where((t == T - 1) | (t == tc - 1), 0.0, up)
    return dn, up


def _dn_post(y, part):
    a = _silu(y)
    nrm = lax.rsqrt(jnp.sum(a * a, axis=-1, keepdims=True) + EPS)
    f = jnp.where(part == 0, nrm * (DH ** -0.5), jnp.where(part == 1, nrm, 1.0))
    return a * f


def _conv3(w_ref, dn, mid, up):
    return w_ref[0:1, :] * dn + w_ref[1:2, :] * mid + w_ref[2:3, :] * up


def _dnprep_fwd(pq, cw, tc, tt):
    T = pq.shape[0]

    def body(p_ref, pp_ref, pn_ref, w_ref, a_ref):
        part, i = pl.program_id(0), pl.program_id(1)
        p = p_ref[...]
        dn, up = _shifts(p, pp_ref[...], pn_ref[...], i, tt, tc, T)
        y = _conv3(w_ref, dn, p, up)
        for h in range(NH):
            a_ref[:, _hs(h)] = _dn_post(y[:, _hs(h)], part)

    return _call(
        body, "dnprep_fwd", (3, T // tt),
        _halo_specs(T, tt, DN, lambda j: j) + [pl.BlockSpec((3, DN), lambda j, i: (0, j))],
        pl.BlockSpec((tt, DN), lambda j, i: (i, j)), _S((T, 3 * DN)))(pq, pq, pq, cw)


def _dnprep_bwd_act(pq, cw, da_f, da_b, tc, tt):
    T = pq.shape[0]

    def body(p_ref, pp_ref, pn_ref, w_ref, df_ref, db_ref, dy_ref):
        part, i = pl.program_id(0), pl.program_id(1)
        p = p_ref[...]
        dn, up = _shifts(p, pp_ref[...], pn_ref[...], i, tt, tc, T)
        y = _conv3(w_ref, dn, p, up)
        for h in range(NH):
            _, vjp = jax.vjp(lambda yh: _dn_post(yh, part), y[:, _hs(h)])
            dy_ref[:, _hs(h)] = vjp(df_ref[:, _hs(h)] + db_ref[:, _hs(h)])[0]

    blk = pl.BlockSpec((tt, DN), lambda j, i: (i, j))
    return _call(
        body, "dnprep_bwd_act", (3, T // tt),
        _halo_specs(T, tt, DN, lambda j: j) + [pl.BlockSpec((3, DN), lambda j, i: (0, j)), blk, blk],
        blk, _S((T, 3 * DN)))(pq, pq, pq, cw, da_f, da_b)


def _conv_bwd(dy, p, cw, tc, tt):
    T, W = p.shape
    cb = DN

    def body(dy_ref, dyp_ref, dyn_ref, p_ref, pp_ref, pn_ref, w_ref, dp_ref, dw_ref):
        i = pl.program_id(1)
        dy, p_ = dy_ref[...], p_ref[...]
        ddn, dup = _shifts(dy, dyp_ref[...], dyn_ref[...], i, tt, tc, T)
        dp_ref[...] = _conv3(w_ref, dup, dy, ddn)
        pdn, pup = _shifts(p_, pp_ref[...], pn_ref[...], i, tt, tc, T)

        @pl.when(i == 0)
        def _():
            dw_ref[...] = jnp.zeros_like(dw_ref)

        dw_ref[0:1, :] += _colsum(dy * pdn)
        dw_ref[1:2, :] += _colsum(dy * p_)
        dw_ref[2:3, :] += _colsum(dy * pup)

    wspec = pl.BlockSpec((3, cb), lambda j, i: (0, j))
    return _call(
        body, "conv_bwd", (W // cb, T // tt),
        _halo_specs(T, tt, cb, lambda j: j) * 2 + [wspec],
        [pl.BlockSpec((tt, cb), lambda j, i: (i, j)), wspec], [_S((T, W)), _S((3, W))])(dy, dy, dy, p, p, p, cw)


def _sc_fwd(sx, sb, sc_, cw, tc, tt):
    T = sx.shape[0]

    def body(x_ref, xp_ref, xn_ref, c_ref, cp_ref, cn_ref, b_ref, w_ref, y_ref):
        i = pl.program_id(1)
        u = c_ref[...] * x_ref[...]
        dn, up = _shifts(u, cp_ref[...] * xp_ref[...], cn_ref[...] * xn_ref[...], i, tt, tc, T)
        y_ref[...] = b_ref[...] * _conv3(w_ref, dn, u, up)

    blk = pl.BlockSpec((tt, LANES), lambda j, i: (i, j))
    return _call(
        body, "sc_fwd", (PW // LANES, T // tt),
        _halo_specs(T, tt, LANES, lambda j: j) * 2 + [blk, pl.BlockSpec((3, LANES), lambda j, i: (0, j))],
        blk, _S((T, PW)))(sx, sx, sx, sc_, sc_, sc_, sb, cw)


def _sc_bwd(sx, sb, sc_, cw, dy, tc, tt):
    T = sx.shape[0]

    def body(x_ref, xp_ref, xn_ref, c_ref, cp_ref, cn_ref, b_ref, bp_ref, bn_ref, dy_ref, dyp_ref, dyn_ref, w_ref,
             dx_ref, db_ref, dc_ref, dw_ref):
        i = pl.program_id(1)
        x, c, dy_ = x_ref[...], c_ref[...], dy_ref[...]
        u = c * x
        udn, uup = _shifts(u, cp_ref[...] * xp_ref[...], cn_ref[...] * xn_ref[...], i, tt, tc, T)
        db_ref[...] = dy_ * _conv3(w_ref, udn, u, uup)
        e = dy_ * b_ref[...]
        edn, eup = _shifts(e, dyp_ref[...] * bp_ref[...], dyn_ref[...] * bn_ref[...], i, tt, tc, T)
        du = _conv3(w_ref, eup, e, edn)
        dx_ref[...] = du * c
        dc_ref[...] = du * x

        @pl.when(i == 0)
        def _():
            dw_ref[...] = jnp.zeros_like(dw_ref)

        dw_ref[0:1, :] += _colsum(e * udn)
        dw_ref[1:2, :] += _colsum(e * u)
        dw_ref[2:3, :] += _colsum(e * uup)

    blk = pl.BlockSpec((tt, LANES), lambda j, i: (i, j))
    wspec = pl.BlockSpec((3, LANES), lambda j, i: (0, j))
    return _call(
        body, "sc_bwd", (PW // LANES, T // tt),
        _halo_specs(T, tt, LANES, lambda j: j) * 4 + [wspec],
        [blk, blk, blk, wspec], [_S((T, PW))] * 3 + [_S((3, PW))])(
            sx, sx, sx, sc_, sc_, sc_, sb, sb, sb, dy, dy, dy, cw)


def _group_select(vals):
    g = _iota((1, PW), 1) // (PW // len(POOL_WIN))
    return jnp.where(g == 0, vals[0], jnp.where(g == 1, vals[1], jnp.where(g == 2, vals[2], vals[3])))


def _nested_box(get, mirror):
    acc, outs, pl_, ph_ = get(0), [], 0, 0
    for lo, hi in POOL_WIN:
        if mirror:
            lo, hi = hi, lo
        for k in range(pl_ + 1, lo + 1):
            acc = acc + get(-k)
        for k in range(ph_ + 1, hi + 1):
            acc = acc + get(k)
        pl_, ph_ = lo, hi
        outs.append(acc)
    return _group_select(outs)


def _box_tokens(a, n, mirror):
    idx = _iota((n, 1), 0)

    def get(k):
        if k == 0:
            return a
        return jnp.where((idx + k >= 0) & (idx + k < n), pltpu.roll(a, (-k) % n, 0), 0.0)

    return _nested_box(get, mirror)


def _inv_count(pos, n):
    return _group_select([1.0 / (jnp.minimum(pos + hi, n - 1) - jnp.maximum(pos - lo, 0) + 1).astype(F32)
                          for lo, hi in POOL_WIN])


def _pool_rows(ref, r, R, tc, mirror):
    def get(k):
        rr = r + k
        rc = jnp.clip(rr, 0, R - 1)
        v = ref[pl.ds(pl.multiple_of(tc + rc * GW, GW), GW), :]
        if mirror:
            v = v * _inv_count(jnp.full((1, PW), rc, jnp.int32), R)
        return jnp.where((rr >= 0) & (rr < R), v, 0.0)

    return _nested_box(get, mirror)


def _pool_fwd(u, pwbd, ps, tc):
    T = u.shape[0]
    R = (T - tc) // GW

    def body(u_ref, pw_ref, ps_ref, y_ref):
        pw, scale = pw_ref[...], ps_ref[...]
        uc = u_ref[0:tc, :]
        mc = _box_tokens(uc, tc, False) * _inv_count(_iota((tc, 1), 0), tc)
        y_ref[0:tc, :] = _dot(mc - uc, pw, NN) * scale
        inv_c = _inv_count(_iota((GW, 1), 0), GW)

        def row(r, carry):
            rs = _pool_rows(u_ref, r, R, tc, False) * _inv_count(jnp.full((1, PW), r, jnp.int32), R)
            m = _box_tokens(rs, GW, False) * inv_c
            sl = pl.ds(pl.multiple_of(tc + r * GW, GW), GW)
            y_ref[sl, :] = _dot(m - u_ref[sl, :], pw, NN) * scale
            return carry

        lax.fori_loop(0, R, row, 0)

    return pl.pallas_call(
        body, name="pool_fwd", out_shape=_S((T, PW)),
        compiler_params=pltpu.CompilerParams(vmem_limit_bytes=VMEM_MB << 20))(u, pwbd, ps)


def _pool_bwd(u, pwbd, ps, dy, tc):
    T = u.shape[0]
    R = (T - tc) // GW

    def body(u_ref, pw_ref, ps_ref, dy_ref, du_ref, dpw_ref, dps_ref, dd_ref):
        pw, scale = pw_ref[...], ps_ref[...]
        dpw_ref[...] = jnp.zeros_like(dpw_ref)
        dps_ref[...] = jnp.zeros_like(dps_ref)

        def back(d, dy_):
            dz = dy_ * scale
            dpw_ref[...] += _dot(d, dz, TN)
            dps_ref[...] += _colsum(dy_ * _dot(d, pw, NN))
            return _dot(dz, pw, NT)

        uc = u_ref[0:tc, :]
        inv_cc = _inv_count(_iota((tc, 1), 0), tc)
        ddc = back(_box_tokens(uc, tc, False) * inv_cc - uc, dy_ref[0:tc, :])
        du_ref[0:tc, :] = _box_tokens(ddc * inv_cc, tc, True) - ddc
        inv_c = _inv_count(_iota((GW, 1), 0), GW)

        def row1(r, carry):
            rs = _pool_rows(u_ref, r, R, tc, False) * _inv_count(jnp.full((1, PW), r, jnp.int32), R)
            m = _box_tokens(rs, GW, False) * inv_c
            sl = pl.ds(pl.multiple_of(tc + r * GW, GW), GW)
            dd_ref[sl, :] = back(m - u_ref[sl, :], dy_ref[sl, :])
            return carry

        lax.fori_loop(0, R, row1, 0)

        def row2(r, carry):
            t1 = _pool_rows(dd_ref, r, R, tc, True)
            sl = pl.ds(pl.multiple_of(tc + r * GW, GW), GW)
            du_ref[sl, :] = _box_tokens(t1 * inv_c, GW, True) - dd_ref[sl, :]
            return carry

        lax.fori_loop(0, R, row2, 0)

    return pl.pallas_call(
        body, name="pool_bwd", out_shape=[_S((T, PW)), _S((PW, PW)), _S((1, PW))],
        scratch_shapes=[pltpu.VMEM((T, PW), F32)],
        compiler_params=pltpu.CompilerParams(vmem_limit_bytes=VMEM_MB << 20))(u, pwbd, ps, dy)


def _scan_consts():
    i = np.arange(CH)
    lower = (i[:, None] >= i[None, :]).astype(np.float32)
    return jnp.asarray(np.stack([lower, lower.T])), jnp.asarray(np.stack([lower.T, lower]))


def _gates(pab, al, dtb, csum):
    sp_in = pab + dtb
    sp = jnp.maximum(sp_in, 0.0) + jnp.log(1.0 + jnp.exp(-jnp.abs(sp_in)))
    nexp = -jnp.exp(al)
    gm = nexp * sp
    return gm, jax.nn.sigmoid(pab), _dot(csum, gm, NN, hi=True), sp_in, nexp


def _lane_col(m, j):
    return jnp.sum(jnp.where(_iota(m.shape, 1) == j, m, 0.0), axis=1, keepdims=True)


def _hs(h):
    return slice(h * DH, (h + 1) * DH)


HS = NH * CH
X3 = "x3"


def _stack(x, base=0):
    return jnp.concatenate([x[:, base + h * DH:base + (h + 1) * DH] for h in range(NH)], axis=0)


def _heads(st):
    return [st[h * CH:(h + 1) * CH] for h in range(NH)]


def _rowsum(a):
    return jnp.sum(a, axis=1, keepdims=True)


def _row_of(col):
    e0 = (_iota((8, LANES), 1) == 0).astype(F32)
    return _dot(e0, jnp.broadcast_to(col, (HS, LANES)), NT, hi=True)[0:1, :]


def _inverses(nms):
    eye = (_iota((HS, HS), 0) == _iota((HS, HS), 1)).astype(F32)
    x0s, mps = [eye + nm for nm in nms], list(nms)
    for _ in range(5):
        mps = [_dot(mp, mp, NN) for mp in mps]
        x0s = [x0 + _dot(x0, mp, NN) for x0, mp in zip(x0s, mps)]
    rs = [eye - _dot(eye - nm, x0, NN, hi=X3) for nm, x0 in zip(nms, x0s)]
    return [x0 + _dot(x0, r, NN) for x0, r in zip(x0s, rs)]


def _dn_chunk_pre(qkv, pab, al, dtb, csum_d, d):
    gm, bm, gcm, sp_in, nexp = _gates(pab, al, dtb, csum_d)
    gc = jnp.concatenate([_lane_col(gcm, d * NH + h) for h in range(NH)], axis=0)
    beta = jnp.concatenate([_lane_col(bm, 8 + d * NH + h) for h in range(NH)], axis=0)
    q, k, v = _stack(qkv, 0), _stack(qkv, DN), _stack(qkv, 2 * DN)
    ii, jj = _iota((HS, HS), 0), _iota((HS, HS), 1)
    sh = CH.bit_length() - 1
    same = (ii >> sh) == (jj >> sh)
    incl = same & ((ii >= jj) if d == 0 else (ii <= jj))
    strict = same & ((ii > jj) if d == 0 else (ii < jj))
    Di = jnp.where(incl, jnp.exp(jnp.where(incl, gc - _row_of(gc), 0.0)), 0.0)
    Ds = jnp.where(strict, Di, 0.0)
    kb = k * beta
    kk = _dot(kb, k, NT)
    return dict(q=q, k=k, v=v, beta=beta, gc=gc, gm=gm, bm=bm, sp_in=sp_in, nexp=nexp, Di=Di, Ds=Ds, strict=strict,
                last=CH - 1 if d == 0 else 0, kb=kb, kk=kk)


def _dn_chunk_post(c, tm, uw=None):
    q, k, v, beta, gc, kb, last = (c[n] for n in ("q", "k", "v", "beta", "gc", "kb", "last"))
    E = jnp.exp(gc)
    gls = [gc[h * CH + last:h * CH + last + 1, :] for h in range(NH)]
    xs = jnp.exp(jnp.concatenate([jnp.broadcast_to(g, (CH, 1)) for g in gls], axis=0) - gc)
    qk = _dot(q, k, NT)
    u, w = uw if uw is not None else (_dot(tm, v * beta, NN, hi=X3), _dot(tm, kb * E, NN, hi=X3))
    return dict(c, tm=tm, E=E, gls=gls, xs=xs, qk=qk, u=u, w=w, ks=k * xs, qd=q * E, aqk=qk * c["Di"])


def _dn_chunks_bwd_math(cs, Ss, dS2s, dos, vns, dvns):
    I = range(len(cs))
    q, k, v, beta, tm, E, xs, kb, u, w = ([c[n] for c in cs] for n in ("q", "k", "v", "beta", "tm", "E", "xs", "kb", "u", "w"))
    doh, vnh, dvnh = ([_heads(a) for a in l] for l in (dos, vns, dvns))
    cat = lambda parts: jnp.concatenate(parts, axis=0)
    dqd = [cat([_dot(doh[i][h], Ss[i][h], NT) for h in range(NH)]) for i in I]
    dks = [cat([_dot(vnh[i][h], dS2s[i][h], NT) for h in range(NH)]) for i in I]
    dw = [-cat([_dot(dvnh[i][h], Ss[i][h], NT) for h in range(NH)]) for i in I]
    daqk = [_dot(dos[i], vns[i], NT) for i in I]
    drb = [_dot(tm[i], dvns[i], TN, hi=X3) for i in I]
    drw = [_dot(tm[i], dw[i], TN, hi=X3) for i in I]
    dA = [jnp.where(cs[i]["strict"], -(_dot(drb[i], u[i], NT) + _dot(drw[i], w[i], NT)), 0.0) for i in I]
    dM1 = [dA[i] * cs[i]["Ds"] for i in I]
    dM2 = [daqk[i] * cs[i]["Di"] for i in I]
    dkb = [_dot(dM1[i], k[i], NN) + drw[i] * E[i] for i in I]
    dk = [_dot(dM1[i], kb[i], TN) + _dot(dM2[i], q[i], TN) + dks[i] * xs[i] for i in I]
    dq = [_dot(dM2[i], k[i], NN) + dqd[i] * E[i] for i in I]
    on_diag = _iota((HS, HS), 0) == _iota((HS, HS), 1)
    out = []
    for i in I:
        G = dM1[i] * cs[i]["kk"] + dM2[i] * cs[i]["qk"]
        col = _rowsum(jnp.where(on_diag, jnp.broadcast_to(_colsum(G), (HS, HS)), 0.0))
        dxx = _rowsum(dks[i] * k[i]) * xs[i]
        dgc = _rowsum(G) - col + (_rowsum(dqd[i] * q[i]) + _rowsum(drw[i] * kb[i])) * E[i] - dxx
        at_last = _iota((CH, 1), 0) == cs[i]["last"]
        ends = []
        for h in range(NH):
            dgl = (_colsum(_rowsum(Ss[i][h] * dS2s[i][h])) * jnp.exp(cs[i]["gls"][h])
                   + _colsum(dxx[h * CH:(h + 1) * CH]))
            ends.append(jnp.where(at_last, dgl, 0.0))
        dbeta = _rowsum(drb[i] * v[i]) + _rowsum(dkb[i] * k[i])
        out.append((dq[i], dk[i] + dkb[i] * beta[i], drb[i] * beta[i], dgc + cat(ends), dbeta))
    return out


def _chunk_group(n, want=2):
    g = want
    while n % g:
        g //= 2
    return g


def _dn_chunks_fwd(qkv, pab, alr, dtr, rider=None):
    T = qkv.shape[0]
    n = T // CH
    G = _chunk_group(n, 4)
    csum, _ = _scan_consts()

    def body(q_ref, p_ref, cs_ref, al_ref, dt_ref, *outs):
        inst = [(g, d) for g in range(G) for d in range(2)]
        pres = [_dn_chunk_pre(q_ref[g * CH:(g + 1) * CH, :], p_ref[g * CH:(g + 1) * CH, :], al_ref[...], dt_ref[...],
                              cs_ref[d], d) for g, d in inst]
        tms = _inverses([-(p["kk"] * p["Ds"]) for p in pres])
        for (g, d), pre, tm in zip(inst, pres, tms):
            rows = slice(g * HS, (g + 1) * HS)
            u_ref, w_ref, ks_ref, qd_ref, aqk_ref, eg_ref, tm_ref = outs[7 * d:7 * d + 7]
            c = _dn_chunk_post(pre, tm)
            tm_ref[rows, :] = tm
            u_ref[rows, :] = c["u"]
            w_ref[rows, :] = c["w"].astype(BF16)
            ks_ref[rows, :] = c["ks"].astype(BF16)
            qd_ref[rows, :] = c["qd"].astype(BF16)
            aqk_ref[rows, :] = c["aqk"].astype(BF16)
            egs = [jnp.broadcast_to(jnp.exp(gl), (1, LANES)) for gl in c["gls"]]
            eg_ref[g * 8:(g + 1) * 8, :] = jnp.concatenate(egs + [jnp.zeros((8 - NH, LANES), F32)], axis=0)

    st = lambda w_: pl.BlockSpec((G * HS, w_), lambda i: (i, 0))
    one = [st(DH)] * 4 + [st(HS), pl.BlockSpec((G * 8, LANES), lambda i: (i, 0)), st(HS)]
    shp = [_S((n * HS, DH)), _S((n * HS, DH), BF16), _S((n * HS, DH), BF16), _S((n * HS, DH), BF16),
           _S((n * HS, HS), BF16), _S((n * 8, LANES)), _S((n * HS, HS))]
    res = _call(
        body, "dn_chunks_fwd", (n // G,),
        [_rows(G * CH, 3 * DN), _rows(G * CH, LANES), _full((2, CH, CH)), _full((1, LANES)), _full((1, LANES))],
        one * 2, shp * 2, rider=rider)(qkv, pab, csum, alr, dtr)
    outs, riding = (res, None) if rider is None else res
    parts = tuple(outs[:7]), tuple(outs[7:])
    return parts if rider is None else (parts, riding)


def _scan_plan(n, ncx):
    sg = 2 if n % 2 == 0 and ncx % 2 == 0 else 1
    ng, ncg = n // sg, ncx // sg
    return sg, ((lambda i: i), (lambda i: jnp.where(i < ncg, ncg - 1 - i, ng - 1 - (i - ncg))))


def _scan_specs(order, sg):
    st = lambda w_: pl.BlockSpec((sg * HS, w_), lambda i: (order(i), 0))
    return dict(st=st(DH), aqk=st(HS), eg=pl.BlockSpec((sg * 8, LANES), lambda i: (order(i), 0)),
                tok=pl.BlockSpec((sg * CH, DN), lambda i: (order(i), 0)),
                state=pl.BlockSpec((sg, DN, DH), lambda i: (order(i), 0, 0)))


def _scan_fwd(parts, T, tc, rider=None):
    n = T // CH
    sg, orders = _scan_plan(n, tc // CH)

    def body(*refs):
        S_f, S_b = refs[-2:]

        @pl.when(pl.program_id(0) == 0)
        def _():
            S_f[...] = jnp.zeros_like(S_f)
            S_b[...] = jnp.zeros_like(S_b)

        for g in range(sg):
            for d, S in enumerate((S_f, S_b)):
                u_ref, w_ref, ks_ref, qd_ref, aqk_ref, eg_ref = refs[6 * d:6 * d + 6]
                o_ref, ss_ref, vn_ref = refs[12 + 3 * d:15 + 3 * d]
                k = g if d == 0 else sg - 1 - g
                rows = slice(k * HS, (k + 1) * HS)
                ss_ref[k] = S[...]
                Sh = [S[_hs(h), :] for h in range(NH)]
                wh, ksh, qdh = _heads(w_ref[rows, :]), _heads(ks_ref[rows, :]), _heads(qd_ref[rows, :])
                vn = u_ref[rows, :] - jnp.concatenate([_dot(wh[h], Sh[h], NN) for h in range(NH)], axis=0)
                vn_ref[rows, :] = vn
                av, vnh = _heads(_dot(aqk_ref[rows, :], vn, NN)), _heads(vn)
                for h in range(NH):
                    o_ref[k * CH:(k + 1) * CH, _hs(h)] = _dot(qdh[h], Sh[h], NN) + av[h]
                    S[_hs(h), :] = Sh[h] * eg_ref[k * 8 + h:k * 8 + h + 1, :] + _dot(ksh[h], vnh[h], TN)

    ins, outs, shp = [], [], []
    for d in range(2):
        sp = _scan_specs(orders[d], sg)
        ins += [sp["st"]] * 4 + [sp["aqk"], sp["eg"]]
        outs += [sp["tok"], sp["state"], sp["st"]]
        shp += [_S((T, DN)), _S((n, DN, DH)), _S((n * HS, DH))]
    res = _call(body, "scan_fwd", (n // sg,), ins, outs, shp,
                scratch=[pltpu.VMEM((DN, DH), F32), pltpu.VMEM((DN, DH), F32)], rider=rider)(*parts[0][:6], *parts[1][:6])
    res, riding = (res, None) if rider is None else res
    out = tuple(res[:3]), tuple(res[3:])
    return out if rider is None else (out, riding)


def _scan_bwd(do, parts, tc):
    T = do.shape[0]
    n = T // CH
    sg, fwd_orders = _scan_plan(n, tc // CH)
    orders = [lambda s, f=f: f(n // sg - 1 - s) for f in fwd_orders]

    def body(*refs):
        dS_f, dS_b = refs[-2:]

        @pl.when(pl.program_id(0) == 0)
        def _():
            dS_f[...] = jnp.zeros_like(dS_f)
            dS_b[...] = jnp.zeros_like(dS_b)

        for g in range(sg):
            for d, dS in enumerate((dS_f, dS_b)):
                do_ref, w_ref, ks_ref, qd_ref, aqk_ref, eg_ref = refs[6 * d:6 * d + 6]
                dvn_ref, dss_ref = refs[12 + 2 * d:14 + 2 * d]
                k = sg - 1 - g if d == 0 else g
                rows = slice(k * HS, (k + 1) * HS)
                dss_ref[k] = dS[...]
                dSh = [dS[_hs(h), :] for h in range(NH)]
                wh, ksh, qdh = _heads(w_ref[rows, :]), _heads(ks_ref[rows, :]), _heads(qd_ref[rows, :])
                do_st = _stack(do_ref[k * CH:(k + 1) * CH, :])
                dvn = (_dot(aqk_ref[rows, :], do_st, TN)
                       + jnp.concatenate([_dot(ksh[h], dSh[h], NN) for h in range(NH)], axis=0))
                dvn_ref[rows, :] = dvn
                doh, dvnh = _heads(do_st), _heads(dvn)
                for h in range(NH):
                    dS[_hs(h), :] = (_dot(qdh[h], doh[h], TN) + dSh[h] * eg_ref[k * 8 + h:k * 8 + h + 1, :]
                                     - _dot(wh[h], dvnh[h], TN))

    ins, outs, shp, args = [], [], [], []
    for d in range(2):
        sp = _scan_specs(orders[d], sg)
        ins += [sp["tok"]] + [sp["st"]] * 3 + [sp["aqk"], sp["eg"]]
        outs += [sp["st"], sp["state"]]
        shp += [_S((n * HS, DH)), _S((n, DN, DH))]
        args += [do, *parts[d][1:6]]
    res = _call(body, "scan_bwd", (n // sg,), ins, outs, shp,
                scratch=[pltpu.VMEM((DN, DH), F32), pltpu.VMEM((DN, DH), F32)])(*args)
    return tuple(res[:2]), tuple(res[2:])


def _dn_chunks_bwd(qkv, pab, alr, dtr, do, fwd, bwd, rider=None):
    T = qkv.shape[0]
    n = T // CH
    G = _chunk_group(n)
    csum, csum_t = _scan_consts()

    def body(q_ref, p_ref, do_ref, cs_ref, cst_ref, al_ref, dt_ref, *refs):
        dq_refs, dp_refs, acc_ref = refs[14:16], refs[16:18], refs[18]

        @pl.when(pl.program_id(0) == 0)
        def _():
            acc_ref[...] = jnp.zeros_like(acc_ref)

        lane = _iota((CH, LANES), 1)
        inst = [(g, d) for g in range(G) for d in range(2)]
        cs, Ss, dS2s, dos, vns, dvns = [], [], [], [], [], []
        for g, d in inst:
            tok, rows = slice(g * CH, (g + 1) * CH), slice(g * HS, (g + 1) * HS)
            vn_ref, dvn_ref, ss_ref, dss_ref, tm_ref, u_ref, w_ref = refs[7 * d:7 * d + 7]
            cs.append(_dn_chunk_post(
                _dn_chunk_pre(q_ref[tok, :], p_ref[tok, :], al_ref[...], dt_ref[...], cs_ref[d], d), tm_ref[rows, :],
                uw=(u_ref[rows, :], w_ref[rows, :])))
            Ss.append([ss_ref[g, _hs(h), :] for h in range(NH)])
            dS2s.append([dss_ref[g, _hs(h), :] for h in range(NH)])
            dos.append(_stack(do_ref[tok, :]))
            vns.append(vn_ref[rows, :])
            dvns.append(dvn_ref[rows, :])
        for (g, d), c, (dq, dk, dv, dgc, dbeta) in zip(inst, cs, _dn_chunks_bwd_math(cs, Ss, dS2s, dos, vns, dvns)):
            tok = slice(g * CH, (g + 1) * CH)
            dgcm = jnp.zeros((CH, LANES), F32)
            dbm = jnp.zeros((CH, LANES), F32)
            for h, (a, b_, c_, e, f) in enumerate(zip(*map(_heads, (dq, dk, dv, dgc, dbeta)))):
                dq_refs[d][tok, _hs(h)] = a
                dq_refs[d][tok, _hs(NH + h)] = b_
                dq_refs[d][tok, _hs(2 * NH + h)] = c_
                dgcm = jnp.where(lane == d * NH + h, e, dgcm)
                dbm = jnp.where(lane == 8 + d * NH + h, f, dbm)
            dgm = _dot(cst_ref[d], dgcm, NN, hi=True)
            dsp = dgm * c["nexp"] * jax.nn.sigmoid(c["sp_in"])
            dp_refs[d][tok, :] = dsp + dbm * c["bm"] * (1.0 - c["bm"])
            acc_ref[0:1, :] += _colsum(dgm * c["gm"])
            acc_ref[1:2, :] += _colsum(dsp)

    st = pl.BlockSpec((G * HS, DH), lambda i: (i, 0))
    state = pl.BlockSpec((G, DN, DH), lambda i: (i, 0, 0))
    return _call(
        body, "dn_chunks_bwd", (n // G,),
        [_rows(G * CH, 3 * DN), _rows(G * CH, LANES), _rows(G * CH, DN), _full((2, CH, CH)), _full((2, CH, CH)),
         _full((1, LANES)), _full((1, LANES))]
        + [st, st, state, state, pl.BlockSpec((G * HS, HS), lambda i: (i, 0)), st, st] * 2,
        [_rows(G * CH, 3 * DN)] * 2 + [_rows(G * CH, LANES)] * 2 + [_full((8, LANES))],
        [_S((T, 3 * DN))] * 2 + [_S((T, LANES))] * 2 + [_S((8, LANES))], rider=rider)(
            qkv, pab, do, csum, csum_t, alr, dtr, *fwd, *bwd)


def _head_out(o, z, g):
    on = o * lax.rsqrt(jnp.mean(o * o, axis=-1, keepdims=True) + EPS) * g
    return on * _silu(z)


def _mix_branches(of_ref, ob_ref, z_ref, yp_ref, ys_ref, pg_ref, gdn_ref, wa_ref, wb_ref, wc_ref):
    ons, ya = [], None
    for h in range(NH):
        on = _head_out(of_ref[:, _hs(h)] + ob_ref[:, _hs(h)], z_ref[:, _hs(h)], gdn_ref[...])
        t = _dot(on, wa_ref[_hs(h), :], NN)
        ya = t if ya is None else ya + t
        ons.append(on)
    ys = [ya, _dot(yp_ref[...], wb_ref[...], NN), _dot(ys_ref[...], wc_ref[...], NN)]
    sg = [jax.nn.sigmoid(pg_ref[:, k * D:(k + 1) * D]) for k in range(3)]
    return ons, ys, sg


def _mix_fwd(X, of, ob, z, yp, ys, pg, mv, gdn, wa, wb, wc, wo, tc, tt):
    T = X.shape[0]

    def body(x_ref, of_ref, ob_ref, z_ref, yp_ref, ys_ref, pg_ref, mv_ref, gdn_ref, wa_ref, wb_ref, wc_ref, wo_ref,
             x1_ref):
        _, yb, sg = _mix_branches(of_ref, ob_ref, z_ref, yp_ref, ys_ref, pg_ref, gdn_ref, wa_ref, wb_ref, wc_ref)
        mix = _dot(sg[0] * yb[0] + sg[1] * yb[1] + sg[2] * yb[2], wo_ref[...], NN)
        _, gate = _stream_rows(mv_ref, pl.program_id(0), tt, tc, 2)
        x1_ref[...] = x_ref[...] + gate * mix

    return _call(
        body, "mix_fwd", (T // tt,),
        [_rows(tt, D), _rows(tt, DN), _rows(tt, DN), _rows(tt, DN), _rows(tt, PW), _rows(tt, PW), _rows(tt, 3 * D),
         _full((8, D)), _full((1, DH)), _full(wa.shape), _full(wb.shape), _full(wc.shape), _full(wo.shape)],
        _rows(tt, D), _S((T, D)))(X, of, ob, z, yp, ys, pg, mv, gdn, wa, wb, wc, wo)


def _mix_bwd(dx1, of, ob, z, yp, ys, pg, mv, gdn, wa, wb, wc, wo, tc, tt):
    T = dx1.shape[0]

    def body(dx_ref, of_ref, ob_ref, z_ref, yp_ref, ys_ref, pg_ref, mv_ref, gdn_ref, wa_ref, wb_ref, wc_ref, wo_ref,
             do_ref, dz_ref, dyp_ref, dys_ref, dpg_ref, dwa_ref, dwb_ref, dwc_ref, dwo_ref, dgdn_ref, dm_ref):
        i = pl.program_id(0)

        @pl.when(i == 0)
        def _():
            for r in (dwa_ref, dwb_ref, dwc_ref, dwo_ref, dgdn_ref, dm_ref):
                r[...] = jnp.zeros_like(r)

        ons, yb, sg = _mix_branches(of_ref, ob_ref, z_ref, yp_ref, ys_ref, pg_ref, gdn_ref, wa_ref, wb_ref, wc_ref)
        ymix = sg[0] * yb[0] + sg[1] * yb[1] + sg[2] * yb[2]
        isc, gate = _stream_rows(mv_ref, i, tt, tc, 2)
        dx = dx_ref[...]
        dmix = dx * gate
        _acc_stream(dm_ref, 2, isc, dx * _dot(ymix, wo_ref[...], NN))
        dwo_ref[...] += _dot(ymix, dmix, TN)
        dymix = _dot(dmix, wo_ref[...], NT)
        dyb = []
        for k in range(3):
            dyb.append(dymix * sg[k])
            dpg_ref[:, k * D:(k + 1) * D] = dymix * yb[k] * sg[k] * (1.0 - sg[k])
        dwb_ref[...] += _dot(yp_ref[...], dyb[1], TN)
        dwc_ref[...] += _dot(ys_ref[...], dyb[2], TN)
        dyp_ref[...] = _dot(dyb[1], wb_ref[...], NT)
        dys_ref[...] = _dot(dyb[2], wc_ref[...], NT)
        dg = jnp.zeros((1, DH), F32)
        for h in range(NH):
            dwa_ref[_hs(h), :] += _dot(ons[h], dyb[0], TN)
            don = _dot(dyb[0], wa_ref[_hs(h), :], NT)
            _, vjp = jax.vjp(_head_out, of_ref[:, _hs(h)] + ob_ref[:, _hs(h)], z_ref[:, _hs(h)], gdn_ref[...])
            do_h, dz_h, dg_h = vjp(don)
            do_ref[:, _hs(h)] = do_h
            dz_ref[:, _hs(h)] = dz_h
            dg = dg + dg_h
        dgdn_ref[...] += dg

    return _call(
        body, "mix_bwd", (T // tt,),
        [_rows(tt, D), _rows(tt, DN), _rows(tt, DN), _rows(tt, DN), _rows(tt, PW), _rows(tt, PW), _rows(tt, 3 * D),
         _full((8, D)), _full((1, DH)), _full(wa.shape), _full(wb.shape), _full(wc.shape), _full(wo.shape)],
        [_rows(tt, DN), _rows(tt, DN), _rows(tt, PW), _rows(tt, PW), _rows(tt, 3 * D),
         _full(wa.shape), _full(wb.shape), _full(wc.shape), _full(wo.shape), _full((1, DH)), _full((8, D))],
        [_S((T, DN)), _S((T, DN)), _S((T, PW)), _S((T, PW)), _S((T, 3 * D)),
         _S(wa.shape), _S(wb.shape), _S(wc.shape), _S(wo.shape), _S((1, DH)), _S((8, D))])(
            dx1, of, ob, z, yp, ys, pg, mv, gdn, wa, wb, wc, wo)


def _ffn_fwd(X1, mv, g, wgu, wd, tc, tt):
    T = X1.shape[0]

    def body(x_ref, mv_ref, g_ref, wgu_ref, wd_ref, x2_ref, ff_ref):
        i = pl.program_id(0)
        _, sh = _stream_rows(mv_ref, i, tt, tc, 0)
        _, sc = _stream_rows(mv_ref, i, tt, tc, 1)
        _, gate = _stream_rows(mv_ref, i, tt, tc, 2)
        x = x_ref[...]
        gu = _dot(_modulate(x, g_ref[...], sh, sc), wgu_ref[...], NN)
        ff = _dot(_silu(gu[:, :DFF]) * gu[:, DFF:], wd_ref[...], NN)
        ff_ref[...] = ff
        x2_ref[...] = x + gate * ff

    return _call(
        body, "ffn_fwd", (T // tt,),
        [_rows(tt, D), _full((8, D)), _full((1, D)), _full(wgu.shape), _full(wd.shape)],
        [_rows(tt, D)] * 2, [_S((T, D))] * 2)(X1, mv, g, wgu, wd)


def _ffn_bwd(X1, ff, dx2, mv, g, wgu, wd, tc, tt, rider=None):
    T = X1.shape[0]

    def body(x_ref, ff_ref, dx2_ref, mv_ref, g_ref, wgu_ref, wd_ref, dx1_ref, ht_ref, dgu_ref, actt_ref, dff_ref, dg_ref,
             dm_ref):
        i = pl.program_id(0)
        isc, sh = _stream_rows(mv_ref, i, tt, tc, 0)
        _, sc = _stream_rows(mv_ref, i, tt, tc, 1)
        _, gate = _stream_rows(mv_ref, i, tt, tc, 2)
        x, dx2_ = x_ref[...], dx2_ref[...]
        h, vjp = jax.vjp(_modulate, x, g_ref[...], sh, sc)
        ht_ref[...] = h.T.astype(BF16)
        gu = jnp.dot(h.astype(BF16), wgu_ref[...], preferred_element_type=F32)
        ga, up = gu[:, :DFF], gu[:, DFF:]
        sg = jax.nn.sigmoid(ga)
        actt_ref[...] = (ga * sg * up).T.astype(BF16)
        dff = dx2_ * gate
        dff_ref[...] = dff.astype(BF16)
        dact = _dot(dff, wd_ref[...], NT)
        dga = (dact * up * (sg * (1.0 + ga * (1.0 - sg)))).astype(BF16)
        dup = (dact * ga * sg).astype(BF16)
        dgu_ref[:, :DFF] = dga
        dgu_ref[:, DFF:] = dup
        dh = _dot(dga, wgu_ref[:, :DFF], NT) + _dot(dup, wgu_ref[:, DFF:], NT)
        dx, dg, dsh, dsc = vjp(dh)
        dx1_ref[...] = dx2_ + dx

        @pl.when(i == 0)
        def _():
            dg_ref[...] = jnp.zeros_like(dg_ref)
            dm_ref[...] = jnp.zeros_like(dm_ref)

        dg_ref[...] += dg
        _acc_stream(dm_ref, 0, isc, dsh)
        _acc_stream(dm_ref, 1, isc, dsc)
        _acc_stream(dm_ref, 2, isc, dx2_ * ff_ref[...])

    return _call(
        body, "ffn_bwd", (T // tt,),
        [_rows(tt, D), _rows(tt, D), _rows(tt, D), _full((8, D)), _full((1, D)), _full(wgu.shape), _full(wd.shape)],
        [_rows(tt, D), _cols(D, tt), _rows(tt, 2 * DFF), _cols(DFF, tt), _rows(tt, D), _full((1, D)), _full((8, D))],
        [_S((T, D)), _S((D, T), BF16), _S((T, 2 * DFF), BF16), _S((DFF, T), BF16), _S((T, D), BF16),
         _S((1, D)), _S((8, D))], rider=rider)(X1, ff, dx2, mv, g, wgu, wd)


def _rms(x, g):
    return x * lax.rsqrt(jnp.mean(x * x, axis=-1, keepdims=True) + EPS) * g


def _loss_head(X2, tgt, gf, tc):
    T = X2.shape[0]

    def body(x_ref, t_ref, g_ref, dx_ref, loss_ref, dg_ref):
        i = pl.program_id(0)

        @pl.when(i == 0)
        def _():
            dx_ref[...] = jnp.zeros_like(dx_ref)
            loss_ref[...] = jnp.zeros_like(loss_ref)
            dg_ref[...] = jnp.zeros_like(dg_ref)

        @pl.when(i > 0)
        def _():
            y, vjp = jax.vjp(_rms, x_ref[...], g_ref[...])
            err = y - t_ref[...]
            dx, dg = vjp(err * (1.0 / D))
            dx_ref[...] = dx
            dg_ref[...] += dg
            loss_ref[...] += (0.5 / D) * jnp.sum(jnp.sum(err * err, axis=1, keepdims=True), axis=0, keepdims=True)

    return _call(
        body, "loss_head", (T // tc,),
        [_rows(tc, D), pl.BlockSpec((tc, D), lambda i: (jnp.maximum(i - 1, 0), 0)), _full((1, D))],
        [_rows(tc, D), _full((8, LANES)), _full((1, D))],
        [_S((T, D)), _S((8, LANES)), _S((1, D))])(X2, tgt, gf)


def _block_diag(pw):
    g, n = pw.shape[0], pw.shape[1]
    out = jnp.zeros((g * n, g * n), pw.dtype)
    for k in range(g):
        out = lax.dynamic_update_slice(out, pw[k], (k * n, k * n))
    return out


IN_TRUE = tuple(IN_BOUNDS[k + 1] - IN_BOUNDS[k] for k in range(8))


def _overlaps(widths, cw):
    starts = np.cumsum([0] + list(widths))
    out = []
    for k in range(N_DEV):
        for i in range(len(widths)):
            a, b = max(k * cw, starts[i]), min((k + 1) * cw, starts[i + 1])
            if a < b:
                out.append((k, i, int(a - k * cw), int(a - starts[i]), int(b - a)))
    return out


def _shards_to_cols(gathered, l, widths, padded):
    nd, _, R, cw = gathered.shape
    tr = _shard_rows(R)

    def body(s_ref, *o_refs):
        for i, o_ref in enumerate(o_refs):
            if padded[i] > widths[i]:
                o_ref[...] = jnp.zeros_like(o_ref)
        for k, i, so, go, n in _overlaps(widths, cw):
            o_refs[i][:, go:go + n] = s_ref[k, 0, :, so:so + n].astype(BF16)

    return _call(
        body, "shards_to_cols", (R // tr,), [pl.BlockSpec((nd, 1, tr, cw), lambda i: (0, l, i, 0))],
        [_rows(tr, p) for p in padded], [_S((R, p), BF16) for p in padded])(gathered)


def _cols_to_shards(groups, widths, cw):
    R = groups[0].shape[0]
    tr = _shard_rows(R)

    def body(*refs):
        o_ref = refs[-1]
        for k, i, so, go, n in _overlaps(widths, cw):
            o_ref[k % 2, k // 2, 0, :, so:so + n] = refs[i][:, go:go + n]

    return _call(
        body, "cols_to_shards", (R // tr,), [_rows(tr, g.shape[1]) for g in groups],
        pl.BlockSpec((2, N_CHIP, 1, tr, cw), lambda i: (0, 0, 0, i, 0)), _S((2, N_CHIP, 1, R, cw)))(*groups)


def _mod_rows(mods_l, k0):
    rows = [mods_l[s, (k0 + k) * D:(k0 + k + 1) * D] for s in (0, 1) for k in range(3)]
    return jnp.stack(rows + [jnp.zeros((D,), F32)] * 2)


def _lane_row(v8):
    return jnp.pad(v8.reshape(1, 8), ((0, 0), (0, LANES - 8)))


LAYERED = ("w_in", "w_br_a", "w_br_b", "w_br_c", "w_o", "w_gu", "w_down")
LATE = ("w_br_a", "w_br_b", "w_br_c", "w_o", "w_gu", "w_down")


def _device_step(x, c, ctx, tgt, wts, tt, comm=None):
    tc = ctx.shape[0]
    X = jnp.concatenate([ctx, x], axis=0)
    if comm is None:
        row = 0
        cc = jnp.concatenate([c, jnp.zeros((CTX_ROW - 1, D), F32), wts["c_ctx"][None, :],
                              jnp.zeros((CC_ROWS - CTX_ROW - 1, D), F32)], axis=0)
        w_ada = wts["w_ada"].astype(BF16)
        mods16 = _mod_fwd(cc, w_ada, wts["b_ada"].reshape(NL, 1, 6 * D))
    else:
        row, mods16 = comm.adaln_fwd(c, wts["c_ctx"])
    mods = jnp.stack([mods16[:, CTX_ROW], lax.dynamic_index_in_dim(mods16, row, 1, keepdims=False)], axis=1)

    saved = []
    for l in range(NL):
        ws = _shards_to_cols(wts["w_in"], l, IN_TRUE, IN_WIDTHS)
        mv1, mv2 = _mod_rows(mods[l], 0), _mod_rows(mods[l], 3)
        g1, g2 = wts["norm1_g"][l][None, :], wts["norm2_g"][l][None, :]
        cw, scw = wts["dn_conv_w"][l], wts["sc_conv_w"][l]
        alr, dtr = _lane_row(wts["dn_a_log"][l]), _lane_row(wts["dn_dt_bias"][l])
        gdn = wts["dn_norm_g"][l][None, :]
        pwbd, ps = _block_diag(wts["pool_w"][l]), wts["pool_scale"][l][None, :]
        hb, pq, pz, pab, pp, sx, sb, sc_, pg = _inproj_fwd(X, mv1, g1, ws, tc, tt)
        qkv = _dnprep_fwd(pq, cw, tc, tt)
        if comm is not None and l == 0:
            parts, riding = _dn_chunks_fwd(qkv, pab, alr, dtr, rider=comm.late_weights_chips())
            ((of, ssf, vnf), (ob, ssb, vnb)), riding = _scan_fwd(parts, X.shape[0], tc,
                                                                 rider=comm.late_weights_pair(riding))
            wts = dict(wts, **comm.late_weights(riding))
        else:
            parts = _dn_chunks_fwd(qkv, pab, alr, dtr)
            (of, ssf, vnf), (ob, ssb, vnb) = _scan_fwd(parts, X.shape[0], tc)
        wbr = [_shards_to_cols(wts[k], l, (2 * DFF,), (2 * DFF,))[0] if k == "w_gu" else wts[k][l].astype(BF16)
               for k in LATE]
        yp = _pool_fwd(pp, pwbd, ps, tc)
        ys = _sc_fwd(sx, sb, sc_, scw, tc, tt)
        X1 = _mix_fwd(X, of, ob, pz, yp, ys, pg, mv1, gdn, *wbr[:4], tc, tt)
        X2, ff = _ffn_fwd(X1, mv2, g2, wbr[4], wbr[5], tc, tt)
        saved.append(dict(X=X, X1=X1, ff=ff, ws=ws, wbr=wbr, mv1=mv1, mv2=mv2, g1=g1, g2=g2, cw=cw, scw=scw, alr=alr, dtr=dtr,
                          gdn=gdn, pwbd=pwbd, ps=ps, hb=hb, pq=pq, pz=pz, pab=pab, pp=pp, sx=sx, sb=sb, sc=sc_, pg=pg,
                          qkv=qkv, of=of, ob=ob, ssf=ssf, ssb=ssb, vnf=vnf, vnb=vnb, parts=parts, yp=yp, ys=ys))
        X = X2

    dX, loss, dgf = _loss_head(X, tgt, wts["final_norm_g"][None, :], tc)

    gl = {k: [None] * NL for k in ("w_in", "norm1_g", "norm2_g", "dn_conv_w", "dn_a_log", "dn_dt_bias", "dn_norm_g",
                                   "pool_w", "pool_scale", "sc_conv_w", "w_br_a", "w_br_b", "w_br_c", "w_o", "w_gu",
                                   "w_down")}
    dmods = [None] * NL
    early = None
    for l in reversed(range(NL)):
        s = saved[l]
        hide = comm is not None and l == 0
        res = _ffn_bwd(s["X1"], s["ff"], dX, s["mv2"], s["g2"], s["wbr"][4], s["wbr"][5], tc, tt,
                       rider=comm.grad_pair_rider([gl[k][1] for k in LAYERED]) if hide else None)
        if hide:
            res, got = res
            chip_rider = comm.grad_chip_rider(got)
        dx1, h2, dgu, act, dff, dg2, dm2 = res
        gl["w_gu"][l] = _cols_to_shards([_dw(h2, dgu, tt)], (2 * DFF,), 2 * DFF // N_DEV)
        gl["w_down"][l] = _dw(act, dff, tt)
        do, dz, dyp, dys, dpg, dwa, dwb, dwc, dwo, dgdn, dmg = _mix_bwd(
            dx1, s["of"], s["ob"], s["pz"], s["yp"], s["ys"], s["pg"], s["mv1"], s["gdn"], *s["wbr"][:4], tc, tt)
        dpp, dpw, dps = _pool_bwd(s["pp"], s["pwbd"], s["ps"], dyp, tc)
        dsx, dsb, dsc, dscw = _sc_bwd(s["sx"], s["sb"], s["sc"], s["scw"], dys, tc, tt)
        (dvnf, dssf), (dvnb, dssb) = _scan_bwd(do, s["parts"], tc)
        res = _dn_chunks_bwd(s["qkv"], s["pab"], s["alr"], s["dtr"], do,
                             (s["vnf"], dvnf, s["ssf"], dssf, s["parts"][0][6], *s["parts"][0][:2]),
                             (s["vnb"], dvnb, s["ssb"], dssb, s["parts"][1][6], *s["parts"][1][:2]),
                             rider=chip_rider if hide else None)
        if hide:
            res, early = res
            early = comm.grad_chip_done(early)
        dqf, dqb, dpf, dpb, gacc = res
        dy = _dnprep_bwd_act(s["pq"], s["cw"], dqf, dqb, tc, tt)
        dpq, dcw = _conv_bwd(dy, s["pq"], s["cw"], tc, tt)
        dps_ = [dpq, dz, dpf, dpb, dpp, dsx, dsb, dsc, dpg]
        dp_w = [0, 1, 2, 2, 3, 4, 5, 6, 7]
        dX, dg1, dm1 = _inproj_bwd(s["X"], s["mv1"], s["g1"], s["ws"], dps_, dp_w, dx1, tc, tt)
        dws = [_dw(s["hb"], dp, tt) for dp in (dpq, dz, dpf + dpb, dpp, dsx, dsb, dsc, dpg)]
        gl["w_in"][l] = _cols_to_shards(dws, IN_TRUE, IN_BOUNDS[-1] // N_DEV)
        gl["norm1_g"][l], gl["norm2_g"][l] = dg1[0], dg2[0]
        gl["dn_conv_w"][l], gl["sc_conv_w"][l] = dcw, dscw
        gl["dn_a_log"][l], gl["dn_dt_bias"][l] = gacc[0, :8].reshape(2, NH), gacc[1, :8].reshape(2, NH)
        gl["dn_norm_g"][l] = dgdn[0]
        gl["pool_w"][l] = jnp.stack([dpw[k * GW:(k + 1) * GW, k * GW:(k + 1) * GW] for k in range(4)])
        gl["pool_scale"][l] = dps[0]
        gl["w_br_a"][l], gl["w_br_b"][l], gl["w_br_c"][l], gl["w_o"][l] = dwa, dwb, dwc, dwo
        dm = dm1 + dmg
        cat = lambda r: jnp.concatenate([dm[r], dm[r + 1], dm[r + 2], dm2[r], dm2[r + 1], dm2[r + 2]])
        dmods[l] = jnp.stack([cat(0), cat(3)])

    dmods = jnp.stack(dmods)
    grads = {k: (v if k in LAYERED else jnp.stack(v)) for k, v in gl.items()}
    if comm is None:
        dm16 = jnp.zeros((NL, CC_ROWS, 6 * D), F32).at[:, CTX_ROW].set(dmods[:, 0]).at[:, row].set(dmods[:, 1])
        dwada, dcc = _mod_bwd(cc, w_ada, dm16)
        grads.update(w_ada=dwada, b_ada=dmods[:, 0] + dmods[:, 1], c_ctx=dcc[CTX_ROW])
    else:
        grads.update(comm.adaln_bwd(dmods))
    grads.update(final_norm_g=dgf[0])
    return loss, dX[tc:], grads, early


def _me():
    return lax.axis_index("x"), lax.axis_index("y"), lax.axis_index("c")


def _dev_index(p):
    return 4 * p[0] + 2 * p[1] + p[2]


def _allgather(parts):
    n = len(parts)

    def body(*refs):
        ins, outs = refs[:n], refs[n:2 * n]
        send_sems, recv_sems = refs[2 * n:]
        x, y, c = _me()
        me, sibling = (x, y, c), (x, y, 1 - c)
        chips = [(1 - x, y), (x, 1 - y), (1 - x, 1 - y)]

        def copy(a, k, block, to, src=None):
            dst = outs[a].at[_dev_index(block)]
            return pltpu.make_async_remote_copy(
                src_ref=dst if src is None else src, dst_ref=dst, send_sem=send_sems.at[a, k], recv_sem=recv_sems.at[a, k],
                device_id=to, device_id_type=MESH_ID)

        first, passed = [], []
        for a in range(n):
            first.append(copy(a, 0, me, sibling, src=ins[a]))
            first += [copy(a, 1 + j, me, (*chip, c), src=ins[a]) for j, chip in enumerate(chips)]
        for cp in first:
            cp.start()
        for a in range(n):
            for j, chip in enumerate(chips):
                copy(a, 1 + j, (*chip, c), me).wait_recv()
                passed.append(copy(a, 4 + j, (*chip, c), sibling))
                passed[-1].start()
        for a in range(n):
            copy(a, 0, sibling, me).wait_recv()
            for j, chip in enumerate(chips):
                copy(a, 4 + j, (*chip, 1 - c), me).wait_recv()
        for cp in first + passed:
            cp.wait_send()

    outs = pl.pallas_call(
        body, name="allgather", in_specs=[HBM_SPEC] * n, out_specs=[HBM_SPEC] * n,
        out_shape=[_S((N_DEV,) + p.shape, p.dtype) for p in parts],
        scratch_shapes=[pltpu.SemaphoreType.DMA((n, 7)), pltpu.SemaphoreType.DMA((n, 7))],
    )(*parts)
    return [_with_own(o, p, _dev_index(_me())) for o, p in zip(outs, parts)]


def _with_own(gathered, own, index):
    return lax.dynamic_update_index_in_dim(gathered, own, index, 0)


def _broadcast_small(small, name="small_exchange"):
    def body(in_ref, out_ref, send_sems, recv_sems, local_sem):
        x, y, c = _me()
        my = _dev_index((x, y, c))
        mine = pltpu.make_async_copy(in_ref, out_ref.at[my], local_sem)
        mine.start()
        remote = []
        for k in range(1, N_DEV):
            cp = pltpu.make_async_remote_copy(
                src_ref=in_ref, dst_ref=out_ref.at[my], send_sem=send_sems.at[k - 1], recv_sem=recv_sems.at[k - 1],
                device_id=(x ^ (k >> 2), y ^ ((k >> 1) & 1), c ^ (k & 1)), device_id_type=MESH_ID)
            cp.start()
            remote.append(cp)
        for cp in remote:
            cp.wait_recv()
        for cp in remote:
            cp.wait_send()
        mine.wait()

    return pl.pallas_call(
        body, name=name, in_specs=[HBM_SPEC], out_specs=HBM_SPEC,
        out_shape=_S((N_DEV,) + small.shape, small.dtype),
        scratch_shapes=[pltpu.SemaphoreType.DMA((7,)), pltpu.SemaphoreType.DMA((7,)), pltpu.SemaphoreType.DMA],
    )(small)


def _run_rider(rider, name):
    ni, no = len(rider.ins), len(rider.out_shapes)

    def body(*refs):
        riding = (refs[:ni], refs[ni:ni + no], refs[ni + no:])
        rider.start(*riding)
        rider.wait(*riding)

    return list(pl.pallas_call(
        body, name=name, in_specs=[HBM_SPEC] * ni, out_specs=[HBM_SPEC] * no, out_shape=rider.out_shapes,
        scratch_shapes=rider.sems)(*rider.ins))


def _chip_peers(x, y):
    return [(k - 1, (x ^ (k >> 1), y ^ (k & 1))) for k in range(1, N_CHIP)]


def _pair_exchange(g2s):
    n = len(g2s)

    def copies(ins, outs, sems):
        x, y, c = _me()
        return [pltpu.make_async_remote_copy(
            src_ref=ins[a].at[1 - c, j], dst_ref=outs[a].at[j], send_sem=sems[0].at[a, j], recv_sem=sems[1].at[a, j],
            device_id=(x, y, 1 - c), device_id_type=MESH_ID) for a in range(n) for j in range(N_CHIP)], []

    return _Rider(g2s, [_S(g.shape[1:], g.dtype) for g in g2s],
                  [pltpu.SemaphoreType.DMA((n, N_CHIP)), pltpu.SemaphoreType.DMA((n, N_CHIP))], copies)


def _my_chip():
    x, y, _ = _me()
    return 2 * x + y


def _chip_exchange(s4s):
    n = len(s4s)

    def copies(ins, outs, sems):
        x, y, c = _me()
        my = 2 * x + y
        return [pltpu.make_async_remote_copy(
            src_ref=ins[a].at[2 * px + py], dst_ref=outs[a].at[my], send_sem=sems[0].at[a, k], recv_sem=sems[1].at[a, k],
            device_id=(px, py, c), device_id_type=MESH_ID) for k, (px, py) in _chip_peers(x, y) for a in range(n)], []

    return _Rider(s4s, [_S(s.shape, s.dtype) for s in s4s],
                  [pltpu.SemaphoreType.DMA((n, N_CHIP - 1)), pltpu.SemaphoreType.DMA((n, N_CHIP - 1))], copies)


def _chip_exchange_done(s4s, recvs):
    my = _my_chip()
    return [_with_own(r, lax.dynamic_index_in_dim(s, my, 0, keepdims=False), my) for s, r in zip(s4s, recvs)]


def _chip_gather(arrs):
    n = len(arrs)

    def copies(ins, outs, sems):
        x, y, c = _me()
        return [pltpu.make_async_remote_copy(
            src_ref=ins[a], dst_ref=outs[a].at[2 * x + y], send_sem=sems[0].at[a, k], recv_sem=sems[1].at[a, k],
            device_id=(px, py, c), device_id_type=MESH_ID) for k, (px, py) in _chip_peers(x, y) for a in range(n)], []

    return _Rider(arrs, [_S((N_CHIP,) + a.shape, a.dtype) for a in arrs],
                  [pltpu.SemaphoreType.DMA((n, N_CHIP - 1)), pltpu.SemaphoreType.DMA((n, N_CHIP - 1))], copies)


def _pair_gather(chips):
    n = len(chips)

    def copies(ins, outs, sems):
        x, y, c = _me()
        return [pltpu.make_async_remote_copy(
            src_ref=ins[a].at[j], dst_ref=outs[a].at[j], send_sem=sems[0].at[a, j], recv_sem=sems[1].at[a, j],
            device_id=(x, y, 1 - c), device_id_type=MESH_ID) for a in range(n) for j in range(N_CHIP)], []

    return _Rider(chips, [_S(a.shape, a.dtype) for a in chips],
                  [pltpu.SemaphoreType.DMA((n, N_CHIP)), pltpu.SemaphoreType.DMA((n, N_CHIP))], copies)


def _shard_rows(r):
    return 256 if r % 256 == 0 else r


def _pair_sum(g2, got):
    _, nc, L, R, C = g2.shape
    tr = _shard_rows(R)

    def body(a_ref, b_ref, o_ref):
        o_ref[...] = (a_ref[0] + b_ref[...]).astype(BF16)

    blk = pl.BlockSpec((1, 1, tr, C), lambda j, l, i: (j, l, i, 0))
    return _call(
        body, "pair_sum", (nc, L, R // tr),
        [pl.BlockSpec((1, 1, 1, tr, C), lambda j, l, i: (lax.axis_index("c"), j, l, i, 0)), blk], blk,
        _S(got.shape, BF16))(g2, got)


def _adam(w, g, m, v):
    m2 = ADAM_B1 * m + (1.0 - ADAM_B1) * g
    v2 = ADAM_B2 * v + (1.0 - ADAM_B2) * (g * g)
    m_hat = m2 / (1.0 - ADAM_B1 ** ADAM_STEP)
    v_hat = v2 / (1.0 - ADAM_B2 ** ADAM_STEP)
    return -ADAM_LR * (m_hat / (jnp.sqrt(v_hat) + ADAM_EPS) + ADAM_WD * w), m2, v2


def _sum_adam(recvs, w, m, v):
    L, R, C = w.shape
    tr = _shard_rows(R)
    nr = len(recvs)

    def body(*refs):
        w_ref, m_ref, v_ref, g_ref, d_ref, m2_ref, v2_ref = refs[nr:]
        g = None
        for li, r_ref in enumerate(refs[:nr]):
            s = r_ref[0, 0].astype(F32)
            for j in range(1, N_CHIP):
                s = s + r_ref[j, 0].astype(F32)
            g = s if g is None else jnp.where(pl.program_id(0) == li, s, g)
        g_ref[0] = g
        d_ref[0], m2_ref[0], v2_ref[0] = _adam(w_ref[0], g, m_ref[0], v_ref[0])

    blk = pl.BlockSpec((1, tr, C), lambda l, i: (l, i, 0))
    rspec = pl.BlockSpec((N_CHIP, 1, tr, C), (lambda l, i: (0, l, i, 0)) if nr == 1 else (lambda l, i: (0, 0, i, 0)))
    return _call(body, "sum_adam", (L, R // tr), [rspec] * nr + [blk, blk, blk], [blk] * 4, [_S(w.shape)] * 4)(
        *recvs, w, m, v)


def _adam_big(w, g, m, v):
    L, R, C = w.shape
    tr = _shard_rows(R)

    def body(w_ref, g_ref, m_ref, v_ref, d_ref, m2_ref, v2_ref):
        d_ref[0], m2_ref[0], v2_ref[0] = _adam(w_ref[0], g_ref[0], m_ref[0], v_ref[0])

    blk = pl.BlockSpec((1, tr, C), lambda l, i: (l, i, 0))
    return _call(body, "adam_big", (L, R // tr), [blk] * 4, [blk] * 3, [_S(w.shape)] * 3)(w, g, m, v)


def _sum_small(recv):
    def body(r_ref, o_ref):
        g = r_ref[0]
        for k in range(1, recv.shape[0]):
            g = g + r_ref[k]
        o_ref[...] = g

    return pl.pallas_call(body, name="sum_small", out_shape=_S(recv.shape[1:]))(recv)


def _adam_small(w, g, m, v):
    def body(w_ref, g_ref, m_ref, v_ref, d_ref, m2_ref, v2_ref):
        d_ref[...], m2_ref[...], v2_ref[...] = _adam(w_ref[...], g_ref[...], m_ref[...], v_ref[...])

    return pl.pallas_call(body, name="adam_small", out_shape=[_S(w.shape)] * 3)(w, g, m, v)


def _pack(arrs, dtype, row_mult):
    parts, offs, r = [], [], 0
    for a in arrs:
        nr = -(-a.size // LANES)
        parts.append(jnp.pad(a.reshape(-1).astype(dtype), (0, nr * LANES - a.size)))
        offs.append(r)
        r += nr
    pad = (-r) % row_mult
    if pad:
        parts.append(jnp.zeros((pad * LANES,), dtype))
    return jnp.concatenate(parts).reshape(r + pad, LANES), offs


def _unpack(packed, offs, shapes, lead=()):
    out = []
    for off, shp in zip(offs, shapes):
        size = int(np.prod(shp))
        nr = -(-size // LANES)
        flat = packed[..., off:off + nr, :].reshape(lead + (nr * LANES,))
        out.append(flat[..., :size].reshape(lead + tuple(shp)))
    return out


BIG = (("w_ada", 2), ("w_in", 2), ("w_br_a", 2), ("w_br_b", 2), ("w_br_c", 2), ("w_o", 1), ("w_gu", 2), ("w_down", 1))
CONV = ("dn_conv_w", "sc_conv_w")
REPL = ("c_ctx", "b_ada", "norm1_g", "norm2_g", "dn_a_log", "dn_dt_bias", "dn_norm_g", "pool_w", "pool_scale",
        "final_norm_g")
WEIGHTS = ("c_ctx", "w_ada", "b_ada", "norm1_g", "norm2_g", "w_in", "dn_conv_w", "dn_a_log", "dn_dt_bias", "dn_norm_g",
           "pool_w", "pool_scale", "sc_conv_w", "w_br_a", "w_br_b", "w_br_c", "w_o", "w_gu", "w_down", "final_norm_g")
TOKEN_TILE = 256


def _join(blocks, axis):
    nd, nl, r, c = blocks.shape
    if axis == 2:
        return blocks.transpose(1, 2, 0, 3).reshape(nl, r, nd * c)
    return blocks.transpose(1, 0, 2, 3).reshape(nl, nd * r, c)


def _split(full, axis):
    nl, r, c = full.shape
    if axis == 2:
        return full.reshape(nl, r, N_CHIP, 2, c // N_DEV).transpose(3, 2, 0, 1, 4)
    return full.reshape(nl, N_CHIP, 2, r // N_DEV, c).transpose(2, 1, 0, 3, 4)


PRESPLIT = ("w_in", "w_gu")


def _presplit(layer_grads):
    return [g if k in PRESPLIT else _split(g[None], dict(BIG)[k]) for k, g in zip(LAYERED, layer_grads)]


class _Comm:
    def __init__(self, late_shards, w_ada, b_ada):
        self.packed = [k for k in LATE if k not in PRESPLIT]
        self.shapes = [late_shards[k].shape for k in self.packed]
        pack, self.offs = _pack([late_shards[k] for k in self.packed], BF16, BF16_ROWS)
        self.late = [pack, late_shards["w_gu"].astype(BF16)]
        self.w_ada, self.b_ada = w_ada.astype(BF16), b_ada
        self.g2s = None

    def adaln_fwd(self, c, c_ctx):
        my = _dev_index(_me())
        ncol = self.w_ada.shape[2]
        c_all = _broadcast_small(c.reshape(8, LANES), "c_exchange").reshape(N_DEV, D)
        self.cc = jnp.concatenate([c_all, c_ctx[None, :], jnp.zeros((CC_ROWS - N_DEV - 1, D), F32)], axis=0)
        b_cols = lax.dynamic_slice_in_dim(self.b_ada, my * ncol, ncol, axis=1).reshape(NL, 1, ncol)
        cols = _mod_fwd(self.cc, self.w_ada, b_cols)
        got = _broadcast_small(cols.reshape(-1, LANES), "mods_exchange").reshape(N_DEV, NL, CC_ROWS, ncol)
        return my, got.transpose(1, 2, 0, 3).reshape(NL, CC_ROWS, N_DEV * ncol)

    def adaln_bwd(self, dmods):
        my = _dev_index(_me())
        ncol = self.w_ada.shape[2]
        got = _broadcast_small(dmods.reshape(-1, LANES), "dmods_exchange")
        rows = got.reshape(N_DEV, NL, 2, 6 * D)
        ctx_sum = _sum_small(rows[:, :, 0].reshape(N_DEV, -1, LANES)).reshape(NL, 1, 6 * D)
        db = _sum_small(rows.transpose(0, 2, 1, 3).reshape(2 * N_DEV, -1, LANES)).reshape(NL, 6 * D)
        dm = jnp.concatenate([rows[:, :, 1].transpose(1, 0, 2), ctx_sum,
                              jnp.zeros((NL, CC_ROWS - N_DEV - 1, 6 * D), F32)], axis=1)
        dw, dcc = _mod_bwd(self.cc, self.w_ada, lax.dynamic_slice_in_dim(dm, my * ncol, ncol, axis=2))
        return dict(w_ada=dw, b_ada=db, c_ctx=dcc[CTX_ROW])

    def late_weights_chips(self):
        return _chip_gather(self.late)

    def late_weights_pair(self, riding):
        self.chips = [_with_own(r, a, _my_chip()) for r, a in zip(riding, self.late)]
        return _pair_gather(self.chips)

    def late_weights(self, riding):
        on_south = lax.axis_index("c") == 0
        by_dev = []
        for mine, other in zip(self.chips, riding):
            both = jnp.stack([jnp.where(on_south, mine, other), jnp.where(on_south, other, mine)], axis=1)
            by_dev.append(both.reshape((N_DEV,) + mine.shape[1:]))
        shards = _unpack(by_dev[0], self.offs, self.shapes, (N_DEV,))
        return dict({k: _join(blocks, dict(BIG)[k]) for k, blocks in zip(self.packed, shards)}, w_gu=by_dev[1])

    def grad_pair_rider(self, layer_grads):
        self.g2s = _presplit(layer_grads)
        return _pair_exchange(self.g2s)

    def grad_chip_rider(self, got):
        self.sums = [_pair_sum(g2, gt) for g2, gt in zip(self.g2s, got)]
        return _chip_exchange(self.sums)

    def grad_chip_done(self, riding):
        return _chip_exchange_done(self.sums, riding)


def kernel(x, c, ctx, c_ctx, w_ada, b_ada, norm1_g, norm2_g, w_in, dn_conv_w, dn_a_log, dn_dt_bias, dn_norm_g, pool_w, pool_scale, sc_conv_w, w_br_a, w_br_b, w_br_c, w_o, w_gu, w_down, final_norm_g, loss_target, m_c_ctx, m_w_ada, m_b_ada, m_norm1_g, m_norm2_g, m_w_in, m_dn_conv_w, m_dn_a_log, m_dn_dt_bias, m_dn_norm_g, m_pool_w, m_pool_scale, m_sc_conv_w, m_w_br_a, m_w_br_b, m_w_br_c, m_w_o, m_w_gu, m_w_down, m_final_norm_g, v_c_ctx, v_w_ada, v_b_ada, v_norm1_g, v_norm2_g, v_w_in, v_dn_conv_w, v_dn_a_log, v_dn_dt_bias, v_dn_norm_g, v_pool_w, v_pool_scale, v_sc_conv_w, v_w_br_a, v_w_br_b, v_w_br_c, v_w_o, v_w_gu, v_w_down, v_final_norm_g):
    loc = dict(c_ctx=c_ctx, w_ada=w_ada, b_ada=b_ada, norm1_g=norm1_g, norm2_g=norm2_g, w_in=w_in, dn_conv_w=dn_conv_w,
               dn_a_log=dn_a_log, dn_dt_bias=dn_dt_bias, dn_norm_g=dn_norm_g, pool_w=pool_w, pool_scale=pool_scale,
               sc_conv_w=sc_conv_w, w_br_a=w_br_a, w_br_b=w_br_b, w_br_c=w_br_c, w_o=w_o, w_gu=w_gu, w_down=w_down,
               final_norm_g=final_norm_g)
    mom_m = dict(c_ctx=m_c_ctx, w_ada=m_w_ada, b_ada=m_b_ada, norm1_g=m_norm1_g, norm2_g=m_norm2_g, w_in=m_w_in,
                 dn_conv_w=m_dn_conv_w, dn_a_log=m_dn_a_log, dn_dt_bias=m_dn_dt_bias, dn_norm_g=m_dn_norm_g,
                 pool_w=m_pool_w, pool_scale=m_pool_scale, sc_conv_w=m_sc_conv_w, w_br_a=m_w_br_a, w_br_b=m_w_br_b,
                 w_br_c=m_w_br_c, w_o=m_w_o, w_gu=m_w_gu, w_down=m_w_down, final_norm_g=m_final_norm_g)
    mom_v = dict(c_ctx=v_c_ctx, w_ada=v_w_ada, b_ada=v_b_ada, norm1_g=v_norm1_g, norm2_g=v_norm2_g, w_in=v_w_in,
                 dn_conv_w=v_dn_conv_w, dn_a_log=v_dn_a_log, dn_dt_bias=v_dn_dt_bias, dn_norm_g=v_dn_norm_g,
                 pool_w=v_pool_w, pool_scale=v_pool_scale, sc_conv_w=v_sc_conv_w, w_br_a=v_w_br_a, w_br_b=v_w_br_b,
                 w_br_c=v_w_br_c, w_o=v_w_o, w_gu=v_w_gu, w_down=v_w_down, final_norm_g=v_final_norm_g)
    my = _dev_index(_me())

    conv_pack, conv_offs = _pack([loc[k] for k in CONV], F32, 8)
    w_in_all, conv_all = _allgather([w_in.astype(BF16), conv_pack])
    full = dict({k: loc[k] for k in REPL}, w_in=w_in_all)
    for k, blocks in zip(CONV, _unpack(conv_all, conv_offs, [loc[k].shape for k in CONV], (N_DEV,))):
        full[k] = _join(blocks, 2)

    loss8, grad_x, g, recv_l1 = _device_step(x[0], c, ctx[0], loss_target[0], full, TOKEN_TILE,
                                             comm=_Comm({k: loc[k] for k in LATE}, w_ada, b_ada))

    tail = _presplit([g[k][0] for k in LAYERED])
    got = _run_rider(_pair_exchange(tail), "pair_exchange")
    sums = [_pair_sum(a, b) for a, b in zip(tail, got)]
    recv_tail = _chip_exchange_done(sums, _run_rider(_chip_exchange(sums), "chip_exchange"))

    small_names = REPL + CONV
    summed = [k for k in small_names if k != "b_ada"]
    small_pack, small_offs = _pack([g[k] for k in summed] + [loss8[0:1, 0:1]], F32, 8)
    small_sum = _sum_small(_broadcast_small(small_pack))
    sums = _unpack(small_sum, small_offs, [g[k].shape for k in summed] + [(1, 1)])
    grads = dict(zip(summed, sums[:-1]), b_ada=g["b_ada"], w_ada=g["w_ada"])
    loss = sums[-1][0, 0]
    for k in CONV:
        w = loc[k].shape[2]
        grads[k] = lax.dynamic_slice_in_dim(grads[k], my * w, w, axis=2)

    delta, new_m, new_v = {}, {}, {}
    for i, k in enumerate(LAYERED):
        grads[k], delta[k], new_m[k], new_v[k] = _sum_adam([recv_tail[i], recv_l1[i]], loc[k], mom_m[k], mom_v[k])
    delta["w_ada"], new_m["w_ada"], new_v["w_ada"] = _adam_big(w_ada, g["w_ada"], m_w_ada, v_w_ada)
    packs = [_pack([src[k] for k in small_names], F32, 8)[0] for src in (loc, grads, mom_m, mom_v)]
    _, offs = _pack([loc[k] for k in small_names], F32, 8)
    shapes = [loc[k].shape for k in small_names]
    for dst, packed in zip((delta, new_m, new_v), _adam_small(*packs)):
        dst.update(zip(small_names, _unpack(packed, offs, shapes)))

    return (loss, grad_x[None], *[grads[k] for k in WEIGHTS], *[delta[k] for k in WEIGHTS],
            *[new_m[k] for k in WEIGHTS], *[new_v[k] for k in WEIGHTS])
```

```python
import functools

import numpy as np
import jax
import jax.numpy as jnp
from jax import lax
from jax.experimental import pallas as pl
from jax.experimental.pallas import tpu as pltpu

F32 = jnp.float32
BF16 = jnp.bfloat16
HI = lax.Precision.HIGHEST

D = 1024
NL = 2
NH = 4
DH = 128
DN = NH * DH
CH = 64
GW = 64
PW = 256
DFF = 2816
EPS = 1e-6
N_DEV = 8
N_CHIP = 4
MESH_ID = pl.DeviceIdType.MESH
HBM_SPEC = pl.BlockSpec(memory_space=pltpu.HBM)
LANES = 128
BF16_ROWS = 16
VMEM_MB = 56
DW_ACC_MB = 12

ADAM_LR, ADAM_B1, ADAM_B2, ADAM_EPS, ADAM_WD, ADAM_STEP = 0.001, 0.9, 0.999, 1e-08, 0.01, 10

IN_BOUNDS = (0, 1536, 2048, 2064, 2320, 2576, 2832, 3088, 6160)
IN_WIDTHS = (1536, 512, 128, 256, 256, 256, 256, 3072)
POOL_WIN = ((1, 0), (2, 1), (4, 3), (8, 7))

NN = ((1,), (0,))
NT = ((1,), (1,))
TN = ((0,), (0,))


def _dot(a, b, dims, hi=False):
    if hi:
        prec = lax.Precision.HIGH if hi == "x3" else HI
        return lax.dot_general(a, b, (dims, ((), ())), precision=prec, preferred_element_type=F32)
    return lax.dot_general(a.astype(BF16), b.astype(BF16), (dims, ((), ())), preferred_element_type=F32)


def _S(shape, dtype=F32):
    return jax.ShapeDtypeStruct(tuple(shape), dtype)


def _full(shape):
    nd = len(shape)
    return pl.BlockSpec(tuple(shape), lambda *_: (0,) * nd)


def _rows(tt, w):
    return pl.BlockSpec((tt, w), lambda i: (i, 0))


class _Rider:
    def __init__(self, ins, out_shapes, sems, copies):
        self.ins, self.out_shapes, self.sems, self.copies = list(ins), list(out_shapes), list(sems), copies

    def start(self, ins, outs, sems):
        remote, local = self.copies(ins, outs, sems)
        for cp in local + remote:
            cp.start()

    def wait(self, ins, outs, sems):
        remote, local = self.copies(ins, outs, sems)
        for cp in remote:
            cp.wait_recv()
        for cp in remote:
            cp.wait_send()
        for cp in local:
            cp.wait()


def _call(body, name, grid, in_specs, out_specs, out_shape, scratch=(), rider=None):
    params = pltpu.CompilerParams(dimension_semantics=("arbitrary",) * len(grid), vmem_limit_bytes=VMEM_MB << 20)
    if rider is None:
        return pl.pallas_call(body, name=name, grid=grid, in_specs=in_specs, out_specs=out_specs, out_shape=out_shape,
                              scratch_shapes=list(scratch), compiler_params=params)
    single = not isinstance(out_shape, (list, tuple))
    out_specs, out_shape = ([out_specs], [out_shape]) if single else (list(out_specs), list(out_shape))
    n_in, n_out, n_scr = len(in_specs), len(out_shape), len(scratch)
    r_in, r_out = len(rider.ins), len(rider.out_shapes)

    def hosted(*refs):
        ins, refs = refs[:n_in + r_in], refs[n_in + r_in:]
        outs, scr = refs[:n_out + r_out], refs[n_out + r_out:]
        riding = (ins[n_in:], outs[n_out:], scr[n_scr:])

        @pl.when(pl.program_id(0) == 0)
        def _():
            rider.start(*riding)

        body(*ins[:n_in], *outs[:n_out], *scr[:n_scr])

        @pl.when(pl.program_id(0) == grid[0] - 1)
        def _():
            rider.wait(*riding)

    call = pl.pallas_call(
        hosted, name=name, grid=grid, in_specs=list(in_specs) + [HBM_SPEC] * r_in,
        out_specs=out_specs + [HBM_SPEC] * r_out, out_shape=out_shape + rider.out_shapes,
        scratch_shapes=list(scratch) + rider.sems, compiler_params=params)

    def run(*args):
        res = call(*args, *rider.ins)
        own = res[:n_out]
        return (own[0] if single else own), list(res[n_out:])

    return run


def _iota(shape, axis):
    return lax.broadcasted_iota(jnp.int32, shape, axis)


def _colsum(a):
    return jnp.sum(a, axis=0, keepdims=True)


def _silu(x):
    return x * jax.nn.sigmoid(x)


def _modulate(x, g, sh, sc):
    xn = x * lax.rsqrt(jnp.mean(x * x, axis=-1, keepdims=True) + EPS)
    return (xn * g) * (1.0 + sc) + sh


def _stream_rows(mv_ref, i, tt, tc, k):
    isc = (i * tt + _iota((tt, 1), 0)) < tc
    return isc, jnp.where(isc, mv_ref[k:k + 1, :], mv_ref[3 + k:4 + k, :])


def _acc_stream(ref, k, isc, val):
    ref[k:k + 1, :] += _colsum(jnp.where(isc, val, 0.0))
    ref[3 + k:4 + k, :] += _colsum(jnp.where(isc, 0.0, val))


CC_ROWS = 16
CTX_ROW = 8


def _mod_cols(n):
    return 1536 if n % 1536 == 0 else n


def _mod_fwd(cc, w_ada, b_ada3):
    n = w_ada.shape[2]
    ct = _mod_cols(n)

    def body(cc_ref, w_ref, b_ref, o_ref):
        o_ref[0] = _dot(_silu(cc_ref[...]), w_ref[0], NN) + b_ref[0]

    return _call(
        body, "mod_fwd", (NL, n // ct),
        [pl.BlockSpec((CC_ROWS, D), lambda l, j: (0, 0)), pl.BlockSpec((1, D, ct), lambda l, j: (l, 0, j)),
         pl.BlockSpec((1, 1, ct), lambda l, j: (l, 0, j))],
        pl.BlockSpec((1, CC_ROWS, ct), lambda l, j: (l, 0, j)), _S((NL, CC_ROWS, n)))(cc, w_ada, b_ada3)


def _mod_bwd(cc, w_ada, dmods):
    n = w_ada.shape[2]
    ct = _mod_cols(n)

    def body(cc_ref, w_ref, dm_ref, dw_ref, dcc_ref):
        first = (pl.program_id(0) == 0) & (pl.program_id(1) == 0)
        cc_ = cc_ref[...]
        sg = jax.nn.sigmoid(cc_)
        dm = dm_ref[0]
        dw_ref[0] = _dot(cc_ * sg, dm, TN)

        @pl.when(first)
        def _():
            dcc_ref[...] = jnp.zeros_like(dcc_ref)

        dcc_ref[...] += _dot(dm, w_ref[0], NT) * (sg * (1.0 + cc_ * (1.0 - sg)))

    return _call(
        body, "mod_bwd", (NL, n // ct),
        [pl.BlockSpec((CC_ROWS, D), lambda l, j: (0, 0)), pl.BlockSpec((1, D, ct), lambda l, j: (l, 0, j)),
         pl.BlockSpec((1, CC_ROWS, ct), lambda l, j: (l, 0, j))],
        [pl.BlockSpec((1, D, ct), lambda l, j: (l, 0, j)), pl.BlockSpec((CC_ROWS, D), lambda l, j: (0, 0))],
        [_S((NL, D, n)), _S((CC_ROWS, D))])(cc, w_ada, dmods)


def _cols(n, tt):
    return pl.BlockSpec((n, tt), lambda i: (0, i))


def _inproj_fwd(X, mv, g, ws, tc, tt):
    T = X.shape[0]
    nw = len(ws)

    def body(x_ref, mv_ref, g_ref, *refs):
        w_refs, ht_ref, p_refs = refs[:nw], refs[nw], refs[nw + 1:]
        i = pl.program_id(0)
        _, sh = _stream_rows(mv_ref, i, tt, tc, 0)
        _, sc = _stream_rows(mv_ref, i, tt, tc, 1)
        h = _modulate(x_ref[...], g_ref[...], sh, sc)
        ht_ref[...] = h.T.astype(BF16)
        hb = h.astype(BF16)
        for w_ref, p_ref in zip(w_refs, p_refs):
            p_ref[...] = jnp.dot(hb, w_ref[...], preferred_element_type=F32)

    return _call(
        body, "inproj_fwd", (T // tt,),
        [_rows(tt, D), _full((8, D)), _full((1, D))] + [_full(w.shape) for w in ws],
        [_cols(D, tt)] + [_rows(tt, w.shape[1]) for w in ws],
        [_S((D, T), BF16)] + [_S((T, w.shape[1])) for w in ws])(X, mv, g, *ws)


def _inproj_bwd(X, mv, g, ws, dps, dp_w, dres, tc, tt):
    T = X.shape[0]
    nw, nd = len(ws), len(dps)

    def body(x_ref, mv_ref, g_ref, dres_ref, *refs):
        w_refs, dp_refs = refs[:nw], refs[nw:nw + nd]
        dx_ref, dg_ref, dm_ref = refs[nw + nd:]
        i = pl.program_id(0)
        isc, sh = _stream_rows(mv_ref, i, tt, tc, 0)
        _, sc = _stream_rows(mv_ref, i, tt, tc, 1)
        dh = None
        for dp_ref, k in zip(dp_refs, dp_w):
            t = _dot(dp_ref[...], w_refs[k][...], NT)
            dh = t if dh is None else dh + t
        _, vjp = jax.vjp(_modulate, x_ref[...], g_ref[...], sh, sc)
        dx, dg, dsh, dsc = vjp(dh)
        dx_ref[...] = dres_ref[...] + dx

        @pl.when(i == 0)
        def _():
            dg_ref[...] = jnp.zeros_like(dg_ref)
            dm_ref[...] = jnp.zeros_like(dm_ref)

        dg_ref[...] += dg
        _acc_stream(dm_ref, 0, isc, dsh)
        _acc_stream(dm_ref, 1, isc, dsc)

    return _call(
        body, "inproj_bwd", (T // tt,),
        [_rows(tt, D), _full((8, D)), _full((1, D)), _rows(tt, D)] + [_full(w.shape) for w in ws]
        + [_rows(tt, dp.shape[1]) for dp in dps],
        [_rows(tt, D), _full((1, D)), _full((8, D))],
        [_S((T, D)), _S((1, D)), _S((8, D))])(X, mv, g, dres, *ws, *dps)


def _dw(At, B, tt):
    K, T = At.shape
    N = B.shape[1]
    tt = 3 * tt if T % (3 * tt) == 0 else tt
    tn = max(t for t in range(LANES, N + 1, LANES) if N % t == 0 and K * t * 4 <= DW_ACC_MB << 20)

    def body(a_ref, b_ref, o_ref):
        @pl.when(pl.program_id(1) == 0)
        def _():
            o_ref[...] = jnp.zeros_like(o_ref)

        o_ref[...] += _dot(a_ref[...], b_ref[...], NN)

    return _call(
        body, "dw", (N // tn, T // tt),
        [pl.BlockSpec((K, tt), lambda j, i: (0, i)), pl.BlockSpec((tt, tn), lambda j, i: (i, j))],
        pl.BlockSpec((K, tn), lambda j, i: (0, j)), _S((K, N)))(At, B)


def _halo_specs(T, tt, cw, col):
    r8, nb8 = tt // 8, T // 8
    return [pl.BlockSpec((tt, cw), lambda j, i: (i, col(j))),
            pl.BlockSpec((8, cw), lambda j, i: (jnp.maximum(i * r8 - 1, 0), col(j))),
            pl.BlockSpec((8, cw), lambda j, i: (jnp.minimum((i + 1) * r8, nb8 - 1), col(j)))]


def _shifts(a, prev8, next8, i, tt, tc, T):
    r = _iota((tt, 1), 0)
    t = i * tt + r
    dn = jnp.where(r == 0, prev8[7:8, :], pltpu.roll(a, 1, 0))
    dn = jnp.where((t == 0) | (t == tc), 0.0, dn)
    up = jnp.where(r == tt - 1, next8[0:1, :], pltpu.roll(a, tt - 1, 0))
    up = jnp.where((t == T - 1) | (t == tc - 1), 0.0, up)
    return dn, up


def _dn_post(y, part):
    a = _silu(y)
    nrm = lax.rsqrt(jnp.sum(a * a, axis=-1, keepdims=True) + EPS)
    f = jnp.where(part == 0, nrm * (DH ** -0.5), jnp.where(part == 1, nrm, 1.0))
    return a * f


def _conv3(w_ref, dn, mid, up):
    return w_ref[0:1, :] * dn + w_ref[1:2, :] * mid + w_ref[2:3, :] * up


def _dnprep_fwd(pq, cw, tc, tt):
    T = pq.shape[0]

    def body(p_ref, pp_ref, pn_ref, w_ref, a_ref):
        part, i = pl.program_id(0), pl.program_id(1)
        p = p_ref[...]
        dn, up = _shifts(p, pp_ref[...], pn_ref[...], i, tt, tc, T)
        y = _conv3(w_ref, dn, p, up)
        for h in range(NH):
            a_ref[:, _hs(h)] = _dn_post(y[:, _hs(h)], part)

    return _call(
        body, "dnprep_fwd", (3, T // tt),
        _halo_specs(T, tt, DN, lambda j: j) + [pl.BlockSpec((3, DN), lambda j, i: (0, j))],
        pl.BlockSpec((tt, DN), lambda j, i: (i, j)), _S((T, 3 * DN)))(pq, pq, pq, cw)


def _dnprep_bwd_act(pq, cw, da_f, da_b, tc, tt):
    T = pq.shape[0]

    def body(p_ref, pp_ref, pn_ref, w_ref, df_ref, db_ref, dy_ref):
        part, i = pl.program_id(0), pl.program_id(1)
        p = p_ref[...]
        dn, up = _shifts(p, pp_ref[...], pn_ref[...], i, tt, tc, T)
        y = _conv3(w_ref, dn, p, up)
        for h in range(NH):
            _, vjp = jax.vjp(lambda yh: _dn_post(yh, part), y[:, _hs(h)])
            dy_ref[:, _hs(h)] = vjp(df_ref[:, _hs(h)] + db_ref[:, _hs(h)])[0]

    blk = pl.BlockSpec((tt, DN), lambda j, i: (i, j))
    return _call(
        body, "dnprep_bwd_act", (3, T // tt),
        _halo_specs(T, tt, DN, lambda j: j) + [pl.BlockSpec((3, DN), lambda j, i: (0, j)), blk, blk],
        blk, _S((T, 3 * DN)))(pq, pq, pq, cw, da_f, da_b)


def _conv_bwd(dy, p, cw, tc, tt):
    T, W = p.shape
    cb = DN

    def body(dy_ref, dyp_ref, dyn_ref, p_ref, pp_ref, pn_ref, w_ref, dp_ref, dw_ref):
        i = pl.program_id(1)
        dy, p_ = dy_ref[...], p_ref[...]
        ddn, dup = _shifts(dy, dyp_ref[...], dyn_ref[...], i, tt, tc, T)
        dp_ref[...] = _conv3(w_ref, dup, dy, ddn)
        pdn, pup = _shifts(p_, pp_ref[...], pn_ref[...], i, tt, tc, T)

        @pl.when(i == 0)
        def _():
            dw_ref[...] = jnp.zeros_like(dw_ref)

        dw_ref[0:1, :] += _colsum(dy * pdn)
        dw_ref[1:2, :] += _colsum(dy * p_)
        dw_ref[2:3, :] += _colsum(dy * pup)

    wspec = pl.BlockSpec((3, cb), lambda j, i: (0, j))
    return _call(
        body, "conv_bwd", (W // cb, T // tt),
        _halo_specs(T, tt, cb, lambda j: j) * 2 + [wspec],
        [pl.BlockSpec((tt, cb), lambda j, i: (i, j)), wspec], [_S((T, W)), _S((3, W))])(dy, dy, dy, p, p, p, cw)


def _sc_fwd(sx, sb, sc_, cw, tc, tt):
    T = sx.shape[0]

    def body(x_ref, xp_ref, xn_ref, c_ref, cp_ref, cn_ref, b_ref, w_ref, y_ref):
        i = pl.program_id(1)
        u = c_ref[...] * x_ref[...]
        dn, up = _shifts(u, cp_ref[...] * xp_ref[...], cn_ref[...] * xn_ref[...], i, tt, tc, T)
        y_ref[...] = b_ref[...] * _conv3(w_ref, dn, u, up)

    blk = pl.BlockSpec((tt, LANES), lambda j, i: (i, j))
    return _call(
        body, "sc_fwd", (PW // LANES, T // tt),
        _halo_specs(T, tt, LANES, lambda j: j) * 2 + [blk, pl.BlockSpec((3, LANES), lambda j, i: (0, j))],
        blk, _S((T, PW)))(sx, sx, sx, sc_, sc_, sc_, sb, cw)


def _sc_bwd(sx, sb, sc_, cw, dy, tc, tt):
    T = sx.shape[0]

    def body(x_ref, xp_ref, xn_ref, c_ref, cp_ref, cn_ref, b_ref, bp_ref, bn_ref, dy_ref, dyp_ref, dyn_ref, w_ref,
             dx_ref, db_ref, dc_ref, dw_ref):
        i = pl.program_id(1)
        x, c, dy_ = x_ref[...], c_ref[...], dy_ref[...]
        u = c * x
        udn, uup = _shifts(u, cp_ref[...] * xp_ref[...], cn_ref[...] * xn_ref[...], i, tt, tc, T)
        db_ref[...] = dy_ * _conv3(w_ref, udn, u, uup)
        e = dy_ * b_ref[...]
        edn, eup = _shifts(e, dyp_ref[...] * bp_ref[...], dyn_ref[...] * bn_ref[...], i, tt, tc, T)
        du = _conv3(w_ref, eup, e, edn)
        dx_ref[...] = du * c
        dc_ref[...] = du * x

        @pl.when(i == 0)
        def _():
            dw_ref[...] = jnp.zeros_like(dw_ref)

        dw_ref[0:1, :] += _colsum(e * udn)
        dw_ref[1:2, :] += _colsum(e * u)
        dw_ref[2:3, :] += _colsum(e * uup)

    blk = pl.BlockSpec((tt, LANES), lambda j, i: (i, j))
    wspec = pl.BlockSpec((3, LANES), lambda j, i: (0, j))
    return _call(
        body, "sc_bwd", (PW // LANES, T // tt),
        _halo_specs(T, tt, LANES, lambda j: j) * 4 + [wspec],
        [blk, blk, blk, wspec], [_S((T, PW))] * 3 + [_S((3, PW))])(
            sx, sx, sx, sc_, sc_, sc_, sb, sb, sb, dy, dy, dy, cw)


def _group_select(vals):
    g = _iota((1, PW), 1) // (PW // len(POOL_WIN))
    return jnp.where(g == 0, vals[0], jnp.where(g == 1, vals[1], jnp.where(g == 2, vals[2], vals[3])))


def _nested_box(get, mirror):
    acc, outs, pl_, ph_ = get(0), [], 0, 0
    for lo, hi in POOL_WIN:
        if mirror:
            lo, hi = hi, lo
        for k in range(pl_ + 1, lo + 1):
            acc = acc + get(-k)
        for k in range(ph_ + 1, hi + 1):
            acc = acc + get(k)
        pl_, ph_ = lo, hi
        outs.append(acc)
    return _group_select(outs)


def _box_tokens(a, n, mirror):
    idx = _iota((n, 1), 0)

    def get(k):
        if k == 0:
            return a
        return jnp.where((idx + k >= 0) & (idx + k < n), pltpu.roll(a, (-k) % n, 0), 0.0)

    return _nested_box(get, mirror)


def _inv_count(pos, n):
    return _group_select([1.0 / (jnp.minimum(pos + hi, n - 1) - jnp.maximum(pos - lo, 0) + 1).astype(F32)
                          for lo, hi in POOL_WIN])


def _pool_rows(ref, r, R, tc, mirror):
    def get(k):
        rr = r + k
        rc = jnp.clip(rr, 0, R - 1)
        v = ref[pl.ds(pl.multiple_of(tc + rc * GW, GW), GW), :]
        if mirror:
            v = v * _inv_count(jnp.full((1, PW), rc, jnp.int32), R)
        return jnp.where((rr >= 0) & (rr < R), v, 0.0)

    return _nested_box(get, mirror)


def _pool_fwd(u, pwbd, ps, tc):
    T = u.shape[0]
    R = (T - tc) // GW

    def body(u_ref, pw_ref, ps_ref, y_ref):
        pw, scale = pw_ref[...], ps_ref[...]
        uc = u_ref[0:tc, :]
        mc = _box_tokens(uc, tc, False) * _inv_count(_iota((tc, 1), 0), tc)
        y_ref[0:tc, :] = _dot(mc - uc, pw, NN) * scale
        inv_c = _inv_count(_iota((GW, 1), 0), GW)

        def row(r, carry):
            rs = _pool_rows(u_ref, r, R, tc, False) * _inv_count(jnp.full((1, PW), r, jnp.int32), R)
            m = _box_tokens(rs, GW, False) * inv_c
            sl = pl.ds(pl.multiple_of(tc + r * GW, GW), GW)
            y_ref[sl, :] = _dot(m - u_ref[sl, :], pw, NN) * scale
            return carry

        lax.fori_loop(0, R, row, 0)

    return pl.pallas_call(
        body, name="pool_fwd", out_shape=_S((T, PW)),
        compiler_params=pltpu.CompilerParams(vmem_limit_bytes=VMEM_MB << 20))(u, pwbd, ps)


def _pool_bwd(u, pwbd, ps, dy, tc):
    T = u.shape[0]
    R = (T - tc) // GW

    def body(u_ref, pw_ref, ps_ref, dy_ref, du_ref, dpw_ref, dps_ref, dd_ref):
        pw, scale = pw_ref[...], ps_ref[...]
        dpw_ref[...] = jnp.zeros_like(dpw_ref)
        dps_ref[...] = jnp.zeros_like(dps_ref)

        def back(d, dy_):
            dz = dy_ * scale
            dpw_ref[...] += _dot(d, dz, TN)
            dps_ref[...] += _colsum(dy_ * _dot(d, pw, NN))
            return _dot(dz, pw, NT)

        uc = u_ref[0:tc, :]
        inv_cc = _inv_count(_iota((tc, 1), 0), tc)
        ddc = back(_box_tokens(uc, tc, False) * inv_cc - uc, dy_ref[0:tc, :])
        du_ref[0:tc, :] = _box_tokens(ddc * inv_cc, tc, True) - ddc
        inv_c = _inv_count(_iota((GW, 1), 0), GW)

        def row1(r, carry):
            rs = _pool_rows(u_ref, r, R, tc, False) * _inv_count(jnp.full((1, PW), r, jnp.int32), R)
            m = _box_tokens(rs, GW, False) * inv_c
            sl = pl.ds(pl.multiple_of(tc + r * GW, GW), GW)
            dd_ref[sl, :] = back(m - u_ref[sl, :], dy_ref[sl, :])
            return carry

        lax.fori_loop(0, R, row1, 0)

        def row2(r, carry):
            t1 = _pool_rows(dd_ref, r, R, tc, True)
            sl = pl.ds(pl.multiple_of(tc + r * GW, GW), GW)
            du_ref[sl, :] = _box_tokens(t1 * inv_c, GW, True) - dd_ref[sl, :]
            return carry

        lax.fori_loop(0, R, row2, 0)

    return pl.pallas_call(
        body, name="pool_bwd", out_shape=[_S((T, PW)), _S((PW, PW)), _S((1, PW))],
        scratch_shapes=[pltpu.VMEM((T, PW), F32)],
        compiler_params=pltpu.CompilerParams(vmem_limit_bytes=VMEM_MB << 20))(u, pwbd, ps, dy)


def _scan_consts():
    i = np.arange(CH)
    lower = (i[:, None] >= i[None, :]).astype(np.float32)
    return jnp.asarray(np.stack([lower, lower.T])), jnp.asarray(np.stack([lower.T, lower]))


def _gates(pab, al, dtb, csum):
    sp_in = pab + dtb
    sp = jnp.maximum(sp_in, 0.0) + jnp.log(1.0 + jnp.exp(-jnp.abs(sp_in)))
    nexp = -jnp.exp(al)
    gm = nexp * sp
    return gm, jax.nn.sigmoid(pab), _dot(csum, gm, NN, hi=True), sp_in, nexp


def _lane_col(m, j):
    return jnp.sum(jnp.where(_iota(m.shape, 1) == j, m, 0.0), axis=1, keepdims=True)


def _hs(h):
    return slice(h * DH, (h + 1) * DH)


HS = NH * CH
X3 = "x3"


def _stack(x, base=0):
    return jnp.concatenate([x[:, base + h * DH:base + (h + 1) * DH] for h in range(NH)], axis=0)


def _heads(st):
    return [st[h * CH:(h + 1) * CH] for h in range(NH)]


def _rowsum(a):
    return jnp.sum(a, axis=1, keepdims=True)


def _row_of(col):
    e0 = (_iota((8, LANES), 1) == 0).astype(F32)
    return _dot(e0, jnp.broadcast_to(col, (HS, LANES)), NT, hi=True)[0:1, :]


def _inverses(nms):
    eye = (_iota((HS, HS), 0) == _iota((HS, HS), 1)).astype(F32)
    x0s, mps = [eye + nm for nm in nms], list(nms)
    for _ in range(5):
        mps = [_dot(mp, mp, NN) for mp in mps]
        x0s = [x0 + _dot(x0, mp, NN) for x0, mp in zip(x0s, mps)]
    rs = [eye - _dot(eye - nm, x0, NN, hi=X3) for nm, x0 in zip(nms, x0s)]
    return [x0 + _dot(x0, r, NN) for x0, r in zip(x0s, rs)]


def _dn_chunk_pre(qkv, pab, al, dtb, csum_d, d):
    gm, bm, gcm, sp_in, nexp = _gates(pab, al, dtb, csum_d)
    gc = jnp.concatenate([_lane_col(gcm, d * NH + h) for h in range(NH)], axis=0)
    beta = jnp.concatenate([_lane_col(bm, 8 + d * NH + h) for h in range(NH)], axis=0)
    q, k, v = _stack(qkv, 0), _stack(qkv, DN), _stack(qkv, 2 * DN)
    ii, jj = _iota((HS, HS), 0), _iota((HS, HS), 1)
    sh = CH.bit_length() - 1
    same = (ii >> sh) == (jj >> sh)
    incl = same & ((ii >= jj) if d == 0 else (ii <= jj))
    strict = same & ((ii > jj) if d == 0 else (ii < jj))
    Di = jnp.where(incl, jnp.exp(jnp.where(incl, gc - _row_of(gc), 0.0)), 0.0)
    Ds = jnp.where(strict, Di, 0.0)
    kb = k * beta
    kk = _dot(kb, k, NT)
    return dict(q=q, k=k, v=v, beta=beta, gc=gc, gm=gm, bm=bm, sp_in=sp_in, nexp=nexp, Di=Di, Ds=Ds, strict=strict,
                last=CH - 1 if d == 0 else 0, kb=kb, kk=kk)


def _dn_chunk_post(c, tm, uw=None):
    q, k, v, beta, gc, kb, last = (c[n] for n in ("q", "k", "v", "beta", "gc", "kb", "last"))
    E = jnp.exp(gc)
    gls = [gc[h * CH + last:h * CH + last + 1, :] for h in range(NH)]
    xs = jnp.exp(jnp.concatenate([jnp.broadcast_to(g, (CH, 1)) for g in gls], axis=0) - gc)
    qk = _dot(q, k, NT)
    u, w = uw if uw is not None else (_dot(tm, v * beta, NN, hi=X3), _dot(tm, kb * E, NN, hi=X3))
    return dict(c, tm=tm, E=E, gls=gls, xs=xs, qk=qk, u=u, w=w, ks=k * xs, qd=q * E, aqk=qk * c["Di"])


def _dn_chunks_bwd_math(cs, Ss, dS2s, dos, vns, dvns):
    I = range(len(cs))
    q, k, v, beta, tm, E, xs, kb, u, w = ([c[n] for c in cs] for n in ("q", "k", "v", "beta", "tm", "E", "xs", "kb", "u", "w"))
    doh, vnh, dvnh = ([_heads(a) for a in l] for l in (dos, vns, dvns))
    cat = lambda parts: jnp.concatenate(parts, axis=0)
    dqd = [cat([_dot(doh[i][h], Ss[i][h], NT) for h in range(NH)]) for i in I]
    dks = [cat([_dot(vnh[i][h], dS2s[i][h], NT) for h in range(NH)]) for i in I]
    dw = [-cat([_dot(dvnh[i][h], Ss[i][h], NT) for h in range(NH)]) for i in I]
    daqk = [_dot(dos[i], vns[i], NT) for i in I]
    drb = [_dot(tm[i], dvns[i], TN, hi=X3) for i in I]
    drw = [_dot(tm[i], dw[i], TN, hi=X3) for i in I]
    dA = [jnp.where(cs[i]["strict"], -(_dot(drb[i], u[i], NT) + _dot(drw[i], w[i], NT)), 0.0) for i in I]
    dM1 = [dA[i] * cs[i]["Ds"] for i in I]
    dM2 = [daqk[i] * cs[i]["Di"] for i in I]
    dkb = [_dot(dM1[i], k[i], NN) + drw[i] * E[i] for i in I]
    dk = [_dot(dM1[i], kb[i], TN) + _dot(dM2[i], q[i], TN) + dks[i] * xs[i] for i in I]
    dq = [_dot(dM2[i], k[i], NN) + dqd[i] * E[i] for i in I]
    on_diag = _iota((HS, HS), 0) == _iota((HS, HS), 1)
    out = []
    for i in I:
        G = dM1[i] * cs[i]["kk"] + dM2[i] * cs[i]["qk"]
        col = _rowsum(jnp.where(on_diag, jnp.broadcast_to(_colsum(G), (HS, HS)), 0.0))
        dxx = _rowsum(dks[i] * k[i]) * xs[i]
        dgc = _rowsum(G) - col + (_rowsum(dqd[i] * q[i]) + _rowsum(drw[i] * kb[i])) * E[i] - dxx
        at_last = _iota((CH, 1), 0) == cs[i]["last"]
        ends = []
        for h in range(NH):
            dgl = (_colsum(_rowsum(Ss[i][h] * dS2s[i][h])) * jnp.exp(cs[i]["gls"][h])
                   + _colsum(dxx[h * CH:(h + 1) * CH]))
            ends.append(jnp.where(at_last, dgl, 0.0))
        dbeta = _rowsum(drb[i] * v[i]) + _rowsum(dkb[i] * k[i])
        out.append((dq[i], dk[i] + dkb[i] * beta[i], drb[i] * beta[i], dgc + cat(ends), dbeta))
    return out


def _chunk_group(n, want=2):
    g = want
    while n % g:
        g //= 2
    return g


def _dn_chunks_fwd(qkv, pab, alr, dtr, rider=None):
    T = qkv.shape[0]
    n = T // CH
    G = _chunk_group(n, 4)
    csum, _ = _scan_consts()

    def body(q_ref, p_ref, cs_ref, al_ref, dt_ref, *outs):
        inst = [(g, d) for g in range(G) for d in range(2)]
        pres = [_dn_chunk_pre(q_ref[g * CH:(g + 1) * CH, :], p_ref[g * CH:(g + 1) * CH, :], al_ref[...], dt_ref[...],
                              cs_ref[d], d) for g, d in inst]
        tms = _inverses([-(p["kk"] * p["Ds"]) for p in pres])
        for (g, d), pre, tm in zip(inst, pres, tms):
            rows = slice(g * HS, (g + 1) * HS)
            u_ref, w_ref, ks_ref, qd_ref, aqk_ref, eg_ref, tm_ref = outs[7 * d:7 * d + 7]
            c = _dn_chunk_post(pre, tm)
            tm_ref[rows, :] = tm
            u_ref[rows, :] = c["u"]
            w_ref[rows, :] = c["w"].astype(BF16)
            ks_ref[rows, :] = c["ks"].astype(BF16)
            qd_ref[rows, :] = c["qd"].astype(BF16)
            aqk_ref[rows, :] = c["aqk"].astype(BF16)
            egs = [jnp.broadcast_to(jnp.exp(gl), (1, LANES)) for gl in c["gls"]]
            eg_ref[g * 8:(g + 1) * 8, :] = jnp.concatenate(egs + [jnp.zeros((8 - NH, LANES), F32)], axis=0)

    st = lambda w_: pl.BlockSpec((G * HS, w_), lambda i: (i, 0))
    one = [st(DH)] * 4 + [st(HS), pl.BlockSpec((G * 8, LANES), lambda i: (i, 0)), st(HS)]
    shp = [_S((n * HS, DH)), _S((n * HS, DH), BF16), _S((n * HS, DH), BF16), _S((n * HS, DH), BF16),
           _S((n * HS, HS), BF16), _S((n * 8, LANES)), _S((n * HS, HS))]
    res = _call(
        body, "dn_chunks_fwd", (n // G,),
        [_rows(G * CH, 3 * DN), _rows(G * CH, LANES), _full((2, CH, CH)), _full((1, LANES)), _full((1, LANES))],
        one * 2, shp * 2, rider=rider)(qkv, pab, csum, alr, dtr)
    outs, riding = (res, None) if rider is None else res
    parts = tuple(outs[:7]), tuple(outs[7:])
    return parts if rider is None else (parts, riding)


def _scan_plan(n, ncx):
    sg = 2 if n % 2 == 0 and ncx % 2 == 0 else 1
    ng, ncg = n // sg, ncx // sg
    return sg, ((lambda i: i), (lambda i: jnp.where(i < ncg, ncg - 1 - i, ng - 1 - (i - ncg))))


def _scan_specs(order, sg):
    st = lambda w_: pl.BlockSpec((sg * HS, w_), lambda i: (order(i), 0))
    return dict(st=st(DH), aqk=st(HS), eg=pl.BlockSpec((sg * 8, LANES), lambda i: (order(i), 0)),
                tok=pl.BlockSpec((sg * CH, DN), lambda i: (order(i), 0)),
                state=pl.BlockSpec((sg, DN, DH), lambda i: (order(i), 0, 0)))


def _scan_fwd(parts, T, tc, rider=None):
    n = T // CH
    sg, orders = _scan_plan(n, tc // CH)

    def body(*refs):
        S_f, S_b = refs[-2:]

        @pl.when(pl.program_id(0) == 0)
        def _():
            S_f[...] = jnp.zeros_like(S_f)
            S_b[...] = jnp.zeros_like(S_b)

        for g in range(sg):
            for d, S in enumerate((S_f, S_b)):
                u_ref, w_ref, ks_ref, qd_ref, aqk_ref, eg_ref = refs[6 * d:6 * d + 6]
                o_ref, ss_ref, vn_ref = refs[12 + 3 * d:15 + 3 * d]
                k = g if d == 0 else sg - 1 - g
                rows = slice(k * HS, (k + 1) * HS)
                ss_ref[k] = S[...]
                Sh = [S[_hs(h), :] for h in range(NH)]
                wh, ksh, qdh = _heads(w_ref[rows, :]), _heads(ks_ref[rows, :]), _heads(qd_ref[rows, :])
                vn = u_ref[rows, :] - jnp.concatenate([_dot(wh[h], Sh[h], NN) for h in range(NH)], axis=0)
                vn_ref[rows, :] = vn
                av, vnh = _heads(_dot(aqk_ref[rows, :], vn, NN)), _heads(vn)
                for h in range(NH):
                    o_ref[k * CH:(k + 1) * CH, _hs(h)] = _dot(qdh[h], Sh[h], NN) + av[h]
                    S[_hs(h), :] = Sh[h] * eg_ref[k * 8 + h:k * 8 + h + 1, :] + _dot(ksh[h], vnh[h], TN)

    ins, outs, shp = [], [], []
    for d in range(2):
        sp = _scan_specs(orders[d], sg)
        ins += [sp["st"]] * 4 + [sp["aqk"], sp["eg"]]
        outs += [sp["tok"], sp["state"], sp["st"]]
        shp += [_S((T, DN)), _S((n, DN, DH)), _S((n * HS, DH))]
    res = _call(body, "scan_fwd", (n // sg,), ins, outs, shp,
                scratch=[pltpu.VMEM((DN, DH), F32), pltpu.VMEM((DN, DH), F32)], rider=rider)(*parts[0][:6], *parts[1][:6])
    res, riding = (res, None) if rider is None else res
    out = tuple(res[:3]), tuple(res[3:])
    return out if rider is None else (out, riding)


def _scan_bwd(do, parts, tc):
    T = do.shape[0]
    n = T // CH
    sg, fwd_orders = _scan_plan(n, tc // CH)
    orders = [lambda s, f=f: f(n // sg - 1 - s) for f in fwd_orders]

    def body(*refs):
        dS_f, dS_b = refs[-2:]

        @pl.when(pl.program_id(0) == 0)
        def _():
            dS_f[...] = jnp.zeros_like(dS_f)
            dS_b[...] = jnp.zeros_like(dS_b)

        for g in range(sg):
            for d, dS in enumerate((dS_f, dS_b)):
                do_ref, w_ref, ks_ref, qd_ref, aqk_ref, eg_ref = refs[6 * d:6 * d + 6]
                dvn_ref, dss_ref = refs[12 + 2 * d:14 + 2 * d]
                k = sg - 1 - g if d == 0 else g
                rows = slice(k * HS, (k + 1) * HS)
                dss_ref[k] = dS[...]
                dSh = [dS[_hs(h), :] for h in range(NH)]
                wh, ksh, qdh = _heads(w_ref[rows, :]), _heads(ks_ref[rows, :]), _heads(qd_ref[rows, :])
                do_st = _stack(do_ref[k * CH:(k + 1) * CH, :])
                dvn = (_dot(aqk_ref[rows, :], do_st, TN)
                       + jnp.concatenate([_dot(ksh[h], dSh[h], NN) for h in range(NH)], axis=0))
                dvn_ref[rows, :] = dvn
                doh, dvnh = _heads(do_st), _heads(dvn)
                for h in range(NH):
                    dS[_hs(h), :] = (_dot(qdh[h], doh[h], TN) + dSh[h] * eg_ref[k * 8 + h:k * 8 + h + 1, :]
                                     - _dot(wh[h], dvnh[h], TN))

    ins, outs, shp, args = [], [], [], []
    for d in range(2):
        sp = _scan_specs(orders[d], sg)
        ins += [sp["tok"]] + [sp["st"]] * 3 + [sp["aqk"], sp["eg"]]
        outs += [sp["st"], sp["state"]]
        shp += [_S((n * HS, DH)), _S((n, DN, DH))]
        args += [do, *parts[d][1:6]]
    res = _call(body, "scan_bwd", (n // sg,), ins, outs, shp,
                scratch=[pltpu.VMEM((DN, DH), F32), pltpu.VMEM((DN, DH), F32)])(*args)
    return tuple(res[:2]), tuple(res[2:])


def _dn_chunks_bwd(qkv, pab, alr, dtr, do, fwd, bwd, rider=None):
    T = qkv.shape[0]
    n = T // CH
    G = _chunk_group(n)
    csum, csum_t = _scan_consts()

    def body(q_ref, p_ref, do_ref, cs_ref, cst_ref, al_ref, dt_ref, *refs):
        dq_refs, dp_refs, acc_ref = refs[14:16], refs[16:18], refs[18]

        @pl.when(pl.program_id(0) == 0)
        def _():
            acc_ref[...] = jnp.zeros_like(acc_ref)

        lane = _iota((CH, LANES), 1)
        inst = [(g, d) for g in range(G) for d in range(2)]
        cs, Ss, dS2s, dos, vns, dvns = [], [], [], [], [], []
        for g, d in inst:
            tok, rows = slice(g * CH, (g + 1) * CH), slice(g * HS, (g + 1) * HS)
            vn_ref, dvn_ref, ss_ref, dss_ref, tm_ref, u_ref, w_ref = refs[7 * d:7 * d + 7]
            cs.append(_dn_chunk_post(
                _dn_chunk_pre(q_ref[tok, :], p_ref[tok, :], al_ref[...], dt_ref[...], cs_ref[d], d), tm_ref[rows, :],
                uw=(u_ref[rows, :], w_ref[rows, :])))
            Ss.append([ss_ref[g, _hs(h), :] for h in range(NH)])
            dS2s.append([dss_ref[g, _hs(h), :] for h in range(NH)])
            dos.append(_stack(do_ref[tok, :]))
            vns.append(vn_ref[rows, :])
            dvns.append(dvn_ref[rows, :])
        for (g, d), c, (dq, dk, dv, dgc, dbeta) in zip(inst, cs, _dn_chunks_bwd_math(cs, Ss, dS2s, dos, vns, dvns)):
            tok = slice(g * CH, (g + 1) * CH)
            dgcm = jnp.zeros((CH, LANES), F32)
            dbm = jnp.zeros((CH, LANES), F32)
            for h, (a, b_, c_, e, f) in enumerate(zip(*map(_heads, (dq, dk, dv, dgc, dbeta)))):
                dq_refs[d][tok, _hs(h)] = a
                dq_refs[d][tok, _hs(NH + h)] = b_
                dq_refs[d][tok, _hs(2 * NH + h)] = c_
                dgcm = jnp.where(lane == d * NH + h, e, dgcm)
                dbm = jnp.where(lane == 8 + d * NH + h, f, dbm)
            dgm = _dot(cst_ref[d], dgcm, NN, hi=True)
            dsp = dgm * c["nexp"] * jax.nn.sigmoid(c["sp_in"])
            dp_refs[d][tok, :] = dsp + dbm * c["bm"] * (1.0 - c["bm"])
            acc_ref[0:1, :] += _colsum(dgm * c["gm"])
            acc_ref[1:2, :] += _colsum(dsp)

    st = pl.BlockSpec((G * HS, DH), lambda i: (i, 0))
    state = pl.BlockSpec((G, DN, DH), lambda i: (i, 0, 0))
    return _call(
        body, "dn_chunks_bwd", (n // G,),
        [_rows(G * CH, 3 * DN), _rows(G * CH, LANES), _rows(G * CH, DN), _full((2, CH, CH)), _full((2, CH, CH)),
         _full((1, LANES)), _full((1, LANES))]
        + [st, st, state, state, pl.BlockSpec((G * HS, HS), lambda i: (i, 0)), st, st] * 2,
        [_rows(G * CH, 3 * DN)] * 2 + [_rows(G * CH, LANES)] * 2 + [_full((8, LANES))],
        [_S((T, 3 * DN))] * 2 + [_S((T, LANES))] * 2 + [_S((8, LANES))], rider=rider)(
            qkv, pab, do, csum, csum_t, alr, dtr, *fwd, *bwd)


def _head_out(o, z, g):
    on = o * lax.rsqrt(jnp.mean(o * o, axis=-1, keepdims=True) + EPS) * g
    return on * _silu(z)


def _mix_branches(of_ref, ob_ref, z_ref, yp_ref, ys_ref, pg_ref, gdn_ref, wa_ref, wb_ref, wc_ref):
    ons, ya = [], None
    for h in range(NH):
        on = _head_out(of_ref[:, _hs(h)] + ob_ref[:, _hs(h)], z_ref[:, _hs(h)], gdn_ref[...])
        t = _dot(on, wa_ref[_hs(h), :], NN)
        ya = t if ya is None else ya + t
        ons.append(on)
    ys = [ya, _dot(yp_ref[...], wb_ref[...], NN), _dot(ys_ref[...], wc_ref[...], NN)]
    sg = [jax.nn.sigmoid(pg_ref[:, k * D:(k + 1) * D]) for k in range(3)]
    return ons, ys, sg


def _mix_fwd(X, of, ob, z, yp, ys, pg, mv, gdn, wa, wb, wc, wo, tc, tt):
    T = X.shape[0]

    def body(x_ref, of_ref, ob_ref, z_ref, yp_ref, ys_ref, pg_ref, mv_ref, gdn_ref, wa_ref, wb_ref, wc_ref, wo_ref,
             x1_ref):
        _, yb, sg = _mix_branches(of_ref, ob_ref, z_ref, yp_ref, ys_ref, pg_ref, gdn_ref, wa_ref, wb_ref, wc_ref)
        mix = _dot(sg[0] * yb[0] + sg[1] * yb[1] + sg[2] * yb[2], wo_ref[...], NN)
        _, gate = _stream_rows(mv_ref, pl.program_id(0), tt, tc, 2)
        x1_ref[...] = x_ref[...] + gate * mix

    return _call(
        body, "mix_fwd", (T // tt,),
        [_rows(tt, D), _rows(tt, DN), _rows(tt, DN), _rows(tt, DN), _rows(tt, PW), _rows(tt, PW), _rows(tt, 3 * D),
         _full((8, D)), _full((1, DH)), _full(wa.shape), _full(wb.shape), _full(wc.shape), _full(wo.shape)],
        _rows(tt, D), _S((T, D)))(X, of, ob, z, yp, ys, pg, mv, gdn, wa, wb, wc, wo)


def _mix_bwd(dx1, of, ob, z, yp, ys, pg, mv, gdn, wa, wb, wc, wo, tc, tt):
    T = dx1.shape[0]

    def body(dx_ref, of_ref, ob_ref, z_ref, yp_ref, ys_ref, pg_ref, mv_ref, gdn_ref, wa_ref, wb_ref, wc_ref, wo_ref,
             do_ref, dz_ref, dyp_ref, dys_ref, dpg_ref, dwa_ref, dwb_ref, dwc_ref, dwo_ref, dgdn_ref, dm_ref):
        i = pl.program_id(0)

        @pl.when(i == 0)
        def _():
            for r in (dwa_ref, dwb_ref, dwc_ref, dwo_ref, dgdn_ref, dm_ref):
                r[...] = jnp.zeros_like(r)

        ons, yb, sg = _mix_branches(of_ref, ob_ref, z_ref, yp_ref, ys_ref, pg_ref, gdn_ref, wa_ref, wb_ref, wc_ref)
        ymix = sg[0] * yb[0] + sg[1] * yb[1] + sg[2] * yb[2]
        isc, gate = _stream_rows(mv_ref, i, tt, tc, 2)
        dx = dx_ref[...]
        dmix = dx * gate
        _acc_stream(dm_ref, 2, isc, dx * _dot(ymix, wo_ref[...], NN))
        dwo_ref[...] += _dot(ymix, dmix, TN)
        dymix = _dot(dmix, wo_ref[...], NT)
        dyb = []
        for k in range(3):
            dyb.append(dymix * sg[k])
            dpg_ref[:, k * D:(k + 1) * D] = dymix * yb[k] * sg[k] * (1.0 - sg[k])
        dwb_ref[...] += _dot(yp_ref[...], dyb[1], TN)
        dwc_ref[...] += _dot(ys_ref[...], dyb[2], TN)
        dyp_ref[...] = _dot(dyb[1], wb_ref[...], NT)
        dys_ref[...] = _dot(dyb[2], wc_ref[...], NT)
        dg = jnp.zeros((1, DH), F32)
        for h in range(NH):
            dwa_ref[_hs(h), :] += _dot(ons[h], dyb[0], TN)
            don = _dot(dyb[0], wa_ref[_hs(h), :], NT)
            _, vjp = jax.vjp(_head_out, of_ref[:, _hs(h)] + ob_ref[:, _hs(h)], z_ref[:, _hs(h)], gdn_ref[...])
            do_h, dz_h, dg_h = vjp(don)
            do_ref[:, _hs(h)] = do_h
            dz_ref[:, _hs(h)] = dz_h
            dg = dg + dg_h
        dgdn_ref[...] += dg

    return _call(
        body, "mix_bwd", (T // tt,),
        [_rows(tt, D), _rows(tt, DN), _rows(tt, DN), _rows(tt, DN), _rows(tt, PW), _rows(tt, PW), _rows(tt, 3 * D),
         _full((8, D)), _full((1, DH)), _full(wa.shape), _full(wb.shape), _full(wc.shape), _full(wo.shape)],
        [_rows(tt, DN), _rows(tt, DN), _rows(tt, PW), _rows(tt, PW), _rows(tt, 3 * D),
         _full(wa.shape), _full(wb.shape), _full(wc.shape), _full(wo.shape), _full((1, DH)), _full((8, D))],
        [_S((T, DN)), _S((T, DN)), _S((T, PW)), _S((T, PW)), _S((T, 3 * D)),
         _S(wa.shape), _S(wb.shape), _S(wc.shape), _S(wo.shape), _S((1, DH)), _S((8, D))])(
            dx1, of, ob, z, yp, ys, pg, mv, gdn, wa, wb, wc, wo)


def _ffn_fwd(X1, mv, g, wgu, wd, tc, tt):
    T = X1.shape[0]

    def body(x_ref, mv_ref, g_ref, wgu_ref, wd_ref, x2_ref, ff_ref):
        i = pl.program_id(0)
        _, sh = _stream_rows(mv_ref, i, tt, tc, 0)
        _, sc = _stream_rows(mv_ref, i, tt, tc, 1)
        _, gate = _stream_rows(mv_ref, i, tt, tc, 2)
        x = x_ref[...]
        gu = _dot(_modulate(x, g_ref[...], sh, sc), wgu_ref[...], NN)
        ff = _dot(_silu(gu[:, :DFF]) * gu[:, DFF:], wd_ref[...], NN)
        ff_ref[...] = ff
        x2_ref[...] = x + gate * ff

    return _call(
        body, "ffn_fwd", (T // tt,),
        [_rows(tt, D), _full((8, D)), _full((1, D)), _full(wgu.shape), _full(wd.shape)],
        [_rows(tt, D)] * 2, [_S((T, D))] * 2)(X1, mv, g, wgu, wd)


def _ffn_bwd(X1, ff, dx2, mv, g, wgu, wd, tc, tt, rider=None):
    T = X1.shape[0]

    def body(x_ref, ff_ref, dx2_ref, mv_ref, g_ref, wgu_ref, wd_ref, dx1_ref, ht_ref, dgu_ref, actt_ref, dff_ref, dg_ref,
             dm_ref):
        i = pl.program_id(0)
        isc, sh = _stream_rows(mv_ref, i, tt, tc, 0)
        _, sc = _stream_rows(mv_ref, i, tt, tc, 1)
        _, gate = _stream_rows(mv_ref, i, tt, tc, 2)
        x, dx2_ = x_ref[...], dx2_ref[...]
        h, vjp = jax.vjp(_modulate, x, g_ref[...], sh, sc)
        ht_ref[...] = h.T.astype(BF16)
        gu = jnp.dot(h.astype(BF16), wgu_ref[...], preferred_element_type=F32)
        ga, up = gu[:, :DFF], gu[:, DFF:]
        sg = jax.nn.sigmoid(ga)
        actt_ref[...] = (ga * sg * up).T.astype(BF16)
        dff = dx2_ * gate
        dff_ref[...] = dff.astype(BF16)
        dact = _dot(dff, wd_ref[...], NT)
        dga = (dact * up * (sg * (1.0 + ga * (1.0 - sg)))).astype(BF16)
        dup = (dact * ga * sg).astype(BF16)
        dgu_ref[:, :DFF] = dga
        dgu_ref[:, DFF:] = dup
        dh = _dot(dga, wgu_ref[:, :DFF], NT) + _dot(dup, wgu_ref[:, DFF:], NT)
        dx, dg, dsh, dsc = vjp(dh)
        dx1_ref[...] = dx2_ + dx

        @pl.when(i == 0)
        def _():
            dg_ref[...] = jnp.zeros_like(dg_ref)
            dm_ref[...] = jnp.zeros_like(dm_ref)

        dg_ref[...] += dg
        _acc_stream(dm_ref, 0, isc, dsh)
        _acc_stream(dm_ref, 1, isc, dsc)
        _acc_stream(dm_ref, 2, isc, dx2_ * ff_ref[...])

    return _call(
        body, "ffn_bwd", (T // tt,),
        [_rows(tt, D), _rows(tt, D), _rows(tt, D), _full((8, D)), _full((1, D)), _full(wgu.shape), _full(wd.shape)],
        [_rows(tt, D), _cols(D, tt), _rows(tt, 2 * DFF), _cols(DFF, tt), _rows(tt, D), _full((1, D)), _full((8, D))],
        [_S((T, D)), _S((D, T), BF16), _S((T, 2 * DFF), BF16), _S((DFF, T), BF16), _S((T, D), BF16),
         _S((1, D)), _S((8, D))], rider=rider)(X1, ff, dx2, mv, g, wgu, wd)


def _rms(x, g):
    return x * lax.rsqrt(jnp.mean(x * x, axis=-1, keepdims=True) + EPS) * g


def _loss_head(X2, tgt, gf, tc):
    T = X2.shape[0]

    def body(x_ref, t_ref, g_ref, dx_ref, loss_ref, dg_ref):
        i = pl.program_id(0)

        @pl.when(i == 0)
        def _():
            dx_ref[...] = jnp.zeros_like(dx_ref)
            loss_ref[...] = jnp.zeros_like(loss_ref)
            dg_ref[...] = jnp.zeros_like(dg_ref)

        @pl.when(i > 0)
        def _():
            y, vjp = jax.vjp(_rms, x_ref[...], g_ref[...])
            err = y - t_ref[...]
            dx, dg = vjp(err * (1.0 / D))
            dx_ref[...] = dx
            dg_ref[...] += dg
            loss_ref[...] += (0.5 / D) * jnp.sum(jnp.sum(err * err, axis=1, keepdims=True), axis=0, keepdims=True)

    return _call(
        body, "loss_head", (T // tc,),
        [_rows(tc, D), pl.BlockSpec((tc, D), lambda i: (jnp.maximum(i - 1, 0), 0)), _full((1, D))],
        [_rows(tc, D), _full((8, LANES)), _full((1, D))],
        [_S((T, D)), _S((8, LANES)), _S((1, D))])(X2, tgt, gf)


def _block_diag(pw):
    g, n = pw.shape[0], pw.shape[1]
    out = jnp.zeros((g * n, g * n), pw.dtype)
    for k in range(g):
        out = lax.dynamic_update_slice(out, pw[k], (k * n, k * n))
    return out


IN_TRUE = tuple(IN_BOUNDS[k + 1] - IN_BOUNDS[k] for k in range(8))


def _overlaps(widths, cw):
    starts = np.cumsum([0] + list(widths))
    out = []
    for k in range(N_DEV):
        for i in range(len(widths)):
            a, b = max(k * cw, starts[i]), min((k + 1) * cw, starts[i + 1])
            if a < b:
                out.append((k, i, int(a - k * cw), int(a - starts[i]), int(b - a)))
    return out


def _shards_to_cols(gathered, l, widths, padded):
    nd, _, R, cw = gathered.shape
    tr = _shard_rows(R)

    def body(s_ref, *o_refs):
        for i, o_ref in enumerate(o_refs):
            if padded[i] > widths[i]:
                o_ref[...] = jnp.zeros_like(o_ref)
        for k, i, so, go, n in _overlaps(widths, cw):
            o_refs[i][:, go:go + n] = s_ref[k, 0, :, so:so + n].astype(BF16)

    return _call(
        body, "shards_to_cols", (R // tr,), [pl.BlockSpec((nd, 1, tr, cw), lambda i: (0, l, i, 0))],
        [_rows(tr, p) for p in padded], [_S((R, p), BF16) for p in padded])(gathered)


def _cols_to_shards(groups, widths, cw):
    R = groups[0].shape[0]
    tr = _shard_rows(R)

    def body(*refs):
        o_ref = refs[-1]
        for k, i, so, go, n in _overlaps(widths, cw):
            o_ref[k % 2, k // 2, 0, :, so:so + n] = refs[i][:, go:go + n]

    return _call(
        body, "cols_to_shards", (R // tr,), [_rows(tr, g.shape[1]) for g in groups],
        pl.BlockSpec((2, N_CHIP, 1, tr, cw), lambda i: (0, 0, 0, i, 0)), _S((2, N_CHIP, 1, R, cw)))(*groups)


def _mod_rows(mods_l, k0):
    rows = [mods_l[s, (k0 + k) * D:(k0 + k + 1) * D] for s in (0, 1) for k in range(3)]
    return jnp.stack(rows + [jnp.zeros((D,), F32)] * 2)


def _lane_row(v8):
    return jnp.pad(v8.reshape(1, 8), ((0, 0), (0, LANES - 8)))


LAYERED = ("w_in", "w_br_a", "w_br_b", "w_br_c", "w_o", "w_gu", "w_down")
LATE = ("w_br_a", "w_br_b", "w_br_c", "w_o", "w_gu", "w_down")


def _device_step(x, c, ctx, tgt, wts, tt, comm=None):
    tc = ctx.shape[0]
    X = jnp.concatenate([ctx, x], axis=0)
    if comm is None:
        row = 0
        cc = jnp.concatenate([c, jnp.zeros((CTX_ROW - 1, D), F32), wts["c_ctx"][None, :],
                              jnp.zeros((CC_ROWS - CTX_ROW - 1, D), F32)], axis=0)
        w_ada = wts["w_ada"].astype(BF16)
        mods16 = _mod_fwd(cc, w_ada, wts["b_ada"].reshape(NL, 1, 6 * D))
    else:
        row, mods16 = comm.adaln_fwd(c, wts["c_ctx"])
    mods = jnp.stack([mods16[:, CTX_ROW], lax.dynamic_index_in_dim(mods16, row, 1, keepdims=False)], axis=1)

    saved = []
    for l in range(NL):
        ws = _shards_to_cols(wts["w_in"][l], 0, IN_TRUE, IN_WIDTHS)
        mv1, mv2 = _mod_rows(mods[l], 0), _mod_rows(mods[l], 3)
        g1, g2 = wts["norm1_g"][l][None, :], wts["norm2_g"][l][None, :]
        cw, scw = wts["dn_conv_w"][l], wts["sc_conv_w"][l]
        alr, dtr = _lane_row(wts["dn_a_log"][l]), _lane_row(wts["dn_dt_bias"][l])
        gdn = wts["dn_norm_g"][l][None, :]
        pwbd, ps = _block_diag(wts["pool_w"][l]), wts["pool_scale"][l][None, :]
        hb, pq, pz, pab, pp, sx, sb, sc_, pg = _inproj_fwd(X, mv1, g1, ws, tc, tt)
        qkv = _dnprep_fwd(pq, cw, tc, tt)
        if comm is not None and l == 0:
            parts, riding = _dn_chunks_fwd(qkv, pab, alr, dtr, rider=comm.late_weights_chips())
            ((of, ssf, vnf), (ob, ssb, vnb)), riding = _scan_fwd(parts, X.shape[0], tc,
                                                                 rider=comm.late_weights_pair(riding))
            late, w_in_1 = comm.late_weights(riding)
            wts = dict(wts, **late, w_in=[wts["w_in"][0], w_in_1])
        else:
            parts = _dn_chunks_fwd(qkv, pab, alr, dtr)
            (of, ssf, vnf), (ob, ssb, vnb) = _scan_fwd(parts, X.shape[0], tc)
        wbr = [_shards_to_cols(wts[k], l, (2 * DFF,), (2 * DFF,))[0] if k == "w_gu" else wts[k][l].astype(BF16)
               for k in LATE]
        yp = _pool_fwd(pp, pwbd, ps, tc)
        ys = _sc_fwd(sx, sb, sc_, scw, tc, tt)
        X1 = _mix_fwd(X, of, ob, pz, yp, ys, pg, mv1, gdn, *wbr[:4], tc, tt)
        X2, ff = _ffn_fwd(X1, mv2, g2, wbr[4], wbr[5], tc, tt)
        saved.append(dict(X=X, X1=X1, ff=ff, ws=ws, wbr=wbr, mv1=mv1, mv2=mv2, g1=g1, g2=g2, cw=cw, scw=scw, alr=alr, dtr=dtr,
                          gdn=gdn, pwbd=pwbd, ps=ps, hb=hb, pq=pq, pz=pz, pab=pab, pp=pp, sx=sx, sb=sb, sc=sc_, pg=pg,
                          qkv=qkv, of=of, ob=ob, ssf=ssf, ssb=ssb, vnf=vnf, vnb=vnb, parts=parts, yp=yp, ys=ys))
        X = X2

    dX, loss, dgf = _loss_head(X, tgt, wts["final_norm_g"][None, :], tc)

    gl = {k: [None] * NL for k in ("w_in", "norm1_g", "norm2_g", "dn_conv_w", "dn_a_log", "dn_dt_bias", "dn_norm_g",
                                   "pool_w", "pool_scale", "sc_conv_w", "w_br_a", "w_br_b", "w_br_c", "w_o", "w_gu",
                                   "w_down")}
    dmods = [None] * NL
    early = None
    for l in reversed(range(NL)):
        s = saved[l]
        hide = comm is not None and l == 0
        res = _ffn_bwd(s["X1"], s["ff"], dX, s["mv2"], s["g2"], s["wbr"][4], s["wbr"][5], tc, tt,
                       rider=comm.grad_pair_rider([gl[k][1] for k in LAYERED]) if hide else None)
        if hide:
            res, got = res
            chip_rider = comm.grad_chip_rider(got)
        dx1, h2, dgu, act, dff, dg2, dm2 = res
        gl["w_gu"][l] = _cols_to_shards([_dw(h2, dgu, tt)], (2 * DFF,), 2 * DFF // N_DEV)
        gl["w_down"][l] = _dw(act, dff, tt)
        do, dz, dyp, dys, dpg, dwa, dwb, dwc, dwo, dgdn, dmg = _mix_bwd(
            dx1, s["of"], s["ob"], s["pz"], s["yp"], s["ys"], s["pg"], s["mv1"], s["gdn"], *s["wbr"][:4], tc, tt)
        dpp, dpw, dps = _pool_bwd(s["pp"], s["pwbd"], s["ps"], dyp, tc)
        dsx, dsb, dsc, dscw = _sc_bwd(s["sx"], s["sb"], s["sc"], s["scw"], dys, tc, tt)
        (dvnf, dssf), (dvnb, dssb) = _scan_bwd(do, s["parts"], tc)
        res = _dn_chunks_bwd(s["qkv"], s["pab"], s["alr"], s["dtr"], do,
                             (s["vnf"], dvnf, s["ssf"], dssf, s["parts"][0][6], *s["parts"][0][:2]),
                             (s["vnb"], dvnb, s["ssb"], dssb, s["parts"][1][6], *s["parts"][1][:2]),
                             rider=chip_rider if hide else None)
        if hide:
            res, early = res
            early = comm.grad_chip_done(early)
        dqf, dqb, dpf, dpb, gacc = res
        dy = _dnprep_bwd_act(s["pq"], s["cw"], dqf, dqb, tc, tt)
        dpq, dcw = _conv_bwd(dy, s["pq"], s["cw"], tc, tt)
        dps_ = [dpq, dz, dpf, dpb, dpp, dsx, dsb, dsc, dpg]
        dp_w = [0, 1, 2, 2, 3, 4, 5, 6, 7]
        dX, dg1, dm1 = _inproj_bwd(s["X"], s["mv1"], s["g1"], s["ws"], dps_, dp_w, dx1, tc, tt)
        dws = [_dw(s["hb"], dp, tt) for dp in (dpq, dz, dpf + dpb, dpp, dsx, dsb, dsc, dpg)]
        gl["w_in"][l] = _cols_to_shards(dws, IN_TRUE, IN_BOUNDS[-1] // N_DEV)
        gl["norm1_g"][l], gl["norm2_g"][l] = dg1[0], dg2[0]
        gl["dn_conv_w"][l], gl["sc_conv_w"][l] = dcw, dscw
        gl["dn_a_log"][l], gl["dn_dt_bias"][l] = gacc[0, :8].reshape(2, NH), gacc[1, :8].reshape(2, NH)
        gl["dn_norm_g"][l] = dgdn[0]
        gl["pool_w"][l] = jnp.stack([dpw[k * GW:(k + 1) * GW, k * GW:(k + 1) * GW] for k in range(4)])
        gl["pool_scale"][l] = dps[0]
        gl["w_br_a"][l], gl["w_br_b"][l], gl["w_br_c"][l], gl["w_o"][l] = dwa, dwb, dwc, dwo
        dm = dm1 + dmg
        cat = lambda r: jnp.concatenate([dm[r], dm[r + 1], dm[r + 2], dm2[r], dm2[r + 1], dm2[r + 2]])
        dmods[l] = jnp.stack([cat(0), cat(3)])

    dmods = jnp.stack(dmods)
    grads = {k: (v if k in LAYERED else jnp.stack(v)) for k, v in gl.items()}
    if comm is None:
        dm16 = jnp.zeros((NL, CC_ROWS, 6 * D), F32).at[:, CTX_ROW].set(dmods[:, 0]).at[:, row].set(dmods[:, 1])
        dwada, dcc = _mod_bwd(cc, w_ada, dm16)
        grads.update(w_ada=dwada, b_ada=dmods[:, 0] + dmods[:, 1], c_ctx=dcc[CTX_ROW])
    else:
        grads.update(comm.adaln_bwd(dmods))
    grads.update(final_norm_g=dgf[0])
    return loss, dX[tc:], grads, early


def _me():
    return lax.axis_index("x"), lax.axis_index("y"), lax.axis_index("c")


def _dev_index(p):
    return 4 * p[0] + 2 * p[1] + p[2]


def _allgather(parts):
    n = len(parts)

    def body(*refs):
        ins, outs = refs[:n], refs[n:2 * n]
        send_sems, recv_sems = refs[2 * n:]
        x, y, c = _me()
        me, sibling = (x, y, c), (x, y, 1 - c)
        chips = [(1 - x, y), (x, 1 - y), (1 - x, 1 - y)]

        def copy(a, k, block, to, src=None):
            dst = outs[a].at[_dev_index(block)]
            return pltpu.make_async_remote_copy(
                src_ref=dst if src is None else src, dst_ref=dst, send_sem=send_sems.at[a, k], recv_sem=recv_sems.at[a, k],
                device_id=to, device_id_type=MESH_ID)

        first, passed = [], []
        for a in range(n):
            first.append(copy(a, 0, me, sibling, src=ins[a]))
            first += [copy(a, 1 + j, me, (*chip, c), src=ins[a]) for j, chip in enumerate(chips)]
        for cp in first:
            cp.start()
        for a in range(n):
            for j, chip in enumerate(chips):
                copy(a, 1 + j, (*chip, c), me).wait_recv()
                passed.append(copy(a, 4 + j, (*chip, c), sibling))
                passed[-1].start()
        for a in range(n):
            copy(a, 0, sibling, me).wait_recv()
            for j, chip in enumerate(chips):
                copy(a, 4 + j, (*chip, 1 - c), me).wait_recv()
        for cp in first + passed:
            cp.wait_send()

    outs = pl.pallas_call(
        body, name="allgather", in_specs=[HBM_SPEC] * n, out_specs=[HBM_SPEC] * n,
        out_shape=[_S((N_DEV,) + p.shape, p.dtype) for p in parts],
        scratch_shapes=[pltpu.SemaphoreType.DMA((n, 7)), pltpu.SemaphoreType.DMA((n, 7))],
    )(*parts)
    return [_with_own(o, p, _dev_index(_me())) for o, p in zip(outs, parts)]


def _with_own(gathered, own, index):
    return lax.dynamic_update_index_in_dim(gathered, own, index, 0)


def _broadcast_small(small, name="small_exchange"):
    def body(in_ref, out_ref, send_sems, recv_sems, local_sem):
        x, y, c = _me()
        my = _dev_index((x, y, c))
        mine = pltpu.make_async_copy(in_ref, out_ref.at[my], local_sem)
        mine.start()
        remote = []
        for k in range(1, N_DEV):
            cp = pltpu.make_async_remote_copy(
                src_ref=in_ref, dst_ref=out_ref.at[my], send_sem=send_sems.at[k - 1], recv_sem=recv_sems.at[k - 1],
                device_id=(x ^ (k >> 2), y ^ ((k >> 1) & 1), c ^ (k & 1)), device_id_type=MESH_ID)
            cp.start()
            remote.append(cp)
        for cp in remote:
            cp.wait_recv()
        for cp in remote:
            cp.wait_send()
        mine.wait()

    return pl.pallas_call(
        body, name=name, in_specs=[HBM_SPEC], out_specs=HBM_SPEC,
        out_shape=_S((N_DEV,) + small.shape, small.dtype),
        scratch_shapes=[pltpu.SemaphoreType.DMA((7,)), pltpu.SemaphoreType.DMA((7,)), pltpu.SemaphoreType.DMA],
    )(small)


def _run_rider(rider, name):
    ni, no = len(rider.ins), len(rider.out_shapes)

    def body(*refs):
        riding = (refs[:ni], refs[ni:ni + no], refs[ni + no:])
        rider.start(*riding)
        rider.wait(*riding)

    return list(pl.pallas_call(
        body, name=name, in_specs=[HBM_SPEC] * ni, out_specs=[HBM_SPEC] * no, out_shape=rider.out_shapes,
        scratch_shapes=rider.sems)(*rider.ins))


def _chip_peers(x, y):
    return [(k - 1, (x ^ (k >> 1), y ^ (k & 1))) for k in range(1, N_CHIP)]


def _pair_exchange(g2s):
    n = len(g2s)

    def copies(ins, outs, sems):
        x, y, c = _me()
        return [pltpu.make_async_remote_copy(
            src_ref=ins[a].at[1 - c, j], dst_ref=outs[a].at[j], send_sem=sems[0].at[a, j], recv_sem=sems[1].at[a, j],
            device_id=(x, y, 1 - c), device_id_type=MESH_ID) for a in range(n) for j in range(N_CHIP)], []

    return _Rider(g2s, [_S(g.shape[1:], g.dtype) for g in g2s],
                  [pltpu.SemaphoreType.DMA((n, N_CHIP)), pltpu.SemaphoreType.DMA((n, N_CHIP))], copies)


def _my_chip():
    x, y, _ = _me()
    return 2 * x + y


def _chip_exchange(s4s):
    n = len(s4s)

    def copies(ins, outs, sems):
        x, y, c = _me()
        my = 2 * x + y
        return [pltpu.make_async_remote_copy(
            src_ref=ins[a].at[2 * px + py], dst_ref=outs[a].at[my], send_sem=sems[0].at[a, k], recv_sem=sems[1].at[a, k],
            device_id=(px, py, c), device_id_type=MESH_ID) for k, (px, py) in _chip_peers(x, y) for a in range(n)], []

    return _Rider(s4s, [_S(s.shape, s.dtype) for s in s4s],
                  [pltpu.SemaphoreType.DMA((n, N_CHIP - 1)), pltpu.SemaphoreType.DMA((n, N_CHIP - 1))], copies)


def _chip_exchange_done(s4s, recvs):
    my = _my_chip()
    return [_with_own(r, lax.dynamic_index_in_dim(s, my, 0, keepdims=False), my) for s, r in zip(s4s, recvs)]


def _chip_gather(arrs):
    n = len(arrs)

    def copies(ins, outs, sems):
        x, y, c = _me()
        return [pltpu.make_async_remote_copy(
            src_ref=ins[a], dst_ref=outs[a].at[2 * x + y], send_sem=sems[0].at[a, k], recv_sem=sems[1].at[a, k],
            device_id=(px, py, c), device_id_type=MESH_ID) for k, (px, py) in _chip_peers(x, y) for a in range(n)], []

    return _Rider(arrs, [_S((N_CHIP,) + a.shape, a.dtype) for a in arrs],
                  [pltpu.SemaphoreType.DMA((n, N_CHIP - 1)), pltpu.SemaphoreType.DMA((n, N_CHIP - 1))], copies)


def _pair_gather(chips):
    n = len(chips)

    def copies(ins, outs, sems):
        x, y, c = _me()
        return [pltpu.make_async_remote_copy(
            src_ref=ins[a].at[j], dst_ref=outs[a].at[j], send_sem=sems[0].at[a, j], recv_sem=sems[1].at[a, j],
            device_id=(x, y, 1 - c), device_id_type=MESH_ID) for a in range(n) for j in range(N_CHIP)], []

    return _Rider(chips, [_S(a.shape, a.dtype) for a in chips],
                  [pltpu.SemaphoreType.DMA((n, N_CHIP)), pltpu.SemaphoreType.DMA((n, N_CHIP))], copies)


def _shard_rows(r):
    return 256 if r % 256 == 0 else r


def _pair_sum(g2, got):
    _, nc, L, R, C = g2.shape
    tr = _shard_rows(R)

    def body(a_ref, b_ref, o_ref):
        o_ref[...] = (a_ref[0] + b_ref[...]).astype(BF16)

    blk = pl.BlockSpec((1, 1, tr, C), lambda j, l, i: (j, l, i, 0))
    return _call(
        body, "pair_sum", (nc, L, R // tr),
        [pl.BlockSpec((1, 1, 1, tr, C), lambda j, l, i: (lax.axis_index("c"), j, l, i, 0)), blk], blk,
        _S(got.shape, BF16))(g2, got)


def _adam(w, g, m, v):
    m2 = ADAM_B1 * m + (1.0 - ADAM_B1) * g
    v2 = ADAM_B2 * v + (1.0 - ADAM_B2) * (g * g)
    m_hat = m2 / (1.0 - ADAM_B1 ** ADAM_STEP)
    v_hat = v2 / (1.0 - ADAM_B2 ** ADAM_STEP)
    return -ADAM_LR * (m_hat / (jnp.sqrt(v_hat) + ADAM_EPS) + ADAM_WD * w), m2, v2


def _sum_adam(recvs, w, m, v):
    L, R, C = w.shape
    tr = _shard_rows(R)
    nr = len(recvs)

    def body(*refs):
        w_ref, m_ref, v_ref, g_ref, d_ref, m2_ref, v2_ref = refs[nr:]
        g = None
        for li, r_ref in enumerate(refs[:nr]):
            s = r_ref[0, 0].astype(F32)
            for j in range(1, N_CHIP):
                s = s + r_ref[j, 0].astype(F32)
            g = s if g is None else jnp.where(pl.program_id(0) == li, s, g)
        g_ref[0] = g
        d_ref[0], m2_ref[0], v2_ref[0] = _adam(w_ref[0], g, m_ref[0], v_ref[0])

    blk = pl.BlockSpec((1, tr, C), lambda l, i: (l, i, 0))
    rspec = pl.BlockSpec((N_CHIP, 1, tr, C), (lambda l, i: (0, l, i, 0)) if nr == 1 else (lambda l, i: (0, 0, i, 0)))
    return _call(body, "sum_adam", (L, R // tr), [rspec] * nr + [blk, blk, blk], [blk] * 4, [_S(w.shape)] * 4)(
        *recvs, w, m, v)


def _adam_big(w, g, m, v):
    L, R, C = w.shape
    tr = _shard_rows(R)

    def body(w_ref, g_ref, m_ref, v_ref, d_ref, m2_ref, v2_ref):
        d_ref[0], m2_ref[0], v2_ref[0] = _adam(w_ref[0], g_ref[0], m_ref[0], v_ref[0])

    blk = pl.BlockSpec((1, tr, C), lambda l, i: (l, i, 0))
    return _call(body, "adam_big", (L, R // tr), [blk] * 4, [blk] * 3, [_S(w.shape)] * 3)(w, g, m, v)


def _sum_small(recv):
    def body(r_ref, o_ref):
        g = r_ref[0]
        for k in range(1, recv.shape[0]):
            g = g + r_ref[k]
        o_ref[...] = g

    return pl.pallas_call(body, name="sum_small", out_shape=_S(recv.shape[1:]))(recv)


def _adam_small(w, g, m, v):
    def body(w_ref, g_ref, m_ref, v_ref, d_ref, m2_ref, v2_ref):
        d_ref[...], m2_ref[...], v2_ref[...] = _adam(w_ref[...], g_ref[...], m_ref[...], v_ref[...])

    return pl.pallas_call(body, name="adam_small", out_shape=[_S(w.shape)] * 3)(w, g, m, v)


def _pack(arrs, dtype, row_mult):
    parts, offs, r = [], [], 0
    for a in arrs:
        nr = -(-a.size // LANES)
        parts.append(jnp.pad(a.reshape(-1).astype(dtype), (0, nr * LANES - a.size)))
        offs.append(r)
        r += nr
    pad = (-r) % row_mult
    if pad:
        parts.append(jnp.zeros((pad * LANES,), dtype))
    return jnp.concatenate(parts).reshape(r + pad, LANES), offs


def _unpack(packed, offs, shapes, lead=()):
    out = []
    for off, shp in zip(offs, shapes):
        size = int(np.prod(shp))
        nr = -(-size // LANES)
        flat = packed[..., off:off + nr, :].reshape(lead + (nr * LANES,))
        out.append(flat[..., :size].reshape(lead + tuple(shp)))
    return out


BIG = (("w_ada", 2), ("w_in", 2), ("w_br_a", 2), ("w_br_b", 2), ("w_br_c", 2), ("w_o", 1), ("w_gu", 2), ("w_down", 1))
CONV = ("dn_conv_w", "sc_conv_w")
REPL = ("c_ctx", "b_ada", "norm1_g", "norm2_g", "dn_a_log", "dn_dt_bias", "dn_norm_g", "pool_w", "pool_scale",
        "final_norm_g")
WEIGHTS = ("c_ctx", "w_ada", "b_ada", "norm1_g", "norm2_g", "w_in", "dn_conv_w", "dn_a_log", "dn_dt_bias", "dn_norm_g",
           "pool_w", "pool_scale", "sc_conv_w", "w_br_a", "w_br_b", "w_br_c", "w_o", "w_gu", "w_down", "final_norm_g")
TOKEN_TILE = 256


def _join(blocks, axis):
    nd, nl, r, c = blocks.shape
    if axis == 2:
        return blocks.transpose(1, 2, 0, 3).reshape(nl, r, nd * c)
    return blocks.transpose(1, 0, 2, 3).reshape(nl, nd * r, c)


def _split(full, axis):
    nl, r, c = full.shape
    if axis == 2:
        return full.reshape(nl, r, N_CHIP, 2, c // N_DEV).transpose(3, 2, 0, 1, 4)
    return full.reshape(nl, N_CHIP, 2, r // N_DEV, c).transpose(2, 1, 0, 3, 4)


PRESPLIT = ("w_in", "w_gu")


def _presplit(layer_grads):
    return [g if k in PRESPLIT else _split(g[None], dict(BIG)[k]) for k, g in zip(LAYERED, layer_grads)]


class _Comm:
    def __init__(self, late_shards, w_ada, b_ada):
        self.packed = [k for k in LATE if k not in PRESPLIT]
        self.shapes = [late_shards[k].shape for k in self.packed]
        pack, self.offs = _pack([late_shards[k] for k in self.packed], BF16, BF16_ROWS)
        self.late = [pack, late_shards["w_gu"].astype(BF16), late_shards["w_in"][1:].astype(BF16)]
        self.w_ada, self.b_ada = w_ada.astype(BF16), b_ada
        self.g2s = None

    def adaln_fwd(self, c, c_ctx):
        my = _dev_index(_me())
        ncol = self.w_ada.shape[2]
        c_all = _broadcast_small(c.reshape(8, LANES), "c_exchange").reshape(N_DEV, D)
        self.cc = jnp.concatenate([c_all, c_ctx[None, :], jnp.zeros((CC_ROWS - N_DEV - 1, D), F32)], axis=0)
        b_cols = lax.dynamic_slice_in_dim(self.b_ada, my * ncol, ncol, axis=1).reshape(NL, 1, ncol)
        cols = _mod_fwd(self.cc, self.w_ada, b_cols)
        got = _broadcast_small(cols.reshape(-1, LANES), "mods_exchange").reshape(N_DEV, NL, CC_ROWS, ncol)
        return my, got.transpose(1, 2, 0, 3).reshape(NL, CC_ROWS, N_DEV * ncol)

    def adaln_bwd(self, dmods):
        my = _dev_index(_me())
        ncol = self.w_ada.shape[2]
        got = _broadcast_small(dmods.reshape(-1, LANES), "dmods_exchange")
        rows = got.reshape(N_DEV, NL, 2, 6 * D)
        ctx_sum = _sum_small(rows[:, :, 0].reshape(N_DEV, -1, LANES)).reshape(NL, 1, 6 * D)
        db = _sum_small(rows.transpose(0, 2, 1, 3).reshape(2 * N_DEV, -1, LANES)).reshape(NL, 6 * D)
        dm = jnp.concatenate([rows[:, :, 1].transpose(1, 0, 2), ctx_sum,
                              jnp.zeros((NL, CC_ROWS - N_DEV - 1, 6 * D), F32)], axis=1)
        dw, dcc = _mod_bwd(self.cc, self.w_ada, lax.dynamic_slice_in_dim(dm, my * ncol, ncol, axis=2))
        return dict(w_ada=dw, b_ada=db, c_ctx=dcc[CTX_ROW])

    def late_weights_chips(self):
        return _chip_gather(self.late)

    def late_weights_pair(self, riding):
        self.chips = [_with_own(r, a, _my_chip()) for r, a in zip(riding, self.late)]
        return _pair_gather(self.chips)

    def late_weights(self, riding):
        on_south = lax.axis_index("c") == 0
        by_dev = []
        for mine, other in zip(self.chips, riding):
            both = jnp.stack([jnp.where(on_south, mine, other), jnp.where(on_south, other, mine)], axis=1)
            by_dev.append(both.reshape((N_DEV,) + mine.shape[1:]))
        shards = _unpack(by_dev[0], self.offs, self.shapes, (N_DEV,))
        return dict({k: _join(blocks, dict(BIG)[k]) for k, blocks in zip(self.packed, shards)}, w_gu=by_dev[1]), by_dev[2]

    def grad_pair_rider(self, layer_grads):
        self.g2s = _presplit(layer_grads)
        return _pair_exchange(self.g2s)

    def grad_chip_rider(self, got):
        self.sums = [_pair_sum(g2, gt) for g2, gt in zip(self.g2s, got)]
        return _chip_exchange(self.sums)

    def grad_chip_done(self, riding):
        return _chip_exchange_done(self.sums, riding)


def kernel(x, c, ctx, c_ctx, w_ada, b_ada, norm1_g, norm2_g, w_in, dn_conv_w, dn_a_log, dn_dt_bias, dn_norm_g, pool_w, pool_scale, sc_conv_w, w_br_a, w_br_b, w_br_c, w_o, w_gu, w_down, final_norm_g, loss_target, m_c_ctx, m_w_ada, m_b_ada, m_norm1_g, m_norm2_g, m_w_in, m_dn_conv_w, m_dn_a_log, m_dn_dt_bias, m_dn_norm_g, m_pool_w, m_pool_scale, m_sc_conv_w, m_w_br_a, m_w_br_b, m_w_br_c, m_w_o, m_w_gu, m_w_down, m_final_norm_g, v_c_ctx, v_w_ada, v_b_ada, v_norm1_g, v_norm2_g, v_w_in, v_dn_conv_w, v_dn_a_log, v_dn_dt_bias, v_dn_norm_g, v_pool_w, v_pool_scale, v_sc_conv_w, v_w_br_a, v_w_br_b, v_w_br_c, v_w_o, v_w_gu, v_w_down, v_final_norm_g):
    loc = dict(c_ctx=c_ctx, w_ada=w_ada, b_ada=b_ada, norm1_g=norm1_g, norm2_g=norm2_g, w_in=w_in, dn_conv_w=dn_conv_w,
               dn_a_log=dn_a_log, dn_dt_bias=dn_dt_bias, dn_norm_g=dn_norm_g, pool_w=pool_w, pool_scale=pool_scale,
               sc_conv_w=sc_conv_w, w_br_a=w_br_a, w_br_b=w_br_b, w_br_c=w_br_c, w_o=w_o, w_gu=w_gu, w_down=w_down,
               final_norm_g=final_norm_g)
    mom_m = dict(c_ctx=m_c_ctx, w_ada=m_w_ada, b_ada=m_b_ada, norm1_g=m_norm1_g, norm2_g=m_norm2_g, w_in=m_w_in,
                 dn_conv_w=m_dn_conv_w, dn_a_log=m_dn_a_log, dn_dt_bias=m_dn_dt_bias, dn_norm_g=m_dn_norm_g,
                 pool_w=m_pool_w, pool_scale=m_pool_scale, sc_conv_w=m_sc_conv_w, w_br_a=m_w_br_a, w_br_b=m_w_br_b,
                 w_br_c=m_w_br_c, w_o=m_w_o, w_gu=m_w_gu, w_down=m_w_down, final_norm_g=m_final_norm_g)
    mom_v = dict(c_ctx=v_c_ctx, w_ada=v_w_ada, b_ada=v_b_ada, norm1_g=v_norm1_g, norm2_g=v_norm2_g, w_in=v_w_in,
                 dn_conv_w=v_dn_conv_w, dn_a_log=v_dn_a_log, dn_dt_bias=v_dn_dt_bias, dn_norm_g=v_dn_norm_g,
                 pool_w=v_pool_w, pool_scale=v_pool_scale, sc_conv_w=v_sc_conv_w, w_br_a=v_w_br_a, w_br_b=v_w_br_b,
                 w_br_c=v_w_br_c, w_o=v_w_o, w_gu=v_w_gu, w_down=v_w_down, final_norm_g=v_final_norm_g)
    my = _dev_index(_me())

    conv_pack, conv_offs = _pack([loc[k] for k in CONV], F32, 8)
    w_in0_all, conv_all = _allgather([w_in[:1].astype(BF16), conv_pack])
    full = dict({k: loc[k] for k in REPL}, w_in=[w_in0_all])
    for k, blocks in zip(CONV, _unpack(conv_all, conv_offs, [loc[k].shape for k in CONV], (N_DEV,))):
        full[k] = _join(blocks, 2)

    loss8, grad_x, g, recv_l1 = _device_step(x[0], c, ctx[0], loss_target[0], full, TOKEN_TILE,
                                             comm=_Comm({k: loc[k] for k in LATE + ("w_in",)}, w_ada, b_ada))

    tail = _presplit([g[k][0] for k in LAYERED])
    got = _run_rider(_pair_exchange(tail), "pair_exchange")
    sums = [_pair_sum(a, b) for a, b in zip(tail, got)]
    recv_tail = _chip_exchange_done(sums, _run_rider(_chip_exchange(sums), "chip_exchange"))

    small_names = REPL + CONV
    summed = [k for k in small_names if k != "b_ada"]
    small_pack, small_offs = _pack([g[k] for k in summed] + [loss8[0:1, 0:1]], F32, 8)
    small_sum = _sum_small(_broadcast_small(small_pack))
    sums = _unpack(small_sum, small_offs, [g[k].shape for k in summed] + [(1, 1)])
    grads = dict(zip(summed, sums[:-1]), b_ada=g["b_ada"], w_ada=g["w_ada"])
    loss = sums[-1][0, 0]
    for k in CONV:
        w = loc[k].shape[2]
        grads[k] = lax.dynamic_slice_in_dim(grads[k], my * w, w, axis=2)

    delta, new_m, new_v = {}, {}, {}
    for i, k in enumerate(LAYERED):
        grads[k], delta[k], new_m[k], new_v[k] = _sum_adam([recv_tail[i], recv_l1[i]], loc[k], mom_m[k], mom_v[k])
    delta["w_ada"], new_m["w_ada"], new_v["w_ada"] = _adam_big(w_ada, g["w_ada"], m_w_ada, v_w_ada)
    packs = [_pack([src[k] for k in small_names], F32, 8)[0] for src in (loc, grads, mom_m, mom_v)]
    _, offs = _pack([loc[k] for k in small_names], F32, 8)
    shapes = [loc[k].shape for k in small_names]
    for dst, packed in zip((delta, new_m, new_v), _adam_small(*packs)):
        dst.update(zip(small_names, _unpack(packed, offs, shapes)))

    return (loss, grad_x[None], *[grads[k] for k in WEIGHTS], *[delta[k] for k in WEIGHTS],
            *[new_m[k] for k in WEIGHTS], *[new_v[k] for k in WEIGHTS])
```

```python
import numpy as np
import jax
import jax.numpy as jnp
from jax import lax
from jax.experimental import pallas as pl
from jax.experimental.pallas import tpu as pltpu

F32 = jnp.float32
BF16 = jnp.bfloat16
HI = lax.Precision.HIGHEST

D = 1024
NL = 2
NH = 4
DH = 128
DN = NH * DH
CH = 64
GW = 64
PW = 256
DFF = 2816
EPS = 1e-6
N_DEV = 8
N_CHIP = 4
MESH_ID = pl.DeviceIdType.MESH
HBM_SPEC = pl.BlockSpec(memory_space=pltpu.HBM)
LANES = 128
BF16_ROWS = 16
VMEM_MB = 56
DW_ACC_MB = 12

ADAM_LR, ADAM_B1, ADAM_B2, ADAM_EPS, ADAM_WD, ADAM_STEP = 0.001, 0.9, 0.999, 1e-08, 0.01, 10

IN_BOUNDS = (0, 1536, 2048, 2064, 2320, 2576, 2832, 3088, 6160)
IN_WIDTHS = (1536, 512, 128, 256, 256, 256, 256, 3072)
POOL_WIN = ((1, 0), (2, 1), (4, 3), (8, 7))

NN = ((1,), (0,))
NT = ((1,), (1,))
TN = ((0,), (0,))


def _dot(a, b, dims, hi=False):
    if hi:
        prec = lax.Precision.HIGH if hi == "x3" else HI
        return lax.dot_general(a, b, (dims, ((), ())), precision=prec, preferred_element_type=F32)
    return lax.dot_general(a.astype(BF16), b.astype(BF16), (dims, ((), ())), preferred_element_type=F32)


def _S(shape, dtype=F32):
    return jax.ShapeDtypeStruct(tuple(shape), dtype)


def _full(shape):
    nd = len(shape)
    return pl.BlockSpec(tuple(shape), lambda *_: (0,) * nd)


def _rows(tt, w):
    return pl.BlockSpec((tt, w), lambda i: (i, 0))


class _Rider:
    def __init__(self, ins, out_shapes, sems, copies):
        self.ins, self.out_shapes, self.sems, self.copies = list(ins), list(out_shapes), list(sems), copies

    def start(self, ins, outs, sems):
        remote, local = self.copies(ins, outs, sems)
        for cp in local + remote:
            cp.start()

    def wait(self, ins, outs, sems):
        remote, local = self.copies(ins, outs, sems)
        for cp in remote:
            cp.wait_recv()
        for cp in remote:
            cp.wait_send()
        for cp in local:
            cp.wait()


def _call(body, name, grid, in_specs, out_specs, out_shape, scratch=(), rider=None):
    params = pltpu.CompilerParams(dimension_semantics=("arbitrary",) * len(grid), vmem_limit_bytes=VMEM_MB << 20)
    if rider is None:
        return pl.pallas_call(body, name=name, grid=grid, in_specs=in_specs, out_specs=out_specs, out_shape=out_shape,
                              scratch_shapes=list(scratch), compiler_params=params)
    single = not isinstance(out_shape, (list, tuple))
    out_specs, out_shape = ([out_specs], [out_shape]) if single else (list(out_specs), list(out_shape))
    n_in, n_out, n_scr = len(in_specs), len(out_shape), len(scratch)
    r_in, r_out = len(rider.ins), len(rider.out_shapes)

    def hosted(*refs):
        ins, refs = refs[:n_in + r_in], refs[n_in + r_in:]
        outs, scr = refs[:n_out + r_out], refs[n_out + r_out:]
        riding = (ins[n_in:], outs[n_out:], scr[n_scr:])

        @pl.when(pl.program_id(0) == 0)
        def _():
            rider.start(*riding)

        body(*ins[:n_in], *outs[:n_out], *scr[:n_scr])

        @pl.when(pl.program_id(0) == grid[0] - 1)
        def _():
            rider.wait(*riding)

    call = pl.pallas_call(
        hosted, name=name, grid=grid, in_specs=list(in_specs) + [HBM_SPEC] * r_in,
        out_specs=out_specs + [HBM_SPEC] * r_out, out_shape=out_shape + rider.out_shapes,
        scratch_shapes=list(scratch) + rider.sems, compiler_params=params)

    def run(*args):
        res = call(*args, *rider.ins)
        own = res[:n_out]
        return (own[0] if single else own), list(res[n_out:])

    return run


def _iota(shape, axis):
    return lax.broadcasted_iota(jnp.int32, shape, axis)


def _colsum(a):
    return jnp.sum(a, axis=0, keepdims=True)


def _silu(x):
    return x * jax.nn.sigmoid(x)


def _modulate(x, g, sh, sc):
    xn = x * lax.rsqrt(jnp.mean(x * x, axis=-1, keepdims=True) + EPS)
    return (xn * g) * (1.0 + sc) + sh


def _stream_rows(mv_ref, i, tt, tc, k):
    isc = (i * tt + _iota((tt, 1), 0)) < tc
    return isc, jnp.where(isc, mv_ref[k:k + 1, :], mv_ref[3 + k:4 + k, :])


def _acc_stream(ref, k, isc, val):
    ref[k:k + 1, :] += _colsum(jnp.where(isc, val, 0.0))
    ref[3 + k:4 + k, :] += _colsum(jnp.where(isc, 0.0, val))


CC_ROWS = 16
CTX_ROW = 8


def _mod_cols(n):
    return 1536 if n % 1536 == 0 else n


def _mod_fwd(cc, w_ada, b_ada3):
    n = w_ada.shape[2]
    ct = _mod_cols(n)

    def body(cc_ref, w_ref, b_ref, o_ref):
        o_ref[0] = _dot(_silu(cc_ref[...]), w_ref[0], NN) + b_ref[0]

    return _call(
        body, "mod_fwd", (NL, n // ct),
        [pl.BlockSpec((CC_ROWS, D), lambda l, j: (0, 0)), pl.BlockSpec((1, D, ct), lambda l, j: (l, 0, j)),
         pl.BlockSpec((1, 1, ct), lambda l, j: (l, 0, j))],
        pl.BlockSpec((1, CC_ROWS, ct), lambda l, j: (l, 0, j)), _S((NL, CC_ROWS, n)))(cc, w_ada, b_ada3)


def _mod_bwd(cc, w_ada, dmods):
    n = w_ada.shape[2]
    ct = _mod_cols(n)

    def body(cc_ref, w_ref, dm_ref, dw_ref, dcc_ref):
        first = (pl.program_id(0) == 0) & (pl.program_id(1) == 0)
        cc_ = cc_ref[...]
        sg = jax.nn.sigmoid(cc_)
        dm = dm_ref[0]
        dw_ref[0] = _dot(cc_ * sg, dm, TN)

        @pl.when(first)
        def _():
            dcc_ref[...] = jnp.zeros_like(dcc_ref)

        dcc_ref[...] += _dot(dm, w_ref[0], NT) * (sg * (1.0 + cc_ * (1.0 - sg)))

    return _call(
        body, "mod_bwd", (NL, n // ct),
        [pl.BlockSpec((CC_ROWS, D), lambda l, j: (0, 0)), pl.BlockSpec((1, D, ct), lambda l, j: (l, 0, j)),
         pl.BlockSpec((1, CC_ROWS, ct), lambda l, j: (l, 0, j))],
        [pl.BlockSpec((1, D, ct), lambda l, j: (l, 0, j)), pl.BlockSpec((CC_ROWS, D), lambda l, j: (0, 0))],
        [_S((NL, D, n)), _S((CC_ROWS, D))])(cc, w_ada, dmods)


def _cols(n, tt):
    return pl.BlockSpec((n, tt), lambda i: (0, i))


def _inproj_fwd(X, mv, g, ws, tc, tt):
    T = X.shape[0]
    nw = len(ws)

    def body(x_ref, mv_ref, g_ref, *refs):
        w_refs, ht_ref, p_refs = refs[:nw], refs[nw], refs[nw + 1:]
        i = pl.program_id(0)
        _, sh = _stream_rows(mv_ref, i, tt, tc, 0)
        _, sc = _stream_rows(mv_ref, i, tt, tc, 1)
        h = _modulate(x_ref[...], g_ref[...], sh, sc)
        ht_ref[...] = h.T.astype(BF16)
        hb = h.astype(BF16)
        for w_ref, p_ref in zip(w_refs, p_refs):
            p_ref[...] = jnp.dot(hb, w_ref[...], preferred_element_type=F32)

    return _call(
        body, "inproj_fwd", (T // tt,),
        [_rows(tt, D), _full((8, D)), _full((1, D))] + [_full(w.shape) for w in ws],
        [_cols(D, tt)] + [_rows(tt, w.shape[1]) for w in ws],
        [_S((D, T), BF16)] + [_S((T, w.shape[1])) for w in ws])(X, mv, g, *ws)


def _inproj_bwd(X, mv, g, ws, dps, dp_w, dres, tc, tt):
    T = X.shape[0]
    nw, nd = len(ws), len(dps)

    def body(x_ref, mv_ref, g_ref, dres_ref, *refs):
        w_refs, dp_refs = refs[:nw], refs[nw:nw + nd]
        dx_ref, dg_ref, dm_ref = refs[nw + nd:]
        i = pl.program_id(0)
        isc, sh = _stream_rows(mv_ref, i, tt, tc, 0)
        _, sc = _stream_rows(mv_ref, i, tt, tc, 1)
        dh = None
        for dp_ref, k in zip(dp_refs, dp_w):
            t = _dot(dp_ref[...], w_refs[k][...], NT)
            dh = t if dh is None else dh + t
        _, vjp = jax.vjp(_modulate, x_ref[...], g_ref[...], sh, sc)
        dx, dg, dsh, dsc = vjp(dh)
        dx_ref[...] = dres_ref[...] + dx

        @pl.when(i == 0)
        def _():
            dg_ref[...] = jnp.zeros_like(dg_ref)
            dm_ref[...] = jnp.zeros_like(dm_ref)

        dg_ref[...] += dg
        _acc_stream(dm_ref, 0, isc, dsh)
        _acc_stream(dm_ref, 1, isc, dsc)

    return _call(
        body, "inproj_bwd", (T // tt,),
        [_rows(tt, D), _full((8, D)), _full((1, D)), _rows(tt, D)] + [_full(w.shape) for w in ws]
        + [_rows(tt, dp.shape[1]) for dp in dps],
        [_rows(tt, D), _full((1, D)), _full((8, D))],
        [_S((T, D)), _S((1, D)), _S((8, D))])(X, mv, g, dres, *ws, *dps)


def _dw(At, B, tt):
    K, T = At.shape
    N = B.shape[1]
    tt = 3 * tt if T % (3 * tt) == 0 else tt
    tn = max(t for t in range(LANES, N + 1, LANES) if N % t == 0 and K * t * 4 <= DW_ACC_MB << 20)

    def body(a_ref, b_ref, o_ref):
        @pl.when(pl.program_id(1) == 0)
        def _():
            o_ref[...] = jnp.zeros_like(o_ref)

        o_ref[...] += _dot(a_ref[...], b_ref[...], NN)

    return _call(
        body, "dw", (N // tn, T // tt),
        [pl.BlockSpec((K, tt), lambda j, i: (0, i)), pl.BlockSpec((tt, tn), lambda j, i: (i, j))],
        pl.BlockSpec((K, tn), lambda j, i: (0, j)), _S((K, N)))(At, B)


def _halo_specs(T, tt, cw, col):
    r8, nb8 = tt // 8, T // 8
    return [pl.BlockSpec((tt, cw), lambda j, i: (i, col(j))),
            pl.BlockSpec((8, cw), lambda j, i: (jnp.maximum(i * r8 - 1, 0), col(j))),
            pl.BlockSpec((8, cw), lambda j, i: (jnp.minimum((i + 1) * r8, nb8 - 1), col(j)))]


def _shifts(a, prev8, next8, i, tt, tc, T):
    r = _iota((tt, 1), 0)
    t = i * tt + r
    dn = jnp.where(r == 0, prev8[7:8, :], pltpu.roll(a, 1, 0))
    dn = jnp.where((t == 0) | (t == tc), 0.0, dn)
    up = jnp.where(r == tt - 1, next8[0:1, :], pltpu.roll(a, tt - 1, 0))
    up = jnp.where((t == T - 1) | (t == tc - 1), 0.0, up)
    return dn, up


def _dn_post(y, part):
    a = _silu(y)
    nrm = lax.rsqrt(jnp.sum(a * a, axis=-1, keepdims=True) + EPS)
    f = jnp.where(part == 0, nrm * (DH ** -0.5), jnp.where(part == 1, nrm, 1.0))
    return a * f


def _conv3(w_ref, dn, mid, up):
    return w_ref[0:1, :] * dn + w_ref[1:2, :] * mid + w_ref[2:3, :] * up


def _dnprep_fwd(pq, cw, tc, tt):
    T = pq.shape[0]

    def body(p_ref, pp_ref, pn_ref, w_ref, a_ref):
        part, i = pl.program_id(0), pl.program_id(1)
        p = p_ref[...]
        dn, up = _shifts(p, pp_ref[...], pn_ref[...], i, tt, tc, T)
        y = _conv3(w_ref, dn, p, up)
        for h in range(NH):
            a_ref[:, _hs(h)] = _dn_post(y[:, _hs(h)], part)

    return _call(
        body, "dnprep_fwd", (3, T // tt),
        _halo_specs(T, tt, DN, lambda j: j) + [pl.BlockSpec((3, DN), lambda j, i: (0, j))],
        pl.BlockSpec((tt, DN), lambda j, i: (i, j)), _S((T, 3 * DN)))(pq, pq, pq, cw)


def _dnprep_bwd_act(pq, cw, da_f, da_b, tc, tt):
    T = pq.shape[0]

    def body(p_ref, pp_ref, pn_ref, w_ref, df_ref, db_ref, dy_ref):
        part, i = pl.program_id(0), pl.program_id(1)
        p = p_ref[...]
        dn, up = _shifts(p, pp_ref[...], pn_ref[...], i, tt, tc, T)
        y = _conv3(w_ref, dn, p, up)
        for h in range(NH):
            _, vjp = jax.vjp(lambda yh: _dn_post(yh, part), y[:, _hs(h)])
            dy_ref[:, _hs(h)] = vjp(df_ref[:, _hs(h)] + db_ref[:, _hs(h)])[0]

    blk = pl.BlockSpec((tt, DN), lambda j, i: (i, j))
    return _call(
        body, "dnprep_bwd_act", (3, T // tt),
        _halo_specs(T, tt, DN, lambda j: j) + [pl.BlockSpec((3, DN), lambda j, i: (0, j)), blk, blk],
        blk, _S((T, 3 * DN)))(pq, pq, pq, cw, da_f, da_b)


def _conv_bwd(dy, p, cw, tc, tt):
    T, W = p.shape
    cb = DN

    def body(dy_ref, dyp_ref, dyn_ref, p_ref, pp_ref, pn_ref, w_ref, dp_ref, dw_ref):
        i = pl.program_id(1)
        dy, p_ = dy_ref[...], p_ref[...]
        ddn, dup = _shifts(dy, dyp_ref[...], dyn_ref[...], i, tt, tc, T)
        dp_ref[...] = _conv3(w_ref, dup, dy, ddn)
        pdn, pup = _shifts(p_, pp_ref[...], pn_ref[...], i, tt, tc, T)

        @pl.when(i == 0)
        def _():
            dw_ref[...] = jnp.zeros_like(dw_ref)

        dw_ref[0:1, :] += _colsum(dy * pdn)
        dw_ref[1:2, :] += _colsum(dy * p_)
        dw_ref[2:3, :] += _colsum(dy * pup)

    wspec = pl.BlockSpec((3, cb), lambda j, i: (0, j))
    return _call(
        body, "conv_bwd", (W // cb, T // tt),
        _halo_specs(T, tt, cb, lambda j: j) * 2 + [wspec],
        [pl.BlockSpec((tt, cb), lambda j, i: (i, j)), wspec], [_S((T, W)), _S((3, W))])(dy, dy, dy, p, p, p, cw)


def _sc_fwd(sx, sb, sc_, cw, tc, tt):
    T = sx.shape[0]

    def body(x_ref, xp_ref, xn_ref, c_ref, cp_ref, cn_ref, b_ref, w_ref, y_ref):
        i = pl.program_id(1)
        u = c_ref[...] * x_ref[...]
        dn, up = _shifts(u, cp_ref[...] * xp_ref[...], cn_ref[...] * xn_ref[...], i, tt, tc, T)
        y_ref[...] = b_ref[...] * _conv3(w_ref, dn, u, up)

    blk = pl.BlockSpec((tt, LANES), lambda j, i: (i, j))
    return _call(
        body, "sc_fwd", (PW // LANES, T // tt),
        _halo_specs(T, tt, LANES, lambda j: j) * 2 + [blk, pl.BlockSpec((3, LANES), lambda j, i: (0, j))],
        blk, _S((T, PW)))(sx, sx, sx, sc_, sc_, sc_, sb, cw)


def _sc_bwd(sx, sb, sc_, cw, dy, tc, tt):
    T = sx.shape[0]

    def body(x_ref, xp_ref, xn_ref, c_ref, cp_ref, cn_ref, b_ref, bp_ref, bn_ref, dy_ref, dyp_ref, dyn_ref, w_ref,
             dx_ref, db_ref, dc_ref, dw_ref):
        i = pl.program_id(1)
        x, c, dy_ = x_ref[...], c_ref[...], dy_ref[...]
        u = c * x
        udn, uup = _shifts(u, cp_ref[...] * xp_ref[...], cn_ref[...] * xn_ref[...], i, tt, tc, T)
        db_ref[...] = dy_ * _conv3(w_ref, udn, u, uup)
        e = dy_ * b_ref[...]
        edn, eup = _shifts(e, dyp_ref[...] * bp_ref[...], dyn_ref[...] * bn_ref[...], i, tt, tc, T)
        du = _conv3(w_ref, eup, e, edn)
        dx_ref[...] = du * c
        dc_ref[...] = du * x

        @pl.when(i == 0)
        def _():
            dw_ref[...] = jnp.zeros_like(dw_ref)

        dw_ref[0:1, :] += _colsum(e * udn)
        dw_ref[1:2, :] += _colsum(e * u)
        dw_ref[2:3, :] += _colsum(e * uup)

    blk = pl.BlockSpec((tt, LANES), lambda j, i: (i, j))
    wspec = pl.BlockSpec((3, LANES), lambda j, i: (0, j))
    return _call(
        body, "sc_bwd", (PW // LANES, T // tt),
        _halo_specs(T, tt, LANES, lambda j: j) * 4 + [wspec],
        [blk, blk, blk, wspec], [_S((T, PW))] * 3 + [_S((3, PW))])(
            sx, sx, sx, sc_, sc_, sc_, sb, sb, sb, dy, dy, dy, cw)


def _group_select(vals):
    g = _iota((1, PW), 1) // (PW // len(POOL_WIN))
    return jnp.where(g == 0, vals[0], jnp.where(g == 1, vals[1], jnp.where(g == 2, vals[2], vals[3])))


def _nested_box(get, mirror):
    acc, outs, pl_, ph_ = get(0), [], 0, 0
    for lo, hi in POOL_WIN:
        if mirror:
            lo, hi = hi, lo
        for k in range(pl_ + 1, lo + 1):
            acc = acc + get(-k)
        for k in range(ph_ + 1, hi + 1):
            acc = acc + get(k)
        pl_, ph_ = lo, hi
        outs.append(acc)
    return _group_select(outs)


def _box_tokens(a, n, mirror):
    idx = _iota((n, 1), 0)

    def get(k):
        if k == 0:
            return a
        return jnp.where((idx + k >= 0) & (idx + k < n), pltpu.roll(a, (-k) % n, 0), 0.0)

    return _nested_box(get, mirror)


def _inv_count(pos, n):
    return _group_select([1.0 / (jnp.minimum(pos + hi, n - 1) - jnp.maximum(pos - lo, 0) + 1).astype(F32)
                          for lo, hi in POOL_WIN])


def _pool_rows(ref, r, R, tc, mirror):
    def get(k):
        rr = r + k
        rc = jnp.clip(rr, 0, R - 1)
        v = ref[pl.ds(pl.multiple_of(tc + rc * GW, GW), GW), :]
        if mirror:
            v = v * _inv_count(jnp.full((1, PW), rc, jnp.int32), R)
        return jnp.where((rr >= 0) & (rr < R), v, 0.0)

    return _nested_box(get, mirror)


def _pool_fwd(u, pwbd, ps, tc):
    T = u.shape[0]
    R = (T - tc) // GW

    def body(u_ref, pw_ref, ps_ref, y_ref):
        pw, scale = pw_ref[...], ps_ref[...]
        uc = u_ref[0:tc, :]
        mc = _box_tokens(uc, tc, False) * _inv_count(_iota((tc, 1), 0), tc)
        y_ref[0:tc, :] = _dot(mc - uc, pw, NN) * scale
        inv_c = _inv_count(_iota((GW, 1), 0), GW)

        def row(r, carry):
            rs = _pool_rows(u_ref, r, R, tc, False) * _inv_count(jnp.full((1, PW), r, jnp.int32), R)
            m = _box_tokens(rs, GW, False) * inv_c
            sl = pl.ds(pl.multiple_of(tc + r * GW, GW), GW)
            y_ref[sl, :] = _dot(m - u_ref[sl, :], pw, NN) * scale
            return carry

        lax.fori_loop(0, R, row, 0)

    return pl.pallas_call(
        body, name="pool_fwd", out_shape=_S((T, PW)),
        compiler_params=pltpu.CompilerParams(vmem_limit_bytes=VMEM_MB << 20))(u, pwbd, ps)


def _pool_bwd(u, pwbd, ps, dy, tc):
    T = u.shape[0]
    R = (T - tc) // GW

    def body(u_ref, pw_ref, ps_ref, dy_ref, du_ref, dpw_ref, dps_ref, dd_ref):
        pw, scale = pw_ref[...], ps_ref[...]
        dpw_ref[...] = jnp.zeros_like(dpw_ref)
        dps_ref[...] = jnp.zeros_like(dps_ref)

        def back(d, dy_):
            dz = dy_ * scale
            dpw_ref[...] += _dot(d, dz, TN)
            dps_ref[...] += _colsum(dy_ * _dot(d, pw, NN))
            return _dot(dz, pw, NT)

        uc = u_ref[0:tc, :]
        inv_cc = _inv_count(_iota((tc, 1), 0), tc)
        ddc = back(_box_tokens(uc, tc, False) * inv_cc - uc, dy_ref[0:tc, :])
        du_ref[0:tc, :] = _box_tokens(ddc * inv_cc, tc, True) - ddc
        inv_c = _inv_count(_iota((GW, 1), 0), GW)

        def row1(r, carry):
            rs = _pool_rows(u_ref, r, R, tc, False) * _inv_count(jnp.full((1, PW), r, jnp.int32), R)
            m = _box_tokens(rs, GW, False) * inv_c
            sl = pl.ds(pl.multiple_of(tc + r * GW, GW), GW)
            dd_ref[sl, :] = back(m - u_ref[sl, :], dy_ref[sl, :])
            return carry

        lax.fori_loop(0, R, row1, 0)

        def row2(r, carry):
            t1 = _pool_rows(dd_ref, r, R, tc, True)
            sl = pl.ds(pl.multiple_of(tc + r * GW, GW), GW)
            du_ref[sl, :] = _box_tokens(t1 * inv_c, GW, True) - dd_ref[sl, :]
            return carry

        lax.fori_loop(0, R, row2, 0)

    return pl.pallas_call(
        body, name="pool_bwd", out_shape=[_S((T, PW)), _S((PW, PW)), _S((1, PW))],
        scratch_shapes=[pltpu.VMEM((T, PW), F32)],
        compiler_params=pltpu.CompilerParams(vmem_limit_bytes=VMEM_MB << 20))(u, pwbd, ps, dy)


def _scan_consts():
    i = np.arange(CH)
    lower = (i[:, None] >= i[None, :]).astype(np.float32)
    return jnp.asarray(np.stack([lower, lower.T])), jnp.asarray(np.stack([lower.T, lower]))


def _gates(pab, al, dtb, csum):
    sp_in = pab + dtb
    sp = jnp.maximum(sp_in, 0.0) + jnp.log(1.0 + jnp.exp(-jnp.abs(sp_in)))
    nexp = -jnp.exp(al)
    gm = nexp * sp
    return gm, jax.nn.sigmoid(pab), _dot(csum, gm, NN, hi=True), sp_in, nexp


def _lane_col(m, j):
    return jnp.sum(jnp.where(_iota(m.shape, 1) == j, m, 0.0), axis=1, keepdims=True)


def _hs(h):
    return slice(h * DH, (h + 1) * DH)


HS = NH * CH
X3 = "x3"


def _stack(x, base=0):
    return jnp.concatenate([x[:, base + h * DH:base + (h + 1) * DH] for h in range(NH)], axis=0)


def _heads(st):
    return [st[h * CH:(h + 1) * CH] for h in range(NH)]


def _rowsum(a):
    return jnp.sum(a, axis=1, keepdims=True)


def _row_of(col):
    e0 = (_iota((8, LANES), 1) == 0).astype(F32)
    return _dot(e0, jnp.broadcast_to(col, (HS, LANES)), NT, hi=True)[0:1, :]


def _inverses(nms):
    eye = (_iota((HS, HS), 0) == _iota((HS, HS), 1)).astype(F32)
    x0s, mps = [eye + nm for nm in nms], list(nms)
    for _ in range(5):
        mps = [_dot(mp, mp, NN) for mp in mps]
        x0s = [x0 + _dot(x0, mp, NN) for x0, mp in zip(x0s, mps)]
    rs = [eye - _dot(eye - nm, x0, NN, hi=X3) for nm, x0 in zip(nms, x0s)]
    return [x0 + _dot(x0, r, NN) for x0, r in zip(x0s, rs)]


def _dn_chunk_pre(qkv, pab, al, dtb, csum_d, d):
    gm, bm, gcm, sp_in, nexp = _gates(pab, al, dtb, csum_d)
    gc = jnp.concatenate([_lane_col(gcm, d * NH + h) for h in range(NH)], axis=0)
    beta = jnp.concatenate([_lane_col(bm, 8 + d * NH + h) for h in range(NH)], axis=0)
    q, k, v = _stack(qkv, 0), _stack(qkv, DN), _stack(qkv, 2 * DN)
    ii, jj = _iota((HS, HS), 0), _iota((HS, HS), 1)
    sh = CH.bit_length() - 1
    same = (ii >> sh) == (jj >> sh)
    incl = same & ((ii >= jj) if d == 0 else (ii <= jj))
    strict = same & ((ii > jj) if d == 0 else (ii < jj))
    Di = jnp.where(incl, jnp.exp(jnp.where(incl, gc - _row_of(gc), 0.0)), 0.0)
    Ds = jnp.where(strict, Di, 0.0)
    kb = k * beta
    kk = _dot(kb, k, NT)
    return dict(q=q, k=k, v=v, beta=beta, gc=gc, gm=gm, bm=bm, sp_in=sp_in, nexp=nexp, Di=Di, Ds=Ds, strict=strict,
                last=CH - 1 if d == 0 else 0, kb=kb, kk=kk)


def _dn_chunk_post(c, tm, uw=None):
    q, k, v, beta, gc, kb, last = (c[n] for n in ("q", "k", "v", "beta", "gc", "kb", "last"))
    E = jnp.exp(gc)
    gls = [gc[h * CH + last:h * CH + last + 1, :] for h in range(NH)]
    xs = jnp.exp(jnp.concatenate([jnp.broadcast_to(g, (CH, 1)) for g in gls], axis=0) - gc)
    qk = _dot(q, k, NT)
    u, w = uw if uw is not None else (_dot(tm, v * beta, NN, hi=X3), _dot(tm, kb * E, NN, hi=X3))
    return dict(c, tm=tm, E=E, gls=gls, xs=xs, qk=qk, u=u, w=w, ks=k * xs, qd=q * E, aqk=qk * c["Di"])


def _dn_chunks_bwd_math(cs, Ss, dS2s, dos, vns, dvns):
    I = range(len(cs))
    q, k, v, beta, tm, E, xs, kb, u, w = ([c[n] for c in cs] for n in ("q", "k", "v", "beta", "tm", "E", "xs", "kb", "u", "w"))
    doh, vnh, dvnh = ([_heads(a) for a in l] for l in (dos, vns, dvns))
    cat = lambda parts: jnp.concatenate(parts, axis=0)
    dqd = [cat([_dot(doh[i][h], Ss[i][h], NT) for h in range(NH)]) for i in I]
    dks = [cat([_dot(vnh[i][h], dS2s[i][h], NT) for h in range(NH)]) for i in I]
    dw = [-cat([_dot(dvnh[i][h], Ss[i][h], NT) for h in range(NH)]) for i in I]
    daqk = [_dot(dos[i], vns[i], NT) for i in I]
    drb = [_dot(tm[i], dvns[i], TN, hi=X3) for i in I]
    drw = [_dot(tm[i], dw[i], TN, hi=X3) for i in I]
    dA = [jnp.where(cs[i]["strict"], -(_dot(drb[i], u[i], NT) + _dot(drw[i], w[i], NT)), 0.0) for i in I]
    dM1 = [dA[i] * cs[i]["Ds"] for i in I]
    dM2 = [daqk[i] * cs[i]["Di"] for i in I]
    dkb = [_dot(dM1[i], k[i], NN) + drw[i] * E[i] for i in I]
    dk = [_dot(dM1[i], kb[i], TN) + _dot(dM2[i], q[i], TN) + dks[i] * xs[i] for i in I]
    dq = [_dot(dM2[i], k[i], NN) + dqd[i] * E[i] for i in I]
    on_diag = _iota((HS, HS), 0) == _iota((HS, HS), 1)
    out = []
    for i in I:
        G = dM1[i] * cs[i]["kk"] + dM2[i] * cs[i]["qk"]
        col = _rowsum(jnp.where(on_diag, jnp.broadcast_to(_colsum(G), (HS, HS)), 0.0))
        dxx = _rowsum(dks[i] * k[i]) * xs[i]
        dgc = _rowsum(G) - col + (_rowsum(dqd[i] * q[i]) + _rowsum(drw[i] * kb[i])) * E[i] - dxx
        at_last = _iota((CH, 1), 0) == cs[i]["last"]
        ends = []
        for h in range(NH):
            dgl = (_colsum(_rowsum(Ss[i][h] * dS2s[i][h])) * jnp.exp(cs[i]["gls"][h])
                   + _colsum(dxx[h * CH:(h + 1) * CH]))
            ends.append(jnp.where(at_last, dgl, 0.0))
        dbeta = _rowsum(drb[i] * v[i]) + _rowsum(dkb[i] * k[i])
        out.append((dq[i], dk[i] + dkb[i] * beta[i], drb[i] * beta[i], dgc + cat(ends), dbeta))
    return out


def _chunk_group(n, want=2):
    g = want
    while n % g:
        g //= 2
    return g


def _dn_chunks_fwd(qkv, pab, alr, dtr, rider=None):
    T = qkv.shape[0]
    n = T // CH
    G = _chunk_group(n, 4)
    csum, _ = _scan_consts()

    def body(q_ref, p_ref, cs_ref, al_ref, dt_ref, *outs):
        inst = [(g, d) for g in range(G) for d in range(2)]
        pres = [_dn_chunk_pre(q_ref[g * CH:(g + 1) * CH, :], p_ref[g * CH:(g + 1) * CH, :], al_ref[...], dt_ref[...],
                              cs_ref[d], d) for g, d in inst]
        tms = _inverses([-(p["kk"] * p["Ds"]) for p in pres])
        for (g, d), pre, tm in zip(inst, pres, tms):
            rows = slice(g * HS, (g + 1) * HS)
            u_ref, w_ref, ks_ref, qd_ref, aqk_ref, eg_ref, tm_ref = outs[7 * d:7 * d + 7]
            c = _dn_chunk_post(pre, tm)
            tm_ref[rows, :] = tm
            u_ref[rows, :] = c["u"]
            w_ref[rows, :] = c["w"].astype(BF16)
            ks_ref[rows, :] = c["ks"].astype(BF16)
            qd_ref[rows, :] = c["qd"].astype(BF16)
            aqk_ref[rows, :] = c["aqk"].astype(BF16)
            egs = [jnp.broadcast_to(jnp.exp(gl), (1, LANES)) for gl in c["gls"]]
            eg_ref[g * 8:(g + 1) * 8, :] = jnp.concatenate(egs + [jnp.zeros((8 - NH, LANES), F32)], axis=0)

    st = lambda w_: pl.BlockSpec((G * HS, w_), lambda i: (i, 0))
    one = [st(DH)] * 4 + [st(HS), pl.BlockSpec((G * 8, LANES), lambda i: (i, 0)), st(HS)]
    shp = [_S((n * HS, DH)), _S((n * HS, DH), BF16), _S((n * HS, DH), BF16), _S((n * HS, DH), BF16),
           _S((n * HS, HS), BF16), _S((n * 8, LANES)), _S((n * HS, HS))]
    res = _call(
        body, "dn_chunks_fwd", (n // G,),
        [_rows(G * CH, 3 * DN), _rows(G * CH, LANES), _full((2, CH, CH)), _full((1, LANES)), _full((1, LANES))],
        one * 2, shp * 2, rider=rider)(qkv, pab, csum, alr, dtr)
    outs, riding = (res, None) if rider is None else res
    parts = tuple(outs[:7]), tuple(outs[7:])
    return parts if rider is None else (parts, riding)


def _scan_plan(n, ncx):
    sg = 2 if n % 2 == 0 and ncx % 2 == 0 else 1
    ng, ncg = n // sg, ncx // sg
    return sg, ((lambda i: i), (lambda i: jnp.where(i < ncg, ncg - 1 - i, ng - 1 - (i - ncg))))


def _scan_specs(order, sg):
    st = lambda w_: pl.BlockSpec((sg * HS, w_), lambda i: (order(i), 0))
    return dict(st=st(DH), aqk=st(HS), eg=pl.BlockSpec((sg * 8, LANES), lambda i: (order(i), 0)),
                tok=pl.BlockSpec((sg * CH, DN), lambda i: (order(i), 0)),
                state=pl.BlockSpec((sg, DN, DH), lambda i: (order(i), 0, 0)))


def _scan_fwd(parts, T, tc, rider=None):
    n = T // CH
    sg, orders = _scan_plan(n, tc // CH)

    def body(*refs):
        S_f, S_b = refs[-2:]

        @pl.when(pl.program_id(0) == 0)
        def _():
            S_f[...] = jnp.zeros_like(S_f)
            S_b[...] = jnp.zeros_like(S_b)

        for g in range(sg):
            for d, S in enumerate((S_f, S_b)):
                u_ref, w_ref, ks_ref, qd_ref, aqk_ref, eg_ref = refs[6 * d:6 * d + 6]
                o_ref, ss_ref, vn_ref = refs[12 + 3 * d:15 + 3 * d]
                k = g if d == 0 else sg - 1 - g
                rows = slice(k * HS, (k + 1) * HS)
                ss_ref[k] = S[...]
                Sh = [S[_hs(h), :] for h in range(NH)]
                wh, ksh, qdh = _heads(w_ref[rows, :]), _heads(ks_ref[rows, :]), _heads(qd_ref[rows, :])
                vn = u_ref[rows, :] - jnp.concatenate([_dot(wh[h], Sh[h], NN) for h in range(NH)], axis=0)
                vn_ref[rows, :] = vn
                av, vnh = _heads(_dot(aqk_ref[rows, :], vn, NN)), _heads(vn)
                for h in range(NH):
                    o_ref[k * CH:(k + 1) * CH, _hs(h)] = _dot(qdh[h], Sh[h], NN) + av[h]
                    S[_hs(h), :] = Sh[h] * eg_ref[k * 8 + h:k * 8 + h + 1, :] + _dot(ksh[h], vnh[h], TN)

    ins, outs, shp = [], [], []
    for d in range(2):
        sp = _scan_specs(orders[d], sg)
        ins += [sp["st"]] * 4 + [sp["aqk"], sp["eg"]]
        outs += [sp["tok"], sp["state"], sp["st"]]
        shp += [_S((T, DN)), _S((n, DN, DH)), _S((n * HS, DH))]
    res = _call(body, "scan_fwd", (n // sg,), ins, outs, shp,
                scratch=[pltpu.VMEM((DN, DH), F32), pltpu.VMEM((DN, DH), F32)], rider=rider)(*parts[0][:6], *parts[1][:6])
    res, riding = (res, None) if rider is None else res
    out = tuple(res[:3]), tuple(res[3:])
    return out if rider is None else (out, riding)


def _scan_bwd(do, parts, tc):
    T = do.shape[0]
    n = T // CH
    sg, fwd_orders = _scan_plan(n, tc // CH)
    orders = [lambda s, f=f: f(n // sg - 1 - s) for f in fwd_orders]

    def body(*refs):
        dS_f, dS_b = refs[-2:]

        @pl.when(pl.program_id(0) == 0)
        def _():
            dS_f[...] = jnp.zeros_like(dS_f)
            dS_b[...] = jnp.zeros_like(dS_b)

        for g in range(sg):
            for d, dS in enumerate((dS_f, dS_b)):
                do_ref, w_ref, ks_ref, qd_ref, aqk_ref, eg_ref = refs[6 * d:6 * d + 6]
                dvn_ref, dss_ref = refs[12 + 2 * d:14 + 2 * d]
                k = sg - 1 - g if d == 0 else g
                rows = slice(k * HS, (k + 1) * HS)
                dss_ref[k] = dS[...]
                dSh = [dS[_hs(h), :] for h in range(NH)]
                wh, ksh, qdh = _heads(w_ref[rows, :]), _heads(ks_ref[rows, :]), _heads(qd_ref[rows, :])
                do_st = _stack(do_ref[k * CH:(k + 1) * CH, :])
                dvn = (_dot(aqk_ref[rows, :], do_st, TN)
                       + jnp.concatenate([_dot(ksh[h], dSh[h], NN) for h in range(NH)], axis=0))
                dvn_ref[rows, :] = dvn
                doh, dvnh = _heads(do_st), _heads(dvn)
                for h in range(NH):
                    dS[_hs(h), :] = (_dot(qdh[h], doh[h], TN) + dSh[h] * eg_ref[k * 8 + h:k * 8 + h + 1, :]
                                     - _dot(wh[h], dvnh[h], TN))

    ins, outs, shp, args = [], [], [], []
    for d in range(2):
        sp = _scan_specs(orders[d], sg)
        ins += [sp["tok"]] + [sp["st"]] * 3 + [sp["aqk"], sp["eg"]]
        outs += [sp["st"], sp["state"]]
        shp += [_S((n * HS, DH)), _S((n, DN, DH))]
        args += [do, *parts[d][1:6]]
    res = _call(body, "scan_bwd", (n // sg,), ins, outs, shp,
                scratch=[pltpu.VMEM((DN, DH), F32), pltpu.VMEM((DN, DH), F32)])(*args)
    return tuple(res[:2]), tuple(res[2:])


def _dn_chunks_bwd(qkv, pab, alr, dtr, do, fwd, bwd, rider=None):
    T = qkv.shape[0]
    n = T // CH
    G = _chunk_group(n)
    csum, csum_t = _scan_consts()

    def body(q_ref, p_ref, do_ref, cs_ref, cst_ref, al_ref, dt_ref, *refs):
        dq_refs, dp_refs, acc_ref = refs[14:16], refs[16:18], refs[18]

        @pl.when(pl.program_id(0) == 0)
        def _():
            acc_ref[...] = jnp.zeros_like(acc_ref)

        lane = _iota((CH, LANES), 1)
        inst = [(g, d) for g in range(G) for d in range(2)]
        cs, Ss, dS2s, dos, vns, dvns = [], [], [], [], [], []
        for g, d in inst:
            tok, rows = slice(g * CH, (g + 1) * CH), slice(g * HS, (g + 1) * HS)
            vn_ref, dvn_ref, ss_ref, dss_ref, tm_ref, u_ref, w_ref = refs[7 * d:7 * d + 7]
            cs.append(_dn_chunk_post(
                _dn_chunk_pre(q_ref[tok, :], p_ref[tok, :], al_ref[...], dt_ref[...], cs_ref[d], d), tm_ref[rows, :],
                uw=(u_ref[rows, :], w_ref[rows, :])))
            Ss.append([ss_ref[g, _hs(h), :] for h in range(NH)])
            dS2s.append([dss_ref[g, _hs(h), :] for h in range(NH)])
            dos.append(_stack(do_ref[tok, :]))
            vns.append(vn_ref[rows, :])
            dvns.append(dvn_ref[rows, :])
        for (g, d), c, (dq, dk, dv, dgc, dbeta) in zip(inst, cs, _dn_chunks_bwd_math(cs, Ss, dS2s, dos, vns, dvns)):
            tok = slice(g * CH, (g + 1) * CH)
            dgcm = jnp.zeros((CH, LANES), F32)
            dbm = jnp.zeros((CH, LANES), F32)
            for h, (a, b_, c_, e, f) in enumerate(zip(*map(_heads, (dq, dk, dv, dgc, dbeta)))):
                dq_refs[d][tok, _hs(h)] = a
                dq_refs[d][tok, _hs(NH + h)] = b_
                dq_refs[d][tok, _hs(2 * NH + h)] = c_
                dgcm = jnp.where(lane == d * NH + h, e, dgcm)
                dbm = jnp.where(lane == 8 + d * NH + h, f, dbm)
            dgm = _dot(cst_ref[d], dgcm, NN, hi=True)
            dsp = dgm * c["nexp"] * jax.nn.sigmoid(c["sp_in"])
            dp_refs[d][tok, :] = dsp + dbm * c["bm"] * (1.0 - c["bm"])
            acc_ref[0:1, :] += _colsum(dgm * c["gm"])
            acc_ref[1:2, :] += _colsum(dsp)

    st = pl.BlockSpec((G * HS, DH), lambda i: (i, 0))
    state = pl.BlockSpec((G, DN, DH), lambda i: (i, 0, 0))
    return _call(
        body, "dn_chunks_bwd", (n // G,),
        [_rows(G * CH, 3 * DN), _rows(G * CH, LANES), _rows(G * CH, DN), _full((2, CH, CH)), _full((2, CH, CH)),
         _full((1, LANES)), _full((1, LANES))]
        + [st, st, state, state, pl.BlockSpec((G * HS, HS), lambda i: (i, 0)), st, st] * 2,
        [_rows(G * CH, 3 * DN)] * 2 + [_rows(G * CH, LANES)] * 2 + [_full((8, LANES))],
        [_S((T, 3 * DN))] * 2 + [_S((T, LANES))] * 2 + [_S((8, LANES))], rider=rider)(
            qkv, pab, do, csum, csum_t, alr, dtr, *fwd, *bwd)


def _head_out(o, z, g):
    on = o * lax.rsqrt(jnp.mean(o * o, axis=-1, keepdims=True) + EPS) * g
    return on * _silu(z)


def _mix_branches(of_ref, ob_ref, z_ref, yp_ref, ys_ref, pg_ref, gdn_ref, wa_ref, wb_ref, wc_ref):
    ons, ya = [], None
    for h in range(NH):
        on = _head_out(of_ref[:, _hs(h)] + ob_ref[:, _hs(h)], z_ref[:, _hs(h)], gdn_ref[...])
        t = _dot(on, wa_ref[_hs(h), :], NN)
        ya = t if ya is None else ya + t
        ons.append(on)
    ys = [ya, _dot(yp_ref[...], wb_ref[...], NN), _dot(ys_ref[...], wc_ref[...], NN)]
    sg = [jax.nn.sigmoid(pg_ref[:, k * D:(k + 1) * D]) for k in range(3)]
    return ons, ys, sg


def _mix_fwd(X, of, ob, z, yp, ys, pg, mv, gdn, wa, wb, wc, wo, tc, tt):
    T = X.shape[0]

    def body(x_ref, of_ref, ob_ref, z_ref, yp_ref, ys_ref, pg_ref, mv_ref, gdn_ref, wa_ref, wb_ref, wc_ref, wo_ref,
             x1_ref):
        _, yb, sg = _mix_branches(of_ref, ob_ref, z_ref, yp_ref, ys_ref, pg_ref, gdn_ref, wa_ref, wb_ref, wc_ref)
        mix = _dot(sg[0] * yb[0] + sg[1] * yb[1] + sg[2] * yb[2], wo_ref[...], NN)
        _, gate = _stream_rows(mv_ref, pl.program_id(0), tt, tc, 2)
        x1_ref[...] = x_ref[...] + gate * mix

    return _call(
        body, "mix_fwd", (T // tt,),
        [_rows(tt, D), _rows(tt, DN), _rows(tt, DN), _rows(tt, DN), _rows(tt, PW), _rows(tt, PW), _rows(tt, 3 * D),
         _full((8, D)), _full((1, DH)), _full(wa.shape), _full(wb.shape), _full(wc.shape), _full(wo.shape)],
        _rows(tt, D), _S((T, D)))(X, of, ob, z, yp, ys, pg, mv, gdn, wa, wb, wc, wo)


def _mix_bwd(dx1, of, ob, z, yp, ys, pg, mv, gdn, wa, wb, wc, wo, tc, tt, rider=None):
    T = dx1.shape[0]

    def body(dx_ref, of_ref, ob_ref, z_ref, yp_ref, ys_ref, pg_ref, mv_ref, gdn_ref, wa_ref, wb_ref, wc_ref, wo_ref,
             do_ref, dz_ref, dyp_ref, dys_ref, dpg_ref, dwa_ref, dwb_ref, dwc_ref, dwo_ref, dgdn_ref, dm_ref):
        i = pl.program_id(0)

        @pl.when(i == 0)
        def _():
            for r in (dwa_ref, dwb_ref, dwc_ref, dwo_ref, dgdn_ref, dm_ref):
                r[...] = jnp.zeros_like(r)

        ons, yb, sg = _mix_branches(of_ref, ob_ref, z_ref, yp_ref, ys_ref, pg_ref, gdn_ref, wa_ref, wb_ref, wc_ref)
        ymix = sg[0] * yb[0] + sg[1] * yb[1] + sg[2] * yb[2]
        isc, gate = _stream_rows(mv_ref, i, tt, tc, 2)
        dx = dx_ref[...]
        dmix = dx * gate
        _acc_stream(dm_ref, 2, isc, dx * _dot(ymix, wo_ref[...], NN))
        dwo_ref[...] += _dot(ymix, dmix, TN)
        dymix = _dot(dmix, wo_ref[...], NT)
        dyb = []
        for k in range(3):
            dyb.append(dymix * sg[k])
            dpg_ref[:, k * D:(k + 1) * D] = dymix * yb[k] * sg[k] * (1.0 - sg[k])
        dwb_ref[...] += _dot(yp_ref[...], dyb[1], TN)
        dwc_ref[...] += _dot(ys_ref[...], dyb[2], TN)
        dyp_ref[...] = _dot(dyb[1], wb_ref[...], NT)
        dys_ref[...] = _dot(dyb[2], wc_ref[...], NT)
        dg = jnp.zeros((1, DH), F32)
        for h in range(NH):
            dwa_ref[_hs(h), :] += _dot(ons[h], dyb[0], TN)
            don = _dot(dyb[0], wa_ref[_hs(h), :], NT)
            _, vjp = jax.vjp(_head_out, of_ref[:, _hs(h)] + ob_ref[:, _hs(h)], z_ref[:, _hs(h)], gdn_ref[...])
            do_h, dz_h, dg_h = vjp(don)
            do_ref[:, _hs(h)] = do_h
            dz_ref[:, _hs(h)] = dz_h
            dg = dg + dg_h
        dgdn_ref[...] += dg

    return _call(
        body, "mix_bwd", (T // tt,),
        [_rows(tt, D), _rows(tt, DN), _rows(tt, DN), _rows(tt, DN), _rows(tt, PW), _rows(tt, PW), _rows(tt, 3 * D),
         _full((8, D)), _full((1, DH)), _full(wa.shape), _full(wb.shape), _full(wc.shape), _full(wo.shape)],
        [_rows(tt, DN), _rows(tt, DN), _rows(tt, PW), _rows(tt, PW), _rows(tt, 3 * D),
         _full(wa.shape), _full(wb.shape), _full(wc.shape), _full(wo.shape), _full((1, DH)), _full((8, D))],
        [_S((T, DN)), _S((T, DN)), _S((T, PW)), _S((T, PW)), _S((T, 3 * D)),
         _S(wa.shape), _S(wb.shape), _S(wc.shape), _S(wo.shape), _S((1, DH)), _S((8, D))], rider=rider)(
            dx1, of, ob, z, yp, ys, pg, mv, gdn, wa, wb, wc, wo)


def _ffn_fwd(X1, mv, g, wgu, wd, tc, tt):
    T = X1.shape[0]

    def body(x_ref, mv_ref, g_ref, wgu_ref, wd_ref, x2_ref, ff_ref):
        i = pl.program_id(0)
        _, sh = _stream_rows(mv_ref, i, tt, tc, 0)
        _, sc = _stream_rows(mv_ref, i, tt, tc, 1)
        _, gate = _stream_rows(mv_ref, i, tt, tc, 2)
        x = x_ref[...]
        gu = _dot(_modulate(x, g_ref[...], sh, sc), wgu_ref[...], NN)
        ff = _dot(_silu(gu[:, :DFF]) * gu[:, DFF:], wd_ref[...], NN)
        ff_ref[...] = ff
        x2_ref[...] = x + gate * ff

    return _call(
        body, "ffn_fwd", (T // tt,),
        [_rows(tt, D), _full((8, D)), _full((1, D)), _full(wgu.shape), _full(wd.shape)],
        [_rows(tt, D)] * 2, [_S((T, D))] * 2)(X1, mv, g, wgu, wd)


def _ffn_bwd(X1, ff, dx2, mv, g, wgu, wd, tc, tt, rider=None):
    T = X1.shape[0]

    def body(x_ref, ff_ref, dx2_ref, mv_ref, g_ref, wgu_ref, wd_ref, dx1_ref, ht_ref, dgu_ref, actt_ref, dff_ref, dg_ref,
             dm_ref):
        i = pl.program_id(0)
        isc, sh = _stream_rows(mv_ref, i, tt, tc, 0)
        _, sc = _stream_rows(mv_ref, i, tt, tc, 1)
        _, gate = _stream_rows(mv_ref, i, tt, tc, 2)
        x, dx2_ = x_ref[...], dx2_ref[...]
        h, vjp = jax.vjp(_modulate, x, g_ref[...], sh, sc)
        ht_ref[...] = h.T.astype(BF16)
        gu = jnp.dot(h.astype(BF16), wgu_ref[...], preferred_element_type=F32)
        ga, up = gu[:, :DFF], gu[:, DFF:]
        sg = jax.nn.sigmoid(ga)
        actt_ref[...] = (ga * sg * up).T.astype(BF16)
        dff = dx2_ * gate
        dff_ref[...] = dff.astype(BF16)
        dact = _dot(dff, wd_ref[...], NT)
        dga = (dact * up * (sg * (1.0 + ga * (1.0 - sg)))).astype(BF16)
        dup = (dact * ga * sg).astype(BF16)
        dgu_ref[:, :DFF] = dga
        dgu_ref[:, DFF:] = dup
        dh = _dot(dga, wgu_ref[:, :DFF], NT) + _dot(dup, wgu_ref[:, DFF:], NT)
        dx, dg, dsh, dsc = vjp(dh)
        dx1_ref[...] = dx2_ + dx

        @pl.when(i == 0)
        def _():
            dg_ref[...] = jnp.zeros_like(dg_ref)
            dm_ref[...] = jnp.zeros_like(dm_ref)

        dg_ref[...] += dg
        _acc_stream(dm_ref, 0, isc, dsh)
        _acc_stream(dm_ref, 1, isc, dsc)
        _acc_stream(dm_ref, 2, isc, dx2_ * ff_ref[...])

    return _call(
        body, "ffn_bwd", (T // tt,),
        [_rows(tt, D), _rows(tt, D), _rows(tt, D), _full((8, D)), _full((1, D)), _full(wgu.shape), _full(wd.shape)],
        [_rows(tt, D), _cols(D, tt), _rows(tt, 2 * DFF), _cols(DFF, tt), _rows(tt, D), _full((1, D)), _full((8, D))],
        [_S((T, D)), _S((D, T), BF16), _S((T, 2 * DFF), BF16), _S((DFF, T), BF16), _S((T, D), BF16),
         _S((1, D)), _S((8, D))], rider=rider)(X1, ff, dx2, mv, g, wgu, wd)


def _rms(x, g):
    return x * lax.rsqrt(jnp.mean(x * x, axis=-1, keepdims=True) + EPS) * g


def _loss_head(X2, tgt, gf, tc):
    T = X2.shape[0]

    def body(x_ref, t_ref, g_ref, dx_ref, loss_ref, dg_ref):
        i = pl.program_id(0)

        @pl.when(i == 0)
        def _():
            dx_ref[...] = jnp.zeros_like(dx_ref)
            loss_ref[...] = jnp.zeros_like(loss_ref)
            dg_ref[...] = jnp.zeros_like(dg_ref)

        @pl.when(i > 0)
        def _():
            y, vjp = jax.vjp(_rms, x_ref[...], g_ref[...])
            err = y - t_ref[...]
            dx, dg = vjp(err * (1.0 / D))
            dx_ref[...] = dx
            dg_ref[...] += dg
            loss_ref[...] += (0.5 / D) * jnp.sum(jnp.sum(err * err, axis=1, keepdims=True), axis=0, keepdims=True)

    return _call(
        body, "loss_head", (T // tc,),
        [_rows(tc, D), pl.BlockSpec((tc, D), lambda i: (jnp.maximum(i - 1, 0), 0)), _full((1, D))],
        [_rows(tc, D), _full((8, LANES)), _full((1, D))],
        [_S((T, D)), _S((8, LANES)), _S((1, D))])(X2, tgt, gf)


def _block_diag(pw):
    g, n = pw.shape[0], pw.shape[1]
    out = jnp.zeros((g * n, g * n), pw.dtype)
    for k in range(g):
        out = lax.dynamic_update_slice(out, pw[k], (k * n, k * n))
    return out


IN_TRUE = tuple(IN_BOUNDS[k + 1] - IN_BOUNDS[k] for k in range(8))


def _overlaps(widths, cw):
    starts = np.cumsum([0] + list(widths))
    out = []
    for k in range(N_DEV):
        for i in range(len(widths)):
            a, b = max(k * cw, starts[i]), min((k + 1) * cw, starts[i + 1])
            if a < b:
                out.append((k, i, int(a - k * cw), int(a - starts[i]), int(b - a)))
    return out


def _shards_to_cols(gathered, l, widths, padded):
    nd, _, R, cw = gathered.shape
    tr = _shard_rows(R)

    def body(s_ref, *o_refs):
        for i, o_ref in enumerate(o_refs):
            if padded[i] > widths[i]:
                o_ref[...] = jnp.zeros_like(o_ref)
        for k, i, so, go, n in _overlaps(widths, cw):
            o_refs[i][:, go:go + n] = s_ref[k, 0, :, so:so + n].astype(BF16)

    return _call(
        body, "shards_to_cols", (R // tr,), [pl.BlockSpec((nd, 1, tr, cw), lambda i: (0, l, i, 0))],
        [_rows(tr, p) for p in padded], [_S((R, p), BF16) for p in padded])(gathered)


def _cols_to_shards(groups, widths, cw):
    R = groups[0].shape[0]
    tr = _shard_rows(R)

    def body(*refs):
        o_ref = refs[-1]
        for k, i, so, go, n in _overlaps(widths, cw):
            o_ref[k % 2, k // 2, 0, :, so:so + n] = refs[i][:, go:go + n]

    return _call(
        body, "cols_to_shards", (R // tr,), [_rows(tr, g.shape[1]) for g in groups],
        pl.BlockSpec((2, N_CHIP, 1, tr, cw), lambda i: (0, 0, 0, i, 0)), _S((2, N_CHIP, 1, R, cw)))(*groups)


def _mod_rows(mods_l, k0):
    rows = [mods_l[s, (k0 + k) * D:(k0 + k + 1) * D] for s in (0, 1) for k in range(3)]
    return jnp.stack(rows + [jnp.zeros((D,), F32)] * 2)


def _lane_row(v8):
    return jnp.pad(v8.reshape(1, 8), ((0, 0), (0, LANES - 8)))


LAYERED = ("w_in", "w_br_a", "w_br_b", "w_br_c", "w_o", "w_gu", "w_down")
LATE = ("w_br_a", "w_br_b", "w_br_c", "w_o", "w_gu", "w_down")
FFN_W = ("w_gu", "w_down")


def _device_step(x, c, ctx, tgt, wts, tt, comm=None):
    tc = ctx.shape[0]
    X = jnp.concatenate([ctx, x], axis=0)
    if comm is None:
        row = 0
        cc = jnp.concatenate([c, jnp.zeros((CTX_ROW - 1, D), F32), wts["c_ctx"][None, :],
                              jnp.zeros((CC_ROWS - CTX_ROW - 1, D), F32)], axis=0)
        w_ada = wts["w_ada"].astype(BF16)
        mods16 = _mod_fwd(cc, w_ada, wts["b_ada"].reshape(NL, 1, 6 * D))
    else:
        row, mods16 = comm.adaln_fwd(c, wts["c_ctx"])
    mods = jnp.stack([mods16[:, CTX_ROW], lax.dynamic_index_in_dim(mods16, row, 1, keepdims=False)], axis=1)

    saved = []
    for l in range(NL):
        ws = _shards_to_cols(wts["w_in"][l], 0, IN_TRUE, IN_WIDTHS)
        mv1, mv2 = _mod_rows(mods[l], 0), _mod_rows(mods[l], 3)
        g1, g2 = wts["norm1_g"][l][None, :], wts["norm2_g"][l][None, :]
        cw, scw = wts["dn_conv_w"][l], wts["sc_conv_w"][l]
        alr, dtr = _lane_row(wts["dn_a_log"][l]), _lane_row(wts["dn_dt_bias"][l])
        gdn = wts["dn_norm_g"][l][None, :]
        pwbd, ps = _block_diag(wts["pool_w"][l]), wts["pool_scale"][l][None, :]
        hb, pq, pz, pab, pp, sx, sb, sc_, pg = _inproj_fwd(X, mv1, g1, ws, tc, tt)
        qkv = _dnprep_fwd(pq, cw, tc, tt)
        if comm is not None and l == 0:
            parts, riding = _dn_chunks_fwd(qkv, pab, alr, dtr, rider=comm.late_weights_chips())
            ((of, ssf, vnf), (ob, ssb, vnb)), riding = _scan_fwd(parts, X.shape[0], tc,
                                                                 rider=comm.late_weights_pair(riding))
            late, w_in_1 = comm.late_weights(riding)
            wts = dict(wts, **late, w_in=[wts["w_in"][0], w_in_1])
        else:
            parts = _dn_chunks_fwd(qkv, pab, alr, dtr)
            (of, ssf, vnf), (ob, ssb, vnb) = _scan_fwd(parts, X.shape[0], tc)
        wbr = [_shards_to_cols(wts[k], l, (2 * DFF,), (2 * DFF,))[0] if k == "w_gu" else wts[k][l].astype(BF16)
               for k in LATE]
        yp = _pool_fwd(pp, pwbd, ps, tc)
        ys = _sc_fwd(sx, sb, sc_, scw, tc, tt)
        X1 = _mix_fwd(X, of, ob, pz, yp, ys, pg, mv1, gdn, *wbr[:4], tc, tt)
        X2, ff = _ffn_fwd(X1, mv2, g2, wbr[4], wbr[5], tc, tt)
        saved.append(dict(X=X, X1=X1, ff=ff, ws=ws, wbr=wbr, mv1=mv1, mv2=mv2, g1=g1, g2=g2, cw=cw, scw=scw, alr=alr, dtr=dtr,
                          gdn=gdn, pwbd=pwbd, ps=ps, hb=hb, pq=pq, pz=pz, pab=pab, pp=pp, sx=sx, sb=sb, sc=sc_, pg=pg,
                          qkv=qkv, of=of, ob=ob, ssf=ssf, ssb=ssb, vnf=vnf, vnb=vnb, parts=parts, yp=yp, ys=ys))
        X = X2

    dX, loss, dgf = _loss_head(X, tgt, wts["final_norm_g"][None, :], tc)

    gl = {k: [None] * NL for k in ("w_in", "norm1_g", "norm2_g", "dn_conv_w", "dn_a_log", "dn_dt_bias", "dn_norm_g",
                                   "pool_w", "pool_scale", "sc_conv_w", "w_br_a", "w_br_b", "w_br_c", "w_o", "w_gu",
                                   "w_down")}
    dmods = [None] * NL
    early = None
    for l in reversed(range(NL)):
        s = saved[l]
        hide = comm is not None and l == 0
        res = _ffn_bwd(s["X1"], s["ff"], dX, s["mv2"], s["g2"], s["wbr"][4], s["wbr"][5], tc, tt,
                       rider=comm.grad_pair_rider([gl[k][1] for k in LAYERED]) if hide else None)
        if hide:
            res, got = res
        dx1, h2, dgu, act, dff, dg2, dm2 = res
        gl["w_gu"][l] = _cols_to_shards([_dw(h2, dgu, tt)], (2 * DFF,), 2 * DFF // N_DEV)
        gl["w_down"][l] = _dw(act, dff, tt)
        res = _mix_bwd(dx1, s["of"], s["ob"], s["pz"], s["yp"], s["ys"], s["pg"], s["mv1"], s["gdn"], *s["wbr"][:4], tc,
                       tt, rider=comm.grad_pair_rider([gl[k][0] for k in FFN_W], FFN_W) if hide else None)
        if hide:
            res, got_ffn = res
            chip_rider = comm.grad_chip_rider(got + got_ffn)
        do, dz, dyp, dys, dpg, dwa, dwb, dwc, dwo, dgdn, dmg = res
        dpp, dpw, dps = _pool_bwd(s["pp"], s["pwbd"], s["ps"], dyp, tc)
        dsx, dsb, dsc, dscw = _sc_bwd(s["sx"], s["sb"], s["sc"], s["scw"], dys, tc, tt)
        (dvnf, dssf), (dvnb, dssb) = _scan_bwd(do, s["parts"], tc)
        res = _dn_chunks_bwd(s["qkv"], s["pab"], s["alr"], s["dtr"], do,
                             (s["vnf"], dvnf, s["ssf"], dssf, s["parts"][0][6], *s["parts"][0][:2]),
                             (s["vnb"], dvnb, s["ssb"], dssb, s["parts"][1][6], *s["parts"][1][:2]),
                             rider=chip_rider if hide else None)
        if hide:
            res, early = res
            early = comm.grad_chip_done(early)
        dqf, dqb, dpf, dpb, gacc = res
        dy = _dnprep_bwd_act(s["pq"], s["cw"], dqf, dqb, tc, tt)
        dpq, dcw = _conv_bwd(dy, s["pq"], s["cw"], tc, tt)
        dps_ = [dpq, dz, dpf, dpb, dpp, dsx, dsb, dsc, dpg]
        dp_w = [0, 1, 2, 2, 3, 4, 5, 6, 7]
        dX, dg1, dm1 = _inproj_bwd(s["X"], s["mv1"], s["g1"], s["ws"], dps_, dp_w, dx1, tc, tt)
        dws = [_dw(s["hb"], dp, tt) for dp in (dpq, dz, dpf + dpb, dpp, dsx, dsb, dsc, dpg)]
        gl["w_in"][l] = _cols_to_shards(dws, IN_TRUE, IN_BOUNDS[-1] // N_DEV)
        gl["norm1_g"][l], gl["norm2_g"][l] = dg1[0], dg2[0]
        gl["dn_conv_w"][l], gl["sc_conv_w"][l] = dcw, dscw
        gl["dn_a_log"][l], gl["dn_dt_bias"][l] = gacc[0, :8].reshape(2, NH), gacc[1, :8].reshape(2, NH)
        gl["dn_norm_g"][l] = dgdn[0]
        gl["pool_w"][l] = jnp.stack([dpw[k * GW:(k + 1) * GW, k * GW:(k + 1) * GW] for k in range(4)])
        gl["pool_scale"][l] = dps[0]
        gl["w_br_a"][l], gl["w_br_b"][l], gl["w_br_c"][l], gl["w_o"][l] = dwa, dwb, dwc, dwo
        dm = dm1 + dmg
        cat = lambda r: jnp.concatenate([dm[r], dm[r + 1], dm[r + 2], dm2[r], dm2[r + 1], dm2[r + 2]])
        dmods[l] = jnp.stack([cat(0), cat(3)])

    dmods = jnp.stack(dmods)
    grads = {k: (v if k in LAYERED else jnp.stack(v)) for k, v in gl.items()}
    if comm is None:
        dm16 = jnp.zeros((NL, CC_ROWS, 6 * D), F32).at[:, CTX_ROW].set(dmods[:, 0]).at[:, row].set(dmods[:, 1])
        dwada, dcc = _mod_bwd(cc, w_ada, dm16)
        grads.update(w_ada=dwada, b_ada=dmods[:, 0] + dmods[:, 1], c_ctx=dcc[CTX_ROW])
    else:
        grads.update(comm.adaln_bwd(dmods))
    grads.update(final_norm_g=dgf[0])
    return loss, dX[tc:], grads, early


def _me():
    return lax.axis_index("x"), lax.axis_index("y"), lax.axis_index("c")


def _dev_index(p):
    return 4 * p[0] + 2 * p[1] + p[2]


def _allgather(parts):
    n = len(parts)

    def body(*refs):
        ins, outs = refs[:n], refs[n:2 * n]
        send_sems, recv_sems = refs[2 * n:]
        x, y, c = _me()
        me, sibling = (x, y, c), (x, y, 1 - c)
        chips = [(1 - x, y), (x, 1 - y), (1 - x, 1 - y)]

        def copy(a, k, block, to, src=None):
            dst = outs[a].at[_dev_index(block)]
            return pltpu.make_async_remote_copy(
                src_ref=dst if src is None else src, dst_ref=dst, send_sem=send_sems.at[a, k], recv_sem=recv_sems.at[a, k],
                device_id=to, device_id_type=MESH_ID)

        first, passed = [], []
        for a in range(n):
            first.append(copy(a, 0, me, sibling, src=ins[a]))
            first += [copy(a, 1 + j, me, (*chip, c), src=ins[a]) for j, chip in enumerate(chips)]
        for cp in first:
            cp.start()
        for a in range(n):
            for j, chip in enumerate(chips):
                copy(a, 1 + j, (*chip, c), me).wait_recv()
                passed.append(copy(a, 4 + j, (*chip, c), sibling))
                passed[-1].start()
        for a in range(n):
            copy(a, 0, sibling, me).wait_recv()
            for j, chip in enumerate(chips):
                copy(a, 4 + j, (*chip, 1 - c), me).wait_recv()
        for cp in first + passed:
            cp.wait_send()

    outs = pl.pallas_call(
        body, name="allgather", in_specs=[HBM_SPEC] * n, out_specs=[HBM_SPEC] * n,
        out_shape=[_S((N_DEV,) + p.shape, p.dtype) for p in parts],
        scratch_shapes=[pltpu.SemaphoreType.DMA((n, 7)), pltpu.SemaphoreType.DMA((n, 7))],
    )(*parts)
    return [_with_own(o, p, _dev_index(_me())) for o, p in zip(outs, parts)]


def _with_own(gathered, own, index):
    return lax.dynamic_update_index_in_dim(gathered, own, index, 0)


def _broadcast_small(small, name="small_exchange"):
    def body(in_ref, out_ref, send_sems, recv_sems, local_sem):
        x, y, c = _me()
        my = _dev_index((x, y, c))
        mine = pltpu.make_async_copy(in_ref, out_ref.at[my], local_sem)
        mine.start()
        remote = []
        for k in range(1, N_DEV):
            cp = pltpu.make_async_remote_copy(
                src_ref=in_ref, dst_ref=out_ref.at[my], send_sem=send_sems.at[k - 1], recv_sem=recv_sems.at[k - 1],
                device_id=(x ^ (k >> 2), y ^ ((k >> 1) & 1), c ^ (k & 1)), device_id_type=MESH_ID)
            cp.start()
            remote.append(cp)
        for cp in remote:
            cp.wait_recv()
        for cp in remote:
            cp.wait_send()
        mine.wait()

    return pl.pallas_call(
        body, name=name, in_specs=[HBM_SPEC], out_specs=HBM_SPEC,
        out_shape=_S((N_DEV,) + small.shape, small.dtype),
        scratch_shapes=[pltpu.SemaphoreType.DMA((7,)), pltpu.SemaphoreType.DMA((7,)), pltpu.SemaphoreType.DMA],
    )(small)


def _run_rider(rider, name):
    ni, no = len(rider.ins), len(rider.out_shapes)

    def body(*refs):
        riding = (refs[:ni], refs[ni:ni + no], refs[ni + no:])
        rider.start(*riding)
        rider.wait(*riding)

    return list(pl.pallas_call(
        body, name=name, in_specs=[HBM_SPEC] * ni, out_specs=[HBM_SPEC] * no, out_shape=rider.out_shapes,
        scratch_shapes=rider.sems)(*rider.ins))


def _chip_peers(x, y):
    return [(k - 1, (x ^ (k >> 1), y ^ (k & 1))) for k in range(1, N_CHIP)]


def _pair_exchange(g2s):
    n = len(g2s)

    def copies(ins, outs, sems):
        x, y, c = _me()
        return [pltpu.make_async_remote_copy(
            src_ref=ins[a].at[1 - c, j], dst_ref=outs[a].at[j], send_sem=sems[0].at[a, j], recv_sem=sems[1].at[a, j],
            device_id=(x, y, 1 - c), device_id_type=MESH_ID) for a in range(n) for j in range(N_CHIP)], []

    return _Rider(g2s, [_S(g.shape[1:], g.dtype) for g in g2s],
                  [pltpu.SemaphoreType.DMA((n, N_CHIP)), pltpu.SemaphoreType.DMA((n, N_CHIP))], copies)


def _my_chip():
    x, y, _ = _me()
    return 2 * x + y


def _chip_exchange(s4s):
    n = len(s4s)

    def copies(ins, outs, sems):
        x, y, c = _me()
        my = 2 * x + y
        return [pltpu.make_async_remote_copy(
            src_ref=ins[a].at[2 * px + py], dst_ref=outs[a].at[my], send_sem=sems[0].at[a, k], recv_sem=sems[1].at[a, k],
            device_id=(px, py, c), device_id_type=MESH_ID) for k, (px, py) in _chip_peers(x, y) for a in range(n)], []

    return _Rider(s4s, [_S(s.shape, s.dtype) for s in s4s],
                  [pltpu.SemaphoreType.DMA((n, N_CHIP - 1)), pltpu.SemaphoreType.DMA((n, N_CHIP - 1))], copies)


def _chip_exchange_done(s4s, recvs):
    my = _my_chip()
    return [_with_own(r, lax.dynamic_index_in_dim(s, my, 0, keepdims=False), my) for s, r in zip(s4s, recvs)]


def _chip_gather(arrs):
    n = len(arrs)

    def copies(ins, outs, sems):
        x, y, c = _me()
        return [pltpu.make_async_remote_copy(
            src_ref=ins[a], dst_ref=outs[a].at[2 * x + y], send_sem=sems[0].at[a, k], recv_sem=sems[1].at[a, k],
            device_id=(px, py, c), device_id_type=MESH_ID) for k, (px, py) in _chip_peers(x, y) for a in range(n)], []

    return _Rider(arrs, [_S((N_CHIP,) + a.shape, a.dtype) for a in arrs],
                  [pltpu.SemaphoreType.DMA((n, N_CHIP - 1)), pltpu.SemaphoreType.DMA((n, N_CHIP - 1))], copies)


def _pair_gather(chips):
    n = len(chips)

    def copies(ins, outs, sems):
        x, y, c = _me()
        return [pltpu.make_async_remote_copy(
            src_ref=ins[a].at[j], dst_ref=outs[a].at[j], send_sem=sems[0].at[a, j], recv_sem=sems[1].at[a, j],
            device_id=(x, y, 1 - c), device_id_type=MESH_ID) for a in range(n) for j in range(N_CHIP)], []

    return _Rider(chips, [_S(a.shape, a.dtype) for a in chips],
                  [pltpu.SemaphoreType.DMA((n, N_CHIP)), pltpu.SemaphoreType.DMA((n, N_CHIP))], copies)


def _shard_rows(r):
    return 256 if r % 256 == 0 else r


def _pair_sum(g2, got):
    _, nc, L, R, C = g2.shape
    tr = _shard_rows(R)

    def body(a_ref, b_ref, o_ref):
        o_ref[...] = (a_ref[0] + b_ref[...]).astype(BF16)

    blk = pl.BlockSpec((1, 1, tr, C), lambda j, l, i: (j, l, i, 0))
    return _call(
        body, "pair_sum", (nc, L, R // tr),
        [pl.BlockSpec((1, 1, 1, tr, C), lambda j, l, i: (lax.axis_index("c"), j, l, i, 0)), blk], blk,
        _S(got.shape, BF16))(g2, got)


def _adam(w, g, m, v):
    m2 = ADAM_B1 * m + (1.0 - ADAM_B1) * g
    v2 = ADAM_B2 * v + (1.0 - ADAM_B2) * (g * g)
    m_hat = m2 / (1.0 - ADAM_B1 ** ADAM_STEP)
    v_hat = v2 / (1.0 - ADAM_B2 ** ADAM_STEP)
    return -ADAM_LR * (m_hat / (jnp.sqrt(v_hat) + ADAM_EPS) + ADAM_WD * w), m2, v2


def _sum_adam(recvs, w, m, v):
    L, R, C = w.shape
    tr = _shard_rows(R)
    nr = len(recvs)

    def body(*refs):
        w_ref, m_ref, v_ref, g_ref, d_ref, m2_ref, v2_ref = refs[nr:]
        g = None
        for li, r_ref in enumerate(refs[:nr]):
            s = r_ref[0, 0].astype(F32)
            for j in range(1, N_CHIP):
                s = s + r_ref[j, 0].astype(F32)
            g = s if g is None else jnp.where(pl.program_id(0) == li, s, g)
        g_ref[0] = g
        d_ref[0], m2_ref[0], v2_ref[0] = _adam(w_ref[0], g, m_ref[0], v_ref[0])

    blk = pl.BlockSpec((1, tr, C), lambda l, i: (l, i, 0))
    rspec = pl.BlockSpec((N_CHIP, 1, tr, C), (lambda l, i: (0, l, i, 0)) if nr == 1 else (lambda l, i: (0, 0, i, 0)))
    return _call(body, "sum_adam", (L, R // tr), [rspec] * nr + [blk, blk, blk], [blk] * 4, [_S(w.shape)] * 4)(
        *recvs, w, m, v)


def _adam_big(w, g, m, v):
    L, R, C = w.shape
    tr = _shard_rows(R)

    def body(w_ref, g_ref, m_ref, v_ref, d_ref, m2_ref, v2_ref):
        d_ref[0], m2_ref[0], v2_ref[0] = _adam(w_ref[0], g_ref[0], m_ref[0], v_ref[0])

    blk = pl.BlockSpec((1, tr, C), lambda l, i: (l, i, 0))
    return _call(body, "adam_big", (L, R // tr), [blk] * 4, [blk] * 3, [_S(w.shape)] * 3)(w, g, m, v)


def _sum_small(recv):
    def body(r_ref, o_ref):
        g = r_ref[0]
        for k in range(1, recv.shape[0]):
            g = g + r_ref[k]
        o_ref[...] = g

    return pl.pallas_call(body, name="sum_small", out_shape=_S(recv.shape[1:]))(recv)


def _adam_small(w, g, m, v):
    def body(w_ref, g_ref, m_ref, v_ref, d_ref, m2_ref, v2_ref):
        d_ref[...], m2_ref[...], v2_ref[...] = _adam(w_ref[...], g_ref[...], m_ref[...], v_ref[...])

    return pl.pallas_call(body, name="adam_small", out_shape=[_S(w.shape)] * 3)(w, g, m, v)


def _pack(arrs, dtype, row_mult):
    parts, offs, r = [], [], 0
    for a in arrs:
        nr = -(-a.size // LANES)
        parts.append(jnp.pad(a.reshape(-1).astype(dtype), (0, nr * LANES - a.size)))
        offs.append(r)
        r += nr
    pad = (-r) % row_mult
    if pad:
        parts.append(jnp.zeros((pad * LANES,), dtype))
    return jnp.concatenate(parts).reshape(r + pad, LANES), offs


def _unpack(packed, offs, shapes, lead=()):
    out = []
    for off, shp in zip(offs, shapes):
        size = int(np.prod(shp))
        nr = -(-size // LANES)
        flat = packed[..., off:off + nr, :].reshape(lead + (nr * LANES,))
        out.append(flat[..., :size].reshape(lead + tuple(shp)))
    return out


BIG = (("w_ada", 2), ("w_in", 2), ("w_br_a", 2), ("w_br_b", 2), ("w_br_c", 2), ("w_o", 1), ("w_gu", 2), ("w_down", 1))
CONV = ("dn_conv_w", "sc_conv_w")
REPL = ("c_ctx", "b_ada", "norm1_g", "norm2_g", "dn_a_log", "dn_dt_bias", "dn_norm_g", "pool_w", "pool_scale",
        "final_norm_g")
WEIGHTS = ("c_ctx", "w_ada", "b_ada", "norm1_g", "norm2_g", "w_in", "dn_conv_w", "dn_a_log", "dn_dt_bias", "dn_norm_g",
           "pool_w", "pool_scale", "sc_conv_w", "w_br_a", "w_br_b", "w_br_c", "w_o", "w_gu", "w_down", "final_norm_g")
TOKEN_TILE = 256


def _join(blocks, axis):
    nd, nl, r, c = blocks.shape
    if axis == 2:
        return blocks.transpose(1, 2, 0, 3).reshape(nl, r, nd * c)
    return blocks.transpose(1, 0, 2, 3).reshape(nl, nd * r, c)


def _split(full, axis):
    nl, r, c = full.shape
    if axis == 2:
        return full.reshape(nl, r, N_CHIP, 2, c // N_DEV).transpose(3, 2, 0, 1, 4)
    return full.reshape(nl, N_CHIP, 2, r // N_DEV, c).transpose(2, 1, 0, 3, 4)


PRESPLIT = ("w_in", "w_gu")


def _presplit(layer_grads, names=LAYERED):
    return [g if k in PRESPLIT else _split(g[None], dict(BIG)[k]) for k, g in zip(names, layer_grads)]


class _Comm:
    def __init__(self, late_shards, w_ada, b_ada):
        self.packed = [k for k in LATE if k not in PRESPLIT]
        self.shapes = [late_shards[k].shape for k in self.packed]
        pack, self.offs = _pack([late_shards[k] for k in self.packed], BF16, BF16_ROWS)
        self.late = [pack, late_shards["w_gu"].astype(BF16), late_shards["w_in"][1:].astype(BF16)]
        self.w_ada, self.b_ada = w_ada.astype(BF16), b_ada
        self.g2s = None

    def adaln_fwd(self, c, c_ctx):
        my = _dev_index(_me())
        ncol = self.w_ada.shape[2]
        c_all = _broadcast_small(c.reshape(8, LANES), "c_exchange").reshape(N_DEV, D)
        self.cc = jnp.concatenate([c_all, c_ctx[None, :], jnp.zeros((CC_ROWS - N_DEV - 1, D), F32)], axis=0)
        b_cols = lax.dynamic_slice_in_dim(self.b_ada, my * ncol, ncol, axis=1).reshape(NL, 1, ncol)
        cols = _mod_fwd(self.cc, self.w_ada, b_cols)
        got = _broadcast_small(cols.reshape(-1, LANES), "mods_exchange").reshape(N_DEV, NL, CC_ROWS, ncol)
        return my, got.transpose(1, 2, 0, 3).reshape(NL, CC_ROWS, N_DEV * ncol)

    def adaln_bwd(self, dmods):
        my = _dev_index(_me())
        ncol = self.w_ada.shape[2]
        got = _broadcast_small(dmods.reshape(-1, LANES), "dmods_exchange")
        rows = got.reshape(N_DEV, NL, 2, 6 * D)
        ctx_sum = _sum_small(rows[:, :, 0].reshape(N_DEV, -1, LANES)).reshape(NL, 1, 6 * D)
        db = _sum_small(rows.transpose(0, 2, 1, 3).reshape(2 * N_DEV, -1, LANES)).reshape(NL, 6 * D)
        dm = jnp.concatenate([rows[:, :, 1].transpose(1, 0, 2), ctx_sum,
                              jnp.zeros((NL, CC_ROWS - N_DEV - 1, 6 * D), F32)], axis=1)
        dw, dcc = _mod_bwd(self.cc, self.w_ada, lax.dynamic_slice_in_dim(dm, my * ncol, ncol, axis=2))
        return dict(w_ada=dw, b_ada=db, c_ctx=dcc[CTX_ROW])

    def late_weights_chips(self):
        return _chip_gather(self.late)

    def late_weights_pair(self, riding):
        self.chips = [_with_own(r, a, _my_chip()) for r, a in zip(riding, self.late)]
        return _pair_gather(self.chips)

    def late_weights(self, riding):
        on_south = lax.axis_index("c") == 0
        by_dev = []
        for mine, other in zip(self.chips, riding):
            both = jnp.stack([jnp.where(on_south, mine, other), jnp.where(on_south, other, mine)], axis=1)
            by_dev.append(both.reshape((N_DEV,) + mine.shape[1:]))
        shards = _unpack(by_dev[0], self.offs, self.shapes, (N_DEV,))
        return dict({k: _join(blocks, dict(BIG)[k]) for k, blocks in zip(self.packed, shards)}, w_gu=by_dev[1]), by_dev[2]

    def grad_pair_rider(self, layer_grads, names=LAYERED):
        g2s = _presplit(layer_grads, names)
        self.g2s = (self.g2s or []) + g2s
        return _pair_exchange(g2s)

    def grad_chip_rider(self, got):
        self.sums = [_pair_sum(g2, gt) for g2, gt in zip(self.g2s, got)]
        return _chip_exchange(self.sums)

    def grad_chip_done(self, riding):
        return _chip_exchange_done(self.sums, riding)


def kernel(x, c, ctx, c_ctx, w_ada, b_ada, norm1_g, norm2_g, w_in, dn_conv_w, dn_a_log, dn_dt_bias, dn_norm_g, pool_w, pool_scale, sc_conv_w, w_br_a, w_br_b, w_br_c, w_o, w_gu, w_down, final_norm_g, loss_target, m_c_ctx, m_w_ada, m_b_ada, m_norm1_g, m_norm2_g, m_w_in, m_dn_conv_w, m_dn_a_log, m_dn_dt_bias, m_dn_norm_g, m_pool_w, m_pool_scale, m_sc_conv_w, m_w_br_a, m_w_br_b, m_w_br_c, m_w_o, m_w_gu, m_w_down, m_final_norm_g, v_c_ctx, v_w_ada, v_b_ada, v_norm1_g, v_norm2_g, v_w_in, v_dn_conv_w, v_dn_a_log, v_dn_dt_bias, v_dn_norm_g, v_pool_w, v_pool_scale, v_sc_conv_w, v_w_br_a, v_w_br_b, v_w_br_c, v_w_o, v_w_gu, v_w_down, v_final_norm_g):
    loc = dict(c_ctx=c_ctx, w_ada=w_ada, b_ada=b_ada, norm1_g=norm1_g, norm2_g=norm2_g, w_in=w_in, dn_conv_w=dn_conv_w,
               dn_a_log=dn_a_log, dn_dt_bias=dn_dt_bias, dn_norm_g=dn_norm_g, pool_w=pool_w, pool_scale=pool_scale,
               sc_conv_w=sc_conv_w, w_br_a=w_br_a, w_br_b=w_br_b, w_br_c=w_br_c, w_o=w_o, w_gu=w_gu, w_down=w_down,
               final_norm_g=final_norm_g)
    mom_m = dict(c_ctx=m_c_ctx, w_ada=m_w_ada, b_ada=m_b_ada, norm1_g=m_norm1_g, norm2_g=m_norm2_g, w_in=m_w_in,
                 dn_conv_w=m_dn_conv_w, dn_a_log=m_dn_a_log, dn_dt_bias=m_dn_dt_bias, dn_norm_g=m_dn_norm_g,
                 pool_w=m_pool_w, pool_scale=m_pool_scale, sc_conv_w=m_sc_conv_w, w_br_a=m_w_br_a, w_br_b=m_w_br_b,
                 w_br_c=m_w_br_c, w_o=m_w_o, w_gu=m_w_gu, w_down=m_w_down, final_norm_g=m_final_norm_g)
    mom_v = dict(c_ctx=v_c_ctx, w_ada=v_w_ada, b_ada=v_b_ada, norm1_g=v_norm1_g, norm2_g=v_norm2_g, w_in=v_w_in,
                 dn_conv_w=v_dn_conv_w, dn_a_log=v_dn_a_log, dn_dt_bias=v_dn_dt_bias, dn_norm_g=v_dn_norm_g,
                 pool_w=v_pool_w, pool_scale=v_pool_scale, sc_conv_w=v_sc_conv_w, w_br_a=v_w_br_a, w_br_b=v_w_br_b,
                 w_br_c=v_w_br_c, w_o=v_w_o, w_gu=v_w_gu, w_down=v_w_down, final_norm_g=v_final_norm_g)
    my = _dev_index(_me())

    conv_pack, conv_offs = _pack([loc[k] for k in CONV], F32, 8)
    w_in0_all, conv_all = _allgather([w_in[:1].astype(BF16), conv_pack])
    full = dict({k: loc[k] for k in REPL}, w_in=[w_in0_all])
    for k, blocks in zip(CONV, _unpack(conv_all, conv_offs, [loc[k].shape for k in CONV], (N_DEV,))):
        full[k] = _join(blocks, 2)

    loss8, grad_x, g, recv_early = _device_step(x[0], c, ctx[0], loss_target[0], full, TOKEN_TILE,
                                                comm=_Comm({k: loc[k] for k in LATE + ("w_in",)}, w_ada, b_ada))

    last = [k for k in LAYERED if k not in FFN_W]
    tail = _presplit([g[k][0] for k in last], last)
    got = _run_rider(_pair_exchange(tail), "pair_exchange")
    sums = [_pair_sum(a, b) for a, b in zip(tail, got)]
    recv_tail = _chip_exchange_done(sums, _run_rider(_chip_exchange(sums), "chip_exchange"))
    recv = {k: [None, r] for k, r in zip(LAYERED, recv_early)}
    for k, r in list(zip(FFN_W, recv_early[len(LAYERED):])) + list(zip(last, recv_tail)):
        recv[k][0] = r

    small_names = REPL + CONV
    summed = [k for k in small_names if k != "b_ada"]
    small_pack, small_offs = _pack([g[k] for k in summed] + [loss8[0:1, 0:1]], F32, 8)
    small_sum = _sum_small(_broadcast_small(small_pack))
    sums = _unpack(small_sum, small_offs, [g[k].shape for k in summed] + [(1, 1)])
    grads = dict(zip(summed, sums[:-1]), b_ada=g["b_ada"], w_ada=g["w_ada"])
    loss = sums[-1][0, 0]
    for k in CONV:
        w = loc[k].shape[2]
        grads[k] = lax.dynamic_slice_in_dim(grads[k], my * w, w, axis=2)

    delta, new_m, new_v = {}, {}, {}
    for k in LAYERED:
        grads[k], delta[k], new_m[k], new_v[k] = _sum_adam(recv[k], loc[k], mom_m[k], mom_v[k])
    delta["w_ada"], new_m["w_ada"], new_v["w_ada"] = _adam_big(w_ada, g["w_ada"], m_w_ada, v_w_ada)
    packs = [_pack([src[k] for k in small_names], F32, 8)[0] for src in (loc, grads, mom_m, mom_v)]
    _, offs = _pack([loc[k] for k in small_names], F32, 8)
    shapes = [loc[k].shape for k in small_names]
    for dst, packed in zip((delta, new_m, new_v), _adam_small(*packs)):
        dst.update(zip(small_names, _unpack(packed, offs, shapes)))

    return (loss, grad_x[None], *[grads[k] for k in WEIGHTS], *[delta[k] for k in WEIGHTS],
            *[new_m[k] for k in WEIGHTS], *[new_v[k] for k in WEIGHTS])
```

```python
import numpy as np
import jax
import jax.numpy as jnp
from jax import lax
from jax.experimental import pallas as pl
from jax.experimental.pallas import tpu as pltpu

F32 = jnp.float32
BF16 = jnp.bfloat16
HI = lax.Precision.HIGHEST

D = 1024
NL = 2
NH = 4
DH = 128
DN = NH * DH
CH = 64
GW = 64
PW = 256
DFF = 2816
EPS = 1e-6
N_DEV = 8
N_CHIP = 4
MESH_ID = pl.DeviceIdType.MESH
HBM_SPEC = pl.BlockSpec(memory_space=pltpu.HBM)
LANES = 128
BF16_ROWS = 16
VMEM_MB = 56
DW_ACC_MB = 12

ADAM_LR, ADAM_B1, ADAM_B2, ADAM_EPS, ADAM_WD, ADAM_STEP = 0.001, 0.9, 0.999, 1e-08, 0.01, 10

IN_BOUNDS = (0, 1536, 2048, 2064, 2320, 2576, 2832, 3088, 6160)
IN_WIDTHS = (1536, 512, 128, 256, 256, 256, 256, 3072)
POOL_WIN = ((1, 0), (2, 1), (4, 3), (8, 7))

NN = ((1,), (0,))
NT = ((1,), (1,))
TN = ((0,), (0,))


def _dot(a, b, dims, hi=False):
    if hi:
        prec = lax.Precision.HIGH if hi == "x3" else HI
        return lax.dot_general(a, b, (dims, ((), ())), precision=prec, preferred_element_type=F32)
    return lax.dot_general(a.astype(BF16), b.astype(BF16), (dims, ((), ())), preferred_element_type=F32)


def _S(shape, dtype=F32):
    return jax.ShapeDtypeStruct(tuple(shape), dtype)


def _full(shape):
    nd = len(shape)
    return pl.BlockSpec(tuple(shape), lambda *_: (0,) * nd)


def _rows(tt, w):
    return pl.BlockSpec((tt, w), lambda i: (i, 0))


class _Rider:
    def __init__(self, ins, out_shapes, sems, copies):
        self.ins, self.out_shapes, self.sems, self.copies = list(ins), list(out_shapes), list(sems), copies

    def start(self, ins, outs, sems):
        remote, local = self.copies(ins, outs, sems)
        for cp in local + remote:
            cp.start()

    def wait(self, ins, outs, sems):
        remote, local = self.copies(ins, outs, sems)
        for cp in remote:
            cp.wait_recv()
        for cp in remote:
            cp.wait_send()
        for cp in local:
            cp.wait()


def _call(body, name, grid, in_specs, out_specs, out_shape, scratch=(), rider=None):
    params = pltpu.CompilerParams(dimension_semantics=("arbitrary",) * len(grid), vmem_limit_bytes=VMEM_MB << 20)
    if rider is None:
        return pl.pallas_call(body, name=name, grid=grid, in_specs=in_specs, out_specs=out_specs, out_shape=out_shape,
                              scratch_shapes=list(scratch), compiler_params=params)
    single = not isinstance(out_shape, (list, tuple))
    out_specs, out_shape = ([out_specs], [out_shape]) if single else (list(out_specs), list(out_shape))
    n_in, n_out, n_scr = len(in_specs), len(out_shape), len(scratch)
    r_in, r_out = len(rider.ins), len(rider.out_shapes)

    def hosted(*refs):
        ins, refs = refs[:n_in + r_in], refs[n_in + r_in:]
        outs, scr = refs[:n_out + r_out], refs[n_out + r_out:]
        riding = (ins[n_in:], outs[n_out:], scr[n_scr:])

        @pl.when(pl.program_id(0) == 0)
        def _():
            rider.start(*riding)

        body(*ins[:n_in], *outs[:n_out], *scr[:n_scr])

        @pl.when(pl.program_id(0) == grid[0] - 1)
        def _():
            rider.wait(*riding)

    call = pl.pallas_call(
        hosted, name=name, grid=grid, in_specs=list(in_specs) + [HBM_SPEC] * r_in,
        out_specs=out_specs + [HBM_SPEC] * r_out, out_shape=out_shape + rider.out_shapes,
        scratch_shapes=list(scratch) + rider.sems, compiler_params=params)

    def run(*args):
        res = call(*args, *rider.ins)
        own = res[:n_out]
        return (own[0] if single else own), list(res[n_out:])

    return run


def _iota(shape, axis):
    return lax.broadcasted_iota(jnp.int32, shape, axis)


def _colsum(a):
    return jnp.sum(a, axis=0, keepdims=True)


def _silu(x):
    return x * jax.nn.sigmoid(x)


def _modulate(x, g, sh, sc):
    xn = x * lax.rsqrt(jnp.mean(x * x, axis=-1, keepdims=True) + EPS)
    return (xn * g) * (1.0 + sc) + sh


def _stream_rows(mv_ref, i, tt, tc, k):
    isc = (i * tt + _iota((tt, 1), 0)) < tc
    return isc, jnp.where(isc, mv_ref[k:k + 1, :], mv_ref[3 + k:4 + k, :])


def _acc_stream(ref, k, isc, val):
    ref[k:k + 1, :] += _colsum(jnp.where(isc, val, 0.0))
    ref[3 + k:4 + k, :] += _colsum(jnp.where(isc, 0.0, val))


CC_ROWS = 16
CTX_ROW = 8


def _mod_cols(n):
    return 1536 if n % 1536 == 0 else n


def _mod_fwd(cc, w_ada, b_ada3):
    n = w_ada.shape[2]
    ct = _mod_cols(n)

    def body(cc_ref, w_ref, b_ref, o_ref):
        o_ref[0] = _dot(_silu(cc_ref[...]), w_ref[0], NN) + b_ref[0]

    return _call(
        body, "mod_fwd", (NL, n // ct),
        [pl.BlockSpec((CC_ROWS, D), lambda l, j: (0, 0)), pl.BlockSpec((1, D, ct), lambda l, j: (l, 0, j)),
         pl.BlockSpec((1, 1, ct), lambda l, j: (l, 0, j))],
        pl.BlockSpec((1, CC_ROWS, ct), lambda l, j: (l, 0, j)), _S((NL, CC_ROWS, n)))(cc, w_ada, b_ada3)


def _mod_bwd(cc, w_ada, dmods):
    n = w_ada.shape[2]
    ct = _mod_cols(n)

    def body(cc_ref, w_ref, dm_ref, dw_ref, dcc_ref):
        first = (pl.program_id(0) == 0) & (pl.program_id(1) == 0)
        cc_ = cc_ref[...]
        sg = jax.nn.sigmoid(cc_)
        dm = dm_ref[0]
        dw_ref[0] = _dot(cc_ * sg, dm, TN)

        @pl.when(first)
        def _():
            dcc_ref[...] = jnp.zeros_like(dcc_ref)

        dcc_ref[...] += _dot(dm, w_ref[0], NT) * (sg * (1.0 + cc_ * (1.0 - sg)))

    return _call(
        body, "mod_bwd", (NL, n // ct),
        [pl.BlockSpec((CC_ROWS, D), lambda l, j: (0, 0)), pl.BlockSpec((1, D, ct), lambda l, j: (l, 0, j)),
         pl.BlockSpec((1, CC_ROWS, ct), lambda l, j: (l, 0, j))],
        [pl.BlockSpec((1, D, ct), lambda l, j: (l, 0, j)), pl.BlockSpec((CC_ROWS, D), lambda l, j: (0, 0))],
        [_S((NL, D, n)), _S((CC_ROWS, D))])(cc, w_ada, dmods)


def _cols(n, tt):
    return pl.BlockSpec((n, tt), lambda i: (0, i))


def _inproj_fwd(X, mv, g, ws, tc, tt):
    T = X.shape[0]
    nw = len(ws)

    def body(x_ref, mv_ref, g_ref, *refs):
        w_refs, ht_ref, p_refs = refs[:nw], refs[nw], refs[nw + 1:]
        i = pl.program_id(0)
        _, sh = _stream_rows(mv_ref, i, tt, tc, 0)
        _, sc = _stream_rows(mv_ref, i, tt, tc, 1)
        h = _modulate(x_ref[...], g_ref[...], sh, sc)
        ht_ref[...] = h.T.astype(BF16)
        hb = h.astype(BF16)
        for w_ref, p_ref in zip(w_refs, p_refs):
            p_ref[...] = jnp.dot(hb, w_ref[...], preferred_element_type=F32)

    return _call(
        body, "inproj_fwd", (T // tt,),
        [_rows(tt, D), _full((8, D)), _full((1, D))] + [_full(w.shape) for w in ws],
        [_cols(D, tt)] + [_rows(tt, w.shape[1]) for w in ws],
        [_S((D, T), BF16)] + [_S((T, w.shape[1])) for w in ws])(X, mv, g, *ws)


def _inproj_bwd(X, mv, g, ws, dps, dp_w, dres, tc, tt):
    T = X.shape[0]
    nw, nd = len(ws), len(dps)

    def body(x_ref, mv_ref, g_ref, dres_ref, *refs):
        w_refs, dp_refs = refs[:nw], refs[nw:nw + nd]
        dx_ref, dg_ref, dm_ref = refs[nw + nd:]
        i = pl.program_id(0)
        isc, sh = _stream_rows(mv_ref, i, tt, tc, 0)
        _, sc = _stream_rows(mv_ref, i, tt, tc, 1)
        dh = None
        for dp_ref, k in zip(dp_refs, dp_w):
            t = _dot(dp_ref[...], w_refs[k][...], NT)
            dh = t if dh is None else dh + t
        _, vjp = jax.vjp(_modulate, x_ref[...], g_ref[...], sh, sc)
        dx, dg, dsh, dsc = vjp(dh)
        dx_ref[...] = dres_ref[...] + dx

        @pl.when(i == 0)
        def _():
            dg_ref[...] = jnp.zeros_like(dg_ref)
            dm_ref[...] = jnp.zeros_like(dm_ref)

        dg_ref[...] += dg
        _acc_stream(dm_ref, 0, isc, dsh)
        _acc_stream(dm_ref, 1, isc, dsc)

    return _call(
        body, "inproj_bwd", (T // tt,),
        [_rows(tt, D), _full((8, D)), _full((1, D)), _rows(tt, D)] + [_full(w.shape) for w in ws]
        + [_rows(tt, dp.shape[1]) for dp in dps],
        [_rows(tt, D), _full((1, D)), _full((8, D))],
        [_S((T, D)), _S((1, D)), _S((8, D))])(X, mv, g, dres, *ws, *dps)


def _dw(At, B, tt):
    K, T = At.shape
    N = B.shape[1]
    tt = 3 * tt if T % (3 * tt) == 0 else tt
    tn = max(t for t in range(LANES, N + 1, LANES) if N % t == 0 and K * t * 4 <= DW_ACC_MB << 20)

    def body(a_ref, b_ref, o_ref):
        @pl.when(pl.program_id(1) == 0)
        def _():
            o_ref[...] = jnp.zeros_like(o_ref)

        o_ref[...] += _dot(a_ref[...], b_ref[...], NN)

    return _call(
        body, "dw", (N // tn, T // tt),
        [pl.BlockSpec((K, tt), lambda j, i: (0, i)), pl.BlockSpec((tt, tn), lambda j, i: (i, j))],
        pl.BlockSpec((K, tn), lambda j, i: (0, j)), _S((K, N)))(At, B)


def _halo_specs(T, tt, cw, col):
    r8, nb8 = tt // 8, T // 8
    return [pl.BlockSpec((tt, cw), lambda j, i: (i, col(j))),
            pl.BlockSpec((8, cw), lambda j, i: (jnp.maximum(i * r8 - 1, 0), col(j))),
            pl.BlockSpec((8, cw), lambda j, i: (jnp.minimum((i + 1) * r8, nb8 - 1), col(j)))]


def _shifts(a, prev8, next8, i, tt, tc, T):
    r = _iota((tt, 1), 0)
    t = i * tt + r
    dn = jnp.where(r == 0, prev8[7:8, :], pltpu.roll(a, 1, 0))
    dn = jnp.where((t == 0) | (t == tc), 0.0, dn)
    up = jnp.where(r == tt - 1, next8[0:1, :], pltpu.roll(a, tt - 1, 0))
    up = jnp.where((t == T - 1) | (t == tc - 1), 0.0, up)
    return dn, up


def _dn_post(y, part):
    a = _silu(y)
    nrm = lax.rsqrt(jnp.sum(a * a, axis=-1, keepdims=True) + EPS)
    f = jnp.where(part == 0, nrm * (DH ** -0.5), jnp.where(part == 1, nrm, 1.0))
    return a * f


def _conv3(w_ref, dn, mid, up):
    return w_ref[0:1, :] * dn + w_ref[1:2, :] * mid + w_ref[2:3, :] * up


def _dnprep_fwd(pq, cw, tc, tt):
    T = pq.shape[0]

    def body(p_ref, pp_ref, pn_ref, w_ref, a_ref):
        part, i = pl.program_id(0), pl.program_id(1)
        p = p_ref[...]
        dn, up = _shifts(p, pp_ref[...], pn_ref[...], i, tt, tc, T)
        y = _conv3(w_ref, dn, p, up)
        for h in range(NH):
            a_ref[:, _hs(h)] = _dn_post(y[:, _hs(h)], part)

    return _call(
        body, "dnprep_fwd", (3, T // tt),
        _halo_specs(T, tt, DN, lambda j: j) + [pl.BlockSpec((3, DN), lambda j, i: (0, j))],
        pl.BlockSpec((tt, DN), lambda j, i: (i, j)), _S((T, 3 * DN)))(pq, pq, pq, cw)


def _dnprep_bwd_act(pq, cw, da_f, da_b, tc, tt):
    T = pq.shape[0]

    def body(p_ref, pp_ref, pn_ref, w_ref, df_ref, db_ref, dy_ref):
        part, i = pl.program_id(0), pl.program_id(1)
        p = p_ref[...]
        dn, up = _shifts(p, pp_ref[...], pn_ref[...], i, tt, tc, T)
        y = _conv3(w_ref, dn, p, up)
        for h in range(NH):
            _, vjp = jax.vjp(lambda yh: _dn_post(yh, part), y[:, _hs(h)])
            dy_ref[:, _hs(h)] = vjp(df_ref[:, _hs(h)] + db_ref[:, _hs(h)])[0]

    blk = pl.BlockSpec((tt, DN), lambda j, i: (i, j))
    return _call(
        body, "dnprep_bwd_act", (3, T // tt),
        _halo_specs(T, tt, DN, lambda j: j) + [pl.BlockSpec((3, DN), lambda j, i: (0, j)), blk, blk],
        blk, _S((T, 3 * DN)))(pq, pq, pq, cw, da_f, da_b)


def _conv_bwd(dy, p, cw, tc, tt):
    T, W = p.shape
    cb = DN

    def body(dy_ref, dyp_ref, dyn_ref, p_ref, pp_ref, pn_ref, w_ref, dp_ref, dw_ref):
        i = pl.program_id(1)
        dy, p_ = dy_ref[...], p_ref[...]
        ddn, dup = _shifts(dy, dyp_ref[...], dyn_ref[...], i, tt, tc, T)
        dp_ref[...] = _conv3(w_ref, dup, dy, ddn)
        pdn, pup = _shifts(p_, pp_ref[...], pn_ref[...], i, tt, tc, T)

        @pl.when(i == 0)
        def _():
            dw_ref[...] = jnp.zeros_like(dw_ref)

        dw_ref[0:1, :] += _colsum(dy * pdn)
        dw_ref[1:2, :] += _colsum(dy * p_)
        dw_ref[2:3, :] += _colsum(dy * pup)

    wspec = pl.BlockSpec((3, cb), lambda j, i: (0, j))
    return _call(
        body, "conv_bwd", (W // cb, T // tt),
        _halo_specs(T, tt, cb, lambda j: j) * 2 + [wspec],
        [pl.BlockSpec((tt, cb), lambda j, i: (i, j)), wspec], [_S((T, W)), _S((3, W))])(dy, dy, dy, p, p, p, cw)


def _sc_fwd(sx, sb, sc_, cw, tc, tt):
    T = sx.shape[0]

    def body(x_ref, xp_ref, xn_ref, c_ref, cp_ref, cn_ref, b_ref, w_ref, y_ref):
        i = pl.program_id(1)
        u = c_ref[...] * x_ref[...]
        dn, up = _shifts(u, cp_ref[...] * xp_ref[...], cn_ref[...] * xn_ref[...], i, tt, tc, T)
        y_ref[...] = b_ref[...] * _conv3(w_ref, dn, u, up)

    blk = pl.BlockSpec((tt, LANES), lambda j, i: (i, j))
    return _call(
        body, "sc_fwd", (PW // LANES, T // tt),
        _halo_specs(T, tt, LANES, lambda j: j) * 2 + [blk, pl.BlockSpec((3, LANES), lambda j, i: (0, j))],
        blk, _S((T, PW)))(sx, sx, sx, sc_, sc_, sc_, sb, cw)


def _sc_bwd(sx, sb, sc_, cw, dy, tc, tt):
    T = sx.shape[0]

    def body(x_ref, xp_ref, xn_ref, c_ref, cp_ref, cn_ref, b_ref, bp_ref, bn_ref, dy_ref, dyp_ref, dyn_ref, w_ref,
             dx_ref, db_ref, dc_ref, dw_ref):
        i = pl.program_id(1)
        x, c, dy_ = x_ref[...], c_ref[...], dy_ref[...]
        u = c * x
        udn, uup = _shifts(u, cp_ref[...] * xp_ref[...], cn_ref[...] * xn_ref[...], i, tt, tc, T)
        db_ref[...] = dy_ * _conv3(w_ref, udn, u, uup)
        e = dy_ * b_ref[...]
        edn, eup = _shifts(e, dyp_ref[...] * bp_ref[...], dyn_ref[...] * bn_ref[...], i, tt, tc, T)
        du = _conv3(w_ref, eup, e, edn)
        dx_ref[...] = du * c
        dc_ref[...] = du * x

        @pl.when(i == 0)
        def _():
            dw_ref[...] = jnp.zeros_like(dw_ref)

        dw_ref[0:1, :] += _colsum(e * udn)
        dw_ref[1:2, :] += _colsum(e * u)
        dw_ref[2:3, :] += _colsum(e * uup)

    blk = pl.BlockSpec((tt, LANES), lambda j, i: (i, j))
    wspec = pl.BlockSpec((3, LANES), lambda j, i: (0, j))
    return _call(
        body, "sc_bwd", (PW // LANES, T // tt),
        _halo_specs(T, tt, LANES, lambda j: j) * 4 + [wspec],
        [blk, blk, blk, wspec], [_S((T, PW))] * 3 + [_S((3, PW))])(
            sx, sx, sx, sc_, sc_, sc_, sb, sb, sb, dy, dy, dy, cw)


def _group_select(vals):
    g = _iota((1, PW), 1) // (PW // len(POOL_WIN))
    return jnp.where(g == 0, vals[0], jnp.where(g == 1, vals[1], jnp.where(g == 2, vals[2], vals[3])))


def _nested_box(get, mirror):
    acc, outs, pl_, ph_ = get(0), [], 0, 0
    for lo, hi in POOL_WIN:
        if mirror:
            lo, hi = hi, lo
        for k in range(pl_ + 1, lo + 1):
            acc = acc + get(-k)
        for k in range(ph_ + 1, hi + 1):
            acc = acc + get(k)
        pl_, ph_ = lo, hi
        outs.append(acc)
    return _group_select(outs)


def _box_tokens(a, n, mirror):
    idx = _iota((n, 1), 0)

    def get(k):
        if k == 0:
            return a
        return jnp.where((idx + k >= 0) & (idx + k < n), pltpu.roll(a, (-k) % n, 0), 0.0)

    return _nested_box(get, mirror)


def _inv_count(pos, n):
    return _group_select([1.0 / (jnp.minimum(pos + hi, n - 1) - jnp.maximum(pos - lo, 0) + 1).astype(F32)
                          for lo, hi in POOL_WIN])


def _pool_rows(ref, r, R, tc, mirror):
    def get(k):
        rr = r + k
        rc = jnp.clip(rr, 0, R - 1)
        v = ref[pl.ds(pl.multiple_of(tc + rc * GW, GW), GW), :]
        if mirror:
            v = v * _inv_count(jnp.full((1, PW), rc, jnp.int32), R)
        return jnp.where((rr >= 0) & (rr < R), v, 0.0)

    return _nested_box(get, mirror)


def _pool_fwd(u, pwbd, ps, tc):
    T = u.shape[0]
    R = (T - tc) // GW

    def body(u_ref, pw_ref, ps_ref, y_ref):
        pw, scale = pw_ref[...], ps_ref[...]
        uc = u_ref[0:tc, :]
        mc = _box_tokens(uc, tc, False) * _inv_count(_iota((tc, 1), 0), tc)
        y_ref[0:tc, :] = _dot(mc - uc, pw, NN) * scale
        inv_c = _inv_count(_iota((GW, 1), 0), GW)

        def row(r, carry):
            rs = _pool_rows(u_ref, r, R, tc, False) * _inv_count(jnp.full((1, PW), r, jnp.int32), R)
            m = _box_tokens(rs, GW, False) * inv_c
            sl = pl.ds(pl.multiple_of(tc + r * GW, GW), GW)
            y_ref[sl, :] = _dot(m - u_ref[sl, :], pw, NN) * scale
            return carry

        lax.fori_loop(0, R, row, 0)

    return pl.pallas_call(
        body, name="pool_fwd", out_shape=_S((T, PW)),
        compiler_params=pltpu.CompilerParams(vmem_limit_bytes=VMEM_MB << 20))(u, pwbd, ps)


def _pool_bwd(u, pwbd, ps, dy, tc):
    T = u.shape[0]
    R = (T - tc) // GW

    def body(u_ref, pw_ref, ps_ref, dy_ref, du_ref, dpw_ref, dps_ref, dd_ref):
        pw, scale = pw_ref[...], ps_ref[...]
        dpw_ref[...] = jnp.zeros_like(dpw_ref)
        dps_ref[...] = jnp.zeros_like(dps_ref)

        def back(d, dy_):
            dz = dy_ * scale
            dpw_ref[...] += _dot(d, dz, TN)
            dps_ref[...] += _colsum(dy_ * _dot(d, pw, NN))
            return _dot(dz, pw, NT)

        uc = u_ref[0:tc, :]
        inv_cc = _inv_count(_iota((tc, 1), 0), tc)
        ddc = back(_box_tokens(uc, tc, False) * inv_cc - uc, dy_ref[0:tc, :])
        du_ref[0:tc, :] = _box_tokens(ddc * inv_cc, tc, True) - ddc
        inv_c = _inv_count(_iota((GW, 1), 0), GW)

        def row1(r, carry):
            rs = _pool_rows(u_ref, r, R, tc, False) * _inv_count(jnp.full((1, PW), r, jnp.int32), R)
            m = _box_tokens(rs, GW, False) * inv_c
            sl = pl.ds(pl.multiple_of(tc + r * GW, GW), GW)
            dd_ref[sl, :] = back(m - u_ref[sl, :], dy_ref[sl, :])
            return carry

        lax.fori_loop(0, R, row1, 0)

        def row2(r, carry):
            t1 = _pool_rows(dd_ref, r, R, tc, True)
            sl = pl.ds(pl.multiple_of(tc + r * GW, GW), GW)
            du_ref[sl, :] = _box_tokens(t1 * inv_c, GW, True) - dd_ref[sl, :]
            return carry

        lax.fori_loop(0, R, row2, 0)

    return pl.pallas_call(
        body, name="pool_bwd", out_shape=[_S((T, PW)), _S((PW, PW)), _S((1, PW))],
        scratch_shapes=[pltpu.VMEM((T, PW), F32)],
        compiler_params=pltpu.CompilerParams(vmem_limit_bytes=VMEM_MB << 20))(u, pwbd, ps, dy)


def _scan_consts():
    i = np.arange(CH)
    lower = (i[:, None] >= i[None, :]).astype(np.float32)
    return jnp.asarray(np.stack([lower, lower.T])), jnp.asarray(np.stack([lower.T, lower]))


def _gates(pab, al, dtb, csum):
    sp_in = pab + dtb
    sp = jnp.maximum(sp_in, 0.0) + jnp.log(1.0 + jnp.exp(-jnp.abs(sp_in)))
    nexp = -jnp.exp(al)
    gm = nexp * sp
    return gm, jax.nn.sigmoid(pab), _dot(csum, gm, NN, hi=True), sp_in, nexp


def _lane_col(m, j):
    return jnp.sum(jnp.where(_iota(m.shape, 1) == j, m, 0.0), axis=1, keepdims=True)


def _hs(h):
    return slice(h * DH, (h + 1) * DH)


HS = NH * CH
X3 = "x3"


def _stack(x, base=0):
    return jnp.concatenate([x[:, base + h * DH:base + (h + 1) * DH] for h in range(NH)], axis=0)


def _heads(st):
    return [st[h * CH:(h + 1) * CH] for h in range(NH)]


def _rowsum(a):
    return jnp.sum(a, axis=1, keepdims=True)


def _row_of(col):
    e0 = (_iota((8, LANES), 1) == 0).astype(F32)
    return _dot(e0, jnp.broadcast_to(col, (HS, LANES)), NT, hi=True)[0:1, :]


def _inverses(nms):
    eye = (_iota((HS, HS), 0) == _iota((HS, HS), 1)).astype(F32)
    x0s, mps = [eye + nm for nm in nms], list(nms)
    for _ in range(5):
        mps = [_dot(mp, mp, NN) for mp in mps]
        x0s = [x0 + _dot(x0, mp, NN) for x0, mp in zip(x0s, mps)]
    rs = [eye - _dot(eye - nm, x0, NN, hi=X3) for nm, x0 in zip(nms, x0s)]
    return [x0 + _dot(x0, r, NN) for x0, r in zip(x0s, rs)]


def _dn_chunk_pre(qkv, pab, al, dtb, csum_d, d):
    gm, bm, gcm, sp_in, nexp = _gates(pab, al, dtb, csum_d)
    gc = jnp.concatenate([_lane_col(gcm, d * NH + h) for h in range(NH)], axis=0)
    beta = jnp.concatenate([_lane_col(bm, 8 + d * NH + h) for h in range(NH)], axis=0)
    q, k, v = _stack(qkv, 0), _stack(qkv, DN), _stack(qkv, 2 * DN)
    ii, jj = _iota((HS, HS), 0), _iota((HS, HS), 1)
    sh = CH.bit_length() - 1
    same = (ii >> sh) == (jj >> sh)
    incl = same & ((ii >= jj) if d == 0 else (ii <= jj))
    strict = same & ((ii > jj) if d == 0 else (ii < jj))
    Di = jnp.where(incl, jnp.exp(jnp.where(incl, gc - _row_of(gc), 0.0)), 0.0)
    Ds = jnp.where(strict, Di, 0.0)
    kb = k * beta
    kk = _dot(kb, k, NT)
    return dict(q=q, k=k, v=v, beta=beta, gc=gc, gm=gm, bm=bm, sp_in=sp_in, nexp=nexp, Di=Di, Ds=Ds, strict=strict,
                last=CH - 1 if d == 0 else 0, kb=kb, kk=kk)


def _dn_chunk_post(c, tm, uw=None):
    q, k, v, beta, gc, kb, last = (c[n] for n in ("q", "k", "v", "beta", "gc", "kb", "last"))
    E = jnp.exp(gc)
    gls = [gc[h * CH + last:h * CH + last + 1, :] for h in range(NH)]
    xs = jnp.exp(jnp.concatenate([jnp.broadcast_to(g, (CH, 1)) for g in gls], axis=0) - gc)
    qk = _dot(q, k, NT)
    if uw is None:
        both = _dot(tm, jnp.concatenate([v * beta, kb * E], axis=1), NN, hi=X3)
        uw = both[:, :DH], both[:, DH:]
    u, w = uw
    return dict(c, tm=tm, E=E, gls=gls, xs=xs, qk=qk, u=u, w=w, ks=k * xs, qd=q * E, aqk=qk * c["Di"])


def _dn_chunks_bwd_math(cs, Ss, dS2s, dos, vns, dvns):
    I = range(len(cs))
    q, k, v, beta, tm, E, xs, kb, u, w = ([c[n] for c in cs] for n in ("q", "k", "v", "beta", "tm", "E", "xs", "kb", "u", "w"))
    doh, vnh, dvnh = ([_heads(a) for a in l] for l in (dos, vns, dvns))
    cat = lambda parts: jnp.concatenate(parts, axis=0)
    dqd = [cat([_dot(doh[i][h], Ss[i][h], NT) for h in range(NH)]) for i in I]
    dks = [cat([_dot(vnh[i][h], dS2s[i][h], NT) for h in range(NH)]) for i in I]
    dw = [-cat([_dot(dvnh[i][h], Ss[i][h], NT) for h in range(NH)]) for i in I]
    daqk = [_dot(dos[i], vns[i], NT) for i in I]
    drbw = [_dot(tm[i], jnp.concatenate([dvns[i], dw[i]], axis=1), TN, hi=X3) for i in I]
    drb, drw = [x[:, :DH] for x in drbw], [x[:, DH:] for x in drbw]
    uw = [jnp.concatenate([u[i], w[i].astype(F32)], axis=1) for i in I]
    dA = [jnp.where(cs[i]["strict"], -_dot(drbw[i], uw[i], NT), 0.0) for i in I]
    dM1 = [dA[i] * cs[i]["Ds"] for i in I]
    dM2 = [daqk[i] * cs[i]["Di"] for i in I]
    dkb = [_dot(dM1[i], k[i], NN) + drw[i] * E[i] for i in I]
    dk = [_dot(dM1[i], kb[i], TN) + _dot(dM2[i], q[i], TN) + dks[i] * xs[i] for i in I]
    dq = [_dot(dM2[i], k[i], NN) + dqd[i] * E[i] for i in I]
    on_diag = _iota((HS, HS), 0) == _iota((HS, HS), 1)
    out = []
    for i in I:
        G = dM1[i] * cs[i]["kk"] + dM2[i] * cs[i]["qk"]
        col = _rowsum(jnp.where(on_diag, jnp.broadcast_to(_colsum(G), (HS, HS)), 0.0))
        dxx = _rowsum(dks[i] * k[i]) * xs[i]
        dgc = _rowsum(G) - col + (_rowsum(dqd[i] * q[i]) + _rowsum(drw[i] * kb[i])) * E[i] - dxx
        at_last = _iota((CH, 1), 0) == cs[i]["last"]
        ends = []
        for h in range(NH):
            dgl = (_colsum(_rowsum(Ss[i][h] * dS2s[i][h])) * jnp.exp(cs[i]["gls"][h])
                   + _colsum(dxx[h * CH:(h + 1) * CH]))
            ends.append(jnp.where(at_last, dgl, 0.0))
        dbeta = _rowsum(drb[i] * v[i]) + _rowsum(dkb[i] * k[i])
        out.append((dq[i], dk[i] + dkb[i] * beta[i], drb[i] * beta[i], dgc + cat(ends), dbeta))
    return out


def _chunk_group(n, want=2):
    g = want
    while n % g:
        g //= 2
    return g


def _dn_chunks_fwd(qkv, pab, alr, dtr, rider=None):
    T = qkv.shape[0]
    n = T // CH
    G = _chunk_group(n, 4)
    csum, _ = _scan_consts()

    def body(q_ref, p_ref, cs_ref, al_ref, dt_ref, *outs):
        inst = [(g, d) for g in range(G) for d in range(2)]
        pres = [_dn_chunk_pre(q_ref[g * CH:(g + 1) * CH, :], p_ref[g * CH:(g + 1) * CH, :], al_ref[...], dt_ref[...],
                              cs_ref[d], d) for g, d in inst]
        tms = _inverses([-(p["kk"] * p["Ds"]) for p in pres])
        for (g, d), pre, tm in zip(inst, pres, tms):
            rows = slice(g * HS, (g + 1) * HS)
            u_ref, w_ref, ks_ref, qd_ref, aqk_ref, eg_ref, tm_ref = outs[7 * d:7 * d + 7]
            c = _dn_chunk_post(pre, tm)
            tm_ref[rows, :] = tm
            u_ref[rows, :] = c["u"]
            w_ref[rows, :] = c["w"].astype(BF16)
            ks_ref[rows, :] = c["ks"].astype(BF16)
            qd_ref[rows, :] = c["qd"].astype(BF16)
            aqk_ref[rows, :] = c["aqk"].astype(BF16)
            egs = [jnp.broadcast_to(jnp.exp(gl), (1, LANES)) for gl in c["gls"]]
            eg_ref[g * 8:(g + 1) * 8, :] = jnp.concatenate(egs + [jnp.zeros((8 - NH, LANES), F32)], axis=0)

    st = lambda w_: pl.BlockSpec((G * HS, w_), lambda i: (i, 0))
    one = [st(DH)] * 4 + [st(HS), pl.BlockSpec((G * 8, LANES), lambda i: (i, 0)), st(HS)]
    shp = [_S((n * HS, DH)), _S((n * HS, DH), BF16), _S((n * HS, DH), BF16), _S((n * HS, DH), BF16),
           _S((n * HS, HS), BF16), _S((n * 8, LANES)), _S((n * HS, HS))]
    res = _call(
        body, "dn_chunks_fwd", (n // G,),
        [_rows(G * CH, 3 * DN), _rows(G * CH, LANES), _full((2, CH, CH)), _full((1, LANES)), _full((1, LANES))],
        one * 2, shp * 2, rider=rider)(qkv, pab, csum, alr, dtr)
    outs, riding = (res, None) if rider is None else res
    parts = tuple(outs[:7]), tuple(outs[7:])
    return parts if rider is None else (parts, riding)


def _scan_plan(n, ncx):
    sg = 2 if n % 2 == 0 and ncx % 2 == 0 else 1
    ng, ncg = n // sg, ncx // sg
    return sg, ((lambda i: i), (lambda i: jnp.where(i < ncg, ncg - 1 - i, ng - 1 - (i - ncg))))


def _scan_specs(order, sg):
    st = lambda w_: pl.BlockSpec((sg * HS, w_), lambda i: (order(i), 0))
    return dict(st=st(DH), aqk=st(HS), eg=pl.BlockSpec((sg * 8, LANES), lambda i: (order(i), 0)),
                tok=pl.BlockSpec((sg * CH, DN), lambda i: (order(i), 0)),
                state=pl.BlockSpec((sg, DN, DH), lambda i: (order(i), 0, 0)))


def _scan_fwd(parts, T, tc, rider=None):
    n = T // CH
    sg, orders = _scan_plan(n, tc // CH)

    def body(*refs):
        S_f, S_b = refs[-2:]

        @pl.when(pl.program_id(0) == 0)
        def _():
            S_f[...] = jnp.zeros_like(S_f)
            S_b[...] = jnp.zeros_like(S_b)

        for g in range(sg):
            for d, S in enumerate((S_f, S_b)):
                u_ref, w_ref, ks_ref, qd_ref, aqk_ref, eg_ref = refs[6 * d:6 * d + 6]
                o_ref, ss_ref, vn_ref = refs[12 + 3 * d:15 + 3 * d]
                k = g if d == 0 else sg - 1 - g
                rows = slice(k * HS, (k + 1) * HS)
                ss_ref[k] = S[...]
                Sh = [S[_hs(h), :] for h in range(NH)]
                wh, ksh, qdh = _heads(w_ref[rows, :]), _heads(ks_ref[rows, :]), _heads(qd_ref[rows, :])
                vn = u_ref[rows, :] - jnp.concatenate([_dot(wh[h], Sh[h], NN) for h in range(NH)], axis=0)
                vn_ref[rows, :] = vn
                av, vnh = _heads(_dot(aqk_ref[rows, :], vn, NN)), _heads(vn)
                for h in range(NH):
                    o_ref[k * CH:(k + 1) * CH, _hs(h)] = _dot(qdh[h], Sh[h], NN) + av[h]
                    S[_hs(h), :] = Sh[h] * eg_ref[k * 8 + h:k * 8 + h + 1, :] + _dot(ksh[h], vnh[h], TN)

    ins, outs, shp = [], [], []
    for d in range(2):
        sp = _scan_specs(orders[d], sg)
        ins += [sp["st"]] * 4 + [sp["aqk"], sp["eg"]]
        outs += [sp["tok"], sp["state"], sp["st"]]
        shp += [_S((T, DN)), _S((n, DN, DH)), _S((n * HS, DH))]
    res = _call(body, "scan_fwd", (n // sg,), ins, outs, shp,
                scratch=[pltpu.VMEM((DN, DH), F32), pltpu.VMEM((DN, DH), F32)], rider=rider)(*parts[0][:6], *parts[1][:6])
    res, riding = (res, None) if rider is None else res
    out = tuple(res[:3]), tuple(res[3:])
    return out if rider is None else (out, riding)


def _scan_bwd(do, parts, tc):
    T = do.shape[0]
    n = T // CH
    sg, fwd_orders = _scan_plan(n, tc // CH)
    orders = [lambda s, f=f: f(n // sg - 1 - s) for f in fwd_orders]

    def body(*refs):
        dS_f, dS_b = refs[-2:]

        @pl.when(pl.program_id(0) == 0)
        def _():
            dS_f[...] = jnp.zeros_like(dS_f)
            dS_b[...] = jnp.zeros_like(dS_b)

        for g in range(sg):
            for d, dS in enumerate((dS_f, dS_b)):
                do_ref, w_ref, ks_ref, qd_ref, aqk_ref, eg_ref = refs[6 * d:6 * d + 6]
                dvn_ref, dss_ref = refs[12 + 2 * d:14 + 2 * d]
                k = sg - 1 - g if d == 0 else g
                rows = slice(k * HS, (k + 1) * HS)
                dss_ref[k] = dS[...]
                dSh = [dS[_hs(h), :] for h in range(NH)]
                wh, ksh, qdh = _heads(w_ref[rows, :]), _heads(ks_ref[rows, :]), _heads(qd_ref[rows, :])
                do_st = _stack(do_ref[k * CH:(k + 1) * CH, :])
                dvn = (_dot(aqk_ref[rows, :], do_st, TN)
                       + jnp.concatenate([_dot(ksh[h], dSh[h], NN) for h in range(NH)], axis=0))
                dvn_ref[rows, :] = dvn
                doh, dvnh = _heads(do_st), _heads(dvn)
                for h in range(NH):
                    dS[_hs(h), :] = (_dot(qdh[h], doh[h], TN) + dSh[h] * eg_ref[k * 8 + h:k * 8 + h + 1, :]
                                     - _dot(wh[h], dvnh[h], TN))

    ins, outs, shp, args = [], [], [], []
    for d in range(2):
        sp = _scan_specs(orders[d], sg)
        ins += [sp["tok"]] + [sp["st"]] * 3 + [sp["aqk"], sp["eg"]]
        outs += [sp["st"], sp["state"]]
        shp += [_S((n * HS, DH)), _S((n, DN, DH))]
        args += [do, *parts[d][1:6]]
    res = _call(body, "scan_bwd", (n // sg,), ins, outs, shp,
                scratch=[pltpu.VMEM((DN, DH), F32), pltpu.VMEM((DN, DH), F32)])(*args)
    return tuple(res[:2]), tuple(res[2:])


def _dn_chunks_bwd(qkv, pab, alr, dtr, do, fwd, bwd, rider=None):
    T = qkv.shape[0]
    n = T // CH
    G = _chunk_group(n)
    csum, csum_t = _scan_consts()

    def body(q_ref, p_ref, do_ref, cs_ref, cst_ref, al_ref, dt_ref, *refs):
        dq_refs, dp_refs, acc_ref = refs[14:16], refs[16:18], refs[18]

        @pl.when(pl.program_id(0) == 0)
        def _():
            acc_ref[...] = jnp.zeros_like(acc_ref)

        lane = _iota((CH, LANES), 1)
        inst = [(g, d) for g in range(G) for d in range(2)]
        cs, Ss, dS2s, dos, vns, dvns = [], [], [], [], [], []
        for g, d in inst:
            tok, rows = slice(g * CH, (g + 1) * CH), slice(g * HS, (g + 1) * HS)
            vn_ref, dvn_ref, ss_ref, dss_ref, tm_ref, u_ref, w_ref = refs[7 * d:7 * d + 7]
            cs.append(_dn_chunk_post(
                _dn_chunk_pre(q_ref[tok, :], p_ref[tok, :], al_ref[...], dt_ref[...], cs_ref[d], d), tm_ref[rows, :],
                uw=(u_ref[rows, :], w_ref[rows, :])))
            Ss.append([ss_ref[g, _hs(h), :] for h in range(NH)])
            dS2s.append([dss_ref[g, _hs(h), :] for h in range(NH)])
            dos.append(_stack(do_ref[tok, :]))
            vns.append(vn_ref[rows, :])
            dvns.append(dvn_ref[rows, :])
        for (g, d), c, (dq, dk, dv, dgc, dbeta) in zip(inst, cs, _dn_chunks_bwd_math(cs, Ss, dS2s, dos, vns, dvns)):
            tok = slice(g * CH, (g + 1) * CH)
            dgcm = jnp.zeros((CH, LANES), F32)
            dbm = jnp.zeros((CH, LANES), F32)
            for h, (a, b_, c_, e, f) in enumerate(zip(*map(_heads, (dq, dk, dv, dgc, dbeta)))):
                dq_refs[d][tok, _hs(h)] = a
                dq_refs[d][tok, _hs(NH + h)] = b_
                dq_refs[d][tok, _hs(2 * NH + h)] = c_
                dgcm = jnp.where(lane == d * NH + h, e, dgcm)
                dbm = jnp.where(lane == 8 + d * NH + h, f, dbm)
            dgm = _dot(cst_ref[d], dgcm, NN, hi=True)
            dsp = dgm * c["nexp"] * jax.nn.sigmoid(c["sp_in"])
            dp_refs[d][tok, :] = dsp + dbm * c["bm"] * (1.0 - c["bm"])
            acc_ref[0:1, :] += _colsum(dgm * c["gm"])
            acc_ref[1:2, :] += _colsum(dsp)

    st = pl.BlockSpec((G * HS, DH), lambda i: (i, 0))
    state = pl.BlockSpec((G, DN, DH), lambda i: (i, 0, 0))
    return _call(
        body, "dn_chunks_bwd", (n // G,),
        [_rows(G * CH, 3 * DN), _rows(G * CH, LANES), _rows(G * CH, DN), _full((2, CH, CH)), _full((2, CH, CH)),
         _full((1, LANES)), _full((1, LANES))]
        + [st, st, state, state, pl.BlockSpec((G * HS, HS), lambda i: (i, 0)), st, st] * 2,
        [_rows(G * CH, 3 * DN)] * 2 + [_rows(G * CH, LANES)] * 2 + [_full((8, LANES))],
        [_S((T, 3 * DN))] * 2 + [_S((T, LANES))] * 2 + [_S((8, LANES))], rider=rider)(
            qkv, pab, do, csum, csum_t, alr, dtr, *fwd, *bwd)


def _head_out(o, z, g):
    on = o * lax.rsqrt(jnp.mean(o * o, axis=-1, keepdims=True) + EPS) * g
    return on * _silu(z)


def _mix_branches(of_ref, ob_ref, z_ref, yp_ref, ys_ref, pg_ref, gdn_ref, wa_ref, wb_ref, wc_ref):
    ons, ya = [], None
    for h in range(NH):
        on = _head_out(of_ref[:, _hs(h)] + ob_ref[:, _hs(h)], z_ref[:, _hs(h)], gdn_ref[...])
        t = _dot(on, wa_ref[_hs(h), :], NN)
        ya = t if ya is None else ya + t
        ons.append(on)
    ys = [ya, _dot(yp_ref[...], wb_ref[...], NN), _dot(ys_ref[...], wc_ref[...], NN)]
    sg = [jax.nn.sigmoid(pg_ref[:, k * D:(k + 1) * D]) for k in range(3)]
    return ons, ys, sg


def _mix_fwd(X, of, ob, z, yp, ys, pg, mv, gdn, wa, wb, wc, wo, tc, tt):
    T = X.shape[0]

    def body(x_ref, of_ref, ob_ref, z_ref, yp_ref, ys_ref, pg_ref, mv_ref, gdn_ref, wa_ref, wb_ref, wc_ref, wo_ref,
             x1_ref):
        _, yb, sg = _mix_branches(of_ref, ob_ref, z_ref, yp_ref, ys_ref, pg_ref, gdn_ref, wa_ref, wb_ref, wc_ref)
        mix = _dot(sg[0] * yb[0] + sg[1] * yb[1] + sg[2] * yb[2], wo_ref[...], NN)
        _, gate = _stream_rows(mv_ref, pl.program_id(0), tt, tc, 2)
        x1_ref[...] = x_ref[...] + gate * mix

    return _call(
        body, "mix_fwd", (T // tt,),
        [_rows(tt, D), _rows(tt, DN), _rows(tt, DN), _rows(tt, DN), _rows(tt, PW), _rows(tt, PW), _rows(tt, 3 * D),
         _full((8, D)), _full((1, DH)), _full(wa.shape), _full(wb.shape), _full(wc.shape), _full(wo.shape)],
        _rows(tt, D), _S((T, D)))(X, of, ob, z, yp, ys, pg, mv, gdn, wa, wb, wc, wo)


def _mix_bwd(dx1, of, ob, z, yp, ys, pg, mv, gdn, wa, wb, wc, wo, tc, tt, rider=None):
    T = dx1.shape[0]

    def body(dx_ref, of_ref, ob_ref, z_ref, yp_ref, ys_ref, pg_ref, mv_ref, gdn_ref, wa_ref, wb_ref, wc_ref, wo_ref,
             do_ref, dz_ref, dyp_ref, dys_ref, dpg_ref, dwa_ref, dwb_ref, dwc_ref, dwo_ref, dgdn_ref, dm_ref):
        i = pl.program_id(0)

        @pl.when(i == 0)
        def _():
            for r in (dwa_ref, dwb_ref, dwc_ref, dwo_ref, dgdn_ref, dm_ref):
                r[...] = jnp.zeros_like(r)

        ons, yb, sg = _mix_branches(of_ref, ob_ref, z_ref, yp_ref, ys_ref, pg_ref, gdn_ref, wa_ref, wb_ref, wc_ref)
        ymix = sg[0] * yb[0] + sg[1] * yb[1] + sg[2] * yb[2]
        isc, gate = _stream_rows(mv_ref, i, tt, tc, 2)
        dx = dx_ref[...]
        dmix = dx * gate
        _acc_stream(dm_ref, 2, isc, dx * _dot(ymix, wo_ref[...], NN))
        dwo_ref[...] += _dot(ymix, dmix, TN)
        dymix = _dot(dmix, wo_ref[...], NT)
        dyb = []
        for k in range(3):
            dyb.append(dymix * sg[k])
            dpg_ref[:, k * D:(k + 1) * D] = dymix * yb[k] * sg[k] * (1.0 - sg[k])
        dwb_ref[...] += _dot(yp_ref[...], dyb[1], TN)
        dwc_ref[...] += _dot(ys_ref[...], dyb[2], TN)
        dyp_ref[...] = _dot(dyb[1], wb_ref[...], NT)
        dys_ref[...] = _dot(dyb[2], wc_ref[...], NT)
        dg = jnp.zeros((1, DH), F32)
        for h in range(NH):
            dwa_ref[_hs(h), :] += _dot(ons[h], dyb[0], TN)
            don = _dot(dyb[0], wa_ref[_hs(h), :], NT)
            _, vjp = jax.vjp(_head_out, of_ref[:, _hs(h)] + ob_ref[:, _hs(h)], z_ref[:, _hs(h)], gdn_ref[...])
            do_h, dz_h, dg_h = vjp(don)
            do_ref[:, _hs(h)] = do_h
            dz_ref[:, _hs(h)] = dz_h
            dg = dg + dg_h
        dgdn_ref[...] += dg

    return _call(
        body, "mix_bwd", (T // tt,),
        [_rows(tt, D), _rows(tt, DN), _rows(tt, DN), _rows(tt, DN), _rows(tt, PW), _rows(tt, PW), _rows(tt, 3 * D),
         _full((8, D)), _full((1, DH)), _full(wa.shape), _full(wb.shape), _full(wc.shape), _full(wo.shape)],
        [_rows(tt, DN), _rows(tt, DN), _rows(tt, PW), _rows(tt, PW), _rows(tt, 3 * D),
         _full(wa.shape), _full(wb.shape), _full(wc.shape), _full(wo.shape), _full((1, DH)), _full((8, D))],
        [_S((T, DN)), _S((T, DN)), _S((T, PW)), _S((T, PW)), _S((T, 3 * D)),
         _S(wa.shape), _S(wb.shape), _S(wc.shape), _S(wo.shape), _S((1, DH)), _S((8, D))], rider=rider)(
            dx1, of, ob, z, yp, ys, pg, mv, gdn, wa, wb, wc, wo)


def _ffn_fwd(X1, mv, g, wgu, wd, tc, tt):
    T = X1.shape[0]

    def body(x_ref, mv_ref, g_ref, wgu_ref, wd_ref, x2_ref, ff_ref):
        i = pl.program_id(0)
        _, sh = _stream_rows(mv_ref, i, tt, tc, 0)
        _, sc = _stream_rows(mv_ref, i, tt, tc, 1)
        _, gate = _stream_rows(mv_ref, i, tt, tc, 2)
        x = x_ref[...]
        gu = _dot(_modulate(x, g_ref[...], sh, sc), wgu_ref[...], NN)
        ff = _dot(_silu(gu[:, :DFF]) * gu[:, DFF:], wd_ref[...], NN)
        ff_ref[...] = ff
        x2_ref[...] = x + gate * ff

    return _call(
        body, "ffn_fwd", (T // tt,),
        [_rows(tt, D), _full((8, D)), _full((1, D)), _full(wgu.shape), _full(wd.shape)],
        [_rows(tt, D)] * 2, [_S((T, D))] * 2)(X1, mv, g, wgu, wd)


def _ffn_bwd(X1, ff, dx2, mv, g, wgu, wd, tc, tt, rider=None):
    T = X1.shape[0]

    def body(x_ref, ff_ref, dx2_ref, mv_ref, g_ref, wgu_ref, wd_ref, dx1_ref, ht_ref, dgu_ref, actt_ref, dff_ref, dg_ref,
             dm_ref):
        i = pl.program_id(0)
        isc, sh = _stream_rows(mv_ref, i, tt, tc, 0)
        _, sc = _stream_rows(mv_ref, i, tt, tc, 1)
        _, gate = _stream_rows(mv_ref, i, tt, tc, 2)
        x, dx2_ = x_ref[...], dx2_ref[...]
        h, vjp = jax.vjp(_modulate, x, g_ref[...], sh, sc)
        ht_ref[...] = h.T.astype(BF16)
        gu = jnp.dot(h.astype(BF16), wgu_ref[...], preferred_element_type=F32)
        ga, up = gu[:, :DFF], gu[:, DFF:]
        sg = jax.nn.sigmoid(ga)
        actt_ref[...] = (ga * sg * up).T.astype(BF16)
        dff = dx2_ * gate
        dff_ref[...] = dff.astype(BF16)
        dact = _dot(dff, wd_ref[...], NT)
        dga = (dact * up * (sg * (1.0 + ga * (1.0 - sg)))).astype(BF16)
        dup = (dact * ga * sg).astype(BF16)
        dgu_ref[:, :DFF] = dga
        dgu_ref[:, DFF:] = dup
        dh = _dot(dga, wgu_ref[:, :DFF], NT) + _dot(dup, wgu_ref[:, DFF:], NT)
        dx, dg, dsh, dsc = vjp(dh)
        dx1_ref[...] = dx2_ + dx

        @pl.when(i == 0)
        def _():
            dg_ref[...] = jnp.zeros_like(dg_ref)
            dm_ref[...] = jnp.zeros_like(dm_ref)

        dg_ref[...] += dg
        _acc_stream(dm_ref, 0, isc, dsh)
        _acc_stream(dm_ref, 1, isc, dsc)
        _acc_stream(dm_ref, 2, isc, dx2_ * ff_ref[...])

    return _call(
        body, "ffn_bwd", (T // tt,),
        [_rows(tt, D), _rows(tt, D), _rows(tt, D), _full((8, D)), _full((1, D)), _full(wgu.shape), _full(wd.shape)],
        [_rows(tt, D), _cols(D, tt), _rows(tt, 2 * DFF), _cols(DFF, tt), _rows(tt, D), _full((1, D)), _full((8, D))],
        [_S((T, D)), _S((D, T), BF16), _S((T, 2 * DFF), BF16), _S((DFF, T), BF16), _S((T, D), BF16),
         _S((1, D)), _S((8, D))], rider=rider)(X1, ff, dx2, mv, g, wgu, wd)


def _rms(x, g):
    return x * lax.rsqrt(jnp.mean(x * x, axis=-1, keepdims=True) + EPS) * g


def _loss_head(X2, tgt, gf, tc):
    T = X2.shape[0]

    def body(x_ref, t_ref, g_ref, dx_ref, loss_ref, dg_ref):
        i = pl.program_id(0)

        @pl.when(i == 0)
        def _():
            dx_ref[...] = jnp.zeros_like(dx_ref)
            loss_ref[...] = jnp.zeros_like(loss_ref)
            dg_ref[...] = jnp.zeros_like(dg_ref)

        @pl.when(i > 0)
        def _():
            y, vjp = jax.vjp(_rms, x_ref[...], g_ref[...])
            err = y - t_ref[...]
            dx, dg = vjp(err * (1.0 / D))
            dx_ref[...] = dx
            dg_ref[...] += dg
            loss_ref[...] += (0.5 / D) * jnp.sum(jnp.sum(err * err, axis=1, keepdims=True), axis=0, keepdims=True)

    return _call(
        body, "loss_head", (T // tc,),
        [_rows(tc, D), pl.BlockSpec((tc, D), lambda i: (jnp.maximum(i - 1, 0), 0)), _full((1, D))],
        [_rows(tc, D), _full((8, LANES)), _full((1, D))],
        [_S((T, D)), _S((8, LANES)), _S((1, D))])(X2, tgt, gf)


def _block_diag(pw):
    g, n = pw.shape[0], pw.shape[1]
    out = jnp.zeros((g * n, g * n), pw.dtype)
    for k in range(g):
        out = lax.dynamic_update_slice(out, pw[k], (k * n, k * n))
    return out


IN_TRUE = tuple(IN_BOUNDS[k + 1] - IN_BOUNDS[k] for k in range(8))


def _overlaps(widths, cw):
    starts = np.cumsum([0] + list(widths))
    out = []
    for k in range(N_DEV):
        for i in range(len(widths)):
            a, b = max(k * cw, starts[i]), min((k + 1) * cw, starts[i + 1])
            if a < b:
                out.append((k, i, int(a - k * cw), int(a - starts[i]), int(b - a)))
    return out


def _shards_to_cols(gathered, l, widths, padded):
    nd, _, R, cw = gathered.shape
    tr = _shard_rows(R)

    def body(s_ref, *o_refs):
        for i, o_ref in enumerate(o_refs):
            if padded[i] > widths[i]:
                o_ref[...] = jnp.zeros_like(o_ref)
        for k, i, so, go, n in _overlaps(widths, cw):
            o_refs[i][:, go:go + n] = s_ref[k, 0, :, so:so + n].astype(BF16)

    return _call(
        body, "shards_to_cols", (R // tr,), [pl.BlockSpec((nd, 1, tr, cw), lambda i: (0, l, i, 0))],
        [_rows(tr, p) for p in padded], [_S((R, p), BF16) for p in padded])(gathered)


def _cols_to_shards(groups, widths, cw):
    R = groups[0].shape[0]
    tr = _shard_rows(R)

    def body(*refs):
        o_ref = refs[-1]
        for k, i, so, go, n in _overlaps(widths, cw):
            o_ref[k % 2, k // 2, 0, :, so:so + n] = refs[i][:, go:go + n]

    return _call(
        body, "cols_to_shards", (R // tr,), [_rows(tr, g.shape[1]) for g in groups],
        pl.BlockSpec((2, N_CHIP, 1, tr, cw), lambda i: (0, 0, 0, i, 0)), _S((2, N_CHIP, 1, R, cw)))(*groups)


def _mod_rows(mods_l, k0):
    rows = [mods_l[s, (k0 + k) * D:(k0 + k + 1) * D] for s in (0, 1) for k in range(3)]
    return jnp.stack(rows + [jnp.zeros((D,), F32)] * 2)


def _lane_row(v8):
    return jnp.pad(v8.reshape(1, 8), ((0, 0), (0, LANES - 8)))


LAYERED = ("w_in", "w_br_a", "w_br_b", "w_br_c", "w_o", "w_gu", "w_down")
LATE = ("w_br_a", "w_br_b", "w_br_c", "w_o", "w_gu", "w_down")
FFN_W = ("w_gu", "w_down")


def _device_step(x, c, ctx, tgt, wts, tt, comm=None):
    tc = ctx.shape[0]
    X = jnp.concatenate([ctx, x], axis=0)
    if comm is None:
        row = 0
        cc = jnp.concatenate([c, jnp.zeros((CTX_ROW - 1, D), F32), wts["c_ctx"][None, :],
                              jnp.zeros((CC_ROWS - CTX_ROW - 1, D), F32)], axis=0)
        w_ada = wts["w_ada"].astype(BF16)
        mods16 = _mod_fwd(cc, w_ada, wts["b_ada"].reshape(NL, 1, 6 * D))
    else:
        row, mods16 = comm.adaln_fwd(c, wts["c_ctx"])
    mods = jnp.stack([mods16[:, CTX_ROW], lax.dynamic_index_in_dim(mods16, row, 1, keepdims=False)], axis=1)

    saved = []
    for l in range(NL):
        ws = _shards_to_cols(wts["w_in"][l], 0, IN_TRUE, IN_WIDTHS)
        mv1, mv2 = _mod_rows(mods[l], 0), _mod_rows(mods[l], 3)
        g1, g2 = wts["norm1_g"][l][None, :], wts["norm2_g"][l][None, :]
        cw, scw = wts["dn_conv_w"][l], wts["sc_conv_w"][l]
        alr, dtr = _lane_row(wts["dn_a_log"][l]), _lane_row(wts["dn_dt_bias"][l])
        gdn = wts["dn_norm_g"][l][None, :]
        pwbd, ps = _block_diag(wts["pool_w"][l]), wts["pool_scale"][l][None, :]
        hb, pq, pz, pab, pp, sx, sb, sc_, pg = _inproj_fwd(X, mv1, g1, ws, tc, tt)
        qkv = _dnprep_fwd(pq, cw, tc, tt)
        if comm is not None and l == 0:
            parts, riding = _dn_chunks_fwd(qkv, pab, alr, dtr, rider=comm.late_weights_chips())
            ((of, ssf, vnf), (ob, ssb, vnb)), riding = _scan_fwd(parts, X.shape[0], tc,
                                                                 rider=comm.late_weights_pair(riding))
            late, w_in_1 = comm.late_weights(riding)
            wts = dict(wts, **late, w_in=[wts["w_in"][0], w_in_1])
        else:
            parts = _dn_chunks_fwd(qkv, pab, alr, dtr)
            (of, ssf, vnf), (ob, ssb, vnb) = _scan_fwd(parts, X.shape[0], tc)
        wbr = [_shards_to_cols(wts[k], l, (2 * DFF,), (2 * DFF,))[0] if k == "w_gu" else wts[k][l].astype(BF16)
               for k in LATE]
        yp = _pool_fwd(pp, pwbd, ps, tc)
        ys = _sc_fwd(sx, sb, sc_, scw, tc, tt)
        X1 = _mix_fwd(X, of, ob, pz, yp, ys, pg, mv1, gdn, *wbr[:4], tc, tt)
        X2, ff = _ffn_fwd(X1, mv2, g2, wbr[4], wbr[5], tc, tt)
        saved.append(dict(X=X, X1=X1, ff=ff, ws=ws, wbr=wbr, mv1=mv1, mv2=mv2, g1=g1, g2=g2, cw=cw, scw=scw, alr=alr, dtr=dtr,
                          gdn=gdn, pwbd=pwbd, ps=ps, hb=hb, pq=pq, pz=pz, pab=pab, pp=pp, sx=sx, sb=sb, sc=sc_, pg=pg,
                          qkv=qkv, of=of, ob=ob, ssf=ssf, ssb=ssb, vnf=vnf, vnb=vnb, parts=parts, yp=yp, ys=ys))
        X = X2

    dX, loss, dgf = _loss_head(X, tgt, wts["final_norm_g"][None, :], tc)

    gl = {k: [None] * NL for k in ("w_in", "norm1_g", "norm2_g", "dn_conv_w", "dn_a_log", "dn_dt_bias", "dn_norm_g",
                                   "pool_w", "pool_scale", "sc_conv_w", "w_br_a", "w_br_b", "w_br_c", "w_o", "w_gu",
                                   "w_down")}
    dmods = [None] * NL
    early = None
    for l in reversed(range(NL)):
        s = saved[l]
        hide = comm is not None and l == 0
        res = _ffn_bwd(s["X1"], s["ff"], dX, s["mv2"], s["g2"], s["wbr"][4], s["wbr"][5], tc, tt,
                       rider=comm.grad_pair_rider([gl[k][1] for k in LAYERED]) if hide else None)
        if hide:
            res, got = res
        dx1, h2, dgu, act, dff, dg2, dm2 = res
        gl["w_gu"][l] = _cols_to_shards([_dw(h2, dgu, tt)], (2 * DFF,), 2 * DFF // N_DEV)
        gl["w_down"][l] = _dw(act, dff, tt)
        res = _mix_bwd(dx1, s["of"], s["ob"], s["pz"], s["yp"], s["ys"], s["pg"], s["mv1"], s["gdn"], *s["wbr"][:4], tc,
                       tt, rider=comm.grad_pair_rider([gl[k][0] for k in FFN_W], FFN_W) if hide else None)
        if hide:
            res, got_ffn = res
            chip_rider = comm.grad_chip_rider(got + got_ffn)
        do, dz, dyp, dys, dpg, dwa, dwb, dwc, dwo, dgdn, dmg = res
        dpp, dpw, dps = _pool_bwd(s["pp"], s["pwbd"], s["ps"], dyp, tc)
        dsx, dsb, dsc, dscw = _sc_bwd(s["sx"], s["sb"], s["sc"], s["scw"], dys, tc, tt)
        (dvnf, dssf), (dvnb, dssb) = _scan_bwd(do, s["parts"], tc)
        res = _dn_chunks_bwd(s["qkv"], s["pab"], s["alr"], s["dtr"], do,
                             (s["vnf"], dvnf, s["ssf"], dssf, s["parts"][0][6], *s["parts"][0][:2]),
                             (s["vnb"], dvnb, s["ssb"], dssb, s["parts"][1][6], *s["parts"][1][:2]),
                             rider=chip_rider if hide else None)
        if hide:
            res, early = res
            early = comm.grad_chip_done(early)
        dqf, dqb, dpf, dpb, gacc = res
        dy = _dnprep_bwd_act(s["pq"], s["cw"], dqf, dqb, tc, tt)
        dpq, dcw = _conv_bwd(dy, s["pq"], s["cw"], tc, tt)
        dps_ = [dpq, dz, dpf, dpb, dpp, dsx, dsb, dsc, dpg]
        dp_w = [0, 1, 2, 2, 3, 4, 5, 6, 7]
        dX, dg1, dm1 = _inproj_bwd(s["X"], s["mv1"], s["g1"], s["ws"], dps_, dp_w, dx1, tc, tt)
        dws = [_dw(s["hb"], dp, tt) for dp in (dpq, dz, dpf + dpb, dpp, dsx, dsb, dsc, dpg)]
        gl["w_in"][l] = _cols_to_shards(dws, IN_TRUE, IN_BOUNDS[-1] // N_DEV)
        gl["norm1_g"][l], gl["norm2_g"][l] = dg1[0], dg2[0]
        gl["dn_conv_w"][l], gl["sc_conv_w"][l] = dcw, dscw
        gl["dn_a_log"][l], gl["dn_dt_bias"][l] = gacc[0, :8].reshape(2, NH), gacc[1, :8].reshape(2, NH)
        gl["dn_norm_g"][l] = dgdn[0]
        gl["pool_w"][l] = jnp.stack([dpw[k * GW:(k + 1) * GW, k * GW:(k + 1) * GW] for k in range(4)])
        gl["pool_scale"][l] = dps[0]
        gl["w_br_a"][l], gl["w_br_b"][l], gl["w_br_c"][l], gl["w_o"][l] = dwa, dwb, dwc, dwo
        dm = dm1 + dmg
        cat = lambda r: jnp.concatenate([dm[r], dm[r + 1], dm[r + 2], dm2[r], dm2[r + 1], dm2[r + 2]])
        dmods[l] = jnp.stack([cat(0), cat(3)])

    dmods = jnp.stack(dmods)
    grads = {k: (v if k in LAYERED else jnp.stack(v)) for k, v in gl.items()}
    if comm is None:
        dm16 = jnp.zeros((NL, CC_ROWS, 6 * D), F32).at[:, CTX_ROW].set(dmods[:, 0]).at[:, row].set(dmods[:, 1])
        dwada, dcc = _mod_bwd(cc, w_ada, dm16)
        grads.update(w_ada=dwada, b_ada=dmods[:, 0] + dmods[:, 1], c_ctx=dcc[CTX_ROW])
    else:
        grads.update(comm.adaln_bwd(dmods))
    grads.update(final_norm_g=dgf[0])
    return loss, dX[tc:], grads, early


def _me():
    return lax.axis_index("x"), lax.axis_index("y"), lax.axis_index("c")


def _dev_index(p):
    return 4 * p[0] + 2 * p[1] + p[2]


def _allgather(parts):
    n = len(parts)

    def body(*refs):
        ins, outs = refs[:n], refs[n:2 * n]
        send_sems, recv_sems = refs[2 * n:]
        x, y, c = _me()
        me, sibling = (x, y, c), (x, y, 1 - c)
        chips = [(1 - x, y), (x, 1 - y), (1 - x, 1 - y)]

        def copy(a, k, block, to, src=None):
            dst = outs[a].at[_dev_index(block)]
            return pltpu.make_async_remote_copy(
                src_ref=dst if src is None else src, dst_ref=dst, send_sem=send_sems.at[a, k], recv_sem=recv_sems.at[a, k],
                device_id=to, device_id_type=MESH_ID)

        first, passed = [], []
        for a in range(n):
            first.append(copy(a, 0, me, sibling, src=ins[a]))
            first += [copy(a, 1 + j, me, (*chip, c), src=ins[a]) for j, chip in enumerate(chips)]
        for cp in first:
            cp.start()
        for a in range(n):
            for j, chip in enumerate(chips):
                copy(a, 1 + j, (*chip, c), me).wait_recv()
                passed.append(copy(a, 4 + j, (*chip, c), sibling))
                passed[-1].start()
        for a in range(n):
            copy(a, 0, sibling, me).wait_recv()
            for j, chip in enumerate(chips):
                copy(a, 4 + j, (*chip, 1 - c), me).wait_recv()
        for cp in first + passed:
            cp.wait_send()

    outs = pl.pallas_call(
        body, name="allgather", in_specs=[HBM_SPEC] * n, out_specs=[HBM_SPEC] * n,
        out_shape=[_S((N_DEV,) + p.shape, p.dtype) for p in parts],
        scratch_shapes=[pltpu.SemaphoreType.DMA((n, 7)), pltpu.SemaphoreType.DMA((n, 7))],
    )(*parts)
    return [_with_own(o, p, _dev_index(_me())) for o, p in zip(outs, parts)]


def _with_own(gathered, own, index):
    return lax.dynamic_update_index_in_dim(gathered, own, index, 0)


def _broadcast_small(small, name="small_exchange"):
    def body(in_ref, out_ref, send_sems, recv_sems, local_sem):
        x, y, c = _me()
        my = _dev_index((x, y, c))
        mine = pltpu.make_async_copy(in_ref, out_ref.at[my], local_sem)
        mine.start()
        remote = []
        for k in range(1, N_DEV):
            cp = pltpu.make_async_remote_copy(
                src_ref=in_ref, dst_ref=out_ref.at[my], send_sem=send_sems.at[k - 1], recv_sem=recv_sems.at[k - 1],
                device_id=(x ^ (k >> 2), y ^ ((k >> 1) & 1), c ^ (k & 1)), device_id_type=MESH_ID)
            cp.start()
            remote.append(cp)
        for cp in remote:
            cp.wait_recv()
        for cp in remote:
            cp.wait_send()
        mine.wait()

    return pl.pallas_call(
        body, name=name, in_specs=[HBM_SPEC], out_specs=HBM_SPEC,
        out_shape=_S((N_DEV,) + small.shape, small.dtype),
        scratch_shapes=[pltpu.SemaphoreType.DMA((7,)), pltpu.SemaphoreType.DMA((7,)), pltpu.SemaphoreType.DMA],
    )(small)


def _run_rider(rider, name):
    ni, no = len(rider.ins), len(rider.out_shapes)

    def body(*refs):
        riding = (refs[:ni], refs[ni:ni + no], refs[ni + no:])
        rider.start(*riding)
        rider.wait(*riding)

    return list(pl.pallas_call(
        body, name=name, in_specs=[HBM_SPEC] * ni, out_specs=[HBM_SPEC] * no, out_shape=rider.out_shapes,
        scratch_shapes=rider.sems)(*rider.ins))


def _chip_peers(x, y):
    return [(k - 1, (x ^ (k >> 1), y ^ (k & 1))) for k in range(1, N_CHIP)]


def _pair_exchange(g2s):
    n = len(g2s)

    def copies(ins, outs, sems):
        x, y, c = _me()
        return [pltpu.make_async_remote_copy(
            src_ref=ins[a].at[1 - c, j], dst_ref=outs[a].at[j], send_sem=sems[0].at[a, j], recv_sem=sems[1].at[a, j],
            device_id=(x, y, 1 - c), device_id_type=MESH_ID) for a in range(n) for j in range(N_CHIP)], []

    return _Rider(g2s, [_S(g.shape[1:], g.dtype) for g in g2s],
                  [pltpu.SemaphoreType.DMA((n, N_CHIP)), pltpu.SemaphoreType.DMA((n, N_CHIP))], copies)


def _my_chip():
    x, y, _ = _me()
    return 2 * x + y


def _chip_exchange(s4s):
    n = len(s4s)

    def copies(ins, outs, sems):
        x, y, c = _me()
        my = 2 * x + y
        return [pltpu.make_async_remote_copy(
            src_ref=ins[a].at[2 * px + py], dst_ref=outs[a].at[my], send_sem=sems[0].at[a, k], recv_sem=sems[1].at[a, k],
            device_id=(px, py, c), device_id_type=MESH_ID) for k, (px, py) in _chip_peers(x, y) for a in range(n)], []

    return _Rider(s4s, [_S(s.shape, s.dtype) for s in s4s],
                  [pltpu.SemaphoreType.DMA((n, N_CHIP - 1)), pltpu.SemaphoreType.DMA((n, N_CHIP - 1))], copies)


def _chip_exchange_done(s4s, recvs):
    my = _my_chip()
    return [_with_own(r, lax.dynamic_index_in_dim(s, my, 0, keepdims=False), my) for s, r in zip(s4s, recvs)]


def _chip_gather(arrs):
    n = len(arrs)

    def copies(ins, outs, sems):
        x, y, c = _me()
        return [pltpu.make_async_remote_copy(
            src_ref=ins[a], dst_ref=outs[a].at[2 * x + y], send_sem=sems[0].at[a, k], recv_sem=sems[1].at[a, k],
            device_id=(px, py, c), device_id_type=MESH_ID) for k, (px, py) in _chip_peers(x, y) for a in range(n)], []

    return _Rider(arrs, [_S((N_CHIP,) + a.shape, a.dtype) for a in arrs],
                  [pltpu.SemaphoreType.DMA((n, N_CHIP - 1)), pltpu.SemaphoreType.DMA((n, N_CHIP - 1))], copies)


def _pair_gather(chips):
    n = len(chips)

    def copies(ins, outs, sems):
        x, y, c = _me()
        return [pltpu.make_async_remote_copy(
            src_ref=ins[a].at[j], dst_ref=outs[a].at[j], send_sem=sems[0].at[a, j], recv_sem=sems[1].at[a, j],
            device_id=(x, y, 1 - c), device_id_type=MESH_ID) for a in range(n) for j in range(N_CHIP)], []

    return _Rider(chips, [_S(a.shape, a.dtype) for a in chips],
                  [pltpu.SemaphoreType.DMA((n, N_CHIP)), pltpu.SemaphoreType.DMA((n, N_CHIP))], copies)


def _shard_rows(r):
    return 256 if r % 256 == 0 else r


def _pair_sum(g2, got):
    _, nc, L, R, C = g2.shape
    tr = _shard_rows(R)

    def body(a_ref, b_ref, o_ref):
        o_ref[...] = (a_ref[0] + b_ref[...]).astype(BF16)

    blk = pl.BlockSpec((1, 1, tr, C), lambda j, l, i: (j, l, i, 0))
    return _call(
        body, "pair_sum", (nc, L, R // tr),
        [pl.BlockSpec((1, 1, 1, tr, C), lambda j, l, i: (lax.axis_index("c"), j, l, i, 0)), blk], blk,
        _S(got.shape, BF16))(g2, got)


def _adam(w, g, m, v):
    m2 = ADAM_B1 * m + (1.0 - ADAM_B1) * g
    v2 = ADAM_B2 * v + (1.0 - ADAM_B2) * (g * g)
    m_hat = m2 / (1.0 - ADAM_B1 ** ADAM_STEP)
    v_hat = v2 / (1.0 - ADAM_B2 ** ADAM_STEP)
    return -ADAM_LR * (m_hat / (jnp.sqrt(v_hat) + ADAM_EPS) + ADAM_WD * w), m2, v2


def _sum_adam(recvs, w, m, v):
    L, R, C = w.shape
    tr = _shard_rows(R)
    nr = len(recvs)

    def body(*refs):
        w_ref, m_ref, v_ref, g_ref, d_ref, m2_ref, v2_ref = refs[nr:]
        g = None
        for li, r_ref in enumerate(refs[:nr]):
            s = r_ref[0, 0].astype(F32)
            for j in range(1, N_CHIP):
                s = s + r_ref[j, 0].astype(F32)
            g = s if g is None else jnp.where(pl.program_id(0) == li, s, g)
        g_ref[0] = g
        d_ref[0], m2_ref[0], v2_ref[0] = _adam(w_ref[0], g, m_ref[0], v_ref[0])

    blk = pl.BlockSpec((1, tr, C), lambda l, i: (l, i, 0))
    rspec = pl.BlockSpec((N_CHIP, 1, tr, C), (lambda l, i: (0, l, i, 0)) if nr == 1 else (lambda l, i: (0, 0, i, 0)))
    return _call(body, "sum_adam", (L, R // tr), [rspec] * nr + [blk, blk, blk], [blk] * 4, [_S(w.shape)] * 4)(
        *recvs, w, m, v)


def _adam_big(w, g, m, v):
    L, R, C = w.shape
    tr = _shard_rows(R)

    def body(w_ref, g_ref, m_ref, v_ref, d_ref, m2_ref, v2_ref):
        d_ref[0], m2_ref[0], v2_ref[0] = _adam(w_ref[0], g_ref[0], m_ref[0], v_ref[0])

    blk = pl.BlockSpec((1, tr, C), lambda l, i: (l, i, 0))
    return _call(body, "adam_big", (L, R // tr), [blk] * 4, [blk] * 3, [_S(w.shape)] * 3)(w, g, m, v)


def _sum_small(recv):
    def body(r_ref, o_ref):
        g = r_ref[0]
        for k in range(1, recv.shape[0]):
            g = g + r_ref[k]
        o_ref[...] = g

    return pl.pallas_call(body, name="sum_small", out_shape=_S(recv.shape[1:]))(recv)


def _adam_small(w, g, m, v):
    def body(w_ref, g_ref, m_ref, v_ref, d_ref, m2_ref, v2_ref):
        d_ref[...], m2_ref[...], v2_ref[...] = _adam(w_ref[...], g_ref[...], m_ref[...], v_ref[...])

    return pl.pallas_call(body, name="adam_small", out_shape=[_S(w.shape)] * 3)(w, g, m, v)


def _pack(arrs, dtype, row_mult):
    parts, offs, r = [], [], 0
    for a in arrs:
        nr = -(-a.size // LANES)
        parts.append(jnp.pad(a.reshape(-1).astype(dtype), (0, nr * LANES - a.size)))
        offs.append(r)
        r += nr
    pad = (-r) % row_mult
    if pad:
        parts.append(jnp.zeros((pad * LANES,), dtype))
    return jnp.concatenate(parts).reshape(r + pad, LANES), offs


def _unpack(packed, offs, shapes, lead=()):
    out = []
    for off, shp in zip(offs, shapes):
        size = int(np.prod(shp))
        nr = -(-size // LANES)
        flat = packed[..., off:off + nr, :].reshape(lead + (nr * LANES,))
        out.append(flat[..., :size].reshape(lead + tuple(shp)))
    return out


BIG = (("w_ada", 2), ("w_in", 2), ("w_br_a", 2), ("w_br_b", 2), ("w_br_c", 2), ("w_o", 1), ("w_gu", 2), ("w_down", 1))
CONV = ("dn_conv_w", "sc_conv_w")
REPL = ("c_ctx", "b_ada", "norm1_g", "norm2_g", "dn_a_log", "dn_dt_bias", "dn_norm_g", "pool_w", "pool_scale",
        "final_norm_g")
WEIGHTS = ("c_ctx", "w_ada", "b_ada", "norm1_g", "norm2_g", "w_in", "dn_conv_w", "dn_a_log", "dn_dt_bias", "dn_norm_g",
           "pool_w", "pool_scale", "sc_conv_w", "w_br_a", "w_br_b", "w_br_c", "w_o", "w_gu", "w_down", "final_norm_g")
TOKEN_TILE = 256


def _join(blocks, axis):
    nd, nl, r, c = blocks.shape
    if axis == 2:
        return blocks.transpose(1, 2, 0, 3).reshape(nl, r, nd * c)
    return blocks.transpose(1, 0, 2, 3).reshape(nl, nd * r, c)


def _split(full, axis):
    nl, r, c = full.shape
    if axis == 2:
        return full.reshape(nl, r, N_CHIP, 2, c // N_DEV).transpose(3, 2, 0, 1, 4)
    return full.reshape(nl, N_CHIP, 2, r // N_DEV, c).transpose(2, 1, 0, 3, 4)


PRESPLIT = ("w_in", "w_gu")


def _presplit(layer_grads, names=LAYERED):
    return [g if k in PRESPLIT else _split(g[None], dict(BIG)[k]) for k, g in zip(names, layer_grads)]


class _Comm:
    def __init__(self, late_shards, w_ada, b_ada):
        self.packed = [k for k in LATE if k not in PRESPLIT]
        self.shapes = [late_shards[k].shape for k in self.packed]
        pack, self.offs = _pack([late_shards[k] for k in self.packed], BF16, BF16_ROWS)
        self.late = [pack, late_shards["w_gu"].astype(BF16), late_shards["w_in"][1:].astype(BF16)]
        self.w_ada, self.b_ada = w_ada.astype(BF16), b_ada
        self.g2s = None

    def adaln_fwd(self, c, c_ctx):
        my = _dev_index(_me())
        ncol = self.w_ada.shape[2]
        c_all = _broadcast_small(c.reshape(8, LANES), "c_exchange").reshape(N_DEV, D)
        self.cc = jnp.concatenate([c_all, c_ctx[None, :], jnp.zeros((CC_ROWS - N_DEV - 1, D), F32)], axis=0)
        b_cols = lax.dynamic_slice_in_dim(self.b_ada, my * ncol, ncol, axis=1).reshape(NL, 1, ncol)
        cols = _mod_fwd(self.cc, self.w_ada, b_cols)
        got = _broadcast_small(cols.reshape(-1, LANES), "mods_exchange").reshape(N_DEV, NL, CC_ROWS, ncol)
        return my, got.transpose(1, 2, 0, 3).reshape(NL, CC_ROWS, N_DEV * ncol)

    def adaln_bwd(self, dmods):
        my = _dev_index(_me())
        ncol = self.w_ada.shape[2]
        got = _broadcast_small(dmods.reshape(-1, LANES), "dmods_exchange")
        rows = got.reshape(N_DEV, NL, 2, 6 * D)
        ctx_sum = _sum_small(rows[:, :, 0].reshape(N_DEV, -1, LANES)).reshape(NL, 1, 6 * D)
        db = _sum_small(rows.transpose(0, 2, 1, 3).reshape(2 * N_DEV, -1, LANES)).reshape(NL, 6 * D)
        dm = jnp.concatenate([rows[:, :, 1].transpose(1, 0, 2), ctx_sum,
                              jnp.zeros((NL, CC_ROWS - N_DEV - 1, 6 * D), F32)], axis=1)
        dw, dcc = _mod_bwd(self.cc, self.w_ada, lax.dynamic_slice_in_dim(dm, my * ncol, ncol, axis=2))
        return dict(w_ada=dw, b_ada=db, c_ctx=dcc[CTX_ROW])

    def late_weights_chips(self):
        return _chip_gather(self.late)

    def late_weights_pair(self, riding):
        self.chips = [_with_own(r, a, _my_chip()) for r, a in zip(riding, self.late)]
        return _pair_gather(self.chips)

    def late_weights(self, riding):
        on_south = lax.axis_index("c") == 0
        by_dev = []
        for mine, other in zip(self.chips, riding):
            both = jnp.stack([jnp.where(on_south, mine, other), jnp.where(on_south, other, mine)], axis=1)
            by_dev.append(both.reshape((N_DEV,) + mine.shape[1:]))
        shards = _unpack(by_dev[0], self.offs, self.shapes, (N_DEV,))
        return dict({k: _join(blocks, dict(BIG)[k]) for k, blocks in zip(self.packed, shards)}, w_gu=by_dev[1]), by_dev[2]

    def grad_pair_rider(self, layer_grads, names=LAYERED):
        g2s = _presplit(layer_grads, names)
        self.g2s = (self.g2s or []) + g2s
        return _pair_exchange(g2s)

    def grad_chip_rider(self, got):
        self.sums = [_pair_sum(g2, gt) for g2, gt in zip(self.g2s, got)]
        return _chip_exchange(self.sums)

    def grad_chip_done(self, riding):
        return _chip_exchange_done(self.sums, riding)


def kernel(x, c, ctx, c_ctx, w_ada, b_ada, norm1_g, norm2_g, w_in, dn_conv_w, dn_a_log, dn_dt_bias, dn_norm_g, pool_w, pool_scale, sc_conv_w, w_br_a, w_br_b, w_br_c, w_o, w_gu, w_down, final_norm_g, loss_target, m_c_ctx, m_w_ada, m_b_ada, m_norm1_g, m_norm2_g, m_w_in, m_dn_conv_w, m_dn_a_log, m_dn_dt_bias, m_dn_norm_g, m_pool_w, m_pool_scale, m_sc_conv_w, m_w_br_a, m_w_br_b, m_w_br_c, m_w_o, m_w_gu, m_w_down, m_final_norm_g, v_c_ctx, v_w_ada, v_b_ada, v_norm1_g, v_norm2_g, v_w_in, v_dn_conv_w, v_dn_a_log, v_dn_dt_bias, v_dn_norm_g, v_pool_w, v_pool_scale, v_sc_conv_w, v_w_br_a, v_w_br_b, v_w_br_c, v_w_o, v_w_gu, v_w_down, v_final_norm_g):
    loc = dict(c_ctx=c_ctx, w_ada=w_ada, b_ada=b_ada, norm1_g=norm1_g, norm2_g=norm2_g, w_in=w_in, dn_conv_w=dn_conv_w,
               dn_a_log=dn_a_log, dn_dt_bias=dn_dt_bias, dn_norm_g=dn_norm_g, pool_w=pool_w, pool_scale=pool_scale,
               sc_conv_w=sc_conv_w, w_br_a=w_br_a, w_br_b=w_br_b, w_br_c=w_br_c, w_o=w_o, w_gu=w_gu, w_down=w_down,
               final_norm_g=final_norm_g)
    mom_m = dict(c_ctx=m_c_ctx, w_ada=m_w_ada, b_ada=m_b_ada, norm1_g=m_norm1_g, norm2_g=m_norm2_g, w_in=m_w_in,
                 dn_conv_w=m_dn_conv_w, dn_a_log=m_dn_a_log, dn_dt_bias=m_dn_dt_bias, dn_norm_g=m_dn_norm_g,
                 pool_w=m_pool_w, pool_scale=m_pool_scale, sc_conv_w=m_sc_conv_w, w_br_a=m_w_br_a, w_br_b=m_w_br_b,
                 w_br_c=m_w_br_c, w_o=m_w_o, w_gu=m_w_gu, w_down=m_w_down, final_norm_g=m_final_norm_g)
    mom_v = dict(c_ctx=v_c_ctx, w_ada=v_w_ada, b_ada=v_b_ada, norm1_g=v_norm1_g, norm2_g=v_norm2_g, w_in=v_w_in,
                 dn_conv_w=v_dn_conv_w, dn_a_log=v_dn_a_log, dn_dt_bias=v_dn_dt_bias, dn_norm_g=v_dn_norm_g,
                 pool_w=v_pool_w, pool_scale=v_pool_scale, sc_conv_w=v_sc_conv_w, w_br_a=v_w_br_a, w_br_b=v_w_br_b,
                 w_br_c=v_w_br_c, w_o=v_w_o, w_gu=v_w_gu, w_down=v_w_down, final_norm_g=v_final_norm_g)
    my = _dev_index(_me())

    conv_pack, conv_offs = _pack([loc[k] for k in CONV], F32, 8)
    w_in0_all, conv_all = _allgather([w_in[:1].astype(BF16), conv_pack])
    full = dict({k: loc[k] for k in REPL}, w_in=[w_in0_all])
    for k, blocks in zip(CONV, _unpack(conv_all, conv_offs, [loc[k].shape for k in CONV], (N_DEV,))):
        full[k] = _join(blocks, 2)

    loss8, grad_x, g, recv_early = _device_step(x[0], c, ctx[0], loss_target[0], full, TOKEN_TILE,
                                                comm=_Comm({k: loc[k] for k in LATE + ("w_in",)}, w_ada, b_ada))

    last = [k for k in LAYERED if k not in FFN_W]
    tail = _presplit([g[k][0] for k in last], last)
    got = _run_rider(_pair_exchange(tail), "pair_exchange")
    sums = [_pair_sum(a, b) for a, b in zip(tail, got)]
    recv_tail = _chip_exchange_done(sums, _run_rider(_chip_exchange(sums), "chip_exchange"))
    recv = {k: [None, r] for k, r in zip(LAYERED, recv_early)}
    for k, r in list(zip(FFN_W, recv_early[len(LAYERED):])) + list(zip(last, recv_tail)):
        recv[k][0] = r

    small_names = REPL + CONV
    summed = [k for k in small_names if k != "b_ada"]
    small_pack, small_offs = _pack([g[k] for k in summed] + [loss8[0:1, 0:1]], F32, 8)
    small_sum = _sum_small(_broadcast_small(small_pack))
    sums = _unpack(small_sum, small_offs, [g[k].shape for k in summed] + [(1, 1)])
    grads = dict(zip(summed, sums[:-1]), b_ada=g["b_ada"], w_ada=g["w_ada"])
    loss = sums[-1][0, 0]
    for k in CONV:
        w = loc[k].shape[2]
        grads[k] = lax.dynamic_slice_in_dim(grads[k], my * w, w, axis=2)

    delta, new_m, new_v = {}, {}, {}
    for k in LAYERED:
        grads[k], delta[k], new_m[k], new_v[k] = _sum_adam(recv[k], loc[k], mom_m[k], mom_v[k])
    delta["w_ada"], new_m["w_ada"], new_v["w_ada"] = _adam_big(w_ada, g["w_ada"], m_w_ada, v_w_ada)
    packs = [_pack([src[k] for k in small_names], F32, 8)[0] for src in (loc, grads, mom_m, mom_v)]
    _, offs = _pack([loc[k] for k in small_names], F32, 8)
    shapes = [loc[k].shape for k in small_names]
    for dst, packed in zip((delta, new_m, new_v), _adam_small(*packs)):
        dst.update(zip(small_names, _unpack(packed, offs, shapes)))

    return (loss, grad_x[None], *[grads[k] for k in WEIGHTS], *[delta[k] for k in WEIGHTS],
            *[new_m[k] for k in WEIGHTS], *[new_v[k] for k in WEIGHTS])
```

```python
import numpy as np
import jax
import jax.numpy as jnp
from jax import lax
from jax.experimental import pallas as pl
from jax.experimental.pallas import tpu as pltpu

F32 = jnp.float32
BF16 = jnp.bfloat16
HI = lax.Precision.HIGHEST

D = 1024
NL = 2
NH = 4
DH = 128
DN = NH * DH
CH = 64
GW = 64
PW = 256
DFF = 2816
EPS = 1e-6
N_DEV = 8
N_CHIP = 4
MESH_ID = pl.DeviceIdType.MESH
HBM_SPEC = pl.BlockSpec(memory_space=pltpu.HBM)
LANES = 128
BF16_ROWS = 16
VMEM_MB = 56
DW_ACC_MB = 12

ADAM_LR, ADAM_B1, ADAM_B2, ADAM_EPS, ADAM_WD, ADAM_STEP = 0.001, 0.9, 0.999, 1e-08, 0.01, 10

IN_BOUNDS = (0, 1536, 2048, 2064, 2320, 2576, 2832, 3088, 6160)
IN_WIDTHS = (1536, 512, 128, 256, 256, 256, 256, 3072)
POOL_WIN = ((1, 0), (2, 1), (4, 3), (8, 7))

NN = ((1,), (0,))
NT = ((1,), (1,))
TN = ((0,), (0,))


def _dot(a, b, dims, hi=False):
    if hi:
        prec = lax.Precision.HIGH if hi == "x3" else HI
        return lax.dot_general(a, b, (dims, ((), ())), precision=prec, preferred_element_type=F32)
    return lax.dot_general(a.astype(BF16), b.astype(BF16), (dims, ((), ())), preferred_element_type=F32)


def _S(shape, dtype=F32):
    return jax.ShapeDtypeStruct(tuple(shape), dtype)


def _full(shape):
    nd = len(shape)
    return pl.BlockSpec(tuple(shape), lambda *_: (0,) * nd)


def _rows(tt, w):
    return pl.BlockSpec((tt, w), lambda i: (i, 0))


class _Rider:
    def __init__(self, ins, out_shapes, sems, copies):
        self.ins, self.out_shapes, self.sems, self.copies = list(ins), list(out_shapes), list(sems), copies

    def start(self, ins, outs, sems):
        remote, local = self.copies(ins, outs, sems)
        for cp in local + remote:
            cp.start()

    def wait(self, ins, outs, sems):
        remote, local = self.copies(ins, outs, sems)
        for cp in remote:
            cp.wait_recv()
        for cp in remote:
            cp.wait_send()
        for cp in local:
            cp.wait()


def _call(body, name, grid, in_specs, out_specs, out_shape, scratch=(), rider=None):
    params = pltpu.CompilerParams(dimension_semantics=("arbitrary",) * len(grid), vmem_limit_bytes=VMEM_MB << 20)
    if rider is None:
        return pl.pallas_call(body, name=name, grid=grid, in_specs=in_specs, out_specs=out_specs, out_shape=out_shape,
                              scratch_shapes=list(scratch), compiler_params=params)
    single = not isinstance(out_shape, (list, tuple))
    out_specs, out_shape = ([out_specs], [out_shape]) if single else (list(out_specs), list(out_shape))
    n_in, n_out, n_scr = len(in_specs), len(out_shape), len(scratch)
    r_in, r_out = len(rider.ins), len(rider.out_shapes)

    def hosted(*refs):
        ins, refs = refs[:n_in + r_in], refs[n_in + r_in:]
        outs, scr = refs[:n_out + r_out], refs[n_out + r_out:]
        riding = (ins[n_in:], outs[n_out:], scr[n_scr:])

        @pl.when(pl.program_id(0) == 0)
        def _():
            rider.start(*riding)

        body(*ins[:n_in], *outs[:n_out], *scr[:n_scr])

        @pl.when(pl.program_id(0) == grid[0] - 1)
        def _():
            rider.wait(*riding)

    call = pl.pallas_call(
        hosted, name=name, grid=grid, in_specs=list(in_specs) + [HBM_SPEC] * r_in,
        out_specs=out_specs + [HBM_SPEC] * r_out, out_shape=out_shape + rider.out_shapes,
        scratch_shapes=list(scratch) + rider.sems, compiler_params=params)

    def run(*args):
        res = call(*args, *rider.ins)
        own = res[:n_out]
        return (own[0] if single else own), list(res[n_out:])

    return run


def _iota(shape, axis):
    return lax.broadcasted_iota(jnp.int32, shape, axis)


def _colsum(a):
    return jnp.sum(a, axis=0, keepdims=True)


def _silu(x):
    return x * jax.nn.sigmoid(x)


def _modulate(x, g, sh, sc):
    xn = x * lax.rsqrt(jnp.mean(x * x, axis=-1, keepdims=True) + EPS)
    return (xn * g) * (1.0 + sc) + sh


def _stream_rows(mv_ref, i, tt, tc, k):
    isc = (i * tt + _iota((tt, 1), 0)) < tc
    return isc, jnp.where(isc, mv_ref[k:k + 1, :], mv_ref[3 + k:4 + k, :])


def _acc_stream(ref, k, isc, val):
    ref[k:k + 1, :] += _colsum(jnp.where(isc, val, 0.0))
    ref[3 + k:4 + k, :] += _colsum(jnp.where(isc, 0.0, val))


CC_ROWS = 16
CTX_ROW = 8


def _mod_cols(n):
    return 1536 if n % 1536 == 0 else n


def _mod_fwd(cc, w_ada, b_ada3):
    n = w_ada.shape[2]
    ct = _mod_cols(n)

    def body(cc_ref, w_ref, b_ref, o_ref):
        o_ref[0] = _dot(_silu(cc_ref[...]), w_ref[0], NN) + b_ref[0]

    return _call(
        body, "mod_fwd", (NL, n // ct),
        [pl.BlockSpec((CC_ROWS, D), lambda l, j: (0, 0)), pl.BlockSpec((1, D, ct), lambda l, j: (l, 0, j)),
         pl.BlockSpec((1, 1, ct), lambda l, j: (l, 0, j))],
        pl.BlockSpec((1, CC_ROWS, ct), lambda l, j: (l, 0, j)), _S((NL, CC_ROWS, n)))(cc, w_ada, b_ada3)


def _mod_bwd(cc, w_ada, dmods):
    n = w_ada.shape[2]
    ct = _mod_cols(n)

    def body(cc_ref, w_ref, dm_ref, dw_ref, dcc_ref):
        first = (pl.program_id(0) == 0) & (pl.program_id(1) == 0)
        cc_ = cc_ref[...]
        sg = jax.nn.sigmoid(cc_)
        dm = dm_ref[0]
        dw_ref[0] = _dot(cc_ * sg, dm, TN)

        @pl.when(first)
        def _():
            dcc_ref[...] = jnp.zeros_like(dcc_ref)

        dcc_ref[...] += _dot(dm, w_ref[0], NT) * (sg * (1.0 + cc_ * (1.0 - sg)))

    return _call(
        body, "mod_bwd", (NL, n // ct),
        [pl.BlockSpec((CC_ROWS, D), lambda l, j: (0, 0)), pl.BlockSpec((1, D, ct), lambda l, j: (l, 0, j)),
         pl.BlockSpec((1, CC_ROWS, ct), lambda l, j: (l, 0, j))],
        [pl.BlockSpec((1, D, ct), lambda l, j: (l, 0, j)), pl.BlockSpec((CC_ROWS, D), lambda l, j: (0, 0))],
        [_S((NL, D, n)), _S((CC_ROWS, D))])(cc, w_ada, dmods)


def _cols(n, tt):
    return pl.BlockSpec((n, tt), lambda i: (0, i))


def _inproj_fwd(X, mv, g, ws, tc, tt):
    T = X.shape[0]
    nw = len(ws)

    def body(x_ref, mv_ref, g_ref, *refs):
        w_refs, ht_ref, p_refs = refs[:nw], refs[nw], refs[nw + 1:]
        i = pl.program_id(0)
        _, sh = _stream_rows(mv_ref, i, tt, tc, 0)
        _, sc = _stream_rows(mv_ref, i, tt, tc, 1)
        h = _modulate(x_ref[...], g_ref[...], sh, sc)
        ht_ref[...] = h.T.astype(BF16)
        hb = h.astype(BF16)
        for w_ref, p_ref in zip(w_refs, p_refs):
            p_ref[...] = jnp.dot(hb, w_ref[...], preferred_element_type=F32)

    return _call(
        body, "inproj_fwd", (T // tt,),
        [_rows(tt, D), _full((8, D)), _full((1, D))] + [_full(w.shape) for w in ws],
        [_cols(D, tt)] + [_rows(tt, w.shape[1]) for w in ws],
        [_S((D, T), BF16)] + [_S((T, w.shape[1])) for w in ws])(X, mv, g, *ws)


def _inproj_bwd(X, mv, g, ws, dps, dp_w, dres, tc, tt):
    T = X.shape[0]
    nw, nd = len(ws), len(dps)

    def body(x_ref, mv_ref, g_ref, dres_ref, *refs):
        w_refs, dp_refs = refs[:nw], refs[nw:nw + nd]
        dx_ref, dg_ref, dm_ref = refs[nw + nd:]
        i = pl.program_id(0)
        isc, sh = _stream_rows(mv_ref, i, tt, tc, 0)
        _, sc = _stream_rows(mv_ref, i, tt, tc, 1)
        dh = None
        for dp_ref, k in zip(dp_refs, dp_w):
            t = _dot(dp_ref[...], w_refs[k][...], NT)
            dh = t if dh is None else dh + t
        _, vjp = jax.vjp(_modulate, x_ref[...], g_ref[...], sh, sc)
        dx, dg, dsh, dsc = vjp(dh)
        dx_ref[...] = dres_ref[...] + dx

        @pl.when(i == 0)
        def _():
            dg_ref[...] = jnp.zeros_like(dg_ref)
            dm_ref[...] = jnp.zeros_like(dm_ref)

        dg_ref[...] += dg
        _acc_stream(dm_ref, 0, isc, dsh)
        _acc_stream(dm_ref, 1, isc, dsc)

    return _call(
        body, "inproj_bwd", (T // tt,),
        [_rows(tt, D), _full((8, D)), _full((1, D)), _rows(tt, D)] + [_full(w.shape) for w in ws]
        + [_rows(tt, dp.shape[1]) for dp in dps],
        [_rows(tt, D), _full((1, D)), _full((8, D))],
        [_S((T, D)), _S((1, D)), _S((8, D))])(X, mv, g, dres, *ws, *dps)


def _dw(At, B, tt):
    K, T = At.shape
    N = B.shape[1]
    tt = 3 * tt if T % (3 * tt) == 0 else tt
    tn = max(t for t in range(LANES, N + 1, LANES) if N % t == 0 and K * t * 4 <= DW_ACC_MB << 20)

    def body(a_ref, b_ref, o_ref):
        @pl.when(pl.program_id(1) == 0)
        def _():
            o_ref[...] = jnp.zeros_like(o_ref)

        o_ref[...] += _dot(a_ref[...], b_ref[...], NN)

    return _call(
        body, "dw", (N // tn, T // tt),
        [pl.BlockSpec((K, tt), lambda j, i: (0, i)), pl.BlockSpec((tt, tn), lambda j, i: (i, j))],
        pl.BlockSpec((K, tn), lambda j, i: (0, j)), _S((K, N)))(At, B)


def _halo_specs(T, tt, cw, col):
    r8, nb8 = tt // 8, T // 8
    return [pl.BlockSpec((tt, cw), lambda j, i: (i, col(j))),
            pl.BlockSpec((8, cw), lambda j, i: (jnp.maximum(i * r8 - 1, 0), col(j))),
            pl.BlockSpec((8, cw), lambda j, i: (jnp.minimum((i + 1) * r8, nb8 - 1), col(j)))]


def _shifts(a, prev8, next8, i, tt, tc, T):
    r = _iota((tt, 1), 0)
    t = i * tt + r
    dn = jnp.where(r == 0, prev8[7:8, :], pltpu.roll(a, 1, 0))
    dn = jnp.where((t == 0) | (t == tc), 0.0, dn)
    up = jnp.where(r == tt - 1, next8[0:1, :], pltpu.roll(a, tt - 1, 0))
    up = jnp.where((t == T - 1) | (t == tc - 1), 0.0, up)
    return dn, up


def _dn_post(y, part):
    a = _silu(y)
    nrm = lax.rsqrt(jnp.sum(a * a, axis=-1, keepdims=True) + EPS)
    f = jnp.where(part == 0, nrm * (DH ** -0.5), jnp.where(part == 1, nrm, 1.0))
    return a * f


def _conv3(w_ref, dn, mid, up):
    return w_ref[0:1, :] * dn + w_ref[1:2, :] * mid + w_ref[2:3, :] * up


def _dnprep_fwd(pq, cw, tc, tt):
    T = pq.shape[0]

    def body(p_ref, pp_ref, pn_ref, w_ref, a_ref):
        part, i = pl.program_id(0), pl.program_id(1)
        p = p_ref[...]
        dn, up = _shifts(p, pp_ref[...], pn_ref[...], i, tt, tc, T)
        y = _conv3(w_ref, dn, p, up)
        for h in range(NH):
            a_ref[:, _hs(h)] = _dn_post(y[:, _hs(h)], part)

    return _call(
        body, "dnprep_fwd", (3, T // tt),
        _halo_specs(T, tt, DN, lambda j: j) + [pl.BlockSpec((3, DN), lambda j, i: (0, j))],
        pl.BlockSpec((tt, DN), lambda j, i: (i, j)), _S((T, 3 * DN)))(pq, pq, pq, cw)


def _dnprep_bwd_act(pq, cw, da_f, da_b, tc, tt):
    T = pq.shape[0]

    def body(p_ref, pp_ref, pn_ref, w_ref, df_ref, db_ref, dy_ref):
        part, i = pl.program_id(0), pl.program_id(1)
        p = p_ref[...]
        dn, up = _shifts(p, pp_ref[...], pn_ref[...], i, tt, tc, T)
        y = _conv3(w_ref, dn, p, up)
        for h in range(NH):
            _, vjp = jax.vjp(lambda yh: _dn_post(yh, part), y[:, _hs(h)])
            dy_ref[:, _hs(h)] = vjp(df_ref[:, _hs(h)] + db_ref[:, _hs(h)])[0]

    blk = pl.BlockSpec((tt, DN), lambda j, i: (i, j))
    return _call(
        body, "dnprep_bwd_act", (3, T // tt),
        _halo_specs(T, tt, DN, lambda j: j) + [pl.BlockSpec((3, DN), lambda j, i: (0, j)), blk, blk],
        blk, _S((T, 3 * DN)))(pq, pq, pq, cw, da_f, da_b)


def _conv_bwd(dy, p, cw, tc, tt):
    T, W = p.shape
    cb = DN

    def body(dy_ref, dyp_ref, dyn_ref, p_ref, pp_ref, pn_ref, w_ref, dp_ref, dw_ref):
        i = pl.program_id(1)
        dy, p_ = dy_ref[...], p_ref[...]
        ddn, dup = _shifts(dy, dyp_ref[...], dyn_ref[...], i, tt, tc, T)
        dp_ref[...] = _conv3(w_ref, dup, dy, ddn)
        pdn, pup = _shifts(p_, pp_ref[...], pn_ref[...], i, tt, tc, T)

        @pl.when(i == 0)
        def _():
            dw_ref[...] = jnp.zeros_like(dw_ref)

        dw_ref[0:1, :] += _colsum(dy * pdn)
        dw_ref[1:2, :] += _colsum(dy * p_)
        dw_ref[2:3, :] += _colsum(dy * pup)

    wspec = pl.BlockSpec((3, cb), lambda j, i: (0, j))
    return _call(
        body, "conv_bwd", (W // cb, T // tt),
        _halo_specs(T, tt, cb, lambda j: j) * 2 + [wspec],
        [pl.BlockSpec((tt, cb), lambda j, i: (i, j)), wspec], [_S((T, W)), _S((3, W))])(dy, dy, dy, p, p, p, cw)


def _sc_fwd(sx, sb, sc_, cw, tc, tt):
    T = sx.shape[0]

    def body(x_ref, xp_ref, xn_ref, c_ref, cp_ref, cn_ref, b_ref, w_ref, y_ref):
        i = pl.program_id(1)
        u = c_ref[...] * x_ref[...]
        dn, up = _shifts(u, cp_ref[...] * xp_ref[...], cn_ref[...] * xn_ref[...], i, tt, tc, T)
        y_ref[...] = b_ref[...] * _conv3(w_ref, dn, u, up)

    blk = pl.BlockSpec((tt, LANES), lambda j, i: (i, j))
    return _call(
        body, "sc_fwd", (PW // LANES, T // tt),
        _halo_specs(T, tt, LANES, lambda j: j) * 2 + [blk, pl.BlockSpec((3, LANES), lambda j, i: (0, j))],
        blk, _S((T, PW)))(sx, sx, sx, sc_, sc_, sc_, sb, cw)


def _sc_bwd(sx, sb, sc_, cw, dy, tc, tt):
    T = sx.shape[0]

    def body(x_ref, xp_ref, xn_ref, c_ref, cp_ref, cn_ref, b_ref, bp_ref, bn_ref, dy_ref, dyp_ref, dyn_ref, w_ref,
             dx_ref, db_ref, dc_ref, dw_ref):
        i = pl.program_id(1)
        x, c, dy_ = x_ref[...], c_ref[...], dy_ref[...]
        u = c * x
        udn, uup = _shifts(u, cp_ref[...] * xp_ref[...], cn_ref[...] * xn_ref[...], i, tt, tc, T)
        db_ref[...] = dy_ * _conv3(w_ref, udn, u, uup)
        e = dy_ * b_ref[...]
        edn, eup = _shifts(e, dyp_ref[...] * bp_ref[...], dyn_ref[...] * bn_ref[...], i, tt, tc, T)
        du = _conv3(w_ref, eup, e, edn)
        dx_ref[...] = du * c
        dc_ref[...] = du * x

        @pl.when(i == 0)
        def _():
            dw_ref[...] = jnp.zeros_like(dw_ref)

        dw_ref[0:1, :] += _colsum(e * udn)
        dw_ref[1:2, :] += _colsum(e * u)
        dw_ref[2:3, :] += _colsum(e * uup)

    blk = pl.BlockSpec((tt, LANES), lambda j, i: (i, j))
    wspec = pl.BlockSpec((3, LANES), lambda j, i: (0, j))
    return _call(
        body, "sc_bwd", (PW // LANES, T // tt),
        _halo_specs(T, tt, LANES, lambda j: j) * 4 + [wspec],
        [blk, blk, blk, wspec], [_S((T, PW))] * 3 + [_S((3, PW))])(
            sx, sx, sx, sc_, sc_, sc_, sb, sb, sb, dy, dy, dy, cw)


def _group_select(vals):
    g = _iota((1, PW), 1) // (PW // len(POOL_WIN))
    return jnp.where(g == 0, vals[0], jnp.where(g == 1, vals[1], jnp.where(g == 2, vals[2], vals[3])))


def _nested_box(get, mirror):
    acc, outs, pl_, ph_ = get(0), [], 0, 0
    for lo, hi in POOL_WIN:
        if mirror:
            lo, hi = hi, lo
        for k in range(pl_ + 1, lo + 1):
            acc = acc + get(-k)
        for k in range(ph_ + 1, hi + 1):
            acc = acc + get(k)
        pl_, ph_ = lo, hi
        outs.append(acc)
    return _group_select(outs)


def _box_tokens(a, n, mirror):
    idx = _iota((n, 1), 0)

    def get(k):
        if k == 0:
            return a
        return jnp.where((idx + k >= 0) & (idx + k < n), pltpu.roll(a, (-k) % n, 0), 0.0)

    return _nested_box(get, mirror)


def _inv_count(pos, n):
    return _group_select([1.0 / (jnp.minimum(pos + hi, n - 1) - jnp.maximum(pos - lo, 0) + 1).astype(F32)
                          for lo, hi in POOL_WIN])


def _pool_rows(ref, r, R, tc, mirror):
    def get(k):
        rr = r + k
        rc = jnp.clip(rr, 0, R - 1)
        v = ref[pl.ds(pl.multiple_of(tc + rc * GW, GW), GW), :]
        if mirror:
            v = v * _inv_count(jnp.full((1, PW), rc, jnp.int32), R)
        return jnp.where((rr >= 0) & (rr < R), v, 0.0)

    return _nested_box(get, mirror)


def _pool_fwd(u, pwbd, ps, tc):
    T = u.shape[0]
    R = (T - tc) // GW

    def body(u_ref, pw_ref, ps_ref, y_ref):
        pw, scale = pw_ref[...], ps_ref[...]
        uc = u_ref[0:tc, :]
        mc = _box_tokens(uc, tc, False) * _inv_count(_iota((tc, 1), 0), tc)
        y_ref[0:tc, :] = _dot(mc - uc, pw, NN) * scale
        inv_c = _inv_count(_iota((GW, 1), 0), GW)

        def row(r, carry):
            rs = _pool_rows(u_ref, r, R, tc, False) * _inv_count(jnp.full((1, PW), r, jnp.int32), R)
            m = _box_tokens(rs, GW, False) * inv_c
            sl = pl.ds(pl.multiple_of(tc + r * GW, GW), GW)
            y_ref[sl, :] = _dot(m - u_ref[sl, :], pw, NN) * scale
            return carry

        lax.fori_loop(0, R, row, 0)

    return pl.pallas_call(
        body, name="pool_fwd", out_shape=_S((T, PW)),
        compiler_params=pltpu.CompilerParams(vmem_limit_bytes=VMEM_MB << 20))(u, pwbd, ps)


def _pool_bwd(u, pwbd, ps, dy, tc):
    T = u.shape[0]
    R = (T - tc) // GW

    def body(u_ref, pw_ref, ps_ref, dy_ref, du_ref, dpw_ref, dps_ref, dd_ref):
        pw, scale = pw_ref[...], ps_ref[...]
        dpw_ref[...] = jnp.zeros_like(dpw_ref)
        dps_ref[...] = jnp.zeros_like(dps_ref)

        def back(d, dy_):
            dz = dy_ * scale
            dpw_ref[...] += _dot(d, dz, TN)
            dps_ref[...] += _colsum(dy_ * _dot(d, pw, NN))
            return _dot(dz, pw, NT)

        uc = u_ref[0:tc, :]
        inv_cc = _inv_count(_iota((tc, 1), 0), tc)
        ddc = back(_box_tokens(uc, tc, False) * inv_cc - uc, dy_ref[0:tc, :])
        du_ref[0:tc, :] = _box_tokens(ddc * inv_cc, tc, True) - ddc
        inv_c = _inv_count(_iota((GW, 1), 0), GW)

        def row1(r, carry):
            rs = _pool_rows(u_ref, r, R, tc, False) * _inv_count(jnp.full((1, PW), r, jnp.int32), R)
            m = _box_tokens(rs, GW, False) * inv_c
            sl = pl.ds(pl.multiple_of(tc + r * GW, GW), GW)
            dd_ref[sl, :] = back(m - u_ref[sl, :], dy_ref[sl, :])
            return carry

        lax.fori_loop(0, R, row1, 0)

        def row2(r, carry):
            t1 = _pool_rows(dd_ref, r, R, tc, True)
            sl = pl.ds(pl.multiple_of(tc + r * GW, GW), GW)
            du_ref[sl, :] = _box_tokens(t1 * inv_c, GW, True) - dd_ref[sl, :]
            return carry

        lax.fori_loop(0, R, row2, 0)

    return pl.pallas_call(
        body, name="pool_bwd", out_shape=[_S((T, PW)), _S((PW, PW)), _S((1, PW))],
        scratch_shapes=[pltpu.VMEM((T, PW), F32)],
        compiler_params=pltpu.CompilerParams(vmem_limit_bytes=VMEM_MB << 20))(u, pwbd, ps, dy)


def _scan_consts():
    i = np.arange(CH)
    lower = (i[:, None] >= i[None, :]).astype(np.float32)
    return jnp.asarray(np.stack([lower, lower.T])), jnp.asarray(np.stack([lower.T, lower]))


def _gates(pab, al, dtb, csum):
    sp_in = pab + dtb
    sp = jnp.maximum(sp_in, 0.0) + jnp.log(1.0 + jnp.exp(-jnp.abs(sp_in)))
    nexp = -jnp.exp(al)
    gm = nexp * sp
    return gm, jax.nn.sigmoid(pab), _dot(csum, gm, NN, hi=True), sp_in, nexp


def _lane_col(m, j):
    return jnp.sum(jnp.where(_iota(m.shape, 1) == j, m, 0.0), axis=1, keepdims=True)


def _hs(h):
    return slice(h * DH, (h + 1) * DH)


HS = NH * CH
X3 = "x3"


def _stack(x, base=0):
    return jnp.concatenate([x[:, base + h * DH:base + (h + 1) * DH] for h in range(NH)], axis=0)


def _heads(st):
    return [st[h * CH:(h + 1) * CH] for h in range(NH)]


def _rowsum(a):
    return jnp.sum(a, axis=1, keepdims=True)


def _row_of(col):
    e0 = (_iota((8, LANES), 1) == 0).astype(F32)
    return _dot(e0, jnp.broadcast_to(col, (HS, LANES)), NT, hi=True)[0:1, :]


def _inverses(nms):
    eye = (_iota((HS, HS), 0) == _iota((HS, HS), 1)).astype(F32)
    x0s, mps = [eye + nm for nm in nms], list(nms)
    for _ in range(5):
        mps = [_dot(mp, mp, NN) for mp in mps]
        x0s = [x0 + _dot(x0, mp, NN) for x0, mp in zip(x0s, mps)]
    rs = [eye - _dot(eye - nm, x0, NN, hi=X3) for nm, x0 in zip(nms, x0s)]
    return [x0 + _dot(x0, r, NN) for x0, r in zip(x0s, rs)]


def _dn_chunk_pre(qkv, pab, al, dtb, csum_d, d):
    gm, bm, gcm, sp_in, nexp = _gates(pab, al, dtb, csum_d)
    gc = jnp.concatenate([_lane_col(gcm, d * NH + h) for h in range(NH)], axis=0)
    beta = jnp.concatenate([_lane_col(bm, 8 + d * NH + h) for h in range(NH)], axis=0)
    q, k, v = _stack(qkv, 0), _stack(qkv, DN), _stack(qkv, 2 * DN)
    ii, jj = _iota((HS, HS), 0), _iota((HS, HS), 1)
    sh = CH.bit_length() - 1
    same = (ii >> sh) == (jj >> sh)
    incl = same & ((ii >= jj) if d == 0 else (ii <= jj))
    strict = same & ((ii > jj) if d == 0 else (ii < jj))
    Di = jnp.where(incl, jnp.exp(jnp.where(incl, gc - _row_of(gc), 0.0)), 0.0)
    Ds = jnp.where(strict, Di, 0.0)
    kb = k * beta
    kk = _dot(kb, k, NT)
    return dict(q=q, k=k, v=v, beta=beta, gc=gc, gm=gm, bm=bm, sp_in=sp_in, nexp=nexp, Di=Di, Ds=Ds, strict=strict,
                last=CH - 1 if d == 0 else 0, kb=kb, kk=kk)


def _dn_chunk_post(c, tm, uw=None):
    q, k, v, beta, gc, kb, last = (c[n] for n in ("q", "k", "v", "beta", "gc", "kb", "last"))
    E = jnp.exp(gc)
    gls = [gc[h * CH + last:h * CH + last + 1, :] for h in range(NH)]
    xs = jnp.exp(jnp.concatenate([jnp.broadcast_to(g, (CH, 1)) for g in gls], axis=0) - gc)
    qk = _dot(q, k, NT)
    if uw is None:
        both = _dot(tm, jnp.concatenate([v * beta, kb * E], axis=1), NN, hi=X3)
        uw = both[:, :DH], both[:, DH:]
    u, w = uw
    return dict(c, tm=tm, E=E, gls=gls, xs=xs, qk=qk, u=u, w=w, ks=k * xs, qd=q * E, aqk=qk * c["Di"])


def _dn_chunks_bwd_math(cs, Ss, dS2s, dos, vns, dvns):
    I = range(len(cs))
    q, k, v, beta, tm, E, xs, kb, u, w = ([c[n] for c in cs] for n in ("q", "k", "v", "beta", "tm", "E", "xs", "kb", "u", "w"))
    cat = lambda parts: jnp.concatenate(parts, axis=0)

    def per_head(a, states):
        full = _dot(a, cat(states), NT)
        return cat([full[h * CH:(h + 1) * CH, _hs(h)] for h in range(NH)])

    dqd = [per_head(dos[i], Ss[i]) for i in I]
    dks = [per_head(vns[i], dS2s[i]) for i in I]
    dw = [-per_head(dvns[i], Ss[i]) for i in I]
    daqk = [_dot(dos[i], vns[i], NT) for i in I]
    drbw = [_dot(tm[i], jnp.concatenate([dvns[i], dw[i]], axis=1), TN, hi=X3) for i in I]
    drb, drw = [x[:, :DH] for x in drbw], [x[:, DH:] for x in drbw]
    uw = [jnp.concatenate([u[i], w[i].astype(F32)], axis=1) for i in I]
    dA = [jnp.where(cs[i]["strict"], -_dot(drbw[i], uw[i], NT), 0.0) for i in I]
    dM1 = [dA[i] * cs[i]["Ds"] for i in I]
    dM2 = [daqk[i] * cs[i]["Di"] for i in I]
    dM = [cat([dM1[i], dM2[i]]) for i in I]
    dMk = [_dot(dM[i], k[i], NN) for i in I]
    dkb = [dMk[i][:HS] + drw[i] * E[i] for i in I]
    dq = [dMk[i][HS:] + dqd[i] * E[i] for i in I]
    dk = [_dot(dM[i], cat([kb[i], q[i]]), TN) + dks[i] * xs[i] for i in I]
    on_diag = _iota((HS, HS), 0) == _iota((HS, HS), 1)
    out = []
    for i in I:
        G = dM1[i] * cs[i]["kk"] + dM2[i] * cs[i]["qk"]
        col = _rowsum(jnp.where(on_diag, jnp.broadcast_to(_colsum(G), (HS, HS)), 0.0))
        dxx = _rowsum(dks[i] * k[i]) * xs[i]
        dgc = _rowsum(G) - col + (_rowsum(dqd[i] * q[i]) + _rowsum(drw[i] * kb[i])) * E[i] - dxx
        at_last = _iota((CH, 1), 0) == cs[i]["last"]
        ends = []
        for h in range(NH):
            dgl = (_colsum(_rowsum(Ss[i][h] * dS2s[i][h])) * jnp.exp(cs[i]["gls"][h])
                   + _colsum(dxx[h * CH:(h + 1) * CH]))
            ends.append(jnp.where(at_last, dgl, 0.0))
        dbeta = _rowsum(drb[i] * v[i]) + _rowsum(dkb[i] * k[i])
        out.append((dq[i], dk[i] + dkb[i] * beta[i], drb[i] * beta[i], dgc + cat(ends), dbeta))
    return out


def _chunk_group(n, want=2):
    g = want
    while n % g:
        g //= 2
    return g


def _dn_chunks_fwd(qkv, pab, alr, dtr, rider=None):
    T = qkv.shape[0]
    n = T // CH
    G = _chunk_group(n, 4)
    csum, _ = _scan_consts()

    def body(q_ref, p_ref, cs_ref, al_ref, dt_ref, *outs):
        inst = [(g, d) for g in range(G) for d in range(2)]
        pres = [_dn_chunk_pre(q_ref[g * CH:(g + 1) * CH, :], p_ref[g * CH:(g + 1) * CH, :], al_ref[...], dt_ref[...],
                              cs_ref[d], d) for g, d in inst]
        tms = _inverses([-(p["kk"] * p["Ds"]) for p in pres])
        for (g, d), pre, tm in zip(inst, pres, tms):
            rows = slice(g * HS, (g + 1) * HS)
            u_ref, w_ref, ks_ref, qd_ref, aqk_ref, eg_ref, tm_ref = outs[7 * d:7 * d + 7]
            c = _dn_chunk_post(pre, tm)
            tm_ref[rows, :] = tm
            u_ref[rows, :] = c["u"]
            w_ref[rows, :] = c["w"].astype(BF16)
            ks_ref[rows, :] = c["ks"].astype(BF16)
            qd_ref[rows, :] = c["qd"].astype(BF16)
            aqk_ref[rows, :] = c["aqk"].astype(BF16)
            egs = [jnp.broadcast_to(jnp.exp(gl), (1, LANES)) for gl in c["gls"]]
            eg_ref[g * 8:(g + 1) * 8, :] = jnp.concatenate(egs + [jnp.zeros((8 - NH, LANES), F32)], axis=0)

    st = lambda w_: pl.BlockSpec((G * HS, w_), lambda i: (i, 0))
    one = [st(DH)] * 4 + [st(HS), pl.BlockSpec((G * 8, LANES), lambda i: (i, 0)), st(HS)]
    shp = [_S((n * HS, DH)), _S((n * HS, DH), BF16), _S((n * HS, DH), BF16), _S((n * HS, DH), BF16),
           _S((n * HS, HS), BF16), _S((n * 8, LANES)), _S((n * HS, HS))]
    res = _call(
        body, "dn_chunks_fwd", (n // G,),
        [_rows(G * CH, 3 * DN), _rows(G * CH, LANES), _full((2, CH, CH)), _full((1, LANES)), _full((1, LANES))],
        one * 2, shp * 2, rider=rider)(qkv, pab, csum, alr, dtr)
    outs, riding = (res, None) if rider is None else res
    parts = tuple(outs[:7]), tuple(outs[7:])
    return parts if rider is None else (parts, riding)


def _scan_plan(n, ncx):
    sg = 2 if n % 2 == 0 and ncx % 2 == 0 else 1
    ng, ncg = n // sg, ncx // sg
    return sg, ((lambda i: i), (lambda i: jnp.where(i < ncg, ncg - 1 - i, ng - 1 - (i - ncg))))


def _scan_specs(order, sg):
    st = lambda w_: pl.BlockSpec((sg * HS, w_), lambda i: (order(i), 0))
    return dict(st=st(DH), aqk=st(HS), eg=pl.BlockSpec((sg * 8, LANES), lambda i: (order(i), 0)),
                tok=pl.BlockSpec((sg * CH, DN), lambda i: (order(i), 0)),
                state=pl.BlockSpec((sg, DN, DH), lambda i: (order(i), 0, 0)))


def _scan_fwd(parts, T, tc, rider=None):
    n = T // CH
    sg, orders = _scan_plan(n, tc // CH)

    def body(*refs):
        S_f, S_b = refs[-2:]

        @pl.when(pl.program_id(0) == 0)
        def _():
            S_f[...] = jnp.zeros_like(S_f)
            S_b[...] = jnp.zeros_like(S_b)

        for g in range(sg):
            for d, S in enumerate((S_f, S_b)):
                u_ref, w_ref, ks_ref, qd_ref, aqk_ref, eg_ref = refs[6 * d:6 * d + 6]
                o_ref, ss_ref, vn_ref = refs[12 + 3 * d:15 + 3 * d]
                k = g if d == 0 else sg - 1 - g
                rows = slice(k * HS, (k + 1) * HS)
                ss_ref[k] = S[...]
                Sh = [S[_hs(h), :] for h in range(NH)]
                wh, ksh, qdh = _heads(w_ref[rows, :]), _heads(ks_ref[rows, :]), _heads(qd_ref[rows, :])
                vn = u_ref[rows, :] - jnp.concatenate([_dot(wh[h], Sh[h], NN) for h in range(NH)], axis=0)
                vn_ref[rows, :] = vn
                av, vnh = _heads(_dot(aqk_ref[rows, :], vn, NN)), _heads(vn)
                for h in range(NH):
                    o_ref[k * CH:(k + 1) * CH, _hs(h)] = _dot(qdh[h], Sh[h], NN) + av[h]
                    S[_hs(h), :] = Sh[h] * eg_ref[k * 8 + h:k * 8 + h + 1, :] + _dot(ksh[h], vnh[h], TN)

    ins, outs, shp = [], [], []
    for d in range(2):
        sp = _scan_specs(orders[d], sg)
        ins += [sp["st"]] * 4 + [sp["aqk"], sp["eg"]]
        outs += [sp["tok"], sp["state"], sp["st"]]
        shp += [_S((T, DN)), _S((n, DN, DH)), _S((n * HS, DH))]
    res = _call(body, "scan_fwd", (n // sg,), ins, outs, shp,
                scratch=[pltpu.VMEM((DN, DH), F32), pltpu.VMEM((DN, DH), F32)], rider=rider)(*parts[0][:6], *parts[1][:6])
    res, riding = (res, None) if rider is None else res
    out = tuple(res[:3]), tuple(res[3:])
    return out if rider is None else (out, riding)


def _scan_bwd(do, parts, tc):
    T = do.shape[0]
    n = T // CH
    sg, fwd_orders = _scan_plan(n, tc // CH)
    orders = [lambda s, f=f: f(n // sg - 1 - s) for f in fwd_orders]

    def body(*refs):
        dS_f, dS_b = refs[-2:]

        @pl.when(pl.program_id(0) == 0)
        def _():
            dS_f[...] = jnp.zeros_like(dS_f)
            dS_b[...] = jnp.zeros_like(dS_b)

        for g in range(sg):
            for d, dS in enumerate((dS_f, dS_b)):
                do_ref, w_ref, ks_ref, qd_ref, aqk_ref, eg_ref = refs[6 * d:6 * d + 6]
                dvn_ref, dss_ref = refs[12 + 2 * d:14 + 2 * d]
                k = sg - 1 - g if d == 0 else g
                rows = slice(k * HS, (k + 1) * HS)
                dss_ref[k] = dS[...]
                dSh = [dS[_hs(h), :] for h in range(NH)]
                wh, ksh, qdh = _heads(w_ref[rows, :]), _heads(ks_ref[rows, :]), _heads(qd_ref[rows, :])
                do_st = _stack(do_ref[k * CH:(k + 1) * CH, :])
                dvn = (_dot(aqk_ref[rows, :], do_st, TN)
                       + jnp.concatenate([_dot(ksh[h], dSh[h], NN) for h in range(NH)], axis=0))
                dvn_ref[rows, :] = dvn
                doh, dvnh = _heads(do_st), _heads(dvn)
                for h in range(NH):
                    dS[_hs(h), :] = (_dot(qdh[h], doh[h], TN) + dSh[h] * eg_ref[k * 8 + h:k * 8 + h + 1, :]
                                     - _dot(wh[h], dvnh[h], TN))

    ins, outs, shp, args = [], [], [], []
    for d in range(2):
        sp = _scan_specs(orders[d], sg)
        ins += [sp["tok"]] + [sp["st"]] * 3 + [sp["aqk"], sp["eg"]]
        outs += [sp["st"], sp["state"]]
        shp += [_S((n * HS, DH)), _S((n, DN, DH))]
        args += [do, *parts[d][1:6]]
    res = _call(body, "scan_bwd", (n // sg,), ins, outs, shp,
                scratch=[pltpu.VMEM((DN, DH), F32), pltpu.VMEM((DN, DH), F32)])(*args)
    return tuple(res[:2]), tuple(res[2:])


def _dn_chunks_bwd(qkv, pab, alr, dtr, do, fwd, bwd, rider=None):
    T = qkv.shape[0]
    n = T // CH
    G = _chunk_group(n)
    csum, csum_t = _scan_consts()

    def body(q_ref, p_ref, do_ref, cs_ref, cst_ref, al_ref, dt_ref, *refs):
        dq_refs, dp_refs, acc_ref = refs[14:16], refs[16:18], refs[18]

        @pl.when(pl.program_id(0) == 0)
        def _():
            acc_ref[...] = jnp.zeros_like(acc_ref)

        lane = _iota((CH, LANES), 1)
        inst = [(g, d) for g in range(G) for d in range(2)]
        cs, Ss, dS2s, dos, vns, dvns = [], [], [], [], [], []
        for g, d in inst:
            tok, rows = slice(g * CH, (g + 1) * CH), slice(g * HS, (g + 1) * HS)
            vn_ref, dvn_ref, ss_ref, dss_ref, tm_ref, u_ref, w_ref = refs[7 * d:7 * d + 7]
            cs.append(_dn_chunk_post(
                _dn_chunk_pre(q_ref[tok, :], p_ref[tok, :], al_ref[...], dt_ref[...], cs_ref[d], d), tm_ref[rows, :],
                uw=(u_ref[rows, :], w_ref[rows, :])))
            Ss.append([ss_ref[g, _hs(h), :] for h in range(NH)])
            dS2s.append([dss_ref[g, _hs(h), :] for h in range(NH)])
            dos.append(_stack(do_ref[tok, :]))
            vns.append(vn_ref[rows, :])
            dvns.append(dvn_ref[rows, :])
        for (g, d), c, (dq, dk, dv, dgc, dbeta) in zip(inst, cs, _dn_chunks_bwd_math(cs, Ss, dS2s, dos, vns, dvns)):
            tok = slice(g * CH, (g + 1) * CH)
            dgcm = jnp.zeros((CH, LANES), F32)
            dbm = jnp.zeros((CH, LANES), F32)
            for h, (a, b_, c_, e, f) in enumerate(zip(*map(_heads, (dq, dk, dv, dgc, dbeta)))):
                dq_refs[d][tok, _hs(h)] = a
                dq_refs[d][tok, _hs(NH + h)] = b_
                dq_refs[d][tok, _hs(2 * NH + h)] = c_
                dgcm = jnp.where(lane == d * NH + h, e, dgcm)
                dbm = jnp.where(lane == 8 + d * NH + h, f, dbm)
            dgm = _dot(cst_ref[d], dgcm, NN, hi=True)
            dsp = dgm * c["nexp"] * jax.nn.sigmoid(c["sp_in"])
            dp_refs[d][tok, :] = dsp + dbm * c["bm"] * (1.0 - c["bm"])
            acc_ref[0:1, :] += _colsum(dgm * c["gm"])
            acc_ref[1:2, :] += _colsum(dsp)

    st = pl.BlockSpec((G * HS, DH), lambda i: (i, 0))
    state = pl.BlockSpec((G, DN, DH), lambda i: (i, 0, 0))
    return _call(
        body, "dn_chunks_bwd", (n // G,),
        [_rows(G * CH, 3 * DN), _rows(G * CH, LANES), _rows(G * CH, DN), _full((2, CH, CH)), _full((2, CH, CH)),
         _full((1, LANES)), _full((1, LANES))]
        + [st, st, state, state, pl.BlockSpec((G * HS, HS), lambda i: (i, 0)), st, st] * 2,
        [_rows(G * CH, 3 * DN)] * 2 + [_rows(G * CH, LANES)] * 2 + [_full((8, LANES))],
        [_S((T, 3 * DN))] * 2 + [_S((T, LANES))] * 2 + [_S((8, LANES))], rider=rider)(
            qkv, pab, do, csum, csum_t, alr, dtr, *fwd, *bwd)


def _head_out(o, z, g):
    on = o * lax.rsqrt(jnp.mean(o * o, axis=-1, keepdims=True) + EPS) * g
    return on * _silu(z)


def _mix_branches(of_ref, ob_ref, z_ref, yp_ref, ys_ref, pg_ref, gdn_ref, wa_ref, wb_ref, wc_ref):
    ons, ya = [], None
    for h in range(NH):
        on = _head_out(of_ref[:, _hs(h)] + ob_ref[:, _hs(h)], z_ref[:, _hs(h)], gdn_ref[...])
        t = _dot(on, wa_ref[_hs(h), :], NN)
        ya = t if ya is None else ya + t
        ons.append(on)
    ys = [ya, _dot(yp_ref[...], wb_ref[...], NN), _dot(ys_ref[...], wc_ref[...], NN)]
    sg = [jax.nn.sigmoid(pg_ref[:, k * D:(k + 1) * D]) for k in range(3)]
    return ons, ys, sg


def _mix_fwd(X, of, ob, z, yp, ys, pg, mv, gdn, wa, wb, wc, wo, tc, tt):
    T = X.shape[0]

    def body(x_ref, of_ref, ob_ref, z_ref, yp_ref, ys_ref, pg_ref, mv_ref, gdn_ref, wa_ref, wb_ref, wc_ref, wo_ref,
             x1_ref):
        _, yb, sg = _mix_branches(of_ref, ob_ref, z_ref, yp_ref, ys_ref, pg_ref, gdn_ref, wa_ref, wb_ref, wc_ref)
        mix = _dot(sg[0] * yb[0] + sg[1] * yb[1] + sg[2] * yb[2], wo_ref[...], NN)
        _, gate = _stream_rows(mv_ref, pl.program_id(0), tt, tc, 2)
        x1_ref[...] = x_ref[...] + gate * mix

    return _call(
        body, "mix_fwd", (T // tt,),
        [_rows(tt, D), _rows(tt, DN), _rows(tt, DN), _rows(tt, DN), _rows(tt, PW), _rows(tt, PW), _rows(tt, 3 * D),
         _full((8, D)), _full((1, DH)), _full(wa.shape), _full(wb.shape), _full(wc.shape), _full(wo.shape)],
        _rows(tt, D), _S((T, D)))(X, of, ob, z, yp, ys, pg, mv, gdn, wa, wb, wc, wo)


def _mix_bwd(dx1, of, ob, z, yp, ys, pg, mv, gdn, wa, wb, wc, wo, tc, tt, rider=None):
    T = dx1.shape[0]

    def body(dx_ref, of_ref, ob_ref, z_ref, yp_ref, ys_ref, pg_ref, mv_ref, gdn_ref, wa_ref, wb_ref, wc_ref, wo_ref,
             do_ref, dz_ref, dyp_ref, dys_ref, dpg_ref, dwa_ref, dwb_ref, dwc_ref, dwo_ref, dgdn_ref, dm_ref):
        i = pl.program_id(0)

        @pl.when(i == 0)
        def _():
            for r in (dwa_ref, dwb_ref, dwc_ref, dwo_ref, dgdn_ref, dm_ref):
                r[...] = jnp.zeros_like(r)

        ons, yb, sg = _mix_branches(of_ref, ob_ref, z_ref, yp_ref, ys_ref, pg_ref, gdn_ref, wa_ref, wb_ref, wc_ref)
        ymix = sg[0] * yb[0] + sg[1] * yb[1] + sg[2] * yb[2]
        isc, gate = _stream_rows(mv_ref, i, tt, tc, 2)
        dx = dx_ref[...]
        dmix = dx * gate
        _acc_stream(dm_ref, 2, isc, dx * _dot(ymix, wo_ref[...], NN))
        dwo_ref[...] += _dot(ymix, dmix, TN)
        dymix = _dot(dmix, wo_ref[...], NT)
        dyb = []
        for k in range(3):
            dyb.append(dymix * sg[k])
            dpg_ref[:, k * D:(k + 1) * D] = dymix * yb[k] * sg[k] * (1.0 - sg[k])
        dwb_ref[...] += _dot(yp_ref[...], dyb[1], TN)
        dwc_ref[...] += _dot(ys_ref[...], dyb[2], TN)
        dyp_ref[...] = _dot(dyb[1], wb_ref[...], NT)
        dys_ref[...] = _dot(dyb[2], wc_ref[...], NT)
        dg = jnp.zeros((1, DH), F32)
        for h in range(NH):
            dwa_ref[_hs(h), :] += _dot(ons[h], dyb[0], TN)
            don = _dot(dyb[0], wa_ref[_hs(h), :], NT)
            _, vjp = jax.vjp(_head_out, of_ref[:, _hs(h)] + ob_ref[:, _hs(h)], z_ref[:, _hs(h)], gdn_ref[...])
            do_h, dz_h, dg_h = vjp(don)
            do_ref[:, _hs(h)] = do_h
            dz_ref[:, _hs(h)] = dz_h
            dg = dg + dg_h
        dgdn_ref[...] += dg

    return _call(
        body, "mix_bwd", (T // tt,),
        [_rows(tt, D), _rows(tt, DN), _rows(tt, DN), _rows(tt, DN), _rows(tt, PW), _rows(tt, PW), _rows(tt, 3 * D),
         _full((8, D)), _full((1, DH)), _full(wa.shape), _full(wb.shape), _full(wc.shape), _full(wo.shape)],
        [_rows(tt, DN), _rows(tt, DN), _rows(tt, PW), _rows(tt, PW), _rows(tt, 3 * D),
         _full(wa.shape), _full(wb.shape), _full(wc.shape), _full(wo.shape), _full((1, DH)), _full((8, D))],
        [_S((T, DN)), _S((T, DN)), _S((T, PW)), _S((T, PW)), _S((T, 3 * D)),
         _S(wa.shape), _S(wb.shape), _S(wc.shape), _S(wo.shape), _S((1, DH)), _S((8, D))], rider=rider)(
            dx1, of, ob, z, yp, ys, pg, mv, gdn, wa, wb, wc, wo)


def _ffn_fwd(X1, mv, g, wgu, wd, tc, tt):
    T = X1.shape[0]

    def body(x_ref, mv_ref, g_ref, wgu_ref, wd_ref, x2_ref, ff_ref):
        i = pl.program_id(0)
        _, sh = _stream_rows(mv_ref, i, tt, tc, 0)
        _, sc = _stream_rows(mv_ref, i, tt, tc, 1)
        _, gate = _stream_rows(mv_ref, i, tt, tc, 2)
        x = x_ref[...]
        gu = _dot(_modulate(x, g_ref[...], sh, sc), wgu_ref[...], NN)
        ff = _dot(_silu(gu[:, :DFF]) * gu[:, DFF:], wd_ref[...], NN)
        ff_ref[...] = ff
        x2_ref[...] = x + gate * ff

    return _call(
        body, "ffn_fwd", (T // tt,),
        [_rows(tt, D), _full((8, D)), _full((1, D)), _full(wgu.shape), _full(wd.shape)],
        [_rows(tt, D)] * 2, [_S((T, D))] * 2)(X1, mv, g, wgu, wd)


def _ffn_bwd(X1, ff, dx2, mv, g, wgu, wd, tc, tt, rider=None):
    T = X1.shape[0]

    def body(x_ref, ff_ref, dx2_ref, mv_ref, g_ref, wgu_ref, wd_ref, dx1_ref, ht_ref, dgu_ref, actt_ref, dff_ref, dg_ref,
             dm_ref):
        i = pl.program_id(0)
        isc, sh = _stream_rows(mv_ref, i, tt, tc, 0)
        _, sc = _stream_rows(mv_ref, i, tt, tc, 1)
        _, gate = _stream_rows(mv_ref, i, tt, tc, 2)
        x, dx2_ = x_ref[...], dx2_ref[...]
        h, vjp = jax.vjp(_modulate, x, g_ref[...], sh, sc)
        ht_ref[...] = h.T.astype(BF16)
        gu = jnp.dot(h.astype(BF16), wgu_ref[...], preferred_element_type=F32)
        ga, up = gu[:, :DFF], gu[:, DFF:]
        sg = jax.nn.sigmoid(ga)
        actt_ref[...] = (ga * sg * up).T.astype(BF16)
        dff = dx2_ * gate
        dff_ref[...] = dff.astype(BF16)
        dact = _dot(dff, wd_ref[...], NT)
        dga = (dact * up * (sg * (1.0 + ga * (1.0 - sg)))).astype(BF16)
        dup = (dact * ga * sg).astype(BF16)
        dgu_ref[:, :DFF] = dga
        dgu_ref[:, DFF:] = dup
        dh = _dot(dga, wgu_ref[:, :DFF], NT) + _dot(dup, wgu_ref[:, DFF:], NT)
        dx, dg, dsh, dsc = vjp(dh)
        dx1_ref[...] = dx2_ + dx

        @pl.when(i == 0)
        def _():
            dg_ref[...] = jnp.zeros_like(dg_ref)
            dm_ref[...] = jnp.zeros_like(dm_ref)

        dg_ref[...] += dg
        _acc_stream(dm_ref, 0, isc, dsh)
        _acc_stream(dm_ref, 1, isc, dsc)
        _acc_stream(dm_ref, 2, isc, dx2_ * ff_ref[...])

    return _call(
        body, "ffn_bwd", (T // tt,),
        [_rows(tt, D), _rows(tt, D), _rows(tt, D), _full((8, D)), _full((1, D)), _full(wgu.shape), _full(wd.shape)],
        [_rows(tt, D), _cols(D, tt), _rows(tt, 2 * DFF), _cols(DFF, tt), _rows(tt, D), _full((1, D)), _full((8, D))],
        [_S((T, D)), _S((D, T), BF16), _S((T, 2 * DFF), BF16), _S((DFF, T), BF16), _S((T, D), BF16),
         _S((1, D)), _S((8, D))], rider=rider)(X1, ff, dx2, mv, g, wgu, wd)


def _rms(x, g):
    return x * lax.rsqrt(jnp.mean(x * x, axis=-1, keepdims=True) + EPS) * g


def _loss_head(X2, tgt, gf, tc):
    T = X2.shape[0]

    def body(x_ref, t_ref, g_ref, dx_ref, loss_ref, dg_ref):
        i = pl.program_id(0)

        @pl.when(i == 0)
        def _():
            dx_ref[...] = jnp.zeros_like(dx_ref)
            loss_ref[...] = jnp.zeros_like(loss_ref)
            dg_ref[...] = jnp.zeros_like(dg_ref)

        @pl.when(i > 0)
        def _():
            y, vjp = jax.vjp(_rms, x_ref[...], g_ref[...])
            err = y - t_ref[...]
            dx, dg = vjp(err * (1.0 / D))
            dx_ref[...] = dx
            dg_ref[...] += dg
            loss_ref[...] += (0.5 / D) * jnp.sum(jnp.sum(err * err, axis=1, keepdims=True), axis=0, keepdims=True)

    return _call(
        body, "loss_head", (T // tc,),
        [_rows(tc, D), pl.BlockSpec((tc, D), lambda i: (jnp.maximum(i - 1, 0), 0)), _full((1, D))],
        [_rows(tc, D), _full((8, LANES)), _full((1, D))],
        [_S((T, D)), _S((8, LANES)), _S((1, D))])(X2, tgt, gf)


def _block_diag(pw):
    g, n = pw.shape[0], pw.shape[1]
    out = jnp.zeros((g * n, g * n), pw.dtype)
    for k in range(g):
        out = lax.dynamic_update_slice(out, pw[k], (k * n, k * n))
    return out


IN_TRUE = tuple(IN_BOUNDS[k + 1] - IN_BOUNDS[k] for k in range(8))


def _overlaps(widths, cw):
    starts = np.cumsum([0] + list(widths))
    out = []
    for k in range(N_DEV):
        for i in range(len(widths)):
            a, b = max(k * cw, starts[i]), min((k + 1) * cw, starts[i + 1])
            if a < b:
                out.append((k, i, int(a - k * cw), int(a - starts[i]), int(b - a)))
    return out


def _shards_to_cols(gathered, l, widths, padded):
    nd, _, R, cw = gathered.shape
    tr = _shard_rows(R)

    def body(s_ref, *o_refs):
        for i, o_ref in enumerate(o_refs):
            if padded[i] > widths[i]:
                o_ref[...] = jnp.zeros_like(o_ref)
        for k, i, so, go, n in _overlaps(widths, cw):
            o_refs[i][:, go:go + n] = s_ref[k, 0, :, so:so + n].astype(BF16)

    return _call(
        body, "shards_to_cols", (R // tr,), [pl.BlockSpec((nd, 1, tr, cw), lambda i: (0, l, i, 0))],
        [_rows(tr, p) for p in padded], [_S((R, p), BF16) for p in padded])(gathered)


def _cols_to_shards(groups, widths, cw):
    R = groups[0].shape[0]
    tr = _shard_rows(R)

    def body(*refs):
        o_ref = refs[-1]
        for k, i, so, go, n in _overlaps(widths, cw):
            o_ref[k % 2, k // 2, 0, :, so:so + n] = refs[i][:, go:go + n]

    return _call(
        body, "cols_to_shards", (R // tr,), [_rows(tr, g.shape[1]) for g in groups],
        pl.BlockSpec((2, N_CHIP, 1, tr, cw), lambda i: (0, 0, 0, i, 0)), _S((2, N_CHIP, 1, R, cw)))(*groups)


def _mod_rows(mods_l, k0):
    rows = [mods_l[s, (k0 + k) * D:(k0 + k + 1) * D] for s in (0, 1) for k in range(3)]
    return jnp.stack(rows + [jnp.zeros((D,), F32)] * 2)


def _lane_row(v8):
    return jnp.pad(v8.reshape(1, 8), ((0, 0), (0, LANES - 8)))


LAYERED = ("w_in", "w_br_a", "w_br_b", "w_br_c", "w_o", "w_gu", "w_down")
LATE = ("w_br_a", "w_br_b", "w_br_c", "w_o", "w_gu", "w_down")
FFN_W = ("w_gu", "w_down")


def _device_step(x, c, ctx, tgt, wts, tt, comm=None):
    tc = ctx.shape[0]
    X = jnp.concatenate([ctx, x], axis=0)
    if comm is None:
        row = 0
        cc = jnp.concatenate([c, jnp.zeros((CTX_ROW - 1, D), F32), wts["c_ctx"][None, :],
                              jnp.zeros((CC_ROWS - CTX_ROW - 1, D), F32)], axis=0)
        w_ada = wts["w_ada"].astype(BF16)
        mods16 = _mod_fwd(cc, w_ada, wts["b_ada"].reshape(NL, 1, 6 * D))
    else:
        row, mods16 = comm.adaln_fwd(c, wts["c_ctx"])
    mods = jnp.stack([mods16[:, CTX_ROW], lax.dynamic_index_in_dim(mods16, row, 1, keepdims=False)], axis=1)

    saved = []
    for l in range(NL):
        ws = _shards_to_cols(wts["w_in"][l], 0, IN_TRUE, IN_WIDTHS)
        mv1, mv2 = _mod_rows(mods[l], 0), _mod_rows(mods[l], 3)
        g1, g2 = wts["norm1_g"][l][None, :], wts["norm2_g"][l][None, :]
        cw, scw = wts["dn_conv_w"][l], wts["sc_conv_w"][l]
        alr, dtr = _lane_row(wts["dn_a_log"][l]), _lane_row(wts["dn_dt_bias"][l])
        gdn = wts["dn_norm_g"][l][None, :]
        pwbd, ps = _block_diag(wts["pool_w"][l]), wts["pool_scale"][l][None, :]
        hb, pq, pz, pab, pp, sx, sb, sc_, pg = _inproj_fwd(X, mv1, g1, ws, tc, tt)
        qkv = _dnprep_fwd(pq, cw, tc, tt)
        if comm is not None and l == 0:
            parts, riding = _dn_chunks_fwd(qkv, pab, alr, dtr, rider=comm.late_weights_chips())
            ((of, ssf, vnf), (ob, ssb, vnb)), riding = _scan_fwd(parts, X.shape[0], tc,
                                                                 rider=comm.late_weights_pair(riding))
            late, w_in_1 = comm.late_weights(riding)
            wts = dict(wts, **late, w_in=[wts["w_in"][0], w_in_1])
        else:
            parts = _dn_chunks_fwd(qkv, pab, alr, dtr)
            (of, ssf, vnf), (ob, ssb, vnb) = _scan_fwd(parts, X.shape[0], tc)
        wbr = [_shards_to_cols(wts[k], l, (2 * DFF,), (2 * DFF,))[0] if k == "w_gu" else wts[k][l].astype(BF16)
               for k in LATE]
        yp = _pool_fwd(pp, pwbd, ps, tc)
        ys = _sc_fwd(sx, sb, sc_, scw, tc, tt)
        X1 = _mix_fwd(X, of, ob, pz, yp, ys, pg, mv1, gdn, *wbr[:4], tc, tt)
        X2, ff = _ffn_fwd(X1, mv2, g2, wbr[4], wbr[5], tc, tt)
        saved.append(dict(X=X, X1=X1, ff=ff, ws=ws, wbr=wbr, mv1=mv1, mv2=mv2, g1=g1, g2=g2, cw=cw, scw=scw, alr=alr, dtr=dtr,
                          gdn=gdn, pwbd=pwbd, ps=ps, hb=hb, pq=pq, pz=pz, pab=pab, pp=pp, sx=sx, sb=sb, sc=sc_, pg=pg,
                          qkv=qkv, of=of, ob=ob, ssf=ssf, ssb=ssb, vnf=vnf, vnb=vnb, parts=parts, yp=yp, ys=ys))
        X = X2

    dX, loss, dgf = _loss_head(X, tgt, wts["final_norm_g"][None, :], tc)

    gl = {k: [None] * NL for k in ("w_in", "norm1_g", "norm2_g", "dn_conv_w", "dn_a_log", "dn_dt_bias", "dn_norm_g",
                                   "pool_w", "pool_scale", "sc_conv_w", "w_br_a", "w_br_b", "w_br_c", "w_o", "w_gu",
                                   "w_down")}
    dmods = [None] * NL
    early = None
    for l in reversed(range(NL)):
        s = saved[l]
        hide = comm is not None and l == 0
        res = _ffn_bwd(s["X1"], s["ff"], dX, s["mv2"], s["g2"], s["wbr"][4], s["wbr"][5], tc, tt,
                       rider=comm.grad_pair_rider([gl[k][1] for k in LAYERED]) if hide else None)
        if hide:
            res, got = res
        dx1, h2, dgu, act, dff, dg2, dm2 = res
        gl["w_gu"][l] = _cols_to_shards([_dw(h2, dgu, tt)], (2 * DFF,), 2 * DFF // N_DEV)
        gl["w_down"][l] = _dw(act, dff, tt)
        res = _mix_bwd(dx1, s["of"], s["ob"], s["pz"], s["yp"], s["ys"], s["pg"], s["mv1"], s["gdn"], *s["wbr"][:4], tc,
                       tt, rider=comm.grad_pair_rider([gl[k][0] for k in FFN_W], FFN_W) if hide else None)
        if hide:
            res, got_ffn = res
            chip_rider = comm.grad_chip_rider(got + got_ffn)
        do, dz, dyp, dys, dpg, dwa, dwb, dwc, dwo, dgdn, dmg = res
        dpp, dpw, dps = _pool_bwd(s["pp"], s["pwbd"], s["ps"], dyp, tc)
        dsx, dsb, dsc, dscw = _sc_bwd(s["sx"], s["sb"], s["sc"], s["scw"], dys, tc, tt)
        (dvnf, dssf), (dvnb, dssb) = _scan_bwd(do, s["parts"], tc)
        res = _dn_chunks_bwd(s["qkv"], s["pab"], s["alr"], s["dtr"], do,
                             (s["vnf"], dvnf, s["ssf"], dssf, s["parts"][0][6], *s["parts"][0][:2]),
                             (s["vnb"], dvnb, s["ssb"], dssb, s["parts"][1][6], *s["parts"][1][:2]),
                             rider=chip_rider if hide else None)
        if hide:
            res, early = res
            early = comm.grad_chip_done(early)
        dqf, dqb, dpf, dpb, gacc = res
        dy = _dnprep_bwd_act(s["pq"], s["cw"], dqf, dqb, tc, tt)
        dpq, dcw = _conv_bwd(dy, s["pq"], s["cw"], tc, tt)
        dps_ = [dpq, dz, dpf, dpb, dpp, dsx, dsb, dsc, dpg]
        dp_w = [0, 1, 2, 2, 3, 4, 5, 6, 7]
        dX, dg1, dm1 = _inproj_bwd(s["X"], s["mv1"], s["g1"], s["ws"], dps_, dp_w, dx1, tc, tt)
        dws = [_dw(s["hb"], dp, tt) for dp in (dpq, dz, dpf + dpb, dpp, dsx, dsb, dsc, dpg)]
        gl["w_in"][l] = _cols_to_shards(dws, IN_TRUE, IN_BOUNDS[-1] // N_DEV)
        gl["norm1_g"][l], gl["norm2_g"][l] = dg1[0], dg2[0]
        gl["dn_conv_w"][l], gl["sc_conv_w"][l] = dcw, dscw
        gl["dn_a_log"][l], gl["dn_dt_bias"][l] = gacc[0, :8].reshape(2, NH), gacc[1, :8].reshape(2, NH)
        gl["dn_norm_g"][l] = dgdn[0]
        gl["pool_w"][l] = jnp.stack([dpw[k * GW:(k + 1) * GW, k * GW:(k + 1) * GW] for k in range(4)])
        gl["pool_scale"][l] = dps[0]
        gl["w_br_a"][l], gl["w_br_b"][l], gl["w_br_c"][l], gl["w_o"][l] = dwa, dwb, dwc, dwo
        dm = dm1 + dmg
        cat = lambda r: jnp.concatenate([dm[r], dm[r + 1], dm[r + 2], dm2[r], dm2[r + 1], dm2[r + 2]])
        dmods[l] = jnp.stack([cat(0), cat(3)])

    dmods = jnp.stack(dmods)
    grads = {k: (v if k in LAYERED else jnp.stack(v)) for k, v in gl.items()}
    if comm is None:
        dm16 = jnp.zeros((NL, CC_ROWS, 6 * D), F32).at[:, CTX_ROW].set(dmods[:, 0]).at[:, row].set(dmods[:, 1])
        dwada, dcc = _mod_bwd(cc, w_ada, dm16)
        grads.update(w_ada=dwada, b_ada=dmods[:, 0] + dmods[:, 1], c_ctx=dcc[CTX_ROW])
    else:
        grads.update(comm.adaln_bwd(dmods))
    grads.update(final_norm_g=dgf[0])
    return loss, dX[tc:], grads, early


def _me():
    return lax.axis_index("x"), lax.axis_index("y"), lax.axis_index("c")


def _dev_index(p):
    return 4 * p[0] + 2 * p[1] + p[2]


def _allgather(parts):
    n = len(parts)

    def body(*refs):
        ins, outs = refs[:n], refs[n:2 * n]
        send_sems, recv_sems = refs[2 * n:]
        x, y, c = _me()
        me, sibling = (x, y, c), (x, y, 1 - c)
        chips = [(1 - x, y), (x, 1 - y), (1 - x, 1 - y)]

        def copy(a, k, block, to, src=None):
            dst = outs[a].at[_dev_index(block)]
            return pltpu.make_async_remote_copy(
                src_ref=dst if src is None else src, dst_ref=dst, send_sem=send_sems.at[a, k], recv_sem=recv_sems.at[a, k],
                device_id=to, device_id_type=MESH_ID)

        first, passed = [], []
        for a in range(n):
            first.append(copy(a, 0, me, sibling, src=ins[a]))
            first += [copy(a, 1 + j, me, (*chip, c), src=ins[a]) for j, chip in enumerate(chips)]
        for cp in first:
            cp.start()
        for a in range(n):
            for j, chip in enumerate(chips):
                copy(a, 1 + j, (*chip, c), me).wait_recv()
                passed.append(copy(a, 4 + j, (*chip, c), sibling))
                passed[-1].start()
        for a in range(n):
            copy(a, 0, sibling, me).wait_recv()
            for j, chip in enumerate(chips):
                copy(a, 4 + j, (*chip, 1 - c), me).wait_recv()
        for cp in first + passed:
            cp.wait_send()

    outs = pl.pallas_call(
        body, name="allgather", in_specs=[HBM_SPEC] * n, out_specs=[HBM_SPEC] * n,
        out_shape=[_S((N_DEV,) + p.shape, p.dtype) for p in parts],
        scratch_shapes=[pltpu.SemaphoreType.DMA((n, 7)), pltpu.SemaphoreType.DMA((n, 7))],
    )(*parts)
    return [_with_own(o, p, _dev_index(_me())) for o, p in zip(outs, parts)]


def _with_own(gathered, own, index):
    return lax.dynamic_update_index_in_dim(gathered, own, index, 0)


def _broadcast_small(small, name="small_exchange"):
    def body(in_ref, out_ref, send_sems, recv_sems, local_sem):
        x, y, c = _me()
        my = _dev_index((x, y, c))
        mine = pltpu.make_async_copy(in_ref, out_ref.at[my], local_sem)
        mine.start()
        remote = []
        for k in range(1, N_DEV):
            cp = pltpu.make_async_remote_copy(
                src_ref=in_ref, dst_ref=out_ref.at[my], send_sem=send_sems.at[k - 1], recv_sem=recv_sems.at[k - 1],
                device_id=(x ^ (k >> 2), y ^ ((k >> 1) & 1), c ^ (k & 1)), device_id_type=MESH_ID)
            cp.start()
            remote.append(cp)
        for cp in remote:
            cp.wait_recv()
        for cp in remote:
            cp.wait_send()
        mine.wait()

    return pl.pallas_call(
        body, name=name, in_specs=[HBM_SPEC], out_specs=HBM_SPEC,
        out_shape=_S((N_DEV,) + small.shape, small.dtype),
        scratch_shapes=[pltpu.SemaphoreType.DMA((7,)), pltpu.SemaphoreType.DMA((7,)), pltpu.SemaphoreType.DMA],
    )(small)


def _run_rider(rider, name):
    ni, no = len(rider.ins), len(rider.out_shapes)

    def body(*refs):
        riding = (refs[:ni], refs[ni:ni + no], refs[ni + no:])
        rider.start(*riding)
        rider.wait(*riding)

    return list(pl.pallas_call(
        body, name=name, in_specs=[HBM_SPEC] * ni, out_specs=[HBM_SPEC] * no, out_shape=rider.out_shapes,
        scratch_shapes=rider.sems)(*rider.ins))


def _chip_peers(x, y):
    return [(k - 1, (x ^ (k >> 1), y ^ (k & 1))) for k in range(1, N_CHIP)]


def _pair_exchange(g2s):
    n = len(g2s)

    def copies(ins, outs, sems):
        x, y, c = _me()
        return [pltpu.make_async_remote_copy(
            src_ref=ins[a].at[1 - c, j], dst_ref=outs[a].at[j], send_sem=sems[0].at[a, j], recv_sem=sems[1].at[a, j],
            device_id=(x, y, 1 - c), device_id_type=MESH_ID) for a in range(n) for j in range(N_CHIP)], []

    return _Rider(g2s, [_S(g.shape[1:], g.dtype) for g in g2s],
                  [pltpu.SemaphoreType.DMA((n, N_CHIP)), pltpu.SemaphoreType.DMA((n, N_CHIP))], copies)


def _my_chip():
    x, y, _ = _me()
    return 2 * x + y


def _chip_exchange(s4s):
    n = len(s4s)

    def copies(ins, outs, sems):
        x, y, c = _me()
        my = 2 * x + y
        return [pltpu.make_async_remote_copy(
            src_ref=ins[a].at[2 * px + py], dst_ref=outs[a].at[my], send_sem=sems[0].at[a, k], recv_sem=sems[1].at[a, k],
            device_id=(px, py, c), device_id_type=MESH_ID) for k, (px, py) in _chip_peers(x, y) for a in range(n)], []

    return _Rider(s4s, [_S(s.shape, s.dtype) for s in s4s],
                  [pltpu.SemaphoreType.DMA((n, N_CHIP - 1)), pltpu.SemaphoreType.DMA((n, N_CHIP - 1))], copies)


def _chip_exchange_done(s4s, recvs):
    my = _my_chip()
    return [_with_own(r, lax.dynamic_index_in_dim(s, my, 0, keepdims=False), my) for s, r in zip(s4s, recvs)]


def _chip_gather(arrs):
    n = len(arrs)

    def copies(ins, outs, sems):
        x, y, c = _me()
        return [pltpu.make_async_remote_copy(
            src_ref=ins[a], dst_ref=outs[a].at[2 * x + y], send_sem=sems[0].at[a, k], recv_sem=sems[1].at[a, k],
            device_id=(px, py, c), device_id_type=MESH_ID) for k, (px, py) in _chip_peers(x, y) for a in range(n)], []

    return _Rider(arrs, [_S((N_CHIP,) + a.shape, a.dtype) for a in arrs],
                  [pltpu.SemaphoreType.DMA((n, N_CHIP - 1)), pltpu.SemaphoreType.DMA((n, N_CHIP - 1))], copies)


def _pair_gather(chips):
    n = len(chips)

    def copies(ins, outs, sems):
        x, y, c = _me()
        return [pltpu.make_async_remote_copy(
            src_ref=ins[a].at[j], dst_ref=outs[a].at[j], send_sem=sems[0].at[a, j], recv_sem=sems[1].at[a, j],
            device_id=(x, y, 1 - c), device_id_type=MESH_ID) for a in range(n) for j in range(N_CHIP)], []

    return _Rider(chips, [_S(a.shape, a.dtype) for a in chips],
                  [pltpu.SemaphoreType.DMA((n, N_CHIP)), pltpu.SemaphoreType.DMA((n, N_CHIP))], copies)


def _shard_rows(r):
    return 256 if r % 256 == 0 else r


def _pair_sum(g2, got):
    _, nc, L, R, C = g2.shape
    tr = _shard_rows(R)

    def body(a_ref, b_ref, o_ref):
        o_ref[...] = (a_ref[0] + b_ref[...]).astype(BF16)

    blk = pl.BlockSpec((1, 1, tr, C), lambda j, l, i: (j, l, i, 0))
    return _call(
        body, "pair_sum", (nc, L, R // tr),
        [pl.BlockSpec((1, 1, 1, tr, C), lambda j, l, i: (lax.axis_index("c"), j, l, i, 0)), blk], blk,
        _S(got.shape, BF16))(g2, got)


def _adam(w, g, m, v):
    m2 = ADAM_B1 * m + (1.0 - ADAM_B1) * g
    v2 = ADAM_B2 * v + (1.0 - ADAM_B2) * (g * g)
    m_hat = m2 / (1.0 - ADAM_B1 ** ADAM_STEP)
    v_hat = v2 / (1.0 - ADAM_B2 ** ADAM_STEP)
    return -ADAM_LR * (m_hat / (jnp.sqrt(v_hat) + ADAM_EPS) + ADAM_WD * w), m2, v2


def _sum_adam(recvs, w, m, v):
    L, R, C = w.shape
    tr = _shard_rows(R)
    nr = len(recvs)

    def body(*refs):
        w_ref, m_ref, v_ref, g_ref, d_ref, m2_ref, v2_ref = refs[nr:]
        g = None
        for li, r_ref in enumerate(refs[:nr]):
            s = r_ref[0, 0].astype(F32)
            for j in range(1, N_CHIP):
                s = s + r_ref[j, 0].astype(F32)
            g = s if g is None else jnp.where(pl.program_id(0) == li, s, g)
        g_ref[0] = g
        d_ref[0], m2_ref[0], v2_ref[0] = _adam(w_ref[0], g, m_ref[0], v_ref[0])

    blk = pl.BlockSpec((1, tr, C), lambda l, i: (l, i, 0))
    rspec = pl.BlockSpec((N_CHIP, 1, tr, C), (lambda l, i: (0, l, i, 0)) if nr == 1 else (lambda l, i: (0, 0, i, 0)))
    return _call(body, "sum_adam", (L, R // tr), [rspec] * nr + [blk, blk, blk], [blk] * 4, [_S(w.shape)] * 4)(
        *recvs, w, m, v)


def _adam_big(w, g, m, v):
    L, R, C = w.shape
    tr = _shard_rows(R)

    def body(w_ref, g_ref, m_ref, v_ref, d_ref, m2_ref, v2_ref):
        d_ref[0], m2_ref[0], v2_ref[0] = _adam(w_ref[0], g_ref[0], m_ref[0], v_ref[0])

    blk = pl.BlockSpec((1, tr, C), lambda l, i: (l, i, 0))
    return _call(body, "adam_big", (L, R // tr), [blk] * 4, [blk] * 3, [_S(w.shape)] * 3)(w, g, m, v)


def _sum_small(recv):
    def body(r_ref, o_ref):
        g = r_ref[0]
        for k in range(1, recv.shape[0]):
            g = g + r_ref[k]
        o_ref[...] = g

    return pl.pallas_call(body, name="sum_small", out_shape=_S(recv.shape[1:]))(recv)


def _adam_small(w, g, m, v):
    def body(w_ref, g_ref, m_ref, v_ref, d_ref, m2_ref, v2_ref):
        d_ref[...], m2_ref[...], v2_ref[...] = _adam(w_ref[...], g_ref[...], m_ref[...], v_ref[...])

    return pl.pallas_call(body, name="adam_small", out_shape=[_S(w.shape)] * 3)(w, g, m, v)


def _pack(arrs, dtype, row_mult):
    parts, offs, r = [], [], 0
    for a in arrs:
        nr = -(-a.size // LANES)
        parts.append(jnp.pad(a.reshape(-1).astype(dtype), (0, nr * LANES - a.size)))
        offs.append(r)
        r += nr
    pad = (-r) % row_mult
    if pad:
        parts.append(jnp.zeros((pad * LANES,), dtype))
    return jnp.concatenate(parts).reshape(r + pad, LANES), offs


def _unpack(packed, offs, shapes, lead=()):
    out = []
    for off, shp in zip(offs, shapes):
        size = int(np.prod(shp))
        nr = -(-size // LANES)
        flat = packed[..., off:off + nr, :].reshape(lead + (nr * LANES,))
        out.append(flat[..., :size].reshape(lead + tuple(shp)))
    return out


BIG = (("w_ada", 2), ("w_in", 2), ("w_br_a", 2), ("w_br_b", 2), ("w_br_c", 2), ("w_o", 1), ("w_gu", 2), ("w_down", 1))
CONV = ("dn_conv_w", "sc_conv_w")
REPL = ("c_ctx", "b_ada", "norm1_g", "norm2_g", "dn_a_log", "dn_dt_bias", "dn_norm_g", "pool_w", "pool_scale",
        "final_norm_g")
WEIGHTS = ("c_ctx", "w_ada", "b_ada", "norm1_g", "norm2_g", "w_in", "dn_conv_w", "dn_a_log", "dn_dt_bias", "dn_norm_g",
           "pool_w", "pool_scale", "sc_conv_w", "w_br_a", "w_br_b", "w_br_c", "w_o", "w_gu", "w_down", "final_norm_g")
TOKEN_TILE = 256


def _join(blocks, axis):
    nd, nl, r, c = blocks.shape
    if axis == 2:
        return blocks.transpose(1, 2, 0, 3).reshape(nl, r, nd * c)
    return blocks.transpose(1, 0, 2, 3).reshape(nl, nd * r, c)


def _split(full, axis):
    nl, r, c = full.shape
    if axis == 2:
        return full.reshape(nl, r, N_CHIP, 2, c // N_DEV).transpose(3, 2, 0, 1, 4)
    return full.reshape(nl, N_CHIP, 2, r // N_DEV, c).transpose(2, 1, 0, 3, 4)


PRESPLIT = ("w_in", "w_gu")


def _presplit(layer_grads, names=LAYERED):
    return [g if k in PRESPLIT else _split(g[None], dict(BIG)[k]) for k, g in zip(names, layer_grads)]


class _Comm:
    def __init__(self, late_shards, w_ada, b_ada):
        self.packed = [k for k in LATE if k not in PRESPLIT]
        self.shapes = [late_shards[k].shape for k in self.packed]
        pack, self.offs = _pack([late_shards[k] for k in self.packed], BF16, BF16_ROWS)
        self.late = [pack, late_shards["w_gu"].astype(BF16), late_shards["w_in"][1:].astype(BF16)]
        self.w_ada, self.b_ada = w_ada.astype(BF16), b_ada
        self.g2s = None

    def adaln_fwd(self, c, c_ctx):
        my = _dev_index(_me())
        ncol = self.w_ada.shape[2]
        c_all = _broadcast_small(c.reshape(8, LANES), "c_exchange").reshape(N_DEV, D)
        self.cc = jnp.concatenate([c_all, c_ctx[None, :], jnp.zeros((CC_ROWS - N_DEV - 1, D), F32)], axis=0)
        b_cols = lax.dynamic_slice_in_dim(self.b_ada, my * ncol, ncol, axis=1).reshape(NL, 1, ncol)
        cols = _mod_fwd(self.cc, self.w_ada, b_cols)
        got = _broadcast_small(cols.reshape(-1, LANES), "mods_exchange").reshape(N_DEV, NL, CC_ROWS, ncol)
        return my, got.transpose(1, 2, 0, 3).reshape(NL, CC_ROWS, N_DEV * ncol)

    def adaln_bwd(self, dmods):
        my = _dev_index(_me())
        ncol = self.w_ada.shape[2]
        got = _broadcast_small(dmods.reshape(-1, LANES), "dmods_exchange")
        rows = got.reshape(N_DEV, NL, 2, 6 * D)
        ctx_sum = _sum_small(rows[:, :, 0].reshape(N_DEV, -1, LANES)).reshape(NL, 1, 6 * D)
        db = _sum_small(rows.transpose(0, 2, 1, 3).reshape(2 * N_DEV, -1, LANES)).reshape(NL, 6 * D)
        dm = jnp.concatenate([rows[:, :, 1].transpose(1, 0, 2), ctx_sum,
                              jnp.zeros((NL, CC_ROWS - N_DEV - 1, 6 * D), F32)], axis=1)
        dw, dcc = _mod_bwd(self.cc, self.w_ada, lax.dynamic_slice_in_dim(dm, my * ncol, ncol, axis=2))
        return dict(w_ada=dw, b_ada=db, c_ctx=dcc[CTX_ROW])

    def late_weights_chips(self):
        return _chip_gather(self.late)

    def late_weights_pair(self, riding):
        self.chips = [_with_own(r, a, _my_chip()) for r, a in zip(riding, self.late)]
        return _pair_gather(self.chips)

    def late_weights(self, riding):
        on_south = lax.axis_index("c") == 0
        by_dev = []
        for mine, other in zip(self.chips, riding):
            both = jnp.stack([jnp.where(on_south, mine, other), jnp.where(on_south, other, mine)], axis=1)
            by_dev.append(both.reshape((N_DEV,) + mine.shape[1:]))
        shards = _unpack(by_dev[0], self.offs, self.shapes, (N_DEV,))
        return dict({k: _join(blocks, dict(BIG)[k]) for k, blocks in zip(self.packed, shards)}, w_gu=by_dev[1]), by_dev[2]

    def grad_pair_rider(self, layer_grads, names=LAYERED):
        g2s = _presplit(layer_grads, names)
        self.g2s = (self.g2s or []) + g2s
        return _pair_exchange(g2s)

    def grad_chip_rider(self, got):
        self.sums = [_pair_sum(g2, gt) for g2, gt in zip(self.g2s, got)]
        return _chip_exchange(self.sums)

    def grad_chip_done(self, riding):
        return _chip_exchange_done(self.sums, riding)


def kernel(x, c, ctx, c_ctx, w_ada, b_ada, norm1_g, norm2_g, w_in, dn_conv_w, dn_a_log, dn_dt_bias, dn_norm_g, pool_w, pool_scale, sc_conv_w, w_br_a, w_br_b, w_br_c, w_o, w_gu, w_down, final_norm_g, loss_target, m_c_ctx, m_w_ada, m_b_ada, m_norm1_g, m_norm2_g, m_w_in, m_dn_conv_w, m_dn_a_log, m_dn_dt_bias, m_dn_norm_g, m_pool_w, m_pool_scale, m_sc_conv_w, m_w_br_a, m_w_br_b, m_w_br_c, m_w_o, m_w_gu, m_w_down, m_final_norm_g, v_c_ctx, v_w_ada, v_b_ada, v_norm1_g, v_norm2_g, v_w_in, v_dn_conv_w, v_dn_a_log, v_dn_dt_bias, v_dn_norm_g, v_pool_w, v_pool_scale, v_sc_conv_w, v_w_br_a, v_w_br_b, v_w_br_c, v_w_o, v_w_gu, v_w_down, v_final_norm_g):
    loc = dict(c_ctx=c_ctx, w_ada=w_ada, b_ada=b_ada, norm1_g=norm1_g, norm2_g=norm2_g, w_in=w_in, dn_conv_w=dn_conv_w,
               dn_a_log=dn_a_log, dn_dt_bias=dn_dt_bias, dn_norm_g=dn_norm_g, pool_w=pool_w, pool_scale=pool_scale,
               sc_conv_w=sc_conv_w, w_br_a=w_br_a, w_br_b=w_br_b, w_br_c=w_br_c, w_o=w_o, w_gu=w_gu, w_down=w_down,
               final_norm_g=final_norm_g)
    mom_m = dict(c_ctx=m_c_ctx, w_ada=m_w_ada, b_ada=m_b_ada, norm1_g=m_norm1_g, norm2_g=m_norm2_g, w_in=m_w_in,
                 dn_conv_w=m_dn_conv_w, dn_a_log=m_dn_a_log, dn_dt_bias=m_dn_dt_bias, dn_norm_g=m_dn_norm_g,
                 pool_w=m_pool_w, pool_scale=m_pool_scale, sc_conv_w=m_sc_conv_w, w_br_a=m_w_br_a, w_br_b=m_w_br_b,
                 w_br_c=m_w_br_c, w_o=m_w_o, w_gu=m_w_gu, w_down=m_w_down, final_norm_g=m_final_norm_g)
    mom_v = dict(c_ctx=v_c_ctx, w_ada=v_w_ada, b_ada=v_b_ada, norm1_g=v_norm1_g, norm2_g=v_norm2_g, w_in=v_w_in,
                 dn_conv_w=v_dn_conv_w, dn_a_log=v_dn_a_log, dn_dt_bias=v_dn_dt_bias, dn_norm_g=v_dn_norm_g,
                 pool_w=v_pool_w, pool_scale=v_pool_scale, sc_conv_w=v_sc_conv_w, w_br_a=v_w_br_a, w_br_b=v_w_br_b,
                 w_br_c=v_w_br_c, w_o=v_w_o, w_gu=v_w_gu, w_down=v_w_down, final_norm_g=v_final_norm_g)
    my = _dev_index(_me())

    conv_pack, conv_offs = _pack([loc[k] for k in CONV], F32, 8)
    w_in0_all, conv_all = _allgather([w_in[:1].astype(BF16), conv_pack])
    full = dict({k: loc[k] for k in REPL}, w_in=[w_in0_all])
    for k, blocks in zip(CONV, _unpack(conv_all, conv_offs, [loc[k].shape for k in CONV], (N_DEV,))):
        full[k] = _join(blocks, 2)

    loss8, grad_x, g, recv_early = _device_step(x[0], c, ctx[0], loss_target[0], full, TOKEN_TILE,
                                                comm=_Comm({k: loc[k] for k in LATE + ("w_in",)}, w_ada, b_ada))

    last = [k for k in LAYERED if k not in FFN_W]
    tail = _presplit([g[k][0] for k in last], last)
    got = _run_rider(_pair_exchange(tail), "pair_exchange")
    sums = [_pair_sum(a, b) for a, b in zip(tail, got)]
    recv_tail = _chip_exchange_done(sums, _run_rider(_chip_exchange(sums), "chip_exchange"))
    recv = {k: [None, r] for k, r in zip(LAYERED, recv_early)}
    for k, r in list(zip(FFN_W, recv_early[len(LAYERED):])) + list(zip(last, recv_tail)):
        recv[k][0] = r

    small_names = REPL + CONV
    summed = [k for k in small_names if k != "b_ada"]
    small_pack, small_offs = _pack([g[k] for k in summed] + [loss8[0:1, 0:1]], F32, 8)
    small_sum = _sum_small(_broadcast_small(small_pack))
    sums = _unpack(small_sum, small_offs, [g[k].shape for k in summed] + [(1, 1)])
    grads = dict(zip(summed, sums[:-1]), b_ada=g["b_ada"], w_ada=g["w_ada"])
    loss = sums[-1][0, 0]
    for k in CONV:
        w = loc[k].shape[2]
        grads[k] = lax.dynamic_slice_in_dim(grads[k], my * w, w, axis=2)

    delta, new_m, new_v = {}, {}, {}
    for k in LAYERED:
        grads[k], delta[k], new_m[k], new_v[k] = _sum_adam(recv[k], loc[k], mom_m[k], mom_v[k])
    delta["w_ada"], new_m["w_ada"], new_v["w_ada"] = _adam_big(w_ada, g["w_ada"], m_w_ada, v_w_ada)
    packs = [_pack([src[k] for k in small_names], F32, 8)[0] for src in (loc, grads, mom_m, mom_v)]
    _, offs = _pack([loc[k] for k in small_names], F32, 8)
    shapes = [loc[k].shape for k in small_names]
    for dst, packed in zip((delta, new_m, new_v), _adam_small(*packs)):
        dst.update(zip(small_names, _unpack(packed, offs, shapes)))

    return (loss, grad_x[None], *[grads[k] for k in WEIGHTS], *[delta[k] for k in WEIGHTS],
            *[new_m[k] for k in WEIGHTS], *[new_v[k] for k in WEIGHTS])
```

```python
import numpy as np
import jax
import jax.numpy as jnp
from jax import lax
from jax.experimental import pallas as pl
from jax.experimental.pallas import tpu as pltpu

F32 = jnp.float32
BF16 = jnp.bfloat16
HI = lax.Precision.HIGHEST

D = 1024
NL = 2
NH = 4
DH = 128
DN = NH * DH
CH = 64
GW = 64
PW = 256
DFF = 2816
EPS = 1e-6
N_DEV = 8
N_CHIP = 4
MESH_ID = pl.DeviceIdType.MESH
HBM_SPEC = pl.BlockSpec(memory_space=pltpu.HBM)
LANES = 128
BF16_ROWS = 16
VMEM_MB = 56
DW_ACC_MB = 12

ADAM_LR, ADAM_B1, ADAM_B2, ADAM_EPS, ADAM_WD, ADAM_STEP = 0.001, 0.9, 0.999, 1e-08, 0.01, 10

IN_BOUNDS = (0, 1536, 2048, 2064, 2320, 2576, 2832, 3088, 6160)
IN_WIDTHS = (1536, 512, 128, 256, 256, 256, 256, 3072)
POOL_WIN = ((1, 0), (2, 1), (4, 3), (8, 7))

NN = ((1,), (0,))
NT = ((1,), (1,))
TN = ((0,), (0,))


def _dot(a, b, dims, hi=False):
    if hi:
        prec = lax.Precision.HIGH if hi == "x3" else HI
        return lax.dot_general(a, b, (dims, ((), ())), precision=prec, preferred_element_type=F32)
    return lax.dot_general(a.astype(BF16), b.astype(BF16), (dims, ((), ())), preferred_element_type=F32)


def _S(shape, dtype=F32):
    return jax.ShapeDtypeStruct(tuple(shape), dtype)


def _full(shape):
    nd = len(shape)
    return pl.BlockSpec(tuple(shape), lambda *_: (0,) * nd)


def _rows(tt, w):
    return pl.BlockSpec((tt, w), lambda i: (i, 0))


class _Rider:
    def __init__(self, ins, out_shapes, sems, copies):
        self.ins, self.out_shapes, self.sems, self.copies = list(ins), list(out_shapes), list(sems), copies

    def start(self, ins, outs, sems):
        remote, local = self.copies(ins, outs, sems)
        for cp in local + remote:
            cp.start()

    def wait(self, ins, outs, sems):
        remote, local = self.copies(ins, outs, sems)
        for cp in remote:
            cp.wait_recv()
        for cp in remote:
            cp.wait_send()
        for cp in local:
            cp.wait()


def _call(body, name, grid, in_specs, out_specs, out_shape, scratch=(), rider=None):
    params = pltpu.CompilerParams(dimension_semantics=("arbitrary",) * len(grid), vmem_limit_bytes=VMEM_MB << 20)
    if rider is None:
        return pl.pallas_call(body, name=name, grid=grid, in_specs=in_specs, out_specs=out_specs, out_shape=out_shape,
                              scratch_shapes=list(scratch), compiler_params=params)
    single = not isinstance(out_shape, (list, tuple))
    out_specs, out_shape = ([out_specs], [out_shape]) if single else (list(out_specs), list(out_shape))
    n_in, n_out, n_scr = len(in_specs), len(out_shape), len(scratch)
    r_in, r_out = len(rider.ins), len(rider.out_shapes)

    def hosted(*refs):
        ins, refs = refs[:n_in + r_in], refs[n_in + r_in:]
        outs, scr = refs[:n_out + r_out], refs[n_out + r_out:]
        riding = (ins[n_in:], outs[n_out:], scr[n_scr:])

        @pl.when(pl.program_id(0) == 0)
        def _():
            rider.start(*riding)

        body(*ins[:n_in], *outs[:n_out], *scr[:n_scr])

        @pl.when(pl.program_id(0) == grid[0] - 1)
        def _():
            rider.wait(*riding)

    call = pl.pallas_call(
        hosted, name=name, grid=grid, in_specs=list(in_specs) + [HBM_SPEC] * r_in,
        out_specs=out_specs + [HBM_SPEC] * r_out, out_shape=out_shape + rider.out_shapes,
        scratch_shapes=list(scratch) + rider.sems, compiler_params=params)

    def run(*args):
        res = call(*args, *rider.ins)
        own = res[:n_out]
        return (own[0] if single else own), list(res[n_out:])

    return run


def _iota(shape, axis):
    return lax.broadcasted_iota(jnp.int32, shape, axis)


def _colsum(a):
    return jnp.sum(a, axis=0, keepdims=True)


def _silu(x):
    return x * jax.nn.sigmoid(x)


def _modulate(x, g, sh, sc):
    xn = x * lax.rsqrt(jnp.mean(x * x, axis=-1, keepdims=True) + EPS)
    return (xn * g) * (1.0 + sc) + sh


def _stream_rows(mv_ref, i, tt, tc, k):
    isc = (i * tt + _iota((tt, 1), 0)) < tc
    return isc, jnp.where(isc, mv_ref[k:k + 1, :], mv_ref[3 + k:4 + k, :])


def _acc_stream(ref, k, isc, val):
    ref[k:k + 1, :] += _colsum(jnp.where(isc, val, 0.0))
    ref[3 + k:4 + k, :] += _colsum(jnp.where(isc, 0.0, val))


CC_ROWS = 16
CTX_ROW = 8


def _mod_cols(n):
    return 1536 if n % 1536 == 0 else n


def _mod_fwd(cc, w_ada, b_ada3):
    n = w_ada.shape[2]
    ct = _mod_cols(n)

    def body(cc_ref, w_ref, b_ref, o_ref):
        o_ref[0] = _dot(_silu(cc_ref[...]), w_ref[0], NN) + b_ref[0]

    return _call(
        body, "mod_fwd", (NL, n // ct),
        [pl.BlockSpec((CC_ROWS, D), lambda l, j: (0, 0)), pl.BlockSpec((1, D, ct), lambda l, j: (l, 0, j)),
         pl.BlockSpec((1, 1, ct), lambda l, j: (l, 0, j))],
        pl.BlockSpec((1, CC_ROWS, ct), lambda l, j: (l, 0, j)), _S((NL, CC_ROWS, n)))(cc, w_ada, b_ada3)


def _mod_bwd(cc, w_ada, dmods):
    n = w_ada.shape[2]
    ct = _mod_cols(n)

    def body(cc_ref, w_ref, dm_ref, dw_ref, dcc_ref):
        first = (pl.program_id(0) == 0) & (pl.program_id(1) == 0)
        cc_ = cc_ref[...]
        sg = jax.nn.sigmoid(cc_)
        dm = dm_ref[0]
        dw_ref[0] = _dot(cc_ * sg, dm, TN)

        @pl.when(first)
        def _():
            dcc_ref[...] = jnp.zeros_like(dcc_ref)

        dcc_ref[...] += _dot(dm, w_ref[0], NT) * (sg * (1.0 + cc_ * (1.0 - sg)))

    return _call(
        body, "mod_bwd", (NL, n // ct),
        [pl.BlockSpec((CC_ROWS, D), lambda l, j: (0, 0)), pl.BlockSpec((1, D, ct), lambda l, j: (l, 0, j)),
         pl.BlockSpec((1, CC_ROWS, ct), lambda l, j: (l, 0, j))],
        [pl.BlockSpec((1, D, ct), lambda l, j: (l, 0, j)), pl.BlockSpec((CC_ROWS, D), lambda l, j: (0, 0))],
        [_S((NL, D, n)), _S((CC_ROWS, D))])(cc, w_ada, dmods)


def _cols(n, tt):
    return pl.BlockSpec((n, tt), lambda i: (0, i))


def _inproj_fwd(X, mv, g, ws, tc, tt):
    T = X.shape[0]
    nw = len(ws)

    def body(x_ref, mv_ref, g_ref, *refs):
        w_refs, ht_ref, p_refs = refs[:nw], refs[nw], refs[nw + 1:]
        i = pl.program_id(0)
        _, sh = _stream_rows(mv_ref, i, tt, tc, 0)
        _, sc = _stream_rows(mv_ref, i, tt, tc, 1)
        h = _modulate(x_ref[...], g_ref[...], sh, sc)
        ht_ref[...] = h.T.astype(BF16)
        hb = h.astype(BF16)
        for w_ref, p_ref in zip(w_refs, p_refs):
            p_ref[...] = jnp.dot(hb, w_ref[...], preferred_element_type=F32)

    return _call(
        body, "inproj_fwd", (T // tt,),
        [_rows(tt, D), _full((8, D)), _full((1, D))] + [_full(w.shape) for w in ws],
        [_cols(D, tt)] + [_rows(tt, w.shape[1]) for w in ws],
        [_S((D, T), BF16)] + [_S((T, w.shape[1])) for w in ws])(X, mv, g, *ws)


def _inproj_bwd(X, mv, g, ws, dps, dp_w, dres, tc, tt):
    T = X.shape[0]
    nw, nd = len(ws), len(dps)

    def body(x_ref, mv_ref, g_ref, dres_ref, *refs):
        w_refs, dp_refs = refs[:nw], refs[nw:nw + nd]
        dx_ref, dg_ref, dm_ref = refs[nw + nd:]
        i = pl.program_id(0)
        isc, sh = _stream_rows(mv_ref, i, tt, tc, 0)
        _, sc = _stream_rows(mv_ref, i, tt, tc, 1)
        dh = None
        for dp_ref, k in zip(dp_refs, dp_w):
            t = _dot(dp_ref[...], w_refs[k][...], NT)
            dh = t if dh is None else dh + t
        _, vjp = jax.vjp(_modulate, x_ref[...], g_ref[...], sh, sc)
        dx, dg, dsh, dsc = vjp(dh)
        dx_ref[...] = dres_ref[...] + dx

        @pl.when(i == 0)
        def _():
            dg_ref[...] = jnp.zeros_like(dg_ref)
            dm_ref[...] = jnp.zeros_like(dm_ref)

        dg_ref[...] += dg
        _acc_stream(dm_ref, 0, isc, dsh)
        _acc_stream(dm_ref, 1, isc, dsc)

    return _call(
        body, "inproj_bwd", (T // tt,),
        [_rows(tt, D), _full((8, D)), _full((1, D)), _rows(tt, D)] + [_full(w.shape) for w in ws]
        + [_rows(tt, dp.shape[1]) for dp in dps],
        [_rows(tt, D), _full((1, D)), _full((8, D))],
        [_S((T, D)), _S((1, D)), _S((8, D))])(X, mv, g, dres, *ws, *dps)


def _dw(At, B, tt):
    K, T = At.shape
    N = B.shape[1]
    tt = 3 * tt if T % (3 * tt) == 0 else tt
    tn = max(t for t in range(LANES, N + 1, LANES) if N % t == 0 and K * t * 4 <= DW_ACC_MB << 20)

    def body(a_ref, b_ref, o_ref):
        @pl.when(pl.program_id(1) == 0)
        def _():
            o_ref[...] = jnp.zeros_like(o_ref)

        o_ref[...] += _dot(a_ref[...], b_ref[...], NN)

    return _call(
        body, "dw", (N // tn, T // tt),
        [pl.BlockSpec((K, tt), lambda j, i: (0, i)), pl.BlockSpec((tt, tn), lambda j, i: (i, j))],
        pl.BlockSpec((K, tn), lambda j, i: (0, j)), _S((K, N)))(At, B)


def _halo_specs(T, tt, cw, col):
    r8, nb8 = tt // 8, T // 8
    return [pl.BlockSpec((tt, cw), lambda j, i: (i, col(j))),
            pl.BlockSpec((8, cw), lambda j, i: (jnp.maximum(i * r8 - 1, 0), col(j))),
            pl.BlockSpec((8, cw), lambda j, i: (jnp.minimum((i + 1) * r8, nb8 - 1), col(j)))]


def _shifts(a, prev8, next8, i, tt, tc, T):
    r = _iota((tt, 1), 0)
    t = i * tt + r
    dn = jnp.where(r == 0, prev8[7:8, :], pltpu.roll(a, 1, 0))
    dn = jnp.where((t == 0) | (t == tc), 0.0, dn)
    up = jnp.where(r == tt - 1, next8[0:1, :], pltpu.roll(a, tt - 1, 0))
    up = jnp.where((t == T - 1) | (t == tc - 1), 0.0, up)
    return dn, up


def _dn_post(y, part):
    a = _silu(y)
    nrm = lax.rsqrt(jnp.sum(a * a, axis=-1, keepdims=True) + EPS)
    f = jnp.where(part == 0, nrm * (DH ** -0.5), jnp.where(part == 1, nrm, 1.0))
    return a * f


def _conv3(w_ref, dn, mid, up):
    return w_ref[0:1, :] * dn + w_ref[1:2, :] * mid + w_ref[2:3, :] * up


def _dnprep_fwd(pq, cw, tc, tt):
    T = pq.shape[0]

    def body(p_ref, pp_ref, pn_ref, w_ref, a_ref):
        part, i = pl.program_id(0), pl.program_id(1)
        p = p_ref[...]
        dn, up = _shifts(p, pp_ref[...], pn_ref[...], i, tt, tc, T)
        y = _conv3(w_ref, dn, p, up)
        for h in range(NH):
            a_ref[:, _hs(h)] = _dn_post(y[:, _hs(h)], part)

    return _call(
        body, "dnprep_fwd", (3, T // tt),
        _halo_specs(T, tt, DN, lambda j: j) + [pl.BlockSpec((3, DN), lambda j, i: (0, j))],
        pl.BlockSpec((tt, DN), lambda j, i: (i, j)), _S((T, 3 * DN)))(pq, pq, pq, cw)


def _dnprep_bwd_act(pq, cw, da_f, da_b, tc, tt):
    T = pq.shape[0]

    def body(p_ref, pp_ref, pn_ref, w_ref, df_ref, db_ref, dy_ref):
        part, i = pl.program_id(0), pl.program_id(1)
        p = p_ref[...]
        dn, up = _shifts(p, pp_ref[...], pn_ref[...], i, tt, tc, T)
        y = _conv3(w_ref, dn, p, up)
        for h in range(NH):
            _, vjp = jax.vjp(lambda yh: _dn_post(yh, part), y[:, _hs(h)])
            dy_ref[:, _hs(h)] = vjp(df_ref[:, _hs(h)] + db_ref[:, _hs(h)])[0]

    blk = pl.BlockSpec((tt, DN), lambda j, i: (i, j))
    return _call(
        body, "dnprep_bwd_act", (3, T // tt),
        _halo_specs(T, tt, DN, lambda j: j) + [pl.BlockSpec((3, DN), lambda j, i: (0, j)), blk, blk],
        blk, _S((T, 3 * DN)))(pq, pq, pq, cw, da_f, da_b)


def _conv_bwd(dy, p, cw, tc, tt):
    T, W = p.shape
    cb = DN

    def body(dy_ref, dyp_ref, dyn_ref, p_ref, pp_ref, pn_ref, w_ref, dp_ref, dw_ref):
        i = pl.program_id(1)
        dy, p_ = dy_ref[...], p_ref[...]
        ddn, dup = _shifts(dy, dyp_ref[...], dyn_ref[...], i, tt, tc, T)
        dp_ref[...] = _conv3(w_ref, dup, dy, ddn)
        pdn, pup = _shifts(p_, pp_ref[...], pn_ref[...], i, tt, tc, T)

        @pl.when(i == 0)
        def _():
            dw_ref[...] = jnp.zeros_like(dw_ref)

        dw_ref[0:1, :] += _colsum(dy * pdn)
        dw_ref[1:2, :] += _colsum(dy * p_)
        dw_ref[2:3, :] += _colsum(dy * pup)

    wspec = pl.BlockSpec((3, cb), lambda j, i: (0, j))
    return _call(
        body, "conv_bwd", (W // cb, T // tt),
        _halo_specs(T, tt, cb, lambda j: j) * 2 + [wspec],
        [pl.BlockSpec((tt, cb), lambda j, i: (i, j)), wspec], [_S((T, W)), _S((3, W))])(dy, dy, dy, p, p, p, cw)


def _sc_fwd(sx, sb, sc_, cw, tc, tt):
    T = sx.shape[0]

    def body(x_ref, xp_ref, xn_ref, c_ref, cp_ref, cn_ref, b_ref, w_ref, y_ref):
        i = pl.program_id(1)
        u = c_ref[...] * x_ref[...]
        dn, up = _shifts(u, cp_ref[...] * xp_ref[...], cn_ref[...] * xn_ref[...], i, tt, tc, T)
        y_ref[...] = b_ref[...] * _conv3(w_ref, dn, u, up)

    blk = pl.BlockSpec((tt, LANES), lambda j, i: (i, j))
    return _call(
        body, "sc_fwd", (PW // LANES, T // tt),
        _halo_specs(T, tt, LANES, lambda j: j) * 2 + [blk, pl.BlockSpec((3, LANES), lambda j, i: (0, j))],
        blk, _S((T, PW)))(sx, sx, sx, sc_, sc_, sc_, sb, cw)


def _sc_bwd(sx, sb, sc_, cw, dy, tc, tt):
    T = sx.shape[0]

    def body(x_ref, xp_ref, xn_ref, c_ref, cp_ref, cn_ref, b_ref, bp_ref, bn_ref, dy_ref, dyp_ref, dyn_ref, w_ref,
             dx_ref, db_ref, dc_ref, dw_ref):
        i = pl.program_id(1)
        x, c, dy_ = x_ref[...], c_ref[...], dy_ref[...]
        u = c * x
        udn, uup = _shifts(u, cp_ref[...] * xp_ref[...], cn_ref[...] * xn_ref[...], i, tt, tc, T)
        db_ref[...] = dy_ * _conv3(w_ref, udn, u, uup)
        e = dy_ * b_ref[...]
        edn, eup = _shifts(e, dyp_ref[...] * bp_ref[...], dyn_ref[...] * bn_ref[...], i, tt, tc, T)
        du = _conv3(w_ref, eup, e, edn)
        dx_ref[...] = du * c
        dc_ref[...] = du * x

        @pl.when(i == 0)
        def _():
            dw_ref[...] = jnp.zeros_like(dw_ref)

        dw_ref[0:1, :] += _colsum(e * udn)
        dw_ref[1:2, :] += _colsum(e * u)
        dw_ref[2:3, :] += _colsum(e * uup)

    blk = pl.BlockSpec((tt, LANES), lambda j, i: (i, j))
    wspec = pl.BlockSpec((3, LANES), lambda j, i: (0, j))
    return _call(
        body, "sc_bwd", (PW // LANES, T // tt),
        _halo_specs(T, tt, LANES, lambda j: j) * 4 + [wspec],
        [blk, blk, blk, wspec], [_S((T, PW))] * 3 + [_S((3, PW))])(
            sx, sx, sx, sc_, sc_, sc_, sb, sb, sb, dy, dy, dy, cw)


def _group_select(vals):
    g = _iota((1, PW), 1) // (PW // len(POOL_WIN))
    return jnp.where(g == 0, vals[0], jnp.where(g == 1, vals[1], jnp.where(g == 2, vals[2], vals[3])))


def _nested_box(get, mirror):
    acc, outs, pl_, ph_ = get(0), [], 0, 0
    for lo, hi in POOL_WIN:
        if mirror:
            lo, hi = hi, lo
        for k in range(pl_ + 1, lo + 1):
            acc = acc + get(-k)
        for k in range(ph_ + 1, hi + 1):
            acc = acc + get(k)
        pl_, ph_ = lo, hi
        outs.append(acc)
    return _group_select(outs)


def _box_tokens(a, n, mirror):
    idx = _iota((n, 1), 0)

    def get(k):
        if k == 0:
            return a
        return jnp.where((idx + k >= 0) & (idx + k < n), pltpu.roll(a, (-k) % n, 0), 0.0)

    return _nested_box(get, mirror)


def _inv_count(pos, n):
    return _group_select([1.0 / (jnp.minimum(pos + hi, n - 1) - jnp.maximum(pos - lo, 0) + 1).astype(F32)
                          for lo, hi in POOL_WIN])


def _pool_rows(ref, r, R, tc, mirror):
    def get(k):
        rr = r + k
        rc = jnp.clip(rr, 0, R - 1)
        v = ref[pl.ds(pl.multiple_of(tc + rc * GW, GW), GW), :]
        if mirror:
            v = v * _inv_count(jnp.full((1, PW), rc, jnp.int32), R)
        return jnp.where((rr >= 0) & (rr < R), v, 0.0)

    return _nested_box(get, mirror)


def _pool_fwd(u, pwbd, ps, tc):
    T = u.shape[0]
    R = (T - tc) // GW

    def body(u_ref, pw_ref, ps_ref, y_ref):
        pw, scale = pw_ref[...], ps_ref[...]
        uc = u_ref[0:tc, :]
        mc = _box_tokens(uc, tc, False) * _inv_count(_iota((tc, 1), 0), tc)
        y_ref[0:tc, :] = _dot(mc - uc, pw, NN) * scale
        inv_c = _inv_count(_iota((GW, 1), 0), GW)

        def row(r, carry):
            rs = _pool_rows(u_ref, r, R, tc, False) * _inv_count(jnp.full((1, PW), r, jnp.int32), R)
            m = _box_tokens(rs, GW, False) * inv_c
            sl = pl.ds(pl.multiple_of(tc + r * GW, GW), GW)
            y_ref[sl, :] = _dot(m - u_ref[sl, :], pw, NN) * scale
            return carry

        lax.fori_loop(0, R, row, 0)

    return pl.pallas_call(
        body, name="pool_fwd", out_shape=_S((T, PW)),
        compiler_params=pltpu.CompilerParams(vmem_limit_bytes=VMEM_MB << 20))(u, pwbd, ps)


def _pool_bwd(u, pwbd, ps, dy, tc):
    T = u.shape[0]
    R = (T - tc) // GW

    def body(u_ref, pw_ref, ps_ref, dy_ref, du_ref, dpw_ref, dps_ref, dd_ref):
        pw, scale = pw_ref[...], ps_ref[...]
        dpw_ref[...] = jnp.zeros_like(dpw_ref)
        dps_ref[...] = jnp.zeros_like(dps_ref)

        def back(d, dy_):
            dz = dy_ * scale
            dpw_ref[...] += _dot(d, dz, TN)
            dps_ref[...] += _colsum(dy_ * _dot(d, pw, NN))
            return _dot(dz, pw, NT)

        uc = u_ref[0:tc, :]
        inv_cc = _inv_count(_iota((tc, 1), 0), tc)
        ddc = back(_box_tokens(uc, tc, False) * inv_cc - uc, dy_ref[0:tc, :])
        du_ref[0:tc, :] = _box_tokens(ddc * inv_cc, tc, True) - ddc
        inv_c = _inv_count(_iota((GW, 1), 0), GW)

        def row1(r, carry):
            rs = _pool_rows(u_ref, r, R, tc, False) * _inv_count(jnp.full((1, PW), r, jnp.int32), R)
            m = _box_tokens(rs, GW, False) * inv_c
            sl = pl.ds(pl.multiple_of(tc + r * GW, GW), GW)
            dd_ref[sl, :] = back(m - u_ref[sl, :], dy_ref[sl, :])
            return carry

        lax.fori_loop(0, R, row1, 0)

        def row2(r, carry):
            t1 = _pool_rows(dd_ref, r, R, tc, True)
            sl = pl.ds(pl.multiple_of(tc + r * GW, GW), GW)
            du_ref[sl, :] = _box_tokens(t1 * inv_c, GW, True) - dd_ref[sl, :]
            return carry

        lax.fori_loop(0, R, row2, 0)

    return pl.pallas_call(
        body, name="pool_bwd", out_shape=[_S((T, PW)), _S((PW, PW)), _S((1, PW))],
        scratch_shapes=[pltpu.VMEM((T, PW), F32)],
        compiler_params=pltpu.CompilerParams(vmem_limit_bytes=VMEM_MB << 20))(u, pwbd, ps, dy)


def _scan_consts():
    i = np.arange(CH)
    lower = (i[:, None] >= i[None, :]).astype(np.float32)
    return jnp.asarray(np.stack([lower, lower.T])), jnp.asarray(np.stack([lower.T, lower]))


def _gates(pab, al, dtb, csum):
    sp_in = pab + dtb
    sp = jnp.maximum(sp_in, 0.0) + jnp.log(1.0 + jnp.exp(-jnp.abs(sp_in)))
    nexp = -jnp.exp(al)
    gm = nexp * sp
    return gm, jax.nn.sigmoid(pab), _dot(csum, gm, NN, hi=True), sp_in, nexp


def _lane_col(m, j):
    return jnp.sum(jnp.where(_iota(m.shape, 1) == j, m, 0.0), axis=1, keepdims=True)


def _hs(h):
    return slice(h * DH, (h + 1) * DH)


HS = NH * CH
X3 = "x3"


def _stack(x, base=0):
    return jnp.concatenate([x[:, base + h * DH:base + (h + 1) * DH] for h in range(NH)], axis=0)


def _heads(st):
    return [st[h * CH:(h + 1) * CH] for h in range(NH)]


def _rowsum(a):
    return jnp.sum(a, axis=1, keepdims=True)


def _row_of(col):
    e0 = (_iota((8, LANES), 1) == 0).astype(F32)
    return _dot(e0, jnp.broadcast_to(col, (HS, LANES)), NT, hi=True)[0:1, :]


def _inverses(nms):
    eye = (_iota((HS, HS), 0) == _iota((HS, HS), 1)).astype(F32)
    x0s, mps = [eye + nm for nm in nms], list(nms)
    for _ in range(5):
        mps = [_dot(mp, mp, NN) for mp in mps]
        x0s = [x0 + _dot(x0, mp, NN) for x0, mp in zip(x0s, mps)]
    rs = [eye - _dot(eye - nm, x0, NN, hi=X3) for nm, x0 in zip(nms, x0s)]
    return [x0 + _dot(x0, r, NN) for x0, r in zip(x0s, rs)]


def _dn_chunk_pre(qkv, pab, al, dtb, csum_d, d):
    gm, bm, gcm, sp_in, nexp = _gates(pab, al, dtb, csum_d)
    gc = jnp.concatenate([_lane_col(gcm, d * NH + h) for h in range(NH)], axis=0)
    beta = jnp.concatenate([_lane_col(bm, 8 + d * NH + h) for h in range(NH)], axis=0)
    q, k, v = _stack(qkv, 0), _stack(qkv, DN), _stack(qkv, 2 * DN)
    ii, jj = _iota((HS, HS), 0), _iota((HS, HS), 1)
    sh = CH.bit_length() - 1
    same = (ii >> sh) == (jj >> sh)
    incl = same & ((ii >= jj) if d == 0 else (ii <= jj))
    strict = same & ((ii > jj) if d == 0 else (ii < jj))
    Di = jnp.where(incl, jnp.exp(jnp.where(incl, gc - _row_of(gc), 0.0)), 0.0)
    Ds = jnp.where(strict, Di, 0.0)
    kb = k * beta
    kk = _dot(kb, k, NT)
    return dict(q=q, k=k, v=v, beta=beta, gc=gc, gm=gm, bm=bm, sp_in=sp_in, nexp=nexp, Di=Di, Ds=Ds, strict=strict,
                last=CH - 1 if d == 0 else 0, kb=kb, kk=kk)


def _dn_chunk_post(c, tm, uw=None):
    q, k, v, beta, gc, kb, last = (c[n] for n in ("q", "k", "v", "beta", "gc", "kb", "last"))
    E = jnp.exp(gc)
    gls = [gc[h * CH + last:h * CH + last + 1, :] for h in range(NH)]
    xs = jnp.exp(jnp.concatenate([jnp.broadcast_to(g, (CH, 1)) for g in gls], axis=0) - gc)
    qk = _dot(q, k, NT)
    if uw is None:
        both = _dot(tm, jnp.concatenate([v * beta, kb * E], axis=1), NN, hi=X3)
        uw = both[:, :DH], both[:, DH:]
    u, w = uw
    return dict(c, tm=tm, E=E, gls=gls, xs=xs, qk=qk, u=u, w=w, ks=k * xs, qd=q * E, aqk=qk * c["Di"])


def _dn_chunks_bwd_math(cs, Ss, dS2s, dos, vns, dvns):
    I = range(len(cs))
    q, k, v, beta, tm, E, xs, kb, u, w = ([c[n] for c in cs] for n in ("q", "k", "v", "beta", "tm", "E", "xs", "kb", "u", "w"))
    cat = lambda parts: jnp.concatenate(parts, axis=0)

    def per_head(a, states):
        full = _dot(a, cat(states), NT)
        return cat([full[h * CH:(h + 1) * CH, _hs(h)] for h in range(NH)])

    dqd = [per_head(dos[i], Ss[i]) for i in I]
    dks = [per_head(vns[i], dS2s[i]) for i in I]
    dw = [-per_head(dvns[i], Ss[i]) for i in I]
    daqk = [_dot(dos[i], vns[i], NT) for i in I]
    drbw = [_dot(tm[i], jnp.concatenate([dvns[i], dw[i]], axis=1), TN, hi=X3) for i in I]
    drb, drw = [x[:, :DH] for x in drbw], [x[:, DH:] for x in drbw]
    uw = [jnp.concatenate([u[i], w[i].astype(F32)], axis=1) for i in I]
    dA = [jnp.where(cs[i]["strict"], -_dot(drbw[i], uw[i], NT), 0.0) for i in I]
    dM1 = [dA[i] * cs[i]["Ds"] for i in I]
    dM2 = [daqk[i] * cs[i]["Di"] for i in I]
    dM = [cat([dM1[i], dM2[i]]) for i in I]
    dMk = [_dot(dM[i], k[i], NN) for i in I]
    dkb = [dMk[i][:HS] + drw[i] * E[i] for i in I]
    dq = [dMk[i][HS:] + dqd[i] * E[i] for i in I]
    dk = [_dot(dM[i], cat([kb[i], q[i]]), TN) + dks[i] * xs[i] for i in I]
    on_diag = _iota((HS, HS), 0) == _iota((HS, HS), 1)
    out = []
    for i in I:
        G = dM1[i] * cs[i]["kk"] + dM2[i] * cs[i]["qk"]
        col = _rowsum(jnp.where(on_diag, jnp.broadcast_to(_colsum(G), (HS, HS)), 0.0))
        dxx = _rowsum(dks[i] * k[i]) * xs[i]
        dgc = _rowsum(G) - col + (_rowsum(dqd[i] * q[i]) + _rowsum(drw[i] * kb[i])) * E[i] - dxx
        at_last = _iota((CH, 1), 0) == cs[i]["last"]
        ends = []
        for h in range(NH):
            dgl = (_colsum(_rowsum(Ss[i][h] * dS2s[i][h])) * jnp.exp(cs[i]["gls"][h])
                   + _colsum(dxx[h * CH:(h + 1) * CH]))
            ends.append(jnp.where(at_last, dgl, 0.0))
        dbeta = _rowsum(drb[i] * v[i]) + _rowsum(dkb[i] * k[i])
        out.append((dq[i], dk[i] + dkb[i] * beta[i], drb[i] * beta[i], dgc + cat(ends), dbeta))
    return out


def _chunk_group(n, want=2):
    g = want
    while n % g:
        g //= 2
    return g


def _dn_chunks_fwd(qkv, pab, alr, dtr, rider=None):
    T = qkv.shape[0]
    n = T // CH
    G = _chunk_group(n, 4)
    csum, _ = _scan_consts()

    def body(q_ref, p_ref, cs_ref, al_ref, dt_ref, *outs):
        inst = [(g, d) for g in range(G) for d in range(2)]
        pres = [_dn_chunk_pre(q_ref[g * CH:(g + 1) * CH, :], p_ref[g * CH:(g + 1) * CH, :], al_ref[...], dt_ref[...],
                              cs_ref[d], d) for g, d in inst]
        tms = _inverses([-(p["kk"] * p["Ds"]) for p in pres])
        for (g, d), pre, tm in zip(inst, pres, tms):
            rows = slice(g * HS, (g + 1) * HS)
            u_ref, w_ref, ks_ref, qd_ref, aqk_ref, eg_ref, tm_ref = outs[7 * d:7 * d + 7]
            c = _dn_chunk_post(pre, tm)
            tm_ref[rows, :] = tm
            u_ref[rows, :] = c["u"]
            w_ref[rows, :] = c["w"].astype(BF16)
            ks_ref[rows, :] = c["ks"].astype(BF16)
            qd_ref[rows, :] = c["qd"].astype(BF16)
            aqk_ref[rows, :] = c["aqk"].astype(BF16)
            egs = [jnp.broadcast_to(jnp.exp(gl), (1, LANES)) for gl in c["gls"]]
            eg_ref[g * 8:(g + 1) * 8, :] = jnp.concatenate(egs + [jnp.zeros((8 - NH, LANES), F32)], axis=0)

    st = lambda w_: pl.BlockSpec((G * HS, w_), lambda i: (i, 0))
    one = [st(DH)] * 4 + [st(HS), pl.BlockSpec((G * 8, LANES), lambda i: (i, 0)), st(HS)]
    shp = [_S((n * HS, DH)), _S((n * HS, DH), BF16), _S((n * HS, DH), BF16), _S((n * HS, DH), BF16),
           _S((n * HS, HS), BF16), _S((n * 8, LANES)), _S((n * HS, HS))]
    res = _call(
        body, "dn_chunks_fwd", (n // G,),
        [_rows(G * CH, 3 * DN), _rows(G * CH, LANES), _full((2, CH, CH)), _full((1, LANES)), _full((1, LANES))],
        one * 2, shp * 2, rider=rider)(qkv, pab, csum, alr, dtr)
    outs, riding = (res, None) if rider is None else res
    parts = tuple(outs[:7]), tuple(outs[7:])
    return parts if rider is None else (parts, riding)


def _scan_plan(n, ncx):
    sg = next(g for g in (4, 2, 1) if n % g == 0 and ncx % g == 0)
    ng, ncg = n // sg, ncx // sg
    return sg, ((lambda i: i), (lambda i: jnp.where(i < ncg, ncg - 1 - i, ng - 1 - (i - ncg))))


def _scan_specs(order, sg):
    st = lambda w_: pl.BlockSpec((sg * HS, w_), lambda i: (order(i), 0))
    return dict(st=st(DH), aqk=st(HS), eg=pl.BlockSpec((sg * 8, LANES), lambda i: (order(i), 0)),
                tok=pl.BlockSpec((sg * CH, DN), lambda i: (order(i), 0)),
                state=pl.BlockSpec((sg, DN, DH), lambda i: (order(i), 0, 0)))


def _scan_fwd(parts, T, tc, rider=None):
    n = T // CH
    sg, orders = _scan_plan(n, tc // CH)

    def body(*refs):
        S_f, S_b = refs[-2:]

        @pl.when(pl.program_id(0) == 0)
        def _():
            S_f[...] = jnp.zeros_like(S_f)
            S_b[...] = jnp.zeros_like(S_b)

        for g in range(sg):
            for d, S in enumerate((S_f, S_b)):
                u_ref, w_ref, ks_ref, qd_ref, aqk_ref, eg_ref = refs[6 * d:6 * d + 6]
                o_ref, ss_ref, vn_ref = refs[12 + 3 * d:15 + 3 * d]
                k = g if d == 0 else sg - 1 - g
                rows = slice(k * HS, (k + 1) * HS)
                ss_ref[k] = S[...]
                Sh = [S[_hs(h), :] for h in range(NH)]
                wh, ksh, qdh = _heads(w_ref[rows, :]), _heads(ks_ref[rows, :]), _heads(qd_ref[rows, :])
                vn = u_ref[rows, :] - jnp.concatenate([_dot(wh[h], Sh[h], NN) for h in range(NH)], axis=0)
                vn_ref[rows, :] = vn
                av, vnh = _heads(_dot(aqk_ref[rows, :], vn, NN)), _heads(vn)
                for h in range(NH):
                    o_ref[k * CH:(k + 1) * CH, _hs(h)] = _dot(qdh[h], Sh[h], NN) + av[h]
                    S[_hs(h), :] = Sh[h] * eg_ref[k * 8 + h:k * 8 + h + 1, :] + _dot(ksh[h], vnh[h], TN)

    ins, outs, shp = [], [], []
    for d in range(2):
        sp = _scan_specs(orders[d], sg)
        ins += [sp["st"]] * 4 + [sp["aqk"], sp["eg"]]
        outs += [sp["tok"], sp["state"], sp["st"]]
        shp += [_S((T, DN)), _S((n, DN, DH)), _S((n * HS, DH))]
    res = _call(body, "scan_fwd", (n // sg,), ins, outs, shp,
                scratch=[pltpu.VMEM((DN, DH), F32), pltpu.VMEM((DN, DH), F32)], rider=rider)(*parts[0][:6], *parts[1][:6])
    res, riding = (res, None) if rider is None else res
    out = tuple(res[:3]), tuple(res[3:])
    return out if rider is None else (out, riding)


def _scan_bwd(do, parts, tc):
    T = do.shape[0]
    n = T // CH
    sg, fwd_orders = _scan_plan(n, tc // CH)
    orders = [lambda s, f=f: f(n // sg - 1 - s) for f in fwd_orders]

    def body(*refs):
        dS_f, dS_b = refs[-2:]

        @pl.when(pl.program_id(0) == 0)
        def _():
            dS_f[...] = jnp.zeros_like(dS_f)
            dS_b[...] = jnp.zeros_like(dS_b)

        for g in range(sg):
            for d, dS in enumerate((dS_f, dS_b)):
                do_ref, w_ref, ks_ref, qd_ref, aqk_ref, eg_ref = refs[6 * d:6 * d + 6]
                dvn_ref, dss_ref = refs[12 + 2 * d:14 + 2 * d]
                k = sg - 1 - g if d == 0 else g
                rows = slice(k * HS, (k + 1) * HS)
                dss_ref[k] = dS[...]
                dSh = [dS[_hs(h), :] for h in range(NH)]
                wh, ksh, qdh = _heads(w_ref[rows, :]), _heads(ks_ref[rows, :]), _heads(qd_ref[rows, :])
                do_st = _stack(do_ref[k * CH:(k + 1) * CH, :])
                dvn = (_dot(aqk_ref[rows, :], do_st, TN)
                       + jnp.concatenate([_dot(ksh[h], dSh[h], NN) for h in range(NH)], axis=0))
                dvn_ref[rows, :] = dvn
                doh, dvnh = _heads(do_st), _heads(dvn)
                for h in range(NH):
                    dS[_hs(h), :] = (_dot(qdh[h], doh[h], TN) + dSh[h] * eg_ref[k * 8 + h:k * 8 + h + 1, :]
                                     - _dot(wh[h], dvnh[h], TN))

    ins, outs, shp, args = [], [], [], []
    for d in range(2):
        sp = _scan_specs(orders[d], sg)
        ins += [sp["tok"]] + [sp["st"]] * 3 + [sp["aqk"], sp["eg"]]
        outs += [sp["st"], sp["state"]]
        shp += [_S((n * HS, DH)), _S((n, DN, DH))]
        args += [do, *parts[d][1:6]]
    res = _call(body, "scan_bwd", (n // sg,), ins, outs, shp,
                scratch=[pltpu.VMEM((DN, DH), F32), pltpu.VMEM((DN, DH), F32)])(*args)
    return tuple(res[:2]), tuple(res[2:])


def _dn_chunks_bwd(qkv, pab, alr, dtr, do, fwd, bwd, rider=None):
    T = qkv.shape[0]
    n = T // CH
    G = _chunk_group(n)
    csum, csum_t = _scan_consts()

    def body(q_ref, p_ref, do_ref, cs_ref, cst_ref, al_ref, dt_ref, *refs):
        dq_refs, dp_refs, acc_ref = refs[14:16], refs[16:18], refs[18]

        @pl.when(pl.program_id(0) == 0)
        def _():
            acc_ref[...] = jnp.zeros_like(acc_ref)

        lane = _iota((CH, LANES), 1)
        inst = [(g, d) for g in range(G) for d in range(2)]
        cs, Ss, dS2s, dos, vns, dvns = [], [], [], [], [], []
        for g, d in inst:
            tok, rows = slice(g * CH, (g + 1) * CH), slice(g * HS, (g + 1) * HS)
            vn_ref, dvn_ref, ss_ref, dss_ref, tm_ref, u_ref, w_ref = refs[7 * d:7 * d + 7]
            cs.append(_dn_chunk_post(
                _dn_chunk_pre(q_ref[tok, :], p_ref[tok, :], al_ref[...], dt_ref[...], cs_ref[d], d), tm_ref[rows, :],
                uw=(u_ref[rows, :], w_ref[rows, :])))
            Ss.append([ss_ref[g, _hs(h), :] for h in range(NH)])
            dS2s.append([dss_ref[g, _hs(h), :] for h in range(NH)])
            dos.append(_stack(do_ref[tok, :]))
            vns.append(vn_ref[rows, :])
            dvns.append(dvn_ref[rows, :])
        for (g, d), c, (dq, dk, dv, dgc, dbeta) in zip(inst, cs, _dn_chunks_bwd_math(cs, Ss, dS2s, dos, vns, dvns)):
            tok = slice(g * CH, (g + 1) * CH)
            dgcm = jnp.zeros((CH, LANES), F32)
            dbm = jnp.zeros((CH, LANES), F32)
            for h, (a, b_, c_, e, f) in enumerate(zip(*map(_heads, (dq, dk, dv, dgc, dbeta)))):
                dq_refs[d][tok, _hs(h)] = a
                dq_refs[d][tok, _hs(NH + h)] = b_
                dq_refs[d][tok, _hs(2 * NH + h)] = c_
                dgcm = jnp.where(lane == d * NH + h, e, dgcm)
                dbm = jnp.where(lane == 8 + d * NH + h, f, dbm)
            dgm = _dot(cst_ref[d], dgcm, NN, hi=True)
            dsp = dgm * c["nexp"] * jax.nn.sigmoid(c["sp_in"])
            dp_refs[d][tok, :] = dsp + dbm * c["bm"] * (1.0 - c["bm"])
            acc_ref[0:1, :] += _colsum(dgm * c["gm"])
            acc_ref[1:2, :] += _colsum(dsp)

    st = pl.BlockSpec((G * HS, DH), lambda i: (i, 0))
    state = pl.BlockSpec((G, DN, DH), lambda i: (i, 0, 0))
    return _call(
        body, "dn_chunks_bwd", (n // G,),
        [_rows(G * CH, 3 * DN), _rows(G * CH, LANES), _rows(G * CH, DN), _full((2, CH, CH)), _full((2, CH, CH)),
         _full((1, LANES)), _full((1, LANES))]
        + [st, st, state, state, pl.BlockSpec((G * HS, HS), lambda i: (i, 0)), st, st] * 2,
        [_rows(G * CH, 3 * DN)] * 2 + [_rows(G * CH, LANES)] * 2 + [_full((8, LANES))],
        [_S((T, 3 * DN))] * 2 + [_S((T, LANES))] * 2 + [_S((8, LANES))], rider=rider)(
            qkv, pab, do, csum, csum_t, alr, dtr, *fwd, *bwd)


def _head_out(o, z, g):
    on = o * lax.rsqrt(jnp.mean(o * o, axis=-1, keepdims=True) + EPS) * g
    return on * _silu(z)


def _mix_branches(of_ref, ob_ref, z_ref, yp_ref, ys_ref, pg_ref, gdn_ref, wa_ref, wb_ref, wc_ref):
    ons, ya = [], None
    for h in range(NH):
        on = _head_out(of_ref[:, _hs(h)] + ob_ref[:, _hs(h)], z_ref[:, _hs(h)], gdn_ref[...])
        t = _dot(on, wa_ref[_hs(h), :], NN)
        ya = t if ya is None else ya + t
        ons.append(on)
    ys = [ya, _dot(yp_ref[...], wb_ref[...], NN), _dot(ys_ref[...], wc_ref[...], NN)]
    sg = [jax.nn.sigmoid(pg_ref[:, k * D:(k + 1) * D]) for k in range(3)]
    return ons, ys, sg


def _mix_fwd(X, of, ob, z, yp, ys, pg, mv, gdn, wa, wb, wc, wo, tc, tt):
    T = X.shape[0]

    def body(x_ref, of_ref, ob_ref, z_ref, yp_ref, ys_ref, pg_ref, mv_ref, gdn_ref, wa_ref, wb_ref, wc_ref, wo_ref,
             x1_ref):
        _, yb, sg = _mix_branches(of_ref, ob_ref, z_ref, yp_ref, ys_ref, pg_ref, gdn_ref, wa_ref, wb_ref, wc_ref)
        mix = _dot(sg[0] * yb[0] + sg[1] * yb[1] + sg[2] * yb[2], wo_ref[...], NN)
        _, gate = _stream_rows(mv_ref, pl.program_id(0), tt, tc, 2)
        x1_ref[...] = x_ref[...] + gate * mix

    return _call(
        body, "mix_fwd", (T // tt,),
        [_rows(tt, D), _rows(tt, DN), _rows(tt, DN), _rows(tt, DN), _rows(tt, PW), _rows(tt, PW), _rows(tt, 3 * D),
         _full((8, D)), _full((1, DH)), _full(wa.shape), _full(wb.shape), _full(wc.shape), _full(wo.shape)],
        _rows(tt, D), _S((T, D)))(X, of, ob, z, yp, ys, pg, mv, gdn, wa, wb, wc, wo)


def _mix_bwd(dx1, of, ob, z, yp, ys, pg, mv, gdn, wa, wb, wc, wo, tc, tt, rider=None):
    T = dx1.shape[0]

    def body(dx_ref, of_ref, ob_ref, z_ref, yp_ref, ys_ref, pg_ref, mv_ref, gdn_ref, wa_ref, wb_ref, wc_ref, wo_ref,
             do_ref, dz_ref, dyp_ref, dys_ref, dpg_ref, dwa_ref, dwb_ref, dwc_ref, dwo_ref, dgdn_ref, dm_ref):
        i = pl.program_id(0)

        @pl.when(i == 0)
        def _():
            for r in (dwa_ref, dwb_ref, dwc_ref, dwo_ref, dgdn_ref, dm_ref):
                r[...] = jnp.zeros_like(r)

        ons, yb, sg = _mix_branches(of_ref, ob_ref, z_ref, yp_ref, ys_ref, pg_ref, gdn_ref, wa_ref, wb_ref, wc_ref)
        ymix = sg[0] * yb[0] + sg[1] * yb[1] + sg[2] * yb[2]
        isc, gate = _stream_rows(mv_ref, i, tt, tc, 2)
        dx = dx_ref[...]
        dmix = dx * gate
        _acc_stream(dm_ref, 2, isc, dx * _dot(ymix, wo_ref[...], NN))
        dwo_ref[...] += _dot(ymix, dmix, TN)
        dymix = _dot(dmix, wo_ref[...], NT)
        dyb = []
        for k in range(3):
            dyb.append(dymix * sg[k])
            dpg_ref[:, k * D:(k + 1) * D] = dymix * yb[k] * sg[k] * (1.0 - sg[k])
        dwb_ref[...] += _dot(yp_ref[...], dyb[1], TN)
        dwc_ref[...] += _dot(ys_ref[...], dyb[2], TN)
        dyp_ref[...] = _dot(dyb[1], wb_ref[...], NT)
        dys_ref[...] = _dot(dyb[2], wc_ref[...], NT)
        dg = jnp.zeros((1, DH), F32)
        for h in range(NH):
            dwa_ref[_hs(h), :] += _dot(ons[h], dyb[0], TN)
            don = _dot(dyb[0], wa_ref[_hs(h), :], NT)
            _, vjp = jax.vjp(_head_out, of_ref[:, _hs(h)] + ob_ref[:, _hs(h)], z_ref[:, _hs(h)], gdn_ref[...])
            do_h, dz_h, dg_h = vjp(don)
            do_ref[:, _hs(h)] = do_h
            dz_ref[:, _hs(h)] = dz_h
            dg = dg + dg_h
        dgdn_ref[...] += dg

    return _call(
        body, "mix_bwd", (T // tt,),
        [_rows(tt, D), _rows(tt, DN), _rows(tt, DN), _rows(tt, DN), _rows(tt, PW), _rows(tt, PW), _rows(tt, 3 * D),
         _full((8, D)), _full((1, DH)), _full(wa.shape), _full(wb.shape), _full(wc.shape), _full(wo.shape)],
        [_rows(tt, DN), _rows(tt, DN), _rows(tt, PW), _rows(tt, PW), _rows(tt, 3 * D),
         _full(wa.shape), _full(wb.shape), _full(wc.shape), _full(wo.shape), _full((1, DH)), _full((8, D))],
        [_S((T, DN)), _S((T, DN)), _S((T, PW)), _S((T, PW)), _S((T, 3 * D)),
         _S(wa.shape), _S(wb.shape), _S(wc.shape), _S(wo.shape), _S((1, DH)), _S((8, D))], rider=rider)(
            dx1, of, ob, z, yp, ys, pg, mv, gdn, wa, wb, wc, wo)


def _ffn_fwd(X1, mv, g, wgu, wd, tc, tt):
    T = X1.shape[0]

    def body(x_ref, mv_ref, g_ref, wgu_ref, wd_ref, x2_ref, ff_ref):
        i = pl.program_id(0)
        _, sh = _stream_rows(mv_ref, i, tt, tc, 0)
        _, sc = _stream_rows(mv_ref, i, tt, tc, 1)
        _, gate = _stream_rows(mv_ref, i, tt, tc, 2)
        x = x_ref[...]
        gu = _dot(_modulate(x, g_ref[...], sh, sc), wgu_ref[...], NN)
        ff = _dot(_silu(gu[:, :DFF]) * gu[:, DFF:], wd_ref[...], NN)
        ff_ref[...] = ff
        x2_ref[...] = x + gate * ff

    return _call(
        body, "ffn_fwd", (T // tt,),
        [_rows(tt, D), _full((8, D)), _full((1, D)), _full(wgu.shape), _full(wd.shape)],
        [_rows(tt, D)] * 2, [_S((T, D))] * 2)(X1, mv, g, wgu, wd)


def _ffn_bwd(X1, ff, dx2, mv, g, wgu, wd, tc, tt, rider=None):
    T = X1.shape[0]

    def body(x_ref, ff_ref, dx2_ref, mv_ref, g_ref, wgu_ref, wd_ref, dx1_ref, ht_ref, dgu_ref, actt_ref, dff_ref, dg_ref,
             dm_ref):
        i = pl.program_id(0)
        isc, sh = _stream_rows(mv_ref, i, tt, tc, 0)
        _, sc = _stream_rows(mv_ref, i, tt, tc, 1)
        _, gate = _stream_rows(mv_ref, i, tt, tc, 2)
        x, dx2_ = x_ref[...], dx2_ref[...]
        h, vjp = jax.vjp(_modulate, x, g_ref[...], sh, sc)
        ht_ref[...] = h.T.astype(BF16)
        gu = jnp.dot(h.astype(BF16), wgu_ref[...], preferred_element_type=F32)
        ga, up = gu[:, :DFF], gu[:, DFF:]
        sg = jax.nn.sigmoid(ga)
        actt_ref[...] = (ga * sg * up).T.astype(BF16)
        dff = dx2_ * gate
        dff_ref[...] = dff.astype(BF16)
        dact = _dot(dff, wd_ref[...], NT)
        dga = (dact * up * (sg * (1.0 + ga * (1.0 - sg)))).astype(BF16)
        dup = (dact * ga * sg).astype(BF16)
        dgu_ref[:, :DFF] = dga
        dgu_ref[:, DFF:] = dup
        dh = _dot(dga, wgu_ref[:, :DFF], NT) + _dot(dup, wgu_ref[:, DFF:], NT)
        dx, dg, dsh, dsc = vjp(dh)
        dx1_ref[...] = dx2_ + dx

        @pl.when(i == 0)
        def _():
            dg_ref[...] = jnp.zeros_like(dg_ref)
            dm_ref[...] = jnp.zeros_like(dm_ref)

        dg_ref[...] += dg
        _acc_stream(dm_ref, 0, isc, dsh)
        _acc_stream(dm_ref, 1, isc, dsc)
        _acc_stream(dm_ref, 2, isc, dx2_ * ff_ref[...])

    return _call(
        body, "ffn_bwd", (T // tt,),
        [_rows(tt, D), _rows(tt, D), _rows(tt, D), _full((8, D)), _full((1, D)), _full(wgu.shape), _full(wd.shape)],
        [_rows(tt, D), _cols(D, tt), _rows(tt, 2 * DFF), _cols(DFF, tt), _rows(tt, D), _full((1, D)), _full((8, D))],
        [_S((T, D)), _S((D, T), BF16), _S((T, 2 * DFF), BF16), _S((DFF, T), BF16), _S((T, D), BF16),
         _S((1, D)), _S((8, D))], rider=rider)(X1, ff, dx2, mv, g, wgu, wd)


def _rms(x, g):
    return x * lax.rsqrt(jnp.mean(x * x, axis=-1, keepdims=True) + EPS) * g


def _loss_head(X2, tgt, gf, tc):
    T = X2.shape[0]

    def body(x_ref, t_ref, g_ref, dx_ref, loss_ref, dg_ref):
        i = pl.program_id(0)

        @pl.when(i == 0)
        def _():
            dx_ref[...] = jnp.zeros_like(dx_ref)
            loss_ref[...] = jnp.zeros_like(loss_ref)
            dg_ref[...] = jnp.zeros_like(dg_ref)

        @pl.when(i > 0)
        def _():
            y, vjp = jax.vjp(_rms, x_ref[...], g_ref[...])
            err = y - t_ref[...]
            dx, dg = vjp(err * (1.0 / D))
            dx_ref[...] = dx
            dg_ref[...] += dg
            loss_ref[...] += (0.5 / D) * jnp.sum(jnp.sum(err * err, axis=1, keepdims=True), axis=0, keepdims=True)

    return _call(
        body, "loss_head", (T // tc,),
        [_rows(tc, D), pl.BlockSpec((tc, D), lambda i: (jnp.maximum(i - 1, 0), 0)), _full((1, D))],
        [_rows(tc, D), _full((8, LANES)), _full((1, D))],
        [_S((T, D)), _S((8, LANES)), _S((1, D))])(X2, tgt, gf)


def _block_diag(pw):
    g, n = pw.shape[0], pw.shape[1]
    out = jnp.zeros((g * n, g * n), pw.dtype)
    for k in range(g):
        out = lax.dynamic_update_slice(out, pw[k], (k * n, k * n))
    return out


IN_TRUE = tuple(IN_BOUNDS[k + 1] - IN_BOUNDS[k] for k in range(8))


def _overlaps(widths, cw):
    starts = np.cumsum([0] + list(widths))
    out = []
    for k in range(N_DEV):
        for i in range(len(widths)):
            a, b = max(k * cw, starts[i]), min((k + 1) * cw, starts[i + 1])
            if a < b:
                out.append((k, i, int(a - k * cw), int(a - starts[i]), int(b - a)))
    return out


def _shards_to_cols(gathered, l, widths, padded):
    nd, _, R, cw = gathered.shape
    tr = _shard_rows(R)

    def body(s_ref, *o_refs):
        for i, o_ref in enumerate(o_refs):
            if padded[i] > widths[i]:
                o_ref[...] = jnp.zeros_like(o_ref)
        for k, i, so, go, n in _overlaps(widths, cw):
            o_refs[i][:, go:go + n] = s_ref[k, 0, :, so:so + n].astype(BF16)

    return _call(
        body, "shards_to_cols", (R // tr,), [pl.BlockSpec((nd, 1, tr, cw), lambda i: (0, l, i, 0))],
        [_rows(tr, p) for p in padded], [_S((R, p), BF16) for p in padded])(gathered)


def _cols_to_shards(groups, widths, cw):
    R = groups[0].shape[0]
    tr = _shard_rows(R)

    def body(*refs):
        o_ref = refs[-1]
        for k, i, so, go, n in _overlaps(widths, cw):
            o_ref[k % 2, k // 2, 0, :, so:so + n] = refs[i][:, go:go + n]

    return _call(
        body, "cols_to_shards", (R // tr,), [_rows(tr, g.shape[1]) for g in groups],
        pl.BlockSpec((2, N_CHIP, 1, tr, cw), lambda i: (0, 0, 0, i, 0)), _S((2, N_CHIP, 1, R, cw)))(*groups)


def _mod_rows(mods_l, k0):
    rows = [mods_l[s, (k0 + k) * D:(k0 + k + 1) * D] for s in (0, 1) for k in range(3)]
    return jnp.stack(rows + [jnp.zeros((D,), F32)] * 2)


def _lane_row(v8):
    return jnp.pad(v8.reshape(1, 8), ((0, 0), (0, LANES - 8)))


LAYERED = ("w_in", "w_br_a", "w_br_b", "w_br_c", "w_o", "w_gu", "w_down")
LATE = ("w_br_a", "w_br_b", "w_br_c", "w_o", "w_gu", "w_down")
FFN_W = ("w_gu", "w_down")


def _device_step(x, c, ctx, tgt, wts, tt, comm=None):
    tc = ctx.shape[0]
    X = jnp.concatenate([ctx, x], axis=0)
    if comm is None:
        row = 0
        cc = jnp.concatenate([c, jnp.zeros((CTX_ROW - 1, D), F32), wts["c_ctx"][None, :],
                              jnp.zeros((CC_ROWS - CTX_ROW - 1, D), F32)], axis=0)
        w_ada = wts["w_ada"].astype(BF16)
        mods16 = _mod_fwd(cc, w_ada, wts["b_ada"].reshape(NL, 1, 6 * D))
    else:
        row, mods16 = comm.adaln_fwd(c, wts["c_ctx"])
    mods = jnp.stack([mods16[:, CTX_ROW], lax.dynamic_index_in_dim(mods16, row, 1, keepdims=False)], axis=1)

    saved = []
    for l in range(NL):
        ws = _shards_to_cols(wts["w_in"][l], 0, IN_TRUE, IN_WIDTHS)
        mv1, mv2 = _mod_rows(mods[l], 0), _mod_rows(mods[l], 3)
        g1, g2 = wts["norm1_g"][l][None, :], wts["norm2_g"][l][None, :]
        cw, scw = wts["dn_conv_w"][l], wts["sc_conv_w"][l]
        alr, dtr = _lane_row(wts["dn_a_log"][l]), _lane_row(wts["dn_dt_bias"][l])
        gdn = wts["dn_norm_g"][l][None, :]
        pwbd, ps = _block_diag(wts["pool_w"][l]), wts["pool_scale"][l][None, :]
        hb, pq, pz, pab, pp, sx, sb, sc_, pg = _inproj_fwd(X, mv1, g1, ws, tc, tt)
        qkv = _dnprep_fwd(pq, cw, tc, tt)
        if comm is not None and l == 0:
            parts, riding = _dn_chunks_fwd(qkv, pab, alr, dtr, rider=comm.late_weights_chips())
            ((of, ssf, vnf), (ob, ssb, vnb)), riding = _scan_fwd(parts, X.shape[0], tc,
                                                                 rider=comm.late_weights_pair(riding))
            late, w_in_1 = comm.late_weights(riding)
            wts = dict(wts, **late, w_in=[wts["w_in"][0], w_in_1])
        else:
            parts = _dn_chunks_fwd(qkv, pab, alr, dtr)
            (of, ssf, vnf), (ob, ssb, vnb) = _scan_fwd(parts, X.shape[0], tc)
        wbr = [_shards_to_cols(wts[k], l, (2 * DFF,), (2 * DFF,))[0] if k == "w_gu" else wts[k][l].astype(BF16)
               for k in LATE]
        yp = _pool_fwd(pp, pwbd, ps, tc)
        ys = _sc_fwd(sx, sb, sc_, scw, tc, tt)
        X1 = _mix_fwd(X, of, ob, pz, yp, ys, pg, mv1, gdn, *wbr[:4], tc, tt)
        X2, ff = _ffn_fwd(X1, mv2, g2, wbr[4], wbr[5], tc, tt)
        saved.append(dict(X=X, X1=X1, ff=ff, ws=ws, wbr=wbr, mv1=mv1, mv2=mv2, g1=g1, g2=g2, cw=cw, scw=scw, alr=alr, dtr=dtr,
                          gdn=gdn, pwbd=pwbd, ps=ps, hb=hb, pq=pq, pz=pz, pab=pab, pp=pp, sx=sx, sb=sb, sc=sc_, pg=pg,
                          qkv=qkv, of=of, ob=ob, ssf=ssf, ssb=ssb, vnf=vnf, vnb=vnb, parts=parts, yp=yp, ys=ys))
        X = X2

    dX, loss, dgf = _loss_head(X, tgt, wts["final_norm_g"][None, :], tc)

    gl = {k: [None] * NL for k in ("w_in", "norm1_g", "norm2_g", "dn_conv_w", "dn_a_log", "dn_dt_bias", "dn_norm_g",
                                   "pool_w", "pool_scale", "sc_conv_w", "w_br_a", "w_br_b", "w_br_c", "w_o", "w_gu",
                                   "w_down")}
    dmods = [None] * NL
    early = None
    for l in reversed(range(NL)):
        s = saved[l]
        hide = comm is not None and l == 0
        res = _ffn_bwd(s["X1"], s["ff"], dX, s["mv2"], s["g2"], s["wbr"][4], s["wbr"][5], tc, tt,
                       rider=comm.grad_pair_rider([gl[k][1] for k in LAYERED]) if hide else None)
        if hide:
            res, got = res
        dx1, h2, dgu, act, dff, dg2, dm2 = res
        gl["w_gu"][l] = _cols_to_shards([_dw(h2, dgu, tt)], (2 * DFF,), 2 * DFF // N_DEV)
        gl["w_down"][l] = _dw(act, dff, tt)
        res = _mix_bwd(dx1, s["of"], s["ob"], s["pz"], s["yp"], s["ys"], s["pg"], s["mv1"], s["gdn"], *s["wbr"][:4], tc,
                       tt, rider=comm.grad_pair_rider([gl[k][0] for k in FFN_W], FFN_W) if hide else None)
        if hide:
            res, got_ffn = res
            chip_rider = comm.grad_chip_rider(got + got_ffn)
        do, dz, dyp, dys, dpg, dwa, dwb, dwc, dwo, dgdn, dmg = res
        dpp, dpw, dps = _pool_bwd(s["pp"], s["pwbd"], s["ps"], dyp, tc)
        dsx, dsb, dsc, dscw = _sc_bwd(s["sx"], s["sb"], s["sc"], s["scw"], dys, tc, tt)
        (dvnf, dssf), (dvnb, dssb) = _scan_bwd(do, s["parts"], tc)
        res = _dn_chunks_bwd(s["qkv"], s["pab"], s["alr"], s["dtr"], do,
                             (s["vnf"], dvnf, s["ssf"], dssf, s["parts"][0][6], *s["parts"][0][:2]),
                             (s["vnb"], dvnb, s["ssb"], dssb, s["parts"][1][6], *s["parts"][1][:2]),
                             rider=chip_rider if hide else None)
        if hide:
            res, early = res
            early = comm.grad_chip_done(early)
        dqf, dqb, dpf, dpb, gacc = res
        dy = _dnprep_bwd_act(s["pq"], s["cw"], dqf, dqb, tc, tt)
        dpq, dcw = _conv_bwd(dy, s["pq"], s["cw"], tc, tt)
        dps_ = [dpq, dz, dpf, dpb, dpp, dsx, dsb, dsc, dpg]
        dp_w = [0, 1, 2, 2, 3, 4, 5, 6, 7]
        dX, dg1, dm1 = _inproj_bwd(s["X"], s["mv1"], s["g1"], s["ws"], dps_, dp_w, dx1, tc, tt)
        dws = [_dw(s["hb"], dp, tt) for dp in (dpq, dz, dpf + dpb, dpp, dsx, dsb, dsc, dpg)]
        gl["w_in"][l] = _cols_to_shards(dws, IN_TRUE, IN_BOUNDS[-1] // N_DEV)
        gl["norm1_g"][l], gl["norm2_g"][l] = dg1[0], dg2[0]
        gl["dn_conv_w"][l], gl["sc_conv_w"][l] = dcw, dscw
        gl["dn_a_log"][l], gl["dn_dt_bias"][l] = gacc[0, :8].reshape(2, NH), gacc[1, :8].reshape(2, NH)
        gl["dn_norm_g"][l] = dgdn[0]
        gl["pool_w"][l] = jnp.stack([dpw[k * GW:(k + 1) * GW, k * GW:(k + 1) * GW] for k in range(4)])
        gl["pool_scale"][l] = dps[0]
        gl["w_br_a"][l], gl["w_br_b"][l], gl["w_br_c"][l], gl["w_o"][l] = dwa, dwb, dwc, dwo
        dm = dm1 + dmg
        cat = lambda r: jnp.concatenate([dm[r], dm[r + 1], dm[r + 2], dm2[r], dm2[r + 1], dm2[r + 2]])
        dmods[l] = jnp.stack([cat(0), cat(3)])

    dmods = jnp.stack(dmods)
    grads = {k: (v if k in LAYERED else jnp.stack(v)) for k, v in gl.items()}
    if comm is None:
        dm16 = jnp.zeros((NL, CC_ROWS, 6 * D), F32).at[:, CTX_ROW].set(dmods[:, 0]).at[:, row].set(dmods[:, 1])
        dwada, dcc = _mod_bwd(cc, w_ada, dm16)
        grads.update(w_ada=dwada, b_ada=dmods[:, 0] + dmods[:, 1], c_ctx=dcc[CTX_ROW])
    else:
        grads.update(comm.adaln_bwd(dmods))
    grads.update(final_norm_g=dgf[0])
    return loss, dX[tc:], grads, early


def _me():
    return lax.axis_index("x"), lax.axis_index("y"), lax.axis_index("c")


def _dev_index(p):
    return 4 * p[0] + 2 * p[1] + p[2]


def _allgather(parts):
    n = len(parts)

    def body(*refs):
        ins, outs = refs[:n], refs[n:2 * n]
        send_sems, recv_sems = refs[2 * n:]
        x, y, c = _me()
        me, sibling = (x, y, c), (x, y, 1 - c)
        chips = [(1 - x, y), (x, 1 - y), (1 - x, 1 - y)]

        def copy(a, k, block, to, src=None):
            dst = outs[a].at[_dev_index(block)]
            return pltpu.make_async_remote_copy(
                src_ref=dst if src is None else src, dst_ref=dst, send_sem=send_sems.at[a, k], recv_sem=recv_sems.at[a, k],
                device_id=to, device_id_type=MESH_ID)

        first, passed = [], []
        for a in range(n):
            first.append(copy(a, 0, me, sibling, src=ins[a]))
            first += [copy(a, 1 + j, me, (*chip, c), src=ins[a]) for j, chip in enumerate(chips)]
        for cp in first:
            cp.start()
        for a in range(n):
            for j, chip in enumerate(chips):
                copy(a, 1 + j, (*chip, c), me).wait_recv()
                passed.append(copy(a, 4 + j, (*chip, c), sibling))
                passed[-1].start()
        for a in range(n):
            copy(a, 0, sibling, me).wait_recv()
            for j, chip in enumerate(chips):
                copy(a, 4 + j, (*chip, 1 - c), me).wait_recv()
        for cp in first + passed:
            cp.wait_send()

    outs = pl.pallas_call(
        body, name="allgather", in_specs=[HBM_SPEC] * n, out_specs=[HBM_SPEC] * n,
        out_shape=[_S((N_DEV,) + p.shape, p.dtype) for p in parts],
        scratch_shapes=[pltpu.SemaphoreType.DMA((n, 7)), pltpu.SemaphoreType.DMA((n, 7))],
    )(*parts)
    return [_with_own(o, p, _dev_index(_me())) for o, p in zip(outs, parts)]


def _with_own(gathered, own, index):
    return lax.dynamic_update_index_in_dim(gathered, own, index, 0)


def _broadcast_small(small, name="small_exchange"):
    def body(in_ref, out_ref, send_sems, recv_sems, local_sem):
        x, y, c = _me()
        my = _dev_index((x, y, c))
        mine = pltpu.make_async_copy(in_ref, out_ref.at[my], local_sem)
        mine.start()
        remote = []
        for k in range(1, N_DEV):
            cp = pltpu.make_async_remote_copy(
                src_ref=in_ref, dst_ref=out_ref.at[my], send_sem=send_sems.at[k - 1], recv_sem=recv_sems.at[k - 1],
                device_id=(x ^ (k >> 2), y ^ ((k >> 1) & 1), c ^ (k & 1)), device_id_type=MESH_ID)
            cp.start()
            remote.append(cp)
        for cp in remote:
            cp.wait_recv()
        for cp in remote:
            cp.wait_send()
        mine.wait()

    return pl.pallas_call(
        body, name=name, in_specs=[HBM_SPEC], out_specs=HBM_SPEC,
        out_shape=_S((N_DEV,) + small.shape, small.dtype),
        scratch_shapes=[pltpu.SemaphoreType.DMA((7,)), pltpu.SemaphoreType.DMA((7,)), pltpu.SemaphoreType.DMA],
    )(small)


def _run_rider(rider, name):
    ni, no = len(rider.ins), len(rider.out_shapes)

    def body(*refs):
        riding = (refs[:ni], refs[ni:ni + no], refs[ni + no:])
        rider.start(*riding)
        rider.wait(*riding)

    return list(pl.pallas_call(
        body, name=name, in_specs=[HBM_SPEC] * ni, out_specs=[HBM_SPEC] * no, out_shape=rider.out_shapes,
        scratch_shapes=rider.sems)(*rider.ins))


def _chip_peers(x, y):
    return [(k - 1, (x ^ (k >> 1), y ^ (k & 1))) for k in range(1, N_CHIP)]


def _pair_exchange(g2s):
    n = len(g2s)

    def copies(ins, outs, sems):
        x, y, c = _me()
        return [pltpu.make_async_remote_copy(
            src_ref=ins[a].at[1 - c, j], dst_ref=outs[a].at[j], send_sem=sems[0].at[a, j], recv_sem=sems[1].at[a, j],
            device_id=(x, y, 1 - c), device_id_type=MESH_ID) for a in range(n) for j in range(N_CHIP)], []

    return _Rider(g2s, [_S(g.shape[1:], g.dtype) for g in g2s],
                  [pltpu.SemaphoreType.DMA((n, N_CHIP)), pltpu.SemaphoreType.DMA((n, N_CHIP))], copies)


def _my_chip():
    x, y, _ = _me()
    return 2 * x + y


def _chip_exchange(s4s):
    n = len(s4s)

    def copies(ins, outs, sems):
        x, y, c = _me()
        my = 2 * x + y
        return [pltpu.make_async_remote_copy(
            src_ref=ins[a].at[2 * px + py], dst_ref=outs[a].at[my], send_sem=sems[0].at[a, k], recv_sem=sems[1].at[a, k],
            device_id=(px, py, c), device_id_type=MESH_ID) for k, (px, py) in _chip_peers(x, y) for a in range(n)], []

    return _Rider(s4s, [_S(s.shape, s.dtype) for s in s4s],
                  [pltpu.SemaphoreType.DMA((n, N_CHIP - 1)), pltpu.SemaphoreType.DMA((n, N_CHIP - 1))], copies)


def _chip_exchange_done(s4s, recvs):
    my = _my_chip()
    return [_with_own(r, lax.dynamic_index_in_dim(s, my, 0, keepdims=False), my) for s, r in zip(s4s, recvs)]


def _chip_gather(arrs):
    n = len(arrs)

    def copies(ins, outs, sems):
        x, y, c = _me()
        return [pltpu.make_async_remote_copy(
            src_ref=ins[a], dst_ref=outs[a].at[2 * x + y], send_sem=sems[0].at[a, k], recv_sem=sems[1].at[a, k],
            device_id=(px, py, c), device_id_type=MESH_ID) for k, (px, py) in _chip_peers(x, y) for a in range(n)], []

    return _Rider(arrs, [_S((N_CHIP,) + a.shape, a.dtype) for a in arrs],
                  [pltpu.SemaphoreType.DMA((n, N_CHIP - 1)), pltpu.SemaphoreType.DMA((n, N_CHIP - 1))], copies)


def _pair_gather(chips):
    n = len(chips)

    def copies(ins, outs, sems):
        x, y, c = _me()
        return [pltpu.make_async_remote_copy(
            src_ref=ins[a].at[j], dst_ref=outs[a].at[j], send_sem=sems[0].at[a, j], recv_sem=sems[1].at[a, j],
            device_id=(x, y, 1 - c), device_id_type=MESH_ID) for a in range(n) for j in range(N_CHIP)], []

    return _Rider(chips, [_S(a.shape, a.dtype) for a in chips],
                  [pltpu.SemaphoreType.DMA((n, N_CHIP)), pltpu.SemaphoreType.DMA((n, N_CHIP))], copies)


def _shard_rows(r):
    return 256 if r % 256 == 0 else r


def _pair_sum(g2, got):
    _, nc, L, R, C = g2.shape
    tr = _shard_rows(R)

    def body(a_ref, b_ref, o_ref):
        o_ref[...] = (a_ref[0] + b_ref[...]).astype(BF16)

    blk = pl.BlockSpec((1, 1, tr, C), lambda j, l, i: (j, l, i, 0))
    return _call(
        body, "pair_sum", (nc, L, R // tr),
        [pl.BlockSpec((1, 1, 1, tr, C), lambda j, l, i: (lax.axis_index("c"), j, l, i, 0)), blk], blk,
        _S(got.shape, BF16))(g2, got)


def _adam(w, g, m, v):
    m2 = ADAM_B1 * m + (1.0 - ADAM_B1) * g
    v2 = ADAM_B2 * v + (1.0 - ADAM_B2) * (g * g)
    m_hat = m2 / (1.0 - ADAM_B1 ** ADAM_STEP)
    v_hat = v2 / (1.0 - ADAM_B2 ** ADAM_STEP)
    return -ADAM_LR * (m_hat / (jnp.sqrt(v_hat) + ADAM_EPS) + ADAM_WD * w), m2, v2


def _sum_adam(recvs, w, m, v):
    L, R, C = w.shape
    tr = _shard_rows(R)
    nr = len(recvs)

    def body(*refs):
        w_ref, m_ref, v_ref, g_ref, d_ref, m2_ref, v2_ref = refs[nr:]
        g = None
        for li, r_ref in enumerate(refs[:nr]):
            s = r_ref[0, 0].astype(F32)
            for j in range(1, N_CHIP):
                s = s + r_ref[j, 0].astype(F32)
            g = s if g is None else jnp.where(pl.program_id(0) == li, s, g)
        g_ref[0] = g
        d_ref[0], m2_ref[0], v2_ref[0] = _adam(w_ref[0], g, m_ref[0], v_ref[0])

    blk = pl.BlockSpec((1, tr, C), lambda l, i: (l, i, 0))
    rspec = pl.BlockSpec((N_CHIP, 1, tr, C), (lambda l, i: (0, l, i, 0)) if nr == 1 else (lambda l, i: (0, 0, i, 0)))
    return _call(body, "sum_adam", (L, R // tr), [rspec] * nr + [blk, blk, blk], [blk] * 4, [_S(w.shape)] * 4)(
        *recvs, w, m, v)


def _adam_big(w, g, m, v):
    L, R, C = w.shape
    tr = _shard_rows(R)

    def body(w_ref, g_ref, m_ref, v_ref, d_ref, m2_ref, v2_ref):
        d_ref[0], m2_ref[0], v2_ref[0] = _adam(w_ref[0], g_ref[0], m_ref[0], v_ref[0])

    blk = pl.BlockSpec((1, tr, C), lambda l, i: (l, i, 0))
    return _call(body, "adam_big", (L, R // tr), [blk] * 4, [blk] * 3, [_S(w.shape)] * 3)(w, g, m, v)


def _sum_small(recv):
    def body(r_ref, o_ref):
        g = r_ref[0]
        for k in range(1, recv.shape[0]):
            g = g + r_ref[k]
        o_ref[...] = g

    return pl.pallas_call(body, name="sum_small", out_shape=_S(recv.shape[1:]))(recv)


def _adam_small(w, g, m, v):
    def body(w_ref, g_ref, m_ref, v_ref, d_ref, m2_ref, v2_ref):
        d_ref[...], m2_ref[...], v2_ref[...] = _adam(w_ref[...], g_ref[...], m_ref[...], v_ref[...])

    return pl.pallas_call(body, name="adam_small", out_shape=[_S(w.shape)] * 3)(w, g, m, v)


def _pack(arrs, dtype, row_mult):
    parts, offs, r = [], [], 0
    for a in arrs:
        nr = -(-a.size // LANES)
        parts.append(jnp.pad(a.reshape(-1).astype(dtype), (0, nr * LANES - a.size)))
        offs.append(r)
        r += nr
    pad = (-r) % row_mult
    if pad:
        parts.append(jnp.zeros((pad * LANES,), dtype))
    return jnp.concatenate(parts).reshape(r + pad, LANES), offs


def _unpack(packed, offs, shapes, lead=()):
    out = []
    for off, shp in zip(offs, shapes):
        size = int(np.prod(shp))
        nr = -(-size // LANES)
        flat = packed[..., off:off + nr, :].reshape(lead + (nr * LANES,))
        out.append(flat[..., :size].reshape(lead + tuple(shp)))
    return out


BIG = (("w_ada", 2), ("w_in", 2), ("w_br_a", 2), ("w_br_b", 2), ("w_br_c", 2), ("w_o", 1), ("w_gu", 2), ("w_down", 1))
CONV = ("dn_conv_w", "sc_conv_w")
REPL = ("c_ctx", "b_ada", "norm1_g", "norm2_g", "dn_a_log", "dn_dt_bias", "dn_norm_g", "pool_w", "pool_scale",
        "final_norm_g")
WEIGHTS = ("c_ctx", "w_ada", "b_ada", "norm1_g", "norm2_g", "w_in", "dn_conv_w", "dn_a_log", "dn_dt_bias", "dn_norm_g",
           "pool_w", "pool_scale", "sc_conv_w", "w_br_a", "w_br_b", "w_br_c", "w_o", "w_gu", "w_down", "final_norm_g")
TOKEN_TILE = 256


def _join(blocks, axis):
    nd, nl, r, c = blocks.shape
    if axis == 2:
        return blocks.transpose(1, 2, 0, 3).reshape(nl, r, nd * c)
    return blocks.transpose(1, 0, 2, 3).reshape(nl, nd * r, c)


def _split(full, axis):
    nl, r, c = full.shape
    if axis == 2:
        return full.reshape(nl, r, N_CHIP, 2, c // N_DEV).transpose(3, 2, 0, 1, 4)
    return full.reshape(nl, N_CHIP, 2, r // N_DEV, c).transpose(2, 1, 0, 3, 4)


PRESPLIT = ("w_in", "w_gu")


def _presplit(layer_grads, names=LAYERED):
    return [g if k in PRESPLIT else _split(g[None], dict(BIG)[k]) for k, g in zip(names, layer_grads)]


class _Comm:
    def __init__(self, late_shards, w_ada, b_ada):
        self.packed = [k for k in LATE if k not in PRESPLIT]
        self.shapes = [late_shards[k].shape for k in self.packed]
        pack, self.offs = _pack([late_shards[k] for k in self.packed], BF16, BF16_ROWS)
        self.late = [pack, late_shards["w_gu"].astype(BF16), late_shards["w_in"][1:].astype(BF16)]
        self.w_ada, self.b_ada = w_ada.astype(BF16), b_ada
        self.g2s = None

    def adaln_fwd(self, c, c_ctx):
        my = _dev_index(_me())
        ncol = self.w_ada.shape[2]
        c_all = _broadcast_small(c.reshape(8, LANES), "c_exchange").reshape(N_DEV, D)
        self.cc = jnp.concatenate([c_all, c_ctx[None, :], jnp.zeros((CC_ROWS - N_DEV - 1, D), F32)], axis=0)
        b_cols = lax.dynamic_slice_in_dim(self.b_ada, my * ncol, ncol, axis=1).reshape(NL, 1, ncol)
        cols = _mod_fwd(self.cc, self.w_ada, b_cols)
        got = _broadcast_small(cols.reshape(-1, LANES), "mods_exchange").reshape(N_DEV, NL, CC_ROWS, ncol)
        return my, got.transpose(1, 2, 0, 3).reshape(NL, CC_ROWS, N_DEV * ncol)

    def adaln_bwd(self, dmods):
        my = _dev_index(_me())
        ncol = self.w_ada.shape[2]
        got = _broadcast_small(dmods.reshape(-1, LANES), "dmods_exchange")
        rows = got.reshape(N_DEV, NL, 2, 6 * D)
        ctx_sum = _sum_small(rows[:, :, 0].reshape(N_DEV, -1, LANES)).reshape(NL, 1, 6 * D)
        db = _sum_small(rows.transpose(0, 2, 1, 3).reshape(2 * N_DEV, -1, LANES)).reshape(NL, 6 * D)
        dm = jnp.concatenate([rows[:, :, 1].transpose(1, 0, 2), ctx_sum,
                              jnp.zeros((NL, CC_ROWS - N_DEV - 1, 6 * D), F32)], axis=1)
        dw, dcc = _mod_bwd(self.cc, self.w_ada, lax.dynamic_slice_in_dim(dm, my * ncol, ncol, axis=2))
        return dict(w_ada=dw, b_ada=db, c_ctx=dcc[CTX_ROW])

    def late_weights_chips(self):
        return _chip_gather(self.late)

    def late_weights_pair(self, riding):
        self.chips = [_with_own(r, a, _my_chip()) for r, a in zip(riding, self.late)]
        return _pair_gather(self.chips)

    def late_weights(self, riding):
        on_south = lax.axis_index("c") == 0
        by_dev = []
        for mine, other in zip(self.chips, riding):
            both = jnp.stack([jnp.where(on_south, mine, other), jnp.where(on_south, other, mine)], axis=1)
            by_dev.append(both.reshape((N_DEV,) + mine.shape[1:]))
        shards = _unpack(by_dev[0], self.offs, self.shapes, (N_DEV,))
        return dict({k: _join(blocks, dict(BIG)[k]) for k, blocks in zip(self.packed, shards)}, w_gu=by_dev[1]), by_dev[2]

    def grad_pair_rider(self, layer_grads, names=LAYERED):
        g2s = _presplit(layer_grads, names)
        self.g2s = (self.g2s or []) + g2s
        return _pair_exchange(g2s)

    def grad_chip_rider(self, got):
        self.sums = [_pair_sum(g2, gt) for g2, gt in zip(self.g2s, got)]
        return _chip_exchange(self.sums)

    def grad_chip_done(self, riding):
        return _chip_exchange_done(self.sums, riding)


def kernel(x, c, ctx, c_ctx, w_ada, b_ada, norm1_g, norm2_g, w_in, dn_conv_w, dn_a_log, dn_dt_bias, dn_norm_g, pool_w, pool_scale, sc_conv_w, w_br_a, w_br_b, w_br_c, w_o, w_gu, w_down, final_norm_g, loss_target, m_c_ctx, m_w_ada, m_b_ada, m_norm1_g, m_norm2_g, m_w_in, m_dn_conv_w, m_dn_a_log, m_dn_dt_bias, m_dn_norm_g, m_pool_w, m_pool_scale, m_sc_conv_w, m_w_br_a, m_w_br_b, m_w_br_c, m_w_o, m_w_gu, m_w_down, m_final_norm_g, v_c_ctx, v_w_ada, v_b_ada, v_norm1_g, v_norm2_g, v_w_in, v_dn_conv_w, v_dn_a_log, v_dn_dt_bias, v_dn_norm_g, v_pool_w, v_pool_scale, v_sc_conv_w, v_w_br_a, v_w_br_b, v_w_br_c, v_w_o, v_w_gu, v_w_down, v_final_norm_g):
    loc = dict(c_ctx=c_ctx, w_ada=w_ada, b_ada=b_ada, norm1_g=norm1_g, norm2_g=norm2_g, w_in=w_in, dn_conv_w=dn_conv_w,
               dn_a_log=dn_a_log, dn_dt_bias=dn_dt_bias, dn_norm_g=dn_norm_g, pool_w=pool_w, pool_scale=pool_scale,
               sc_conv_w=sc_conv_w, w_br_a=w_br_a, w_br_b=w_br_b, w_br_c=w_br_c, w_o=w_o, w_gu=w_gu, w_down=w_down,
               final_norm_g=final_norm_g)
    mom_m = dict(c_ctx=m_c_ctx, w_ada=m_w_ada, b_ada=m_b_ada, norm1_g=m_norm1_g, norm2_g=m_norm2_g, w_in=m_w_in,
                 dn_conv_w=m_dn_conv_w, dn_a_log=m_dn_a_log, dn_dt_bias=m_dn_dt_bias, dn_norm_g=m_dn_norm_g,
                 pool_w=m_pool_w, pool_scale=m_pool_scale, sc_conv_w=m_sc_conv_w, w_br_a=m_w_br_a, w_br_b=m_w_br_b,
                 w_br_c=m_w_br_c, w_o=m_w_o, w_gu=m_w_gu, w_down=m_w_down, final_norm_g=m_final_norm_g)
    mom_v = dict(c_ctx=v_c_ctx, w_ada=v_w_ada, b_ada=v_b_ada, norm1_g=v_norm1_g, norm2_g=v_norm2_g, w_in=v_w_in,
                 dn_conv_w=v_dn_conv_w, dn_a_log=v_dn_a_log, dn_dt_bias=v_dn_dt_bias, dn_norm_g=v_dn_norm_g,
                 pool_w=v_pool_w, pool_scale=v_pool_scale, sc_conv_w=v_sc_conv_w, w_br_a=v_w_br_a, w_br_b=v_w_br_b,
                 w_br_c=v_w_br_c, w_o=v_w_o, w_gu=v_w_gu, w_down=v_w_down, final_norm_g=v_final_norm_g)
    my = _dev_index(_me())

    conv_pack, conv_offs = _pack([loc[k] for k in CONV], F32, 8)
    w_in0_all, conv_all = _allgather([w_in[:1].astype(BF16), conv_pack])
    full = dict({k: loc[k] for k in REPL}, w_in=[w_in0_all])
    for k, blocks in zip(CONV, _unpack(conv_all, conv_offs, [loc[k].shape for k in CONV], (N_DEV,))):
        full[k] = _join(blocks, 2)

    loss8, grad_x, g, recv_early = _device_step(x[0], c, ctx[0], loss_target[0], full, TOKEN_TILE,
                                                comm=_Comm({k: loc[k] for k in LATE + ("w_in",)}, w_ada, b_ada))

    last = [k for k in LAYERED if k not in FFN_W]
    tail = _presplit([g[k][0] for k in last], last)
    got = _run_rider(_pair_exchange(tail), "pair_exchange")
    sums = [_pair_sum(a, b) for a, b in zip(tail, got)]
    recv_tail = _chip_exchange_done(sums, _run_rider(_chip_exchange(sums), "chip_exchange"))
    recv = {k: [None, r] for k, r in zip(LAYERED, recv_early)}
    for k, r in list(zip(FFN_W, recv_early[len(LAYERED):])) + list(zip(last, recv_tail)):
        recv[k][0] = r

    small_names = REPL + CONV
    summed = [k for k in small_names if k != "b_ada"]
    small_pack, small_offs = _pack([g[k] for k in summed] + [loss8[0:1, 0:1]], F32, 8)
    small_sum = _sum_small(_broadcast_small(small_pack))
    sums = _unpack(small_sum, small_offs, [g[k].shape for k in summed] + [(1, 1)])
    grads = dict(zip(summed, sums[:-1]), b_ada=g["b_ada"], w_ada=g["w_ada"])
    loss = sums[-1][0, 0]
    for k in CONV:
        w = loc[k].shape[2]
        grads[k] = lax.dynamic_slice_in_dim(grads[k], my * w, w, axis=2)

    delta, new_m, new_v = {}, {}, {}
    for k in LAYERED:
        grads[k], delta[k], new_m[k], new_v[k] = _sum_adam(recv[k], loc[k], mom_m[k], mom_v[k])
    delta["w_ada"], new_m["w_ada"], new_v["w_ada"] = _adam_big(w_ada, g["w_ada"], m_w_ada, v_w_ada)
    packs = [_pack([src[k] for k in small_names], F32, 8)[0] for src in (loc, grads, mom_m, mom_v)]
    _, offs = _pack([loc[k] for k in small_names], F32, 8)
    shapes = [loc[k].shape for k in small_names]
    for dst, packed in zip((delta, new_m, new_v), _adam_small(*packs)):
        dst.update(zip(small_names, _unpack(packed, offs, shapes)))

    return (loss, grad_x[None], *[grads[k] for k in WEIGHTS], *[delta[k] for k in WEIGHTS],
            *[new_m[k] for k in WEIGHTS], *[new_v[k] for k in WEIGHTS])
```

```python
import numpy as np
import jax
import jax.numpy as jnp
from jax import lax
from jax.experimental import pallas as pl
from jax.experimental.pallas import tpu as pltpu

F32 = jnp.float32
BF16 = jnp.bfloat16
HI = lax.Precision.HIGHEST

D = 1024
NL = 2
NH = 4
DH = 128
DN = NH * DH
CH = 64
GW = 64
PW = 256
DFF = 2816
EPS = 1e-6
N_DEV = 8
N_CHIP = 4
MESH_ID = pl.DeviceIdType.MESH
HBM_SPEC = pl.BlockSpec(memory_space=pltpu.HBM)
LANES = 128
BF16_ROWS = 16
VMEM_MB = 56
DW_ACC_MB = 12

ADAM_LR, ADAM_B1, ADAM_B2, ADAM_EPS, ADAM_WD, ADAM_STEP = 0.001, 0.9, 0.999, 1e-08, 0.01, 10

IN_BOUNDS = (0, 1536, 2048, 2064, 2320, 2576, 2832, 3088, 6160)
IN_WIDTHS = (1536, 512, 128, 256, 256, 256, 256, 3072)
POOL_WIN = ((1, 0), (2, 1), (4, 3), (8, 7))

NN = ((1,), (0,))
NT = ((1,), (1,))
TN = ((0,), (0,))


def _dot(a, b, dims, hi=False):
    if hi:
        prec = lax.Precision.HIGH if hi == "x3" else HI
        return lax.dot_general(a, b, (dims, ((), ())), precision=prec, preferred_element_type=F32)
    return lax.dot_general(a.astype(BF16), b.astype(BF16), (dims, ((), ())), preferred_element_type=F32)


def _S(shape, dtype=F32):
    return jax.ShapeDtypeStruct(tuple(shape), dtype)


def _full(shape):
    nd = len(shape)
    return pl.BlockSpec(tuple(shape), lambda *_: (0,) * nd)


def _rows(tt, w):
    return pl.BlockSpec((tt, w), lambda i: (i, 0))


class _Rider:
    def __init__(self, ins, out_shapes, sems, copies):
        self.ins, self.out_shapes, self.sems, self.copies = list(ins), list(out_shapes), list(sems), copies

    def start(self, ins, outs, sems):
        remote, local = self.copies(ins, outs, sems)
        for cp in local + remote:
            cp.start()

    def wait(self, ins, outs, sems):
        remote, local = self.copies(ins, outs, sems)
        for cp in remote:
            cp.wait_recv()
        for cp in remote:
            cp.wait_send()
        for cp in local:
            cp.wait()


def _call(body, name, grid, in_specs, out_specs, out_shape, scratch=(), rider=None):
    params = pltpu.CompilerParams(dimension_semantics=("arbitrary",) * len(grid), vmem_limit_bytes=VMEM_MB << 20)
    if rider is None:
        return pl.pallas_call(body, name=name, grid=grid, in_specs=in_specs, out_specs=out_specs, out_shape=out_shape,
                              scratch_shapes=list(scratch), compiler_params=params)
    single = not isinstance(out_shape, (list, tuple))
    out_specs, out_shape = ([out_specs], [out_shape]) if single else (list(out_specs), list(out_shape))
    n_in, n_out, n_scr = len(in_specs), len(out_shape), len(scratch)
    r_in, r_out = len(rider.ins), len(rider.out_shapes)

    def hosted(*refs):
        ins, refs = refs[:n_in + r_in], refs[n_in + r_in:]
        outs, scr = refs[:n_out + r_out], refs[n_out + r_out:]
        riding = (ins[n_in:], outs[n_out:], scr[n_scr:])

        @pl.when(pl.program_id(0) == 0)
        def _():
            rider.start(*riding)

        body(*ins[:n_in], *outs[:n_out], *scr[:n_scr])

        @pl.when(pl.program_id(0) == grid[0] - 1)
        def _():
            rider.wait(*riding)

    call = pl.pallas_call(
        hosted, name=name, grid=grid, in_specs=list(in_specs) + [HBM_SPEC] * r_in,
        out_specs=out_specs + [HBM_SPEC] * r_out, out_shape=out_shape + rider.out_shapes,
        scratch_shapes=list(scratch) + rider.sems, compiler_params=params)

    def run(*args):
        res = call(*args, *rider.ins)
        own = res[:n_out]
        return (own[0] if single else own), list(res[n_out:])

    return run


def _iota(shape, axis):
    return lax.broadcasted_iota(jnp.int32, shape, axis)


def _colsum(a):
    return jnp.sum(a, axis=0, keepdims=True)


def _silu(x):
    return x * jax.nn.sigmoid(x)


def _modulate(x, g, sh, sc):
    xn = x * lax.rsqrt(jnp.mean(x * x, axis=-1, keepdims=True) + EPS)
    return (xn * g) * (1.0 + sc) + sh


def _stream_rows(mv_ref, i, tt, tc, k):
    isc = (i * tt + _iota((tt, 1), 0)) < tc
    return isc, jnp.where(isc, mv_ref[k:k + 1, :], mv_ref[3 + k:4 + k, :])


def _acc_stream(ref, k, isc, val):
    ref[k:k + 1, :] += _colsum(jnp.where(isc, val, 0.0))
    ref[3 + k:4 + k, :] += _colsum(jnp.where(isc, 0.0, val))


CC_ROWS = 16
CTX_ROW = 8


def _mod_cols(n):
    return 1536 if n % 1536 == 0 else n


def _mod_fwd(cc, w_ada, b_ada3):
    n = w_ada.shape[2]
    ct = _mod_cols(n)

    def body(cc_ref, w_ref, b_ref, o_ref):
        o_ref[0] = _dot(_silu(cc_ref[...]), w_ref[0], NN) + b_ref[0]

    return _call(
        body, "mod_fwd", (NL, n // ct),
        [pl.BlockSpec((CC_ROWS, D), lambda l, j: (0, 0)), pl.BlockSpec((1, D, ct), lambda l, j: (l, 0, j)),
         pl.BlockSpec((1, 1, ct), lambda l, j: (l, 0, j))],
        pl.BlockSpec((1, CC_ROWS, ct), lambda l, j: (l, 0, j)), _S((NL, CC_ROWS, n)))(cc, w_ada, b_ada3)


def _mod_bwd(cc, w_ada, dmods):
    n = w_ada.shape[2]
    ct = _mod_cols(n)

    def body(cc_ref, w_ref, dm_ref, dw_ref, dcc_ref):
        first = (pl.program_id(0) == 0) & (pl.program_id(1) == 0)
        cc_ = cc_ref[...]
        sg = jax.nn.sigmoid(cc_)
        dm = dm_ref[0]
        dw_ref[0] = _dot(cc_ * sg, dm, TN)

        @pl.when(first)
        def _():
            dcc_ref[...] = jnp.zeros_like(dcc_ref)

        dcc_ref[...] += _dot(dm, w_ref[0], NT) * (sg * (1.0 + cc_ * (1.0 - sg)))

    return _call(
        body, "mod_bwd", (NL, n // ct),
        [pl.BlockSpec((CC_ROWS, D), lambda l, j: (0, 0)), pl.BlockSpec((1, D, ct), lambda l, j: (l, 0, j)),
         pl.BlockSpec((1, CC_ROWS, ct), lambda l, j: (l, 0, j))],
        [pl.BlockSpec((1, D, ct), lambda l, j: (l, 0, j)), pl.BlockSpec((CC_ROWS, D), lambda l, j: (0, 0))],
        [_S((NL, D, n)), _S((CC_ROWS, D))])(cc, w_ada, dmods)


def _cols(n, tt):
    return pl.BlockSpec((n, tt), lambda i: (0, i))


def _inproj_fwd(X, mv, g, ws, tc, tt):
    T = X.shape[0]
    nw = len(ws)

    def body(x_ref, mv_ref, g_ref, *refs):
        w_refs, ht_ref, p_refs = refs[:nw], refs[nw], refs[nw + 1:]
        i = pl.program_id(0)
        _, sh = _stream_rows(mv_ref, i, tt, tc, 0)
        _, sc = _stream_rows(mv_ref, i, tt, tc, 1)
        h = _modulate(x_ref[...], g_ref[...], sh, sc)
        ht_ref[...] = h.T.astype(BF16)
        hb = h.astype(BF16)
        for w_ref, p_ref in zip(w_refs, p_refs):
            p_ref[...] = jnp.dot(hb, w_ref[...], preferred_element_type=F32)

    return _call(
        body, "inproj_fwd", (T // tt,),
        [_rows(tt, D), _full((8, D)), _full((1, D))] + [_full(w.shape) for w in ws],
        [_cols(D, tt)] + [_rows(tt, w.shape[1]) for w in ws],
        [_S((D, T), BF16)] + [_S((T, w.shape[1])) for w in ws])(X, mv, g, *ws)


def _inproj_bwd(X, mv, g, ws, dps, dp_w, dres, tc, tt):
    T = X.shape[0]
    nw, nd = len(ws), len(dps)

    def body(x_ref, mv_ref, g_ref, dres_ref, *refs):
        w_refs, dp_refs = refs[:nw], refs[nw:nw + nd]
        dx_ref, dg_ref, dm_ref = refs[nw + nd:]
        i = pl.program_id(0)
        isc, sh = _stream_rows(mv_ref, i, tt, tc, 0)
        _, sc = _stream_rows(mv_ref, i, tt, tc, 1)
        dh = None
        for dp_ref, k in zip(dp_refs, dp_w):
            t = _dot(dp_ref[...], w_refs[k][...], NT)
            dh = t if dh is None else dh + t
        _, vjp = jax.vjp(_modulate, x_ref[...], g_ref[...], sh, sc)
        dx, dg, dsh, dsc = vjp(dh)
        dx_ref[...] = dres_ref[...] + dx

        @pl.when(i == 0)
        def _():
            dg_ref[...] = jnp.zeros_like(dg_ref)
            dm_ref[...] = jnp.zeros_like(dm_ref)

        dg_ref[...] += dg
        _acc_stream(dm_ref, 0, isc, dsh)
        _acc_stream(dm_ref, 1, isc, dsc)

    return _call(
        body, "inproj_bwd", (T // tt,),
        [_rows(tt, D), _full((8, D)), _full((1, D)), _rows(tt, D)] + [_full(w.shape) for w in ws]
        + [_rows(tt, dp.shape[1]) for dp in dps],
        [_rows(tt, D), _full((1, D)), _full((8, D))],
        [_S((T, D)), _S((1, D)), _S((8, D))])(X, mv, g, dres, *ws, *dps)


def _dw(At, B, tt):
    K, T = At.shape
    N = B.shape[1]
    tt = 3 * tt if T % (3 * tt) == 0 else tt
    tn = max(t for t in range(LANES, N + 1, LANES) if N % t == 0 and K * t * 4 <= DW_ACC_MB << 20)

    def body(a_ref, b_ref, o_ref):
        @pl.when(pl.program_id(1) == 0)
        def _():
            o_ref[...] = jnp.zeros_like(o_ref)

        o_ref[...] += _dot(a_ref[...], b_ref[...], NN)

    return _call(
        body, "dw", (N // tn, T // tt),
        [pl.BlockSpec((K, tt), lambda j, i: (0, i)), pl.BlockSpec((tt, tn), lambda j, i: (i, j))],
        pl.BlockSpec((K, tn), lambda j, i: (0, j)), _S((K, N)))(At, B)


def _halo_specs(T, tt, cw, col):
    r8, nb8 = tt // 8, T // 8
    return [pl.BlockSpec((tt, cw), lambda j, i: (i, col(j))),
            pl.BlockSpec((8, cw), lambda j, i: (jnp.maximum(i * r8 - 1, 0), col(j))),
            pl.BlockSpec((8, cw), lambda j, i: (jnp.minimum((i + 1) * r8, nb8 - 1), col(j)))]


def _shifts(a, prev8, next8, i, tt, tc, T):
    r = _iota((tt, 1), 0)
    t = i * tt + r
    dn = jnp.where(r == 0, prev8[7:8, :], pltpu.roll(a, 1, 0))
    dn = jnp.where((t == 0) | (t == tc), 0.0, dn)
    up = jnp.where(r == tt - 1, next8[0:1, :], pltpu.roll(a, tt - 1, 0))
    up = jnp.where((t == T - 1) | (t == tc - 1), 0.0, up)
    return dn, up


def _dn_post(y, part):
    a = _silu(y)
    nrm = lax.rsqrt(jnp.sum(a * a, axis=-1, keepdims=True) + EPS)
    f = jnp.where(part == 0, nrm * (DH ** -0.5), jnp.where(part == 1, nrm, 1.0))
    return a * f


def _conv3(w_ref, dn, mid, up):
    return w_ref[0:1, :] * dn + w_ref[1:2, :] * mid + w_ref[2:3, :] * up


def _dnprep_fwd(pq, cw, tc, tt):
    T = pq.shape[0]

    def body(p_ref, pp_ref, pn_ref, w_ref, a_ref):
        part, i = pl.program_id(0), pl.program_id(1)
        p = p_ref[...]
        dn, up = _shifts(p, pp_ref[...], pn_ref[...], i, tt, tc, T)
        y = _conv3(w_ref, dn, p, up)
        for h in range(NH):
            a_ref[:, _hs(h)] = _dn_post(y[:, _hs(h)], part)

    return _call(
        body, "dnprep_fwd", (3, T // tt),
        _halo_specs(T, tt, DN, lambda j: j) + [pl.BlockSpec((3, DN), lambda j, i: (0, j))],
        pl.BlockSpec((tt, DN), lambda j, i: (i, j)), _S((T, 3 * DN)))(pq, pq, pq, cw)


def _dnprep_bwd_act(pq, cw, da_f, da_b, tc, tt):
    T = pq.shape[0]

    def body(p_ref, pp_ref, pn_ref, w_ref, df_ref, db_ref, dy_ref):
        part, i = pl.program_id(0), pl.program_id(1)
        p = p_ref[...]
        dn, up = _shifts(p, pp_ref[...], pn_ref[...], i, tt, tc, T)
        y = _conv3(w_ref, dn, p, up)
        for h in range(NH):
            _, vjp = jax.vjp(lambda yh: _dn_post(yh, part), y[:, _hs(h)])
            dy_ref[:, _hs(h)] = vjp(df_ref[:, _hs(h)] + db_ref[:, _hs(h)])[0]

    blk = pl.BlockSpec((tt, DN), lambda j, i: (i, j))
    return _call(
        body, "dnprep_bwd_act", (3, T // tt),
        _halo_specs(T, tt, DN, lambda j: j) + [pl.BlockSpec((3, DN), lambda j, i: (0, j)), blk, blk],
        blk, _S((T, 3 * DN)))(pq, pq, pq, cw, da_f, da_b)


def _conv_bwd(dy, p, cw, tc, tt):
    T, W = p.shape
    cb = DN

    def body(dy_ref, dyp_ref, dyn_ref, p_ref, pp_ref, pn_ref, w_ref, dp_ref, dw_ref):
        i = pl.program_id(1)
        dy, p_ = dy_ref[...], p_ref[...]
        ddn, dup = _shifts(dy, dyp_ref[...], dyn_ref[...], i, tt, tc, T)
        dp_ref[...] = _conv3(w_ref, dup, dy, ddn).astype(BF16)
        pdn, pup = _shifts(p_, pp_ref[...], pn_ref[...], i, tt, tc, T)

        @pl.when(i == 0)
        def _():
            dw_ref[...] = jnp.zeros_like(dw_ref)

        dw_ref[0:1, :] += _colsum(dy * pdn)
        dw_ref[1:2, :] += _colsum(dy * p_)
        dw_ref[2:3, :] += _colsum(dy * pup)

    wspec = pl.BlockSpec((3, cb), lambda j, i: (0, j))
    return _call(
        body, "conv_bwd", (W // cb, T // tt),
        _halo_specs(T, tt, cb, lambda j: j) * 2 + [wspec],
        [pl.BlockSpec((tt, cb), lambda j, i: (i, j)), wspec], [_S((T, W), BF16), _S((3, W))])(dy, dy, dy, p, p, p, cw)


def _sc_fwd(sx, sb, sc_, cw, tc, tt):
    T = sx.shape[0]

    def body(x_ref, xp_ref, xn_ref, c_ref, cp_ref, cn_ref, b_ref, w_ref, y_ref):
        i = pl.program_id(1)
        u = c_ref[...] * x_ref[...]
        dn, up = _shifts(u, cp_ref[...] * xp_ref[...], cn_ref[...] * xn_ref[...], i, tt, tc, T)
        y_ref[...] = b_ref[...] * _conv3(w_ref, dn, u, up)

    blk = pl.BlockSpec((tt, LANES), lambda j, i: (i, j))
    return _call(
        body, "sc_fwd", (PW // LANES, T // tt),
        _halo_specs(T, tt, LANES, lambda j: j) * 2 + [blk, pl.BlockSpec((3, LANES), lambda j, i: (0, j))],
        blk, _S((T, PW)))(sx, sx, sx, sc_, sc_, sc_, sb, cw)


def _sc_bwd(sx, sb, sc_, cw, dy, tc, tt):
    T = sx.shape[0]

    def body(x_ref, xp_ref, xn_ref, c_ref, cp_ref, cn_ref, b_ref, bp_ref, bn_ref, dy_ref, dyp_ref, dyn_ref, w_ref,
             dx_ref, db_ref, dc_ref, dw_ref):
        i = pl.program_id(1)
        x, c, dy_ = x_ref[...], c_ref[...], dy_ref[...]
        u = c * x
        udn, uup = _shifts(u, cp_ref[...] * xp_ref[...], cn_ref[...] * xn_ref[...], i, tt, tc, T)
        db_ref[...] = dy_ * _conv3(w_ref, udn, u, uup)
        e = dy_ * b_ref[...]
        edn, eup = _shifts(e, dyp_ref[...] * bp_ref[...], dyn_ref[...] * bn_ref[...], i, tt, tc, T)
        du = _conv3(w_ref, eup, e, edn)
        dx_ref[...] = du * c
        dc_ref[...] = du * x

        @pl.when(i == 0)
        def _():
            dw_ref[...] = jnp.zeros_like(dw_ref)

        dw_ref[0:1, :] += _colsum(e * udn)
        dw_ref[1:2, :] += _colsum(e * u)
        dw_ref[2:3, :] += _colsum(e * uup)

    blk = pl.BlockSpec((tt, LANES), lambda j, i: (i, j))
    wspec = pl.BlockSpec((3, LANES), lambda j, i: (0, j))
    return _call(
        body, "sc_bwd", (PW // LANES, T // tt),
        _halo_specs(T, tt, LANES, lambda j: j) * 4 + [wspec],
        [blk, blk, blk, wspec], [_S((T, PW))] * 3 + [_S((3, PW))])(
            sx, sx, sx, sc_, sc_, sc_, sb, sb, sb, dy, dy, dy, cw)


def _group_select(vals):
    g = _iota((1, PW), 1) // (PW // len(POOL_WIN))
    return jnp.where(g == 0, vals[0], jnp.where(g == 1, vals[1], jnp.where(g == 2, vals[2], vals[3])))


def _nested_box(get, mirror):
    acc, outs, pl_, ph_ = get(0), [], 0, 0
    for lo, hi in POOL_WIN:
        if mirror:
            lo, hi = hi, lo
        for k in range(pl_ + 1, lo + 1):
            acc = acc + get(-k)
        for k in range(ph_ + 1, hi + 1):
            acc = acc + get(k)
        pl_, ph_ = lo, hi
        outs.append(acc)
    return _group_select(outs)


def _box_tokens(a, n, mirror):
    idx = _iota((n, 1), 0)

    def get(k):
        if k == 0:
            return a
        return jnp.where((idx + k >= 0) & (idx + k < n), pltpu.roll(a, (-k) % n, 0), 0.0)

    return _nested_box(get, mirror)


def _inv_count(pos, n):
    return _group_select([1.0 / (jnp.minimum(pos + hi, n - 1) - jnp.maximum(pos - lo, 0) + 1).astype(F32)
                          for lo, hi in POOL_WIN])


def _pool_rows(ref, r, R, tc, mirror):
    def get(k):
        rr = r + k
        rc = jnp.clip(rr, 0, R - 1)
        v = ref[pl.ds(pl.multiple_of(tc + rc * GW, GW), GW), :]
        if mirror:
            v = v * _inv_count(jnp.full((1, PW), rc, jnp.int32), R)
        return jnp.where((rr >= 0) & (rr < R), v, 0.0)

    return _nested_box(get, mirror)


def _pool_fwd(u, pwbd, ps, tc):
    T = u.shape[0]
    R = (T - tc) // GW

    def body(u_ref, pw_ref, ps_ref, y_ref):
        pw, scale = pw_ref[...], ps_ref[...]
        uc = u_ref[0:tc, :]
        mc = _box_tokens(uc, tc, False) * _inv_count(_iota((tc, 1), 0), tc)
        y_ref[0:tc, :] = _dot(mc - uc, pw, NN) * scale
        inv_c = _inv_count(_iota((GW, 1), 0), GW)

        def row(r, carry):
            rs = _pool_rows(u_ref, r, R, tc, False) * _inv_count(jnp.full((1, PW), r, jnp.int32), R)
            m = _box_tokens(rs, GW, False) * inv_c
            sl = pl.ds(pl.multiple_of(tc + r * GW, GW), GW)
            y_ref[sl, :] = _dot(m - u_ref[sl, :], pw, NN) * scale
            return carry

        lax.fori_loop(0, R, row, 0)

    return pl.pallas_call(
        body, name="pool_fwd", out_shape=_S((T, PW)),
        compiler_params=pltpu.CompilerParams(vmem_limit_bytes=VMEM_MB << 20))(u, pwbd, ps)


def _pool_bwd(u, pwbd, ps, dy, tc):
    T = u.shape[0]
    R = (T - tc) // GW

    def body(u_ref, pw_ref, ps_ref, dy_ref, du_ref, dpw_ref, dps_ref, dd_ref):
        pw, scale = pw_ref[...], ps_ref[...]
        dpw_ref[...] = jnp.zeros_like(dpw_ref)
        dps_ref[...] = jnp.zeros_like(dps_ref)

        def back(d, dy_):
            dz = dy_ * scale
            dpw_ref[...] += _dot(d, dz, TN)
            dps_ref[...] += _colsum(dy_ * _dot(d, pw, NN))
            return _dot(dz, pw, NT)

        uc = u_ref[0:tc, :]
        inv_cc = _inv_count(_iota((tc, 1), 0), tc)
        ddc = back(_box_tokens(uc, tc, False) * inv_cc - uc, dy_ref[0:tc, :])
        du_ref[0:tc, :] = _box_tokens(ddc * inv_cc, tc, True) - ddc
        inv_c = _inv_count(_iota((GW, 1), 0), GW)

        def row1(r, carry):
            rs = _pool_rows(u_ref, r, R, tc, False) * _inv_count(jnp.full((1, PW), r, jnp.int32), R)
            m = _box_tokens(rs, GW, False) * inv_c
            sl = pl.ds(pl.multiple_of(tc + r * GW, GW), GW)
            dd_ref[sl, :] = back(m - u_ref[sl, :], dy_ref[sl, :])
            return carry

        lax.fori_loop(0, R, row1, 0)

        def row2(r, carry):
            t1 = _pool_rows(dd_ref, r, R, tc, True)
            sl = pl.ds(pl.multiple_of(tc + r * GW, GW), GW)
            du_ref[sl, :] = _box_tokens(t1 * inv_c, GW, True) - dd_ref[sl, :]
            return carry

        lax.fori_loop(0, R, row2, 0)

    return pl.pallas_call(
        body, name="pool_bwd", out_shape=[_S((T, PW)), _S((PW, PW)), _S((1, PW))],
        scratch_shapes=[pltpu.VMEM((T, PW), F32)],
        compiler_params=pltpu.CompilerParams(vmem_limit_bytes=VMEM_MB << 20))(u, pwbd, ps, dy)


def _scan_consts():
    i = np.arange(CH)
    lower = (i[:, None] >= i[None, :]).astype(np.float32)
    return jnp.asarray(np.stack([lower, lower.T])), jnp.asarray(np.stack([lower.T, lower]))


def _gates(pab, al, dtb, csum):
    sp_in = pab + dtb
    sp = jnp.maximum(sp_in, 0.0) + jnp.log(1.0 + jnp.exp(-jnp.abs(sp_in)))
    nexp = -jnp.exp(al)
    gm = nexp * sp
    return gm, jax.nn.sigmoid(pab), _dot(csum, gm, NN, hi=True), sp_in, nexp


def _lane_col(m, j):
    return jnp.sum(jnp.where(_iota(m.shape, 1) == j, m, 0.0), axis=1, keepdims=True)


def _hs(h):
    return slice(h * DH, (h + 1) * DH)


HS = NH * CH
X3 = "x3"


def _stack(x, base=0):
    return jnp.concatenate([x[:, base + h * DH:base + (h + 1) * DH] for h in range(NH)], axis=0)


def _heads(st):
    return [st[h * CH:(h + 1) * CH] for h in range(NH)]


def _rowsum(a):
    return jnp.sum(a, axis=1, keepdims=True)


def _row_of(col):
    e0 = (_iota((8, LANES), 1) == 0).astype(F32)
    return _dot(e0, jnp.broadcast_to(col, (HS, LANES)), NT, hi=True)[0:1, :]


def _inverses(nms):
    eye = (_iota((HS, HS), 0) == _iota((HS, HS), 1)).astype(F32)
    x0s, mps = [eye + nm for nm in nms], list(nms)
    for _ in range(5):
        mps = [_dot(mp, mp, NN) for mp in mps]
        x0s = [x0 + _dot(x0, mp, NN) for x0, mp in zip(x0s, mps)]
    rs = [eye - _dot(eye - nm, x0, NN, hi=X3) for nm, x0 in zip(nms, x0s)]
    return [x0 + _dot(x0, r, NN) for x0, r in zip(x0s, rs)]


def _dn_chunk_pre(qkv, pab, al, dtb, csum_d, d):
    gm, bm, gcm, sp_in, nexp = _gates(pab, al, dtb, csum_d)
    gc = jnp.concatenate([_lane_col(gcm, d * NH + h) for h in range(NH)], axis=0)
    beta = jnp.concatenate([_lane_col(bm, 8 + d * NH + h) for h in range(NH)], axis=0)
    q, k, v = _stack(qkv, 0), _stack(qkv, DN), _stack(qkv, 2 * DN)
    ii, jj = _iota((HS, HS), 0), _iota((HS, HS), 1)
    sh = CH.bit_length() - 1
    same = (ii >> sh) == (jj >> sh)
    incl = same & ((ii >= jj) if d == 0 else (ii <= jj))
    strict = same & ((ii > jj) if d == 0 else (ii < jj))
    Di = jnp.where(incl, jnp.exp(jnp.where(incl, gc - _row_of(gc), 0.0)), 0.0)
    Ds = jnp.where(strict, Di, 0.0)
    kb = k * beta
    kk = _dot(kb, k, NT)
    return dict(q=q, k=k, v=v, beta=beta, gc=gc, gm=gm, bm=bm, sp_in=sp_in, nexp=nexp, Di=Di, Ds=Ds, strict=strict,
                last=CH - 1 if d == 0 else 0, kb=kb, kk=kk)


def _dn_chunk_post(c, tm, uw=None):
    q, k, v, beta, gc, kb, last = (c[n] for n in ("q", "k", "v", "beta", "gc", "kb", "last"))
    E = jnp.exp(gc)
    gls = [gc[h * CH + last:h * CH + last + 1, :] for h in range(NH)]
    xs = jnp.exp(jnp.concatenate([jnp.broadcast_to(g, (CH, 1)) for g in gls], axis=0) - gc)
    qk = _dot(q, k, NT)
    if uw is None:
        both = _dot(tm, jnp.concatenate([v * beta, kb * E], axis=1), NN, hi=X3)
        uw = both[:, :DH], both[:, DH:]
    u, w = uw
    return dict(c, tm=tm, E=E, gls=gls, xs=xs, qk=qk, u=u, w=w, ks=k * xs, qd=q * E, aqk=qk * c["Di"])


def _dn_chunks_bwd_math(cs, Ss, dS2s, dos, vns, dvns):
    I = range(len(cs))
    q, k, v, beta, tm, E, xs, kb, u, w = ([c[n] for c in cs] for n in ("q", "k", "v", "beta", "tm", "E", "xs", "kb", "u", "w"))
    cat = lambda parts: jnp.concatenate(parts, axis=0)

    def per_head(a, states):
        full = _dot(a, cat(states), NT)
        return cat([full[h * CH:(h + 1) * CH, _hs(h)] for h in range(NH)])

    dqd = [per_head(dos[i], Ss[i]) for i in I]
    dks = [per_head(vns[i], dS2s[i]) for i in I]
    dw = [-per_head(dvns[i], Ss[i]) for i in I]
    daqk = [_dot(dos[i], vns[i], NT) for i in I]
    drbw = [_dot(tm[i], jnp.concatenate([dvns[i], dw[i]], axis=1), TN, hi=X3) for i in I]
    drb, drw = [x[:, :DH] for x in drbw], [x[:, DH:] for x in drbw]
    uw = [jnp.concatenate([u[i], w[i].astype(F32)], axis=1) for i in I]
    dA = [jnp.where(cs[i]["strict"], -_dot(drbw[i], uw[i], NT), 0.0) for i in I]
    dM1 = [dA[i] * cs[i]["Ds"] for i in I]
    dM2 = [daqk[i] * cs[i]["Di"] for i in I]
    dM = [cat([dM1[i], dM2[i]]) for i in I]
    dMk = [_dot(dM[i], k[i], NN) for i in I]
    dkb = [dMk[i][:HS] + drw[i] * E[i] for i in I]
    dq = [dMk[i][HS:] + dqd[i] * E[i] for i in I]
    dk = [_dot(dM[i], cat([kb[i], q[i]]), TN) + dks[i] * xs[i] for i in I]
    on_diag = _iota((HS, HS), 0) == _iota((HS, HS), 1)
    out = []
    for i in I:
        G = dM1[i] * cs[i]["kk"] + dM2[i] * cs[i]["qk"]
        col = _rowsum(jnp.where(on_diag, jnp.broadcast_to(_colsum(G), (HS, HS)), 0.0))
        dxx = _rowsum(dks[i] * k[i]) * xs[i]
        dgc = _rowsum(G) - col + (_rowsum(dqd[i] * q[i]) + _rowsum(drw[i] * kb[i])) * E[i] - dxx
        at_last = _iota((CH, 1), 0) == cs[i]["last"]
        ends = []
        for h in range(NH):
            dgl = (_colsum(_rowsum(Ss[i][h] * dS2s[i][h])) * jnp.exp(cs[i]["gls"][h])
                   + _colsum(dxx[h * CH:(h + 1) * CH]))
            ends.append(jnp.where(at_last, dgl, 0.0))
        dbeta = _rowsum(drb[i] * v[i]) + _rowsum(dkb[i] * k[i])
        out.append((dq[i], dk[i] + dkb[i] * beta[i], drb[i] * beta[i], dgc + cat(ends), dbeta))
    return out


def _chunk_group(n, want=2):
    g = want
    while n % g:
        g //= 2
    return g


def _dn_chunks_fwd(qkv, pab, alr, dtr, rider=None):
    T = qkv.shape[0]
    n = T // CH
    G = _chunk_group(n, 4)
    csum, _ = _scan_consts()

    def body(q_ref, p_ref, cs_ref, al_ref, dt_ref, *outs):
        inst = [(g, d) for g in range(G) for d in range(2)]
        pres = [_dn_chunk_pre(q_ref[g * CH:(g + 1) * CH, :], p_ref[g * CH:(g + 1) * CH, :], al_ref[...], dt_ref[...],
                              cs_ref[d], d) for g, d in inst]
        tms = _inverses([-(p["kk"] * p["Ds"]) for p in pres])
        for (g, d), pre, tm in zip(inst, pres, tms):
            rows = slice(g * HS, (g + 1) * HS)
            u_ref, w_ref, ks_ref, qd_ref, aqk_ref, eg_ref, tm_ref = outs[7 * d:7 * d + 7]
            c = _dn_chunk_post(pre, tm)
            tm_ref[rows, :] = tm
            u_ref[rows, :] = c["u"]
            w_ref[rows, :] = c["w"].astype(BF16)
            ks_ref[rows, :] = c["ks"].astype(BF16)
            qd_ref[rows, :] = c["qd"].astype(BF16)
            aqk_ref[rows, :] = c["aqk"].astype(BF16)
            egs = [jnp.broadcast_to(jnp.exp(gl), (1, LANES)) for gl in c["gls"]]
            eg_ref[g * 8:(g + 1) * 8, :] = jnp.concatenate(egs + [jnp.zeros((8 - NH, LANES), F32)], axis=0)

    st = lambda w_: pl.BlockSpec((G * HS, w_), lambda i: (i, 0))
    one = [st(DH)] * 4 + [st(HS), pl.BlockSpec((G * 8, LANES), lambda i: (i, 0)), st(HS)]
    shp = [_S((n * HS, DH)), _S((n * HS, DH), BF16), _S((n * HS, DH), BF16), _S((n * HS, DH), BF16),
           _S((n * HS, HS), BF16), _S((n * 8, LANES)), _S((n * HS, HS))]
    res = _call(
        body, "dn_chunks_fwd", (n // G,),
        [_rows(G * CH, 3 * DN), _rows(G * CH, LANES), _full((2, CH, CH)), _full((1, LANES)), _full((1, LANES))],
        one * 2, shp * 2, rider=rider)(qkv, pab, csum, alr, dtr)
    outs, riding = (res, None) if rider is None else res
    parts = tuple(outs[:7]), tuple(outs[7:])
    return parts if rider is None else (parts, riding)


def _scan_plan(n, ncx):
    sg = next(g for g in (4, 2, 1) if n % g == 0 and ncx % g == 0)
    ng, ncg = n // sg, ncx // sg
    return sg, ((lambda i: i), (lambda i: jnp.where(i < ncg, ncg - 1 - i, ng - 1 - (i - ncg))))


def _scan_specs(order, sg):
    st = lambda w_: pl.BlockSpec((sg * HS, w_), lambda i: (order(i), 0))
    return dict(st=st(DH), aqk=st(HS), eg=pl.BlockSpec((sg * 8, LANES), lambda i: (order(i), 0)),
                tok=pl.BlockSpec((sg * CH, DN), lambda i: (order(i), 0)),
                state=pl.BlockSpec((sg, DN, DH), lambda i: (order(i), 0, 0)))


def _scan_fwd(parts, T, tc, rider=None):
    n = T // CH
    sg, orders = _scan_plan(n, tc // CH)

    def body(*refs):
        S_f, S_b = refs[-2:]

        @pl.when(pl.program_id(0) == 0)
        def _():
            S_f[...] = jnp.zeros_like(S_f)
            S_b[...] = jnp.zeros_like(S_b)

        for g in range(sg):
            for d, S in enumerate((S_f, S_b)):
                u_ref, w_ref, ks_ref, qd_ref, aqk_ref, eg_ref = refs[6 * d:6 * d + 6]
                o_ref, ss_ref, vn_ref = refs[12 + 3 * d:15 + 3 * d]
                k = g if d == 0 else sg - 1 - g
                rows = slice(k * HS, (k + 1) * HS)
                ss_ref[k] = S[...]
                Sh = [S[_hs(h), :] for h in range(NH)]
                wh, ksh, qdh = _heads(w_ref[rows, :]), _heads(ks_ref[rows, :]), _heads(qd_ref[rows, :])
                vn = u_ref[rows, :] - jnp.concatenate([_dot(wh[h], Sh[h], NN) for h in range(NH)], axis=0)
                vn_ref[rows, :] = vn
                av, vnh = _heads(_dot(aqk_ref[rows, :], vn, NN)), _heads(vn)
                for h in range(NH):
                    o_ref[k * CH:(k + 1) * CH, _hs(h)] = _dot(qdh[h], Sh[h], NN) + av[h]
                    S[_hs(h), :] = Sh[h] * eg_ref[k * 8 + h:k * 8 + h + 1, :] + _dot(ksh[h], vnh[h], TN)

    ins, outs, shp = [], [], []
    for d in range(2):
        sp = _scan_specs(orders[d], sg)
        ins += [sp["st"]] * 4 + [sp["aqk"], sp["eg"]]
        outs += [sp["tok"], sp["state"], sp["st"]]
        shp += [_S((T, DN)), _S((n, DN, DH)), _S((n * HS, DH))]
    res = _call(body, "scan_fwd", (n // sg,), ins, outs, shp,
                scratch=[pltpu.VMEM((DN, DH), F32), pltpu.VMEM((DN, DH), F32)], rider=rider)(*parts[0][:6], *parts[1][:6])
    res, riding = (res, None) if rider is None else res
    out = tuple(res[:3]), tuple(res[3:])
    return out if rider is None else (out, riding)


def _scan_bwd(do, parts, tc):
    T = do.shape[0]
    n = T // CH
    sg, fwd_orders = _scan_plan(n, tc // CH)
    orders = [lambda s, f=f: f(n // sg - 1 - s) for f in fwd_orders]

    def body(*refs):
        dS_f, dS_b = refs[-2:]

        @pl.when(pl.program_id(0) == 0)
        def _():
            dS_f[...] = jnp.zeros_like(dS_f)
            dS_b[...] = jnp.zeros_like(dS_b)

        for g in range(sg):
            for d, dS in enumerate((dS_f, dS_b)):
                do_ref, w_ref, ks_ref, qd_ref, aqk_ref, eg_ref = refs[6 * d:6 * d + 6]
                dvn_ref, dss_ref = refs[12 + 2 * d:14 + 2 * d]
                k = sg - 1 - g if d == 0 else g
                rows = slice(k * HS, (k + 1) * HS)
                dss_ref[k] = dS[...]
                dSh = [dS[_hs(h), :] for h in range(NH)]
                wh, ksh, qdh = _heads(w_ref[rows, :]), _heads(ks_ref[rows, :]), _heads(qd_ref[rows, :])
                do_st = _stack(do_ref[k * CH:(k + 1) * CH, :])
                dvn = (_dot(aqk_ref[rows, :], do_st, TN)
                       + jnp.concatenate([_dot(ksh[h], dSh[h], NN) for h in range(NH)], axis=0))
                dvn_ref[rows, :] = dvn
                doh, dvnh = _heads(do_st), _heads(dvn)
                for h in range(NH):
                    dS[_hs(h), :] = (_dot(qdh[h], doh[h], TN) + dSh[h] * eg_ref[k * 8 + h:k * 8 + h + 1, :]
                                     - _dot(wh[h], dvnh[h], TN))

    ins, outs, shp, args = [], [], [], []
    for d in range(2):
        sp = _scan_specs(orders[d], sg)
        ins += [sp["tok"]] + [sp["st"]] * 3 + [sp["aqk"], sp["eg"]]
        outs += [sp["st"], sp["state"]]
        shp += [_S((n * HS, DH)), _S((n, DN, DH))]
        args += [do, *parts[d][1:6]]
    res = _call(body, "scan_bwd", (n // sg,), ins, outs, shp,
                scratch=[pltpu.VMEM((DN, DH), F32), pltpu.VMEM((DN, DH), F32)])(*args)
    return tuple(res[:2]), tuple(res[2:])


def _dn_chunks_bwd(qkv, pab, alr, dtr, do, fwd, bwd, rider=None):
    T = qkv.shape[0]
    n = T // CH
    G = _chunk_group(n)
    csum, csum_t = _scan_consts()

    def body(q_ref, p_ref, do_ref, cs_ref, cst_ref, al_ref, dt_ref, *refs):
        dq_refs, dp_refs, acc_ref = refs[14:16], refs[16:18], refs[18]

        @pl.when(pl.program_id(0) == 0)
        def _():
            acc_ref[...] = jnp.zeros_like(acc_ref)

        lane = _iota((CH, LANES), 1)
        inst = [(g, d) for g in range(G) for d in range(2)]
        cs, Ss, dS2s, dos, vns, dvns = [], [], [], [], [], []
        for g, d in inst:
            tok, rows = slice(g * CH, (g + 1) * CH), slice(g * HS, (g + 1) * HS)
            vn_ref, dvn_ref, ss_ref, dss_ref, tm_ref, u_ref, w_ref = refs[7 * d:7 * d + 7]
            cs.append(_dn_chunk_post(
                _dn_chunk_pre(q_ref[tok, :], p_ref[tok, :], al_ref[...], dt_ref[...], cs_ref[d], d), tm_ref[rows, :],
                uw=(u_ref[rows, :], w_ref[rows, :])))
            Ss.append([ss_ref[g, _hs(h), :] for h in range(NH)])
            dS2s.append([dss_ref[g, _hs(h), :] for h in range(NH)])
            dos.append(_stack(do_ref[tok, :]))
            vns.append(vn_ref[rows, :])
            dvns.append(dvn_ref[rows, :])
        for (g, d), c, (dq, dk, dv, dgc, dbeta) in zip(inst, cs, _dn_chunks_bwd_math(cs, Ss, dS2s, dos, vns, dvns)):
            tok = slice(g * CH, (g + 1) * CH)
            dgcm = jnp.zeros((CH, LANES), F32)
            dbm = jnp.zeros((CH, LANES), F32)
            for h, (a, b_, c_, e, f) in enumerate(zip(*map(_heads, (dq, dk, dv, dgc, dbeta)))):
                dq_refs[d][tok, _hs(h)] = a
                dq_refs[d][tok, _hs(NH + h)] = b_
                dq_refs[d][tok, _hs(2 * NH + h)] = c_
                dgcm = jnp.where(lane == d * NH + h, e, dgcm)
                dbm = jnp.where(lane == 8 + d * NH + h, f, dbm)
            dgm = _dot(cst_ref[d], dgcm, NN, hi=True)
            dsp = dgm * c["nexp"] * jax.nn.sigmoid(c["sp_in"])
            dp_refs[d][tok, :] = dsp + dbm * c["bm"] * (1.0 - c["bm"])
            acc_ref[0:1, :] += _colsum(dgm * c["gm"])
            acc_ref[1:2, :] += _colsum(dsp)

    st = pl.BlockSpec((G * HS, DH), lambda i: (i, 0))
    state = pl.BlockSpec((G, DN, DH), lambda i: (i, 0, 0))
    return _call(
        body, "dn_chunks_bwd", (n // G,),
        [_rows(G * CH, 3 * DN), _rows(G * CH, LANES), _rows(G * CH, DN), _full((2, CH, CH)), _full((2, CH, CH)),
         _full((1, LANES)), _full((1, LANES))]
        + [st, st, state, state, pl.BlockSpec((G * HS, HS), lambda i: (i, 0)), st, st] * 2,
        [_rows(G * CH, 3 * DN)] * 2 + [_rows(G * CH, LANES)] * 2 + [_full((8, LANES))],
        [_S((T, 3 * DN))] * 2 + [_S((T, LANES))] * 2 + [_S((8, LANES))], rider=rider)(
            qkv, pab, do, csum, csum_t, alr, dtr, *fwd, *bwd)


def _head_out(o, z, g):
    on = o * lax.rsqrt(jnp.mean(o * o, axis=-1, keepdims=True) + EPS) * g
    return on * _silu(z)


def _mix_branches(of_ref, ob_ref, z_ref, yp_ref, ys_ref, pg_ref, gdn_ref, wa_ref, wb_ref, wc_ref):
    ons, ya = [], None
    for h in range(NH):
        on = _head_out(of_ref[:, _hs(h)] + ob_ref[:, _hs(h)], z_ref[:, _hs(h)], gdn_ref[...])
        t = _dot(on, wa_ref[_hs(h), :], NN)
        ya = t if ya is None else ya + t
        ons.append(on)
    ys = [ya, _dot(yp_ref[...], wb_ref[...], NN), _dot(ys_ref[...], wc_ref[...], NN)]
    sg = [jax.nn.sigmoid(pg_ref[:, k * D:(k + 1) * D]) for k in range(3)]
    return ons, ys, sg


def _mix_fwd(X, of, ob, z, yp, ys, pg, mv, gdn, wa, wb, wc, wo, tc, tt):
    T = X.shape[0]

    def body(x_ref, of_ref, ob_ref, z_ref, yp_ref, ys_ref, pg_ref, mv_ref, gdn_ref, wa_ref, wb_ref, wc_ref, wo_ref,
             x1_ref):
        _, yb, sg = _mix_branches(of_ref, ob_ref, z_ref, yp_ref, ys_ref, pg_ref, gdn_ref, wa_ref, wb_ref, wc_ref)
        mix = _dot(sg[0] * yb[0] + sg[1] * yb[1] + sg[2] * yb[2], wo_ref[...], NN)
        _, gate = _stream_rows(mv_ref, pl.program_id(0), tt, tc, 2)
        x1_ref[...] = x_ref[...] + gate * mix

    return _call(
        body, "mix_fwd", (T // tt,),
        [_rows(tt, D), _rows(tt, DN), _rows(tt, DN), _rows(tt, DN), _rows(tt, PW), _rows(tt, PW), _rows(tt, 3 * D),
         _full((8, D)), _full((1, DH)), _full(wa.shape), _full(wb.shape), _full(wc.shape), _full(wo.shape)],
        _rows(tt, D), _S((T, D)))(X, of, ob, z, yp, ys, pg, mv, gdn, wa, wb, wc, wo)


def _mix_bwd(dx1, of, ob, z, yp, ys, pg, mv, gdn, wa, wb, wc, wo, tc, tt, rider=None):
    T = dx1.shape[0]

    def body(dx_ref, of_ref, ob_ref, z_ref, yp_ref, ys_ref, pg_ref, mv_ref, gdn_ref, wa_ref, wb_ref, wc_ref, wo_ref,
             do_ref, dz_ref, dyp_ref, dys_ref, dpg_ref, dwa_ref, dwb_ref, dwc_ref, dwo_ref, dgdn_ref, dm_ref):
        i = pl.program_id(0)

        @pl.when(i == 0)
        def _():
            for r in (dwa_ref, dwb_ref, dwc_ref, dwo_ref, dgdn_ref, dm_ref):
                r[...] = jnp.zeros_like(r)

        ons, yb, sg = _mix_branches(of_ref, ob_ref, z_ref, yp_ref, ys_ref, pg_ref, gdn_ref, wa_ref, wb_ref, wc_ref)
        ymix = sg[0] * yb[0] + sg[1] * yb[1] + sg[2] * yb[2]
        isc, gate = _stream_rows(mv_ref, i, tt, tc, 2)
        dx = dx_ref[...]
        dmix = dx * gate
        _acc_stream(dm_ref, 2, isc, dx * _dot(ymix, wo_ref[...], NN))
        dwo_ref[...] += _dot(ymix, dmix, TN)
        dymix = _dot(dmix, wo_ref[...], NT)
        dyb = []
        for k in range(3):
            dyb.append(dymix * sg[k])
            dpg_ref[:, k * D:(k + 1) * D] = (dymix * yb[k] * sg[k] * (1.0 - sg[k])).astype(BF16)
        dwb_ref[...] += _dot(yp_ref[...], dyb[1], TN)
        dwc_ref[...] += _dot(ys_ref[...], dyb[2], TN)
        dyp_ref[...] = _dot(dyb[1], wb_ref[...], NT)
        dys_ref[...] = _dot(dyb[2], wc_ref[...], NT)
        dg = jnp.zeros((1, DH), F32)
        for h in range(NH):
            dwa_ref[_hs(h), :] += _dot(ons[h], dyb[0], TN)
            don = _dot(dyb[0], wa_ref[_hs(h), :], NT)
            _, vjp = jax.vjp(_head_out, of_ref[:, _hs(h)] + ob_ref[:, _hs(h)], z_ref[:, _hs(h)], gdn_ref[...])
            do_h, dz_h, dg_h = vjp(don)
            do_ref[:, _hs(h)] = do_h
            dz_ref[:, _hs(h)] = dz_h.astype(BF16)
            dg = dg + dg_h
        dgdn_ref[...] += dg

    return _call(
        body, "mix_bwd", (T // tt,),
        [_rows(tt, D), _rows(tt, DN), _rows(tt, DN), _rows(tt, DN), _rows(tt, PW), _rows(tt, PW), _rows(tt, 3 * D),
         _full((8, D)), _full((1, DH)), _full(wa.shape), _full(wb.shape), _full(wc.shape), _full(wo.shape)],
        [_rows(tt, DN), _rows(tt, DN), _rows(tt, PW), _rows(tt, PW), _rows(tt, 3 * D),
         _full(wa.shape), _full(wb.shape), _full(wc.shape), _full(wo.shape), _full((1, DH)), _full((8, D))],
        [_S((T, DN)), _S((T, DN), BF16), _S((T, PW)), _S((T, PW)), _S((T, 3 * D), BF16),
         _S(wa.shape), _S(wb.shape), _S(wc.shape), _S(wo.shape), _S((1, DH)), _S((8, D))], rider=rider)(
            dx1, of, ob, z, yp, ys, pg, mv, gdn, wa, wb, wc, wo)


def _ffn_fwd(X1, mv, g, wgu, wd, tc, tt):
    T = X1.shape[0]

    def body(x_ref, mv_ref, g_ref, wgu_ref, wd_ref, x2_ref, ff_ref):
        i = pl.program_id(0)
        _, sh = _stream_rows(mv_ref, i, tt, tc, 0)
        _, sc = _stream_rows(mv_ref, i, tt, tc, 1)
        _, gate = _stream_rows(mv_ref, i, tt, tc, 2)
        x = x_ref[...]
        gu = _dot(_modulate(x, g_ref[...], sh, sc), wgu_ref[...], NN)
        ff = _dot(_silu(gu[:, :DFF]) * gu[:, DFF:], wd_ref[...], NN)
        ff_ref[...] = ff
        x2_ref[...] = x + gate * ff

    return _call(
        body, "ffn_fwd", (T // tt,),
        [_rows(tt, D), _full((8, D)), _full((1, D)), _full(wgu.shape), _full(wd.shape)],
        [_rows(tt, D)] * 2, [_S((T, D))] * 2)(X1, mv, g, wgu, wd)


def _ffn_bwd(X1, ff, dx2, mv, g, wgu, wd, tc, tt, rider=None):
    T = X1.shape[0]

    def body(x_ref, ff_ref, dx2_ref, mv_ref, g_ref, wgu_ref, wd_ref, dx1_ref, ht_ref, dgu_ref, actt_ref, dff_ref, dg_ref,
             dm_ref):
        i = pl.program_id(0)
        isc, sh = _stream_rows(mv_ref, i, tt, tc, 0)
        _, sc = _stream_rows(mv_ref, i, tt, tc, 1)
        _, gate = _stream_rows(mv_ref, i, tt, tc, 2)
        x, dx2_ = x_ref[...], dx2_ref[...]
        h, vjp = jax.vjp(_modulate, x, g_ref[...], sh, sc)
        ht_ref[...] = h.T.astype(BF16)
        gu = jnp.dot(h.astype(BF16), wgu_ref[...], preferred_element_type=F32)
        ga, up = gu[:, :DFF], gu[:, DFF:]
        sg = jax.nn.sigmoid(ga)
        actt_ref[...] = (ga * sg * up).T.astype(BF16)
        dff = dx2_ * gate
        dff_ref[...] = dff.astype(BF16)
        dact = _dot(dff, wd_ref[...], NT)
        dga = (dact * up * (sg * (1.0 + ga * (1.0 - sg)))).astype(BF16)
        dup = (dact * ga * sg).astype(BF16)
        dgu_ref[:, :DFF] = dga
        dgu_ref[:, DFF:] = dup
        dh = _dot(dga, wgu_ref[:, :DFF], NT) + _dot(dup, wgu_ref[:, DFF:], NT)
        dx, dg, dsh, dsc = vjp(dh)
        dx1_ref[...] = dx2_ + dx

        @pl.when(i == 0)
        def _():
            dg_ref[...] = jnp.zeros_like(dg_ref)
            dm_ref[...] = jnp.zeros_like(dm_ref)

        dg_ref[...] += dg
        _acc_stream(dm_ref, 0, isc, dsh)
        _acc_stream(dm_ref, 1, isc, dsc)
        _acc_stream(dm_ref, 2, isc, dx2_ * ff_ref[...])

    return _call(
        body, "ffn_bwd", (T // tt,),
        [_rows(tt, D), _rows(tt, D), _rows(tt, D), _full((8, D)), _full((1, D)), _full(wgu.shape), _full(wd.shape)],
        [_rows(tt, D), _cols(D, tt), _rows(tt, 2 * DFF), _cols(DFF, tt), _rows(tt, D), _full((1, D)), _full((8, D))],
        [_S((T, D)), _S((D, T), BF16), _S((T, 2 * DFF), BF16), _S((DFF, T), BF16), _S((T, D), BF16),
         _S((1, D)), _S((8, D))], rider=rider)(X1, ff, dx2, mv, g, wgu, wd)


def _rms(x, g):
    return x * lax.rsqrt(jnp.mean(x * x, axis=-1, keepdims=True) + EPS) * g


def _loss_head(X2, tgt, gf, tc):
    T = X2.shape[0]

    def body(x_ref, t_ref, g_ref, dx_ref, loss_ref, dg_ref):
        i = pl.program_id(0)

        @pl.when(i == 0)
        def _():
            dx_ref[...] = jnp.zeros_like(dx_ref)
            loss_ref[...] = jnp.zeros_like(loss_ref)
            dg_ref[...] = jnp.zeros_like(dg_ref)

        @pl.when(i > 0)
        def _():
            y, vjp = jax.vjp(_rms, x_ref[...], g_ref[...])
            err = y - t_ref[...]
            dx, dg = vjp(err * (1.0 / D))
            dx_ref[...] = dx
            dg_ref[...] += dg
            loss_ref[...] += (0.5 / D) * jnp.sum(jnp.sum(err * err, axis=1, keepdims=True), axis=0, keepdims=True)

    return _call(
        body, "loss_head", (T // tc,),
        [_rows(tc, D), pl.BlockSpec((tc, D), lambda i: (jnp.maximum(i - 1, 0), 0)), _full((1, D))],
        [_rows(tc, D), _full((8, LANES)), _full((1, D))],
        [_S((T, D)), _S((8, LANES)), _S((1, D))])(X2, tgt, gf)


def _block_diag(pw):
    g, n = pw.shape[0], pw.shape[1]
    out = jnp.zeros((g * n, g * n), pw.dtype)
    for k in range(g):
        out = lax.dynamic_update_slice(out, pw[k], (k * n, k * n))
    return out


IN_TRUE = tuple(IN_BOUNDS[k + 1] - IN_BOUNDS[k] for k in range(8))


def _overlaps(widths, cw):
    starts = np.cumsum([0] + list(widths))
    out = []
    for k in range(N_DEV):
        for i in range(len(widths)):
            a, b = max(k * cw, starts[i]), min((k + 1) * cw, starts[i + 1])
            if a < b:
                out.append((k, i, int(a - k * cw), int(a - starts[i]), int(b - a)))
    return out


def _shards_to_cols(gathered, l, widths, padded):
    nd, _, R, cw = gathered.shape
    tr = _shard_rows(R)

    def body(s_ref, *o_refs):
        for i, o_ref in enumerate(o_refs):
            if padded[i] > widths[i]:
                o_ref[...] = jnp.zeros_like(o_ref)
        for k, i, so, go, n in _overlaps(widths, cw):
            o_refs[i][:, go:go + n] = s_ref[k, 0, :, so:so + n].astype(BF16)

    return _call(
        body, "shards_to_cols", (R // tr,), [pl.BlockSpec((nd, 1, tr, cw), lambda i: (0, l, i, 0))],
        [_rows(tr, p) for p in padded], [_S((R, p), BF16) for p in padded])(gathered)


def _cols_to_shards(groups, widths, cw):
    R = groups[0].shape[0]
    tr = _shard_rows(R)

    def body(*refs):
        o_ref = refs[-1]
        for k, i, so, go, n in _overlaps(widths, cw):
            o_ref[k % 2, k // 2, 0, :, so:so + n] = refs[i][:, go:go + n]

    return _call(
        body, "cols_to_shards", (R // tr,), [_rows(tr, g.shape[1]) for g in groups],
        pl.BlockSpec((2, N_CHIP, 1, tr, cw), lambda i: (0, 0, 0, i, 0)), _S((2, N_CHIP, 1, R, cw)))(*groups)


def _mod_rows(mods_l, k0):
    rows = [mods_l[s, (k0 + k) * D:(k0 + k + 1) * D] for s in (0, 1) for k in range(3)]
    return jnp.stack(rows + [jnp.zeros((D,), F32)] * 2)


def _lane_row(v8):
    return jnp.pad(v8.reshape(1, 8), ((0, 0), (0, LANES - 8)))


LAYERED = ("w_in", "w_br_a", "w_br_b", "w_br_c", "w_o", "w_gu", "w_down")
LATE = ("w_br_a", "w_br_b", "w_br_c", "w_o", "w_gu", "w_down")
FFN_W = ("w_gu", "w_down")


def _device_step(x, c, ctx, tgt, wts, tt, comm=None):
    tc = ctx.shape[0]
    X = jnp.concatenate([ctx, x], axis=0)
    if comm is None:
        row = 0
        cc = jnp.concatenate([c, jnp.zeros((CTX_ROW - 1, D), F32), wts["c_ctx"][None, :],
                              jnp.zeros((CC_ROWS - CTX_ROW - 1, D), F32)], axis=0)
        w_ada = wts["w_ada"].astype(BF16)
        mods16 = _mod_fwd(cc, w_ada, wts["b_ada"].reshape(NL, 1, 6 * D))
    else:
        row, mods16 = comm.adaln_fwd(c, wts["c_ctx"])
    mods = jnp.stack([mods16[:, CTX_ROW], lax.dynamic_index_in_dim(mods16, row, 1, keepdims=False)], axis=1)

    saved = []
    for l in range(NL):
        ws = _shards_to_cols(wts["w_in"][l], 0, IN_TRUE, IN_WIDTHS)
        mv1, mv2 = _mod_rows(mods[l], 0), _mod_rows(mods[l], 3)
        g1, g2 = wts["norm1_g"][l][None, :], wts["norm2_g"][l][None, :]
        cw, scw = wts["dn_conv_w"][l], wts["sc_conv_w"][l]
        alr, dtr = _lane_row(wts["dn_a_log"][l]), _lane_row(wts["dn_dt_bias"][l])
        gdn = wts["dn_norm_g"][l][None, :]
        pwbd, ps = _block_diag(wts["pool_w"][l]), wts["pool_scale"][l][None, :]
        hb, pq, pz, pab, pp, sx, sb, sc_, pg = _inproj_fwd(X, mv1, g1, ws, tc, tt)
        qkv = _dnprep_fwd(pq, cw, tc, tt)
        if comm is not None and l == 0:
            parts, riding = _dn_chunks_fwd(qkv, pab, alr, dtr, rider=comm.late_weights_chips())
            ((of, ssf, vnf), (ob, ssb, vnb)), riding = _scan_fwd(parts, X.shape[0], tc,
                                                                 rider=comm.late_weights_pair(riding))
            late, w_in_1 = comm.late_weights(riding)
            wts = dict(wts, **late, w_in=[wts["w_in"][0], w_in_1])
        else:
            parts = _dn_chunks_fwd(qkv, pab, alr, dtr)
            (of, ssf, vnf), (ob, ssb, vnb) = _scan_fwd(parts, X.shape[0], tc)
        wbr = [_shards_to_cols(wts[k], l, (2 * DFF,), (2 * DFF,))[0] if k == "w_gu" else wts[k][l].astype(BF16)
               for k in LATE]
        yp = _pool_fwd(pp, pwbd, ps, tc)
        ys = _sc_fwd(sx, sb, sc_, scw, tc, tt)
        X1 = _mix_fwd(X, of, ob, pz, yp, ys, pg, mv1, gdn, *wbr[:4], tc, tt)
        X2, ff = _ffn_fwd(X1, mv2, g2, wbr[4], wbr[5], tc, tt)
        saved.append(dict(X=X, X1=X1, ff=ff, ws=ws, wbr=wbr, mv1=mv1, mv2=mv2, g1=g1, g2=g2, cw=cw, scw=scw, alr=alr, dtr=dtr,
                          gdn=gdn, pwbd=pwbd, ps=ps, hb=hb, pq=pq, pz=pz, pab=pab, pp=pp, sx=sx, sb=sb, sc=sc_, pg=pg,
                          qkv=qkv, of=of, ob=ob, ssf=ssf, ssb=ssb, vnf=vnf, vnb=vnb, parts=parts, yp=yp, ys=ys))
        X = X2

    dX, loss, dgf = _loss_head(X, tgt, wts["final_norm_g"][None, :], tc)

    gl = {k: [None] * NL for k in ("w_in", "norm1_g", "norm2_g", "dn_conv_w", "dn_a_log", "dn_dt_bias", "dn_norm_g",
                                   "pool_w", "pool_scale", "sc_conv_w", "w_br_a", "w_br_b", "w_br_c", "w_o", "w_gu",
                                   "w_down")}
    dmods = [None] * NL
    early = None
    for l in reversed(range(NL)):
        s = saved[l]
        hide = comm is not None and l == 0
        res = _ffn_bwd(s["X1"], s["ff"], dX, s["mv2"], s["g2"], s["wbr"][4], s["wbr"][5], tc, tt,
                       rider=comm.grad_pair_rider([gl[k][1] for k in LAYERED]) if hide else None)
        if hide:
            res, got = res
        dx1, h2, dgu, act, dff, dg2, dm2 = res
        gl["w_gu"][l] = _cols_to_shards([_dw(h2, dgu, tt)], (2 * DFF,), 2 * DFF // N_DEV)
        gl["w_down"][l] = _dw(act, dff, tt)
        res = _mix_bwd(dx1, s["of"], s["ob"], s["pz"], s["yp"], s["ys"], s["pg"], s["mv1"], s["gdn"], *s["wbr"][:4], tc,
                       tt, rider=comm.grad_pair_rider([gl[k][0] for k in FFN_W], FFN_W) if hide else None)
        if hide:
            res, got_ffn = res
            chip_rider = comm.grad_chip_rider(got + got_ffn)
        do, dz, dyp, dys, dpg, dwa, dwb, dwc, dwo, dgdn, dmg = res
        dpp, dpw, dps = _pool_bwd(s["pp"], s["pwbd"], s["ps"], dyp, tc)
        dsx, dsb, dsc, dscw = _sc_bwd(s["sx"], s["sb"], s["sc"], s["scw"], dys, tc, tt)
        (dvnf, dssf), (dvnb, dssb) = _scan_bwd(do, s["parts"], tc)
        res = _dn_chunks_bwd(s["qkv"], s["pab"], s["alr"], s["dtr"], do,
                             (s["vnf"], dvnf, s["ssf"], dssf, s["parts"][0][6], *s["parts"][0][:2]),
                             (s["vnb"], dvnb, s["ssb"], dssb, s["parts"][1][6], *s["parts"][1][:2]),
                             rider=chip_rider if hide else None)
        if hide:
            res, early = res
            early = comm.grad_chip_done(early)
        dqf, dqb, dpf, dpb, gacc = res
        dy = _dnprep_bwd_act(s["pq"], s["cw"], dqf, dqb, tc, tt)
        dpq, dcw = _conv_bwd(dy, s["pq"], s["cw"], tc, tt)
        dps_ = [dpq, dz, dpf, dpb, dpp, dsx, dsb, dsc, dpg]
        dp_w = [0, 1, 2, 2, 3, 4, 5, 6, 7]
        dX, dg1, dm1 = _inproj_bwd(s["X"], s["mv1"], s["g1"], s["ws"], dps_, dp_w, dx1, tc, tt)
        dws = [_dw(s["hb"], dp, tt) for dp in (dpq, dz, dpf + dpb, dpp, dsx, dsb, dsc, dpg)]
        gl["w_in"][l] = _cols_to_shards(dws, IN_TRUE, IN_BOUNDS[-1] // N_DEV)
        gl["norm1_g"][l], gl["norm2_g"][l] = dg1[0], dg2[0]
        gl["dn_conv_w"][l], gl["sc_conv_w"][l] = dcw, dscw
        gl["dn_a_log"][l], gl["dn_dt_bias"][l] = gacc[0, :8].reshape(2, NH), gacc[1, :8].reshape(2, NH)
        gl["dn_norm_g"][l] = dgdn[0]
        gl["pool_w"][l] = jnp.stack([dpw[k * GW:(k + 1) * GW, k * GW:(k + 1) * GW] for k in range(4)])
        gl["pool_scale"][l] = dps[0]
        gl["w_br_a"][l], gl["w_br_b"][l], gl["w_br_c"][l], gl["w_o"][l] = dwa, dwb, dwc, dwo
        dm = dm1 + dmg
        cat = lambda r: jnp.concatenate([dm[r], dm[r + 1], dm[r + 2], dm2[r], dm2[r + 1], dm2[r + 2]])
        dmods[l] = jnp.stack([cat(0), cat(3)])

    dmods = jnp.stack(dmods)
    grads = {k: (v if k in LAYERED else jnp.stack(v)) for k, v in gl.items()}
    if comm is None:
        dm16 = jnp.zeros((NL, CC_ROWS, 6 * D), F32).at[:, CTX_ROW].set(dmods[:, 0]).at[:, row].set(dmods[:, 1])
        dwada, dcc = _mod_bwd(cc, w_ada, dm16)
        grads.update(w_ada=dwada, b_ada=dmods[:, 0] + dmods[:, 1], c_ctx=dcc[CTX_ROW])
    else:
        grads.update(comm.adaln_bwd(dmods))
    grads.update(final_norm_g=dgf[0])
    return loss, dX[tc:], grads, early


def _me():
    return lax.axis_index("x"), lax.axis_index("y"), lax.axis_index("c")


def _dev_index(p):
    return 4 * p[0] + 2 * p[1] + p[2]


def _allgather(parts):
    n = len(parts)

    def body(*refs):
        ins, outs = refs[:n], refs[n:2 * n]
        send_sems, recv_sems = refs[2 * n:]
        x, y, c = _me()
        me, sibling = (x, y, c), (x, y, 1 - c)
        chips = [(1 - x, y), (x, 1 - y), (1 - x, 1 - y)]

        def copy(a, k, block, to, src=None):
            dst = outs[a].at[_dev_index(block)]
            return pltpu.make_async_remote_copy(
                src_ref=dst if src is None else src, dst_ref=dst, send_sem=send_sems.at[a, k], recv_sem=recv_sems.at[a, k],
                device_id=to, device_id_type=MESH_ID)

        first, passed = [], []
        for a in range(n):
            first.append(copy(a, 0, me, sibling, src=ins[a]))
            first += [copy(a, 1 + j, me, (*chip, c), src=ins[a]) for j, chip in enumerate(chips)]
        for cp in first:
            cp.start()
        for a in range(n):
            for j, chip in enumerate(chips):
                copy(a, 1 + j, (*chip, c), me).wait_recv()
                passed.append(copy(a, 4 + j, (*chip, c), sibling))
                passed[-1].start()
        for a in range(n):
            copy(a, 0, sibling, me).wait_recv()
            for j, chip in enumerate(chips):
                copy(a, 4 + j, (*chip, 1 - c), me).wait_recv()
        for cp in first + passed:
            cp.wait_send()

    outs = pl.pallas_call(
        body, name="allgather", in_specs=[HBM_SPEC] * n, out_specs=[HBM_SPEC] * n,
        out_shape=[_S((N_DEV,) + p.shape, p.dtype) for p in parts],
        scratch_shapes=[pltpu.SemaphoreType.DMA((n, 7)), pltpu.SemaphoreType.DMA((n, 7))],
    )(*parts)
    return [_with_own(o, p, _dev_index(_me())) for o, p in zip(outs, parts)]


def _with_own(gathered, own, index):
    return lax.dynamic_update_index_in_dim(gathered, own, index, 0)


def _broadcast_small(small, name="small_exchange"):
    def body(in_ref, out_ref, send_sems, recv_sems, local_sem):
        x, y, c = _me()
        my = _dev_index((x, y, c))
        mine = pltpu.make_async_copy(in_ref, out_ref.at[my], local_sem)
        mine.start()
        remote = []
        for k in range(1, N_DEV):
            cp = pltpu.make_async_remote_copy(
                src_ref=in_ref, dst_ref=out_ref.at[my], send_sem=send_sems.at[k - 1], recv_sem=recv_sems.at[k - 1],
                device_id=(x ^ (k >> 2), y ^ ((k >> 1) & 1), c ^ (k & 1)), device_id_type=MESH_ID)
            cp.start()
            remote.append(cp)
        for cp in remote:
            cp.wait_recv()
        for cp in remote:
            cp.wait_send()
        mine.wait()

    return pl.pallas_call(
        body, name=name, in_specs=[HBM_SPEC], out_specs=HBM_SPEC,
        out_shape=_S((N_DEV,) + small.shape, small.dtype),
        scratch_shapes=[pltpu.SemaphoreType.DMA((7,)), pltpu.SemaphoreType.DMA((7,)), pltpu.SemaphoreType.DMA],
    )(small)


def _run_rider(rider, name):
    ni, no = len(rider.ins), len(rider.out_shapes)

    def body(*refs):
        riding = (refs[:ni], refs[ni:ni + no], refs[ni + no:])
        rider.start(*riding)
        rider.wait(*riding)

    return list(pl.pallas_call(
        body, name=name, in_specs=[HBM_SPEC] * ni, out_specs=[HBM_SPEC] * no, out_shape=rider.out_shapes,
        scratch_shapes=rider.sems)(*rider.ins))


def _chip_peers(x, y):
    return [(k - 1, (x ^ (k >> 1), y ^ (k & 1))) for k in range(1, N_CHIP)]


def _pair_exchange(g2s):
    n = len(g2s)

    def copies(ins, outs, sems):
        x, y, c = _me()
        return [pltpu.make_async_remote_copy(
            src_ref=ins[a].at[1 - c, j], dst_ref=outs[a].at[j], send_sem=sems[0].at[a, j], recv_sem=sems[1].at[a, j],
            device_id=(x, y, 1 - c), device_id_type=MESH_ID) for a in range(n) for j in range(N_CHIP)], []

    return _Rider(g2s, [_S(g.shape[1:], g.dtype) for g in g2s],
                  [pltpu.SemaphoreType.DMA((n, N_CHIP)), pltpu.SemaphoreType.DMA((n, N_CHIP))], copies)


def _my_chip():
    x, y, _ = _me()
    return 2 * x + y


def _chip_exchange(s4s):
    n = len(s4s)

    def copies(ins, outs, sems):
        x, y, c = _me()
        my = 2 * x + y
        return [pltpu.make_async_remote_copy(
            src_ref=ins[a].at[2 * px + py], dst_ref=outs[a].at[my], send_sem=sems[0].at[a, k], recv_sem=sems[1].at[a, k],
            device_id=(px, py, c), device_id_type=MESH_ID) for k, (px, py) in _chip_peers(x, y) for a in range(n)], []

    return _Rider(s4s, [_S(s.shape, s.dtype) for s in s4s],
                  [pltpu.SemaphoreType.DMA((n, N_CHIP - 1)), pltpu.SemaphoreType.DMA((n, N_CHIP - 1))], copies)


def _chip_exchange_done(s4s, recvs):
    my = _my_chip()
    return [_with_own(r, lax.dynamic_index_in_dim(s, my, 0, keepdims=False), my) for s, r in zip(s4s, recvs)]


def _chip_gather(arrs):
    n = len(arrs)

    def copies(ins, outs, sems):
        x, y, c = _me()
        return [pltpu.make_async_remote_copy(
            src_ref=ins[a], dst_ref=outs[a].at[2 * x + y], send_sem=sems[0].at[a, k], recv_sem=sems[1].at[a, k],
            device_id=(px, py, c), device_id_type=MESH_ID) for k, (px, py) in _chip_peers(x, y) for a in range(n)], []

    return _Rider(arrs, [_S((N_CHIP,) + a.shape, a.dtype) for a in arrs],
                  [pltpu.SemaphoreType.DMA((n, N_CHIP - 1)), pltpu.SemaphoreType.DMA((n, N_CHIP - 1))], copies)


def _pair_gather(chips):
    n = len(chips)

    def copies(ins, outs, sems):
        x, y, c = _me()
        return [pltpu.make_async_remote_copy(
            src_ref=ins[a].at[j], dst_ref=outs[a].at[j], send_sem=sems[0].at[a, j], recv_sem=sems[1].at[a, j],
            device_id=(x, y, 1 - c), device_id_type=MESH_ID) for a in range(n) for j in range(N_CHIP)], []

    return _Rider(chips, [_S(a.shape, a.dtype) for a in chips],
                  [pltpu.SemaphoreType.DMA((n, N_CHIP)), pltpu.SemaphoreType.DMA((n, N_CHIP))], copies)


def _shard_rows(r):
    return 256 if r % 256 == 0 else r


def _pair_sum(g2, got):
    _, nc, L, R, C = g2.shape
    tr = _shard_rows(R)

    def body(a_ref, b_ref, o_ref):
        o_ref[...] = (a_ref[0] + b_ref[...]).astype(BF16)

    blk = pl.BlockSpec((1, 1, tr, C), lambda j, l, i: (j, l, i, 0))
    return _call(
        body, "pair_sum", (nc, L, R // tr),
        [pl.BlockSpec((1, 1, 1, tr, C), lambda j, l, i: (lax.axis_index("c"), j, l, i, 0)), blk], blk,
        _S(got.shape, BF16))(g2, got)


def _adam(w, g, m, v):
    m2 = ADAM_B1 * m + (1.0 - ADAM_B1) * g
    v2 = ADAM_B2 * v + (1.0 - ADAM_B2) * (g * g)
    m_hat = m2 / (1.0 - ADAM_B1 ** ADAM_STEP)
    v_hat = v2 / (1.0 - ADAM_B2 ** ADAM_STEP)
    return -ADAM_LR * (m_hat / (jnp.sqrt(v_hat) + ADAM_EPS) + ADAM_WD * w), m2, v2


def _sum_adam(recvs, w, m, v):
    L, R, C = w.shape
    tr = _shard_rows(R)
    nr = len(recvs)

    def body(*refs):
        w_ref, m_ref, v_ref, g_ref, d_ref, m2_ref, v2_ref = refs[nr:]
        g = None
        for li, r_ref in enumerate(refs[:nr]):
            s = r_ref[0, 0].astype(F32)
            for j in range(1, N_CHIP):
                s = s + r_ref[j, 0].astype(F32)
            g = s if g is None else jnp.where(pl.program_id(0) == li, s, g)
        g_ref[0] = g
        d_ref[0], m2_ref[0], v2_ref[0] = _adam(w_ref[0], g, m_ref[0], v_ref[0])

    blk = pl.BlockSpec((1, tr, C), lambda l, i: (l, i, 0))
    rspec = pl.BlockSpec((N_CHIP, 1, tr, C), (lambda l, i: (0, l, i, 0)) if nr == 1 else (lambda l, i: (0, 0, i, 0)))
    return _call(body, "sum_adam", (L, R // tr), [rspec] * nr + [blk, blk, blk], [blk] * 4, [_S(w.shape)] * 4)(
        *recvs, w, m, v)


def _adam_big(w, g, m, v):
    L, R, C = w.shape
    tr = _shard_rows(R)

    def body(w_ref, g_ref, m_ref, v_ref, d_ref, m2_ref, v2_ref):
        d_ref[0], m2_ref[0], v2_ref[0] = _adam(w_ref[0], g_ref[0], m_ref[0], v_ref[0])

    blk = pl.BlockSpec((1, tr, C), lambda l, i: (l, i, 0))
    return _call(body, "adam_big", (L, R // tr), [blk] * 4, [blk] * 3, [_S(w.shape)] * 3)(w, g, m, v)


def _sum_small(recv):
    def body(r_ref, o_ref):
        g = r_ref[0]
        for k in range(1, recv.shape[0]):
            g = g + r_ref[k]
        o_ref[...] = g

    return pl.pallas_call(body, name="sum_small", out_shape=_S(recv.shape[1:]))(recv)


def _adam_small(w, g, m, v):
    def body(w_ref, g_ref, m_ref, v_ref, d_ref, m2_ref, v2_ref):
        d_ref[...], m2_ref[...], v2_ref[...] = _adam(w_ref[...], g_ref[...], m_ref[...], v_ref[...])

    return pl.pallas_call(body, name="adam_small", out_shape=[_S(w.shape)] * 3)(w, g, m, v)


def _pack(arrs, dtype, row_mult):
    parts, offs, r = [], [], 0
    for a in arrs:
        nr = -(-a.size // LANES)
        parts.append(jnp.pad(a.reshape(-1).astype(dtype), (0, nr * LANES - a.size)))
        offs.append(r)
        r += nr
    pad = (-r) % row_mult
    if pad:
        parts.append(jnp.zeros((pad * LANES,), dtype))
    return jnp.concatenate(parts).reshape(r + pad, LANES), offs


def _unpack(packed, offs, shapes, lead=()):
    out = []
    for off, shp in zip(offs, shapes):
        size = int(np.prod(shp))
        nr = -(-size // LANES)
        flat = packed[..., off:off + nr, :].reshape(lead + (nr * LANES,))
        out.append(flat[..., :size].reshape(lead + tuple(shp)))
    return out


BIG = (("w_ada", 2), ("w_in", 2), ("w_br_a", 2), ("w_br_b", 2), ("w_br_c", 2), ("w_o", 1), ("w_gu", 2), ("w_down", 1))
CONV = ("dn_conv_w", "sc_conv_w")
REPL = ("c_ctx", "b_ada", "norm1_g", "norm2_g", "dn_a_log", "dn_dt_bias", "dn_norm_g", "pool_w", "pool_scale",
        "final_norm_g")
WEIGHTS = ("c_ctx", "w_ada", "b_ada", "norm1_g", "norm2_g", "w_in", "dn_conv_w", "dn_a_log", "dn_dt_bias", "dn_norm_g",
           "pool_w", "pool_scale", "sc_conv_w", "w_br_a", "w_br_b", "w_br_c", "w_o", "w_gu", "w_down", "final_norm_g")
TOKEN_TILE = 256


def _join(blocks, axis):
    nd, nl, r, c = blocks.shape
    if axis == 2:
        return blocks.transpose(1, 2, 0, 3).reshape(nl, r, nd * c)
    return blocks.transpose(1, 0, 2, 3).reshape(nl, nd * r, c)


def _split(full, axis):
    nl, r, c = full.shape
    if axis == 2:
        return full.reshape(nl, r, N_CHIP, 2, c // N_DEV).transpose(3, 2, 0, 1, 4)
    return full.reshape(nl, N_CHIP, 2, r // N_DEV, c).transpose(2, 1, 0, 3, 4)


PRESPLIT = ("w_in", "w_gu")


def _presplit(layer_grads, names=LAYERED):
    return [g if k in PRESPLIT else _split(g[None], dict(BIG)[k]) for k, g in zip(names, layer_grads)]


class _Comm:
    def __init__(self, late_shards, w_ada, b_ada):
        self.packed = [k for k in LATE if k not in PRESPLIT]
        self.shapes = [late_shards[k].shape for k in self.packed]
        pack, self.offs = _pack([late_shards[k] for k in self.packed], BF16, BF16_ROWS)
        self.late = [pack, late_shards["w_gu"].astype(BF16), late_shards["w_in"][1:].astype(BF16)]
        self.w_ada, self.b_ada = w_ada.astype(BF16), b_ada
        self.g2s = None

    def adaln_fwd(self, c, c_ctx):
        my = _dev_index(_me())
        ncol = self.w_ada.shape[2]
        c_all = _broadcast_small(c.reshape(8, LANES), "c_exchange").reshape(N_DEV, D)
        self.cc = jnp.concatenate([c_all, c_ctx[None, :], jnp.zeros((CC_ROWS - N_DEV - 1, D), F32)], axis=0)
        b_cols = lax.dynamic_slice_in_dim(self.b_ada, my * ncol, ncol, axis=1).reshape(NL, 1, ncol)
        cols = _mod_fwd(self.cc, self.w_ada, b_cols)
        got = _broadcast_small(cols.reshape(-1, LANES), "mods_exchange").reshape(N_DEV, NL, CC_ROWS, ncol)
        return my, got.transpose(1, 2, 0, 3).reshape(NL, CC_ROWS, N_DEV * ncol)

    def adaln_bwd(self, dmods):
        my = _dev_index(_me())
        ncol = self.w_ada.shape[2]
        got = _broadcast_small(dmods.reshape(-1, LANES), "dmods_exchange")
        rows = got.reshape(N_DEV, NL, 2, 6 * D)
        ctx_sum = _sum_small(rows[:, :, 0].reshape(N_DEV, -1, LANES)).reshape(NL, 1, 6 * D)
        db = _sum_small(rows.transpose(0, 2, 1, 3).reshape(2 * N_DEV, -1, LANES)).reshape(NL, 6 * D)
        dm = jnp.concatenate([rows[:, :, 1].transpose(1, 0, 2), ctx_sum,
                              jnp.zeros((NL, CC_ROWS - N_DEV - 1, 6 * D), F32)], axis=1)
        dw, dcc = _mod_bwd(self.cc, self.w_ada, lax.dynamic_slice_in_dim(dm, my * ncol, ncol, axis=2))
        return dict(w_ada=dw, b_ada=db, c_ctx=dcc[CTX_ROW])

    def late_weights_chips(self):
        return _chip_gather(self.late)

    def late_weights_pair(self, riding):
        self.chips = [_with_own(r, a, _my_chip()) for r, a in zip(riding, self.late)]
        return _pair_gather(self.chips)

    def late_weights(self, riding):
        on_south = lax.axis_index("c") == 0
        by_dev = []
        for mine, other in zip(self.chips, riding):
            both = jnp.stack([jnp.where(on_south, mine, other), jnp.where(on_south, other, mine)], axis=1)
            by_dev.append(both.reshape((N_DEV,) + mine.shape[1:]))
        shards = _unpack(by_dev[0], self.offs, self.shapes, (N_DEV,))
        return dict({k: _join(blocks, dict(BIG)[k]) for k, blocks in zip(self.packed, shards)}, w_gu=by_dev[1]), by_dev[2]

    def grad_pair_rider(self, layer_grads, names=LAYERED):
        g2s = _presplit(layer_grads, names)
        self.g2s = (self.g2s or []) + g2s
        return _pair_exchange(g2s)

    def grad_chip_rider(self, got):
        self.sums = [_pair_sum(g2, gt) for g2, gt in zip(self.g2s, got)]
        return _chip_exchange(self.sums)

    def grad_chip_done(self, riding):
        return _chip_exchange_done(self.sums, riding)


def kernel(x, c, ctx, c_ctx, w_ada, b_ada, norm1_g, norm2_g, w_in, dn_conv_w, dn_a_log, dn_dt_bias, dn_norm_g, pool_w, pool_scale, sc_conv_w, w_br_a, w_br_b, w_br_c, w_o, w_gu, w_down, final_norm_g, loss_target, m_c_ctx, m_w_ada, m_b_ada, m_norm1_g, m_norm2_g, m_w_in, m_dn_conv_w, m_dn_a_log, m_dn_dt_bias, m_dn_norm_g, m_pool_w, m_pool_scale, m_sc_conv_w, m_w_br_a, m_w_br_b, m_w_br_c, m_w_o, m_w_gu, m_w_down, m_final_norm_g, v_c_ctx, v_w_ada, v_b_ada, v_norm1_g, v_norm2_g, v_w_in, v_dn_conv_w, v_dn_a_log, v_dn_dt_bias, v_dn_norm_g, v_pool_w, v_pool_scale, v_sc_conv_w, v_w_br_a, v_w_br_b, v_w_br_c, v_w_o, v_w_gu, v_w_down, v_final_norm_g):
    loc = dict(c_ctx=c_ctx, w_ada=w_ada, b_ada=b_ada, norm1_g=norm1_g, norm2_g=norm2_g, w_in=w_in, dn_conv_w=dn_conv_w,
               dn_a_log=dn_a_log, dn_dt_bias=dn_dt_bias, dn_norm_g=dn_norm_g, pool_w=pool_w, pool_scale=pool_scale,
               sc_conv_w=sc_conv_w, w_br_a=w_br_a, w_br_b=w_br_b, w_br_c=w_br_c, w_o=w_o, w_gu=w_gu, w_down=w_down,
               final_norm_g=final_norm_g)
    mom_m = dict(c_ctx=m_c_ctx, w_ada=m_w_ada, b_ada=m_b_ada, norm1_g=m_norm1_g, norm2_g=m_norm2_g, w_in=m_w_in,
                 dn_conv_w=m_dn_conv_w, dn_a_log=m_dn_a_log, dn_dt_bias=m_dn_dt_bias, dn_norm_g=m_dn_norm_g,
                 pool_w=m_pool_w, pool_scale=m_pool_scale, sc_conv_w=m_sc_conv_w, w_br_a=m_w_br_a, w_br_b=m_w_br_b,
                 w_br_c=m_w_br_c, w_o=m_w_o, w_gu=m_w_gu, w_down=m_w_down, final_norm_g=m_final_norm_g)
    mom_v = dict(c_ctx=v_c_ctx, w_ada=v_w_ada, b_ada=v_b_ada, norm1_g=v_norm1_g, norm2_g=v_norm2_g, w_in=v_w_in,
                 dn_conv_w=v_dn_conv_w, dn_a_log=v_dn_a_log, dn_dt_bias=v_dn_dt_bias, dn_norm_g=v_dn_norm_g,
                 pool_w=v_pool_w, pool_scale=v_pool_scale, sc_conv_w=v_sc_conv_w, w_br_a=v_w_br_a, w_br_b=v_w_br_b,
                 w_br_c=v_w_br_c, w_o=v_w_o, w_gu=v_w_gu, w_down=v_w_down, final_norm_g=v_final_norm_g)
    my = _dev_index(_me())

    conv_pack, conv_offs = _pack([loc[k] for k in CONV], F32, 8)
    w_in0_all, conv_all = _allgather([w_in[:1].astype(BF16), conv_pack])
    full = dict({k: loc[k] for k in REPL}, w_in=[w_in0_all])
    for k, blocks in zip(CONV, _unpack(conv_all, conv_offs, [loc[k].shape for k in CONV], (N_DEV,))):
        full[k] = _join(blocks, 2)

    loss8, grad_x, g, recv_early = _device_step(x[0], c, ctx[0], loss_target[0], full, TOKEN_TILE,
                                                comm=_Comm({k: loc[k] for k in LATE + ("w_in",)}, w_ada, b_ada))

    last = [k for k in LAYERED if k not in FFN_W]
    tail = _presplit([g[k][0] for k in last], last)
    got = _run_rider(_pair_exchange(tail), "pair_exchange")
    sums = [_pair_sum(a, b) for a, b in zip(tail, got)]
    recv_tail = _chip_exchange_done(sums, _run_rider(_chip_exchange(sums), "chip_exchange"))
    recv = {k: [None, r] for k, r in zip(LAYERED, recv_early)}
    for k, r in list(zip(FFN_W, recv_early[len(LAYERED):])) + list(zip(last, recv_tail)):
        recv[k][0] = r

    small_names = REPL + CONV
    summed = [k for k in small_names if k != "b_ada"]
    small_pack, small_offs = _pack([g[k] for k in summed] + [loss8[0:1, 0:1]], F32, 8)
    small_sum = _sum_small(_broadcast_small(small_pack))
    sums = _unpack(small_sum, small_offs, [g[k].shape for k in summed] + [(1, 1)])
    grads = dict(zip(summed, sums[:-1]), b_ada=g["b_ada"], w_ada=g["w_ada"])
    loss = sums[-1][0, 0]
    for k in CONV:
        w = loc[k].shape[2]
        grads[k] = lax.dynamic_slice_in_dim(grads[k], my * w, w, axis=2)

    delta, new_m, new_v = {}, {}, {}
    for k in LAYERED:
        grads[k], delta[k], new_m[k], new_v[k] = _sum_adam(recv[k], loc[k], mom_m[k], mom_v[k])
    delta["w_ada"], new_m["w_ada"], new_v["w_ada"] = _adam_big(w_ada, g["w_ada"], m_w_ada, v_w_ada)
    packs = [_pack([src[k] for k in small_names], F32, 8)[0] for src in (loc, grads, mom_m, mom_v)]
    _, offs = _pack([loc[k] for k in small_names], F32, 8)
    shapes = [loc[k].shape for k in small_names]
    for dst, packed in zip((delta, new_m, new_v), _adam_small(*packs)):
        dst.update(zip(small_names, _unpack(packed, offs, shapes)))

    return (loss, grad_x[None], *[grads[k] for k in WEIGHTS], *[delta[k] for k in WEIGHTS],
            *[new_m[k] for k in WEIGHTS], *[new_v[k] for k in WEIGHTS])
```
